```python
import numpy as np
import jax
import jax.numpy as jnp
from jax import lax

D_MODEL = 2048
BATCH = 4
SEQ = 4096
DEPTH = 2

MEM_LEN = 256
N_BRANCH = 4
MIXER_WIDTH = D_MODEL // 4
HEAD_DIM = 128
N_HEADS_MIX = MIXER_WIDTH // HEAD_DIM
POOL_WINDOWS = (2, 4, 8, 16)
POOL_GROUP_DIM = MIXER_WIDTH // len(POOL_WINDOWS)
CONV_WIDTH = 3
MLSTM_CHUNK = 64
RET_CHUNK = 64
ROPE_BASE = 10000.0
XATTN_HEADS = 4
XATTN_HEAD_DIM = 128
N_EXPERTS = 64
TOP_K = 8
N_GROUPS = 8
TOPK_GROUPS = 4
EXPERT_DIM = D_MODEL // 4
SHARED_DIM = D_MODEL // 4
ROUTE_SCALE = 2.5
MOE_BLOCK = 128
LN_EPS = 1e-5
DEEPNORM_ALPHA = (2 * DEPTH) ** 0.25
DEEPNORM_BETA = (8 * DEPTH) ** -0.25
PROJ_SIZES = (MIXER_WIDTH,
              MIXER_WIDTH, MIXER_WIDTH, MIXER_WIDTH,
              MIXER_WIDTH, MIXER_WIDTH, MIXER_WIDTH, MIXER_WIDTH, N_HEADS_MIX, N_HEADS_MIX,
              MIXER_WIDTH, MIXER_WIDTH, MIXER_WIDTH, MIXER_WIDTH,
              N_BRANCH * D_MODEL)
PROJ_WIDTH = sum(PROJ_SIZES)

kernel_name = 'hybrid_pool_conv_mlstm_retention_moe_deepnorm'

F32 = jnp.float32


def layer_norm(x, w, b):
    xf = x.astype(F32)
    mu = xf.mean(-1, keepdims=True)
    var = jnp.square(xf - mu).mean(-1, keepdims=True)
    return ((xf - mu) * lax.rsqrt(var + LN_EPS) * w.astype(F32) + b.astype(F32)).astype(x.dtype)


def head_norm(h, w):
    hf = h.astype(F32)
    mu = hf.mean(-1, keepdims=True)
    var = jnp.square(hf - mu).mean(-1, keepdims=True)
    y = ((hf - mu) * lax.rsqrt(var + LN_EPS)).reshape(h.shape[0], h.shape[1], -1) * w.astype(F32)
    return y.astype(h.dtype)


def pool_mixer(u, pool_w, pool_scale):
    b, s, _ = u.shape
    ug = u.astype(F32).reshape(b, s, len(POOL_WINDOWS), POOL_GROUP_DIM)
    csum = jnp.cumsum(ug, axis=1)
    t = jnp.arange(s)
    pooled = []
    for g, win in enumerate(POOL_WINDOWS):
        cg = csum[:, :, g]
        lagged = jnp.pad(cg, ((0, 0), (win, 0), (0, 0)))[:, :s]
        count = jnp.minimum(t + 1, win).astype(F32)[None, :, None]
        pooled.append((cg - lagged) / count)
    mixed = jnp.stack(pooled, axis=2) - ug
    y = jnp.einsum('bsgc,gcd->bsgd', mixed, pool_w.astype(F32))
    return (y.reshape(b, s, -1) * pool_scale.astype(F32)).astype(u.dtype)


def short_conv(h, gate_b, gate_c, conv_w):
    s = h.shape[1]
    z = gate_c * h
    zp = jnp.pad(z, ((0, 0), (CONV_WIDTH - 1, 0), (0, 0)))
    conv = conv_w[0] * zp[:, CONV_WIDTH - 1:CONV_WIDTH - 1 + s]
    for lag in range(1, CONV_WIDTH):
        start = CONV_WIDTH - 1 - lag
        conv = conv + conv_w[lag] * zp[:, start:start + s]
    return gate_b * conv


def _to_chunks(a, chunk):
    b, s, h = a.shape[:3]
    a = a.astype(F32).reshape(b, s // chunk, chunk, h, *a.shape[3:])
    return jnp.moveaxis(a, (1, 3), (0, 2))


def _from_chunks(a):
    a = jnp.moveaxis(a, (0, 2), (1, 3))
    return a.reshape(a.shape[0], a.shape[1] * a.shape[2], a.shape[3], a.shape[4])


def mlstm_chunkwise(q, k, v, i_pre, f_pre):
    b, s, h, d = q.shape
    L = MLSTM_CHUNK
    qc = _to_chunks(q, L)
    kc = _to_chunks(k, L) * d ** -0.5
    vc = _to_chunks(v, L)
    ic = _to_chunks(i_pre, L)
    fc = jax.nn.log_sigmoid(_to_chunks(f_pre, L))
    causal = jnp.tril(jnp.ones((L, L), dtype=bool))

    def step(carry, xs):
        c_mat, n_vec, m_prev = carry
        q_, k_, v_, i_, lf = xs
        cum_f = jnp.cumsum(lf, axis=-1)
        dmat = jnp.where(causal, cum_f[..., :, None] - cum_f[..., None, :] + i_[..., None, :], -jnp.inf)
        m_inter = cum_f + m_prev[..., None]
        m_t = jnp.maximum(m_inter, dmat.max(-1))
        p = jnp.exp(dmat - m_t[..., None]) * jnp.einsum('bhtd,bhsd->bhts', q_, k_)
        inter = jnp.exp(m_inter - m_t)
        num = inter[..., None] * jnp.einsum('bhtk,bhkv->bhtv', q_, c_mat) + jnp.einsum('bhts,bhsv->bhtv', p, v_)
        den = inter * jnp.einsum('bhtk,bhk->bht', q_, n_vec) + p.sum(-1)
        h_out = num / jnp.maximum(jnp.abs(den), jnp.exp(-m_t))[..., None]
        f_tot = cum_f[..., -1]
        g = f_tot[..., None] - cum_f + i_
        m_new = jnp.maximum(f_tot + m_prev, g.max(-1))
        decay = jnp.exp(f_tot + m_prev - m_new)
        wg = jnp.exp(g - m_new[..., None])
        c_mat = decay[..., None, None] * c_mat + jnp.einsum('bhs,bhsk,bhsv->bhkv', wg, k_, v_)
        n_vec = decay[..., None] * n_vec + jnp.einsum('bhs,bhsk->bhk', wg, k_)
        return (c_mat, n_vec, m_new), h_out

    init = (jnp.zeros((b, h, d, d), F32), jnp.zeros((b, h, d), F32), jnp.zeros((b, h), F32))
    _, hs = lax.scan(step, init, (qc, kc, vc, ic, fc))
    return _from_chunks(hs).astype(q.dtype)


def rotary(x):
    s, d = x.shape[1], x.shape[3]
    half = d // 2
    inv = ROPE_BASE ** (-jnp.arange(half, dtype=F32) / half)
    ang = jnp.arange(s, dtype=F32)[:, None] * inv[None, :]
    cos = jnp.cos(ang)[None, :, None, :]
    sin = jnp.sin(ang)[None, :, None, :]
    x1, x2 = x[..., :half], x[..., half:]
    return jnp.concatenate([x1 * cos - x2 * sin, x1 * sin + x2 * cos], axis=-1)


def retention_chunkwise(q, k, v):
    b, s, h, d = q.shape
    L = RET_CHUNK
    log_g = jnp.log(1.0 - 2.0 ** (-5.0 - jnp.arange(h, dtype=F32)))
    t = jnp.arange(L, dtype=F32)
    lag = t[:, None] - t[None, :]
    intra = jnp.where(lag >= 0, jnp.exp(jnp.maximum(lag, 0.0)[None] * log_g[:, None, None]), 0.0)
    cross = jnp.exp((t + 1.0)[None, :] * log_g[:, None])
    zeta = jnp.exp((L - 1.0 - t)[None, :] * log_g[:, None])
    chunk_decay = jnp.exp(L * log_g)
    qc = _to_chunks(q, L)
    kc = _to_chunks(k, L) * d ** -0.5
    vc = _to_chunks(v, L)

    def step(r_state, xs):
        q_, k_, v_ = xs
        inner = jnp.einsum('bhts,bhsv->bhtv', jnp.einsum('bhtd,bhsd->bhts', q_, k_) * intra, v_)
        crs = jnp.einsum('bhtk,bhkv->bhtv', q_, r_state) * cross[..., None]
        r_state = chunk_decay[:, None, None] * r_state + jnp.einsum('bhsk,bhsv->bhkv', k_ * zeta[..., None], v_)
        return r_state, inner + crs

    _, ys = lax.scan(step, jnp.zeros((b, h, d, d), F32), (qc, kc, vc))
    return _from_chunks(ys).astype(v.dtype)


def mixer_sublayer(h, w_in, gate_b, pool_w, pool_scale, conv_w, mlstm_norm_w, ret_norm_w, w_branch, w_out):
    b, s, _ = h.shape
    nh, hd = N_HEADS_MIX, HEAD_DIM
    split_at = np.cumsum(PROJ_SIZES)[:-1].tolist()
    (u_pool, c_h, c_b, c_c, m_q, m_k, m_v, m_o, m_i, m_f,
     r_q, r_k, r_v, r_g, g_pre) = jnp.split(h @ w_in, split_at, axis=-1)

    def heads(a):
        return a.reshape(b, s, nh, hd)

    y_pool = pool_mixer(u_pool, pool_w, pool_scale)
    y_conv = short_conv(c_h, c_b, c_c, conv_w)
    m_h = mlstm_chunkwise(heads(m_q), heads(m_k), heads(m_v), m_i + gate_b[:nh], m_f + gate_b[nh:])
    y_mlstm = head_norm(heads(jax.nn.sigmoid(m_o)) * m_h, mlstm_norm_w)
    r_h = retention_chunkwise(rotary(heads(r_q).astype(F32)), rotary(heads(r_k).astype(F32)), heads(r_v))
    y_ret = jax.nn.silu(r_g) * head_norm(r_h, ret_norm_w)

    gates = jax.nn.sigmoid(g_pre.reshape(b, s, N_BRANCH, D_MODEL))
    branches = (y_pool, y_conv, y_mlstm, y_ret)
    merged = gates[:, :, 0] * (branches[0] @ w_branch[0])
    for n in range(1, N_BRANCH):
        merged = merged + gates[:, :, n] * (branches[n] @ w_branch[n])
    return merged @ w_out


def memory_xattn(h, mem, wq, wk, wv, wo):
    b, s, _ = h.shape
    m = mem.shape[1]
    q = (h @ wq).reshape(b, s, XATTN_HEADS, XATTN_HEAD_DIM)
    k = (mem @ wk).reshape(b, m, XATTN_HEADS, XATTN_HEAD_DIM)
    v = (mem @ wv).reshape(b, m, XATTN_HEADS, XATTN_HEAD_DIM)
    scores = jnp.einsum('bshd,bmhd->bhsm', q, k).astype(F32) * XATTN_HEAD_DIM ** -0.5
    probs = jax.nn.softmax(scores, axis=-1).astype(h.dtype)
    o = jnp.einsum('bhsm,bmhd->bshd', probs, v).reshape(b, s, XATTN_HEADS * XATTN_HEAD_DIM)
    return o @ wo


def swiglu(x, w_gu, w_dn):
    g, u = jnp.split(x @ w_gu, 2, axis=-1)
    return (jax.nn.silu(g) * u) @ w_dn


def route(xt, router_w, router_b):
    t = xt.shape[0]
    scores = jax.nn.sigmoid((xt @ router_w).astype(F32))
    biased = scores + router_b.astype(F32)
    grp = biased.reshape(t, N_GROUPS, N_EXPERTS // N_GROUPS)
    grp_score = lax.top_k(grp, 2)[0].sum(-1)
    _, grp_idx = lax.top_k(grp_score, TOPK_GROUPS)
    grp_mask = jax.nn.one_hot(grp_idx, N_GROUPS, dtype=F32).sum(1)
    exp_mask = jnp.repeat(grp_mask, N_EXPERTS // N_GROUPS, axis=1) > 0
    _, idx = lax.top_k(jnp.where(exp_mask, biased, -jnp.inf), TOP_K)
    w = jnp.take_along_axis(scores, idx, axis=1)
    w = w / w.sum(-1, keepdims=True) * ROUTE_SCALE
    return idx, w


def routed_experts(xt, idx, wts, w_gu, w_dn):
    t, dm = xt.shape
    flat_e = idx.reshape(-1).astype(jnp.int32)
    flat_w = wts.reshape(-1)
    a = flat_e.shape[0]
    order = jnp.argsort(flat_e).astype(jnp.int32)
    e_sorted = flat_e[order]
    counts = jnp.bincount(flat_e, length=N_EXPERTS).astype(jnp.int32)
    starts = jnp.cumsum(counts) - counts
    pcounts = (counts + MOE_BLOCK - 1) // MOE_BLOCK * MOE_BLOCK
    pends = jnp.cumsum(pcounts)
    pstarts = pends - pcounts
    dest = pstarts[e_sorted] + jnp.arange(a, dtype=jnp.int32) - starts[e_sorted]
    n_blocks = -(-a // MOE_BLOCK) + N_EXPERTS
    rows = n_blocks * MOE_BLOCK
    tok_buf = jnp.full((rows,), t, jnp.int32).at[dest].set(order // TOP_K)
    w_buf = jnp.zeros((rows,), xt.dtype).at[dest].set(flat_w[order].astype(xt.dtype))
    blk_e = jnp.minimum(jnp.searchsorted(pends, jnp.arange(n_blocks, dtype=jnp.int32) * MOE_BLOCK, side='right'),
                        N_EXPERTS - 1).astype(jnp.int32)
    x_pad = jnp.concatenate([xt, jnp.zeros((1, dm), xt.dtype)], axis=0)

    def step(out, blk):
        tok, wb, e = blk
        y = swiglu(x_pad[tok], w_gu[e], w_dn[e])
        return out.at[tok].add(y * wb[:, None]), None

    out, _ = lax.scan(step, jnp.zeros((t + 1, dm), xt.dtype),
                      (tok_buf.reshape(n_blocks, MOE_BLOCK), w_buf.reshape(n_blocks, MOE_BLOCK), blk_e))
    return out[:t]


def moe_sublayer(h, router_w, router_b, w_gu, w_dn, s_gu, s_dn):
    b, s, dm = h.shape
    xt = h.reshape(b * s, dm)
    idx, w = route(xt, router_w, router_b)
    y = routed_experts(xt, idx, w, w_gu, w_dn) + swiglu(xt, s_gu, s_dn)
    return y.reshape(b, s, dm)


def setup_inputs(seed: int = 0) -> dict:
    key = jax.random.key(seed)
    ks = jax.random.split(key, 24)

    def nrm(k, shape, scale):
        return jax.random.normal(k, shape, F32) * scale

    W, H, D = MIXER_WIDTH, N_HEADS_MIX, D_MODEL
    beta = DEEPNORM_BETA
    x = nrm(ks[0], (BATCH, SEQ, D), 1.0)
    mem = nrm(ks[1], (BATCH, MEM_LEN, D), 1.0)
    w_in = nrm(ks[2], (DEPTH, D, PROJ_WIDTH), D ** -0.5)
    i_bias = nrm(ks[3], (DEPTH, H), 0.1)
    f_bias = jnp.linspace(3.0, 6.0, H, dtype=F32)[None, :] + nrm(ks[4], (DEPTH, H), 0.1)
    mlstm_gate_b = jnp.concatenate([i_bias, f_bias], axis=-1)
    pool_w = nrm(ks[5], (DEPTH, len(POOL_WINDOWS), POOL_GROUP_DIM, POOL_GROUP_DIM), POOL_GROUP_DIM ** -0.5)
    pool_scale = 1.0 + nrm(ks[6], (DEPTH, W), 0.02)
    conv_w = nrm(ks[7], (DEPTH, CONV_WIDTH, W), CONV_WIDTH ** -0.5)
    mlstm_norm_w = 1.0 + nrm(ks[8], (DEPTH, W), 0.02)
    ret_norm_w = 1.0 + nrm(ks[9], (DEPTH, W), 0.02)
    w_branch = nrm(ks[10], (DEPTH, N_BRANCH, W, D), W ** -0.5)
    w_mix_out = nrm(ks[11], (DEPTH, D, D), D ** -0.5 * beta)
    xa_wq = nrm(ks[12], (DEPTH, D, XATTN_HEADS * XATTN_HEAD_DIM), D ** -0.5)
    xa_wk = nrm(ks[13], (DEPTH, D, XATTN_HEADS * XATTN_HEAD_DIM), D ** -0.5)
    xa_wv = nrm(ks[14], (DEPTH, D, XATTN_HEADS * XATTN_HEAD_DIM), D ** -0.5 * beta)
    xa_wo = nrm(ks[15], (DEPTH, XATTN_HEADS * XATTN_HEAD_DIM, D), (XATTN_HEADS * XATTN_HEAD_DIM) ** -0.5 * beta)
    router_w = nrm(ks[16], (DEPTH, D, N_EXPERTS), D ** -0.5)
    router_b = nrm(ks[17], (DEPTH, N_EXPERTS), 0.01)
    moe_w_gu = nrm(ks[18], (DEPTH, N_EXPERTS, D, 2 * EXPERT_DIM), D ** -0.5)
    moe_w_dn = nrm(ks[19], (DEPTH, N_EXPERTS, EXPERT_DIM, D), EXPERT_DIM ** -0.5 * beta)
    shared_w_gu = nrm(ks[20], (DEPTH, D, 2 * SHARED_DIM), D ** -0.5)
    shared_w_dn = nrm(ks[21], (DEPTH, SHARED_DIM, D), SHARED_DIM ** -0.5 * beta)
    ln_w = 1.0 + nrm(ks[22], (DEPTH, 3, D), 0.02)
    ln_b = nrm(ks[23], (DEPTH, 3, D), 0.02)
    return {'x': x, 'mem': mem, 'w_in': w_in, 'mlstm_gate_b': mlstm_gate_b, 'pool_w': pool_w,
            'pool_scale': pool_scale, 'conv_w': conv_w, 'mlstm_norm_w': mlstm_norm_w,
            'ret_norm_w': ret_norm_w, 'w_branch': w_branch, 'w_mix_out': w_mix_out,
            'xa_wq': xa_wq, 'xa_wk': xa_wk, 'xa_wv': xa_wv, 'xa_wo': xa_wo,
            'router_w': router_w, 'router_b': router_b, 'moe_w_gu': moe_w_gu, 'moe_w_dn': moe_w_dn,
            'shared_w_gu': shared_w_gu, 'shared_w_dn': shared_w_dn, 'ln_w': ln_w, 'ln_b': ln_b}


def reference(x, mem, w_in, mlstm_gate_b, pool_w, pool_scale, conv_w, mlstm_norm_w, ret_norm_w,
              w_branch, w_mix_out, xa_wq, xa_wk, xa_wv, xa_wo, router_w, router_b,
              moe_w_gu, moe_w_dn, shared_w_gu, shared_w_dn, ln_w, ln_b):
    for l in range(DEPTH):
        y = mixer_sublayer(x, w_in[l], mlstm_gate_b[l], pool_w[l], pool_scale[l], conv_w[l],
                           mlstm_norm_w[l], ret_norm_w[l], w_branch[l], w_mix_out[l])
        x = layer_norm(DEEPNORM_ALPHA * x + y, ln_w[l, 0], ln_b[l, 0])
        y = memory_xattn(x, mem, xa_wq[l], xa_wk[l], xa_wv[l], xa_wo[l])
        x = layer_norm(DEEPNORM_ALPHA * x + y, ln_w[l, 1], ln_b[l, 1])
        y = moe_sublayer(x, router_w[l], router_b[l], moe_w_gu[l], moe_w_dn[l], shared_w_gu[l], shared_w_dn[l])
        x = layer_norm(DEEPNORM_ALPHA * x + y, ln_w[l, 2], ln_b[l, 2])
    return x
```

```python
import functools

import numpy as np
import jax
import jax.numpy as jnp
from jax import lax
from jax.experimental import pallas as pl
from jax.experimental.pallas import tpu as pltpu

F32 = jnp.float32
BF16 = jnp.bfloat16
I32 = jnp.int32

N_BRANCH = 4
HEAD_DIM = 128
POOL_WINDOWS = (2, 4, 8, 16)
CONV_WIDTH = 3
ROPE_BASE = 10000.0
XATTN_HEADS = 4
N_EXPERTS = 64
TOP_K = 8
N_GROUPS = 8
TOPK_GROUPS = 4
ROUTE_SCALE = 2.5
LN_EPS = 1e-5

LANES = 128
V7X_VMEM_BYTES = 64 * 1024 * 1024
VMEM_LIMIT = 56 * 1024 * 1024

SCAN_CHUNK = 128
SCAN_BLOCK = 512
HALO = 16
MOE_BM = 256


def _cp(*sem):
    return pltpu.CompilerParams(dimension_semantics=sem, vmem_limit_bytes=VMEM_LIMIT)


def _sigmoid(x):
    return 1.0 / (1.0 + jnp.exp(-x))


def _silu(x):
    return x * _sigmoid(x)


def _log_sigmoid(x):
    return jnp.minimum(x, 0.0) - jnp.log(1.0 + jnp.exp(-jnp.abs(x)))


def _layer_norm(z, w, b):
    mu = jnp.mean(z, axis=-1, keepdims=True)
    d = z - mu
    var = jnp.mean(d * d, axis=-1, keepdims=True)
    return d * lax.rsqrt(var + LN_EPS) * w + b


def _head_norm(h, w):
    mu = jnp.mean(h, axis=-1, keepdims=True)
    d = h - mu
    var = jnp.mean(d * d, axis=-1, keepdims=True)
    return d * lax.rsqrt(var + LN_EPS) * w


def _mm_kernel(x_ref, w_ref, o_ref, xb_ref):
    @pl.when(pl.program_id(1) == 0)
    def _():
        xb_ref[...] = x_ref[...].astype(BF16)

    o_ref[...] = jnp.dot(xb_ref[...], w_ref[...], preferred_element_type=F32)


def _matmul(x, w, *, tm, tn):
    t, k = x.shape
    n = w.shape[1]
    return pl.pallas_call(
        _mm_kernel,
        grid=(t // tm, n // tn),
        in_specs=[pl.BlockSpec((tm, k), lambda i, j: (i, 0)),
                  pl.BlockSpec((k, tn), lambda i, j: (0, j))],
        out_specs=pl.BlockSpec((tm, tn), lambda i, j: (i, j)),
        out_shape=jax.ShapeDtypeStruct((t, n), F32),
        scratch_shapes=[pltpu.VMEM((tm, k), BF16)],
        compiler_params=_cp("parallel", "arbitrary"),
    )(x, w)


def _poolconv_kernel(u_ref, uh_ref, h_ref, hh_ref, b_ref, c_ref, ch_ref, pw_ref, ps_ref, cw_ref,
                     yp_ref, yc_ref, ubuf, zbuf, *, tb, n_tblk):
    first = (pl.program_id(0) % n_tblk) == 0
    ubuf[0:HALO, :] = jnp.where(first, 0.0, uh_ref[...])
    ubuf[HALO:HALO + tb, :] = u_ref[...]
    zbuf[0:HALO, :] = jnp.where(first, 0.0, ch_ref[...] * hh_ref[...])
    zbuf[HALO:HALO + tb, :] = c_ref[...] * h_ref[...]

    t_pos = (pl.program_id(0) % n_tblk) * tb + lax.broadcasted_iota(I32, (tb, LANES), 0)
    gw = u_ref.shape[1] // len(POOL_WINDOWS)
    for grp, win in enumerate(POOL_WINDOWS):
        lanes = slice(grp * gw, (grp + 1) * gw)
        cur = ubuf[HALO:HALO + tb, lanes]
        acc = cur
        for lag in range(1, win):
            acc = acc + ubuf[HALO - lag:HALO - lag + tb, lanes]
        count = jnp.minimum(t_pos + 1, win).astype(F32)
        mixed = acc / count - cur
        y = jnp.dot(mixed.astype(BF16), pw_ref[grp], preferred_element_type=F32)
        yp_ref[:, lanes] = y * ps_ref[:, lanes]

    conv = cw_ref[0:1, :] * zbuf[HALO:HALO + tb, :]
    for lag in range(1, CONV_WIDTH):
        conv = conv + cw_ref[lag:lag + 1, :] * zbuf[HALO - lag:HALO - lag + tb, :]
    yc_ref[...] = b_ref[...] * conv


def _pool_conv(proj, pool_w, pool_scale, conv_w, *, seq, width, tb):
    t = proj.shape[0]
    n_tblk = seq // tb
    ratio = tb // HALO

    def cur(col):
        return pl.BlockSpec((tb, width), lambda g: (g, col))

    def halo(col):
        return pl.BlockSpec((HALO, width), lambda g: (jnp.maximum(g * ratio - 1, 0), col))

    full = lambda shape: pl.BlockSpec(shape, lambda g: (0,) * len(shape))
    return pl.pallas_call(
        functools.partial(_poolconv_kernel, tb=tb, n_tblk=n_tblk),
        grid=(t // tb,),
        in_specs=[cur(0), halo(0), cur(1), halo(1), cur(2), cur(3), halo(3),
                  full(pool_w.shape), full(pool_scale.shape), full(conv_w.shape)],
        out_specs=[pl.BlockSpec((tb, width), lambda g: (g, 0))] * 2,
        out_shape=[jax.ShapeDtypeStruct((t, width), F32)] * 2,
        scratch_shapes=[pltpu.VMEM((HALO + tb, width), F32)] * 2,
        compiler_params=_cp("parallel"),
    )(proj, proj, proj, proj, proj, proj, proj, pool_w, pool_scale, conv_w)


def _cum_rows(x, op, fill):
    n = x.shape[0]
    row = lax.broadcasted_iota(I32, x.shape, 0)
    shift = 1
    while shift < n:
        x = op(x, jnp.where(row >= shift, pltpu.roll(x, shift, axis=0), fill))
        shift *= 2
    return x


def _mlstm_kernel(q_ref, k_ref, v_ref, o_ref, g_ref, gb_ref, nw_ref, y_ref, c_scr, n_scr, m_scr,
                  *, chunk, heads):
    @pl.when(pl.program_id(1) == 0)
    def _():
        c_scr[...] = jnp.zeros_like(c_scr)
        n_scr[...] = jnp.zeros_like(n_scr)
        m_scr[...] = jnp.zeros_like(m_scr)

    tb = q_ref.shape[0]
    hd = HEAD_DIM
    scale = hd ** -0.5
    tri = (lax.broadcasted_iota(I32, (chunk, chunk), 0) >= lax.broadcasted_iota(I32, (chunk, chunk), 1))
    for c in range(tb // chunk):
        rows = slice(c * chunk, (c + 1) * chunk)
        gates = g_ref[rows, :] + gb_ref[...]
        lf = _log_sigmoid(pltpu.roll(gates, LANES - heads, axis=1))
        cumf = _cum_rows(lf, jnp.add, 0.0)
        a = gates - cumf
        m_prev = m_scr[...]
        mu = jnp.maximum(_cum_rows(a, jnp.maximum, -jnp.inf), m_prev)
        mu_last = mu[chunk - 1:chunk, :]
        a_t = a.T
        for h in range(heads):
            cols = slice(h * hd, (h + 1) * hd)
            q = q_ref[rows, cols]
            k = k_ref[rows, cols] * scale
            v = v_ref[rows, cols]
            qb, kb, vb = q.astype(BF16), k.astype(BF16), v.astype(BF16)
            mu_col = mu[:, h:h + 1]
            a_col = a[:, h:h + 1]
            m_prev_h = m_prev[:, h:h + 1]
            mu_last_h = mu_last[:, h:h + 1]
            dmat = jnp.exp(jnp.where(tri, a_t[h:h + 1, :] - mu_col, -jnp.inf))
            s = lax.dot_general(qb, kb, (((1,), (1,)), ((), ())), preferred_element_type=F32)
            p = dmat * s
            inter = jnp.exp(m_prev_h - mu_col)
            c_h = c_scr[h]
            n_h = n_scr[h]
            num = inter * jnp.dot(qb, c_h.astype(BF16), preferred_element_type=F32) \
                + jnp.dot(p.astype(BF16), vb, preferred_element_type=F32)
            den = inter * jnp.sum(q * n_h, axis=-1, keepdims=True) + jnp.sum(p, axis=-1, keepdims=True)
            floor = jnp.exp(-(cumf[:, h:h + 1] + mu_col))
            h_out = num / jnp.maximum(jnp.abs(den), floor)

            wg = jnp.exp(a_col - mu_last_h)
            decay = jnp.exp(m_prev_h - mu_last_h)
            kw = k * wg
            c_scr[h] = decay * c_h + lax.dot_general(kw.astype(BF16), vb, (((0,), (0,)), ((), ())),
                                                     preferred_element_type=F32)
            n_scr[h] = decay * n_h + jnp.sum(kw, axis=0, keepdims=True)

            gated = _sigmoid(o_ref[rows, cols]) * h_out
            y_ref[rows, cols] = _head_norm(gated, nw_ref[:, cols])
        m_scr[...] = cumf[chunk - 1:chunk, :] + mu_last


def _mlstm(proj, gates, gate_b, norm_w, *, batch, seq, width, col0):
    t = proj.shape[0]
    heads = width // HEAD_DIM
    n_tblk = seq // SCAN_BLOCK
    blk = lambda col: pl.BlockSpec((SCAN_BLOCK, width), lambda b, i: (b * n_tblk + i, col))
    return pl.pallas_call(
        functools.partial(_mlstm_kernel, chunk=SCAN_CHUNK, heads=heads),
        grid=(batch, n_tblk),
        in_specs=[blk(col0), blk(col0 + 1), blk(col0 + 2), blk(col0 + 3),
                  pl.BlockSpec((SCAN_BLOCK, LANES), lambda b, i: (b * n_tblk + i, 0)),
                  pl.BlockSpec((1, LANES), lambda b, i: (0, 0)),
                  pl.BlockSpec((1, width), lambda b, i: (0, 0))],
        out_specs=pl.BlockSpec((SCAN_BLOCK, width), lambda b, i: (b * n_tblk + i, 0)),
        out_shape=jax.ShapeDtypeStruct((t, width), F32),
        scratch_shapes=[pltpu.VMEM((heads, HEAD_DIM, HEAD_DIM), F32),
                        pltpu.VMEM((heads, 1, HEAD_DIM), F32),
                        pltpu.VMEM((1, LANES), F32)],
        compiler_params=_cp("parallel", "arbitrary"),
    )(proj, proj, proj, proj, gates, gate_b, norm_w)


def _ret_kernel(q_ref, k_ref, v_ref, g_ref, cos_ref, sin_ref, intra_ref, cross_ref, zeta_ref, nw_ref,
                y_ref, r_scr, *, chunk, heads, chunk_decay):
    @pl.when(pl.program_id(1) == 0)
    def _():
        r_scr[...] = jnp.zeros_like(r_scr)

    tb = q_ref.shape[0]
    hd = HEAD_DIM
    scale = hd ** -0.5
    for c in range(tb // chunk):
        rows = slice(c * chunk, (c + 1) * chunk)
        cos = cos_ref[rows, :]
        sin = sin_ref[rows, :]
        for h in range(heads):
            cols = slice(h * hd, (h + 1) * hd)
            q = q_ref[rows, cols]
            k = k_ref[rows, cols]
            q = q * cos + pltpu.roll(q, hd // 2, axis=1) * sin
            k = (k * cos + pltpu.roll(k, hd // 2, axis=1) * sin) * scale
            vb = v_ref[rows, cols].astype(BF16)
            qb = q.astype(BF16)
            s = lax.dot_general(qb, k.astype(BF16), (((1,), (1,)), ((), ())), preferred_element_type=F32)
            inner = jnp.dot((s * intra_ref[h]).astype(BF16), vb, preferred_element_type=F32)
            r_h = r_scr[h]
            crs = jnp.dot(qb, r_h.astype(BF16), preferred_element_type=F32) * cross_ref[h]
            r_scr[h] = chunk_decay[h] * r_h + lax.dot_general(
                (k * zeta_ref[h]).astype(BF16), vb, (((0,), (0,)), ((), ())), preferred_element_type=F32)
            y_ref[rows, cols] = _silu(g_ref[rows, cols]) * _head_norm(inner + crs, nw_ref[:, cols])


def _retention(proj, norm_w, *, batch, seq, width, col0):
    t = proj.shape[0]
    heads = width // HEAD_DIM
    chunk = SCAN_CHUNK
    n_tblk = seq // SCAN_BLOCK
    half = HEAD_DIM // 2
    inv = ROPE_BASE ** (-jnp.arange(half, dtype=F32) / half)
    ang = jnp.arange(seq, dtype=F32)[:, None] * inv[None, :]
    cos_t = jnp.concatenate([jnp.cos(ang), jnp.cos(ang)], axis=-1)
    sin_t = jnp.concatenate([-jnp.sin(ang), jnp.sin(ang)], axis=-1)
    log_g = jnp.log(1.0 - 2.0 ** (-5.0 - jnp.arange(heads, dtype=F32)))
    tt = jnp.arange(chunk, dtype=F32)
    lag = tt[:, None] - tt[None, :]
    intra = jnp.where(lag >= 0, jnp.exp(jnp.maximum(lag, 0.0)[None] * log_g[:, None, None]), 0.0)
    cross = jnp.broadcast_to(jnp.exp((tt + 1.0)[None, :] * log_g[:, None])[:, :, None], (heads, chunk, HEAD_DIM))
    zeta = jnp.broadcast_to(jnp.exp((chunk - 1.0 - tt)[None, :] * log_g[:, None])[:, :, None],
                            (heads, chunk, HEAD_DIM))
    chunk_decay = tuple(float((1.0 - 2.0 ** (-5.0 - h)) ** chunk) for h in range(heads))

    blk = lambda col: pl.BlockSpec((SCAN_BLOCK, width), lambda b, i: (b * n_tblk + i, col))
    pos = pl.BlockSpec((SCAN_BLOCK, HEAD_DIM), lambda b, i: (i, 0))
    full3 = lambda a: pl.BlockSpec(a.shape, lambda b, i: (0, 0, 0))
    return pl.pallas_call(
        functools.partial(_ret_kernel, chunk=chunk, heads=heads, chunk_decay=chunk_decay),
        grid=(batch, n_tblk),
        in_specs=[blk(col0), blk(col0 + 1), blk(col0 + 2), blk(col0 + 3), pos, pos,
                  full3(intra), full3(cross), full3(zeta),
                  pl.BlockSpec((1, width), lambda b, i: (0, 0))],
        out_specs=pl.BlockSpec((SCAN_BLOCK, width), lambda b, i: (b * n_tblk + i, 0)),
        out_shape=jax.ShapeDtypeStruct((t, width), F32),
        scratch_shapes=[pltpu.VMEM((heads, HEAD_DIM, HEAD_DIM), F32)],
        compiler_params=_cp("parallel", "arbitrary"),
    )(proj, proj, proj, proj, cos_t, sin_t, intra, cross, zeta, norm_w)


def _merge_kernel(x_ref, y0_ref, y1_ref, y2_ref, y3_ref, wg_ref, wb_ref, o_ref, xb_ref):
    @pl.when(pl.program_id(1) == 0)
    def _():
        xb_ref[...] = x_ref[...].astype(BF16)

    xb = xb_ref[...]
    acc = None
    for n, y_ref in enumerate((y0_ref, y1_ref, y2_ref, y3_ref)):
        gate = _sigmoid(jnp.dot(xb, wg_ref[n], preferred_element_type=F32))
        term = gate * jnp.dot(y_ref[...].astype(BF16), wb_ref[n], preferred_element_type=F32)
        acc = term if acc is None else acc + term
    o_ref[...] = acc


def _merge(x, branches, w_gate, w_branch, *, tm, tn):
    t, d = x.shape
    width = branches[0].shape[1]
    ybs = pl.BlockSpec((tm, width), lambda i, j: (i, 0))
    return pl.pallas_call(
        _merge_kernel,
        grid=(t // tm, d // tn),
        in_specs=[pl.BlockSpec((tm, d), lambda i, j: (i, 0)), ybs, ybs, ybs, ybs,
                  pl.BlockSpec((N_BRANCH, d, tn), lambda i, j: (0, 0, j)),
                  pl.BlockSpec((N_BRANCH, width, tn), lambda i, j: (0, 0, j))],
        out_specs=pl.BlockSpec((tm, tn), lambda i, j: (i, j)),
        out_shape=jax.ShapeDtypeStruct((t, d), F32),
        scratch_shapes=[pltpu.VMEM((tm, d), BF16)],
        compiler_params=_cp("parallel", "arbitrary"),
    )(x, *branches, w_gate, w_branch)


def _proj_ln_kernel(a_ref, w_ref, r_ref, lw_ref, lb_ref, o_ref, *, alpha):
    y = jnp.dot(a_ref[...].astype(BF16), w_ref[...], preferred_element_type=F32)
    o_ref[...] = _layer_norm(alpha * r_ref[...] + y, lw_ref[...], lb_ref[...])


def _proj_ln(a, w, resid, ln_w, ln_b, *, alpha, tm):
    t, k = a.shape
    d = w.shape[1]
    row = lambda n: pl.BlockSpec((tm, n), lambda i: (i, 0))
    const = lambda shape: pl.BlockSpec(shape, lambda i: (0, 0))
    return pl.pallas_call(
        functools.partial(_proj_ln_kernel, alpha=alpha),
        grid=(t // tm,),
        in_specs=[row(k), const((k, d)), row(d), const((1, d)), const((1, d))],
        out_specs=row(d),
        out_shape=jax.ShapeDtypeStruct((t, d), F32),
        compiler_params=_cp("parallel"),
    )(a, w, resid, ln_w, ln_b)


def _xattn_kernel(x_ref, wq_ref, kv_ref, wo_ref, lw_ref, lb_ref, o_ref, *, alpha, heads):
    x = x_ref[...]
    hd = HEAD_DIM
    inner = heads * hd
    q = jnp.dot(x.astype(BF16), wq_ref[...], preferred_element_type=F32)
    outs = []
    for h in range(heads):
        qh = q[:, h * hd:(h + 1) * hd].astype(BF16)
        kh = kv_ref[:, h * hd:(h + 1) * hd].astype(BF16)
        vh = kv_ref[:, inner + h * hd:inner + (h + 1) * hd].astype(BF16)
        s = lax.dot_general(qh, kh, (((1,), (1,)), ((), ())), preferred_element_type=F32) * hd ** -0.5
        s = s - jnp.max(s, axis=-1, keepdims=True)
        e = jnp.exp(s)
        p = e / jnp.sum(e, axis=-1, keepdims=True)
        outs.append(jnp.dot(p.astype(BF16), vh, preferred_element_type=F32).astype(BF16))
    o = jnp.concatenate(outs, axis=-1)
    y = jnp.dot(o, wo_ref[...], preferred_element_type=F32)
    o_ref[...] = _layer_norm(alpha * x + y, lw_ref[...], lb_ref[...])


def _xattn(x, kv, wq, wo, ln_w, ln_b, *, alpha, seq, mem_len, tm):
    t, d = x.shape
    inner = wq.shape[1]
    n_tblk = seq // tm
    const = lambda shape: pl.BlockSpec(shape, lambda i: (0, 0))
    return pl.pallas_call(
        functools.partial(_xattn_kernel, alpha=alpha, heads=XATTN_HEADS),
        grid=(t // tm,),
        in_specs=[pl.BlockSpec((tm, d), lambda i: (i, 0)), const((d, inner)),
                  pl.BlockSpec((mem_len, 2 * inner), lambda i: (i // n_tblk, 0)),
                  const((inner, d)), const((1, d)), const((1, d))],
        out_specs=pl.BlockSpec((tm, d), lambda i: (i, 0)),
        out_shape=jax.ShapeDtypeStruct((t, d), F32),
        compiler_params=_cp("parallel"),
    )(x, wq, kv, wo, ln_w, ln_b)


def _route_kernel(x_ref, wr_ref, rb_ref, idx_ref, w_ref, pos_ref, cnt_ref, carry, *, tm):
    @pl.when(pl.program_id(0) == 0)
    def _():
        carry[...] = jnp.zeros_like(carry)

    e_n, g_n = N_EXPERTS, N_GROUPS
    per = e_n // g_n
    logits = lax.dot_general(wr_ref[...], x_ref[...], (((1,), (1,)), ((), ())),
                             precision=lax.Precision.HIGHEST, preferred_element_type=F32)
    scores = _sigmoid(logits)
    biased = scores + rb_ref[...]
    b3 = biased.reshape(g_n, per, tm)
    member = lax.broadcasted_iota(I32, (g_n, per, tm), 1)
    top1 = jnp.max(b3, axis=1, keepdims=True)
    first = jnp.min(jnp.where(b3 == top1, member, per), axis=1, keepdims=True)
    top2 = jnp.max(jnp.where(member == first, -jnp.inf, b3), axis=1, keepdims=True)
    gs = top1 + top2
    gid = lax.broadcasted_iota(I32, (g_n, 1, tm), 0)
    rank = jnp.zeros((g_n, 1, tm), I32)
    for other in range(g_n):
        o = gs[other:other + 1]
        ahead = jnp.logical_or(o > gs, jnp.logical_and(o == gs, other < gid))
        rank = rank + jnp.where(ahead, 1, 0)
    cur = jnp.where(rank < TOPK_GROUPS, b3, -jnp.inf).reshape(e_n, tm)

    eid = lax.broadcasted_iota(I32, (e_n, tm), 0)
    picks, vals = [], []
    sel = jnp.zeros((e_n, tm), F32)
    for k in range(TOP_K):
        mx = jnp.max(cur, axis=0, keepdims=True)
        ik = jnp.min(jnp.where(cur == mx, eid, e_n), axis=0, keepdims=True)
        hit = eid == ik
        vals.append(jnp.sum(jnp.where(hit, scores, 0.0), axis=0, keepdims=True))
        cur = jnp.where(hit, -jnp.inf, cur)
        sel = jnp.where(hit, 1.0, sel)
        picks.append(ik)
    total = vals[0]
    for v in vals[1:]:
        total = total + v

    tri =jnp.where(lax.broadcasted_iota(I32, (tm, tm), 0) <= lax.broadcasted_iota(I32, (tm, tm), 1), 1.0, 0.0)
    incl = jnp.dot(sel.astype(BF16), tri.astype(BF16), preferred_element_type=F32)
    base = carry[:, 0:1] + incl - 1.0
    for k in range(TOP_K):
        idx_ref[k:k + 1, :] = picks[k]
        w_ref[k:k + 1, :] = vals[k] / total * ROUTE_SCALE
        pos_ref[k:k + 1, :] = jnp.sum(jnp.where(eid == picks[k], base, 0.0), axis=0, keepdims=True).astype(I32)
    carry[...] = carry[...] + incl[:, tm - 1:tm]
    cnt_ref[...] = carry[...]


def _route(x, router_w_t, router_b, *, tm):
    t, d = x.shape
    e_n = N_EXPERTS
    row = pl.BlockSpec((TOP_K, tm), lambda i: (0, i))
    return pl.pallas_call(
        functools.partial(_route_kernel, tm=tm),
        grid=(t // tm,),
        in_specs=[pl.BlockSpec((tm, d), lambda i: (i, 0)),
                  pl.BlockSpec((e_n, d), lambda i: (0, 0)),
                  pl.BlockSpec((e_n, 1), lambda i: (0, 0))],
        out_specs=[row, row, row, pl.BlockSpec((e_n, LANES), lambda i: (0, 0))],
        out_shape=[jax.ShapeDtypeStruct((TOP_K, t), I32), jax.ShapeDtypeStruct((TOP_K, t), F32),
                   jax.ShapeDtypeStruct((TOP_K, t), I32), jax.ShapeDtypeStruct((e_n, LANES), F32)],
        scratch_shapes=[pltpu.VMEM((e_n, LANES), F32)],
        compiler_params=_cp("arbitrary"),
    )(x, router_w_t, router_b)


def _dispatch_kernel(dest_ref, pend_ref, pcnt_ref, x_ref, xs_ref, zbuf, sem, zsem, *, tm, bm):
    i = pl.program_id(0)

    def pad_copy(e):
        return pltpu.make_async_copy(zbuf, xs_ref.at[pl.ds(pl.multiple_of(pend_ref[e] - bm, bm), bm), :], zsem)

    @pl.when(i == 0)
    def _():
        zbuf[...] = jnp.zeros_like(zbuf)

        def start(e, c):
            @pl.when(pcnt_ref[e] > 0)
            def _():
                pad_copy(e).start()
            return c

        def wait(e, c):
            @pl.when(pcnt_ref[e] > 0)
            def _():
                pad_copy(e).wait()
            return c

        lax.fori_loop(0, N_EXPERTS, start, 0)
        lax.fori_loop(0, N_EXPERTS, wait, 0)

    def row_copy(r, k):
        d = dest_ref[(i * tm + r) * TOP_K + k]
        return pltpu.make_async_copy(x_ref.at[pl.ds(r, 1), :], xs_ref.at[pl.ds(d, 1), :], sem)

    def start_row(r, c):
        for k in range(TOP_K):
            row_copy(r, k).start()
        return c

    def wait_row(r, c):
        for k in range(TOP_K):
            row_copy(r, k).wait()
        return c

    lax.fori_loop(0, tm, start_row, 0)
    lax.fori_loop(0, tm, wait_row, 0)


def _dispatch(x, dest_flat, pend, pcnt, *, rows, tm, bm):
    t, d = x.shape
    return pl.pallas_call(
        functools.partial(_dispatch_kernel, tm=tm, bm=bm),
        grid_spec=pltpu.PrefetchScalarGridSpec(
            num_scalar_prefetch=3,
            grid=(t // tm,),
            in_specs=[pl.BlockSpec((tm, d), lambda i, *_: (i, 0))],
            out_specs=pl.BlockSpec(memory_space=pl.ANY),
            scratch_shapes=[pltpu.VMEM((bm, d), F32), pltpu.SemaphoreType.DMA(()), pltpu.SemaphoreType.DMA(())],
        ),
        out_shape=jax.ShapeDtypeStruct((rows, d), F32),
        compiler_params=_cp("arbitrary"),
    )(dest_flat, pend, pcnt, x)


def _expert_kernel(blk_e_ref, nused_ref, xs_ref, wgu_ref, wdn_ref, ys_ref, wgu_b, wdn_b):
    j = pl.program_id(0)
    used = j < nused_ref[0]
    changed = jnp.logical_or(j == 0, blk_e_ref[j] != blk_e_ref[jnp.maximum(j - 1, 0)])

    @pl.when(jnp.logical_and(used, changed))
    def _():
        wgu_b[...] = wgu_ref[...].astype(BF16)
        wdn_b[...] = wdn_ref[...].astype(BF16)

    @pl.when(used)
    def _():
        f = wdn_b.shape[0]
        gu = jnp.dot(xs_ref[...].astype(BF16), wgu_b[...], preferred_element_type=F32)
        hidden = (_silu(gu[:, :f]) * gu[:, f:]).astype(BF16)
        ys_ref[...] = jnp.dot(hidden, wdn_b[...], preferred_element_type=F32)


def _experts(xs, blk_e, nused, w_gu, w_dn, *, bm):
    rows, d = xs.shape
    f2 = w_gu.shape[2]
    f = w_dn.shape[1]
    blk = lambda j, be, nu: (jnp.minimum(j, nu[0] - 1), 0)
    return pl.pallas_call(
        _expert_kernel,
        grid_spec=pltpu.PrefetchScalarGridSpec(
            num_scalar_prefetch=2,
            grid=(rows // bm,),
            in_specs=[pl.BlockSpec((bm, d), blk),
                      pl.BlockSpec((None, d, f2), lambda j, be, nu: (be[j], 0, 0)),
                      pl.BlockSpec((None, f, d), lambda j, be, nu: (be[j], 0, 0))],
            out_specs=pl.BlockSpec((bm, d), blk),
            scratch_shapes=[pltpu.VMEM((d, f2), BF16), pltpu.VMEM((f, d), BF16)],
        ),
        out_shape=jax.ShapeDtypeStruct((rows, d), F32),
        compiler_params=_cp("arbitrary"),
    )(blk_e, nused, xs, w_gu, w_dn)


def _combine_kernel(dest_ref, x_ref, w_ref, sgu_ref, sdn_ref, lw_ref, lb_ref, ys_ref, o_ref, buf, sem,
                    *, tm, alpha):
    i = pl.program_id(0)

    def row_copy(r, k):
        d = dest_ref[(i * tm + r) * TOP_K + k]
        return pltpu.make_async_copy(ys_ref.at[pl.ds(d, 1), :], buf.at[k, pl.ds(r, 1), :], sem)

    def start_row(r, c):
        for k in range(TOP_K):
            row_copy(r, k).start()
        return c

    def wait_row(r, c):
        for k in range(TOP_K):
            row_copy(r, k).wait()
        return c

    lax.fori_loop(0, tm, start_row, 0)

    x = x_ref[...]
    f = sdn_ref.shape[0]
    gu = jnp.dot(x.astype(BF16), sgu_ref[...], preferred_element_type=F32)
    hidden = (_silu(gu[:, :f]) * gu[:, f:]).astype(BF16)
    acc = alpha * x + jnp.dot(hidden, sdn_ref[...], preferred_element_type=F32)

    lax.fori_loop(0, tm, wait_row, 0)
    for k in range(TOP_K):
        acc = acc + w_ref[:, k:k + 1] * buf[k]
    o_ref[...] = _layer_norm(acc, lw_ref[...], lb_ref[...])


def _combine(x, ys, dest_flat, w_tk, s_gu, s_dn, ln_w, ln_b, *, alpha, tm):
    t, d = x.shape
    const = lambda shape: pl.BlockSpec(shape, lambda i, *_: (0, 0))
    return pl.pallas_call(
        functools.partial(_combine_kernel, tm=tm, alpha=alpha),
        grid_spec=pltpu.PrefetchScalarGridSpec(
            num_scalar_prefetch=1,
            grid=(t // tm,),
            in_specs=[pl.BlockSpec((tm, d), lambda i, *_: (i, 0)),
                      pl.BlockSpec((tm, TOP_K), lambda i, *_: (i, 0)),
                      const(s_gu.shape), const(s_dn.shape), const((1, d)), const((1, d)),
                      pl.BlockSpec(memory_space=pl.ANY)],
            out_specs=pl.BlockSpec((tm, d), lambda i, *_: (i, 0)),
            scratch_shapes=[pltpu.VMEM((TOP_K, tm, d), F32), pltpu.SemaphoreType.DMA(())],
        ),
        out_shape=jax.ShapeDtypeStruct((t, d), F32),
        compiler_params=_cp("arbitrary"),
    )(dest_flat, x, w_tk, s_gu, s_dn, ln_w, ln_b, ys)


def _mixer_sublayer(x, w_in, gate_b, pool_w, pool_scale, conv_w, mlstm_norm_w, ret_norm_w, w_branch, w_out,
                    ln_w, ln_b, *, batch, seq, alpha):
    t, d = x.shape
    width = d // N_BRANCH
    heads = width // HEAD_DIM
    gate_off = 8 * width
    ret_off = gate_off + 2 * heads
    g_off = ret_off + 4 * width
    w_main = jnp.concatenate([w_in[:, :gate_off], w_in[:, ret_off:g_off]], axis=1).astype(BF16)
    w_if = jnp.pad(w_in[:, gate_off:ret_off], ((0, 0), (0, LANES - 2 * heads))).astype(BF16)
    w_gate = w_in[:, g_off:].reshape(d, N_BRANCH, d).transpose(1, 0, 2).astype(BF16)
    gate_bias = jnp.pad(gate_b, (0, LANES - 2 * heads)).reshape(1, LANES)

    proj = _matmul(x, w_main, tm=1024, tn=512)
    gates = _matmul(x, w_if, tm=1024, tn=LANES)
    y_pool, y_conv = _pool_conv(proj, pool_w.astype(BF16), pool_scale.reshape(1, width), conv_w,
                                seq=seq, width=width, tb=512)
    y_mlstm = _mlstm(proj, gates, gate_bias, mlstm_norm_w.reshape(1, width),
                     batch=batch, seq=seq, width=width, col0=4)
    y_ret = _retention(proj, ret_norm_w.reshape(1, width), batch=batch, seq=seq, width=width, col0=8)
    merged = _merge(x, (y_pool, y_conv, y_mlstm, y_ret), w_gate, w_branch.astype(BF16), tm=512, tn=512)
    return _proj_ln(merged, w_out.astype(BF16), x, ln_w, ln_b, alpha=alpha, tm=256)


def _xattn_sublayer(x, mem2d, wq, wk, wv, wo, ln_w, ln_b, *, seq, mem_len, alpha):
    w_kv = jnp.concatenate([wk, wv], axis=1).astype(BF16)
    kv = _matmul(mem2d, w_kv, tm=min(mem2d.shape[0], 1024), tn=512)
    return _xattn(x, kv, wq.astype(BF16), wo.astype(BF16), ln_w, ln_b,
                  alpha=alpha, seq=seq, mem_len=mem_len, tm=256)


def _moe_sublayer(x, router_w, router_b, w_gu, w_dn, s_gu, s_dn, ln_w, ln_b, *, alpha):
    t, d = x.shape
    e_n, bm = N_EXPERTS, MOE_BM
    idx, wts, pos, cnt = _route(x, router_w.T, router_b.reshape(e_n, 1), tm=512)

    counts = cnt[:, 0].astype(I32)
    pcnt = (counts + bm - 1) // bm * bm
    pend = jnp.cumsum(pcnt).astype(I32)
    pstart = pend - pcnt
    dest_flat = (pstart[idx] + pos).T.reshape(-1)
    n_blocks = (t * TOP_K) // bm + e_n
    nused = pend[-1] // bm
    blk_e = jnp.minimum(jnp.searchsorted(pend, jnp.arange(n_blocks, dtype=I32) * bm, side='right'),
                        e_n - 1).astype(I32)
    blk_e = jnp.where(jnp.arange(n_blocks) < nused, blk_e, blk_e[jnp.maximum(nused - 1, 0)])

    xs = _dispatch(x, dest_flat, pend, pcnt, rows=n_blocks * bm, tm=256, bm=bm)
    ys = _experts(xs, blk_e, nused.reshape(1), w_gu, w_dn, bm=bm)
    return _combine(x, ys, dest_flat, wts.T, s_gu.astype(BF16), s_dn.astype(BF16), ln_w, ln_b,
                    alpha=alpha, tm=128)


def kernel(x, mem, w_in, mlstm_gate_b, pool_w, pool_scale, conv_w, mlstm_norm_w, ret_norm_w, w_branch,
           w_mix_out, xa_wq, xa_wk, xa_wv, xa_wo, router_w, router_b, moe_w_gu, moe_w_dn, shared_w_gu,
           shared_w_dn, ln_w, ln_b):
    batch, seq, d = x.shape
    depth = w_in.shape[0]
    mem_len = mem.shape[1]
    alpha = (2 * depth) ** 0.25
    h = x.reshape(batch * seq, d)
    mem2d = mem.reshape(batch * mem_len, d)
    for l in range(depth):
        lw = ln_w[l].reshape(3, 1, d)
        lb = ln_b[l].reshape(3, 1, d)
        h = _mixer_sublayer(h, w_in[l], mlstm_gate_b[l], pool_w[l], pool_scale[l], conv_w[l], mlstm_norm_w[l],
                            ret_norm_w[l], w_branch[l], w_mix_out[l], lw[0], lb[0],
                            batch=batch, seq=seq, alpha=alpha)
        h = _xattn_sublayer(h, mem2d, xa_wq[l], xa_wk[l], xa_wv[l], xa_wo[l], lw[1], lb[1],
                            seq=seq, mem_len=mem_len, alpha=alpha)
        h = _moe_sublayer(h, router_w[l], router_b[l], moe_w_gu[l], moe_w_dn[l], shared_w_gu[l], shared_w_dn[l],
                          lw[2], lb[2], alpha=alpha)
    return h.reshape(batch, seq, d)
```

```python
import functools

import numpy as np
import jax
import jax.numpy as jnp
from jax import lax
from jax.experimental import pallas as pl
from jax.experimental.pallas import tpu as pltpu

F32 = jnp.float32
BF16 = jnp.bfloat16
I32 = jnp.int32

N_BRANCH = 4
HEAD_DIM = 128
POOL_WINDOWS = (2, 4, 8, 16)
CONV_WIDTH = 3
ROPE_BASE = 10000.0
XATTN_HEADS = 4
N_EXPERTS = 64
TOP_K = 8
N_GROUPS = 8
TOPK_GROUPS = 4
ROUTE_SCALE = 2.5
LN_EPS = 1e-5

LANES = 128
V7X_VMEM_BYTES = 64 * 1024 * 1024
VMEM_LIMIT = 56 * 1024 * 1024

SCAN_CHUNK = 128
SCAN_BLOCK = 512
HALO = 16
MOE_BM = 256
ROUTE_TM = 512


def _cp(*sem):
    return pltpu.CompilerParams(dimension_semantics=sem, vmem_limit_bytes=VMEM_LIMIT)


def _sigmoid(x):
    return 1.0 / (1.0 + jnp.exp(-x))


def _silu(x):
    return x * _sigmoid(x)


def _log_sigmoid(x):
    return jnp.minimum(x, 0.0) - jnp.log(1.0 + jnp.exp(-jnp.abs(x)))


def _layer_norm(z, w, b):
    mu = jnp.mean(z, axis=-1, keepdims=True)
    d = z - mu
    var = jnp.mean(d * d, axis=-1, keepdims=True)
    return d * lax.rsqrt(var + LN_EPS) * w + b


def _head_norm(h, w):
    mu = jnp.mean(h, axis=-1, keepdims=True)
    d = h - mu
    var = jnp.mean(d * d, axis=-1, keepdims=True)
    return d * lax.rsqrt(var + LN_EPS) * w


def _mm_kernel(x_ref, w_ref, o_ref, xb_ref):
    @pl.when(pl.program_id(1) == 0)
    def _():
        xb_ref[...] = x_ref[...].astype(BF16)

    o_ref[...] = jnp.dot(xb_ref[...], w_ref[...], preferred_element_type=F32)


def _shift_cast_kernel(a_ref, b_ref, o_ref, *, shift):
    tn = o_ref.shape[1]
    both = jnp.concatenate([a_ref[...], b_ref[...]], axis=1)
    o_ref[...] = both[:, shift:shift + tn].astype(BF16)


def _shifted_cast(w, *, col0, shift, ncols, tr, tn):
    rows = w.shape[0]
    return pl.pallas_call(
        functools.partial(_shift_cast_kernel, shift=shift),
        grid=(rows // tr, ncols // tn),
        in_specs=[pl.BlockSpec((tr, tn), lambda i, j: (i, col0 // tn + j)),
                  pl.BlockSpec((tr, LANES), lambda i, j: (i, (col0 + (j + 1) * tn) // LANES))],
        out_specs=pl.BlockSpec((tr, tn), lambda i, j: (i, j)),
        out_shape=jax.ShapeDtypeStruct((rows, ncols), BF16),
        compiler_params=_cp("parallel", "parallel"),
    )(w, w)


def _matmul(x, w, *, tm, tn, ncols=None):
    t, k = x.shape
    n = w.shape[1] if ncols is None else ncols
    return pl.pallas_call(
        _mm_kernel,
        grid=(t // tm, n // tn),
        in_specs=[pl.BlockSpec((tm, k), lambda i, j: (i, 0)),
                  pl.BlockSpec((k, tn), lambda i, j: (0, j))],
        out_specs=pl.BlockSpec((tm, tn), lambda i, j: (i, j)),
        out_shape=jax.ShapeDtypeStruct((t, n), F32),
        scratch_shapes=[pltpu.VMEM((tm, k), BF16)],
        compiler_params=_cp("parallel", "arbitrary"),
    )(x, w)


def _poolconv_kernel(u_ref, uh_ref, h_ref, hh_ref, b_ref, c_ref, ch_ref, pw_ref, ps_ref, cw_ref,
                     yp_ref, yc_ref, ubuf, zbuf, *, tb, n_tblk):
    first = (pl.program_id(0) % n_tblk) == 0
    ubuf[0:HALO, :] = jnp.where(first, 0.0, uh_ref[...])
    ubuf[HALO:HALO + tb, :] = u_ref[...]
    zbuf[0:HALO, :] = jnp.where(first, 0.0, ch_ref[...] * hh_ref[...])
    zbuf[HALO:HALO + tb, :] = c_ref[...] * h_ref[...]

    t_pos = (pl.program_id(0) % n_tblk) * tb + lax.broadcasted_iota(I32, (tb, LANES), 0)
    gw = u_ref.shape[1] // len(POOL_WINDOWS)
    for grp, win in enumerate(POOL_WINDOWS):
        lanes = slice(grp * gw, (grp + 1) * gw)
        cur = ubuf[HALO:HALO + tb, lanes]
        acc = cur
        for lag in range(1, win):
            acc = acc + ubuf[HALO - lag:HALO - lag + tb, lanes]
        count = jnp.minimum(t_pos + 1, win).astype(F32)
        mixed = acc / count - cur
        y = jnp.dot(mixed.astype(BF16), pw_ref[grp], preferred_element_type=F32)
        yp_ref[:, lanes] = y * ps_ref[:, lanes]

    conv = cw_ref[0:1, :] * zbuf[HALO:HALO + tb, :]
    for lag in range(1, CONV_WIDTH):
        conv = conv + cw_ref[lag:lag + 1, :] * zbuf[HALO - lag:HALO - lag + tb, :]
    yc_ref[...] = b_ref[...] * conv


def _pool_conv(proj, pool_w, pool_scale, conv_w, *, seq, width, tb):
    t = proj.shape[0]
    n_tblk = seq // tb
    ratio = tb // HALO

    def cur(col):
        return pl.BlockSpec((tb, width), lambda g: (g, col))

    def halo(col):
        return pl.BlockSpec((HALO, width), lambda g: (jnp.maximum(g * ratio - 1, 0), col))

    full = lambda shape: pl.BlockSpec(shape, lambda g: (0,) * len(shape))
    return pl.pallas_call(
        functools.partial(_poolconv_kernel, tb=tb, n_tblk=n_tblk),
        grid=(t // tb,),
        in_specs=[cur(0), halo(0), cur(1), halo(1), cur(2), cur(3), halo(3),
                  full(pool_w.shape), full(pool_scale.shape), full(conv_w.shape)],
        out_specs=[pl.BlockSpec((tb, width), lambda g: (g, 0))] * 2,
        out_shape=[jax.ShapeDtypeStruct((t, width), F32)] * 2,
        scratch_shapes=[pltpu.VMEM((HALO + tb, width), F32)] * 2,
        compiler_params=_cp("parallel"),
    )(proj, proj, proj, proj, proj, proj, proj, pool_w, pool_scale, conv_w)


def _cum_rows(x, op, fill):
    n = x.shape[0]
    row = lax.broadcasted_iota(I32, x.shape, 0)
    shift = 1
    while shift < n:
        x = op(x, jnp.where(row >= shift, pltpu.roll(x, shift, axis=0), fill))
        shift *= 2
    return x


def _mlstm_kernel(q_ref, k_ref, v_ref, o_ref, g_ref, gb_ref, nw_ref, y_ref, c_scr, n_scr, m_scr,
                  *, chunk, heads):
    @pl.when(pl.program_id(1) == 0)
    def _():
        c_scr[...] = jnp.zeros_like(c_scr)
        n_scr[...] = jnp.zeros_like(n_scr)
        m_scr[...] = jnp.zeros_like(m_scr)

    tb = q_ref.shape[0]
    hd = HEAD_DIM
    scale = hd ** -0.5
    tri = (lax.broadcasted_iota(I32, (chunk, chunk), 0) >= lax.broadcasted_iota(I32, (chunk, chunk), 1))
    for c in range(tb // chunk):
        rows = slice(c * chunk, (c + 1) * chunk)
        gates = g_ref[rows, :] + gb_ref[...]
        lf = _log_sigmoid(pltpu.roll(gates, LANES - heads, axis=1))
        cumf = _cum_rows(lf, jnp.add, 0.0)
        a = gates - cumf
        m_prev = m_scr[...]
        mu = jnp.maximum(_cum_rows(a, jnp.maximum, -jnp.inf), m_prev)
        mu_last = mu[chunk - 1:chunk, :]
        a_t = a.T
        for h in range(heads):
            cols = slice(h * hd, (h + 1) * hd)
            q = q_ref[rows, cols]
            k = k_ref[rows, cols] * scale
            v = v_ref[rows, cols]
            qb, kb, vb = q.astype(BF16), k.astype(BF16), v.astype(BF16)
            mu_col = mu[:, h:h + 1]
            a_col = a[:, h:h + 1]
            m_prev_h = m_prev[:, h:h + 1]
            mu_last_h = mu_last[:, h:h + 1]
            dmat = jnp.exp(jnp.where(tri, a_t[h:h + 1, :] - mu_col, -jnp.inf))
            s = lax.dot_general(qb, kb, (((1,), (1,)), ((), ())), preferred_element_type=F32)
            p = dmat * s
            inter = jnp.exp(m_prev_h - mu_col)
            c_h = c_scr[h]
            n_h = n_scr[h]
            num = inter * jnp.dot(qb, c_h.astype(BF16), preferred_element_type=F32) \
                + jnp.dot(p.astype(BF16), vb, preferred_element_type=F32)
            den = inter * jnp.sum(q * n_h, axis=-1, keepdims=True) + jnp.sum(p, axis=-1, keepdims=True)
            floor = jnp.exp(-(cumf[:, h:h + 1] + mu_col))
            h_out = num / jnp.maximum(jnp.abs(den), floor)

            wg = jnp.exp(a_col - mu_last_h)
            decay = jnp.exp(m_prev_h - mu_last_h)
            kw = k * wg
            c_scr[h] = decay * c_h + lax.dot_general(kw.astype(BF16), vb, (((0,), (0,)), ((), ())),
                                                     preferred_element_type=F32)
            n_scr[h] = decay * n_h + jnp.sum(kw, axis=0, keepdims=True)

            gated = _sigmoid(o_ref[rows, cols]) * h_out
            y_ref[rows, cols] = _head_norm(gated, nw_ref[:, cols])
        m_scr[...] = cumf[chunk - 1:chunk, :] + mu_last


def _mlstm(proj, gates, gate_b, norm_w, *, batch, seq, width, col0):
    t = proj.shape[0]
    heads = width // HEAD_DIM
    n_tblk = seq // SCAN_BLOCK
    blk = lambda col: pl.BlockSpec((SCAN_BLOCK, width), lambda b, i: (b * n_tblk + i, col))
    return pl.pallas_call(
        functools.partial(_mlstm_kernel, chunk=SCAN_CHUNK, heads=heads),
        grid=(batch, n_tblk),
        in_specs=[blk(col0), blk(col0 + 1), blk(col0 + 2), blk(col0 + 3),
                  pl.BlockSpec((SCAN_BLOCK, LANES), lambda b, i: (b * n_tblk + i, 0)),
                  pl.BlockSpec((1, LANES), lambda b, i: (0, 0)),
                  pl.BlockSpec((1, width), lambda b, i: (0, 0))],
        out_specs=pl.BlockSpec((SCAN_BLOCK, width), lambda b, i: (b * n_tblk + i, 0)),
        out_shape=jax.ShapeDtypeStruct((t, width), F32),
        scratch_shapes=[pltpu.VMEM((heads, HEAD_DIM, HEAD_DIM), F32),
                        pltpu.VMEM((heads, 1, HEAD_DIM), F32),
                        pltpu.VMEM((1, LANES), F32)],
        compiler_params=_cp("parallel", "arbitrary"),
    )(proj, proj, proj, proj, gates, gate_b, norm_w)


def _ret_kernel(q_ref, k_ref, v_ref, g_ref, cos_ref, sin_ref, intra_ref, cross_ref, zeta_ref, nw_ref,
                y_ref, r_scr, *, chunk, heads, chunk_decay):
    @pl.when(pl.program_id(1) == 0)
    def _():
        r_scr[...] = jnp.zeros_like(r_scr)

    tb = q_ref.shape[0]
    hd = HEAD_DIM
    scale = hd ** -0.5
    for c in range(tb // chunk):
        rows = slice(c * chunk, (c + 1) * chunk)
        cos = cos_ref[rows, :]
        sin = sin_ref[rows, :]
        for h in range(heads):
            cols = slice(h * hd, (h + 1) * hd)
            q = q_ref[rows, cols]
            k = k_ref[rows, cols]
            q = q * cos + pltpu.roll(q, hd // 2, axis=1) * sin
            k = (k * cos + pltpu.roll(k, hd // 2, axis=1) * sin) * scale
            vb = v_ref[rows, cols].astype(BF16)
            qb = q.astype(BF16)
            s = lax.dot_general(qb, k.astype(BF16), (((1,), (1,)), ((), ())), preferred_element_type=F32)
            inner = jnp.dot((s * intra_ref[h]).astype(BF16), vb, preferred_element_type=F32)
            r_h = r_scr[h]
            crs = jnp.dot(qb, r_h.astype(BF16), preferred_element_type=F32) * cross_ref[h]
            r_scr[h] = chunk_decay[h] * r_h + lax.dot_general(
                (k * zeta_ref[h]).astype(BF16), vb, (((0,), (0,)), ((), ())), preferred_element_type=F32)
            y_ref[rows, cols] = _silu(g_ref[rows, cols]) * _head_norm(inner + crs, nw_ref[:, cols])


def _retention(proj, norm_w, *, batch, seq, width, col0):
    t = proj.shape[0]
    heads = width // HEAD_DIM
    chunk = SCAN_CHUNK
    n_tblk = seq // SCAN_BLOCK
    half = HEAD_DIM // 2
    inv = ROPE_BASE ** (-jnp.arange(half, dtype=F32) / half)
    ang = jnp.arange(seq, dtype=F32)[:, None] * inv[None, :]
    cos_t = jnp.concatenate([jnp.cos(ang), jnp.cos(ang)], axis=-1)
    sin_t = jnp.concatenate([-jnp.sin(ang), jnp.sin(ang)], axis=-1)
    log_g = jnp.log(1.0 - 2.0 ** (-5.0 - jnp.arange(heads, dtype=F32)))
    tt = jnp.arange(chunk, dtype=F32)
    lag = tt[:, None] - tt[None, :]
    intra = jnp.where(lag >= 0, jnp.exp(jnp.maximum(lag, 0.0)[None] * log_g[:, None, None]), 0.0)
    cross = jnp.broadcast_to(jnp.exp((tt + 1.0)[None, :] * log_g[:, None])[:, :, None], (heads, chunk, HEAD_DIM))
    zeta = jnp.broadcast_to(jnp.exp((chunk - 1.0 - tt)[None, :] * log_g[:, None])[:, :, None],
                            (heads, chunk, HEAD_DIM))
    chunk_decay = tuple(float((1.0 - 2.0 ** (-5.0 - h)) ** chunk) for h in range(heads))

    blk = lambda col: pl.BlockSpec((SCAN_BLOCK, width), lambda b, i: (b * n_tblk + i, col))
    pos = pl.BlockSpec((SCAN_BLOCK, HEAD_DIM), lambda b, i: (i, 0))
    full3 = lambda a: pl.BlockSpec(a.shape, lambda b, i: (0, 0, 0))
    return pl.pallas_call(
        functools.partial(_ret_kernel, chunk=chunk, heads=heads, chunk_decay=chunk_decay),
        grid=(batch, n_tblk),
        in_specs=[blk(col0), blk(col0 + 1), blk(col0 + 2), blk(col0 + 3), pos, pos,
                  full3(intra), full3(cross), full3(zeta),
                  pl.BlockSpec((1, width), lambda b, i: (0, 0))],
        out_specs=pl.BlockSpec((SCAN_BLOCK, width), lambda b, i: (b * n_tblk + i, 0)),
        out_shape=jax.ShapeDtypeStruct((t, width), F32),
        scratch_shapes=[pltpu.VMEM((heads, HEAD_DIM, HEAD_DIM), F32)],
        compiler_params=_cp("parallel", "arbitrary"),
    )(proj, proj, proj, proj, cos_t, sin_t, intra, cross, zeta, norm_w)


def _merge_kernel(x_ref, *refs):
    y_refs, wg_refs = refs[:N_BRANCH], refs[N_BRANCH:2 * N_BRANCH]
    wb_ref, o_ref, xb_ref = refs[2 * N_BRANCH:]

    @pl.when(pl.program_id(1) == 0)
    def _():
        xb_ref[...] = x_ref[...].astype(BF16)

    xb = xb_ref[...]
    acc = None
    for n in range(N_BRANCH):
        gate = _sigmoid(jnp.dot(xb, wg_refs[n][...], preferred_element_type=F32))
        term = gate * jnp.dot(y_refs[n][...].astype(BF16), wb_ref[n], preferred_element_type=F32)
        acc = term if acc is None else acc + term
    o_ref[...] = acc


def _merge(x, branches, w_gate, gate_col0, w_branch, *, tm, tn):
    t, d = x.shape
    width = branches[0].shape[1]
    ybs = pl.BlockSpec((tm, width), lambda i, j: (i, 0))

    def gate_spec(n):
        return pl.BlockSpec((d, tn), lambda i, j: (0, (gate_col0 + n * d) // tn + j))

    return pl.pallas_call(
        _merge_kernel,
        grid=(t // tm, d // tn),
        in_specs=[pl.BlockSpec((tm, d), lambda i, j: (i, 0))] + [ybs] * N_BRANCH
                 + [gate_spec(n) for n in range(N_BRANCH)]
                 + [pl.BlockSpec((N_BRANCH, width, tn), lambda i, j: (0, 0, j))],
        out_specs=pl.BlockSpec((tm, tn), lambda i, j: (i, j)),
        out_shape=jax.ShapeDtypeStruct((t, d), F32),
        scratch_shapes=[pltpu.VMEM((tm, d), BF16)],
        compiler_params=_cp("parallel", "arbitrary"),
    )(x, *branches, *([w_gate] * N_BRANCH), w_branch)


def _proj_ln_kernel(a_ref, w_ref, r_ref, lw_ref, lb_ref, o_ref, *, alpha):
    y = jnp.dot(a_ref[...].astype(BF16), w_ref[...], preferred_element_type=F32)
    o_ref[...] = _layer_norm(alpha * r_ref[...] + y, lw_ref[...], lb_ref[...])


def _proj_ln(a, w, resid, ln_w, ln_b, *, alpha, tm):
    t, k = a.shape
    d = w.shape[1]
    row = lambda n: pl.BlockSpec((tm, n), lambda i: (i, 0))
    const = lambda shape: pl.BlockSpec(shape, lambda i: (0, 0))
    return pl.pallas_call(
        functools.partial(_proj_ln_kernel, alpha=alpha),
        grid=(t // tm,),
        in_specs=[row(k), const((k, d)), row(d), const((1, d)), const((1, d))],
        out_specs=row(d),
        out_shape=jax.ShapeDtypeStruct((t, d), F32),
        compiler_params=_cp("parallel"),
    )(a, w, resid, ln_w, ln_b)


def _xattn_kernel(x_ref, wq_ref, kv_ref, wo_ref, lw_ref, lb_ref, o_ref, *, alpha, heads):
    x = x_ref[...]
    hd = HEAD_DIM
    inner = heads * hd
    q = jnp.dot(x.astype(BF16), wq_ref[...], preferred_element_type=F32)
    outs = []
    for h in range(heads):
        qh = q[:, h * hd:(h + 1) * hd].astype(BF16)
        kh = kv_ref[:, h * hd:(h + 1) * hd].astype(BF16)
        vh = kv_ref[:, inner + h * hd:inner + (h + 1) * hd].astype(BF16)
        s = lax.dot_general(qh, kh, (((1,), (1,)), ((), ())), preferred_element_type=F32) * hd ** -0.5
        s = s - jnp.max(s, axis=-1, keepdims=True)
        e = jnp.exp(s)
        p = e / jnp.sum(e, axis=-1, keepdims=True)
        outs.append(jnp.dot(p.astype(BF16), vh, preferred_element_type=F32).astype(BF16))
    o = jnp.concatenate(outs, axis=-1)
    y = jnp.dot(o, wo_ref[...], preferred_element_type=F32)
    o_ref[...] = _layer_norm(alpha * x + y, lw_ref[...], lb_ref[...])


def _xattn(x, kv, wq, wo, ln_w, ln_b, *, alpha, seq, mem_len, tm):
    t, d = x.shape
    inner = wq.shape[1]
    n_tblk = seq // tm
    const = lambda shape: pl.BlockSpec(shape, lambda i: (0, 0))
    return pl.pallas_call(
        functools.partial(_xattn_kernel, alpha=alpha, heads=XATTN_HEADS),
        grid=(t // tm,),
        in_specs=[pl.BlockSpec((tm, d), lambda i: (i, 0)), const((d, inner)),
                  pl.BlockSpec((mem_len, 2 * inner), lambda i: (i // n_tblk, 0)),
                  const((inner, d)), const((1, d)), const((1, d))],
        out_specs=pl.BlockSpec((tm, d), lambda i: (i, 0)),
        out_shape=jax.ShapeDtypeStruct((t, d), F32),
        compiler_params=_cp("parallel"),
    )(x, wq, kv, wo, ln_w, ln_b)


def _route_kernel(x_ref, wr_ref, rb_ref, w_ref, dest_ref, blk_ref, seg_ref, idx_all, pos_all, carry, *, tm, bm):
    step = pl.program_id(0)

    @pl.when(step == 0)
    def _():
        carry[...] = jnp.zeros_like(carry)

    e_n, g_n = N_EXPERTS, N_GROUPS
    per = e_n // g_n
    logits = lax.dot_general(wr_ref[...], x_ref[...], (((1,), (1,)), ((), ())),
                             precision=lax.Precision.HIGHEST, preferred_element_type=F32)
    scores = _sigmoid(logits)
    biased = scores + rb_ref[...]
    b3 = biased.reshape(g_n, per, tm)
    member = lax.broadcasted_iota(I32, (g_n, per, tm), 1)
    top1 = jnp.max(b3, axis=1, keepdims=True)
    first = jnp.min(jnp.where(b3 == top1, member, per), axis=1, keepdims=True)
    top2 = jnp.max(jnp.where(member == first, -jnp.inf, b3), axis=1, keepdims=True)
    gs = top1 + top2
    gid = lax.broadcasted_iota(I32, (g_n, 1, tm), 0)
    rank = jnp.zeros((g_n, 1, tm), I32)
    for other in range(g_n):
        o = gs[other:other + 1]
        ahead = jnp.logical_or(o > gs, jnp.logical_and(o == gs, other < gid))
        rank = rank + jnp.where(ahead, 1, 0)
    cur = jnp.where(rank < TOPK_GROUPS, b3, -jnp.inf).reshape(e_n, tm)

    eid = lax.broadcasted_iota(I32, (e_n, tm), 0)
    picks, vals = [], []
    sel = jnp.zeros((e_n, tm), F32)
    for k in range(TOP_K):
        mx = jnp.max(cur, axis=0, keepdims=True)
        ik = jnp.min(jnp.where(cur == mx, eid, e_n), axis=0, keepdims=True)
        hit = eid == ik
        vals.append(jnp.sum(jnp.where(hit, scores, 0.0), axis=0, keepdims=True))
        cur = jnp.where(hit, -jnp.inf, cur)
        sel = jnp.where(hit, 1.0, sel)
        picks.append(ik)
    total = vals[0]
    for v in vals[1:]:
        total = total + v

    tri =jnp.where(lax.broadcasted_iota(I32, (tm, tm), 0) <= lax.broadcasted_iota(I32, (tm, tm), 1), 1.0, 0.0)
    incl = jnp.dot(sel.astype(BF16), tri.astype(BF16), preferred_element_type=F32)
    base = carry[:, 0:1] + incl - 1.0
    for k in range(TOP_K):
        idx_all[step, k:k + 1, :] = picks[k]
        w_ref[k:k + 1, :] = vals[k] / total * ROUTE_SCALE
        pos_all[step, k:k + 1, :] = jnp.sum(jnp.where(eid == picks[k], base, 0.0),
                                            axis=0, keepdims=True).astype(I32)
    carry[...] = carry[...] + incl[:, tm - 1:tm]

    @pl.when(step == pl.num_programs(0) - 1)
    def _():
        shift = bm.bit_length() - 1
        pcnt = jnp.left_shift(jnp.right_shift(carry[...].astype(I32) + (bm - 1), shift), shift)
        pend = _cum_rows(pcnt, jnp.add, 0)
        pstart = pend - pcnt
        starts = [pstart[e:e + 1, 0:1] for e in range(e_n)]

        def tile_dest(i, c):
            idx_t = idx_all[i]
            acc = pos_all[i]
            for e in range(e_n):
                acc = acc + jnp.where(idx_t == e, starts[e], 0)
            dest_ref[i] = acc
            return c

        lax.fori_loop(0, pl.num_programs(0), tile_dest, 0)

        nb = blk_ref.shape[1]
        row0 = lax.broadcasted_iota(I32, (e_n, nb), 1) * bm
        total_rows = pend[e_n - 1:e_n, 0:1]
        owner = jnp.sum(jnp.where(pend[:, 0:1] <= row0, 1, 0), axis=0, keepdims=True)
        last_owner = jnp.sum(jnp.where(pend[:, 0:1] < total_rows, 1, 0), axis=0, keepdims=True)
        blk_ref[...] = jnp.where(row0[0:1, :] < total_rows, jnp.minimum(owner, e_n - 1), last_owner)
        seg_ref[0] = pend
        seg_ref[1] = pcnt


def _route(x, router_w_t, router_b, *, tm, bm, n_blocks):
    t, d = x.shape
    e_n = N_EXPERTS
    n_t = t // tm
    assert bm & (bm - 1) == 0
    nb_pad = -(-n_blocks // LANES) * LANES
    return pl.pallas_call(
        functools.partial(_route_kernel, tm=tm, bm=bm),
        grid=(n_t,),
        in_specs=[pl.BlockSpec((tm, d), lambda i: (i, 0)),
                  pl.BlockSpec((e_n, d), lambda i: (0, 0)),
                  pl.BlockSpec((e_n, 1), lambda i: (0, 0))],
        out_specs=[pl.BlockSpec((TOP_K, tm), lambda i: (0, i)),
                   pl.BlockSpec((n_t, TOP_K, tm), lambda i: (0, 0, 0)),
                   pl.BlockSpec((1, nb_pad), lambda i: (0, 0)),
                   pl.BlockSpec((2, e_n, LANES), lambda i: (0, 0, 0))],
        out_shape=[jax.ShapeDtypeStruct((TOP_K, t), F32), jax.ShapeDtypeStruct((n_t, TOP_K, tm), I32),
                   jax.ShapeDtypeStruct((1, nb_pad), I32), jax.ShapeDtypeStruct((2, e_n, LANES), I32)],
        scratch_shapes=[pltpu.VMEM((n_t, TOP_K, tm), I32), pltpu.VMEM((n_t, TOP_K, tm), I32),
                        pltpu.VMEM((e_n, LANES), F32)],
        compiler_params=_cp("arbitrary"),
    )(x, router_w_t, router_b)


def _dest_base(tile, tm):
    t0 = tile * tm
    return (t0 // ROUTE_TM) * (TOP_K * ROUTE_TM) + t0 % ROUTE_TM


def _dispatch_kernel(dest_ref, pend_ref, pcnt_ref, x_ref, xs_ref, zbuf, sem, zsem, *, tm, bm):
    i = pl.program_id(0)
    base = _dest_base(i, tm)

    def pad_copy(e):
        return pltpu.make_async_copy(zbuf, xs_ref.at[pl.ds(pl.multiple_of(pend_ref[e] - bm, bm), bm), :], zsem)

    @pl.when(i == 0)
    def _():
        zbuf[...] = jnp.zeros_like(zbuf)

        def start(e, c):
            @pl.when(pcnt_ref[e] > 0)
            def _():
                pad_copy(e).start()
            return c

        def wait(e, c):
            @pl.when(pcnt_ref[e] > 0)
            def _():
                pad_copy(e).wait()
            return c

        lax.fori_loop(0, N_EXPERTS, start, 0)
        lax.fori_loop(0, N_EXPERTS, wait, 0)

    def row_copy(r, k):
        d = dest_ref[base + k * ROUTE_TM + r]
        return pltpu.make_async_copy(x_ref.at[pl.ds(r, 1), :], xs_ref.at[pl.ds(d, 1), :], sem)

    def start_row(r, c):
        for k in range(TOP_K):
            row_copy(r, k).start()
        return c

    def wait_row(r, c):
        for k in range(TOP_K):
            row_copy(r, k).wait()
        return c

    lax.fori_loop(0, tm, start_row, 0)
    lax.fori_loop(0, tm, wait_row, 0)


def _dispatch(x, dest_flat, pend, pcnt, *, rows, tm, bm):
    t, d = x.shape
    return pl.pallas_call(
        functools.partial(_dispatch_kernel, tm=tm, bm=bm),
        grid_spec=pltpu.PrefetchScalarGridSpec(
            num_scalar_prefetch=3,
            grid=(t // tm,),
            in_specs=[pl.BlockSpec((tm, d), lambda i, *_: (i, 0))],
            out_specs=pl.BlockSpec(memory_space=pl.ANY),
            scratch_shapes=[pltpu.VMEM((bm, d), F32), pltpu.SemaphoreType.DMA(()), pltpu.SemaphoreType.DMA(())],
        ),
        out_shape=jax.ShapeDtypeStruct((rows, d), F32),
        compiler_params=_cp("arbitrary"),
    )(dest_flat, pend, pcnt, x)


def _expert_kernel(blk_e_ref, nused_ref, xs_ref, wgu_ref, wdn_ref, ys_ref, wgu_b, wdn_b):
    j = pl.program_id(0)
    used = j < nused_ref[0]
    changed = jnp.logical_or(j == 0, blk_e_ref[j] != blk_e_ref[jnp.maximum(j - 1, 0)])

    @pl.when(jnp.logical_and(used, changed))
    def _():
        wgu_b[...] = wgu_ref[...].astype(BF16)
        wdn_b[...] = wdn_ref[...].astype(BF16)

    @pl.when(used)
    def _():
        f = wdn_b.shape[0]
        gu = jnp.dot(xs_ref[...].astype(BF16), wgu_b[...], preferred_element_type=F32)
        hidden = (_silu(gu[:, :f]) * gu[:, f:]).astype(BF16)
        ys_ref[...] = jnp.dot(hidden, wdn_b[...], preferred_element_type=F32)


def _experts(xs, blk_e, nused, w_gu, w_dn, *, bm):
    rows, d = xs.shape
    f2 = w_gu.shape[2]
    f = w_dn.shape[1]
    blk = lambda j, be, nu: (jnp.minimum(j, nu[0] - 1), 0)
    return pl.pallas_call(
        _expert_kernel,
        grid_spec=pltpu.PrefetchScalarGridSpec(
            num_scalar_prefetch=2,
            grid=(rows // bm,),
            in_specs=[pl.BlockSpec((bm, d), blk),
                      pl.BlockSpec((None, d, f2), lambda j, be, nu: (be[j], 0, 0)),
                      pl.BlockSpec((None, f, d), lambda j, be, nu: (be[j], 0, 0))],
            out_specs=pl.BlockSpec((bm, d), blk),
            scratch_shapes=[pltpu.VMEM((d, f2), BF16), pltpu.VMEM((f, d), BF16)],
        ),
        out_shape=jax.ShapeDtypeStruct((rows, d), F32),
        compiler_params=_cp("arbitrary"),
    )(blk_e, nused, xs, w_gu, w_dn)


def _combine_kernel(dest_ref, x_ref, w_ref, sgu_ref, sdn_ref, lw_ref, lb_ref, ys_ref, o_ref, buf, sem,
                    *, tm, alpha):
    i = pl.program_id(0)
    base = _dest_base(i, tm)

    def row_copy(r, k):
        d = dest_ref[base + k * ROUTE_TM + r]
        return pltpu.make_async_copy(ys_ref.at[pl.ds(d, 1), :], buf.at[k, pl.ds(r, 1), :], sem)

    def start_row(r, c):
        for k in range(TOP_K):
            row_copy(r, k).start()
        return c

    def wait_row(r, c):
        for k in range(TOP_K):
            row_copy(r, k).wait()
        return c

    lax.fori_loop(0, tm, start_row, 0)

    x = x_ref[...]
    f = sdn_ref.shape[0]
    gu = jnp.dot(x.astype(BF16), sgu_ref[...], preferred_element_type=F32)
    hidden = (_silu(gu[:, :f]) * gu[:, f:]).astype(BF16)
    acc = alpha * x + jnp.dot(hidden, sdn_ref[...], preferred_element_type=F32)

    lax.fori_loop(0, tm, wait_row, 0)
    for k in range(TOP_K):
        acc = acc + w_ref[:, k:k + 1] * buf[k]
    o_ref[...] = _layer_norm(acc, lw_ref[...], lb_ref[...])


def _combine(x, ys, dest_flat, w_tk, s_gu, s_dn, ln_w, ln_b, *, alpha, tm):
    t, d = x.shape
    const = lambda shape: pl.BlockSpec(shape, lambda i, *_: (0, 0))
    return pl.pallas_call(
        functools.partial(_combine_kernel, tm=tm, alpha=alpha),
        grid_spec=pltpu.PrefetchScalarGridSpec(
            num_scalar_prefetch=1,
            grid=(t // tm,),
            in_specs=[pl.BlockSpec((tm, d), lambda i, *_: (i, 0)),
                      pl.BlockSpec((tm, TOP_K), lambda i, *_: (i, 0)),
                      const(s_gu.shape), const(s_dn.shape), const((1, d)), const((1, d)),
                      pl.BlockSpec(memory_space=pl.ANY)],
            out_specs=pl.BlockSpec((tm, d), lambda i, *_: (i, 0)),
            scratch_shapes=[pltpu.VMEM((TOP_K, tm, d), F32), pltpu.SemaphoreType.DMA(())],
        ),
        out_shape=jax.ShapeDtypeStruct((t, d), F32),
        compiler_params=_cp("arbitrary"),
    )(dest_flat, x, w_tk, s_gu, s_dn, ln_w, ln_b, ys)


def _mixer_sublayer(x, w_in, gate_b, pool_w, pool_scale, conv_w, mlstm_norm_w, ret_norm_w, w_branch, w_out,
                    ln_w, ln_b, *, batch, seq, alpha):
    t, d = x.shape
    width = d // N_BRANCH
    heads = width // HEAD_DIM
    gate_off = 8 * width
    ret_off = gate_off + 2 * heads
    w_head = w_in[:, :gate_off].astype(BF16)
    w_if = jnp.pad(w_in[:, gate_off:ret_off], ((0, 0), (0, LANES - 2 * heads))).astype(BF16)
    w_tail = _shifted_cast(w_in, col0=gate_off, shift=2 * heads, ncols=4 * width + N_BRANCH * d,
                           tr=512, tn=512)
    gate_bias = jnp.pad(gate_b, (0, LANES - 2 * heads)).reshape(1, LANES)

    proj_a = _matmul(x, w_head, tm=1024, tn=512)
    proj_b = _matmul(x, w_tail, tm=1024, tn=512, ncols=4 * width)
    gates = _matmul(x, w_if, tm=1024, tn=LANES)
    y_pool, y_conv = _pool_conv(proj_a, pool_w.astype(BF16), pool_scale.reshape(1, width), conv_w,
                                seq=seq, width=width, tb=512)
    y_mlstm = _mlstm(proj_a, gates, gate_bias, mlstm_norm_w.reshape(1, width),
                     batch=batch, seq=seq, width=width, col0=4)
    y_ret = _retention(proj_b, ret_norm_w.reshape(1, width), batch=batch, seq=seq, width=width, col0=0)
    merged = _merge(x, (y_pool, y_conv, y_mlstm, y_ret), w_tail, 4 * width, w_branch.astype(BF16),
                    tm=512, tn=512)
    return _proj_ln(merged, w_out.astype(BF16), x, ln_w, ln_b, alpha=alpha, tm=256)


def _xattn_sublayer(x, mem2d, wq, wk, wv, wo, ln_w, ln_b, *, seq, mem_len, alpha):
    w_kv = jnp.concatenate([wk, wv], axis=1).astype(BF16)
    kv = _matmul(mem2d, w_kv, tm=min(mem2d.shape[0], 1024), tn=512)
    return _xattn(x, kv, wq.astype(BF16), wo.astype(BF16), ln_w, ln_b,
                  alpha=alpha, seq=seq, mem_len=mem_len, tm=256)


def _moe_sublayer(x, router_w, router_b, w_gu, w_dn, s_gu, s_dn, ln_w, ln_b, *, alpha):
    t, d = x.shape
    e_n, bm = N_EXPERTS, MOE_BM
    n_blocks = (t * TOP_K) // bm + e_n
    wts, dest, blk, seg = _route(x, router_w.T, router_b.reshape(e_n, 1), tm=ROUTE_TM, bm=bm, n_blocks=n_blocks)
    dest_flat = dest.reshape(-1)
    pend, pcnt = seg[0, :, 0], seg[1, :, 0]
    blk_e = blk[0, :n_blocks]
    nused = pend[e_n - 1] // bm

    xs = _dispatch(x, dest_flat, pend, pcnt, rows=n_blocks * bm, tm=256, bm=bm)
    ys = _experts(xs, blk_e, nused.reshape(1), w_gu, w_dn, bm=bm)
    return _combine(x, ys, dest_flat, wts.T, s_gu.astype(BF16), s_dn.astype(BF16), ln_w, ln_b,
                    alpha=alpha, tm=128)


def kernel(x, mem, w_in, mlstm_gate_b, pool_w, pool_scale, conv_w, mlstm_norm_w, ret_norm_w, w_branch,
           w_mix_out, xa_wq, xa_wk, xa_wv, xa_wo, router_w, router_b, moe_w_gu, moe_w_dn, shared_w_gu,
           shared_w_dn, ln_w, ln_b):
    batch, seq, d = x.shape
    depth = w_in.shape[0]
    mem_len = mem.shape[1]
    alpha = (2 * depth) ** 0.25
    h = x.reshape(batch * seq, d)
    mem2d = mem.reshape(batch * mem_len, d)
    for l in range(depth):
        lw = ln_w[l].reshape(3, 1, d)
        lb = ln_b[l].reshape(3, 1, d)
        h = _mixer_sublayer(h, w_in[l], mlstm_gate_b[l], pool_w[l], pool_scale[l], conv_w[l], mlstm_norm_w[l],
                            ret_norm_w[l], w_branch[l], w_mix_out[l], lw[0], lb[0],
                            batch=batch, seq=seq, alpha=alpha)
        h = _xattn_sublayer(h, mem2d, xa_wq[l], xa_wk[l], xa_wv[l], xa_wo[l], lw[1], lb[1],
                            seq=seq, mem_len=mem_len, alpha=alpha)
        h = _moe_sublayer(h, router_w[l], router_b[l], moe_w_gu[l], moe_w_dn[l], shared_w_gu[l], shared_w_dn[l],
                          lw[2], lb[2], alpha=alpha)
    return h.reshape(batch, seq, d)
```

```python
import functools

import numpy as np
import jax
import jax.numpy as jnp
from jax import lax
from jax.experimental import pallas as pl
from jax.experimental.pallas import tpu as pltpu

F32 = jnp.float32
BF16 = jnp.bfloat16
I32 = jnp.int32

N_BRANCH = 4
HEAD_DIM = 128
POOL_WINDOWS = (2, 4, 8, 16)
CONV_WIDTH = 3
ROPE_BASE = 10000.0
XATTN_HEADS = 4
N_EXPERTS = 64
TOP_K = 8
N_GROUPS = 8
TOPK_GROUPS = 4
ROUTE_SCALE = 2.5
LN_EPS = 1e-5

LANES = 128
V7X_VMEM_BYTES = 64 * 1024 * 1024
VMEM_LIMIT = 56 * 1024 * 1024

SCAN_CHUNK = 128
SCAN_BLOCK = 512
HALO = 16
MOE_BM = 256
ROUTE_TM = 512
SEG_ALIGN = 16
ONEHOT_ROWS = 512


def _cp(*sem):
    return pltpu.CompilerParams(dimension_semantics=sem, vmem_limit_bytes=VMEM_LIMIT)


def _sigmoid(x):
    return 1.0 / (1.0 + jnp.exp(-x))


def _silu(x):
    return x * _sigmoid(x)


def _log_sigmoid(x):
    return jnp.minimum(x, 0.0) - jnp.log(1.0 + jnp.exp(-jnp.abs(x)))


def _layer_norm(z, w, b):
    mu = jnp.mean(z, axis=-1, keepdims=True)
    d = z - mu
    var = jnp.mean(d * d, axis=-1, keepdims=True)
    return d * lax.rsqrt(var + LN_EPS) * w + b


def _head_norm(h, w):
    mu = jnp.mean(h, axis=-1, keepdims=True)
    d = h - mu
    var = jnp.mean(d * d, axis=-1, keepdims=True)
    return d * lax.rsqrt(var + LN_EPS) * w


def _mm_kernel(x_ref, w_ref, o_ref, xb_ref):
    @pl.when(pl.program_id(1) == 0)
    def _():
        xb_ref[...] = x_ref[...].astype(BF16)

    o_ref[...] = jnp.dot(xb_ref[...], w_ref[...], preferred_element_type=F32)


def _shift_cast_kernel(a_ref, b_ref, o_ref, *, shift):
    tn = o_ref.shape[1]
    both = jnp.concatenate([a_ref[...], b_ref[...]], axis=1)
    o_ref[...] = both[:, shift:shift + tn].astype(BF16)


def _shifted_cast(w, *, col0, shift, ncols, tr, tn):
    rows = w.shape[0]
    return pl.pallas_call(
        functools.partial(_shift_cast_kernel, shift=shift),
        grid=(rows // tr, ncols // tn),
        in_specs=[pl.BlockSpec((tr, tn), lambda i, j: (i, col0 // tn + j)),
                  pl.BlockSpec((tr, LANES), lambda i, j: (i, (col0 + (j + 1) * tn) // LANES))],
        out_specs=pl.BlockSpec((tr, tn), lambda i, j: (i, j)),
        out_shape=jax.ShapeDtypeStruct((rows, ncols), BF16),
        compiler_params=_cp("parallel", "parallel"),
    )(w, w)


def _matmul(x, w, *, tm, tn, ncols=None):
    t, k = x.shape
    n = w.shape[1] if ncols is None else ncols
    return pl.pallas_call(
        _mm_kernel,
        grid=(t // tm, n // tn),
        in_specs=[pl.BlockSpec((tm, k), lambda i, j: (i, 0)),
                  pl.BlockSpec((k, tn), lambda i, j: (0, j))],
        out_specs=pl.BlockSpec((tm, tn), lambda i, j: (i, j)),
        out_shape=jax.ShapeDtypeStruct((t, n), F32),
        scratch_shapes=[pltpu.VMEM((tm, k), BF16)],
        compiler_params=_cp("parallel", "arbitrary"),
    )(x, w)


def _poolconv_kernel(u_ref, uh_ref, h_ref, hh_ref, b_ref, c_ref, ch_ref, pw_ref, ps_ref, cw_ref,
                     yp_ref, yc_ref, ubuf, zbuf, *, tb, n_tblk):
    first = (pl.program_id(0) % n_tblk) == 0
    ubuf[0:HALO, :] = jnp.where(first, 0.0, uh_ref[...])
    ubuf[HALO:HALO + tb, :] = u_ref[...]
    zbuf[0:HALO, :] = jnp.where(first, 0.0, ch_ref[...] * hh_ref[...])
    zbuf[HALO:HALO + tb, :] = c_ref[...] * h_ref[...]

    t_pos = (pl.program_id(0) % n_tblk) * tb + lax.broadcasted_iota(I32, (tb, LANES), 0)
    gw = u_ref.shape[1] // len(POOL_WINDOWS)
    for grp, win in enumerate(POOL_WINDOWS):
        lanes = slice(grp * gw, (grp + 1) * gw)
        cur = ubuf[HALO:HALO + tb, lanes]
        acc = cur
        for lag in range(1, win):
            acc = acc + ubuf[HALO - lag:HALO - lag + tb, lanes]
        count = jnp.minimum(t_pos + 1, win).astype(F32)
        mixed = acc / count - cur
        y = jnp.dot(mixed.astype(BF16), pw_ref[grp], preferred_element_type=F32)
        yp_ref[:, lanes] = y * ps_ref[:, lanes]

    conv = cw_ref[0:1, :] * zbuf[HALO:HALO + tb, :]
    for lag in range(1, CONV_WIDTH):
        conv = conv + cw_ref[lag:lag + 1, :] * zbuf[HALO - lag:HALO - lag + tb, :]
    yc_ref[...] = b_ref[...] * conv


def _pool_conv(proj, pool_w, pool_scale, conv_w, *, seq, width, tb):
    t = proj.shape[0]
    n_tblk = seq // tb
    ratio = tb // HALO

    def cur(col):
        return pl.BlockSpec((tb, width), lambda g: (g, col))

    def halo(col):
        return pl.BlockSpec((HALO, width), lambda g: (jnp.maximum(g * ratio - 1, 0), col))

    full = lambda shape: pl.BlockSpec(shape, lambda g: (0,) * len(shape))
    return pl.pallas_call(
        functools.partial(_poolconv_kernel, tb=tb, n_tblk=n_tblk),
        grid=(t // tb,),
        in_specs=[cur(0), halo(0), cur(1), halo(1), cur(2), cur(3), halo(3),
                  full(pool_w.shape), full(pool_scale.shape), full(conv_w.shape)],
        out_specs=[pl.BlockSpec((tb, width), lambda g: (g, 0))] * 2,
        out_shape=[jax.ShapeDtypeStruct((t, width), F32)] * 2,
        scratch_shapes=[pltpu.VMEM((HALO + tb, width), F32)] * 2,
        compiler_params=_cp("parallel"),
    )(proj, proj, proj, proj, proj, proj, proj, pool_w, pool_scale, conv_w)


def _cum_rows(x, op, fill):
    n = x.shape[0]
    row = lax.broadcasted_iota(I32, x.shape, 0)
    shift = 1
    while shift < n:
        x = op(x, jnp.where(row >= shift, pltpu.roll(x, shift, axis=0), fill))
        shift *= 2
    return x


def _mlstm_kernel(q_ref, k_ref, v_ref, o_ref, g_ref, gb_ref, nw_ref, y_ref, c_scr, n_scr, m_scr,
                  *, chunk, heads):
    @pl.when(pl.program_id(1) == 0)
    def _():
        c_scr[...] = jnp.zeros_like(c_scr)
        n_scr[...] = jnp.zeros_like(n_scr)
        m_scr[...] = jnp.zeros_like(m_scr)

    tb = q_ref.shape[0]
    hd = HEAD_DIM
    scale = hd ** -0.5
    tri = (lax.broadcasted_iota(I32, (chunk, chunk), 0) >= lax.broadcasted_iota(I32, (chunk, chunk), 1))
    for c in range(tb // chunk):
        rows = slice(c * chunk, (c + 1) * chunk)
        gates = g_ref[rows, :] + gb_ref[...]
        lf = _log_sigmoid(pltpu.roll(gates, LANES - heads, axis=1))
        cumf = _cum_rows(lf, jnp.add, 0.0)
        a = gates - cumf
        m_prev = m_scr[...]
        mu = jnp.maximum(_cum_rows(a, jnp.maximum, -jnp.inf), m_prev)
        mu_last = mu[chunk - 1:chunk, :]
        a_t = a.T
        for h in range(heads):
            cols = slice(h * hd, (h + 1) * hd)
            q = q_ref[rows, cols]
            k = k_ref[rows, cols] * scale
            v = v_ref[rows, cols]
            qb, kb, vb = q.astype(BF16), k.astype(BF16), v.astype(BF16)
            mu_col = mu[:, h:h + 1]
            a_col = a[:, h:h + 1]
            m_prev_h = m_prev[:, h:h + 1]
            mu_last_h = mu_last[:, h:h + 1]
            dmat = jnp.exp(jnp.where(tri, a_t[h:h + 1, :] - mu_col, -jnp.inf))
            s = lax.dot_general(qb, kb, (((1,), (1,)), ((), ())), preferred_element_type=F32)
            p = dmat * s
            inter = jnp.exp(m_prev_h - mu_col)
            c_h = c_scr[h]
            n_h = n_scr[h]
            num = inter * jnp.dot(qb, c_h.astype(BF16), preferred_element_type=F32) \
                + jnp.dot(p.astype(BF16), vb, preferred_element_type=F32)
            den = inter * jnp.sum(q * n_h, axis=-1, keepdims=True) + jnp.sum(p, axis=-1, keepdims=True)
            floor = jnp.exp(-(cumf[:, h:h + 1] + mu_col))
            h_out = num / jnp.maximum(jnp.abs(den), floor)

            wg = jnp.exp(a_col - mu_last_h)
            decay = jnp.exp(m_prev_h - mu_last_h)
            kw = k * wg
            c_scr[h] = decay * c_h + lax.dot_general(kw.astype(BF16), vb, (((0,), (0,)), ((), ())),
                                                     preferred_element_type=F32)
            n_scr[h] = decay * n_h + jnp.sum(kw, axis=0, keepdims=True)

            gated = _sigmoid(o_ref[rows, cols]) * h_out
            y_ref[rows, cols] = _head_norm(gated, nw_ref[:, cols])
        m_scr[...] = cumf[chunk - 1:chunk, :] + mu_last


def _mlstm(proj, gates, gate_b, norm_w, *, batch, seq, width, col0):
    t = proj.shape[0]
    heads = width // HEAD_DIM
    n_tblk = seq // SCAN_BLOCK
    blk = lambda col: pl.BlockSpec((SCAN_BLOCK, width), lambda b, i: (b * n_tblk + i, col))
    return pl.pallas_call(
        functools.partial(_mlstm_kernel, chunk=SCAN_CHUNK, heads=heads),
        grid=(batch, n_tblk),
        in_specs=[blk(col0), blk(col0 + 1), blk(col0 + 2), blk(col0 + 3),
                  pl.BlockSpec((SCAN_BLOCK, LANES), lambda b, i: (b * n_tblk + i, 0)),
                  pl.BlockSpec((1, LANES), lambda b, i: (0, 0)),
                  pl.BlockSpec((1, width), lambda b, i: (0, 0))],
        out_specs=pl.BlockSpec((SCAN_BLOCK, width), lambda b, i: (b * n_tblk + i, 0)),
        out_shape=jax.ShapeDtypeStruct((t, width), F32),
        scratch_shapes=[pltpu.VMEM((heads, HEAD_DIM, HEAD_DIM), F32),
                        pltpu.VMEM((heads, 1, HEAD_DIM), F32),
                        pltpu.VMEM((1, LANES), F32)],
        compiler_params=_cp("parallel", "arbitrary"),
    )(proj, proj, proj, proj, gates, gate_b, norm_w)


def _ret_kernel(q_ref, k_ref, v_ref, g_ref, cos_ref, sin_ref, intra_ref, cross_ref, zeta_ref, nw_ref,
                y_ref, r_scr, *, chunk, heads, chunk_decay):
    @pl.when(pl.program_id(1) == 0)
    def _():
        r_scr[...] = jnp.zeros_like(r_scr)

    tb = q_ref.shape[0]
    hd = HEAD_DIM
    scale = hd ** -0.5
    for c in range(tb // chunk):
        rows = slice(c * chunk, (c + 1) * chunk)
        cos = cos_ref[rows, :]
        sin = sin_ref[rows, :]
        for h in range(heads):
            cols = slice(h * hd, (h + 1) * hd)
            q = q_ref[rows, cols]
            k = k_ref[rows, cols]
            q = q * cos + pltpu.roll(q, hd // 2, axis=1) * sin
            k = (k * cos + pltpu.roll(k, hd // 2, axis=1) * sin) * scale
            vb = v_ref[rows, cols].astype(BF16)
            qb = q.astype(BF16)
            s = lax.dot_general(qb, k.astype(BF16), (((1,), (1,)), ((), ())), preferred_element_type=F32)
            inner = jnp.dot((s * intra_ref[h]).astype(BF16), vb, preferred_element_type=F32)
            r_h = r_scr[h]
            crs = jnp.dot(qb, r_h.astype(BF16), preferred_element_type=F32) * cross_ref[h]
            r_scr[h] = chunk_decay[h] * r_h + lax.dot_general(
                (k * zeta_ref[h]).astype(BF16), vb, (((0,), (0,)), ((), ())), preferred_element_type=F32)
            y_ref[rows, cols] = _silu(g_ref[rows, cols]) * _head_norm(inner + crs, nw_ref[:, cols])


def _retention(proj, norm_w, *, batch, seq, width, col0):
    t = proj.shape[0]
    heads = width // HEAD_DIM
    chunk = SCAN_CHUNK
    n_tblk = seq // SCAN_BLOCK
    half = HEAD_DIM // 2
    inv = ROPE_BASE ** (-jnp.arange(half, dtype=F32) / half)
    ang = jnp.arange(seq, dtype=F32)[:, None] * inv[None, :]
    cos_t = jnp.concatenate([jnp.cos(ang), jnp.cos(ang)], axis=-1)
    sin_t = jnp.concatenate([-jnp.sin(ang), jnp.sin(ang)], axis=-1)
    log_g = jnp.log(1.0 - 2.0 ** (-5.0 - jnp.arange(heads, dtype=F32)))
    tt = jnp.arange(chunk, dtype=F32)
    lag = tt[:, None] - tt[None, :]
    intra = jnp.where(lag >= 0, jnp.exp(jnp.maximum(lag, 0.0)[None] * log_g[:, None, None]), 0.0)
    cross = jnp.broadcast_to(jnp.exp((tt + 1.0)[None, :] * log_g[:, None])[:, :, None], (heads, chunk, HEAD_DIM))
    zeta = jnp.broadcast_to(jnp.exp((chunk - 1.0 - tt)[None, :] * log_g[:, None])[:, :, None],
                            (heads, chunk, HEAD_DIM))
    chunk_decay = tuple(float((1.0 - 2.0 ** (-5.0 - h)) ** chunk) for h in range(heads))

    blk = lambda col: pl.BlockSpec((SCAN_BLOCK, width), lambda b, i: (b * n_tblk + i, col))
    pos = pl.BlockSpec((SCAN_BLOCK, HEAD_DIM), lambda b, i: (i, 0))
    full3 = lambda a: pl.BlockSpec(a.shape, lambda b, i: (0, 0, 0))
    return pl.pallas_call(
        functools.partial(_ret_kernel, chunk=chunk, heads=heads, chunk_decay=chunk_decay),
        grid=(batch, n_tblk),
        in_specs=[blk(col0), blk(col0 + 1), blk(col0 + 2), blk(col0 + 3), pos, pos,
                  full3(intra), full3(cross), full3(zeta),
                  pl.BlockSpec((1, width), lambda b, i: (0, 0))],
        out_specs=pl.BlockSpec((SCAN_BLOCK, width), lambda b, i: (b * n_tblk + i, 0)),
        out_shape=jax.ShapeDtypeStruct((t, width), F32),
        scratch_shapes=[pltpu.VMEM((heads, HEAD_DIM, HEAD_DIM), F32)],
        compiler_params=_cp("parallel", "arbitrary"),
    )(proj, proj, proj, proj, cos_t, sin_t, intra, cross, zeta, norm_w)


def _merge_kernel(x_ref, *refs):
    y_refs, wg_refs = refs[:N_BRANCH], refs[N_BRANCH:2 * N_BRANCH]
    wb_ref, o_ref, xb_ref = refs[2 * N_BRANCH:]

    @pl.when(pl.program_id(1) == 0)
    def _():
        xb_ref[...] = x_ref[...].astype(BF16)

    xb = xb_ref[...]
    acc = None
    for n in range(N_BRANCH):
        gate = _sigmoid(jnp.dot(xb, wg_refs[n][...], preferred_element_type=F32))
        term = gate * jnp.dot(y_refs[n][...].astype(BF16), wb_ref[n], preferred_element_type=F32)
        acc = term if acc is None else acc + term
    o_ref[...] = acc


def _merge(x, branches, w_gate, gate_col0, w_branch, *, tm, tn):
    t, d = x.shape
    width = branches[0].shape[1]
    ybs = pl.BlockSpec((tm, width), lambda i, j: (i, 0))

    def gate_spec(n):
        return pl.BlockSpec((d, tn), lambda i, j: (0, (gate_col0 + n * d) // tn + j))

    return pl.pallas_call(
        _merge_kernel,
        grid=(t // tm, d // tn),
        in_specs=[pl.BlockSpec((tm, d), lambda i, j: (i, 0))] + [ybs] * N_BRANCH
                 + [gate_spec(n) for n in range(N_BRANCH)]
                 + [pl.BlockSpec((N_BRANCH, width, tn), lambda i, j: (0, 0, j))],
        out_specs=pl.BlockSpec((tm, tn), lambda i, j: (i, j)),
        out_shape=jax.ShapeDtypeStruct((t, d), F32),
        scratch_shapes=[pltpu.VMEM((tm, d), BF16)],
        compiler_params=_cp("parallel", "arbitrary"),
    )(x, *branches, *([w_gate] * N_BRANCH), w_branch)


def _proj_ln_kernel(a_ref, w_ref, r_ref, lw_ref, lb_ref, o_ref, *, alpha):
    y = jnp.dot(a_ref[...].astype(BF16), w_ref[...], preferred_element_type=F32)
    o_ref[...] = _layer_norm(alpha * r_ref[...] + y, lw_ref[...], lb_ref[...])


def _proj_ln(a, w, resid, ln_w, ln_b, *, alpha, tm):
    t, k = a.shape
    d = w.shape[1]
    row = lambda n: pl.BlockSpec((tm, n), lambda i: (i, 0))
    const = lambda shape: pl.BlockSpec(shape, lambda i: (0, 0))
    return pl.pallas_call(
        functools.partial(_proj_ln_kernel, alpha=alpha),
        grid=(t // tm,),
        in_specs=[row(k), const((k, d)), row(d), const((1, d)), const((1, d))],
        out_specs=row(d),
        out_shape=jax.ShapeDtypeStruct((t, d), F32),
        compiler_params=_cp("parallel"),
    )(a, w, resid, ln_w, ln_b)


def _xattn_kernel(x_ref, wq_ref, kv_ref, wo_ref, lw_ref, lb_ref, o_ref, *, alpha, heads):
    x = x_ref[...]
    hd = HEAD_DIM
    inner = heads * hd
    q = jnp.dot(x.astype(BF16), wq_ref[...], preferred_element_type=F32)
    outs = []
    for h in range(heads):
        qh = q[:, h * hd:(h + 1) * hd].astype(BF16)
        kh = kv_ref[:, h * hd:(h + 1) * hd].astype(BF16)
        vh = kv_ref[:, inner + h * hd:inner + (h + 1) * hd].astype(BF16)
        s = lax.dot_general(qh, kh, (((1,), (1,)), ((), ())), preferred_element_type=F32) * hd ** -0.5
        s = s - jnp.max(s, axis=-1, keepdims=True)
        e = jnp.exp(s)
        p = e / jnp.sum(e, axis=-1, keepdims=True)
        outs.append(jnp.dot(p.astype(BF16), vh, preferred_element_type=F32).astype(BF16))
    o = jnp.concatenate(outs, axis=-1)
    y = jnp.dot(o, wo_ref[...], preferred_element_type=F32)
    o_ref[...] = _layer_norm(alpha * x + y, lw_ref[...], lb_ref[...])


def _xattn(x, kv, wq, wo, ln_w, ln_b, *, alpha, seq, mem_len, tm):
    t, d = x.shape
    inner = wq.shape[1]
    n_tblk = seq // tm
    const = lambda shape: pl.BlockSpec(shape, lambda i: (0, 0))
    return pl.pallas_call(
        functools.partial(_xattn_kernel, alpha=alpha, heads=XATTN_HEADS),
        grid=(t // tm,),
        in_specs=[pl.BlockSpec((tm, d), lambda i: (i, 0)), const((d, inner)),
                  pl.BlockSpec((mem_len, 2 * inner), lambda i: (i // n_tblk, 0)),
                  const((inner, d)), const((1, d)), const((1, d))],
        out_specs=pl.BlockSpec((tm, d), lambda i: (i, 0)),
        out_shape=jax.ShapeDtypeStruct((t, d), F32),
        compiler_params=_cp("parallel"),
    )(x, wq, kv, wo, ln_w, ln_b)


def _round_up_pow2(x, m):
    shift = m.bit_length() - 1
    return jnp.left_shift(jnp.right_shift(x + (m - 1), shift), shift)


def _route_kernel(x_ref, wr_ref, rb_ref, w_ref, lrow_ref, segtab_ref, blk_ref, seg_ref, size_all, tot, *, tm, bm):
    step = pl.program_id(0)

    @pl.when(step == 0)
    def _():
        tot[...] = jnp.zeros_like(tot)

    e_n, g_n = N_EXPERTS, N_GROUPS
    per = e_n // g_n
    logits = lax.dot_general(wr_ref[...], x_ref[...], (((1,), (1,)), ((), ())),
                             precision=lax.Precision.HIGHEST, preferred_element_type=F32)
    scores = _sigmoid(logits)
    biased = scores + rb_ref[...]
    b3 = biased.reshape(g_n, per, tm)
    member = lax.broadcasted_iota(I32, (g_n, per, tm), 1)
    top1 = jnp.max(b3, axis=1, keepdims=True)
    first = jnp.min(jnp.where(b3 == top1, member, per), axis=1, keepdims=True)
    top2 = jnp.max(jnp.where(member == first, -jnp.inf, b3), axis=1, keepdims=True)
    gs = top1 + top2
    gid = lax.broadcasted_iota(I32, (g_n, 1, tm), 0)
    rank = jnp.zeros((g_n, 1, tm), I32)
    for other in range(g_n):
        o = gs[other:other + 1]
        ahead = jnp.logical_or(o > gs, jnp.logical_and(o == gs, other < gid))
        rank = rank + jnp.where(ahead, 1, 0)
    cur = jnp.where(rank < TOPK_GROUPS, b3, -jnp.inf).reshape(e_n, tm)

    eid = lax.broadcasted_iota(I32, (e_n, tm), 0)
    picks, vals = [], []
    sel = jnp.zeros((e_n, tm), F32)
    for k in range(TOP_K):
        mx = jnp.max(cur, axis=0, keepdims=True)
        ik = jnp.min(jnp.where(cur == mx, eid, e_n), axis=0, keepdims=True)
        hit = eid == ik
        vals.append(jnp.sum(jnp.where(hit, scores, 0.0), axis=0, keepdims=True))
        cur = jnp.where(hit, -jnp.inf, cur)
        sel = jnp.where(hit, 1.0, sel)
        picks.append(ik)
    total = vals[0]
    for v in vals[1:]:
        total = total + v

    tri = jnp.where(lax.broadcasted_iota(I32, (tm, tm), 0) <= lax.broadcasted_iota(I32, (tm, tm), 1), 1.0, 0.0)
    incl = jnp.dot(sel.astype(BF16), tri.astype(BF16), preferred_element_type=F32)
    size = _round_up_pow2(jnp.broadcast_to(incl[:, tm - 1:tm], (e_n, LANES)).astype(I32), SEG_ALIGN)
    loff = _cum_rows(size, jnp.add, 0) - size
    base = loff[:, 0:1].astype(F32) + incl - 1.0
    for k in range(TOP_K):
        w_ref[k:k + 1, :] = vals[k] / total * ROUTE_SCALE
        lrow_ref[0, k:k + 1, :] = jnp.sum(jnp.where(eid == picks[k], base, 0.0),
                                          axis=0, keepdims=True).astype(I32)
    size_all[step] = size
    tot[...] = tot[...] + size

    @pl.when(step == pl.num_programs(0) - 1)
    def _():
        pcnt = _round_up_pow2(tot[...], bm)
        pend = _cum_rows(pcnt, jnp.add, 0)

        def tile_seg(i, run):
            segtab_ref[i, 0] = run
            segtab_ref[i, 1] = size_all[i]
            return run + size_all[i]

        lax.fori_loop(0, pl.num_programs(0), tile_seg, pend - pcnt)

        nb = blk_ref.shape[1]
        row0 = lax.broadcasted_iota(I32, (e_n, nb), 1) * bm
        total_rows = pend[e_n - 1:e_n, 0:1]
        owner = jnp.sum(jnp.where(pend[:, 0:1] <= row0, 1, 0), axis=0, keepdims=True)
        last_owner = jnp.sum(jnp.where(pend[:, 0:1] < total_rows, 1, 0), axis=0, keepdims=True)
        blk_ref[...] = jnp.where(row0[0:1, :] < total_rows, jnp.minimum(owner, e_n - 1), last_owner)
        seg_ref[0] = pend
        seg_ref[1] = pcnt


def _route(x, router_w_t, router_b, *, tm, bm, n_blocks):
    t, d = x.shape
    e_n = N_EXPERTS
    n_t = t // tm
    assert bm & (bm - 1) == 0
    nb_pad = -(-n_blocks // LANES) * LANES
    return pl.pallas_call(
        functools.partial(_route_kernel, tm=tm, bm=bm),
        grid=(n_t,),
        in_specs=[pl.BlockSpec((tm, d), lambda i: (i, 0)),
                  pl.BlockSpec((e_n, d), lambda i: (0, 0)),
                  pl.BlockSpec((e_n, 1), lambda i: (0, 0))],
        out_specs=[pl.BlockSpec((TOP_K, tm), lambda i: (0, i)),
                   pl.BlockSpec((1, TOP_K, tm), lambda i: (i, 0, 0)),
                   pl.BlockSpec((n_t, 2, e_n, LANES), lambda i: (0, 0, 0, 0)),
                   pl.BlockSpec((1, nb_pad), lambda i: (0, 0)),
                   pl.BlockSpec((2, e_n, LANES), lambda i: (0, 0, 0))],
        out_shape=[jax.ShapeDtypeStruct((TOP_K, t), F32), jax.ShapeDtypeStruct((n_t, TOP_K, tm), I32),
                   jax.ShapeDtypeStruct((n_t, 2, e_n, LANES), I32),
                   jax.ShapeDtypeStruct((1, nb_pad), I32), jax.ShapeDtypeStruct((2, e_n, LANES), I32)],
        scratch_shapes=[pltpu.VMEM((n_t, e_n, LANES), I32), pltpu.VMEM((e_n, LANES), I32)],
        compiler_params=_cp("arbitrary"),
    )(x, router_w_t, router_b)


def _tile_rows(tm):
    worst = TOP_K * tm + N_EXPERTS * (SEG_ALIGN - 1)
    return -(-worst // ONEHOT_ROWS) * ONEHOT_ROWS


def _onehot_rows(chunk, lrow, values, tm):
    rid = chunk * ONEHOT_ROWS + lax.broadcasted_iota(I32, (ONEHOT_ROWS, tm), 0)
    acc = jnp.zeros((ONEHOT_ROWS, tm), F32)
    for k in range(TOP_K):
        acc = jnp.where(rid == lrow[k:k + 1, :], 1.0 if values is None else values[k:k + 1, :], acc)
    return acc.astype(BF16)


def _segments(gstart_ref, size_ref, tile, make_copy):
    def walk(act):
        def body(e, loff):
            n = size_ref[tile * N_EXPERTS + e]

            @pl.when(n > 0)
            def _():
                act(make_copy(pl.multiple_of(gstart_ref[tile * N_EXPERTS + e], SEG_ALIGN),
                              pl.multiple_of(loff, SEG_ALIGN), pl.multiple_of(n, SEG_ALIGN)))
            return loff + n
        return lax.fori_loop(0, N_EXPERTS, body, 0)
    return (lambda: walk(lambda cp: cp.start())), (lambda: walk(lambda cp: cp.wait()))


def _dispatch_kernel(gstart_ref, size_ref, pend_ref, pcnt_ref, x_ref, lrow_ref, xs_ref, stage, sem, zsem,
                     *, tm, bm):
    i = pl.program_id(0)
    zbuf = stage.at[0:bm, :]

    def pad_copy(e):
        return pltpu.make_async_copy(zbuf, xs_ref.at[pl.ds(pl.multiple_of(pend_ref[e] - bm, bm), bm), :], zsem)

    @pl.when(i == 0)
    def _():
        stage[0:bm, :] = jnp.zeros((bm, stage.shape[1]), stage.dtype)

        def start(e, c):
            @pl.when(pcnt_ref[e] > 0)
            def _():
                pad_copy(e).start()
            return c

        def wait(e, c):
            @pl.when(pcnt_ref[e] > 0)
            def _():
                pad_copy(e).wait()
            return c

        lax.fori_loop(0, N_EXPERTS, start, 0)
        lax.fori_loop(0, N_EXPERTS, wait, 0)

    n_rows = lax.fori_loop(0, N_EXPERTS, lambda e, s: s + size_ref[i * N_EXPERTS + e], 0)
    xb = x_ref[...].astype(BF16)
    lrow = lrow_ref[0]
    for c in range(stage.shape[0] // ONEHOT_ROWS):
        @pl.when(c * ONEHOT_ROWS < n_rows)
        def _():
            stage[c * ONEHOT_ROWS:(c + 1) * ONEHOT_ROWS, :] = jnp.dot(
                _onehot_rows(c, lrow, None, tm), xb, preferred_element_type=F32).astype(BF16)

    start_all, wait_all = _segments(
        gstart_ref, size_ref, i,
        lambda g, loff, n: pltpu.make_async_copy(stage.at[pl.ds(loff, n), :], xs_ref.at[pl.ds(g, n), :], sem))
    start_all()
    wait_all()


def _dispatch(x, lrow, gstart, size, pend, pcnt, *, rows, tm, bm):
    t, d = x.shape
    return pl.pallas_call(
        functools.partial(_dispatch_kernel, tm=tm, bm=bm),
        grid_spec=pltpu.PrefetchScalarGridSpec(
            num_scalar_prefetch=4,
            grid=(t // tm,),
            in_specs=[pl.BlockSpec((tm, d), lambda i, *_: (i, 0)),
                      pl.BlockSpec((1, TOP_K, tm), lambda i, *_: (i, 0, 0))],
            out_specs=pl.BlockSpec(memory_space=pl.ANY),
            scratch_shapes=[pltpu.VMEM((_tile_rows(tm), d), BF16),
                            pltpu.SemaphoreType.DMA(()), pltpu.SemaphoreType.DMA(())],
        ),
        out_shape=jax.ShapeDtypeStruct((rows, d), BF16),
        compiler_params=_cp("arbitrary"),
    )(gstart, size, pend, pcnt, x, lrow)


def _expert_kernel(blk_e_ref, nused_ref, xs_ref, wgu_ref, wdn_ref, ys_ref, wgu_b, wdn_b):
    j = pl.program_id(0)
    used = j < nused_ref[0]
    changed = jnp.logical_or(j == 0, blk_e_ref[j] != blk_e_ref[jnp.maximum(j - 1, 0)])

    @pl.when(jnp.logical_and(used, changed))
    def _():
        wgu_b[...] = wgu_ref[...].astype(BF16)
        wdn_b[...] = wdn_ref[...].astype(BF16)

    @pl.when(used)
    def _():
        f = wdn_b.shape[0]
        gu = jnp.dot(xs_ref[...], wgu_b[...], preferred_element_type=F32)
        hidden = (_silu(gu[:, :f]) * gu[:, f:]).astype(BF16)
        ys_ref[...] = jnp.dot(hidden, wdn_b[...], preferred_element_type=F32).astype(BF16)


def _experts(xs, blk_e, nused, w_gu, w_dn, layer, *, bm):
    rows, d = xs.shape
    f2 = w_gu.shape[3]
    f = w_dn.shape[2]
    blk = lambda j, be, nu: (jnp.minimum(j, nu[0] - 1), 0)
    return pl.pallas_call(
        _expert_kernel,
        grid_spec=pltpu.PrefetchScalarGridSpec(
            num_scalar_prefetch=2,
            grid=(rows // bm,),
            in_specs=[pl.BlockSpec((bm, d), blk),
                      pl.BlockSpec((None, None, d, f2), lambda j, be, nu: (layer, be[j], 0, 0)),
                      pl.BlockSpec((None, None, f, d), lambda j, be, nu: (layer, be[j], 0, 0))],
            out_specs=pl.BlockSpec((bm, d), blk),
            scratch_shapes=[pltpu.VMEM((d, f2), BF16), pltpu.VMEM((f, d), BF16)],
        ),
        out_shape=jax.ShapeDtypeStruct((rows, d), BF16),
        compiler_params=_cp("arbitrary"),
    )(blk_e, nused, xs, w_gu, w_dn)


def _combine_kernel(gstart_ref, size_ref, x_ref, lrow_ref, w_ref, sgu_ref, sdn_ref, lw_ref, lb_ref, ys_ref,
                    o_ref, ybuf, wt, hid, acc, sem, *, tm, alpha):
    i, half = pl.program_id(0), pl.program_id(1)
    dh = ybuf.shape[1]
    col0 = pl.multiple_of(half * dh, dh)

    @pl.when(jnp.logical_and(i == 0, half == 0))
    def _():
        ybuf[...] = jnp.zeros_like(ybuf)

    start_all, wait_all = _segments(
        gstart_ref, size_ref, i,
        lambda g, loff, n: pltpu.make_async_copy(ys_ref.at[pl.ds(g, n), pl.ds(col0, dh)],
                                                 ybuf.at[pl.ds(loff, n), :], sem))
    n_rows = start_all()
    n_chunks = ybuf.shape[0] // ONEHOT_ROWS

    @pl.when(half == 0)
    def _():
        lrow, w = lrow_ref[0], w_ref[...]
        for c in range(n_chunks):
            @pl.when(c * ONEHOT_ROWS < n_rows)
            def _():
                wt[c * ONEHOT_ROWS:(c + 1) * ONEHOT_ROWS, :] = _onehot_rows(c, lrow, w, tm)
        f = hid.shape[1]
        gu = jnp.dot(x_ref[...].astype(BF16), sgu_ref[...], preferred_element_type=F32)
        hid[...] = (_silu(gu[:, :f]) * gu[:, f:]).astype(BF16)

    acc[...] = jnp.dot(hid[...], sdn_ref[...], preferred_element_type=F32)
    wait_all()
    for c in range(n_chunks):
        @pl.when(c * ONEHOT_ROWS < n_rows)
        def _():
            rows = slice(c * ONEHOT_ROWS, (c + 1) * ONEHOT_ROWS)
            acc[...] += lax.dot_general(wt[rows, :], ybuf[rows, :], (((0,), (0,)), ((), ())),
                                        preferred_element_type=F32)

    @pl.when(half == 0)
    def _():
        o_ref[:, 0:dh] = acc[...]

    @pl.when(half == 1)
    def _():
        o_ref[:, dh:2 * dh] = acc[...]
        o_ref[...] = _layer_norm(alpha * x_ref[...] + o_ref[...], lw_ref[...], lb_ref[...])


def _combine(x, ys, lrow, wts, gstart, size, s_gu, s_dn, ln_w, ln_b, *, alpha, tm):
    t, d = x.shape
    dh = d // 2
    f = s_dn.shape[0]
    const = lambda shape: pl.BlockSpec(shape, lambda i, h, *_: (0, 0))
    return pl.pallas_call(
        functools.partial(_combine_kernel, tm=tm, alpha=alpha),
        grid_spec=pltpu.PrefetchScalarGridSpec(
            num_scalar_prefetch=2,
            grid=(t // tm, 2),
            in_specs=[pl.BlockSpec((tm, d), lambda i, h, *_: (i, 0)),
                      pl.BlockSpec((1, TOP_K, tm), lambda i, h, *_: (i, 0, 0)),
                      pl.BlockSpec((TOP_K, tm), lambda i, h, *_: (0, i)),
                      const(s_gu.shape),
                      pl.BlockSpec((f, dh), lambda i, h, *_: (0, h)),
                      const((1, d)), const((1, d)),
                      pl.BlockSpec(memory_space=pl.ANY)],
            out_specs=pl.BlockSpec((tm, d), lambda i, h, *_: (i, 0)),
            scratch_shapes=[pltpu.VMEM((_tile_rows(tm), dh), BF16), pltpu.VMEM((_tile_rows(tm), tm), BF16),
                            pltpu.VMEM((tm, f), BF16), pltpu.VMEM((tm, dh), F32), pltpu.SemaphoreType.DMA(())],
        ),
        out_shape=jax.ShapeDtypeStruct((t, d), F32),
        compiler_params=_cp("arbitrary", "arbitrary"),
    )(gstart, size, x, lrow, wts, s_gu, s_dn, ln_w, ln_b, ys)


def _mixer_sublayer(x, w_in, gate_b, pool_w, pool_scale, conv_w, mlstm_norm_w, ret_norm_w, w_branch, w_out,
                    ln_w, ln_b, *, batch, seq, alpha):
    t, d = x.shape
    width = d // N_BRANCH
    heads = width // HEAD_DIM
    gate_off = 8 * width
    ret_off = gate_off + 2 * heads
    w_head = w_in[:, :gate_off].astype(BF16)
    w_if = jnp.pad(w_in[:, gate_off:ret_off], ((0, 0), (0, LANES - 2 * heads))).astype(BF16)
    w_tail = _shifted_cast(w_in, col0=gate_off, shift=2 * heads, ncols=4 * width + N_BRANCH * d,
                           tr=512, tn=512)
    gate_bias = jnp.pad(gate_b, (0, LANES - 2 * heads)).reshape(1, LANES)

    proj_a = _matmul(x, w_head, tm=1024, tn=512)
    proj_b = _matmul(x, w_tail, tm=1024, tn=512, ncols=4 * width)
    gates = _matmul(x, w_if, tm=1024, tn=LANES)
    y_pool, y_conv = _pool_conv(proj_a, pool_w.astype(BF16), pool_scale.reshape(1, width), conv_w,
                                seq=seq, width=width, tb=512)
    y_mlstm = _mlstm(proj_a, gates, gate_bias, mlstm_norm_w.reshape(1, width),
                     batch=batch, seq=seq, width=width, col0=4)
    y_ret = _retention(proj_b, ret_norm_w.reshape(1, width), batch=batch, seq=seq, width=width, col0=0)
    merged = _merge(x, (y_pool, y_conv, y_mlstm, y_ret), w_tail, 4 * width, w_branch.astype(BF16),
                    tm=512, tn=512)
    return _proj_ln(merged, w_out.astype(BF16), x, ln_w, ln_b, alpha=alpha, tm=256)


def _xattn_sublayer(x, mem2d, wq, wk, wv, wo, ln_w, ln_b, *, seq, mem_len, alpha):
    w_kv = jnp.concatenate([wk, wv], axis=1).astype(BF16)
    kv = _matmul(mem2d, w_kv, tm=min(mem2d.shape[0], 1024), tn=512)
    return _xattn(x, kv, wq.astype(BF16), wo.astype(BF16), ln_w, ln_b,
                  alpha=alpha, seq=seq, mem_len=mem_len, tm=256)


def _moe_sublayer(x, router_w, router_b, w_gu, w_dn, layer, s_gu, s_dn, ln_w, ln_b, *, alpha):
    t, d = x.shape
    e_n, bm = N_EXPERTS, MOE_BM
    tm = ROUTE_TM
    n_blocks = -(-(t * TOP_K + (t // tm) * e_n * (SEG_ALIGN - 1)) // bm) + e_n
    wts, lrow, segtab, blk, seg = _route(x, router_w.T, router_b.reshape(e_n, 1), tm=tm, bm=bm, n_blocks=n_blocks)
    gstart, size = segtab[:, 0, :, 0].reshape(-1), segtab[:, 1, :, 0].reshape(-1)
    pend, pcnt = seg[0, :, 0], seg[1, :, 0]
    blk_e = blk[0, :n_blocks]
    nused = pend[e_n - 1] // bm

    xs = _dispatch(x, lrow, gstart, size, pend, pcnt, rows=n_blocks * bm, tm=tm, bm=bm)
    ys = _experts(xs, blk_e, nused.reshape(1), w_gu, w_dn, layer, bm=bm)
    return _combine(x, ys, lrow, wts, gstart, size, s_gu.astype(BF16), s_dn.astype(BF16), ln_w, ln_b,
                    alpha=alpha, tm=tm)


def kernel(x, mem, w_in, mlstm_gate_b, pool_w, pool_scale, conv_w, mlstm_norm_w, ret_norm_w, w_branch,
           w_mix_out, xa_wq, xa_wk, xa_wv, xa_wo, router_w, router_b, moe_w_gu, moe_w_dn, shared_w_gu,
           shared_w_dn, ln_w, ln_b):
    batch, seq, d = x.shape
    depth = w_in.shape[0]
    mem_len = mem.shape[1]
    alpha = (2 * depth) ** 0.25
    h = x.reshape(batch * seq, d)
    mem2d = mem.reshape(batch * mem_len, d)
    for l in range(depth):
        lw = ln_w[l].reshape(3, 1, d)
        lb = ln_b[l].reshape(3, 1, d)
        h = _mixer_sublayer(h, w_in[l], mlstm_gate_b[l], pool_w[l], pool_scale[l], conv_w[l], mlstm_norm_w[l],
                            ret_norm_w[l], w_branch[l], w_mix_out[l], lw[0], lb[0],
                            batch=batch, seq=seq, alpha=alpha)
        h = _xattn_sublayer(h, mem2d, xa_wq[l], xa_wk[l], xa_wv[l], xa_wo[l], lw[1], lb[1],
                            seq=seq, mem_len=mem_len, alpha=alpha)
        h = _moe_sublayer(h, router_w[l], router_b[l], moe_w_gu, moe_w_dn, l, shared_w_gu[l], shared_w_dn[l],
                          lw[2], lb[2], alpha=alpha)
    return h.reshape(batch, seq, d)
```

```python
import functools

import numpy as np
import jax
import jax.numpy as jnp
from jax import lax
from jax.experimental import pallas as pl
from jax.experimental.pallas import tpu as pltpu

F32 = jnp.float32
BF16 = jnp.bfloat16
I32 = jnp.int32

N_BRANCH = 4
HEAD_DIM = 128
POOL_WINDOWS = (2, 4, 8, 16)
CONV_WIDTH = 3
ROPE_BASE = 10000.0
XATTN_HEADS = 4
N_EXPERTS = 64
TOP_K = 8
N_GROUPS = 8
TOPK_GROUPS = 4
ROUTE_SCALE = 2.5
LN_EPS = 1e-5

LANES = 128
V7X_VMEM_BYTES = 64 * 1024 * 1024
VMEM_LIMIT = 56 * 1024 * 1024

SCAN_CHUNK = 128
SCAN_BLOCK = 512
HALO = 16
MOE_BM = 256
ROUTE_TM = 512
SEG_ALIGN = 16
ONEHOT_ROWS = 512


def _cp(*sem):
    return pltpu.CompilerParams(dimension_semantics=sem, vmem_limit_bytes=VMEM_LIMIT)


def _sigmoid(x):
    return 1.0 / (1.0 + jnp.exp(-x))


def _silu(x):
    return x * _sigmoid(x)


def _log_sigmoid(x):
    return jnp.minimum(x, 0.0) - jnp.log(1.0 + jnp.exp(-jnp.abs(x)))


def _layer_norm(z, w, b):
    mu = jnp.mean(z, axis=-1, keepdims=True)
    d = z - mu
    var = jnp.mean(d * d, axis=-1, keepdims=True)
    return d * lax.rsqrt(var + LN_EPS) * w + b


def _head_norm(h, w):
    mu = jnp.mean(h, axis=-1, keepdims=True)
    d = h - mu
    var = jnp.mean(d * d, axis=-1, keepdims=True)
    return d * lax.rsqrt(var + LN_EPS) * w


def _mm_kernel(x_ref, w_ref, o_ref, xb_ref):
    @pl.when(pl.program_id(1) == 0)
    def _():
        xb_ref[...] = x_ref[...].astype(BF16)

    o_ref[...] = jnp.dot(xb_ref[...], w_ref[...], preferred_element_type=F32)


def _shift_cast_kernel(a_ref, b_ref, o_ref, *, shift):
    tn = o_ref.shape[1]
    both = jnp.concatenate([a_ref[...], b_ref[...]], axis=1)
    o_ref[...] = both[:, shift:shift + tn].astype(BF16)


def _shifted_cast(w, *, col0, shift, ncols, tr, tn):
    rows = w.shape[0]
    return pl.pallas_call(
        functools.partial(_shift_cast_kernel, shift=shift),
        grid=(rows // tr, ncols // tn),
        in_specs=[pl.BlockSpec((tr, tn), lambda i, j: (i, col0 // tn + j)),
                  pl.BlockSpec((tr, LANES), lambda i, j: (i, (col0 + (j + 1) * tn) // LANES))],
        out_specs=pl.BlockSpec((tr, tn), lambda i, j: (i, j)),
        out_shape=jax.ShapeDtypeStruct((rows, ncols), BF16),
        compiler_params=_cp("parallel", "parallel"),
    )(w, w)


def _matmul(x, w, *, tm, tn, ncols=None):
    t, k = x.shape
    n = w.shape[1] if ncols is None else ncols
    return pl.pallas_call(
        _mm_kernel,
        grid=(t // tm, n // tn),
        in_specs=[pl.BlockSpec((tm, k), lambda i, j: (i, 0)),
                  pl.BlockSpec((k, tn), lambda i, j: (0, j))],
        out_specs=pl.BlockSpec((tm, tn), lambda i, j: (i, j)),
        out_shape=jax.ShapeDtypeStruct((t, n), F32),
        scratch_shapes=[pltpu.VMEM((tm, k), BF16)],
        compiler_params=_cp("parallel", "arbitrary"),
    )(x, w)


def _poolconv_kernel(u_ref, uh_ref, h_ref, hh_ref, b_ref, c_ref, ch_ref, pw_ref, ps_ref, cw_ref,
                     yp_ref, yc_ref, ubuf, zbuf, *, tb, n_tblk):
    first = (pl.program_id(0) % n_tblk) == 0
    ubuf[0:HALO, :] = jnp.where(first, 0.0, uh_ref[...])
    ubuf[HALO:HALO + tb, :] = u_ref[...]
    zbuf[0:HALO, :] = jnp.where(first, 0.0, ch_ref[...] * hh_ref[...])
    zbuf[HALO:HALO + tb, :] = c_ref[...] * h_ref[...]

    t_pos = (pl.program_id(0) % n_tblk) * tb + lax.broadcasted_iota(I32, (tb, LANES), 0)
    gw = u_ref.shape[1] // len(POOL_WINDOWS)
    for grp, win in enumerate(POOL_WINDOWS):
        lanes = slice(grp * gw, (grp + 1) * gw)
        cur = ubuf[HALO:HALO + tb, lanes]
        acc = cur
        for lag in range(1, win):
            acc = acc + ubuf[HALO - lag:HALO - lag + tb, lanes]
        count = jnp.minimum(t_pos + 1, win).astype(F32)
        mixed = acc / count - cur
        y = jnp.dot(mixed.astype(BF16), pw_ref[grp], preferred_element_type=F32)
        yp_ref[:, lanes] = y * ps_ref[:, lanes]

    conv = cw_ref[0:1, :] * zbuf[HALO:HALO + tb, :]
    for lag in range(1, CONV_WIDTH):
        conv = conv + cw_ref[lag:lag + 1, :] * zbuf[HALO - lag:HALO - lag + tb, :]
    yc_ref[...] = b_ref[...] * conv


def _pool_conv(proj, pool_w, pool_scale, conv_w, *, seq, width, tb):
    t = proj.shape[0]
    n_tblk = seq // tb
    ratio = tb // HALO

    def cur(col):
        return pl.BlockSpec((tb, width), lambda g: (g, col))

    def halo(col):
        return pl.BlockSpec((HALO, width), lambda g: (jnp.maximum(g * ratio - 1, 0), col))

    full = lambda shape: pl.BlockSpec(shape, lambda g: (0,) * len(shape))
    return pl.pallas_call(
        functools.partial(_poolconv_kernel, tb=tb, n_tblk=n_tblk),
        grid=(t // tb,),
        in_specs=[cur(0), halo(0), cur(1), halo(1), cur(2), cur(3), halo(3),
                  full(pool_w.shape), full(pool_scale.shape), full(conv_w.shape)],
        out_specs=[pl.BlockSpec((tb, width), lambda g: (g, 0))] * 2,
        out_shape=[jax.ShapeDtypeStruct((t, width), F32)] * 2,
        scratch_shapes=[pltpu.VMEM((HALO + tb, width), F32)] * 2,
        compiler_params=_cp("parallel"),
    )(proj, proj, proj, proj, proj, proj, proj, pool_w, pool_scale, conv_w)


def _cum_rows(x, op, fill):
    n = x.shape[0]
    row = lax.broadcasted_iota(I32, x.shape, 0)
    shift = 1
    while shift < n:
        x = op(x, jnp.where(row >= shift, pltpu.roll(x, shift, axis=0), fill))
        shift *= 2
    return x


def _mlstm_kernel(q_ref, k_ref, v_ref, o_ref, g_ref, gb_ref, nw_ref, y_ref, c_scr, n_scr, m_scr,
                  *, chunk, heads):
    @pl.when(pl.program_id(1) == 0)
    def _():
        c_scr[...] = jnp.zeros_like(c_scr)
        n_scr[...] = jnp.zeros_like(n_scr)
        m_scr[...] = jnp.zeros_like(m_scr)

    tb = q_ref.shape[0]
    hd = HEAD_DIM
    scale = hd ** -0.5
    tri = (lax.broadcasted_iota(I32, (chunk, chunk), 0) >= lax.broadcasted_iota(I32, (chunk, chunk), 1))
    for c in range(tb // chunk):
        rows = slice(c * chunk, (c + 1) * chunk)
        gates = g_ref[rows, :] + gb_ref[...]
        lf = _log_sigmoid(pltpu.roll(gates, LANES - heads, axis=1))
        cumf = _cum_rows(lf, jnp.add, 0.0)
        a = gates - cumf
        m_prev = m_scr[...]
        mu = jnp.maximum(_cum_rows(a, jnp.maximum, -jnp.inf), m_prev)
        mu_last = mu[chunk - 1:chunk, :]
        a_t = a.T
        for h in range(heads):
            cols = slice(h * hd, (h + 1) * hd)
            q = q_ref[rows, cols]
            k = k_ref[rows, cols] * scale
            v = v_ref[rows, cols]
            qb, kb, vb = q.astype(BF16), k.astype(BF16), v.astype(BF16)
            mu_col = mu[:, h:h + 1]
            a_col = a[:, h:h + 1]
            m_prev_h = m_prev[:, h:h + 1]
            mu_last_h = mu_last[:, h:h + 1]
            dmat = jnp.exp(jnp.where(tri, a_t[h:h + 1, :] - mu_col, -jnp.inf))
            s = lax.dot_general(qb, kb, (((1,), (1,)), ((), ())), preferred_element_type=F32)
            p = dmat * s
            inter = jnp.exp(m_prev_h - mu_col)
            c_h = c_scr[h]
            n_h = n_scr[h]
            num = inter * jnp.dot(qb, c_h.astype(BF16), preferred_element_type=F32) \
                + jnp.dot(p.astype(BF16), vb, preferred_element_type=F32)
            den = inter * jnp.sum(q * n_h, axis=-1, keepdims=True) + jnp.sum(p, axis=-1, keepdims=True)
            floor = jnp.exp(-(cumf[:, h:h + 1] + mu_col))
            h_out = num / jnp.maximum(jnp.abs(den), floor)

            wg = jnp.exp(a_col - mu_last_h)
            decay = jnp.exp(m_prev_h - mu_last_h)
            kw = k * wg
            c_scr[h] = decay * c_h + lax.dot_general(kw.astype(BF16), vb, (((0,), (0,)), ((), ())),
                                                     preferred_element_type=F32)
            n_scr[h] = decay * n_h + jnp.sum(kw, axis=0, keepdims=True)

            gated = _sigmoid(o_ref[rows, cols]) * h_out
            y_ref[rows, cols] = _head_norm(gated, nw_ref[:, cols])
        m_scr[...] = cumf[chunk - 1:chunk, :] + mu_last


def _mlstm(proj, gates, gate_b, norm_w, *, batch, seq, width, col0):
    t = proj.shape[0]
    heads = width // HEAD_DIM
    n_tblk = seq // SCAN_BLOCK
    blk = lambda col: pl.BlockSpec((SCAN_BLOCK, width), lambda b, i: (b * n_tblk + i, col))
    return pl.pallas_call(
        functools.partial(_mlstm_kernel, chunk=SCAN_CHUNK, heads=heads),
        grid=(batch, n_tblk),
        in_specs=[blk(col0), blk(col0 + 1), blk(col0 + 2), blk(col0 + 3),
                  pl.BlockSpec((SCAN_BLOCK, LANES), lambda b, i: (b * n_tblk + i, 0)),
                  pl.BlockSpec((1, LANES), lambda b, i: (0, 0)),
                  pl.BlockSpec((1, width), lambda b, i: (0, 0))],
        out_specs=pl.BlockSpec((SCAN_BLOCK, width), lambda b, i: (b * n_tblk + i, 0)),
        out_shape=jax.ShapeDtypeStruct((t, width), F32),
        scratch_shapes=[pltpu.VMEM((heads, HEAD_DIM, HEAD_DIM), F32),
                        pltpu.VMEM((heads, 1, HEAD_DIM), F32),
                        pltpu.VMEM((1, LANES), F32)],
        compiler_params=_cp("parallel", "arbitrary"),
    )(proj, proj, proj, proj, gates, gate_b, norm_w)


def _ret_kernel(q_ref, k_ref, v_ref, g_ref, cos_ref, sin_ref, intra_ref, cross_ref, zeta_ref, nw_ref,
                y_ref, r_scr, *, chunk, heads, chunk_decay):
    @pl.when(pl.program_id(1) == 0)
    def _():
        r_scr[...] = jnp.zeros_like(r_scr)

    tb = q_ref.shape[0]
    hd = HEAD_DIM
    scale = hd ** -0.5
    for c in range(tb // chunk):
        rows = slice(c * chunk, (c + 1) * chunk)
        cos = cos_ref[rows, :]
        sin = sin_ref[rows, :]
        for h in range(heads):
            cols = slice(h * hd, (h + 1) * hd)
            q = q_ref[rows, cols]
            k = k_ref[rows, cols]
            q = q * cos + pltpu.roll(q, hd // 2, axis=1) * sin
            k = (k * cos + pltpu.roll(k, hd // 2, axis=1) * sin) * scale
            vb = v_ref[rows, cols].astype(BF16)
            qb = q.astype(BF16)
            s = lax.dot_general(qb, k.astype(BF16), (((1,), (1,)), ((), ())), preferred_element_type=F32)
            inner = jnp.dot((s * intra_ref[h]).astype(BF16), vb, preferred_element_type=F32)
            r_h = r_scr[h]
            crs = jnp.dot(qb, r_h.astype(BF16), preferred_element_type=F32) * cross_ref[h]
            r_scr[h] = chunk_decay[h] * r_h + lax.dot_general(
                (k * zeta_ref[h]).astype(BF16), vb, (((0,), (0,)), ((), ())), preferred_element_type=F32)
            y_ref[rows, cols] = _silu(g_ref[rows, cols]) * _head_norm(inner + crs, nw_ref[:, cols])


def _retention(proj, norm_w, *, batch, seq, width, col0):
    t = proj.shape[0]
    heads = width // HEAD_DIM
    chunk = SCAN_CHUNK
    n_tblk = seq // SCAN_BLOCK
    half = HEAD_DIM // 2
    inv = ROPE_BASE ** (-jnp.arange(half, dtype=F32) / half)
    ang = jnp.arange(seq, dtype=F32)[:, None] * inv[None, :]
    cos_t = jnp.concatenate([jnp.cos(ang), jnp.cos(ang)], axis=-1)
    sin_t = jnp.concatenate([-jnp.sin(ang), jnp.sin(ang)], axis=-1)
    log_g = jnp.log(1.0 - 2.0 ** (-5.0 - jnp.arange(heads, dtype=F32)))
    tt = jnp.arange(chunk, dtype=F32)
    lag = tt[:, None] - tt[None, :]
    intra = jnp.where(lag >= 0, jnp.exp(jnp.maximum(lag, 0.0)[None] * log_g[:, None, None]), 0.0)
    cross = jnp.broadcast_to(jnp.exp((tt + 1.0)[None, :] * log_g[:, None])[:, :, None], (heads, chunk, HEAD_DIM))
    zeta = jnp.broadcast_to(jnp.exp((chunk - 1.0 - tt)[None, :] * log_g[:, None])[:, :, None],
                            (heads, chunk, HEAD_DIM))
    chunk_decay = tuple(float((1.0 - 2.0 ** (-5.0 - h)) ** chunk) for h in range(heads))

    blk = lambda col: pl.BlockSpec((SCAN_BLOCK, width), lambda b, i: (b * n_tblk + i, col))
    pos = pl.BlockSpec((SCAN_BLOCK, HEAD_DIM), lambda b, i: (i, 0))
    full3 = lambda a: pl.BlockSpec(a.shape, lambda b, i: (0, 0, 0))
    return pl.pallas_call(
        functools.partial(_ret_kernel, chunk=chunk, heads=heads, chunk_decay=chunk_decay),
        grid=(batch, n_tblk),
        in_specs=[blk(col0), blk(col0 + 1), blk(col0 + 2), blk(col0 + 3), pos, pos,
                  full3(intra), full3(cross), full3(zeta),
                  pl.BlockSpec((1, width), lambda b, i: (0, 0))],
        out_specs=pl.BlockSpec((SCAN_BLOCK, width), lambda b, i: (b * n_tblk + i, 0)),
        out_shape=jax.ShapeDtypeStruct((t, width), F32),
        scratch_shapes=[pltpu.VMEM((heads, HEAD_DIM, HEAD_DIM), F32)],
        compiler_params=_cp("parallel", "arbitrary"),
    )(proj, proj, proj, proj, cos_t, sin_t, intra, cross, zeta, norm_w)


def _merge_kernel(x_ref, *refs):
    y_refs, wg_refs = refs[:N_BRANCH], refs[N_BRANCH:2 * N_BRANCH]
    wb_ref, o_ref, xb_ref = refs[2 * N_BRANCH:]

    @pl.when(pl.program_id(1) == 0)
    def _():
        xb_ref[...] = x_ref[...].astype(BF16)

    xb = xb_ref[...]
    acc = None
    for n in range(N_BRANCH):
        gate = _sigmoid(jnp.dot(xb, wg_refs[n][...], preferred_element_type=F32))
        term = gate * jnp.dot(y_refs[n][...].astype(BF16), wb_ref[n], preferred_element_type=F32)
        acc = term if acc is None else acc + term
    o_ref[...] = acc


def _merge(x, branches, w_gate, gate_col0, w_branch, *, tm, tn):
    t, d = x.shape
    width = branches[0].shape[1]
    ybs = pl.BlockSpec((tm, width), lambda i, j: (i, 0))

    def gate_spec(n):
        return pl.BlockSpec((d, tn), lambda i, j: (0, (gate_col0 + n * d) // tn + j))

    return pl.pallas_call(
        _merge_kernel,
        grid=(t // tm, d // tn),
        in_specs=[pl.BlockSpec((tm, d), lambda i, j: (i, 0))] + [ybs] * N_BRANCH
                 + [gate_spec(n) for n in range(N_BRANCH)]
                 + [pl.BlockSpec((N_BRANCH, width, tn), lambda i, j: (0, 0, j))],
        out_specs=pl.BlockSpec((tm, tn), lambda i, j: (i, j)),
        out_shape=jax.ShapeDtypeStruct((t, d), F32),
        scratch_shapes=[pltpu.VMEM((tm, d), BF16)],
        compiler_params=_cp("parallel", "arbitrary"),
    )(x, *branches, *([w_gate] * N_BRANCH), w_branch)


def _proj_ln_kernel(a_ref, w_ref, r_ref, lw_ref, lb_ref, o_ref, *, alpha):
    y = jnp.dot(a_ref[...].astype(BF16), w_ref[...], preferred_element_type=F32)
    o_ref[...] = _layer_norm(alpha * r_ref[...] + y, lw_ref[...], lb_ref[...])


def _proj_ln(a, w, resid, ln_w, ln_b, *, alpha, tm):
    t, k = a.shape
    d = w.shape[1]
    row = lambda n: pl.BlockSpec((tm, n), lambda i: (i, 0))
    const = lambda shape: pl.BlockSpec(shape, lambda i: (0, 0))
    return pl.pallas_call(
        functools.partial(_proj_ln_kernel, alpha=alpha),
        grid=(t // tm,),
        in_specs=[row(k), const((k, d)), row(d), const((1, d)), const((1, d))],
        out_specs=row(d),
        out_shape=jax.ShapeDtypeStruct((t, d), F32),
        compiler_params=_cp("parallel"),
    )(a, w, resid, ln_w, ln_b)


def _xattn_kernel(x_ref, wq_ref, kv_ref, wo_ref, lw_ref, lb_ref, o_ref, *, alpha, heads):
    x = x_ref[...]
    hd = HEAD_DIM
    inner = heads * hd
    q = jnp.dot(x.astype(BF16), wq_ref[...], preferred_element_type=F32)
    outs = []
    for h in range(heads):
        qh = q[:, h * hd:(h + 1) * hd].astype(BF16)
        kh = kv_ref[:, h * hd:(h + 1) * hd].astype(BF16)
        vh = kv_ref[:, inner + h * hd:inner + (h + 1) * hd].astype(BF16)
        s = lax.dot_general(qh, kh, (((1,), (1,)), ((), ())), preferred_element_type=F32) * hd ** -0.5
        s = s - jnp.max(s, axis=-1, keepdims=True)
        e = jnp.exp(s)
        p = e / jnp.sum(e, axis=-1, keepdims=True)
        outs.append(jnp.dot(p.astype(BF16), vh, preferred_element_type=F32).astype(BF16))
    o = jnp.concatenate(outs, axis=-1)
    y = jnp.dot(o, wo_ref[...], preferred_element_type=F32)
    o_ref[...] = _layer_norm(alpha * x + y, lw_ref[...], lb_ref[...])


def _xattn(x, kv, wq, wo, ln_w, ln_b, *, alpha, seq, mem_len, tm):
    t, d = x.shape
    inner = wq.shape[1]
    n_tblk = seq // tm
    const = lambda shape: pl.BlockSpec(shape, lambda i: (0, 0))
    return pl.pallas_call(
        functools.partial(_xattn_kernel, alpha=alpha, heads=XATTN_HEADS),
        grid=(t // tm,),
        in_specs=[pl.BlockSpec((tm, d), lambda i: (i, 0)), const((d, inner)),
                  pl.BlockSpec((mem_len, 2 * inner), lambda i: (i // n_tblk, 0)),
                  const((inner, d)), const((1, d)), const((1, d))],
        out_specs=pl.BlockSpec((tm, d), lambda i: (i, 0)),
        out_shape=jax.ShapeDtypeStruct((t, d), F32),
        compiler_params=_cp("parallel"),
    )(x, wq, kv, wo, ln_w, ln_b)


def _round_up_pow2(x, m):
    shift = m.bit_length() - 1
    return jnp.left_shift(jnp.right_shift(x + (m - 1), shift), shift)


def _route_kernel(x_ref, wr_ref, rb_ref, w_ref, lrow_ref, segtab_ref, blk_ref, seg_ref, size_all, tot, *, tm, bm):
    step = pl.program_id(0)

    @pl.when(step == 0)
    def _():
        tot[...] = jnp.zeros_like(tot)

    e_n, g_n = N_EXPERTS, N_GROUPS
    per = e_n // g_n
    logits = lax.dot_general(wr_ref[...], x_ref[...], (((1,), (1,)), ((), ())),
                             precision=lax.Precision.HIGHEST, preferred_element_type=F32)
    scores = _sigmoid(logits)
    biased = scores + rb_ref[...]
    b3 = biased.reshape(g_n, per, tm)
    member = lax.broadcasted_iota(I32, (g_n, per, tm), 1)
    top1 = jnp.max(b3, axis=1, keepdims=True)
    first = jnp.min(jnp.where(b3 == top1, member, per), axis=1, keepdims=True)
    top2 = jnp.max(jnp.where(member == first, -jnp.inf, b3), axis=1, keepdims=True)
    gs = top1 + top2
    gid = lax.broadcasted_iota(I32, (g_n, 1, tm), 0)
    rank = jnp.zeros((g_n, 1, tm), I32)
    for other in range(g_n):
        o = gs[other:other + 1]
        ahead = jnp.logical_or(o > gs, jnp.logical_and(o == gs, other < gid))
        rank = rank + jnp.where(ahead, 1, 0)
    cur = jnp.where(rank < TOPK_GROUPS, b3, -jnp.inf).reshape(e_n, tm)

    eid = lax.broadcasted_iota(I32, (e_n, tm), 0)
    picks, vals = [], []
    sel = jnp.zeros((e_n, tm), F32)
    for k in range(TOP_K):
        mx = jnp.max(cur, axis=0, keepdims=True)
        ik = jnp.min(jnp.where(cur == mx, eid, e_n), axis=0, keepdims=True)
        hit = eid == ik
        vals.append(jnp.sum(jnp.where(hit, scores, 0.0), axis=0, keepdims=True))
        cur = jnp.where(hit, -jnp.inf, cur)
        sel = jnp.where(hit, 1.0, sel)
        picks.append(ik)
    total = vals[0]
    for v in vals[1:]:
        total = total + v

    tri = jnp.where(lax.broadcasted_iota(I32, (tm, tm), 0) <= lax.broadcasted_iota(I32, (tm, tm), 1), 1.0, 0.0)
    incl = jnp.dot(sel.astype(BF16), tri.astype(BF16), preferred_element_type=F32)
    size = _round_up_pow2(jnp.broadcast_to(incl[:, tm - 1:tm], (e_n, LANES)).astype(I32), SEG_ALIGN)
    loff = _cum_rows(size, jnp.add, 0) - size
    base = loff[:, 0:1].astype(F32) + incl - 1.0
    for k in range(TOP_K):
        w_ref[k:k + 1, :] = vals[k] / total * ROUTE_SCALE
        lrow_ref[0, k:k + 1, :] = jnp.sum(jnp.where(eid == picks[k], base, 0.0),
                                          axis=0, keepdims=True).astype(I32)
    size_all[step] = size
    tot[...] = tot[...] + size

    @pl.when(step == pl.num_programs(0) - 1)
    def _():
        pcnt = _round_up_pow2(tot[...], bm)
        pend = _cum_rows(pcnt, jnp.add, 0)

        def tile_seg(i, run):
            segtab_ref[i, 0] = run
            segtab_ref[i, 1] = size_all[i]
            return run + size_all[i]

        lax.fori_loop(0, pl.num_programs(0), tile_seg, pend - pcnt)

        nb = blk_ref.shape[1]
        row0 = lax.broadcasted_iota(I32, (e_n, nb), 1) * bm
        total_rows = pend[e_n - 1:e_n, 0:1]
        owner = jnp.sum(jnp.where(pend[:, 0:1] <= row0, 1, 0), axis=0, keepdims=True)
        last_owner = jnp.sum(jnp.where(pend[:, 0:1] < total_rows, 1, 0), axis=0, keepdims=True)
        blk_ref[...] = jnp.where(row0[0:1, :] < total_rows, jnp.minimum(owner, e_n - 1), last_owner)
        seg_ref[0] = pend
        seg_ref[1] = pcnt


def _route(x, router_w_t, router_b, *, tm, bm, n_blocks):
    t, d = x.shape
    e_n = N_EXPERTS
    n_t = t // tm
    assert bm & (bm - 1) == 0
    nb_pad = -(-n_blocks // LANES) * LANES
    return pl.pallas_call(
        functools.partial(_route_kernel, tm=tm, bm=bm),
        grid=(n_t,),
        in_specs=[pl.BlockSpec((tm, d), lambda i: (i, 0)),
                  pl.BlockSpec((e_n, d), lambda i: (0, 0)),
                  pl.BlockSpec((e_n, 1), lambda i: (0, 0))],
        out_specs=[pl.BlockSpec((TOP_K, tm), lambda i: (0, i)),
                   pl.BlockSpec((1, TOP_K, tm), lambda i: (i, 0, 0)),
                   pl.BlockSpec((n_t, 2, e_n, LANES), lambda i: (0, 0, 0, 0)),
                   pl.BlockSpec((1, nb_pad), lambda i: (0, 0)),
                   pl.BlockSpec((2, e_n, LANES), lambda i: (0, 0, 0))],
        out_shape=[jax.ShapeDtypeStruct((TOP_K, t), F32), jax.ShapeDtypeStruct((n_t, TOP_K, tm), I32),
                   jax.ShapeDtypeStruct((n_t, 2, e_n, LANES), I32),
                   jax.ShapeDtypeStruct((1, nb_pad), I32), jax.ShapeDtypeStruct((2, e_n, LANES), I32)],
        scratch_shapes=[pltpu.VMEM((n_t, e_n, LANES), I32), pltpu.VMEM((e_n, LANES), I32)],
        compiler_params=_cp("arbitrary"),
    )(x, router_w_t, router_b)


def _tile_rows(tm):
    worst = TOP_K * tm + N_EXPERTS * (SEG_ALIGN - 1)
    return -(-worst // ONEHOT_ROWS) * ONEHOT_ROWS


def _onehot_rows(chunk, lrow, values, tm):
    rid = chunk * ONEHOT_ROWS + lax.broadcasted_iota(I32, (ONEHOT_ROWS, tm), 0)
    acc = jnp.zeros((ONEHOT_ROWS, tm), F32)
    for k in range(TOP_K):
        acc = jnp.where(rid == lrow[k:k + 1, :], 1.0 if values is None else values[k:k + 1, :], acc)
    return acc.astype(BF16)


def _segments(gstart_ref, size_ref, tile, make_copy, act, keep=None):
    def body(e, loff):
        n = size_ref[tile * N_EXPERTS + e]
        wanted = n > 0 if keep is None else jnp.logical_and(n > 0, keep(loff + n))

        @pl.when(wanted)
        def _():
            act(make_copy(pl.multiple_of(gstart_ref[tile * N_EXPERTS + e], SEG_ALIGN),
                          pl.multiple_of(loff, SEG_ALIGN), pl.multiple_of(n, SEG_ALIGN)))
        return loff + n
    return lax.fori_loop(0, N_EXPERTS, body, 0)


def _start(cp):
    cp.start()


def _wait(cp):
    cp.wait()


def _dispatch_kernel(gstart_ref, size_ref, pend_ref, pcnt_ref, x_ref, lrow_ref, xs_ref, stage, sem_a, sem_b, zsem,
                     *, tm, bm):
    i = pl.program_id(0)
    last = pl.num_programs(0) - 1
    n_chunks = stage.shape[0] // ONEHOT_ROWS
    split_chunk = n_chunks // 2
    split = split_chunk * ONEHOT_ROWS
    assert tm <= ONEHOT_ROWS
    zbuf = stage.at[0:bm, :]

    def seg_walk(tile, act, phase_b):
        sem = sem_b if phase_b else sem_a
        return _segments(
            gstart_ref, size_ref, tile,
            lambda g, loff, n: pltpu.make_async_copy(stage.at[pl.ds(loff, n), :], xs_ref.at[pl.ds(g, n), :], sem),
            act, (lambda end: end > split) if phase_b else (lambda end: end <= split))

    def pad_copy(e):
        return pltpu.make_async_copy(zbuf, xs_ref.at[pl.ds(pl.multiple_of(pend_ref[e] - bm, bm), bm), :], zsem)

    @pl.when(i == 0)
    def _():
        stage[0:bm, :] = jnp.zeros((bm, stage.shape[1]), stage.dtype)

        def start(e, c):
            @pl.when(pcnt_ref[e] > 0)
            def _():
                pad_copy(e).start()
            return c

        def wait(e, c):
            @pl.when(pcnt_ref[e] > 0)
            def _():
                pad_copy(e).wait()
            return c

        lax.fori_loop(0, N_EXPERTS, start, 0)
        lax.fori_loop(0, N_EXPERTS, wait, 0)

    xb = x_ref[...].astype(BF16)
    lrow = lrow_ref[0]

    def chunks(lo, hi):
        for c in range(lo, hi):
            stage[c * ONEHOT_ROWS:(c + 1) * ONEHOT_ROWS, :] = jnp.dot(
                _onehot_rows(c, lrow, None, tm), xb, preferred_element_type=F32).astype(BF16)

    prev = jnp.maximum(i - 1, 0)

    @pl.when(i > 0)
    def _():
        seg_walk(prev, _wait, False)
    chunks(0, split_chunk - 1)

    @pl.when(i > 0)
    def _():
        seg_walk(prev, _wait, True)
    chunks(split_chunk - 1, split_chunk)
    seg_walk(i, _start, False)
    chunks(split_chunk, n_chunks)
    seg_walk(i, _start, True)

    @pl.when(i == last)
    def _():
        seg_walk(i, _wait, False)
        seg_walk(i, _wait, True)


def _dispatch(x, lrow, gstart, size, pend, pcnt, *, rows, tm, bm):
    t, d = x.shape
    return pl.pallas_call(
        functools.partial(_dispatch_kernel, tm=tm, bm=bm),
        grid_spec=pltpu.PrefetchScalarGridSpec(
            num_scalar_prefetch=4,
            grid=(t // tm,),
            in_specs=[pl.BlockSpec((tm, d), lambda i, *_: (i, 0)),
                      pl.BlockSpec((1, TOP_K, tm), lambda i, *_: (i, 0, 0))],
            out_specs=pl.BlockSpec(memory_space=pl.ANY),
            scratch_shapes=[pltpu.VMEM((_tile_rows(tm), d), BF16)] + [pltpu.SemaphoreType.DMA(())] * 3,
        ),
        out_shape=jax.ShapeDtypeStruct((rows, d), BF16),
        compiler_params=_cp("arbitrary"),
    )(gstart, size, pend, pcnt, x, lrow)


def _expert_kernel(blk_e_ref, nused_ref, xs_ref, wgu_ref, wdn_ref, ys_ref, wgu_b, wdn_b):
    j = pl.program_id(0)
    used = j < nused_ref[0]
    changed = jnp.logical_or(j == 0, blk_e_ref[j] != blk_e_ref[jnp.maximum(j - 1, 0)])

    @pl.when(jnp.logical_and(used, changed))
    def _():
        wgu_b[...] = wgu_ref[...].astype(BF16)
        wdn_b[...] = wdn_ref[...].astype(BF16)

    @pl.when(used)
    def _():
        f = wdn_b.shape[0]
        gu = jnp.dot(xs_ref[...], wgu_b[...], preferred_element_type=F32)
        hidden = (_silu(gu[:, :f]) * gu[:, f:]).astype(BF16)
        ys_ref[...] = jnp.dot(hidden, wdn_b[...], preferred_element_type=F32).astype(BF16)


def _experts(xs, blk_e, nused, w_gu, w_dn, layer, *, bm):
    rows, d = xs.shape
    f2 = w_gu.shape[3]
    f = w_dn.shape[2]
    blk = lambda j, be, nu: (jnp.minimum(j, nu[0] - 1), 0)
    return pl.pallas_call(
        _expert_kernel,
        grid_spec=pltpu.PrefetchScalarGridSpec(
            num_scalar_prefetch=2,
            grid=(rows // bm,),
            in_specs=[pl.BlockSpec((bm, d), blk),
                      pl.BlockSpec((None, None, d, f2), lambda j, be, nu: (layer, be[j], 0, 0)),
                      pl.BlockSpec((None, None, f, d), lambda j, be, nu: (layer, be[j], 0, 0))],
            out_specs=pl.BlockSpec((bm, d), blk),
            scratch_shapes=[pltpu.VMEM((d, f2), BF16), pltpu.VMEM((f, d), BF16)],
        ),
        out_shape=jax.ShapeDtypeStruct((rows, d), BF16),
        compiler_params=_cp("arbitrary"),
    )(blk_e, nused, xs, w_gu, w_dn)


def _combine_kernel(gstart_ref, size_ref, x_ref, lrow_ref, w_ref, sdn_ref, lw_ref, lb_ref, sgu_hbm, ys_ref,
                    o_ref, ybuf, wt, hid, sgu, sems, wsem, *, tm, alpha):
    i, half = pl.program_id(0), pl.program_id(1)
    last_tile = pl.num_programs(0) - 1
    dh = ybuf.shape[2]
    n_chunks = ybuf.shape[1] // ONEHOT_ROWS

    def seg_walk(tile, hf, act):
        return _segments(
            gstart_ref, size_ref, tile,
            lambda g, loff, n: pltpu.make_async_copy(ys_ref.at[pl.ds(g, n), hf * dh:(hf + 1) * dh],
                                                     ybuf.at[hf, pl.ds(loff, n), :], sems.at[hf]),
            act)

    @pl.when(jnp.logical_and(i == 0, half == 0))
    def _():
        ybuf[...] = jnp.zeros_like(ybuf)
        weights = pltpu.make_async_copy(sgu_hbm, sgu, wsem)
        weights.start()
        weights.wait()
        seg_walk(0, 0, _start)

    @pl.when(half == 0)
    def _():
        seg_walk(i, 1, _start)
        f = hid.shape[1]
        gu = jnp.dot(x_ref[...].astype(BF16), sgu[...], preferred_element_type=F32)
        hidden = (_silu(gu[:, :f]) * gu[:, f:]).astype(BF16)
        hid[...] = hidden
        seg_walk(i, 0, _wait)
        lrow, w = lrow_ref[0], w_ref[...]
        acc = jnp.dot(hidden, sdn_ref[...], preferred_element_type=F32)
        for c in range(n_chunks):
            rows = slice(c * ONEHOT_ROWS, (c + 1) * ONEHOT_ROWS)
            w_rows = _onehot_rows(c, lrow, w, tm)
            wt[rows, :] = w_rows
            acc = acc + lax.dot_general(w_rows, ybuf[0, rows, :], (((0,), (0,)), ((), ())),
                                        preferred_element_type=F32)
        o_ref[:, 0:dh] = acc

    @pl.when(half == 1)
    def _():
        @pl.when(i < last_tile)
        def _():
            seg_walk(i + 1, 0, _start)
        seg_walk(i, 1, _wait)
        routed = lax.dot_general(wt[...], ybuf[1], (((0,), (0,)), ((), ())), preferred_element_type=F32)
        o_ref[:, dh:2 * dh] = routed + jnp.dot(hid[...], sdn_ref[...], preferred_element_type=F32)
        o_ref[...] = _layer_norm(alpha * x_ref[...] + o_ref[...], lw_ref[...], lb_ref[...])


def _combine(x, ys, lrow, wts, gstart, size, s_gu, s_dn, ln_w, ln_b, *, alpha, tm):
    t, d = x.shape
    dh = d // 2
    f = s_dn.shape[0]
    const = lambda shape: pl.BlockSpec(shape, lambda i, h, *_: (0, 0))
    return pl.pallas_call(
        functools.partial(_combine_kernel, tm=tm, alpha=alpha),
        grid_spec=pltpu.PrefetchScalarGridSpec(
            num_scalar_prefetch=2,
            grid=(t // tm, 2),
            in_specs=[pl.BlockSpec((tm, d), lambda i, h, *_: (i, 0)),
                      pl.BlockSpec((1, TOP_K, tm), lambda i, h, *_: (i, 0, 0)),
                      pl.BlockSpec((TOP_K, tm), lambda i, h, *_: (0, i)),
                      pl.BlockSpec((f, dh), lambda i, h, *_: (0, h)),
                      const((1, d)), const((1, d)),
                      pl.BlockSpec(memory_space=pl.ANY), pl.BlockSpec(memory_space=pl.ANY)],
            out_specs=pl.BlockSpec((tm, d), lambda i, h, *_: (i, 0)),
            scratch_shapes=[pltpu.VMEM((2, _tile_rows(tm), dh), BF16), pltpu.VMEM((_tile_rows(tm), tm), BF16),
                            pltpu.VMEM((tm, f), BF16), pltpu.VMEM(s_gu.shape, BF16),
                            pltpu.SemaphoreType.DMA((2,)), pltpu.SemaphoreType.DMA(())],
        ),
        out_shape=jax.ShapeDtypeStruct((t, d), F32),
        compiler_params=_cp("arbitrary", "arbitrary"),
    )(gstart, size, x, lrow, wts, s_dn, ln_w, ln_b, s_gu, ys)


def _mixer_sublayer(x, w_in, gate_b, pool_w, pool_scale, conv_w, mlstm_norm_w, ret_norm_w, w_branch, w_out,
                    ln_w, ln_b, *, batch, seq, alpha):
    t, d = x.shape
    width = d // N_BRANCH
    heads = width // HEAD_DIM
    gate_off = 8 * width
    ret_off = gate_off + 2 * heads
    w_head = w_in[:, :gate_off].astype(BF16)
    w_if = jnp.pad(w_in[:, gate_off:ret_off], ((0, 0), (0, LANES - 2 * heads))).astype(BF16)
    w_tail = _shifted_cast(w_in, col0=gate_off, shift=2 * heads, ncols=4 * width + N_BRANCH * d,
                           tr=512, tn=512)
    gate_bias = jnp.pad(gate_b, (0, LANES - 2 * heads)).reshape(1, LANES)

    proj_a = _matmul(x, w_head, tm=1024, tn=512)
    proj_b = _matmul(x, w_tail, tm=1024, tn=512, ncols=4 * width)
    gates = _matmul(x, w_if, tm=1024, tn=LANES)
    y_pool, y_conv = _pool_conv(proj_a, pool_w.astype(BF16), pool_scale.reshape(1, width), conv_w,
                                seq=seq, width=width, tb=512)
    y_mlstm = _mlstm(proj_a, gates, gate_bias, mlstm_norm_w.reshape(1, width),
                     batch=batch, seq=seq, width=width, col0=4)
    y_ret = _retention(proj_b, ret_norm_w.reshape(1, width), batch=batch, seq=seq, width=width, col0=0)
    merged = _merge(x, (y_pool, y_conv, y_mlstm, y_ret), w_tail, 4 * width, w_branch.astype(BF16),
                    tm=512, tn=512)
    return _proj_ln(merged, w_out.astype(BF16), x, ln_w, ln_b, alpha=alpha, tm=256)


def _xattn_sublayer(x, mem2d, wq, wk, wv, wo, ln_w, ln_b, *, seq, mem_len, alpha):
    w_kv = jnp.concatenate([wk, wv], axis=1).astype(BF16)
    kv = _matmul(mem2d, w_kv, tm=min(mem2d.shape[0], 1024), tn=512)
    return _xattn(x, kv, wq.astype(BF16), wo.astype(BF16), ln_w, ln_b,
                  alpha=alpha, seq=seq, mem_len=mem_len, tm=256)


def _moe_sublayer(x, router_w, router_b, w_gu, w_dn, layer, s_gu, s_dn, ln_w, ln_b, *, alpha):
    t, d = x.shape
    e_n, bm = N_EXPERTS, MOE_BM
    tm = ROUTE_TM
    n_blocks = -(-(t * TOP_K + (t // tm) * e_n * (SEG_ALIGN - 1)) // bm) + e_n
    wts, lrow, segtab, blk, seg = _route(x, router_w.T, router_b.reshape(e_n, 1), tm=tm, bm=bm, n_blocks=n_blocks)
    gstart, size = segtab[:, 0, :, 0].reshape(-1), segtab[:, 1, :, 0].reshape(-1)
    pend, pcnt = seg[0, :, 0], seg[1, :, 0]
    blk_e = blk[0, :n_blocks]
    nused = pend[e_n - 1] // bm

    xs = _dispatch(x, lrow, gstart, size, pend, pcnt, rows=n_blocks * bm, tm=tm, bm=bm)
    ys = _experts(xs, blk_e, nused.reshape(1), w_gu, w_dn, layer, bm=bm)
    return _combine(x, ys, lrow, wts, gstart, size, s_gu.astype(BF16), s_dn.astype(BF16), ln_w, ln_b,
                    alpha=alpha, tm=tm)


def kernel(x, mem, w_in, mlstm_gate_b, pool_w, pool_scale, conv_w, mlstm_norm_w, ret_norm_w, w_branch,
           w_mix_out, xa_wq, xa_wk, xa_wv, xa_wo, router_w, router_b, moe_w_gu, moe_w_dn, shared_w_gu,
           shared_w_dn, ln_w, ln_b):
    batch, seq, d = x.shape
    depth = w_in.shape[0]
    mem_len = mem.shape[1]
    alpha = (2 * depth) ** 0.25
    h = x.reshape(batch * seq, d)
    mem2d = mem.reshape(batch * mem_len, d)
    for l in range(depth):
        lw = ln_w[l].reshape(3, 1, d)
        lb = ln_b[l].reshape(3, 1, d)
        h = _mixer_sublayer(h, w_in[l], mlstm_gate_b[l], pool_w[l], pool_scale[l], conv_w[l], mlstm_norm_w[l],
                            ret_norm_w[l], w_branch[l], w_mix_out[l], lw[0], lb[0],
                            batch=batch, seq=seq, alpha=alpha)
        h = _xattn_sublayer(h, mem2d, xa_wq[l], xa_wk[l], xa_wv[l], xa_wo[l], lw[1], lb[1],
                            seq=seq, mem_len=mem_len, alpha=alpha)
        h = _moe_sublayer(h, router_w[l], router_b[l], moe_w_gu, moe_w_dn, l, shared_w_gu[l], shared_w_dn[l],
                          lw[2], lb[2], alpha=alpha)
    return h.reshape(batch, seq, d)
```

```python
import functools

import numpy as np
import jax
import jax.numpy as jnp
from jax import lax
from jax.experimental import pallas as pl
from jax.experimental.pallas import tpu as pltpu

F32 = jnp.float32
BF16 = jnp.bfloat16
I32 = jnp.int32

N_BRANCH = 4
HEAD_DIM = 128
POOL_WINDOWS = (2, 4, 8, 16)
CONV_WIDTH = 3
ROPE_BASE = 10000.0
XATTN_HEADS = 4
N_EXPERTS = 64
TOP_K = 8
N_GROUPS = 8
TOPK_GROUPS = 4
ROUTE_SCALE = 2.5
LN_EPS = 1e-5

LANES = 128
V7X_VMEM_BYTES = 64 * 1024 * 1024
VMEM_LIMIT = 56 * 1024 * 1024

SCAN_CHUNK = 256
SCAN_BLOCK = 512
HALO = 16
MOE_BM = 512
ROUTE_TM = 512
SEG_ALIGN = 16
ONEHOT_ROWS = 512


def _cp(*sem):
    return pltpu.CompilerParams(dimension_semantics=sem, vmem_limit_bytes=VMEM_LIMIT)


def _sigmoid(x):
    return 1.0 / (1.0 + jnp.exp(-x))


def _silu(x):
    return x * _sigmoid(x)


def _log_sigmoid(x):
    return jnp.minimum(x, 0.0) - jnp.log(1.0 + jnp.exp(-jnp.abs(x)))


def _layer_norm(z, w, b):
    mu = jnp.mean(z, axis=-1, keepdims=True)
    d = z - mu
    var = jnp.mean(d * d, axis=-1, keepdims=True)
    return d * lax.rsqrt(var + LN_EPS) * w + b


def _head_norm(h, w):
    mu = jnp.mean(h, axis=-1, keepdims=True)
    d = h - mu
    var = jnp.mean(d * d, axis=-1, keepdims=True)
    return d * lax.rsqrt(var + LN_EPS) * w


def _mm_kernel(x_ref, w_ref, o_ref, xb_ref):
    @pl.when(pl.program_id(1) == 0)
    def _():
        xb_ref[...] = x_ref[...].astype(BF16)

    o_ref[...] = jnp.dot(xb_ref[...], w_ref[...], preferred_element_type=F32)


def _shift_cast_kernel(a_ref, b_ref, o_ref, *, shift):
    tn = o_ref.shape[1]
    both = jnp.concatenate([a_ref[...], b_ref[...]], axis=1)
    o_ref[...] = both[:, shift:shift + tn].astype(BF16)


def _shifted_cast(w, *, col0, shift, ncols, tr, tn):
    rows = w.shape[0]
    return pl.pallas_call(
        functools.partial(_shift_cast_kernel, shift=shift),
        grid=(rows // tr, ncols // tn),
        in_specs=[pl.BlockSpec((tr, tn), lambda i, j: (i, col0 // tn + j)),
                  pl.BlockSpec((tr, LANES), lambda i, j: (i, (col0 + (j + 1) * tn) // LANES))],
        out_specs=pl.BlockSpec((tr, tn), lambda i, j: (i, j)),
        out_shape=jax.ShapeDtypeStruct((rows, ncols), BF16),
        compiler_params=_cp("parallel", "parallel"),
    )(w, w)


def _matmul(x, w, *, tm, tn, ncols=None):
    t, k = x.shape
    n = w.shape[1] if ncols is None else ncols
    return pl.pallas_call(
        _mm_kernel,
        grid=(t // tm, n // tn),
        in_specs=[pl.BlockSpec((tm, k), lambda i, j: (i, 0)),
                  pl.BlockSpec((k, tn), lambda i, j: (0, j))],
        out_specs=pl.BlockSpec((tm, tn), lambda i, j: (i, j)),
        out_shape=jax.ShapeDtypeStruct((t, n), F32),
        scratch_shapes=[pltpu.VMEM((tm, k), BF16)],
        compiler_params=_cp("parallel", "arbitrary"),
    )(x, w)


def _poolconv_kernel(u_ref, uh_ref, h_ref, hh_ref, b_ref, c_ref, ch_ref, pw_ref, ps_ref, cw_ref,
                     yp_ref, yc_ref, ubuf, zbuf, *, tb, n_tblk):
    first = (pl.program_id(0) % n_tblk) == 0
    ubuf[0:HALO, :] = jnp.where(first, 0.0, uh_ref[...])
    ubuf[HALO:HALO + tb, :] = u_ref[...]
    zbuf[0:HALO, :] = jnp.where(first, 0.0, ch_ref[...] * hh_ref[...])
    zbuf[HALO:HALO + tb, :] = c_ref[...] * h_ref[...]

    t_pos = (pl.program_id(0) % n_tblk) * tb + lax.broadcasted_iota(I32, (tb, LANES), 0)
    gw = u_ref.shape[1] // len(POOL_WINDOWS)
    for grp, win in enumerate(POOL_WINDOWS):
        lanes = slice(grp * gw, (grp + 1) * gw)
        cur = ubuf[HALO:HALO + tb, lanes]
        acc = cur
        for lag in range(1, win):
            acc = acc + ubuf[HALO - lag:HALO - lag + tb, lanes]
        count = jnp.minimum(t_pos + 1, win).astype(F32)
        mixed = acc / count - cur
        y = jnp.dot(mixed.astype(BF16), pw_ref[grp], preferred_element_type=F32)
        yp_ref[:, lanes] = y * ps_ref[:, lanes]

    conv = cw_ref[0:1, :] * zbuf[HALO:HALO + tb, :]
    for lag in range(1, CONV_WIDTH):
        conv = conv + cw_ref[lag:lag + 1, :] * zbuf[HALO - lag:HALO - lag + tb, :]
    yc_ref[...] = b_ref[...] * conv


def _pool_conv(proj, pool_w, pool_scale, conv_w, *, seq, width, tb):
    t = proj.shape[0]
    n_tblk = seq // tb
    ratio = tb // HALO

    def cur(col):
        return pl.BlockSpec((tb, width), lambda g: (g, col))

    def halo(col):
        return pl.BlockSpec((HALO, width), lambda g: (jnp.maximum(g * ratio - 1, 0), col))

    full = lambda shape: pl.BlockSpec(shape, lambda g: (0,) * len(shape))
    return pl.pallas_call(
        functools.partial(_poolconv_kernel, tb=tb, n_tblk=n_tblk),
        grid=(t // tb,),
        in_specs=[cur(0), halo(0), cur(1), halo(1), cur(2), cur(3), halo(3),
                  full(pool_w.shape), full(pool_scale.shape), full(conv_w.shape)],
        out_specs=[pl.BlockSpec((tb, width), lambda g: (g, 0))] * 2,
        out_shape=[jax.ShapeDtypeStruct((t, width), F32)] * 2,
        scratch_shapes=[pltpu.VMEM((HALO + tb, width), F32)] * 2,
        compiler_params=_cp("parallel"),
    )(proj, proj, proj, proj, proj, proj, proj, pool_w, pool_scale, conv_w)


def _cum_rows(x, op, fill):
    n = x.shape[0]
    row = lax.broadcasted_iota(I32, x.shape, 0)
    shift = 1
    while shift < n:
        x = op(x, jnp.where(row >= shift, pltpu.roll(x, shift, axis=0), fill))
        shift *= 2
    return x


def _mlstm_kernel(q_ref, k_ref, v_ref, o_ref, g_ref, gb_ref, nw_ref, y_ref, c_scr, n_scr, m_scr,
                  *, chunk, heads):
    @pl.when(pl.program_id(1) == 0)
    def _():
        c_scr[...] = jnp.zeros_like(c_scr)
        n_scr[...] = jnp.zeros_like(n_scr)
        m_scr[...] = jnp.zeros_like(m_scr)

    tb = q_ref.shape[0]
    hd = HEAD_DIM
    scale = hd ** -0.5
    tri = (lax.broadcasted_iota(I32, (chunk, chunk), 0) >= lax.broadcasted_iota(I32, (chunk, chunk), 1))
    for c in range(tb // chunk):
        rows = slice(c * chunk, (c + 1) * chunk)
        gates = g_ref[rows, :] + gb_ref[...]
        lf = _log_sigmoid(pltpu.roll(gates, LANES - heads, axis=1))
        cumf = _cum_rows(lf, jnp.add, 0.0)
        a = gates - cumf
        m_prev = m_scr[...]
        mu = jnp.maximum(_cum_rows(a, jnp.maximum, -jnp.inf), m_prev)
        mu_last = mu[chunk - 1:chunk, :]
        a_t = a.T
        for h in range(heads):
            cols = slice(h * hd, (h + 1) * hd)
            q = q_ref[rows, cols]
            k = k_ref[rows, cols] * scale
            v = v_ref[rows, cols]
            qb, kb, vb = q.astype(BF16), k.astype(BF16), v.astype(BF16)
            mu_col = mu[:, h:h + 1]
            a_col = a[:, h:h + 1]
            m_prev_h = m_prev[:, h:h + 1]
            mu_last_h = mu_last[:, h:h + 1]
            dmat = jnp.exp(jnp.where(tri, a_t[h:h + 1, :] - mu_col, -jnp.inf))
            s = lax.dot_general(qb, kb, (((1,), (1,)), ((), ())), preferred_element_type=F32)
            p = dmat * s
            inter = jnp.exp(m_prev_h - mu_col)
            c_h = c_scr[h]
            n_h = n_scr[h]
            num = inter * jnp.dot(qb, c_h.astype(BF16), preferred_element_type=F32) \
                + jnp.dot(p.astype(BF16), vb, preferred_element_type=F32)
            den = inter * jnp.sum(q * n_h, axis=-1, keepdims=True) + jnp.sum(p, axis=-1, keepdims=True)
            floor = jnp.exp(-(cumf[:, h:h + 1] + mu_col))
            h_out = num / jnp.maximum(jnp.abs(den), floor)

            wg = jnp.exp(a_col - mu_last_h)
            decay = jnp.exp(m_prev_h - mu_last_h)
            kw = k * wg
            c_scr[h] = decay * c_h + lax.dot_general(kw.astype(BF16), vb, (((0,), (0,)), ((), ())),
                                                     preferred_element_type=F32)
            n_scr[h] = decay * n_h + jnp.sum(kw, axis=0, keepdims=True)

            gated = _sigmoid(o_ref[rows, cols]) * h_out
            y_ref[rows, cols] = _head_norm(gated, nw_ref[:, cols])
        m_scr[...] = cumf[chunk - 1:chunk, :] + mu_last


def _mlstm(proj, gates, gate_b, norm_w, *, batch, seq, width, col0):
    t = proj.shape[0]
    heads = width // HEAD_DIM
    n_tblk = seq // SCAN_BLOCK
    blk = lambda col: pl.BlockSpec((SCAN_BLOCK, width), lambda b, i: (b * n_tblk + i, col))
    return pl.pallas_call(
        functools.partial(_mlstm_kernel, chunk=SCAN_CHUNK, heads=heads),
        grid=(batch, n_tblk),
        in_specs=[blk(col0), blk(col0 + 1), blk(col0 + 2), blk(col0 + 3),
                  pl.BlockSpec((SCAN_BLOCK, LANES), lambda b, i: (b * n_tblk + i, 0)),
                  pl.BlockSpec((1, LANES), lambda b, i: (0, 0)),
                  pl.BlockSpec((1, width), lambda b, i: (0, 0))],
        out_specs=pl.BlockSpec((SCAN_BLOCK, width), lambda b, i: (b * n_tblk + i, 0)),
        out_shape=jax.ShapeDtypeStruct((t, width), F32),
        scratch_shapes=[pltpu.VMEM((heads, HEAD_DIM, HEAD_DIM), F32),
                        pltpu.VMEM((heads, 1, HEAD_DIM), F32),
                        pltpu.VMEM((1, LANES), F32)],
        compiler_params=_cp("parallel", "arbitrary"),
    )(proj, proj, proj, proj, gates, gate_b, norm_w)


def _ret_kernel(q_ref, k_ref, v_ref, g_ref, cos_ref, sin_ref, intra_ref, cross_ref, zeta_ref, nw_ref,
                y_ref, r_scr, *, chunk, heads, chunk_decay):
    @pl.when(pl.program_id(1) == 0)
    def _():
        r_scr[...] = jnp.zeros_like(r_scr)

    tb = q_ref.shape[0]
    hd = HEAD_DIM
    scale = hd ** -0.5
    for c in range(tb // chunk):
        rows = slice(c * chunk, (c + 1) * chunk)
        cos = cos_ref[rows, :]
        sin = sin_ref[rows, :]
        for h in range(heads):
            cols = slice(h * hd, (h + 1) * hd)
            q = q_ref[rows, cols]
            k = k_ref[rows, cols]
            q = q * cos + pltpu.roll(q, hd // 2, axis=1) * sin
            k = (k * cos + pltpu.roll(k, hd // 2, axis=1) * sin) * scale
            vb = v_ref[rows, cols].astype(BF16)
            qb = q.astype(BF16)
            s = lax.dot_general(qb, k.astype(BF16), (((1,), (1,)), ((), ())), preferred_element_type=F32)
            inner = jnp.dot((s * intra_ref[h]).astype(BF16), vb, preferred_element_type=F32)
            r_h = r_scr[h]
            crs = jnp.dot(qb, r_h.astype(BF16), preferred_element_type=F32) * cross_ref[h]
            r_scr[h] = chunk_decay[h] * r_h + lax.dot_general(
                (k * zeta_ref[h]).astype(BF16), vb, (((0,), (0,)), ((), ())), preferred_element_type=F32)
            y_ref[rows, cols] = _silu(g_ref[rows, cols]) * _head_norm(inner + crs, nw_ref[:, cols])


def _retention(proj, norm_w, *, batch, seq, width, col0):
    t = proj.shape[0]
    heads = width // HEAD_DIM
    chunk = SCAN_CHUNK
    n_tblk = seq // SCAN_BLOCK
    half = HEAD_DIM // 2
    inv = np.float32(ROPE_BASE) ** (-np.arange(half, dtype=np.float32) / np.float32(half))
    ang = (np.arange(seq, dtype=np.float32)[:, None] * inv[None, :]).astype(np.float64)
    cos_t = jnp.asarray(np.concatenate([np.cos(ang), np.cos(ang)], axis=-1), F32)
    sin_t = jnp.asarray(np.concatenate([-np.sin(ang), np.sin(ang)], axis=-1), F32)
    log_g = jnp.log(1.0 - 2.0 ** (-5.0 - jnp.arange(heads, dtype=F32)))
    tt = jnp.arange(chunk, dtype=F32)
    lag = tt[:, None] - tt[None, :]
    intra = jnp.where(lag >= 0, jnp.exp(jnp.maximum(lag, 0.0)[None] * log_g[:, None, None]), 0.0)
    cross = jnp.broadcast_to(jnp.exp((tt + 1.0)[None, :] * log_g[:, None])[:, :, None], (heads, chunk, HEAD_DIM))
    zeta = jnp.broadcast_to(jnp.exp((chunk - 1.0 - tt)[None, :] * log_g[:, None])[:, :, None],
                            (heads, chunk, HEAD_DIM))
    chunk_decay = tuple(float((1.0 - 2.0 ** (-5.0 - h)) ** chunk) for h in range(heads))

    blk = lambda col: pl.BlockSpec((SCAN_BLOCK, width), lambda b, i: (b * n_tblk + i, col))
    pos = pl.BlockSpec((SCAN_BLOCK, HEAD_DIM), lambda b, i: (i, 0))
    full3 = lambda a: pl.BlockSpec(a.shape, lambda b, i: (0, 0, 0))
    return pl.pallas_call(
        functools.partial(_ret_kernel, chunk=chunk, heads=heads, chunk_decay=chunk_decay),
        grid=(batch, n_tblk),
        in_specs=[blk(col0), blk(col0 + 1), blk(col0 + 2), blk(col0 + 3), pos, pos,
                  full3(intra), full3(cross), full3(zeta),
                  pl.BlockSpec((1, width), lambda b, i: (0, 0))],
        out_specs=pl.BlockSpec((SCAN_BLOCK, width), lambda b, i: (b * n_tblk + i, 0)),
        out_shape=jax.ShapeDtypeStruct((t, width), F32),
        scratch_shapes=[pltpu.VMEM((heads, HEAD_DIM, HEAD_DIM), F32)],
        compiler_params=_cp("parallel", "arbitrary"),
    )(proj, proj, proj, proj, cos_t, sin_t, intra, cross, zeta, norm_w)


def _merge_kernel(x_ref, *refs):
    y_refs, wg_refs = refs[:N_BRANCH], refs[N_BRANCH:2 * N_BRANCH]
    wb_ref, o_ref, xb_ref = refs[2 * N_BRANCH:]

    @pl.when(pl.program_id(1) == 0)
    def _():
        xb_ref[...] = x_ref[...].astype(BF16)

    xb = xb_ref[...]
    acc = None
    for n in range(N_BRANCH):
        gate = _sigmoid(jnp.dot(xb, wg_refs[n][...], preferred_element_type=F32))
        term = gate * jnp.dot(y_refs[n][...].astype(BF16), wb_ref[n], preferred_element_type=F32)
        acc = term if acc is None else acc + term
    o_ref[...] = acc


def _merge(x, branches, w_gate, gate_col0, w_branch, *, tm, tn):
    t, d = x.shape
    width = branches[0].shape[1]
    ybs = pl.BlockSpec((tm, width), lambda i, j: (i, 0))

    def gate_spec(n):
        return pl.BlockSpec((d, tn), lambda i, j: (0, (gate_col0 + n * d) // tn + j))

    return pl.pallas_call(
        _merge_kernel,
        grid=(t // tm, d // tn),
        in_specs=[pl.BlockSpec((tm, d), lambda i, j: (i, 0))] + [ybs] * N_BRANCH
                 + [gate_spec(n) for n in range(N_BRANCH)]
                 + [pl.BlockSpec((N_BRANCH, width, tn), lambda i, j: (0, 0, j))],
        out_specs=pl.BlockSpec((tm, tn), lambda i, j: (i, j)),
        out_shape=jax.ShapeDtypeStruct((t, d), F32),
        scratch_shapes=[pltpu.VMEM((tm, d), BF16)],
        compiler_params=_cp("parallel", "arbitrary"),
    )(x, *branches, *([w_gate] * N_BRANCH), w_branch)


def _proj_ln_kernel(a_ref, w_ref, r_ref, lw_ref, lb_ref, o_ref, *, alpha):
    y = jnp.dot(a_ref[...].astype(BF16), w_ref[...], preferred_element_type=F32)
    o_ref[...] = _layer_norm(alpha * r_ref[...] + y, lw_ref[...], lb_ref[...])


def _proj_ln(a, w, resid, ln_w, ln_b, *, alpha, tm):
    t, k = a.shape
    d = w.shape[1]
    row = lambda n: pl.BlockSpec((tm, n), lambda i: (i, 0))
    const = lambda shape: pl.BlockSpec(shape, lambda i: (0, 0))
    return pl.pallas_call(
        functools.partial(_proj_ln_kernel, alpha=alpha),
        grid=(t // tm,),
        in_specs=[row(k), const((k, d)), row(d), const((1, d)), const((1, d))],
        out_specs=row(d),
        out_shape=jax.ShapeDtypeStruct((t, d), F32),
        compiler_params=_cp("parallel"),
    )(a, w, resid, ln_w, ln_b)


def _xattn_kernel(x_ref, wq_ref, kv_ref, wo_ref, lw_ref, lb_ref, o_ref, *, alpha, heads):
    x = x_ref[...]
    hd = HEAD_DIM
    inner = heads * hd
    q = jnp.dot(x.astype(BF16), wq_ref[...], preferred_element_type=F32)
    outs = []
    for h in range(heads):
        qh = q[:, h * hd:(h + 1) * hd].astype(BF16)
        kh = kv_ref[:, h * hd:(h + 1) * hd].astype(BF16)
        vh = kv_ref[:, inner + h * hd:inner + (h + 1) * hd].astype(BF16)
        s = lax.dot_general(qh, kh, (((1,), (1,)), ((), ())), preferred_element_type=F32) * hd ** -0.5
        s = s - jnp.max(s, axis=-1, keepdims=True)
        e = jnp.exp(s)
        p = e / jnp.sum(e, axis=-1, keepdims=True)
        outs.append(jnp.dot(p.astype(BF16), vh, preferred_element_type=F32).astype(BF16))
    o = jnp.concatenate(outs, axis=-1)
    y = jnp.dot(o, wo_ref[...], preferred_element_type=F32)
    o_ref[...] = _layer_norm(alpha * x + y, lw_ref[...], lb_ref[...])


def _xattn(x, kv, wq, wo, ln_w, ln_b, *, alpha, seq, mem_len, tm):
    t, d = x.shape
    inner = wq.shape[1]
    n_tblk = seq // tm
    const = lambda shape: pl.BlockSpec(shape, lambda i: (0, 0))
    return pl.pallas_call(
        functools.partial(_xattn_kernel, alpha=alpha, heads=XATTN_HEADS),
        grid=(t // tm,),
        in_specs=[pl.BlockSpec((tm, d), lambda i: (i, 0)), const((d, inner)),
                  pl.BlockSpec((mem_len, 2 * inner), lambda i: (i // n_tblk, 0)),
                  const((inner, d)), const((1, d)), const((1, d))],
        out_specs=pl.BlockSpec((tm, d), lambda i: (i, 0)),
        out_shape=jax.ShapeDtypeStruct((t, d), F32),
        compiler_params=_cp("parallel"),
    )(x, wq, kv, wo, ln_w, ln_b)


def _round_up_pow2(x, m):
    shift = m.bit_length() - 1
    return jnp.left_shift(jnp.right_shift(x + (m - 1), shift), shift)


def _route_kernel(x_ref, wr_ref, rb_ref, w_ref, lrow_ref, segtab_ref, blk_ref, seg_ref, size_all, tot, *, tm, bm):
    step = pl.program_id(0)

    @pl.when(step == 0)
    def _():
        tot[...] = jnp.zeros_like(tot)

    e_n, g_n = N_EXPERTS, N_GROUPS
    per = e_n // g_n
    logits = lax.dot_general(wr_ref[...], x_ref[...], (((1,), (1,)), ((), ())),
                             precision=lax.Precision.HIGHEST, preferred_element_type=F32)
    scores = _sigmoid(logits)
    biased = scores + rb_ref[...]
    b3 = biased.reshape(g_n, per, tm)
    member = lax.broadcasted_iota(I32, (g_n, per, tm), 1)
    top1 = jnp.max(b3, axis=1, keepdims=True)
    first = jnp.min(jnp.where(b3 == top1, member, per), axis=1, keepdims=True)
    top2 = jnp.max(jnp.where(member == first, -jnp.inf, b3), axis=1, keepdims=True)
    gs = top1 + top2
    gid = lax.broadcasted_iota(I32, (g_n, 1, tm), 0)
    rank = jnp.zeros((g_n, 1, tm), I32)
    for other in range(g_n):
        o = gs[other:other + 1]
        ahead = jnp.logical_or(o > gs, jnp.logical_and(o == gs, other < gid))
        rank = rank + jnp.where(ahead, 1, 0)
    cur = jnp.where(rank < TOPK_GROUPS, b3, -jnp.inf).reshape(e_n, tm)

    eid = lax.broadcasted_iota(I32, (e_n, tm), 0)
    picks, vals = [], []
    sel = jnp.zeros((e_n, tm), F32)
    for k in range(TOP_K):
        mx = jnp.max(cur, axis=0, keepdims=True)
        ik = jnp.min(jnp.where(cur == mx, eid, e_n), axis=0, keepdims=True)
        hit = eid == ik
        vals.append(jnp.sum(jnp.where(hit, scores, 0.0), axis=0, keepdims=True))
        cur = jnp.where(hit, -jnp.inf, cur)
        sel = jnp.where(hit, 1.0, sel)
        picks.append(ik)
    total = vals[0]
    for v in vals[1:]:
        total = total + v

    tri = jnp.where(lax.broadcasted_iota(I32, (tm, tm), 0) <= lax.broadcasted_iota(I32, (tm, tm), 1), 1.0, 0.0)
    incl = jnp.dot(sel.astype(BF16), tri.astype(BF16), preferred_element_type=F32)
    size = _round_up_pow2(jnp.broadcast_to(incl[:, tm - 1:tm], (e_n, LANES)).astype(I32), SEG_ALIGN)
    loff = _cum_rows(size, jnp.add, 0) - size
    base = loff[:, 0:1].astype(F32) + incl - 1.0
    for k in range(TOP_K):
        w_ref[k:k + 1, :] = vals[k] / total * ROUTE_SCALE
        lrow_ref[0, k:k + 1, :] = jnp.sum(jnp.where(eid == picks[k], base, 0.0),
                                          axis=0, keepdims=True).astype(I32)
    size_all[step] = size
    tot[...] = tot[...] + size

    @pl.when(step == pl.num_programs(0) - 1)
    def _():
        pcnt = _round_up_pow2(tot[...], bm)
        pend = _cum_rows(pcnt, jnp.add, 0)

        def tile_seg(i, run):
            segtab_ref[i, 0] = run
            segtab_ref[i, 1] = size_all[i]
            return run + size_all[i]

        lax.fori_loop(0, pl.num_programs(0), tile_seg, pend - pcnt)

        nb = blk_ref.shape[1]
        row0 = lax.broadcasted_iota(I32, (e_n, nb), 1) * bm
        total_rows = pend[e_n - 1:e_n, 0:1]
        owner = jnp.sum(jnp.where(pend[:, 0:1] <= row0, 1, 0), axis=0, keepdims=True)
        last_owner = jnp.sum(jnp.where(pend[:, 0:1] < total_rows, 1, 0), axis=0, keepdims=True)
        blk_ref[...] = jnp.where(row0[0:1, :] < total_rows, jnp.minimum(owner, e_n - 1), last_owner)
        seg_ref[0] = pend
        seg_ref[1] = pcnt


def _route(x, router_w_t, router_b, *, tm, bm, n_blocks):
    t, d = x.shape
    e_n = N_EXPERTS
    n_t = t // tm
    assert bm & (bm - 1) == 0
    nb_pad = -(-n_blocks // LANES) * LANES
    return pl.pallas_call(
        functools.partial(_route_kernel, tm=tm, bm=bm),
        grid=(n_t,),
        in_specs=[pl.BlockSpec((tm, d), lambda i: (i, 0)),
                  pl.BlockSpec((e_n, d), lambda i: (0, 0)),
                  pl.BlockSpec((e_n, 1), lambda i: (0, 0))],
        out_specs=[pl.BlockSpec((TOP_K, tm), lambda i: (0, i)),
                   pl.BlockSpec((1, TOP_K, tm), lambda i: (i, 0, 0)),
                   pl.BlockSpec((n_t, 2, e_n, LANES), lambda i: (0, 0, 0, 0)),
                   pl.BlockSpec((1, nb_pad), lambda i: (0, 0)),
                   pl.BlockSpec((2, e_n, LANES), lambda i: (0, 0, 0))],
        out_shape=[jax.ShapeDtypeStruct((TOP_K, t), F32), jax.ShapeDtypeStruct((n_t, TOP_K, tm), I32),
                   jax.ShapeDtypeStruct((n_t, 2, e_n, LANES), I32),
                   jax.ShapeDtypeStruct((1, nb_pad), I32), jax.ShapeDtypeStruct((2, e_n, LANES), I32)],
        scratch_shapes=[pltpu.VMEM((n_t, e_n, LANES), I32), pltpu.VMEM((e_n, LANES), I32)],
        compiler_params=_cp("arbitrary"),
    )(x, router_w_t, router_b)


def _tile_rows(tm):
    worst = TOP_K * tm + N_EXPERTS * (SEG_ALIGN - 1)
    return -(-worst // ONEHOT_ROWS) * ONEHOT_ROWS


def _onehot_rows(chunk, lrow, values, tm):
    rid = chunk * ONEHOT_ROWS + lax.broadcasted_iota(I32, (ONEHOT_ROWS, tm), 0)
    acc = jnp.zeros((ONEHOT_ROWS, tm), F32)
    for k in range(TOP_K):
        acc = jnp.where(rid == lrow[k:k + 1, :], 1.0 if values is None else values[k:k + 1, :], acc)
    return acc.astype(BF16)


def _segments(gstart_ref, size_ref, tile, make_copy, act, keep=None):
    def body(e, loff):
        n = size_ref[tile * N_EXPERTS + e]
        wanted = n > 0 if keep is None else jnp.logical_and(n > 0, keep(loff + n))

        @pl.when(wanted)
        def _():
            act(make_copy(pl.multiple_of(gstart_ref[tile * N_EXPERTS + e], SEG_ALIGN),
                          pl.multiple_of(loff, SEG_ALIGN), pl.multiple_of(n, SEG_ALIGN)))
        return loff + n
    return lax.fori_loop(0, N_EXPERTS, body, 0)


def _start(cp):
    cp.start()


def _wait(cp):
    cp.wait()


def _dispatch_kernel(gstart_ref, size_ref, pend_ref, pcnt_ref, x_ref, lrow_ref, xs_ref, stage, sem_a, sem_b, zsem,
                     *, tm, bm):
    i = pl.program_id(0)
    last = pl.num_programs(0) - 1
    n_chunks = stage.shape[0] // ONEHOT_ROWS
    split_chunk = n_chunks // 2
    split = split_chunk * ONEHOT_ROWS
    assert tm <= ONEHOT_ROWS
    zbuf = stage.at[0:bm, :]

    def seg_walk(tile, act, phase_b):
        sem = sem_b if phase_b else sem_a
        return _segments(
            gstart_ref, size_ref, tile,
            lambda g, loff, n: pltpu.make_async_copy(stage.at[pl.ds(loff, n), :], xs_ref.at[pl.ds(g, n), :], sem),
            act, (lambda end: end > split) if phase_b else (lambda end: end <= split))

    def pad_copy(e):
        return pltpu.make_async_copy(zbuf, xs_ref.at[pl.ds(pl.multiple_of(pend_ref[e] - bm, bm), bm), :], zsem)

    @pl.when(i == 0)
    def _():
        stage[0:bm, :] = jnp.zeros((bm, stage.shape[1]), stage.dtype)

        def start(e, c):
            @pl.when(pcnt_ref[e] > 0)
            def _():
                pad_copy(e).start()
            return c

        def wait(e, c):
            @pl.when(pcnt_ref[e] > 0)
            def _():
                pad_copy(e).wait()
            return c

        lax.fori_loop(0, N_EXPERTS, start, 0)
        lax.fori_loop(0, N_EXPERTS, wait, 0)

    xb = x_ref[...].astype(BF16)
    lrow = lrow_ref[0]

    def chunks(lo, hi):
        for c in range(lo, hi):
            stage[c * ONEHOT_ROWS:(c + 1) * ONEHOT_ROWS, :] = jnp.dot(
                _onehot_rows(c, lrow, None, tm), xb, preferred_element_type=F32).astype(BF16)

    prev = jnp.maximum(i - 1, 0)

    @pl.when(i > 0)
    def _():
        seg_walk(prev, _wait, False)
    chunks(0, split_chunk - 1)

    @pl.when(i > 0)
    def _():
        seg_walk(prev, _wait, True)
    chunks(split_chunk - 1, split_chunk)
    seg_walk(i, _start, False)
    chunks(split_chunk, n_chunks)
    seg_walk(i, _start, True)

    @pl.when(i == last)
    def _():
        seg_walk(i, _wait, False)
        seg_walk(i, _wait, True)


def _dispatch(x, lrow, gstart, size, pend, pcnt, *, rows, tm, bm):
    t, d = x.shape
    return pl.pallas_call(
        functools.partial(_dispatch_kernel, tm=tm, bm=bm),
        grid_spec=pltpu.PrefetchScalarGridSpec(
            num_scalar_prefetch=4,
            grid=(t // tm,),
            in_specs=[pl.BlockSpec((tm, d), lambda i, *_: (i, 0)),
                      pl.BlockSpec((1, TOP_K, tm), lambda i, *_: (i, 0, 0))],
            out_specs=pl.BlockSpec(memory_space=pl.ANY),
            scratch_shapes=[pltpu.VMEM((_tile_rows(tm), d), BF16)] + [pltpu.SemaphoreType.DMA(())] * 3,
        ),
        out_shape=jax.ShapeDtypeStruct((rows, d), BF16),
        compiler_params=_cp("arbitrary"),
    )(gstart, size, pend, pcnt, x, lrow)


def _expert_kernel(blk_e_ref, nused_ref, xs_ref, wgu_ref, wdn_ref, ys_ref, wgu_b, wdn_b):
    j = pl.program_id(0)
    used = j < nused_ref[0]
    changed = jnp.logical_or(j == 0, blk_e_ref[j] != blk_e_ref[jnp.maximum(j - 1, 0)])

    @pl.when(jnp.logical_and(used, changed))
    def _():
        wgu_b[...] = wgu_ref[...].astype(BF16)
        wdn_b[...] = wdn_ref[...].astype(BF16)

    @pl.when(used)
    def _():
        f = wdn_b.shape[0]
        gu = jnp.dot(xs_ref[...], wgu_b[...], preferred_element_type=F32)
        hidden = (_silu(gu[:, :f]) * gu[:, f:]).astype(BF16)
        ys_ref[...] = jnp.dot(hidden, wdn_b[...], preferred_element_type=F32).astype(BF16)


def _experts(xs, blk_e, nused, w_gu, w_dn, layer, *, bm):
    rows, d = xs.shape
    f2 = w_gu.shape[3]
    f = w_dn.shape[2]
    blk = lambda j, be, nu: (jnp.minimum(j, nu[0] - 1), 0)
    return pl.pallas_call(
        _expert_kernel,
        grid_spec=pltpu.PrefetchScalarGridSpec(
            num_scalar_prefetch=2,
            grid=(rows // bm,),
            in_specs=[pl.BlockSpec((bm, d), blk),
                      pl.BlockSpec((None, None, d, f2), lambda j, be, nu: (layer, be[j], 0, 0)),
                      pl.BlockSpec((None, None, f, d), lambda j, be, nu: (layer, be[j], 0, 0))],
            out_specs=pl.BlockSpec((bm, d), blk),
            scratch_shapes=[pltpu.VMEM((d, f2), BF16), pltpu.VMEM((f, d), BF16)],
        ),
        out_shape=jax.ShapeDtypeStruct((rows, d), BF16),
        compiler_params=_cp("arbitrary"),
    )(blk_e, nused, xs, w_gu, w_dn)


def _combine_kernel(gstart_ref, size_ref, x_ref, lrow_ref, w_ref, sdn_ref, lw_ref, lb_ref, sgu_hbm, ys_ref,
                    o_ref, ybuf, wt, hid, sgu, sems, wsem, *, tm, alpha):
    i, half = pl.program_id(0), pl.program_id(1)
    last_tile = pl.num_programs(0) - 1
    dh = ybuf.shape[2]
    n_chunks = ybuf.shape[1] // ONEHOT_ROWS

    def seg_walk(tile, hf, act):
        return _segments(
            gstart_ref, size_ref, tile,
            lambda g, loff, n: pltpu.make_async_copy(ys_ref.at[pl.ds(g, n), hf * dh:(hf + 1) * dh],
                                                     ybuf.at[hf, pl.ds(loff, n), :], sems.at[hf]),
            act)

    @pl.when(jnp.logical_and(i == 0, half == 0))
    def _():
        ybuf[...] = jnp.zeros_like(ybuf)
        weights = pltpu.make_async_copy(sgu_hbm, sgu, wsem)
        weights.start()
        weights.wait()
        seg_walk(0, 0, _start)

    @pl.when(half == 0)
    def _():
        seg_walk(i, 1, _start)
        f = hid.shape[1]
        gu = jnp.dot(x_ref[...].astype(BF16), sgu[...], preferred_element_type=F32)
        hidden = (_silu(gu[:, :f]) * gu[:, f:]).astype(BF16)
        hid[...] = hidden
        seg_walk(i, 0, _wait)
        lrow, w = lrow_ref[0], w_ref[...]
        acc = jnp.dot(hidden, sdn_ref[...], preferred_element_type=F32)
        for c in range(n_chunks):
            rows = slice(c * ONEHOT_ROWS, (c + 1) * ONEHOT_ROWS)
            w_rows = _onehot_rows(c, lrow, w, tm)
            wt[rows, :] = w_rows
            acc = acc + lax.dot_general(w_rows, ybuf[0, rows, :], (((0,), (0,)), ((), ())),
                                        preferred_element_type=F32)
        o_ref[:, 0:dh] = acc

    @pl.when(half == 1)
    def _():
        @pl.when(i < last_tile)
        def _():
            seg_walk(i + 1, 0, _start)
        seg_walk(i, 1, _wait)
        routed = lax.dot_general(wt[...], ybuf[1], (((0,), (0,)), ((), ())), preferred_element_type=F32)
        o_ref[:, dh:2 * dh] = routed + jnp.dot(hid[...], sdn_ref[...], preferred_element_type=F32)
        o_ref[...] = _layer_norm(alpha * x_ref[...] + o_ref[...], lw_ref[...], lb_ref[...])


def _combine(x, ys, lrow, wts, gstart, size, s_gu, s_dn, ln_w, ln_b, *, alpha, tm):
    t, d = x.shape
    dh = d // 2
    f = s_dn.shape[0]
    const = lambda shape: pl.BlockSpec(shape, lambda i, h, *_: (0, 0))
    return pl.pallas_call(
        functools.partial(_combine_kernel, tm=tm, alpha=alpha),
        grid_spec=pltpu.PrefetchScalarGridSpec(
            num_scalar_prefetch=2,
            grid=(t // tm, 2),
            in_specs=[pl.BlockSpec((tm, d), lambda i, h, *_: (i, 0)),
                      pl.BlockSpec((1, TOP_K, tm), lambda i, h, *_: (i, 0, 0)),
                      pl.BlockSpec((TOP_K, tm), lambda i, h, *_: (0, i)),
                      pl.BlockSpec((f, dh), lambda i, h, *_: (0, h)),
                      const((1, d)), const((1, d)),
                      pl.BlockSpec(memory_space=pl.ANY), pl.BlockSpec(memory_space=pl.ANY)],
            out_specs=pl.BlockSpec((tm, d), lambda i, h, *_: (i, 0)),
            scratch_shapes=[pltpu.VMEM((2, _tile_rows(tm), dh), BF16), pltpu.VMEM((_tile_rows(tm), tm), BF16),
                            pltpu.VMEM((tm, f), BF16), pltpu.VMEM(s_gu.shape, BF16),
                            pltpu.SemaphoreType.DMA((2,)), pltpu.SemaphoreType.DMA(())],
        ),
        out_shape=jax.ShapeDtypeStruct((t, d), F32),
        compiler_params=_cp("arbitrary", "arbitrary"),
    )(gstart, size, x, lrow, wts, s_dn, ln_w, ln_b, s_gu, ys)


def _mixer_sublayer(x, w_in, gate_b, pool_w, pool_scale, conv_w, mlstm_norm_w, ret_norm_w, w_branch, w_out,
                    ln_w, ln_b, *, batch, seq, alpha):
    t, d = x.shape
    width = d // N_BRANCH
    heads = width // HEAD_DIM
    gate_off = 8 * width
    ret_off = gate_off + 2 * heads
    w_head = w_in[:, :gate_off].astype(BF16)
    w_if = jnp.pad(w_in[:, gate_off:ret_off], ((0, 0), (0, LANES - 2 * heads))).astype(BF16)
    w_tail = _shifted_cast(w_in, col0=gate_off, shift=2 * heads, ncols=4 * width + N_BRANCH * d,
                           tr=512, tn=512)
    gate_bias = jnp.pad(gate_b, (0, LANES - 2 * heads)).reshape(1, LANES)

    proj_a = _matmul(x, w_head, tm=1024, tn=1024)
    proj_b = _matmul(x, w_tail, tm=1024, tn=1024, ncols=4 * width)
    gates = _matmul(x, w_if, tm=1024, tn=LANES)
    y_pool, y_conv = _pool_conv(proj_a, pool_w.astype(BF16), pool_scale.reshape(1, width), conv_w,
                                seq=seq, width=width, tb=512)
    y_mlstm = _mlstm(proj_a, gates, gate_bias, mlstm_norm_w.reshape(1, width),
                     batch=batch, seq=seq, width=width, col0=4)
    y_ret = _retention(proj_b, ret_norm_w.reshape(1, width), batch=batch, seq=seq, width=width, col0=0)
    merged = _merge(x, (y_pool, y_conv, y_mlstm, y_ret), w_tail, 4 * width, w_branch.astype(BF16),
                    tm=512, tn=512)
    return _proj_ln(merged, w_out.astype(BF16), x, ln_w, ln_b, alpha=alpha, tm=512)


def _xattn_sublayer(x, mem2d, wq, wk, wv, wo, ln_w, ln_b, *, seq, mem_len, alpha):
    w_kv = jnp.concatenate([wk, wv], axis=1).astype(BF16)
    kv = _matmul(mem2d, w_kv, tm=min(mem2d.shape[0], 1024), tn=512)
    return _xattn(x, kv, wq.astype(BF16), wo.astype(BF16), ln_w, ln_b,
                  alpha=alpha, seq=seq, mem_len=mem_len, tm=512)


def _moe_sublayer(x, router_w, router_b, w_gu, w_dn, layer, s_gu, s_dn, ln_w, ln_b, *, alpha):
    t, d = x.shape
    e_n, bm = N_EXPERTS, MOE_BM
    tm = ROUTE_TM
    n_blocks = -(-(t * TOP_K + (t // tm) * e_n * (SEG_ALIGN - 1)) // bm) + e_n
    wts, lrow, segtab, blk, seg = _route(x, router_w.T, router_b.reshape(e_n, 1), tm=tm, bm=bm, n_blocks=n_blocks)
    gstart, size = segtab[:, 0, :, 0].reshape(-1), segtab[:, 1, :, 0].reshape(-1)
    pend, pcnt = seg[0, :, 0], seg[1, :, 0]
    blk_e = blk[0, :n_blocks]
    nused = pend[e_n - 1] // bm

    xs = _dispatch(x, lrow, gstart, size, pend, pcnt, rows=n_blocks * bm, tm=tm, bm=bm)
    ys = _experts(xs, blk_e, nused.reshape(1), w_gu, w_dn, layer, bm=bm)
    return _combine(x, ys, lrow, wts, gstart, size, s_gu.astype(BF16), s_dn.astype(BF16), ln_w, ln_b,
                    alpha=alpha, tm=tm)


def kernel(x, mem, w_in, mlstm_gate_b, pool_w, pool_scale, conv_w, mlstm_norm_w, ret_norm_w, w_branch,
           w_mix_out, xa_wq, xa_wk, xa_wv, xa_wo, router_w, router_b, moe_w_gu, moe_w_dn, shared_w_gu,
           shared_w_dn, ln_w, ln_b):
    batch, seq, d = x.shape
    depth = w_in.shape[0]
    mem_len = mem.shape[1]
    alpha = (2 * depth) ** 0.25
    h = x.reshape(batch * seq, d)
    mem2d = mem.reshape(batch * mem_len, d)
    for l in range(depth):
        lw = ln_w[l].reshape(3, 1, d)
        lb = ln_b[l].reshape(3, 1, d)
        h = _mixer_sublayer(h, w_in[l], mlstm_gate_b[l], pool_w[l], pool_scale[l], conv_w[l], mlstm_norm_w[l],
                            ret_norm_w[l], w_branch[l], w_mix_out[l], lw[0], lb[0],
                            batch=batch, seq=seq, alpha=alpha)
        h = _xattn_sublayer(h, mem2d, xa_wq[l], xa_wk[l], xa_wv[l], xa_wo[l], lw[1], lb[1],
                            seq=seq, mem_len=mem_len, alpha=alpha)
        h = _moe_sublayer(h, router_w[l], router_b[l], moe_w_gu, moe_w_dn, l, shared_w_gu[l], shared_w_dn[l],
                          lw[2], lb[2], alpha=alpha)
    return h.reshape(batch, seq, d)
```

```python
import functools

import numpy as np
import jax
import jax.numpy as jnp
from jax import lax
from jax.experimental import pallas as pl
from jax.experimental.pallas import tpu as pltpu

F32 = jnp.float32
BF16 = jnp.bfloat16
I32 = jnp.int32

N_BRANCH = 4
HEAD_DIM = 128
POOL_WINDOWS = (2, 4, 8, 16)
CONV_WIDTH = 3
ROPE_BASE = 10000.0
XATTN_HEADS = 4
N_EXPERTS = 64
TOP_K = 8
N_GROUPS = 8
TOPK_GROUPS = 4
ROUTE_SCALE = 2.5
LN_EPS = 1e-5

LANES = 128
V7X_VMEM_BYTES = 64 * 1024 * 1024
VMEM_LIMIT = 56 * 1024 * 1024

SCAN_CHUNK = 256
SCAN_BLOCK = 512
HALO = 16
MOE_BM = 512
ROUTE_TM = 256
SEG_ALIGN = 16
ONEHOT_ROWS = 512


def _cp(*sem):
    return pltpu.CompilerParams(dimension_semantics=sem, vmem_limit_bytes=VMEM_LIMIT)


def _sigmoid(x):
    return 1.0 / (1.0 + jnp.exp(-x))


def _silu(x):
    return x * _sigmoid(x)


def _log_sigmoid(x):
    return jnp.minimum(x, 0.0) - jnp.log(1.0 + jnp.exp(-jnp.abs(x)))


def _layer_norm(z, w, b):
    mu = jnp.mean(z, axis=-1, keepdims=True)
    d = z - mu
    var = jnp.mean(d * d, axis=-1, keepdims=True)
    return d * lax.rsqrt(var + LN_EPS) * w + b


def _head_norm(h, w):
    mu = jnp.mean(h, axis=-1, keepdims=True)
    d = h - mu
    var = jnp.mean(d * d, axis=-1, keepdims=True)
    return d * lax.rsqrt(var + LN_EPS) * w


def _mm_kernel(x_ref, w_ref, o_ref, xb_ref):
    @pl.when(pl.program_id(1) == 0)
    def _():
        xb_ref[...] = x_ref[...].astype(BF16)

    o_ref[...] = jnp.dot(xb_ref[...], w_ref[...], preferred_element_type=F32)


def _realign_cast_kernel(a_ref, b_ref, o_ref, *, shift, lo, hi):
    j = pl.program_id(1)
    tn = o_ref.shape[1]
    shifted = jnp.logical_and(j >= lo, j < hi)

    @pl.when(shifted)
    def _():
        both = jnp.concatenate([a_ref[...], b_ref[...]], axis=1)
        o_ref[...] = both[:, shift:shift + tn].astype(BF16)

    @pl.when(jnp.logical_not(shifted))
    def _():
        o_ref[...] = a_ref[...].astype(BF16)


def _realign_cast(w_all, layer, *, lo_col, hi_col, shift, tr, tn):
    _, rows, _ = w_all.shape
    lo, hi = lo_col // tn, hi_col // tn
    src = lambda j: jnp.where(j == hi, lo, j)
    return pl.pallas_call(
        functools.partial(_realign_cast_kernel, shift=shift, lo=lo, hi=hi),
        grid=(rows // tr, hi + 1),
        in_specs=[pl.BlockSpec((None, tr, tn), lambda i, j: (layer, i, src(j))),
                  pl.BlockSpec((None, tr, LANES), lambda i, j: (layer, i, (src(j) + 1) * (tn // LANES)))],
        out_specs=pl.BlockSpec((tr, tn), lambda i, j: (i, j)),
        out_shape=jax.ShapeDtypeStruct((rows, hi_col + tn), BF16),
        compiler_params=_cp("parallel", "parallel"),
    )(w_all, w_all)


def _matmul(x, w, *, tm, tn, ncols=None, col0=0):
    t, k = x.shape
    n = w.shape[1] if ncols is None else ncols
    return pl.pallas_call(
        _mm_kernel,
        grid=(t // tm, n // tn),
        in_specs=[pl.BlockSpec((tm, k), lambda i, j: (i, 0)),
                  pl.BlockSpec((k, tn), lambda i, j: (0, col0 // tn + j))],
        out_specs=pl.BlockSpec((tm, tn), lambda i, j: (i, j)),
        out_shape=jax.ShapeDtypeStruct((t, n), F32),
        scratch_shapes=[pltpu.VMEM((tm, k), BF16)],
        compiler_params=_cp("parallel", "arbitrary"),
    )(x, w)


def _poolconv_kernel(u_ref, uh_ref, h_ref, hh_ref, b_ref, c_ref, ch_ref, pw_ref, ps_ref, cw_ref,
                     yp_ref, yc_ref, ubuf, zbuf, *, tb, n_tblk):
    first = (pl.program_id(0) % n_tblk) == 0
    ubuf[0:HALO, :] = jnp.where(first, 0.0, uh_ref[...])
    ubuf[HALO:HALO + tb, :] = u_ref[...]
    zbuf[0:HALO, :] = jnp.where(first, 0.0, ch_ref[...] * hh_ref[...])
    zbuf[HALO:HALO + tb, :] = c_ref[...] * h_ref[...]

    t_pos = (pl.program_id(0) % n_tblk) * tb + lax.broadcasted_iota(I32, (tb, LANES), 0)
    gw = u_ref.shape[1] // len(POOL_WINDOWS)
    for grp, win in enumerate(POOL_WINDOWS):
        lanes = slice(grp * gw, (grp + 1) * gw)
        cur = ubuf[HALO:HALO + tb, lanes]
        acc = cur
        for lag in range(1, win):
            acc = acc + ubuf[HALO - lag:HALO - lag + tb, lanes]
        count = jnp.minimum(t_pos + 1, win).astype(F32)
        mixed = acc / count - cur
        y = jnp.dot(mixed.astype(BF16), pw_ref[grp], preferred_element_type=F32)
        yp_ref[:, lanes] = y * ps_ref[:, lanes]

    conv = cw_ref[0:1, :] * zbuf[HALO:HALO + tb, :]
    for lag in range(1, CONV_WIDTH):
        conv = conv + cw_ref[lag:lag + 1, :] * zbuf[HALO - lag:HALO - lag + tb, :]
    yc_ref[...] = b_ref[...] * conv


def _pool_conv(proj, pool_w, pool_scale, conv_w, *, seq, width, tb):
    t = proj.shape[0]
    n_tblk = seq // tb
    ratio = tb // HALO

    def cur(col):
        return pl.BlockSpec((tb, width), lambda g: (g, col))

    def halo(col):
        return pl.BlockSpec((HALO, width), lambda g: (jnp.maximum(g * ratio - 1, 0), col))

    full = lambda shape: pl.BlockSpec(shape, lambda g: (0,) * len(shape))
    return pl.pallas_call(
        functools.partial(_poolconv_kernel, tb=tb, n_tblk=n_tblk),
        grid=(t // tb,),
        in_specs=[cur(0), halo(0), cur(1), halo(1), cur(2), cur(3), halo(3),
                  full(pool_w.shape), full(pool_scale.shape), full(conv_w.shape)],
        out_specs=[pl.BlockSpec((tb, width), lambda g: (g, 0))] * 2,
        out_shape=[jax.ShapeDtypeStruct((t, width), F32)] * 2,
        scratch_shapes=[pltpu.VMEM((HALO + tb, width), F32)] * 2,
        compiler_params=_cp("parallel"),
    )(proj, proj, proj, proj, proj, proj, proj, pool_w, pool_scale, conv_w)


def _cum_rows(x, op, fill):
    n = x.shape[0]
    row = lax.broadcasted_iota(I32, x.shape, 0)
    shift = 1
    while shift < n:
        x = op(x, jnp.where(row >= shift, pltpu.roll(x, shift, axis=0), fill))
        shift *= 2
    return x


def _mlstm_kernel(q_ref, k_ref, v_ref, o_ref, g_ref, gb_ref, nw_ref, y_ref, c_scr, n_scr, m_scr,
                  *, chunk, heads):
    @pl.when(pl.program_id(1) == 0)
    def _():
        c_scr[...] = jnp.zeros_like(c_scr)
        n_scr[...] = jnp.zeros_like(n_scr)
        m_scr[...] = jnp.zeros_like(m_scr)

    tb = q_ref.shape[0]
    hd = HEAD_DIM
    scale = hd ** -0.5
    tri = (lax.broadcasted_iota(I32, (chunk, chunk), 0) >= lax.broadcasted_iota(I32, (chunk, chunk), 1))
    for c in range(tb // chunk):
        rows = slice(c * chunk, (c + 1) * chunk)
        gates = g_ref[rows, :] + gb_ref[...]
        lf = _log_sigmoid(pltpu.roll(gates, LANES - heads, axis=1))
        cumf = _cum_rows(lf, jnp.add, 0.0)
        a = gates - cumf
        m_prev = m_scr[...]
        mu = jnp.maximum(_cum_rows(a, jnp.maximum, -jnp.inf), m_prev)
        mu_last = mu[chunk - 1:chunk, :]
        a_t = a.T
        for h in range(heads):
            cols = slice(h * hd, (h + 1) * hd)
            q = q_ref[rows, cols]
            k = k_ref[rows, cols] * scale
            v = v_ref[rows, cols]
            qb, kb, vb = q.astype(BF16), k.astype(BF16), v.astype(BF16)
            mu_col = mu[:, h:h + 1]
            a_col = a[:, h:h + 1]
            m_prev_h = m_prev[:, h:h + 1]
            mu_last_h = mu_last[:, h:h + 1]
            dmat = jnp.exp(jnp.where(tri, a_t[h:h + 1, :] - mu_col, -jnp.inf))
            s = lax.dot_general(qb, kb, (((1,), (1,)), ((), ())), preferred_element_type=F32)
            p = dmat * s
            inter = jnp.exp(m_prev_h - mu_col)
            c_h = c_scr[h]
            n_h = n_scr[h]
            num = inter * jnp.dot(qb, c_h.astype(BF16), preferred_element_type=F32) \
                + jnp.dot(p.astype(BF16), vb, preferred_element_type=F32)
            den = inter * jnp.sum(q * n_h, axis=-1, keepdims=True) + jnp.sum(p, axis=-1, keepdims=True)
            floor = jnp.exp(-(cumf[:, h:h + 1] + mu_col))
            h_out = num / jnp.maximum(jnp.abs(den), floor)

            wg = jnp.exp(a_col - mu_last_h)
            decay = jnp.exp(m_prev_h - mu_last_h)
            kw = k * wg
            c_scr[h] = decay * c_h + lax.dot_general(kw.astype(BF16), vb, (((0,), (0,)), ((), ())),
                                                     preferred_element_type=F32)
            n_scr[h] = decay * n_h + jnp.sum(kw, axis=0, keepdims=True)

            gated = _sigmoid(o_ref[rows, cols]) * h_out
            y_ref[rows, cols] = _head_norm(gated, nw_ref[:, cols])
        m_scr[...] = cumf[chunk - 1:chunk, :] + mu_last


def _mlstm(proj, gates, gate_b, norm_w, *, batch, seq, width, col0):
    t = proj.shape[0]
    heads = width // HEAD_DIM
    n_tblk = seq // SCAN_BLOCK
    blk = lambda col: pl.BlockSpec((SCAN_BLOCK, width), lambda b, i: (b * n_tblk + i, col))
    return pl.pallas_call(
        functools.partial(_mlstm_kernel, chunk=SCAN_CHUNK, heads=heads),
        grid=(batch, n_tblk),
        in_specs=[blk(col0), blk(col0 + 1), blk(col0 + 2), blk(col0 + 3),
                  pl.BlockSpec((SCAN_BLOCK, LANES), lambda b, i: (b * n_tblk + i, 0)),
                  pl.BlockSpec((1, LANES), lambda b, i: (0, 0)),
                  pl.BlockSpec((1, width), lambda b, i: (0, 0))],
        out_specs=pl.BlockSpec((SCAN_BLOCK, width), lambda b, i: (b * n_tblk + i, 0)),
        out_shape=jax.ShapeDtypeStruct((t, width), F32),
        scratch_shapes=[pltpu.VMEM((heads, HEAD_DIM, HEAD_DIM), F32),
                        pltpu.VMEM((heads, 1, HEAD_DIM), F32),
                        pltpu.VMEM((1, LANES), F32)],
        compiler_params=_cp("parallel", "arbitrary"),
    )(proj, proj, proj, proj, gates, gate_b, norm_w)


def _ret_kernel(q_ref, k_ref, v_ref, g_ref, cos_ref, sin_ref, intra_ref, cross_ref, zeta_ref, nw_ref,
                y_ref, r_scr, *, chunk, heads, chunk_decay):
    @pl.when(pl.program_id(1) == 0)
    def _():
        r_scr[...] = jnp.zeros_like(r_scr)

    tb = q_ref.shape[0]
    hd = HEAD_DIM
    scale = hd ** -0.5
    for c in range(tb // chunk):
        rows = slice(c * chunk, (c + 1) * chunk)
        cos = cos_ref[rows, :]
        sin = sin_ref[rows, :]
        for h in range(heads):
            cols = slice(h * hd, (h + 1) * hd)
            q = q_ref[rows, cols]
            k = k_ref[rows, cols]
            q = q * cos + pltpu.roll(q, hd // 2, axis=1) * sin
            k = (k * cos + pltpu.roll(k, hd // 2, axis=1) * sin) * scale
            vb = v_ref[rows, cols].astype(BF16)
            qb = q.astype(BF16)
            s = lax.dot_general(qb, k.astype(BF16), (((1,), (1,)), ((), ())), preferred_element_type=F32)
            inner = jnp.dot((s * intra_ref[h]).astype(BF16), vb, preferred_element_type=F32)
            r_h = r_scr[h]
            crs = jnp.dot(qb, r_h.astype(BF16), preferred_element_type=F32) * cross_ref[h]
            r_scr[h] = chunk_decay[h] * r_h + lax.dot_general(
                (k * zeta_ref[h]).astype(BF16), vb, (((0,), (0,)), ((), ())), preferred_element_type=F32)
            y_ref[rows, cols] = _silu(g_ref[rows, cols]) * _head_norm(inner + crs, nw_ref[:, cols])


def _retention(proj, norm_w, *, batch, seq, width, col0):
    t = proj.shape[0]
    heads = width // HEAD_DIM
    chunk = SCAN_CHUNK
    n_tblk = seq // SCAN_BLOCK
    half = HEAD_DIM // 2
    inv = np.float32(ROPE_BASE) ** (-np.arange(half, dtype=np.float32) / np.float32(half))
    ang = (np.arange(seq, dtype=np.float32)[:, None] * inv[None, :]).astype(np.float64)
    cos_t = jnp.asarray(np.concatenate([np.cos(ang), np.cos(ang)], axis=-1), F32)
    sin_t = jnp.asarray(np.concatenate([-np.sin(ang), np.sin(ang)], axis=-1), F32)
    log_g = jnp.log(1.0 - 2.0 ** (-5.0 - jnp.arange(heads, dtype=F32)))
    tt = jnp.arange(chunk, dtype=F32)
    lag = tt[:, None] - tt[None, :]
    intra = jnp.where(lag >= 0, jnp.exp(jnp.maximum(lag, 0.0)[None] * log_g[:, None, None]), 0.0)
    cross = jnp.broadcast_to(jnp.exp((tt + 1.0)[None, :] * log_g[:, None])[:, :, None], (heads, chunk, HEAD_DIM))
    zeta = jnp.broadcast_to(jnp.exp((chunk - 1.0 - tt)[None, :] * log_g[:, None])[:, :, None],
                            (heads, chunk, HEAD_DIM))
    chunk_decay = tuple(float((1.0 - 2.0 ** (-5.0 - h)) ** chunk) for h in range(heads))

    blk = lambda col: pl.BlockSpec((SCAN_BLOCK, width), lambda b, i: (b * n_tblk + i, col))
    pos = pl.BlockSpec((SCAN_BLOCK, HEAD_DIM), lambda b, i: (i, 0))
    full3 = lambda a: pl.BlockSpec(a.shape, lambda b, i: (0, 0, 0))
    return pl.pallas_call(
        functools.partial(_ret_kernel, chunk=chunk, heads=heads, chunk_decay=chunk_decay),
        grid=(batch, n_tblk),
        in_specs=[blk(col0), blk(col0 + 1), blk(col0 + 2), blk(col0 + 3), pos, pos,
                  full3(intra), full3(cross), full3(zeta),
                  pl.BlockSpec((1, width), lambda b, i: (0, 0))],
        out_specs=pl.BlockSpec((SCAN_BLOCK, width), lambda b, i: (b * n_tblk + i, 0)),
        out_shape=jax.ShapeDtypeStruct((t, width), F32),
        scratch_shapes=[pltpu.VMEM((heads, HEAD_DIM, HEAD_DIM), F32)],
        compiler_params=_cp("parallel", "arbitrary"),
    )(proj, proj, proj, proj, cos_t, sin_t, intra, cross, zeta, norm_w)


def _merge_kernel(x_ref, *refs):
    y_refs, wg_refs = refs[:N_BRANCH], refs[N_BRANCH:2 * N_BRANCH]
    wb_ref, o_ref, xb_ref = refs[2 * N_BRANCH:]

    @pl.when(pl.program_id(1) == 0)
    def _():
        xb_ref[...] = x_ref[...].astype(BF16)

    xb = xb_ref[...]
    acc = None
    for n in range(N_BRANCH):
        gate = _sigmoid(jnp.dot(xb, wg_refs[n][...], preferred_element_type=F32))
        term = gate * jnp.dot(y_refs[n][...].astype(BF16), wb_ref[n], preferred_element_type=F32)
        acc = term if acc is None else acc + term
    o_ref[...] = acc


def _merge(x, branches, w_gate, gate_col0, w_branch, *, tm, tn):
    t, d = x.shape
    width = branches[0].shape[1]
    ybs = pl.BlockSpec((tm, width), lambda i, j: (i, 0))

    def gate_spec(n):
        return pl.BlockSpec((d, tn), lambda i, j: (0, (gate_col0 + n * d) // tn + j))

    return pl.pallas_call(
        _merge_kernel,
        grid=(t // tm, d // tn),
        in_specs=[pl.BlockSpec((tm, d), lambda i, j: (i, 0))] + [ybs] * N_BRANCH
                 + [gate_spec(n) for n in range(N_BRANCH)]
                 + [pl.BlockSpec((N_BRANCH, width, tn), lambda i, j: (0, 0, j))],
        out_specs=pl.BlockSpec((tm, tn), lambda i, j: (i, j)),
        out_shape=jax.ShapeDtypeStruct((t, d), F32),
        scratch_shapes=[pltpu.VMEM((tm, d), BF16)],
        compiler_params=_cp("parallel", "arbitrary"),
    )(x, *branches, *([w_gate] * N_BRANCH), w_branch)


def _proj_ln_kernel(a_ref, w_ref, r_ref, lw_ref, lb_ref, o_ref, *, alpha):
    y = jnp.dot(a_ref[...].astype(BF16), w_ref[...], preferred_element_type=F32)
    o_ref[...] = _layer_norm(alpha * r_ref[...] + y, lw_ref[...], lb_ref[...])


def _proj_ln(a, w, resid, ln_w, ln_b, *, alpha, tm):
    t, k = a.shape
    d = w.shape[1]
    row = lambda n: pl.BlockSpec((tm, n), lambda i: (i, 0))
    const = lambda shape: pl.BlockSpec(shape, lambda i: (0, 0))
    return pl.pallas_call(
        functools.partial(_proj_ln_kernel, alpha=alpha),
        grid=(t // tm,),
        in_specs=[row(k), const((k, d)), row(d), const((1, d)), const((1, d))],
        out_specs=row(d),
        out_shape=jax.ShapeDtypeStruct((t, d), F32),
        compiler_params=_cp("parallel"),
    )(a, w, resid, ln_w, ln_b)


def _xattn_kernel(x_ref, wq_ref, kv_ref, wo_ref, lw_ref, lb_ref, o_ref, *, alpha, heads):
    x = x_ref[...]
    hd = HEAD_DIM
    inner = heads * hd
    q = jnp.dot(x.astype(BF16), wq_ref[...], preferred_element_type=F32)
    outs = []
    for h in range(heads):
        qh = q[:, h * hd:(h + 1) * hd].astype(BF16)
        kh = kv_ref[:, h * hd:(h + 1) * hd].astype(BF16)
        vh = kv_ref[:, inner + h * hd:inner + (h + 1) * hd].astype(BF16)
        s = lax.dot_general(qh, kh, (((1,), (1,)), ((), ())), preferred_element_type=F32) * hd ** -0.5
        s = s - jnp.max(s, axis=-1, keepdims=True)
        e = jnp.exp(s)
        p = e / jnp.sum(e, axis=-1, keepdims=True)
        outs.append(jnp.dot(p.astype(BF16), vh, preferred_element_type=F32).astype(BF16))
    o = jnp.concatenate(outs, axis=-1)
    y = jnp.dot(o, wo_ref[...], preferred_element_type=F32)
    o_ref[...] = _layer_norm(alpha * x + y, lw_ref[...], lb_ref[...])


def _xattn(x, kv, wq, wo, ln_w, ln_b, *, alpha, seq, mem_len, tm):
    t, d = x.shape
    inner = wq.shape[1]
    n_tblk = seq // tm
    const = lambda shape: pl.BlockSpec(shape, lambda i: (0, 0))
    return pl.pallas_call(
        functools.partial(_xattn_kernel, alpha=alpha, heads=XATTN_HEADS),
        grid=(t // tm,),
        in_specs=[pl.BlockSpec((tm, d), lambda i: (i, 0)), const((d, inner)),
                  pl.BlockSpec((mem_len, 2 * inner), lambda i: (i // n_tblk, 0)),
                  const((inner, d)), const((1, d)), const((1, d))],
        out_specs=pl.BlockSpec((tm, d), lambda i: (i, 0)),
        out_shape=jax.ShapeDtypeStruct((t, d), F32),
        compiler_params=_cp("parallel"),
    )(x, wq, kv, wo, ln_w, ln_b)


def _round_up_pow2(x, m):
    shift = m.bit_length() - 1
    return jnp.left_shift(jnp.right_shift(x + (m - 1), shift), shift)


def _route_kernel(x_ref, wr_ref, rb_ref, w_ref, lrow_ref, segtab_ref, blk_ref, seg_ref, size_all, tot, *, tm, bm):
    step = pl.program_id(0)

    @pl.when(step == 0)
    def _():
        tot[...] = jnp.zeros_like(tot)

    e_n, g_n = N_EXPERTS, N_GROUPS
    per = e_n // g_n
    logits = lax.dot_general(wr_ref[...], x_ref[...], (((1,), (1,)), ((), ())),
                             precision=lax.Precision.HIGHEST, preferred_element_type=F32)
    scores = _sigmoid(logits)
    biased = scores + rb_ref[...]
    b3 = biased.reshape(g_n, per, tm)
    member = lax.broadcasted_iota(I32, (g_n, per, tm), 1)
    top1 = jnp.max(b3, axis=1, keepdims=True)
    first = jnp.min(jnp.where(b3 == top1, member, per), axis=1, keepdims=True)
    top2 = jnp.max(jnp.where(member == first, -jnp.inf, b3), axis=1, keepdims=True)
    gs = top1 + top2
    gid = lax.broadcasted_iota(I32, (g_n, 1, tm), 0)
    rank = jnp.zeros((g_n, 1, tm), I32)
    for other in range(g_n):
        o = gs[other:other + 1]
        ahead = jnp.logical_or(o > gs, jnp.logical_and(o == gs, other < gid))
        rank = rank + jnp.where(ahead, 1, 0)
    cur = jnp.where(rank < TOPK_GROUPS, b3, -jnp.inf).reshape(e_n, tm)

    eid = lax.broadcasted_iota(I32, (e_n, tm), 0)
    picks, vals = [], []
    sel = jnp.zeros((e_n, tm), F32)
    for k in range(TOP_K):
        mx = jnp.max(cur, axis=0, keepdims=True)
        ik = jnp.min(jnp.where(cur == mx, eid, e_n), axis=0, keepdims=True)
        hit = eid == ik
        vals.append(jnp.sum(jnp.where(hit, scores, 0.0), axis=0, keepdims=True))
        cur = jnp.where(hit, -jnp.inf, cur)
        sel = jnp.where(hit, 1.0, sel)
        picks.append(ik)
    total = vals[0]
    for v in vals[1:]:
        total = total + v

    tri = jnp.where(lax.broadcasted_iota(I32, (tm, tm), 0) <= lax.broadcasted_iota(I32, (tm, tm), 1), 1.0, 0.0)
    incl = jnp.dot(sel.astype(BF16), tri.astype(BF16), preferred_element_type=F32)
    size = _round_up_pow2(jnp.broadcast_to(incl[:, tm - 1:tm], (e_n, LANES)).astype(I32), SEG_ALIGN)
    loff = _cum_rows(size, jnp.add, 0) - size
    base = loff[:, 0:1].astype(F32) + incl - 1.0
    for k in range(TOP_K):
        w_ref[k:k + 1, :] = vals[k] / total * ROUTE_SCALE
        lrow_ref[0, k:k + 1, :] = jnp.sum(jnp.where(eid == picks[k], base, 0.0),
                                          axis=0, keepdims=True).astype(I32)
    size_all[step] = size
    tot[...] = tot[...] + size

    @pl.when(step == pl.num_programs(0) - 1)
    def _():
        pcnt = _round_up_pow2(tot[...], bm)
        pend = _cum_rows(pcnt, jnp.add, 0)

        def tile_seg(i, run):
            segtab_ref[i, 0] = run
            segtab_ref[i, 1] = size_all[i]
            return run + size_all[i]

        lax.fori_loop(0, pl.num_programs(0), tile_seg, pend - pcnt)

        nb = blk_ref.shape[1]
        row0 = lax.broadcasted_iota(I32, (e_n, nb), 1) * bm
        total_rows = pend[e_n - 1:e_n, 0:1]
        owner = jnp.sum(jnp.where(pend[:, 0:1] <= row0, 1, 0), axis=0, keepdims=True)
        last_owner = jnp.sum(jnp.where(pend[:, 0:1] < total_rows, 1, 0), axis=0, keepdims=True)
        blk_ref[...] = jnp.where(row0[0:1, :] < total_rows, jnp.minimum(owner, e_n - 1), last_owner)
        seg_ref[0] = pend
        seg_ref[1] = pcnt


def _route(x, router_w_t, router_b, *, tm, bm, n_blocks):
    t, d = x.shape
    e_n = N_EXPERTS
    n_t = t // tm
    assert bm & (bm - 1) == 0
    nb_pad = -(-n_blocks // LANES) * LANES
    return pl.pallas_call(
        functools.partial(_route_kernel, tm=tm, bm=bm),
        grid=(n_t,),
        in_specs=[pl.BlockSpec((tm, d), lambda i: (i, 0)),
                  pl.BlockSpec((e_n, d), lambda i: (0, 0)),
                  pl.BlockSpec((e_n, 1), lambda i: (0, 0))],
        out_specs=[pl.BlockSpec((TOP_K, tm), lambda i: (0, i)),
                   pl.BlockSpec((1, TOP_K, tm), lambda i: (i, 0, 0)),
                   pl.BlockSpec((n_t, 2, e_n, LANES), lambda i: (0, 0, 0, 0)),
                   pl.BlockSpec((1, nb_pad), lambda i: (0, 0)),
                   pl.BlockSpec((2, e_n, LANES), lambda i: (0, 0, 0))],
        out_shape=[jax.ShapeDtypeStruct((TOP_K, t), F32), jax.ShapeDtypeStruct((n_t, TOP_K, tm), I32),
                   jax.ShapeDtypeStruct((n_t, 2, e_n, LANES), I32),
                   jax.ShapeDtypeStruct((1, nb_pad), I32), jax.ShapeDtypeStruct((2, e_n, LANES), I32)],
        scratch_shapes=[pltpu.VMEM((n_t, e_n, LANES), I32), pltpu.VMEM((e_n, LANES), I32)],
        compiler_params=_cp("arbitrary"),
    )(x, router_w_t, router_b)


def _tile_rows(tm):
    worst = TOP_K * tm + N_EXPERTS * (SEG_ALIGN - 1)
    return -(-worst // ONEHOT_ROWS) * ONEHOT_ROWS


def _onehot_rows(chunk, lrow, values, tm):
    rid = chunk * ONEHOT_ROWS + lax.broadcasted_iota(I32, (ONEHOT_ROWS, tm), 0)
    acc = jnp.zeros((ONEHOT_ROWS, tm), F32)
    for k in range(TOP_K):
        acc = jnp.where(rid == lrow[k:k + 1, :], 1.0 if values is None else values[k:k + 1, :], acc)
    return acc.astype(BF16)


def _segments(gstart_ref, size_ref, tile, make_copy, act, keep=None):
    def body(e, loff):
        n = size_ref[tile * N_EXPERTS + e]
        wanted = n > 0 if keep is None else jnp.logical_and(n > 0, keep(loff + n))

        @pl.when(wanted)
        def _():
            act(make_copy(pl.multiple_of(gstart_ref[tile * N_EXPERTS + e], SEG_ALIGN),
                          pl.multiple_of(loff, SEG_ALIGN), pl.multiple_of(n, SEG_ALIGN)))
        return loff + n
    return lax.fori_loop(0, N_EXPERTS, body, 0)


def _start(cp):
    cp.start()


def _wait(cp):
    cp.wait()


def _dispatch_kernel(gstart_ref, size_ref, pend_ref, pcnt_ref, x_ref, lrow_ref, xs_ref, stage, sem_a, sem_b, zsem,
                     *, tm, bm):
    i = pl.program_id(0)
    last = pl.num_programs(0) - 1
    n_chunks = stage.shape[0] // ONEHOT_ROWS
    split_chunk = n_chunks // 2
    split = split_chunk * ONEHOT_ROWS
    assert tm <= ONEHOT_ROWS
    zbuf = stage.at[0:bm, :]

    def seg_walk(tile, act, phase_b):
        sem = sem_b if phase_b else sem_a
        return _segments(
            gstart_ref, size_ref, tile,
            lambda g, loff, n: pltpu.make_async_copy(stage.at[pl.ds(loff, n), :], xs_ref.at[pl.ds(g, n), :], sem),
            act, (lambda end: end > split) if phase_b else (lambda end: end <= split))

    def pad_copy(e):
        return pltpu.make_async_copy(zbuf, xs_ref.at[pl.ds(pl.multiple_of(pend_ref[e] - bm, bm), bm), :], zsem)

    @pl.when(i == 0)
    def _():
        stage[0:bm, :] = jnp.zeros((bm, stage.shape[1]), stage.dtype)

        def start(e, c):
            @pl.when(pcnt_ref[e] > 0)
            def _():
                pad_copy(e).start()
            return c

        def wait(e, c):
            @pl.when(pcnt_ref[e] > 0)
            def _():
                pad_copy(e).wait()
            return c

        lax.fori_loop(0, N_EXPERTS, start, 0)
        lax.fori_loop(0, N_EXPERTS, wait, 0)

    xb = x_ref[...].astype(BF16)
    lrow = lrow_ref[0]
    n_rows = lax.fori_loop(0, N_EXPERTS, lambda e, s: s + size_ref[i * N_EXPERTS + e], 0)

    def chunks(lo, hi):
        for c in range(lo, hi):
            def one(c=c):
                stage[c * ONEHOT_ROWS:(c + 1) * ONEHOT_ROWS, :] = jnp.dot(
                    _onehot_rows(c, lrow, None, tm), xb, preferred_element_type=F32).astype(BF16)
            if c * ONEHOT_ROWS < TOP_K * tm:
                one()
            else:
                pl.when(c * ONEHOT_ROWS < n_rows)(one)

    prev = jnp.maximum(i - 1, 0)

    @pl.when(i > 0)
    def _():
        seg_walk(prev, _wait, False)
    chunks(0, split_chunk - 1)

    @pl.when(i > 0)
    def _():
        seg_walk(prev, _wait, True)
    chunks(split_chunk - 1, split_chunk)
    seg_walk(i, _start, False)
    chunks(split_chunk, n_chunks)
    seg_walk(i, _start, True)

    @pl.when(i == last)
    def _():
        seg_walk(i, _wait, False)
        seg_walk(i, _wait, True)


def _dispatch(x, lrow, gstart, size, pend, pcnt, *, rows, tm, bm):
    t, d = x.shape
    return pl.pallas_call(
        functools.partial(_dispatch_kernel, tm=tm, bm=bm),
        grid_spec=pltpu.PrefetchScalarGridSpec(
            num_scalar_prefetch=4,
            grid=(t // tm,),
            in_specs=[pl.BlockSpec((tm, d), lambda i, *_: (i, 0)),
                      pl.BlockSpec((1, TOP_K, tm), lambda i, *_: (i, 0, 0))],
            out_specs=pl.BlockSpec(memory_space=pl.ANY),
            scratch_shapes=[pltpu.VMEM((_tile_rows(tm), d), BF16)] + [pltpu.SemaphoreType.DMA(())] * 3,
        ),
        out_shape=jax.ShapeDtypeStruct((rows, d), BF16),
        compiler_params=_cp("arbitrary"),
    )(gstart, size, pend, pcnt, x, lrow)


def _expert_kernel(blk_e_ref, nused_ref, xs_ref, wgu_ref, wdn_ref, ys_ref, wgu_b, wdn_b):
    j = pl.program_id(0)
    used = j < nused_ref[0]
    changed = jnp.logical_or(j == 0, blk_e_ref[j] != blk_e_ref[jnp.maximum(j - 1, 0)])

    @pl.when(jnp.logical_and(used, changed))
    def _():
        wgu_b[...] = wgu_ref[...].astype(BF16)
        wdn_b[...] = wdn_ref[...].astype(BF16)

    @pl.when(used)
    def _():
        f = wdn_b.shape[0]
        gu = jnp.dot(xs_ref[...], wgu_b[...], preferred_element_type=F32)
        hidden = (_silu(gu[:, :f]) * gu[:, f:]).astype(BF16)
        ys_ref[...] = jnp.dot(hidden, wdn_b[...], preferred_element_type=F32).astype(BF16)


def _experts(xs, blk_e, nused, w_gu, w_dn, layer, *, bm):
    rows, d = xs.shape
    f2 = w_gu.shape[3]
    f = w_dn.shape[2]
    blk = lambda j, be, nu: (jnp.minimum(j, nu[0] - 1), 0)
    return pl.pallas_call(
        _expert_kernel,
        grid_spec=pltpu.PrefetchScalarGridSpec(
            num_scalar_prefetch=2,
            grid=(rows // bm,),
            in_specs=[pl.BlockSpec((bm, d), blk),
                      pl.BlockSpec((None, None, d, f2), lambda j, be, nu: (layer, be[j], 0, 0)),
                      pl.BlockSpec((None, None, f, d), lambda j, be, nu: (layer, be[j], 0, 0))],
            out_specs=pl.BlockSpec((bm, d), blk),
            scratch_shapes=[pltpu.VMEM((d, f2), BF16), pltpu.VMEM((f, d), BF16)],
        ),
        out_shape=jax.ShapeDtypeStruct((rows, d), BF16),
        compiler_params=_cp("arbitrary"),
    )(blk_e, nused, xs, w_gu, w_dn)


def _combine_kernel(gstart_ref, size_ref, x_ref, lrow_ref, w_ref, sdn_ref, lw_ref, lb_ref, sgu_hbm, ys_ref,
                    o_ref, ybuf, wt, hid, sgu, sems, wsem, *, tm, alpha):
    i, half = pl.program_id(0), pl.program_id(1)
    last_tile = pl.num_programs(0) - 1
    dh = ybuf.shape[2]
    n_chunks = ybuf.shape[1] // ONEHOT_ROWS
    sure_chunks = (TOP_K * tm) // ONEHOT_ROWS

    def contract_rows(w_rows, y_rows):
        return lax.dot_general(w_rows, y_rows, (((0,), (0,)), ((), ())), preferred_element_type=F32)

    def seg_walk(tile, hf, act):
        return _segments(
            gstart_ref, size_ref, tile,
            lambda g, loff, n: pltpu.make_async_copy(ys_ref.at[pl.ds(g, n), hf * dh:(hf + 1) * dh],
                                                     ybuf.at[hf, pl.ds(loff, n), :], sems.at[hf]),
            act)

    @pl.when(jnp.logical_and(i == 0, half == 0))
    def _():
        ybuf[...] = jnp.zeros_like(ybuf)
        weights = pltpu.make_async_copy(sgu_hbm, sgu, wsem)
        weights.start()
        weights.wait()
        seg_walk(0, 0, _start)

    @pl.when(half == 0)
    def _():
        seg_walk(i, 1, _start)
        f = hid.shape[1]
        gu = jnp.dot(x_ref[...].astype(BF16), sgu[...], preferred_element_type=F32)
        hidden = (_silu(gu[:, :f]) * gu[:, f:]).astype(BF16)
        hid[...] = hidden
        n_rows = seg_walk(i, 0, _wait)
        lrow, w = lrow_ref[0], w_ref[...]
        acc = jnp.dot(hidden, sdn_ref[...], preferred_element_type=F32)
        for c in range(sure_chunks):
            rows = slice(c * ONEHOT_ROWS, (c + 1) * ONEHOT_ROWS)
            w_rows = _onehot_rows(c, lrow, w, tm)
            wt[rows, :] = w_rows
            acc = acc + contract_rows(w_rows, ybuf[0, rows, :])
        o_ref[:, 0:dh] = acc
        for c in range(sure_chunks, n_chunks):
            @pl.when(c * ONEHOT_ROWS < n_rows)
            def _():
                rows = slice(c * ONEHOT_ROWS, (c + 1) * ONEHOT_ROWS)
                w_rows = _onehot_rows(c, lrow, w, tm)
                wt[rows, :] = w_rows
                o_ref[:, 0:dh] += contract_rows(w_rows, ybuf[0, rows, :])

    @pl.when(half == 1)
    def _():
        @pl.when(i < last_tile)
        def _():
            seg_walk(i + 1, 0, _start)
        n_rows = seg_walk(i, 1, _wait)
        sure = slice(0, sure_chunks * ONEHOT_ROWS)
        o_ref[:, dh:2 * dh] = contract_rows(wt[sure, :], ybuf[1, sure, :]) \
            + jnp.dot(hid[...], sdn_ref[...], preferred_element_type=F32)
        for c in range(sure_chunks, n_chunks):
            @pl.when(c * ONEHOT_ROWS < n_rows)
            def _():
                rows = slice(c * ONEHOT_ROWS, (c + 1) * ONEHOT_ROWS)
                o_ref[:, dh:2 * dh] += contract_rows(wt[rows, :], ybuf[1, rows, :])
        o_ref[...] = _layer_norm(alpha * x_ref[...] + o_ref[...], lw_ref[...], lb_ref[...])


def _combine(x, ys, lrow, wts, gstart, size, s_gu, s_dn, ln_w, ln_b, *, alpha, tm):
    t, d = x.shape
    dh = d // 2
    f = s_dn.shape[0]
    const = lambda shape: pl.BlockSpec(shape, lambda i, h, *_: (0, 0))
    return pl.pallas_call(
        functools.partial(_combine_kernel, tm=tm, alpha=alpha),
        grid_spec=pltpu.PrefetchScalarGridSpec(
            num_scalar_prefetch=2,
            grid=(t // tm, 2),
            in_specs=[pl.BlockSpec((tm, d), lambda i, h, *_: (i, 0)),
                      pl.BlockSpec((1, TOP_K, tm), lambda i, h, *_: (i, 0, 0)),
                      pl.BlockSpec((TOP_K, tm), lambda i, h, *_: (0, i)),
                      pl.BlockSpec((f, dh), lambda i, h, *_: (0, h)),
                      const((1, d)), const((1, d)),
                      pl.BlockSpec(memory_space=pl.ANY), pl.BlockSpec(memory_space=pl.ANY)],
            out_specs=pl.BlockSpec((tm, d), lambda i, h, *_: (i, 0)),
            scratch_shapes=[pltpu.VMEM((2, _tile_rows(tm), dh), BF16), pltpu.VMEM((_tile_rows(tm), tm), BF16),
                            pltpu.VMEM((tm, f), BF16), pltpu.VMEM(s_gu.shape, BF16),
                            pltpu.SemaphoreType.DMA((2,)), pltpu.SemaphoreType.DMA(())],
        ),
        out_shape=jax.ShapeDtypeStruct((t, d), F32),
        compiler_params=_cp("arbitrary", "arbitrary"),
    )(gstart, size, x, lrow, wts, s_dn, ln_w, ln_b, s_gu, ys)


def _mixer_sublayer(x, w_in_all, layer, gate_b, pool_w, pool_scale, conv_w, mlstm_norm_w, ret_norm_w, w_branch,
                    w_out, ln_w, ln_b, *, batch, seq, alpha):
    t, d = x.shape
    width = d // N_BRANCH
    heads = width // HEAD_DIM
    gate_off = 8 * width
    ret_off = gate_off
    g_off = ret_off + 4 * width
    if_off = g_off + N_BRANCH * d
    w_bf16 = _realign_cast(w_in_all, layer, lo_col=gate_off, hi_col=if_off, shift=2 * heads, tr=512, tn=512)
    gate_bias = jnp.pad(gate_b, (0, LANES - 2 * heads)).reshape(1, LANES)

    proj_a = _matmul(x, w_bf16, tm=1024, tn=1024, ncols=gate_off)
    proj_b = _matmul(x, w_bf16, tm=1024, tn=1024, ncols=4 * width, col0=ret_off)
    gates = _matmul(x, w_bf16, tm=1024, tn=LANES, ncols=LANES, col0=if_off)
    y_pool, y_conv = _pool_conv(proj_a, pool_w.astype(BF16), pool_scale.reshape(1, width), conv_w,
                                seq=seq, width=width, tb=512)
    y_mlstm = _mlstm(proj_a, gates, gate_bias, mlstm_norm_w.reshape(1, width),
                     batch=batch, seq=seq, width=width, col0=4)
    y_ret = _retention(proj_b, ret_norm_w.reshape(1, width), batch=batch, seq=seq, width=width, col0=0)
    merged = _merge(x, (y_pool, y_conv, y_mlstm, y_ret), w_bf16, g_off, w_branch.astype(BF16),
                    tm=512, tn=512)
    return _proj_ln(merged, w_out.astype(BF16), x, ln_w, ln_b, alpha=alpha, tm=512)


def _xattn_sublayer(x, mem2d, wq, wk, wv, wo, ln_w, ln_b, *, seq, mem_len, alpha):
    w_kv = jnp.concatenate([wk, wv], axis=1).astype(BF16)
    kv = _matmul(mem2d, w_kv, tm=min(mem2d.shape[0], 1024), tn=512)
    return _xattn(x, kv, wq.astype(BF16), wo.astype(BF16), ln_w, ln_b,
                  alpha=alpha, seq=seq, mem_len=mem_len, tm=512)


def _moe_sublayer(x, router_w, router_b, w_gu, w_dn, layer, s_gu, s_dn, ln_w, ln_b, *, alpha):
    t, d = x.shape
    e_n, bm = N_EXPERTS, MOE_BM
    tm = ROUTE_TM
    n_blocks = -(-(t * TOP_K + (t // tm) * e_n * (SEG_ALIGN - 1)) // bm) + e_n
    wts, lrow, segtab, blk, seg = _route(x, router_w.T, router_b.reshape(e_n, 1), tm=tm, bm=bm, n_blocks=n_blocks)
    gstart, size = segtab[:, 0, :, 0].reshape(-1), segtab[:, 1, :, 0].reshape(-1)
    pend, pcnt = seg[0, :, 0], seg[1, :, 0]
    blk_e = blk[0, :n_blocks]
    nused = pend[e_n - 1] // bm

    xs = _dispatch(x, lrow, gstart, size, pend, pcnt, rows=n_blocks * bm, tm=tm, bm=bm)
    ys = _experts(xs, blk_e, nused.reshape(1), w_gu, w_dn, layer, bm=bm)
    return _combine(x, ys, lrow, wts, gstart, size, s_gu.astype(BF16), s_dn.astype(BF16), ln_w, ln_b,
                    alpha=alpha, tm=tm)


def kernel(x, mem, w_in, mlstm_gate_b, pool_w, pool_scale, conv_w, mlstm_norm_w, ret_norm_w, w_branch,
           w_mix_out, xa_wq, xa_wk, xa_wv, xa_wo, router_w, router_b, moe_w_gu, moe_w_dn, shared_w_gu,
           shared_w_dn, ln_w, ln_b):
    batch, seq, d = x.shape
    depth = w_in.shape[0]
    mem_len = mem.shape[1]
    alpha = (2 * depth) ** 0.25
    h = x.reshape(batch * seq, d)
    mem2d = mem.reshape(batch * mem_len, d)
    for l in range(depth):
        lw = ln_w[l].reshape(3, 1, d)
        lb = ln_b[l].reshape(3, 1, d)
        h = _mixer_sublayer(h, w_in, l, mlstm_gate_b[l], pool_w[l], pool_scale[l], conv_w[l], mlstm_norm_w[l],
                            ret_norm_w[l], w_branch[l], w_mix_out[l], lw[0], lb[0],
                            batch=batch, seq=seq, alpha=alpha)
        h = _xattn_sublayer(h, mem2d, xa_wq[l], xa_wk[l], xa_wv[l], xa_wo[l], lw[1], lb[1],
                            seq=seq, mem_len=mem_len, alpha=alpha)
        h = _moe_sublayer(h, router_w[l], router_b[l], moe_w_gu, moe_w_dn, l, shared_w_gu[l], shared_w_dn[l],
                          lw[2], lb[2], alpha=alpha)
    return h.reshape(batch, seq, d)
```

```python
import functools

import numpy as np
import jax
import jax.numpy as jnp
from jax import lax
from jax.experimental import pallas as pl
from jax.experimental.pallas import tpu as pltpu

F32 = jnp.float32
BF16 = jnp.bfloat16
I32 = jnp.int32

N_BRANCH = 4
HEAD_DIM = 128
POOL_WINDOWS = (2, 4, 8, 16)
CONV_WIDTH = 3
ROPE_BASE = 10000.0
XATTN_HEADS = 4
N_EXPERTS = 64
TOP_K = 8
N_GROUPS = 8
TOPK_GROUPS = 4
ROUTE_SCALE = 2.5
LN_EPS = 1e-5

LANES = 128
V7X_VMEM_BYTES = 64 * 1024 * 1024
VMEM_LIMIT = 56 * 1024 * 1024

SCAN_CHUNK = 256
SCAN_BLOCK = 512
HALO = 16
MOE_BM = 512
ROUTE_TM = 256
SEG_ALIGN = 16
ONEHOT_ROWS = 512


def _cp(*sem):
    return pltpu.CompilerParams(dimension_semantics=sem, vmem_limit_bytes=VMEM_LIMIT)


def _sigmoid(x):
    return 1.0 / (1.0 + jnp.exp(-x))


def _silu(x):
    return x * _sigmoid(x)


def _log_sigmoid(x):
    return jnp.minimum(x, 0.0) - jnp.log(1.0 + jnp.exp(-jnp.abs(x)))


def _layer_norm(z, w, b):
    mu = jnp.mean(z, axis=-1, keepdims=True)
    d = z - mu
    var = jnp.mean(d * d, axis=-1, keepdims=True)
    return d * lax.rsqrt(var + LN_EPS) * w + b


def _head_norm(h, w):
    mu = jnp.mean(h, axis=-1, keepdims=True)
    d = h - mu
    var = jnp.mean(d * d, axis=-1, keepdims=True)
    return d * lax.rsqrt(var + LN_EPS) * w


def _mm_kernel(x_ref, w_ref, o_ref, xb_ref):
    @pl.when(pl.program_id(1) == 0)
    def _():
        xb_ref[...] = x_ref[...].astype(BF16)

    o_ref[...] = jnp.dot(xb_ref[...], w_ref[...], preferred_element_type=F32)


def _realign_cast_kernel(a_ref, b_ref, o_ref, *, shift, lo, hi):
    j = pl.program_id(1)
    tn = o_ref.shape[1]
    shifted = jnp.logical_and(j >= lo, j < hi)

    @pl.when(shifted)
    def _():
        both = jnp.concatenate([a_ref[...], b_ref[...]], axis=1)
        o_ref[...] = both[:, shift:shift + tn].astype(BF16)

    @pl.when(jnp.logical_not(shifted))
    def _():
        o_ref[...] = a_ref[...].astype(BF16)


def _realign_cast(w_all, layer, *, lo_col, hi_col, shift, tr, tn):
    _, rows, _ = w_all.shape
    lo, hi = lo_col // tn, hi_col // tn
    src = lambda j: jnp.where(j == hi, lo, j)
    return pl.pallas_call(
        functools.partial(_realign_cast_kernel, shift=shift, lo=lo, hi=hi),
        grid=(rows // tr, hi + 1),
        in_specs=[pl.BlockSpec((None, tr, tn), lambda i, j: (layer, i, src(j))),
                  pl.BlockSpec((None, tr, LANES), lambda i, j: (layer, i, (src(j) + 1) * (tn // LANES)))],
        out_specs=pl.BlockSpec((tr, tn), lambda i, j: (i, j)),
        out_shape=jax.ShapeDtypeStruct((rows, hi_col + tn), BF16),
        compiler_params=_cp("parallel", "parallel"),
    )(w_all, w_all)


def _matmul(x, w, *, tm, tn, ncols=None, col0=0):
    t, k = x.shape
    n = w.shape[1] if ncols is None else ncols
    return pl.pallas_call(
        _mm_kernel,
        grid=(t // tm, n // tn),
        in_specs=[pl.BlockSpec((tm, k), lambda i, j: (i, 0)),
                  pl.BlockSpec((k, tn), lambda i, j: (0, col0 // tn + j))],
        out_specs=pl.BlockSpec((tm, tn), lambda i, j: (i, j)),
        out_shape=jax.ShapeDtypeStruct((t, n), F32),
        scratch_shapes=[pltpu.VMEM((tm, k), BF16)],
        compiler_params=_cp("parallel", "arbitrary"),
    )(x, w)


def _poolconv_kernel(u_ref, uh_ref, h_ref, hh_ref, b_ref, c_ref, ch_ref, pw_ref, ps_ref, cw_ref,
                     yp_ref, yc_ref, ubuf, zbuf, *, tb, n_tblk):
    first = (pl.program_id(0) % n_tblk) == 0
    ubuf[0:HALO, :] = jnp.where(first, 0.0, uh_ref[...])
    ubuf[HALO:HALO + tb, :] = u_ref[...]
    zbuf[0:HALO, :] = jnp.where(first, 0.0, ch_ref[...] * hh_ref[...])
    zbuf[HALO:HALO + tb, :] = c_ref[...] * h_ref[...]

    t_pos = (pl.program_id(0) % n_tblk) * tb + lax.broadcasted_iota(I32, (tb, LANES), 0)
    gw = u_ref.shape[1] // len(POOL_WINDOWS)
    for grp, win in enumerate(POOL_WINDOWS):
        lanes = slice(grp * gw, (grp + 1) * gw)
        cur = ubuf[HALO:HALO + tb, lanes]
        acc = cur
        for lag in range(1, win):
            acc = acc + ubuf[HALO - lag:HALO - lag + tb, lanes]
        count = jnp.minimum(t_pos + 1, win).astype(F32)
        mixed = acc / count - cur
        y = jnp.dot(mixed.astype(BF16), pw_ref[grp], preferred_element_type=F32)
        yp_ref[:, lanes] = y * ps_ref[:, lanes]

    conv = cw_ref[0:1, :] * zbuf[HALO:HALO + tb, :]
    for lag in range(1, CONV_WIDTH):
        conv = conv + cw_ref[lag:lag + 1, :] * zbuf[HALO - lag:HALO - lag + tb, :]
    yc_ref[...] = b_ref[...] * conv


def _pool_conv(proj, pool_w, pool_scale, conv_w, *, seq, width, tb):
    t = proj.shape[0]
    n_tblk = seq // tb
    ratio = tb // HALO

    def cur(col):
        return pl.BlockSpec((tb, width), lambda g: (g, col))

    def halo(col):
        return pl.BlockSpec((HALO, width), lambda g: (jnp.maximum(g * ratio - 1, 0), col))

    full = lambda shape: pl.BlockSpec(shape, lambda g: (0,) * len(shape))
    return pl.pallas_call(
        functools.partial(_poolconv_kernel, tb=tb, n_tblk=n_tblk),
        grid=(t // tb,),
        in_specs=[cur(0), halo(0), cur(1), halo(1), cur(2), cur(3), halo(3),
                  full(pool_w.shape), full(pool_scale.shape), full(conv_w.shape)],
        out_specs=[pl.BlockSpec((tb, width), lambda g: (g, 0))] * 2,
        out_shape=[jax.ShapeDtypeStruct((t, width), F32)] * 2,
        scratch_shapes=[pltpu.VMEM((HALO + tb, width), F32)] * 2,
        compiler_params=_cp("parallel"),
    )(proj, proj, proj, proj, proj, proj, proj, pool_w, pool_scale, conv_w)


def _cum_rows(x, op, fill):
    n = x.shape[0]
    row = lax.broadcasted_iota(I32, x.shape, 0)
    shift = 1
    while shift < n:
        x = op(x, jnp.where(row >= shift, pltpu.roll(x, shift, axis=0), fill))
        shift *= 2
    return x


def _mlstm_kernel(q_ref, k_ref, v_ref, o_ref, g_ref, gb_ref, nw_ref, y_ref, c_scr, n_scr, m_scr,
                  *, chunk, heads):
    @pl.when(pl.program_id(1) == 0)
    def _():
        c_scr[...] = jnp.zeros_like(c_scr)
        n_scr[...] = jnp.zeros_like(n_scr)
        m_scr[...] = jnp.zeros_like(m_scr)

    tb = q_ref.shape[0]
    hd = HEAD_DIM
    scale = hd ** -0.5
    tri = (lax.broadcasted_iota(I32, (chunk, chunk), 0) >= lax.broadcasted_iota(I32, (chunk, chunk), 1))
    for c in range(tb // chunk):
        rows = slice(c * chunk, (c + 1) * chunk)
        gates = g_ref[rows, :] + gb_ref[...]
        lf = _log_sigmoid(pltpu.roll(gates, LANES - heads, axis=1))
        cumf = _cum_rows(lf, jnp.add, 0.0)
        a = gates - cumf
        m_prev = m_scr[...]
        mu = jnp.maximum(_cum_rows(a, jnp.maximum, -jnp.inf), m_prev)
        mu_last = mu[chunk - 1:chunk, :]
        a_t = a.T
        for h in range(heads):
            cols = slice(h * hd, (h + 1) * hd)
            q = q_ref[rows, cols]
            k = k_ref[rows, cols] * scale
            v = v_ref[rows, cols]
            qb, kb, vb = q.astype(BF16), k.astype(BF16), v.astype(BF16)
            mu_col = mu[:, h:h + 1]
            a_col = a[:, h:h + 1]
            m_prev_h = m_prev[:, h:h + 1]
            mu_last_h = mu_last[:, h:h + 1]
            dmat = jnp.exp(jnp.where(tri, a_t[h:h + 1, :] - mu_col, -jnp.inf))
            s = lax.dot_general(qb, kb, (((1,), (1,)), ((), ())), preferred_element_type=F32)
            p = dmat * s
            inter = jnp.exp(m_prev_h - mu_col)
            c_h = c_scr[h]
            n_h = n_scr[h]
            num = inter * jnp.dot(qb, c_h.astype(BF16), preferred_element_type=F32) \
                + jnp.dot(p.astype(BF16), vb, preferred_element_type=F32)
            den = inter * jnp.sum(q * n_h, axis=-1, keepdims=True) + jnp.sum(p, axis=-1, keepdims=True)
            floor = jnp.exp(-(cumf[:, h:h + 1] + mu_col))
            h_out = num / jnp.maximum(jnp.abs(den), floor)

            wg = jnp.exp(a_col - mu_last_h)
            decay = jnp.exp(m_prev_h - mu_last_h)
            kw = k * wg
            c_scr[h] = decay * c_h + lax.dot_general(kw.astype(BF16), vb, (((0,), (0,)), ((), ())),
                                                     preferred_element_type=F32)
            n_scr[h] = decay * n_h + jnp.sum(kw, axis=0, keepdims=True)

            gated = _sigmoid(o_ref[rows, cols]) * h_out
            y_ref[rows, cols] = _head_norm(gated, nw_ref[:, cols])
        m_scr[...] = cumf[chunk - 1:chunk, :] + mu_last


def _mlstm(proj, gates, gate_b, norm_w, *, batch, seq, width, col0):
    t = proj.shape[0]
    heads = width // HEAD_DIM
    n_tblk = seq // SCAN_BLOCK
    blk = lambda col: pl.BlockSpec((SCAN_BLOCK, width), lambda b, i: (b * n_tblk + i, col))
    return pl.pallas_call(
        functools.partial(_mlstm_kernel, chunk=SCAN_CHUNK, heads=heads),
        grid=(batch, n_tblk),
        in_specs=[blk(col0), blk(col0 + 1), blk(col0 + 2), blk(col0 + 3),
                  pl.BlockSpec((SCAN_BLOCK, LANES), lambda b, i: (b * n_tblk + i, 0)),
                  pl.BlockSpec((1, LANES), lambda b, i: (0, 0)),
                  pl.BlockSpec((1, width), lambda b, i: (0, 0))],
        out_specs=pl.BlockSpec((SCAN_BLOCK, width), lambda b, i: (b * n_tblk + i, 0)),
        out_shape=jax.ShapeDtypeStruct((t, width), F32),
        scratch_shapes=[pltpu.VMEM((heads, HEAD_DIM, HEAD_DIM), F32),
                        pltpu.VMEM((heads, 1, HEAD_DIM), F32),
                        pltpu.VMEM((1, LANES), F32)],
        compiler_params=_cp("parallel", "arbitrary"),
    )(proj, proj, proj, proj, gates, gate_b, norm_w)


def _ret_kernel(q_ref, k_ref, v_ref, g_ref, cos_ref, sin_ref, intra_ref, cross_ref, zeta_ref, nw_ref,
                y_ref, r_scr, *, chunk, heads, chunk_decay):
    @pl.when(pl.program_id(1) == 0)
    def _():
        r_scr[...] = jnp.zeros_like(r_scr)

    tb = q_ref.shape[0]
    hd = HEAD_DIM
    scale = hd ** -0.5
    for c in range(tb // chunk):
        rows = slice(c * chunk, (c + 1) * chunk)
        cos = cos_ref[rows, :]
        sin = sin_ref[rows, :]
        for h in range(heads):
            cols = slice(h * hd, (h + 1) * hd)
            q = q_ref[rows, cols]
            k = k_ref[rows, cols]
            q = q * cos + pltpu.roll(q, hd // 2, axis=1) * sin
            k = (k * cos + pltpu.roll(k, hd // 2, axis=1) * sin) * scale
            vb = v_ref[rows, cols].astype(BF16)
            qb = q.astype(BF16)
            s = lax.dot_general(qb, k.astype(BF16), (((1,), (1,)), ((), ())), preferred_element_type=F32)
            inner = jnp.dot((s * intra_ref[h]).astype(BF16), vb, preferred_element_type=F32)
            r_h = r_scr[h]
            crs = jnp.dot(qb, r_h.astype(BF16), preferred_element_type=F32) * cross_ref[h]
            r_scr[h] = chunk_decay[h] * r_h + lax.dot_general(
                (k * zeta_ref[h]).astype(BF16), vb, (((0,), (0,)), ((), ())), preferred_element_type=F32)
            y_ref[rows, cols] = _silu(g_ref[rows, cols]) * _head_norm(inner + crs, nw_ref[:, cols])


def _retention(proj, norm_w, *, batch, seq, width, col0):
    t = proj.shape[0]
    heads = width // HEAD_DIM
    chunk = SCAN_CHUNK
    n_tblk = seq // SCAN_BLOCK
    half = HEAD_DIM // 2
    inv = np.float32(ROPE_BASE) ** (-np.arange(half, dtype=np.float32) / np.float32(half))
    ang = (np.arange(seq, dtype=np.float32)[:, None] * inv[None, :]).astype(np.float64)
    cos_t = jnp.asarray(np.concatenate([np.cos(ang), np.cos(ang)], axis=-1), F32)
    sin_t = jnp.asarray(np.concatenate([-np.sin(ang), np.sin(ang)], axis=-1), F32)
    log_g = jnp.log(1.0 - 2.0 ** (-5.0 - jnp.arange(heads, dtype=F32)))
    tt = jnp.arange(chunk, dtype=F32)
    lag = tt[:, None] - tt[None, :]
    intra = jnp.where(lag >= 0, jnp.exp(jnp.maximum(lag, 0.0)[None] * log_g[:, None, None]), 0.0)
    cross = jnp.broadcast_to(jnp.exp((tt + 1.0)[None, :] * log_g[:, None])[:, :, None], (heads, chunk, HEAD_DIM))
    zeta = jnp.broadcast_to(jnp.exp((chunk - 1.0 - tt)[None, :] * log_g[:, None])[:, :, None],
                            (heads, chunk, HEAD_DIM))
    chunk_decay = tuple(float((1.0 - 2.0 ** (-5.0 - h)) ** chunk) for h in range(heads))

    blk = lambda col: pl.BlockSpec((SCAN_BLOCK, width), lambda b, i: (b * n_tblk + i, col))
    pos = pl.BlockSpec((SCAN_BLOCK, HEAD_DIM), lambda b, i: (i, 0))
    full3 = lambda a: pl.BlockSpec(a.shape, lambda b, i: (0, 0, 0))
    return pl.pallas_call(
        functools.partial(_ret_kernel, chunk=chunk, heads=heads, chunk_decay=chunk_decay),
        grid=(batch, n_tblk),
        in_specs=[blk(col0), blk(col0 + 1), blk(col0 + 2), blk(col0 + 3), pos, pos,
                  full3(intra), full3(cross), full3(zeta),
                  pl.BlockSpec((1, width), lambda b, i: (0, 0))],
        out_specs=pl.BlockSpec((SCAN_BLOCK, width), lambda b, i: (b * n_tblk + i, 0)),
        out_shape=jax.ShapeDtypeStruct((t, width), F32),
        scratch_shapes=[pltpu.VMEM((heads, HEAD_DIM, HEAD_DIM), F32)],
        compiler_params=_cp("parallel", "arbitrary"),
    )(proj, proj, proj, proj, cos_t, sin_t, intra, cross, zeta, norm_w)


def _merge_kernel(x_ref, *refs):
    y_refs, wg_refs = refs[:N_BRANCH], refs[N_BRANCH:2 * N_BRANCH]
    wb_ref, o_ref, xb_ref = refs[2 * N_BRANCH:]

    @pl.when(pl.program_id(1) == 0)
    def _():
        xb_ref[...] = x_ref[...].astype(BF16)

    xb = xb_ref[...]
    acc = None
    for n in range(N_BRANCH):
        gate = _sigmoid(jnp.dot(xb, wg_refs[n][...], preferred_element_type=F32))
        term = gate * jnp.dot(y_refs[n][...].astype(BF16), wb_ref[n], preferred_element_type=F32)
        acc = term if acc is None else acc + term
    o_ref[...] = acc


def _merge(x, branches, w_gate, gate_col0, w_branch, *, tm, tn):
    t, d = x.shape
    width = branches[0].shape[1]
    ybs = pl.BlockSpec((tm, width), lambda i, j: (i, 0))

    def gate_spec(n):
        return pl.BlockSpec((d, tn), lambda i, j: (0, (gate_col0 + n * d) // tn + j))

    return pl.pallas_call(
        _merge_kernel,
        grid=(t // tm, d // tn),
        in_specs=[pl.BlockSpec((tm, d), lambda i, j: (i, 0))] + [ybs] * N_BRANCH
                 + [gate_spec(n) for n in range(N_BRANCH)]
                 + [pl.BlockSpec((N_BRANCH, width, tn), lambda i, j: (0, 0, j))],
        out_specs=pl.BlockSpec((tm, tn), lambda i, j: (i, j)),
        out_shape=jax.ShapeDtypeStruct((t, d), F32),
        scratch_shapes=[pltpu.VMEM((tm, d), BF16)],
        compiler_params=_cp("parallel", "arbitrary"),
    )(x, *branches, *([w_gate] * N_BRANCH), w_branch)


def _proj_ln_kernel(a_ref, w_ref, r_ref, lw_ref, lb_ref, o_ref, *, alpha):
    y = jnp.dot(a_ref[...].astype(BF16), w_ref[...], preferred_element_type=F32)
    o_ref[...] = _layer_norm(alpha * r_ref[...] + y, lw_ref[...], lb_ref[...])


def _proj_ln(a, w, resid, ln_w, ln_b, *, alpha, tm):
    t, k = a.shape
    d = w.shape[1]
    row = lambda n: pl.BlockSpec((tm, n), lambda i: (i, 0))
    const = lambda shape: pl.BlockSpec(shape, lambda i: (0, 0))
    return pl.pallas_call(
        functools.partial(_proj_ln_kernel, alpha=alpha),
        grid=(t // tm,),
        in_specs=[row(k), const((k, d)), row(d), const((1, d)), const((1, d))],
        out_specs=row(d),
        out_shape=jax.ShapeDtypeStruct((t, d), F32),
        compiler_params=_cp("parallel"),
    )(a, w, resid, ln_w, ln_b)


def _xattn_kernel(x_ref, wq_ref, kv_ref, wo_ref, lw_ref, lb_ref, o_ref, *, alpha, heads):
    x = x_ref[...]
    hd = HEAD_DIM
    inner = heads * hd
    q = jnp.dot(x.astype(BF16), wq_ref[...], preferred_element_type=F32)
    outs = []
    for h in range(heads):
        qh = q[:, h * hd:(h + 1) * hd].astype(BF16)
        kh = kv_ref[:, h * hd:(h + 1) * hd].astype(BF16)
        vh = kv_ref[:, inner + h * hd:inner + (h + 1) * hd].astype(BF16)
        s = lax.dot_general(qh, kh, (((1,), (1,)), ((), ())), preferred_element_type=F32) * hd ** -0.5
        s = s - jnp.max(s, axis=-1, keepdims=True)
        e = jnp.exp(s)
        p = e / jnp.sum(e, axis=-1, keepdims=True)
        outs.append(jnp.dot(p.astype(BF16), vh, preferred_element_type=F32).astype(BF16))
    o = jnp.concatenate(outs, axis=-1)
    y = jnp.dot(o, wo_ref[...], preferred_element_type=F32)
    o_ref[...] = _layer_norm(alpha * x + y, lw_ref[...], lb_ref[...])


def _xattn(x, kv, wq, wo, ln_w, ln_b, *, alpha, seq, mem_len, tm):
    t, d = x.shape
    inner = wq.shape[1]
    n_tblk = seq // tm
    const = lambda shape: pl.BlockSpec(shape, lambda i: (0, 0))
    return pl.pallas_call(
        functools.partial(_xattn_kernel, alpha=alpha, heads=XATTN_HEADS),
        grid=(t // tm,),
        in_specs=[pl.BlockSpec((tm, d), lambda i: (i, 0)), const((d, inner)),
                  pl.BlockSpec((mem_len, 2 * inner), lambda i: (i // n_tblk, 0)),
                  const((inner, d)), const((1, d)), const((1, d))],
        out_specs=pl.BlockSpec((tm, d), lambda i: (i, 0)),
        out_shape=jax.ShapeDtypeStruct((t, d), F32),
        compiler_params=_cp("parallel"),
    )(x, wq, kv, wo, ln_w, ln_b)


def _round_up_pow2(x, m):
    shift = m.bit_length() - 1
    return jnp.left_shift(jnp.right_shift(x + (m - 1), shift), shift)


def _route_kernel(x_ref, wr_ref, rb_ref, w_ref, lrow_ref, segtab_ref, seg_ref, size_all, tot, *, tm):
    step = pl.program_id(0)

    @pl.when(step == 0)
    def _():
        tot[...] = jnp.zeros_like(tot)

    e_n, g_n = N_EXPERTS, N_GROUPS
    per = e_n // g_n
    logits = lax.dot_general(wr_ref[...], x_ref[...], (((1,), (1,)), ((), ())),
                             precision=lax.Precision.HIGHEST, preferred_element_type=F32)
    scores = _sigmoid(logits)
    biased = scores + rb_ref[...]
    b3 = biased.reshape(g_n, per, tm)
    member = lax.broadcasted_iota(I32, (g_n, per, tm), 1)
    top1 = jnp.max(b3, axis=1, keepdims=True)
    first = jnp.min(jnp.where(b3 == top1, member, per), axis=1, keepdims=True)
    top2 = jnp.max(jnp.where(member == first, -jnp.inf, b3), axis=1, keepdims=True)
    gs = top1 + top2
    gid = lax.broadcasted_iota(I32, (g_n, 1, tm), 0)
    rank = jnp.zeros((g_n, 1, tm), I32)
    for other in range(g_n):
        o = gs[other:other + 1]
        ahead = jnp.logical_or(o > gs, jnp.logical_and(o == gs, other < gid))
        rank = rank + jnp.where(ahead, 1, 0)
    cur = jnp.where(rank < TOPK_GROUPS, b3, -jnp.inf).reshape(e_n, tm)

    eid = lax.broadcasted_iota(I32, (e_n, tm), 0)
    picks, vals = [], []
    sel = jnp.zeros((e_n, tm), F32)
    for k in range(TOP_K):
        mx = jnp.max(cur, axis=0, keepdims=True)
        ik = jnp.min(jnp.where(cur == mx, eid, e_n), axis=0, keepdims=True)
        hit = eid == ik
        vals.append(jnp.sum(jnp.where(hit, scores, 0.0), axis=0, keepdims=True))
        cur = jnp.where(hit, -jnp.inf, cur)
        sel = jnp.where(hit, 1.0, sel)
        picks.append(ik)
    total = vals[0]
    for v in vals[1:]:
        total = total + v

    tri = jnp.where(lax.broadcasted_iota(I32, (tm, tm), 0) <= lax.broadcasted_iota(I32, (tm, tm), 1), 1.0, 0.0)
    incl = jnp.dot(sel.astype(BF16), tri.astype(BF16), preferred_element_type=F32)
    size = _round_up_pow2(jnp.broadcast_to(incl[:, tm - 1:tm], (e_n, LANES)).astype(I32), SEG_ALIGN)
    loff = _cum_rows(size, jnp.add, 0) - size
    base = loff[:, 0:1].astype(F32) + incl - 1.0
    for k in range(TOP_K):
        w_ref[k:k + 1, :] = vals[k] / total * ROUTE_SCALE
        lrow_ref[0, k:k + 1, :] = jnp.sum(jnp.where(eid == picks[k], base, 0.0),
                                          axis=0, keepdims=True).astype(I32)
    size_all[step] = size
    tot[...] = tot[...] + size

    @pl.when(step == pl.num_programs(0) - 1)
    def _():
        rows = tot[...]
        start = _cum_rows(rows, jnp.add, 0) - rows

        def tile_seg(i, run):
            segtab_ref[i, 0] = run
            segtab_ref[i, 1] = size_all[i]
            return run + size_all[i]

        lax.fori_loop(0, pl.num_programs(0), tile_seg, start)
        seg_ref[0] = start
        seg_ref[1] = rows


def _route(x, router_w_t, router_b, *, tm):
    t, d = x.shape
    e_n = N_EXPERTS
    n_t = t // tm
    return pl.pallas_call(
        functools.partial(_route_kernel, tm=tm),
        grid=(n_t,),
        in_specs=[pl.BlockSpec((tm, d), lambda i: (i, 0)),
                  pl.BlockSpec((e_n, d), lambda i: (0, 0)),
                  pl.BlockSpec((e_n, 1), lambda i: (0, 0))],
        out_specs=[pl.BlockSpec((TOP_K, tm), lambda i: (0, i)),
                   pl.BlockSpec((1, TOP_K, tm), lambda i: (i, 0, 0)),
                   pl.BlockSpec((n_t, 2, e_n, LANES), lambda i: (0, 0, 0, 0)),
                   pl.BlockSpec((2, e_n, LANES), lambda i: (0, 0, 0))],
        out_shape=[jax.ShapeDtypeStruct((TOP_K, t), F32), jax.ShapeDtypeStruct((n_t, TOP_K, tm), I32),
                   jax.ShapeDtypeStruct((n_t, 2, e_n, LANES), I32), jax.ShapeDtypeStruct((2, e_n, LANES), I32)],
        scratch_shapes=[pltpu.VMEM((n_t, e_n, LANES), I32), pltpu.VMEM((e_n, LANES), I32)],
        compiler_params=_cp("arbitrary"),
    )(x, router_w_t, router_b)


def _tile_rows(tm):
    worst = TOP_K * tm + N_EXPERTS * (SEG_ALIGN - 1)
    return -(-worst // ONEHOT_ROWS) * ONEHOT_ROWS


def _onehot_rows(chunk, lrow, values, tm):
    rid = chunk * ONEHOT_ROWS + lax.broadcasted_iota(I32, (ONEHOT_ROWS, tm), 0)
    acc = jnp.zeros((ONEHOT_ROWS, tm), F32)
    for k in range(TOP_K):
        acc = jnp.where(rid == lrow[k:k + 1, :], 1.0 if values is None else values[k:k + 1, :], acc)
    return acc.astype(BF16)


def _segments(gstart_ref, size_ref, tile, make_copy, act, keep=None):
    def body(e, loff):
        n = size_ref[tile * N_EXPERTS + e]
        wanted = n > 0 if keep is None else jnp.logical_and(n > 0, keep(loff + n))

        @pl.when(wanted)
        def _():
            act(make_copy(pl.multiple_of(gstart_ref[tile * N_EXPERTS + e], SEG_ALIGN),
                          pl.multiple_of(loff, SEG_ALIGN), pl.multiple_of(n, SEG_ALIGN)))
        return loff + n
    return lax.fori_loop(0, N_EXPERTS, body, 0)


def _start(cp):
    cp.start()


def _wait(cp):
    cp.wait()


def _dispatch_kernel(gstart_ref, size_ref, x_ref, lrow_ref, xs_ref, stage, sem_a, sem_b, *, tm):
    i = pl.program_id(0)
    last = pl.num_programs(0) - 1
    n_chunks = stage.shape[0] // ONEHOT_ROWS
    split_chunk = n_chunks // 2
    split = split_chunk * ONEHOT_ROWS
    assert tm <= ONEHOT_ROWS

    def seg_walk(tile, act, phase_b):
        sem = sem_b if phase_b else sem_a
        return _segments(
            gstart_ref, size_ref, tile,
            lambda g, loff, n: pltpu.make_async_copy(stage.at[pl.ds(loff, n), :], xs_ref.at[pl.ds(g, n), :], sem),
            act, (lambda end: end > split) if phase_b else (lambda end: end <= split))

    xb = x_ref[...].astype(BF16)
    lrow = lrow_ref[0]
    n_rows = lax.fori_loop(0, N_EXPERTS, lambda e, s: s + size_ref[i * N_EXPERTS + e], 0)

    def chunks(lo, hi):
        for c in range(lo, hi):
            def one(c=c):
                stage[c * ONEHOT_ROWS:(c + 1) * ONEHOT_ROWS, :] = jnp.dot(
                    _onehot_rows(c, lrow, None, tm), xb, preferred_element_type=F32).astype(BF16)
            if c * ONEHOT_ROWS < TOP_K * tm:
                one()
            else:
                pl.when(c * ONEHOT_ROWS < n_rows)(one)

    prev = jnp.maximum(i - 1, 0)

    @pl.when(i > 0)
    def _():
        seg_walk(prev, _wait, False)
    chunks(0, split_chunk - 1)

    @pl.when(i > 0)
    def _():
        seg_walk(prev, _wait, True)
    chunks(split_chunk - 1, split_chunk)
    seg_walk(i, _start, False)
    chunks(split_chunk, n_chunks)
    seg_walk(i, _start, True)

    @pl.when(i == last)
    def _():
        seg_walk(i, _wait, False)
        seg_walk(i, _wait, True)


def _dispatch(x, lrow, gstart, size, *, rows, tm):
    t, d = x.shape
    return pl.pallas_call(
        functools.partial(_dispatch_kernel, tm=tm),
        grid_spec=pltpu.PrefetchScalarGridSpec(
            num_scalar_prefetch=2,
            grid=(t // tm,),
            in_specs=[pl.BlockSpec((tm, d), lambda i, *_: (i, 0)),
                      pl.BlockSpec((1, TOP_K, tm), lambda i, *_: (i, 0, 0))],
            out_specs=pl.BlockSpec(memory_space=pl.ANY),
            scratch_shapes=[pltpu.VMEM((_tile_rows(tm), d), BF16)] + [pltpu.SemaphoreType.DMA(())] * 2,
        ),
        out_shape=jax.ShapeDtypeStruct((rows, d), BF16),
        compiler_params=_cp("arbitrary"),
    )(gstart, size, x, lrow)


def _expert_kernel(start_ref, rows_ref, wgu_ref, wdn_ref, xs_ref, ys_ref, wgu_b, wdn_b, xbuf, ybuf, sem_in, sem_out,
                   *, bm):
    e = pl.program_id(0)
    row0 = start_ref[e]
    n = rows_ref[e]
    n_blk = (n + bm - 1) // bm

    @pl.when(e == 0)
    def _():
        xbuf[...] = jnp.zeros_like(xbuf)

    wgu_b[...] = wgu_ref[...].astype(BF16)
    wdn_b[...] = wdn_ref[...].astype(BF16)

    def rows_of(b):
        return pl.multiple_of(jnp.minimum(bm, n - b * bm), SEG_ALIGN)

    def in_copy(b, slot):
        r = rows_of(b)
        src = xs_ref.at[pl.ds(pl.multiple_of(row0 + b * bm, SEG_ALIGN), r), :]
        return pltpu.make_async_copy(src, xbuf.at[slot, pl.ds(0, r), :], sem_in.at[slot])

    def out_copy(b, slot):
        r = rows_of(b)
        dst = ys_ref.at[pl.ds(pl.multiple_of(row0 + b * bm, SEG_ALIGN), r), :]
        return pltpu.make_async_copy(ybuf.at[slot, pl.ds(0, r), :], dst, sem_out.at[slot])

    @pl.when(n_blk > 0)
    def _():
        in_copy(0, 0).start()

    def block(b, carry):
        slot = b % 2

        @pl.when(b + 1 < n_blk)
        def _():
            in_copy(b + 1, 1 - slot).start()

        in_copy(b, slot).wait()

        @pl.when(b >= 2)
        def _():
            out_copy(b - 2, slot).wait()

        f = wdn_b.shape[0]
        gu = jnp.dot(xbuf[slot], wgu_b[...], preferred_element_type=F32)
        hidden = (_silu(gu[:, :f]) * gu[:, f:]).astype(BF16)
        ybuf[slot] = jnp.dot(hidden, wdn_b[...], preferred_element_type=F32).astype(BF16)
        out_copy(b, slot).start()
        return carry

    lax.fori_loop(0, n_blk, block, 0)

    for back in (2, 1):
        @pl.when(n_blk >= back)
        def _():
            b = n_blk - back
            out_copy(b, b % 2).wait()


def _experts(xs, start, rows, w_gu, w_dn, layer, *, bm):
    total, d = xs.shape
    f2 = w_gu.shape[3]
    f = w_dn.shape[2]
    return pl.pallas_call(
        functools.partial(_expert_kernel, bm=bm),
        grid_spec=pltpu.PrefetchScalarGridSpec(
            num_scalar_prefetch=2,
            grid=(N_EXPERTS,),
            in_specs=[pl.BlockSpec((None, None, d, f2), lambda e, *_: (layer, e, 0, 0)),
                      pl.BlockSpec((None, None, f, d), lambda e, *_: (layer, e, 0, 0)),
                      pl.BlockSpec(memory_space=pl.ANY)],
            out_specs=pl.BlockSpec(memory_space=pl.ANY),
            scratch_shapes=[pltpu.VMEM((d, f2), BF16), pltpu.VMEM((f, d), BF16),
                            pltpu.VMEM((2, bm, d), BF16), pltpu.VMEM((2, bm, d), BF16),
                            pltpu.SemaphoreType.DMA((2,)), pltpu.SemaphoreType.DMA((2,))],
        ),
        out_shape=jax.ShapeDtypeStruct((total, d), BF16),
        compiler_params=_cp("arbitrary"),
    )(start, rows, w_gu, w_dn, xs)


def _combine_kernel(gstart_ref, size_ref, x_ref, lrow_ref, w_ref, sdn_ref, lw_ref, lb_ref, sgu_hbm, ys_ref,
                    o_ref, ybuf, wt, hid, sgu, sems, wsem, *, tm, alpha):
    i, half = pl.program_id(0), pl.program_id(1)
    last_tile = pl.num_programs(0) - 1
    dh = ybuf.shape[2]
    n_chunks = ybuf.shape[1] // ONEHOT_ROWS
    sure_chunks = (TOP_K * tm) // ONEHOT_ROWS

    def contract_rows(w_rows, y_rows):
        return lax.dot_general(w_rows, y_rows, (((0,), (0,)), ((), ())), preferred_element_type=F32)

    def seg_walk(tile, hf, act):
        return _segments(
            gstart_ref, size_ref, tile,
            lambda g, loff, n: pltpu.make_async_copy(ys_ref.at[pl.ds(g, n), hf * dh:(hf + 1) * dh],
                                                     ybuf.at[hf, pl.ds(loff, n), :], sems.at[hf]),
            act)

    @pl.when(jnp.logical_and(i == 0, half == 0))
    def _():
        ybuf[...] = jnp.zeros_like(ybuf)
        weights = pltpu.make_async_copy(sgu_hbm, sgu, wsem)
        weights.start()
        weights.wait()
        seg_walk(0, 0, _start)

    @pl.when(half == 0)
    def _():
        seg_walk(i, 1, _start)
        f = hid.shape[1]
        gu = jnp.dot(x_ref[...].astype(BF16), sgu[...], preferred_element_type=F32)
        hidden = (_silu(gu[:, :f]) * gu[:, f:]).astype(BF16)
        hid[...] = hidden
        n_rows = seg_walk(i, 0, _wait)
        lrow, w = lrow_ref[0], w_ref[...]
        acc = jnp.dot(hidden, sdn_ref[...], preferred_element_type=F32)
        for c in range(sure_chunks):
            rows = slice(c * ONEHOT_ROWS, (c + 1) * ONEHOT_ROWS)
            w_rows = _onehot_rows(c, lrow, w, tm)
            wt[rows, :] = w_rows
            acc = acc + contract_rows(w_rows, ybuf[0, rows, :])
        o_ref[:, 0:dh] = acc
        for c in range(sure_chunks, n_chunks):
            @pl.when(c * ONEHOT_ROWS < n_rows)
            def _():
                rows = slice(c * ONEHOT_ROWS, (c + 1) * ONEHOT_ROWS)
                w_rows = _onehot_rows(c, lrow, w, tm)
                wt[rows, :] = w_rows
                o_ref[:, 0:dh] += contract_rows(w_rows, ybuf[0, rows, :])

    @pl.when(half == 1)
    def _():
        @pl.when(i < last_tile)
        def _():
            seg_walk(i + 1, 0, _start)
        n_rows = seg_walk(i, 1, _wait)
        sure = slice(0, sure_chunks * ONEHOT_ROWS)
        o_ref[:, dh:2 * dh] = contract_rows(wt[sure, :], ybuf[1, sure, :]) \
            + jnp.dot(hid[...], sdn_ref[...], preferred_element_type=F32)
        for c in range(sure_chunks, n_chunks):
            @pl.when(c * ONEHOT_ROWS < n_rows)
            def _():
                rows = slice(c * ONEHOT_ROWS, (c + 1) * ONEHOT_ROWS)
                o_ref[:, dh:2 * dh] += contract_rows(wt[rows, :], ybuf[1, rows, :])
        o_ref[...] = _layer_norm(alpha * x_ref[...] + o_ref[...], lw_ref[...], lb_ref[...])


def _combine(x, ys, lrow, wts, gstart, size, s_gu, s_dn, ln_w, ln_b, *, alpha, tm):
    t, d = x.shape
    dh = d // 2
    f = s_dn.shape[0]
    const = lambda shape: pl.BlockSpec(shape, lambda i, h, *_: (0, 0))
    return pl.pallas_call(
        functools.partial(_combine_kernel, tm=tm, alpha=alpha),
        grid_spec=pltpu.PrefetchScalarGridSpec(
            num_scalar_prefetch=2,
            grid=(t // tm, 2),
            in_specs=[pl.BlockSpec((tm, d), lambda i, h, *_: (i, 0)),
                      pl.BlockSpec((1, TOP_K, tm), lambda i, h, *_: (i, 0, 0)),
                      pl.BlockSpec((TOP_K, tm), lambda i, h, *_: (0, i)),
                      pl.BlockSpec((f, dh), lambda i, h, *_: (0, h)),
                      const((1, d)), const((1, d)),
                      pl.BlockSpec(memory_space=pl.ANY), pl.BlockSpec(memory_space=pl.ANY)],
            out_specs=pl.BlockSpec((tm, d), lambda i, h, *_: (i, 0)),
            scratch_shapes=[pltpu.VMEM((2, _tile_rows(tm), dh), BF16), pltpu.VMEM((_tile_rows(tm), tm), BF16),
                            pltpu.VMEM((tm, f), BF16), pltpu.VMEM(s_gu.shape, BF16),
                            pltpu.SemaphoreType.DMA((2,)), pltpu.SemaphoreType.DMA(())],
        ),
        out_shape=jax.ShapeDtypeStruct((t, d), F32),
        compiler_params=_cp("arbitrary", "arbitrary"),
    )(gstart, size, x, lrow, wts, s_dn, ln_w, ln_b, s_gu, ys)


def _mixer_sublayer(x, w_in_all, layer, gate_b, pool_w, pool_scale, conv_w, mlstm_norm_w, ret_norm_w, w_branch,
                    w_out, ln_w, ln_b, *, batch, seq, alpha):
    t, d = x.shape
    width = d // N_BRANCH
    heads = width // HEAD_DIM
    gate_off = 8 * width
    ret_off = gate_off
    g_off = ret_off + 4 * width
    if_off = g_off + N_BRANCH * d
    w_bf16 = _realign_cast(w_in_all, layer, lo_col=gate_off, hi_col=if_off, shift=2 * heads, tr=512, tn=512)
    gate_bias = jnp.pad(gate_b, (0, LANES - 2 * heads)).reshape(1, LANES)

    proj_a = _matmul(x, w_bf16, tm=1024, tn=1024, ncols=gate_off)
    proj_b = _matmul(x, w_bf16, tm=1024, tn=1024, ncols=4 * width, col0=ret_off)
    gates = _matmul(x, w_bf16, tm=1024, tn=LANES, ncols=LANES, col0=if_off)
    y_pool, y_conv = _pool_conv(proj_a, pool_w.astype(BF16), pool_scale.reshape(1, width), conv_w,
                                seq=seq, width=width, tb=512)
    y_mlstm = _mlstm(proj_a, gates, gate_bias, mlstm_norm_w.reshape(1, width),
                     batch=batch, seq=seq, width=width, col0=4)
    y_ret = _retention(proj_b, ret_norm_w.reshape(1, width), batch=batch, seq=seq, width=width, col0=0)
    merged = _merge(x, (y_pool, y_conv, y_mlstm, y_ret), w_bf16, g_off, w_branch.astype(BF16),
                    tm=512, tn=512)
    return _proj_ln(merged, w_out.astype(BF16), x, ln_w, ln_b, alpha=alpha, tm=512)


def _xattn_sublayer(x, mem2d, wq, wk, wv, wo, ln_w, ln_b, *, seq, mem_len, alpha):
    w_kv = jnp.concatenate([wk, wv], axis=1).astype(BF16)
    kv = _matmul(mem2d, w_kv, tm=min(mem2d.shape[0], 1024), tn=512)
    return _xattn(x, kv, wq.astype(BF16), wo.astype(BF16), ln_w, ln_b,
                  alpha=alpha, seq=seq, mem_len=mem_len, tm=512)


def _moe_sublayer(x, router_w, router_b, w_gu, w_dn, layer, s_gu, s_dn, ln_w, ln_b, *, alpha):
    t, d = x.shape
    e_n = N_EXPERTS
    tm = ROUTE_TM
    rows = t * TOP_K + (t // tm) * e_n * (SEG_ALIGN - 1)
    wts, lrow, segtab, seg = _route(x, router_w.T, router_b.reshape(e_n, 1), tm=tm)
    gstart, size = segtab[:, 0, :, 0].reshape(-1), segtab[:, 1, :, 0].reshape(-1)

    xs = _dispatch(x, lrow, gstart, size, rows=rows, tm=tm)
    ys = _experts(xs, seg[0, :, 0], seg[1, :, 0], w_gu, w_dn, layer, bm=MOE_BM)
    return _combine(x, ys, lrow, wts, gstart, size, s_gu.astype(BF16), s_dn.astype(BF16), ln_w, ln_b,
                    alpha=alpha, tm=tm)


def kernel(x, mem, w_in, mlstm_gate_b, pool_w, pool_scale, conv_w, mlstm_norm_w, ret_norm_w, w_branch,
           w_mix_out, xa_wq, xa_wk, xa_wv, xa_wo, router_w, router_b, moe_w_gu, moe_w_dn, shared_w_gu,
           shared_w_dn, ln_w, ln_b):
    batch, seq, d = x.shape
    depth = w_in.shape[0]
    mem_len = mem.shape[1]
    alpha = (2 * depth) ** 0.25
    h = x.reshape(batch * seq, d)
    mem2d = mem.reshape(batch * mem_len, d)
    for l in range(depth):
        lw = ln_w[l].reshape(3, 1, d)
        lb = ln_b[l].reshape(3, 1, d)
        h = _mixer_sublayer(h, w_in, l, mlstm_gate_b[l], pool_w[l], pool_scale[l], conv_w[l], mlstm_norm_w[l],
                            ret_norm_w[l], w_branch[l], w_mix_out[l], lw[0], lb[0],
                            batch=batch, seq=seq, alpha=alpha)
        h = _xattn_sublayer(h, mem2d, xa_wq[l], xa_wk[l], xa_wv[l], xa_wo[l], lw[1], lb[1],
                            seq=seq, mem_len=mem_len, alpha=alpha)
        h = _moe_sublayer(h, router_w[l], router_b[l], moe_w_gu, moe_w_dn, l, shared_w_gu[l], shared_w_dn[l],
                          lw[2], lb[2], alpha=alpha)
    return h.reshape(batch, seq, d)
```

```python
import functools

import numpy as np
import jax
import jax.numpy as jnp
from jax import lax
from jax.experimental import pallas as pl
from jax.experimental.pallas import tpu as pltpu

F32 = jnp.float32
BF16 = jnp.bfloat16
I32 = jnp.int32

N_BRANCH = 4
HEAD_DIM = 128
POOL_WINDOWS = (2, 4, 8, 16)
CONV_WIDTH = 3
ROPE_BASE = 10000.0
XATTN_HEADS = 4
N_EXPERTS = 64
TOP_K = 8
N_GROUPS = 8
TOPK_GROUPS = 4
ROUTE_SCALE = 2.5
LN_EPS = 1e-5

LANES = 128
V7X_VMEM_BYTES = 64 * 1024 * 1024
VMEM_LIMIT = 56 * 1024 * 1024

SCAN_CHUNK = 256
SCAN_BLOCK = 512
HALO = 16
MOE_BM = 512
ROUTE_TM = 256
SEG_ALIGN = 16
ONEHOT_ROWS = 512
ROW_DMA_PRIORITY = 1


def _cp(*sem):
    return pltpu.CompilerParams(dimension_semantics=sem, vmem_limit_bytes=VMEM_LIMIT)


def _sigmoid(x):
    return 1.0 / (1.0 + jnp.exp(-x))


def _silu(x):
    return x * _sigmoid(x)


def _log_sigmoid(x):
    return jnp.minimum(x, 0.0) - jnp.log(1.0 + jnp.exp(-jnp.abs(x)))


def _layer_norm(z, w, b):
    mu = jnp.mean(z, axis=-1, keepdims=True)
    d = z - mu
    var = jnp.mean(d * d, axis=-1, keepdims=True)
    return d * lax.rsqrt(var + LN_EPS) * w + b


def _head_norm(h, w):
    mu = jnp.mean(h, axis=-1, keepdims=True)
    d = h - mu
    var = jnp.mean(d * d, axis=-1, keepdims=True)
    return d * lax.rsqrt(var + LN_EPS) * w


def _mm_kernel(x_ref, w_ref, o_ref, xb_ref):
    @pl.when(pl.program_id(1) == 0)
    def _():
        xb_ref[...] = x_ref[...].astype(BF16)

    o_ref[...] = jnp.dot(xb_ref[...], w_ref[...], preferred_element_type=F32)


def _realign_cast_kernel(a_ref, b_ref, o_ref, *, shift, lo, hi):
    j = pl.program_id(1)
    tn = o_ref.shape[1]
    shifted = jnp.logical_and(j >= lo, j < hi)

    @pl.when(shifted)
    def _():
        both = jnp.concatenate([a_ref[...], b_ref[...]], axis=1)
        o_ref[...] = both[:, shift:shift + tn].astype(BF16)

    @pl.when(jnp.logical_not(shifted))
    def _():
        o_ref[...] = a_ref[...].astype(BF16)


def _realign_cast(w_all, layer, *, lo_col, hi_col, shift, tr, tn):
    _, rows, _ = w_all.shape
    lo, hi = lo_col // tn, hi_col // tn
    src = lambda j: jnp.where(j == hi, lo, j)
    return pl.pallas_call(
        functools.partial(_realign_cast_kernel, shift=shift, lo=lo, hi=hi),
        grid=(rows // tr, hi + 1),
        in_specs=[pl.BlockSpec((None, tr, tn), lambda i, j: (layer, i, src(j))),
                  pl.BlockSpec((None, tr, LANES), lambda i, j: (layer, i, (src(j) + 1) * (tn // LANES)))],
        out_specs=pl.BlockSpec((tr, tn), lambda i, j: (i, j)),
        out_shape=jax.ShapeDtypeStruct((rows, hi_col + tn), BF16),
        compiler_params=_cp("parallel", "parallel"),
    )(w_all, w_all)


def _matmul(x, w, *, tm, tn, ncols=None, col0=0):
    t, k = x.shape
    n = w.shape[1] if ncols is None else ncols
    return pl.pallas_call(
        _mm_kernel,
        grid=(t // tm, n // tn),
        in_specs=[pl.BlockSpec((tm, k), lambda i, j: (i, 0)),
                  pl.BlockSpec((k, tn), lambda i, j: (0, col0 // tn + j))],
        out_specs=pl.BlockSpec((tm, tn), lambda i, j: (i, j)),
        out_shape=jax.ShapeDtypeStruct((t, n), F32),
        scratch_shapes=[pltpu.VMEM((tm, k), BF16)],
        compiler_params=_cp("parallel", "arbitrary"),
    )(x, w)


def _poolconv_kernel(u_ref, uh_ref, h_ref, hh_ref, b_ref, c_ref, ch_ref, pw_ref, ps_ref, cw_ref,
                     yp_ref, yc_ref, ubuf, zbuf, *, tb, n_tblk):
    first = (pl.program_id(0) % n_tblk) == 0
    ubuf[0:HALO, :] = jnp.where(first, 0.0, uh_ref[...])
    ubuf[HALO:HALO + tb, :] = u_ref[...]
    zbuf[0:HALO, :] = jnp.where(first, 0.0, ch_ref[...] * hh_ref[...])
    zbuf[HALO:HALO + tb, :] = c_ref[...] * h_ref[...]

    t_pos = (pl.program_id(0) % n_tblk) * tb + lax.broadcasted_iota(I32, (tb, LANES), 0)
    gw = u_ref.shape[1] // len(POOL_WINDOWS)
    for grp, win in enumerate(POOL_WINDOWS):
        lanes = slice(grp * gw, (grp + 1) * gw)
        cur = ubuf[HALO:HALO + tb, lanes]
        acc = cur
        for lag in range(1, win):
            acc = acc + ubuf[HALO - lag:HALO - lag + tb, lanes]
        count = jnp.minimum(t_pos + 1, win).astype(F32)
        mixed = acc / count - cur
        y = jnp.dot(mixed.astype(BF16), pw_ref[grp], preferred_element_type=F32)
        yp_ref[:, lanes] = y * ps_ref[:, lanes]

    conv = cw_ref[0:1, :] * zbuf[HALO:HALO + tb, :]
    for lag in range(1, CONV_WIDTH):
        conv = conv + cw_ref[lag:lag + 1, :] * zbuf[HALO - lag:HALO - lag + tb, :]
    yc_ref[...] = b_ref[...] * conv


def _pool_conv(proj, pool_w, pool_scale, conv_w, *, seq, width, tb):
    t = proj.shape[0]
    n_tblk = seq // tb
    ratio = tb // HALO

    def cur(col):
        return pl.BlockSpec((tb, width), lambda g: (g, col))

    def halo(col):
        return pl.BlockSpec((HALO, width), lambda g: (jnp.maximum(g * ratio - 1, 0), col))

    full = lambda shape: pl.BlockSpec(shape, lambda g: (0,) * len(shape))
    return pl.pallas_call(
        functools.partial(_poolconv_kernel, tb=tb, n_tblk=n_tblk),
        grid=(t // tb,),
        in_specs=[cur(0), halo(0), cur(1), halo(1), cur(2), cur(3), halo(3),
                  full(pool_w.shape), full(pool_scale.shape), full(conv_w.shape)],
        out_specs=[pl.BlockSpec((tb, width), lambda g: (g, 0))] * 2,
        out_shape=[jax.ShapeDtypeStruct((t, width), F32)] * 2,
        scratch_shapes=[pltpu.VMEM((HALO + tb, width), F32)] * 2,
        compiler_params=_cp("parallel"),
    )(proj, proj, proj, proj, proj, proj, proj, pool_w, pool_scale, conv_w)


def _cum_rows(x, op, fill):
    n = x.shape[0]
    row = lax.broadcasted_iota(I32, x.shape, 0)
    shift = 1
    while shift < n:
        x = op(x, jnp.where(row >= shift, pltpu.roll(x, shift, axis=0), fill))
        shift *= 2
    return x


def _mlstm_kernel(q_ref, k_ref, v_ref, o_ref, g_ref, gb_ref, nw_ref, y_ref, c_scr, n_scr, m_scr,
                  *, chunk, heads):
    @pl.when(pl.program_id(1) == 0)
    def _():
        c_scr[...] = jnp.zeros_like(c_scr)
        n_scr[...] = jnp.zeros_like(n_scr)
        m_scr[...] = jnp.zeros_like(m_scr)

    tb = q_ref.shape[0]
    hd = HEAD_DIM
    scale = hd ** -0.5
    tri = (lax.broadcasted_iota(I32, (chunk, chunk), 0) >= lax.broadcasted_iota(I32, (chunk, chunk), 1))
    for c in range(tb // chunk):
        rows = slice(c * chunk, (c + 1) * chunk)
        gates = g_ref[rows, :] + gb_ref[...]
        lf = _log_sigmoid(pltpu.roll(gates, LANES - heads, axis=1))
        cumf = _cum_rows(lf, jnp.add, 0.0)
        a = gates - cumf
        m_prev = m_scr[...]
        mu = jnp.maximum(_cum_rows(a, jnp.maximum, -jnp.inf), m_prev)
        mu_last = mu[chunk - 1:chunk, :]
        a_t = a.T
        for h in range(heads):
            cols = slice(h * hd, (h + 1) * hd)
            q = q_ref[rows, cols]
            k = k_ref[rows, cols] * scale
            v = v_ref[rows, cols]
            qb, kb, vb = q.astype(BF16), k.astype(BF16), v.astype(BF16)
            mu_col = mu[:, h:h + 1]
            a_col = a[:, h:h + 1]
            m_prev_h = m_prev[:, h:h + 1]
            mu_last_h = mu_last[:, h:h + 1]
            dmat = jnp.exp(jnp.where(tri, a_t[h:h + 1, :] - mu_col, -jnp.inf))
            s = lax.dot_general(qb, kb, (((1,), (1,)), ((), ())), preferred_element_type=F32)
            p = dmat * s
            inter = jnp.exp(m_prev_h - mu_col)
            c_h = c_scr[h]
            n_h = n_scr[h]
            num = inter * jnp.dot(qb, c_h.astype(BF16), preferred_element_type=F32) \
                + jnp.dot(p.astype(BF16), vb, preferred_element_type=F32)
            den = inter * jnp.sum(q * n_h, axis=-1, keepdims=True) + jnp.sum(p, axis=-1, keepdims=True)
            floor = jnp.exp(-(cumf[:, h:h + 1] + mu_col))
            h_out = num / jnp.maximum(jnp.abs(den), floor)

            wg = jnp.exp(a_col - mu_last_h)
            decay = jnp.exp(m_prev_h - mu_last_h)
            kw = k * wg
            c_scr[h] = decay * c_h + lax.dot_general(kw.astype(BF16), vb, (((0,), (0,)), ((), ())),
                                                     preferred_element_type=F32)
            n_scr[h] = decay * n_h + jnp.sum(kw, axis=0, keepdims=True)

            gated = _sigmoid(o_ref[rows, cols]) * h_out
            y_ref[rows, cols] = _head_norm(gated, nw_ref[:, cols])
        m_scr[...] = cumf[chunk - 1:chunk, :] + mu_last


def _mlstm(proj, gates, gate_b, norm_w, *, batch, seq, width, col0):
    t = proj.shape[0]
    heads = width // HEAD_DIM
    n_tblk = seq // SCAN_BLOCK
    blk = lambda col: pl.BlockSpec((SCAN_BLOCK, width), lambda b, i: (b * n_tblk + i, col))
    return pl.pallas_call(
        functools.partial(_mlstm_kernel, chunk=SCAN_CHUNK, heads=heads),
        grid=(batch, n_tblk),
        in_specs=[blk(col0), blk(col0 + 1), blk(col0 + 2), blk(col0 + 3),
                  pl.BlockSpec((SCAN_BLOCK, LANES), lambda b, i: (b * n_tblk + i, 0)),
                  pl.BlockSpec((1, LANES), lambda b, i: (0, 0)),
                  pl.BlockSpec((1, width), lambda b, i: (0, 0))],
        out_specs=pl.BlockSpec((SCAN_BLOCK, width), lambda b, i: (b * n_tblk + i, 0)),
        out_shape=jax.ShapeDtypeStruct((t, width), F32),
        scratch_shapes=[pltpu.VMEM((heads, HEAD_DIM, HEAD_DIM), F32),
                        pltpu.VMEM((heads, 1, HEAD_DIM), F32),
                        pltpu.VMEM((1, LANES), F32)],
        compiler_params=_cp("parallel", "arbitrary"),
    )(proj, proj, proj, proj, gates, gate_b, norm_w)


def _ret_kernel(q_ref, k_ref, v_ref, g_ref, cos_ref, sin_ref, intra_ref, cross_ref, zeta_ref, nw_ref,
                y_ref, r_scr, *, chunk, heads, chunk_decay):
    @pl.when(pl.program_id(1) == 0)
    def _():
        r_scr[...] = jnp.zeros_like(r_scr)

    tb = q_ref.shape[0]
    hd = HEAD_DIM
    scale = hd ** -0.5
    for c in range(tb // chunk):
        rows = slice(c * chunk, (c + 1) * chunk)
        cos = cos_ref[rows, :]
        sin = sin_ref[rows, :]
        for h in range(heads):
            cols = slice(h * hd, (h + 1) * hd)
            q = q_ref[rows, cols]
            k = k_ref[rows, cols]
            q = q * cos + pltpu.roll(q, hd // 2, axis=1) * sin
            k = (k * cos + pltpu.roll(k, hd // 2, axis=1) * sin) * scale
            vb = v_ref[rows, cols].astype(BF16)
            qb = q.astype(BF16)
            s = lax.dot_general(qb, k.astype(BF16), (((1,), (1,)), ((), ())), preferred_element_type=F32)
            inner = jnp.dot((s * intra_ref[h]).astype(BF16), vb, preferred_element_type=F32)
            r_h = r_scr[h]
            crs = jnp.dot(qb, r_h.astype(BF16), preferred_element_type=F32) * cross_ref[h]
            r_scr[h] = chunk_decay[h] * r_h + lax.dot_general(
                (k * zeta_ref[h]).astype(BF16), vb, (((0,), (0,)), ((), ())), preferred_element_type=F32)
            y_ref[rows, cols] = _silu(g_ref[rows, cols]) * _head_norm(inner + crs, nw_ref[:, cols])


def _retention(proj, norm_w, *, batch, seq, width, col0):
    t = proj.shape[0]
    heads = width // HEAD_DIM
    chunk = SCAN_CHUNK
    n_tblk = seq // SCAN_BLOCK
    half = HEAD_DIM // 2
    inv = np.float32(ROPE_BASE) ** (-np.arange(half, dtype=np.float32) / np.float32(half))
    ang = (np.arange(seq, dtype=np.float32)[:, None] * inv[None, :]).astype(np.float64)
    cos_t = jnp.asarray(np.concatenate([np.cos(ang), np.cos(ang)], axis=-1), F32)
    sin_t = jnp.asarray(np.concatenate([-np.sin(ang), np.sin(ang)], axis=-1), F32)
    log_g = jnp.log(1.0 - 2.0 ** (-5.0 - jnp.arange(heads, dtype=F32)))
    tt = jnp.arange(chunk, dtype=F32)
    lag = tt[:, None] - tt[None, :]
    intra = jnp.where(lag >= 0, jnp.exp(jnp.maximum(lag, 0.0)[None] * log_g[:, None, None]), 0.0)
    cross = jnp.broadcast_to(jnp.exp((tt + 1.0)[None, :] * log_g[:, None])[:, :, None], (heads, chunk, HEAD_DIM))
    zeta = jnp.broadcast_to(jnp.exp((chunk - 1.0 - tt)[None, :] * log_g[:, None])[:, :, None],
                            (heads, chunk, HEAD_DIM))
    chunk_decay = tuple(float((1.0 - 2.0 ** (-5.0 - h)) ** chunk) for h in range(heads))

    blk = lambda col: pl.BlockSpec((SCAN_BLOCK, width), lambda b, i: (b * n_tblk + i, col))
    pos = pl.BlockSpec((SCAN_BLOCK, HEAD_DIM), lambda b, i: (i, 0))
    full3 = lambda a: pl.BlockSpec(a.shape, lambda b, i: (0, 0, 0))
    return pl.pallas_call(
        functools.partial(_ret_kernel, chunk=chunk, heads=heads, chunk_decay=chunk_decay),
        grid=(batch, n_tblk),
        in_specs=[blk(col0), blk(col0 + 1), blk(col0 + 2), blk(col0 + 3), pos, pos,
                  full3(intra), full3(cross), full3(zeta),
                  pl.BlockSpec((1, width), lambda b, i: (0, 0))],
        out_specs=pl.BlockSpec((SCAN_BLOCK, width), lambda b, i: (b * n_tblk + i, 0)),
        out_shape=jax.ShapeDtypeStruct((t, width), F32),
        scratch_shapes=[pltpu.VMEM((heads, HEAD_DIM, HEAD_DIM), F32)],
        compiler_params=_cp("parallel", "arbitrary"),
    )(proj, proj, proj, proj, cos_t, sin_t, intra, cross, zeta, norm_w)


def _merge_kernel(x_ref, *refs):
    y_refs, wg_refs = refs[:N_BRANCH], refs[N_BRANCH:2 * N_BRANCH]
    wb_ref, o_ref, xb_ref = refs[2 * N_BRANCH:]

    @pl.when(pl.program_id(1) == 0)
    def _():
        xb_ref[...] = x_ref[...].astype(BF16)

    xb = xb_ref[...]
    acc = None
    for n in range(N_BRANCH):
        gate = _sigmoid(jnp.dot(xb, wg_refs[n][...], preferred_element_type=F32))
        term = gate * jnp.dot(y_refs[n][...].astype(BF16), wb_ref[n], preferred_element_type=F32)
        acc = term if acc is None else acc + term
    o_ref[...] = acc


def _merge(x, branches, w_gate, gate_col0, w_branch, *, tm, tn):
    t, d = x.shape
    width = branches[0].shape[1]
    ybs = pl.BlockSpec((tm, width), lambda i, j: (i, 0))

    def gate_spec(n):
        return pl.BlockSpec((d, tn), lambda i, j: (0, (gate_col0 + n * d) // tn + j))

    return pl.pallas_call(
        _merge_kernel,
        grid=(t // tm, d // tn),
        in_specs=[pl.BlockSpec((tm, d), lambda i, j: (i, 0))] + [ybs] * N_BRANCH
                 + [gate_spec(n) for n in range(N_BRANCH)]
                 + [pl.BlockSpec((N_BRANCH, width, tn), lambda i, j: (0, 0, j))],
        out_specs=pl.BlockSpec((tm, tn), lambda i, j: (i, j)),
        out_shape=jax.ShapeDtypeStruct((t, d), F32),
        scratch_shapes=[pltpu.VMEM((tm, d), BF16)],
        compiler_params=_cp("parallel", "arbitrary"),
    )(x, *branches, *([w_gate] * N_BRANCH), w_branch)


def _proj_ln_kernel(a_ref, w_ref, r_ref, lw_ref, lb_ref, o_ref, *, alpha):
    y = jnp.dot(a_ref[...].astype(BF16), w_ref[...], preferred_element_type=F32)
    o_ref[...] = _layer_norm(alpha * r_ref[...] + y, lw_ref[...], lb_ref[...])


def _proj_ln(a, w, resid, ln_w, ln_b, *, alpha, tm):
    t, k = a.shape
    d = w.shape[1]
    row = lambda n: pl.BlockSpec((tm, n), lambda i: (i, 0))
    const = lambda shape: pl.BlockSpec(shape, lambda i: (0, 0))
    return pl.pallas_call(
        functools.partial(_proj_ln_kernel, alpha=alpha),
        grid=(t // tm,),
        in_specs=[row(k), const((k, d)), row(d), const((1, d)), const((1, d))],
        out_specs=row(d),
        out_shape=jax.ShapeDtypeStruct((t, d), F32),
        compiler_params=_cp("parallel"),
    )(a, w, resid, ln_w, ln_b)


def _xattn_kernel(x_ref, wq_ref, kv_ref, wo_ref, lw_ref, lb_ref, o_ref, *, alpha, heads):
    x = x_ref[...]
    hd = HEAD_DIM
    inner = heads * hd
    q = jnp.dot(x.astype(BF16), wq_ref[...], preferred_element_type=F32)
    outs = []
    for h in range(heads):
        qh = q[:, h * hd:(h + 1) * hd].astype(BF16)
        kh = kv_ref[:, h * hd:(h + 1) * hd].astype(BF16)
        vh = kv_ref[:, inner + h * hd:inner + (h + 1) * hd].astype(BF16)
        s = lax.dot_general(qh, kh, (((1,), (1,)), ((), ())), preferred_element_type=F32) * hd ** -0.5
        s = s - jnp.max(s, axis=-1, keepdims=True)
        e = jnp.exp(s)
        p = e / jnp.sum(e, axis=-1, keepdims=True)
        outs.append(jnp.dot(p.astype(BF16), vh, preferred_element_type=F32).astype(BF16))
    o = jnp.concatenate(outs, axis=-1)
    y = jnp.dot(o, wo_ref[...], preferred_element_type=F32)
    o_ref[...] = _layer_norm(alpha * x + y, lw_ref[...], lb_ref[...])


def _xattn(x, kv, wq, wo, ln_w, ln_b, *, alpha, seq, mem_len, tm):
    t, d = x.shape
    inner = wq.shape[1]
    n_tblk = seq // tm
    const = lambda shape: pl.BlockSpec(shape, lambda i: (0, 0))
    return pl.pallas_call(
        functools.partial(_xattn_kernel, alpha=alpha, heads=XATTN_HEADS),
        grid=(t // tm,),
        in_specs=[pl.BlockSpec((tm, d), lambda i: (i, 0)), const((d, inner)),
                  pl.BlockSpec((mem_len, 2 * inner), lambda i: (i // n_tblk, 0)),
                  const((inner, d)), const((1, d)), const((1, d))],
        out_specs=pl.BlockSpec((tm, d), lambda i: (i, 0)),
        out_shape=jax.ShapeDtypeStruct((t, d), F32),
        compiler_params=_cp("parallel"),
    )(x, wq, kv, wo, ln_w, ln_b)


def _round_up_pow2(x, m):
    shift = m.bit_length() - 1
    return jnp.left_shift(jnp.right_shift(x + (m - 1), shift), shift)


def _route_kernel(x_ref, wr_ref, rb_ref, w_ref, lrow_ref, segtab_ref, seg_ref, size_all, tot, *, tm):
    step = pl.program_id(0)

    @pl.when(step == 0)
    def _():
        tot[...] = jnp.zeros_like(tot)

    e_n, g_n = N_EXPERTS, N_GROUPS
    per = e_n // g_n
    logits = lax.dot_general(wr_ref[...], x_ref[...], (((1,), (1,)), ((), ())),
                             precision=lax.Precision.HIGHEST, preferred_element_type=F32)
    scores = _sigmoid(logits)
    biased = scores + rb_ref[...]
    b3 = biased.reshape(g_n, per, tm)
    member = lax.broadcasted_iota(I32, (g_n, per, tm), 1)
    top1 = jnp.max(b3, axis=1, keepdims=True)
    first = jnp.min(jnp.where(b3 == top1, member, per), axis=1, keepdims=True)
    top2 = jnp.max(jnp.where(member == first, -jnp.inf, b3), axis=1, keepdims=True)
    gs = top1 + top2
    gid = lax.broadcasted_iota(I32, (g_n, 1, tm), 0)
    rank = jnp.zeros((g_n, 1, tm), I32)
    for other in range(g_n):
        o = gs[other:other + 1]
        ahead = jnp.logical_or(o > gs, jnp.logical_and(o == gs, other < gid))
        rank = rank + jnp.where(ahead, 1, 0)
    cur = jnp.where(rank < TOPK_GROUPS, b3, -jnp.inf).reshape(e_n, tm)

    eid = lax.broadcasted_iota(I32, (e_n, tm), 0)
    picks, vals = [], []
    sel = jnp.zeros((e_n, tm), F32)
    for k in range(TOP_K):
        mx = jnp.max(cur, axis=0, keepdims=True)
        ik = jnp.min(jnp.where(cur == mx, eid, e_n), axis=0, keepdims=True)
        hit = eid == ik
        vals.append(jnp.sum(jnp.where(hit, scores, 0.0), axis=0, keepdims=True))
        cur = jnp.where(hit, -jnp.inf, cur)
        sel = jnp.where(hit, 1.0, sel)
        picks.append(ik)
    total = vals[0]
    for v in vals[1:]:
        total = total + v

    tri = jnp.where(lax.broadcasted_iota(I32, (tm, tm), 0) <= lax.broadcasted_iota(I32, (tm, tm), 1), 1.0, 0.0)
    incl = jnp.dot(sel.astype(BF16), tri.astype(BF16), preferred_element_type=F32)
    size = _round_up_pow2(jnp.broadcast_to(incl[:, tm - 1:tm], (e_n, LANES)).astype(I32), SEG_ALIGN)
    loff = _cum_rows(size, jnp.add, 0) - size
    base = loff[:, 0:1].astype(F32) + incl - 1.0
    for k in range(TOP_K):
        w_ref[k:k + 1, :] = vals[k] / total * ROUTE_SCALE
        lrow_ref[0, k:k + 1, :] = jnp.sum(jnp.where(eid == picks[k], base, 0.0),
                                          axis=0, keepdims=True).astype(I32)
    size_all[step] = size
    tot[...] = tot[...] + size

    @pl.when(step == pl.num_programs(0) - 1)
    def _():
        rows = tot[...]
        start = _cum_rows(rows, jnp.add, 0) - rows

        def tile_seg(i, run):
            segtab_ref[i, 0] = run
            segtab_ref[i, 1] = size_all[i]
            return run + size_all[i]

        lax.fori_loop(0, pl.num_programs(0), tile_seg, start)
        seg_ref[0] = start
        seg_ref[1] = rows


def _route(x, router_w_t, router_b, *, tm):
    t, d = x.shape
    e_n = N_EXPERTS
    n_t = t // tm
    return pl.pallas_call(
        functools.partial(_route_kernel, tm=tm),
        grid=(n_t,),
        in_specs=[pl.BlockSpec((tm, d), lambda i: (i, 0)),
                  pl.BlockSpec((e_n, d), lambda i: (0, 0)),
                  pl.BlockSpec((e_n, 1), lambda i: (0, 0))],
        out_specs=[pl.BlockSpec((TOP_K, tm), lambda i: (0, i)),
                   pl.BlockSpec((1, TOP_K, tm), lambda i: (i, 0, 0)),
                   pl.BlockSpec((n_t, 2, e_n, LANES), lambda i: (0, 0, 0, 0)),
                   pl.BlockSpec((2, e_n, LANES), lambda i: (0, 0, 0))],
        out_shape=[jax.ShapeDtypeStruct((TOP_K, t), F32), jax.ShapeDtypeStruct((n_t, TOP_K, tm), I32),
                   jax.ShapeDtypeStruct((n_t, 2, e_n, LANES), I32), jax.ShapeDtypeStruct((2, e_n, LANES), I32)],
        scratch_shapes=[pltpu.VMEM((n_t, e_n, LANES), I32), pltpu.VMEM((e_n, LANES), I32)],
        compiler_params=_cp("arbitrary"),
    )(x, router_w_t, router_b)


def _tile_rows(tm):
    worst = TOP_K * tm + N_EXPERTS * (SEG_ALIGN - 1)
    return -(-worst // ONEHOT_ROWS) * ONEHOT_ROWS


def _onehot_rows(chunk, lrow, values, tm):
    rid = chunk * ONEHOT_ROWS + lax.broadcasted_iota(I32, (ONEHOT_ROWS, tm), 0)
    acc = jnp.zeros((ONEHOT_ROWS, tm), F32)
    for k in range(TOP_K):
        acc = jnp.where(rid == lrow[k:k + 1, :], 1.0 if values is None else values[k:k + 1, :], acc)
    return acc.astype(BF16)


def _segments(gstart_ref, size_ref, tile, make_copy, act, keep=None):
    def body(e, loff):
        n = size_ref[tile * N_EXPERTS + e]
        wanted = n > 0 if keep is None else jnp.logical_and(n > 0, keep(loff + n))

        @pl.when(wanted)
        def _():
            act(make_copy(pl.multiple_of(gstart_ref[tile * N_EXPERTS + e], SEG_ALIGN),
                          pl.multiple_of(loff, SEG_ALIGN), pl.multiple_of(n, SEG_ALIGN)))
        return loff + n
    return lax.fori_loop(0, N_EXPERTS, body, 0)


def _start(cp):
    cp.start()


def _wait(cp):
    cp.wait()


def _dispatch_kernel(gstart_ref, size_ref, x_ref, lrow_ref, xs_ref, stage, sem_a, sem_b, *, tm):
    i = pl.program_id(0)
    last = pl.num_programs(0) - 1
    n_chunks = stage.shape[0] // ONEHOT_ROWS
    split_chunk = n_chunks // 2
    split = split_chunk * ONEHOT_ROWS
    assert tm <= ONEHOT_ROWS

    def seg_walk(tile, act, phase_b):
        sem = sem_b if phase_b else sem_a
        return _segments(
            gstart_ref, size_ref, tile,
            lambda g, loff, n: pltpu.make_async_copy(stage.at[pl.ds(loff, n), :], xs_ref.at[pl.ds(g, n), :], sem),
            act, (lambda end: end > split) if phase_b else (lambda end: end <= split))

    xb = x_ref[...].astype(BF16)
    lrow = lrow_ref[0]
    n_rows = lax.fori_loop(0, N_EXPERTS, lambda e, s: s + size_ref[i * N_EXPERTS + e], 0)

    def chunks(lo, hi):
        for c in range(lo, hi):
            def one(c=c):
                stage[c * ONEHOT_ROWS:(c + 1) * ONEHOT_ROWS, :] = jnp.dot(
                    _onehot_rows(c, lrow, None, tm), xb, preferred_element_type=F32).astype(BF16)
            if c * ONEHOT_ROWS < TOP_K * tm:
                one()
            else:
                pl.when(c * ONEHOT_ROWS < n_rows)(one)

    prev = jnp.maximum(i - 1, 0)

    @pl.when(i > 0)
    def _():
        seg_walk(prev, _wait, False)
    chunks(0, split_chunk - 1)

    @pl.when(i > 0)
    def _():
        seg_walk(prev, _wait, True)
    chunks(split_chunk - 1, split_chunk)
    seg_walk(i, _start, False)
    chunks(split_chunk, n_chunks)
    seg_walk(i, _start, True)

    @pl.when(i == last)
    def _():
        seg_walk(i, _wait, False)
        seg_walk(i, _wait, True)


def _dispatch(x, lrow, gstart, size, *, rows, tm):
    t, d = x.shape
    return pl.pallas_call(
        functools.partial(_dispatch_kernel, tm=tm),
        grid_spec=pltpu.PrefetchScalarGridSpec(
            num_scalar_prefetch=2,
            grid=(t // tm,),
            in_specs=[pl.BlockSpec((tm, d), lambda i, *_: (i, 0)),
                      pl.BlockSpec((1, TOP_K, tm), lambda i, *_: (i, 0, 0))],
            out_specs=pl.BlockSpec(memory_space=pl.ANY),
            scratch_shapes=[pltpu.VMEM((_tile_rows(tm), d), BF16)] + [pltpu.SemaphoreType.DMA(())] * 2,
        ),
        out_shape=jax.ShapeDtypeStruct((rows, d), BF16),
        compiler_params=_cp("arbitrary"),
    )(gstart, size, x, lrow)


def _expert_kernel(start_ref, rows_ref, wgu_ref, wdn_ref, xs_ref, ys_ref, wgu_b, wdn_b, xbuf, ybuf, sem_in, sem_out,
                   *, bm):
    e = pl.program_id(0)
    row0 = start_ref[e]
    n = rows_ref[e]
    n_blk = (n + bm - 1) // bm

    @pl.when(e == 0)
    def _():
        xbuf[...] = jnp.zeros_like(xbuf)

    def rows_of(b):
        return pl.multiple_of(jnp.minimum(bm, n - b * bm), SEG_ALIGN)

    def in_copy(b, slot):
        r = rows_of(b)
        src = xs_ref.at[pl.ds(pl.multiple_of(row0 + b * bm, SEG_ALIGN), r), :]
        return pltpu.make_async_copy(src, xbuf.at[slot, pl.ds(0, r), :], sem_in.at[slot])

    def out_copy(b, slot):
        r = rows_of(b)
        dst = ys_ref.at[pl.ds(pl.multiple_of(row0 + b * bm, SEG_ALIGN), r), :]
        return pltpu.make_async_copy(ybuf.at[slot, pl.ds(0, r), :], dst, sem_out.at[slot])

    @pl.when(n_blk > 0)
    def _():
        in_copy(0, 0).start(priority=ROW_DMA_PRIORITY)

    wgu_b[...] = wgu_ref[...].astype(BF16)
    wdn_b[...] = wdn_ref[...].astype(BF16)

    def block(b, carry):
        slot = b % 2

        @pl.when(b + 1 < n_blk)
        def _():
            in_copy(b + 1, 1 - slot).start(priority=ROW_DMA_PRIORITY)

        in_copy(b, slot).wait()

        @pl.when(b >= 2)
        def _():
            out_copy(b - 2, slot).wait()

        f = wdn_b.shape[0]
        gu = jnp.dot(xbuf[slot], wgu_b[...], preferred_element_type=F32)
        hidden = (_silu(gu[:, :f]) * gu[:, f:]).astype(BF16)
        ybuf[slot] = jnp.dot(hidden, wdn_b[...], preferred_element_type=F32).astype(BF16)
        out_copy(b, slot).start(priority=ROW_DMA_PRIORITY)
        return carry

    lax.fori_loop(0, n_blk, block, 0)

    for back in (2, 1):
        @pl.when(n_blk >= back)
        def _():
            b = n_blk - back
            out_copy(b, b % 2).wait()


def _experts(xs, start, rows, w_gu, w_dn, layer, *, bm):
    total, d = xs.shape
    f2 = w_gu.shape[3]
    f = w_dn.shape[2]
    return pl.pallas_call(
        functools.partial(_expert_kernel, bm=bm),
        grid_spec=pltpu.PrefetchScalarGridSpec(
            num_scalar_prefetch=2,
            grid=(N_EXPERTS,),
            in_specs=[pl.BlockSpec((None, None, d, f2), lambda e, *_: (layer, e, 0, 0)),
                      pl.BlockSpec((None, None, f, d), lambda e, *_: (layer, e, 0, 0)),
                      pl.BlockSpec(memory_space=pl.ANY)],
            out_specs=pl.BlockSpec(memory_space=pl.ANY),
            scratch_shapes=[pltpu.VMEM((d, f2), BF16), pltpu.VMEM((f, d), BF16),
                            pltpu.VMEM((2, bm, d), BF16), pltpu.VMEM((2, bm, d), BF16),
                            pltpu.SemaphoreType.DMA((2,)), pltpu.SemaphoreType.DMA((2,))],
        ),
        out_shape=jax.ShapeDtypeStruct((total, d), BF16),
        compiler_params=_cp("arbitrary"),
    )(start, rows, w_gu, w_dn, xs)


def _combine_kernel(gstart_ref, size_ref, x_ref, lrow_ref, w_ref, sdn_ref, lw_ref, lb_ref, sgu_hbm, ys_ref,
                    o_ref, ybuf, wt, hid, sgu, sems, wsem, *, tm, alpha):
    i, half = pl.program_id(0), pl.program_id(1)
    last_tile = pl.num_programs(0) - 1
    dh = ybuf.shape[2]
    n_chunks = ybuf.shape[1] // ONEHOT_ROWS
    sure_chunks = (TOP_K * tm) // ONEHOT_ROWS

    def contract_rows(w_rows, y_rows):
        return lax.dot_general(w_rows, y_rows, (((0,), (0,)), ((), ())), preferred_element_type=F32)

    def seg_walk(tile, hf, act):
        return _segments(
            gstart_ref, size_ref, tile,
            lambda g, loff, n: pltpu.make_async_copy(ys_ref.at[pl.ds(g, n), hf * dh:(hf + 1) * dh],
                                                     ybuf.at[hf, pl.ds(loff, n), :], sems.at[hf]),
            act)

    @pl.when(jnp.logical_and(i == 0, half == 0))
    def _():
        ybuf[...] = jnp.zeros_like(ybuf)
        weights = pltpu.make_async_copy(sgu_hbm, sgu, wsem)
        weights.start()
        weights.wait()
        seg_walk(0, 0, _start)

    @pl.when(half == 0)
    def _():
        seg_walk(i, 1, _start)
        f = hid.shape[1]
        gu = jnp.dot(x_ref[...].astype(BF16), sgu[...], preferred_element_type=F32)
        hidden = (_silu(gu[:, :f]) * gu[:, f:]).astype(BF16)
        hid[...] = hidden
        n_rows = seg_walk(i, 0, _wait)
        lrow, w = lrow_ref[0], w_ref[...]
        acc = jnp.dot(hidden, sdn_ref[...], preferred_element_type=F32)
        for c in range(sure_chunks):
            rows = slice(c * ONEHOT_ROWS, (c + 1) * ONEHOT_ROWS)
            w_rows = _onehot_rows(c, lrow, w, tm)
            wt[rows, :] = w_rows
            acc = acc + contract_rows(w_rows, ybuf[0, rows, :])
        o_ref[:, 0:dh] = acc
        for c in range(sure_chunks, n_chunks):
            @pl.when(c * ONEHOT_ROWS < n_rows)
            def _():
                rows = slice(c * ONEHOT_ROWS, (c + 1) * ONEHOT_ROWS)
                w_rows = _onehot_rows(c, lrow, w, tm)
                wt[rows, :] = w_rows
                o_ref[:, 0:dh] += contract_rows(w_rows, ybuf[0, rows, :])

    @pl.when(half == 1)
    def _():
        @pl.when(i < last_tile)
        def _():
            seg_walk(i + 1, 0, _start)
        n_rows = seg_walk(i, 1, _wait)
        sure = slice(0, sure_chunks * ONEHOT_ROWS)
        o_ref[:, dh:2 * dh] = contract_rows(wt[sure, :], ybuf[1, sure, :]) \
            + jnp.dot(hid[...], sdn_ref[...], preferred_element_type=F32)
        for c in range(sure_chunks, n_chunks):
            @pl.when(c * ONEHOT_ROWS < n_rows)
            def _():
                rows = slice(c * ONEHOT_ROWS, (c + 1) * ONEHOT_ROWS)
                o_ref[:, dh:2 * dh] += contract_rows(wt[rows, :], ybuf[1, rows, :])
        o_ref[...] = _layer_norm(alpha * x_ref[...] + o_ref[...], lw_ref[...], lb_ref[...])


def _combine(x, ys, lrow, wts, gstart, size, s_gu, s_dn, ln_w, ln_b, *, alpha, tm):
    t, d = x.shape
    dh = d // 2
    f = s_dn.shape[0]
    const = lambda shape: pl.BlockSpec(shape, lambda i, h, *_: (0, 0))
    return pl.pallas_call(
        functools.partial(_combine_kernel, tm=tm, alpha=alpha),
        grid_spec=pltpu.PrefetchScalarGridSpec(
            num_scalar_prefetch=2,
            grid=(t // tm, 2),
            in_specs=[pl.BlockSpec((tm, d), lambda i, h, *_: (i, 0)),
                      pl.BlockSpec((1, TOP_K, tm), lambda i, h, *_: (i, 0, 0)),
                      pl.BlockSpec((TOP_K, tm), lambda i, h, *_: (0, i)),
                      pl.BlockSpec((f, dh), lambda i, h, *_: (0, h)),
                      const((1, d)), const((1, d)),
                      pl.BlockSpec(memory_space=pl.ANY), pl.BlockSpec(memory_space=pl.ANY)],
            out_specs=pl.BlockSpec((tm, d), lambda i, h, *_: (i, 0)),
            scratch_shapes=[pltpu.VMEM((2, _tile_rows(tm), dh), BF16), pltpu.VMEM((_tile_rows(tm), tm), BF16),
                            pltpu.VMEM((tm, f), BF16), pltpu.VMEM(s_gu.shape, BF16),
                            pltpu.SemaphoreType.DMA((2,)), pltpu.SemaphoreType.DMA(())],
        ),
        out_shape=jax.ShapeDtypeStruct((t, d), F32),
        compiler_params=_cp("arbitrary", "arbitrary"),
    )(gstart, size, x, lrow, wts, s_dn, ln_w, ln_b, s_gu, ys)


def _mixer_sublayer(x, w_in_all, layer, gate_b, pool_w, pool_scale, conv_w, mlstm_norm_w, ret_norm_w, w_branch,
                    w_out, ln_w, ln_b, *, batch, seq, alpha):
    t, d = x.shape
    width = d // N_BRANCH
    heads = width // HEAD_DIM
    gate_off = 8 * width
    ret_off = gate_off
    g_off = ret_off + 4 * width
    if_off = g_off + N_BRANCH * d
    w_bf16 = _realign_cast(w_in_all, layer, lo_col=gate_off, hi_col=if_off, shift=2 * heads, tr=512, tn=512)
    gate_bias = jnp.pad(gate_b, (0, LANES - 2 * heads)).reshape(1, LANES)

    proj_a = _matmul(x, w_bf16, tm=1024, tn=1024, ncols=gate_off)
    proj_b = _matmul(x, w_bf16, tm=1024, tn=1024, ncols=4 * width, col0=ret_off)
    gates = _matmul(x, w_bf16, tm=1024, tn=LANES, ncols=LANES, col0=if_off)
    y_pool, y_conv = _pool_conv(proj_a, pool_w.astype(BF16), pool_scale.reshape(1, width), conv_w,
                                seq=seq, width=width, tb=512)
    y_mlstm = _mlstm(proj_a, gates, gate_bias, mlstm_norm_w.reshape(1, width),
                     batch=batch, seq=seq, width=width, col0=4)
    y_ret = _retention(proj_b, ret_norm_w.reshape(1, width), batch=batch, seq=seq, width=width, col0=0)
    merged = _merge(x, (y_pool, y_conv, y_mlstm, y_ret), w_bf16, g_off, w_branch.astype(BF16),
                    tm=512, tn=512)
    return _proj_ln(merged, w_out.astype(BF16), x, ln_w, ln_b, alpha=alpha, tm=512)


def _xattn_sublayer(x, mem2d, wq, wk, wv, wo, ln_w, ln_b, *, seq, mem_len, alpha):
    w_kv = jnp.concatenate([wk, wv], axis=1).astype(BF16)
    kv = _matmul(mem2d, w_kv, tm=min(mem2d.shape[0], 1024), tn=512)
    return _xattn(x, kv, wq.astype(BF16), wo.astype(BF16), ln_w, ln_b,
                  alpha=alpha, seq=seq, mem_len=mem_len, tm=512)


def _moe_sublayer(x, router_w, router_b, w_gu, w_dn, layer, s_gu, s_dn, ln_w, ln_b, *, alpha):
    t, d = x.shape
    e_n = N_EXPERTS
    tm = ROUTE_TM
    rows = t * TOP_K + (t // tm) * e_n * (SEG_ALIGN - 1)
    wts, lrow, segtab, seg = _route(x, router_w.T, router_b.reshape(e_n, 1), tm=tm)
    gstart, size = segtab[:, 0, :, 0].reshape(-1), segtab[:, 1, :, 0].reshape(-1)

    xs = _dispatch(x, lrow, gstart, size, rows=rows, tm=tm)
    ys = _experts(xs, seg[0, :, 0], seg[1, :, 0], w_gu, w_dn, layer, bm=MOE_BM)
    return _combine(x, ys, lrow, wts, gstart, size, s_gu.astype(BF16), s_dn.astype(BF16), ln_w, ln_b,
                    alpha=alpha, tm=tm)


def kernel(x, mem, w_in, mlstm_gate_b, pool_w, pool_scale, conv_w, mlstm_norm_w, ret_norm_w, w_branch,
           w_mix_out, xa_wq, xa_wk, xa_wv, xa_wo, router_w, router_b, moe_w_gu, moe_w_dn, shared_w_gu,
           shared_w_dn, ln_w, ln_b):
    batch, seq, d = x.shape
    depth = w_in.shape[0]
    mem_len = mem.shape[1]
    alpha = (2 * depth) ** 0.25
    h = x.reshape(batch * seq, d)
    mem2d = mem.reshape(batch * mem_len, d)
    for l in range(depth):
        lw = ln_w[l].reshape(3, 1, d)
        lb = ln_b[l].reshape(3, 1, d)
        h = _mixer_sublayer(h, w_in, l, mlstm_gate_b[l], pool_w[l], pool_scale[l], conv_w[l], mlstm_norm_w[l],
                            ret_norm_w[l], w_branch[l], w_mix_out[l], lw[0], lb[0],
                            batch=batch, seq=seq, alpha=alpha)
        h = _xattn_sublayer(h, mem2d, xa_wq[l], xa_wk[l], xa_wv[l], xa_wo[l], lw[1], lb[1],
                            seq=seq, mem_len=mem_len, alpha=alpha)
        h = _moe_sublayer(h, router_w[l], router_b[l], moe_w_gu, moe_w_dn, l, shared_w_gu[l], shared_w_dn[l],
                          lw[2], lb[2], alpha=alpha)
    return h.reshape(batch, seq, d)
```

```python
import functools

import numpy as np
import jax
import jax.numpy as jnp
from jax import lax
from jax.experimental import pallas as pl
from jax.experimental.pallas import tpu as pltpu

F32 = jnp.float32
BF16 = jnp.bfloat16
I32 = jnp.int32

N_BRANCH = 4
HEAD_DIM = 128
POOL_WINDOWS = (2, 4, 8, 16)
CONV_WIDTH = 3
ROPE_BASE = 10000.0
XATTN_HEADS = 4
N_EXPERTS = 64
TOP_K = 8
N_GROUPS = 8
TOPK_GROUPS = 4
ROUTE_SCALE = 2.5
LN_EPS = 1e-5

LANES = 128
V7X_VMEM_BYTES = 64 * 1024 * 1024
VMEM_LIMIT = 56 * 1024 * 1024

SCAN_CHUNK = 256
SCAN_BLOCK = 512
HALO = 16
MOE_BM = 512
ROUTE_TM = 256
SEG_ALIGN = 16
ONEHOT_ROWS = 512
ROW_DMA_PRIORITY = 1


def _cp(*sem):
    return pltpu.CompilerParams(dimension_semantics=sem, vmem_limit_bytes=VMEM_LIMIT)


def _sigmoid(x):
    return 1.0 / (1.0 + jnp.exp(-x))


def _silu(x):
    return x * _sigmoid(x)


def _log_sigmoid(x):
    return jnp.minimum(x, 0.0) - jnp.log(1.0 + jnp.exp(-jnp.abs(x)))


def _layer_norm(z, w, b):
    mu = jnp.mean(z, axis=-1, keepdims=True)
    d = z - mu
    var = jnp.mean(d * d, axis=-1, keepdims=True)
    return d * lax.rsqrt(var + LN_EPS) * w + b


def _head_norm(h, w):
    mu = jnp.mean(h, axis=-1, keepdims=True)
    d = h - mu
    var = jnp.mean(d * d, axis=-1, keepdims=True)
    return d * lax.rsqrt(var + LN_EPS) * w


def _mm_kernel(x_ref, w_ref, o_ref, xb_ref):
    @pl.when(pl.program_id(1) == 0)
    def _():
        xb_ref[...] = x_ref[...].astype(BF16)

    o_ref[...] = jnp.dot(xb_ref[...], w_ref[...], preferred_element_type=F32)


def _realign_cast_kernel(a_ref, b_ref, o_ref, *, shift, lo, hi):
    j = pl.program_id(1)
    tn = o_ref.shape[1]
    shifted = jnp.logical_and(j >= lo, j < hi)

    @pl.when(shifted)
    def _():
        both = jnp.concatenate([a_ref[...], b_ref[...]], axis=1)
        o_ref[...] = both[:, shift:shift + tn].astype(BF16)

    @pl.when(jnp.logical_not(shifted))
    def _():
        o_ref[...] = a_ref[...].astype(BF16)


def _realign_cast(w_all, layer, *, lo_col, hi_col, shift, tr, tn):
    _, rows, _ = w_all.shape
    lo, hi = lo_col // tn, hi_col // tn
    src = lambda j: jnp.where(j == hi, lo, j)
    return pl.pallas_call(
        functools.partial(_realign_cast_kernel, shift=shift, lo=lo, hi=hi),
        grid=(rows // tr, hi + 1),
        in_specs=[pl.BlockSpec((None, tr, tn), lambda i, j: (layer, i, src(j))),
                  pl.BlockSpec((None, tr, LANES), lambda i, j: (layer, i, (src(j) + 1) * (tn // LANES)))],
        out_specs=pl.BlockSpec((tr, tn), lambda i, j: (i, j)),
        out_shape=jax.ShapeDtypeStruct((rows, hi_col + tn), BF16),
        compiler_params=_cp("parallel", "parallel"),
    )(w_all, w_all)


def _matmul(x, w, *, tm, tn, ncols=None, col0=0):
    t, k = x.shape
    n = w.shape[1] if ncols is None else ncols
    return pl.pallas_call(
        _mm_kernel,
        grid=(t // tm, n // tn),
        in_specs=[pl.BlockSpec((tm, k), lambda i, j: (i, 0)),
                  pl.BlockSpec((k, tn), lambda i, j: (0, col0 // tn + j))],
        out_specs=pl.BlockSpec((tm, tn), lambda i, j: (i, j)),
        out_shape=jax.ShapeDtypeStruct((t, n), F32),
        scratch_shapes=[pltpu.VMEM((tm, k), BF16)],
        compiler_params=_cp("parallel", "arbitrary"),
    )(x, w)


def _poolconv_kernel(u_ref, uh_ref, h_ref, hh_ref, b_ref, c_ref, ch_ref, pw_ref, ps_ref, cw_ref,
                     yp_ref, yc_ref, ubuf, zbuf, *, tb, n_tblk):
    first = (pl.program_id(0) % n_tblk) == 0
    ubuf[0:HALO, :] = jnp.where(first, 0.0, uh_ref[...])
    ubuf[HALO:HALO + tb, :] = u_ref[...]
    zbuf[0:HALO, :] = jnp.where(first, 0.0, ch_ref[...] * hh_ref[...])
    zbuf[HALO:HALO + tb, :] = c_ref[...] * h_ref[...]

    t_pos = (pl.program_id(0) % n_tblk) * tb + lax.broadcasted_iota(I32, (tb, LANES), 0)
    gw = u_ref.shape[1] // len(POOL_WINDOWS)
    for grp, win in enumerate(POOL_WINDOWS):
        lanes = slice(grp * gw, (grp + 1) * gw)
        cur = ubuf[HALO:HALO + tb, lanes]
        acc = cur
        for lag in range(1, win):
            acc = acc + ubuf[HALO - lag:HALO - lag + tb, lanes]
        count = jnp.minimum(t_pos + 1, win).astype(F32)
        mixed = acc / count - cur
        y = jnp.dot(mixed.astype(BF16), pw_ref[grp], preferred_element_type=F32)
        yp_ref[:, lanes] = y * ps_ref[:, lanes]

    conv = cw_ref[0:1, :] * zbuf[HALO:HALO + tb, :]
    for lag in range(1, CONV_WIDTH):
        conv = conv + cw_ref[lag:lag + 1, :] * zbuf[HALO - lag:HALO - lag + tb, :]
    yc_ref[...] = b_ref[...] * conv


def _pool_conv(proj, pool_w, pool_scale, conv_w, *, seq, width, tb):
    t = proj.shape[0]
    n_tblk = seq // tb
    ratio = tb // HALO

    def cur(col):
        return pl.BlockSpec((tb, width), lambda g: (g, col))

    def halo(col):
        return pl.BlockSpec((HALO, width), lambda g: (jnp.maximum(g * ratio - 1, 0), col))

    full = lambda shape: pl.BlockSpec(shape, lambda g: (0,) * len(shape))
    return pl.pallas_call(
        functools.partial(_poolconv_kernel, tb=tb, n_tblk=n_tblk),
        grid=(t // tb,),
        in_specs=[cur(0), halo(0), cur(1), halo(1), cur(2), cur(3), halo(3),
                  full(pool_w.shape), full(pool_scale.shape), full(conv_w.shape)],
        out_specs=[pl.BlockSpec((tb, width), lambda g: (g, 0))] * 2,
        out_shape=[jax.ShapeDtypeStruct((t, width), F32)] * 2,
        scratch_shapes=[pltpu.VMEM((HALO + tb, width), F32)] * 2,
        compiler_params=_cp("parallel"),
    )(proj, proj, proj, proj, proj, proj, proj, pool_w, pool_scale, conv_w)


def _cum_rows(x, op, fill):
    n = x.shape[0]
    row = lax.broadcasted_iota(I32, x.shape, 0)
    shift = 1
    while shift < n:
        x = op(x, jnp.where(row >= shift, pltpu.roll(x, shift, axis=0), fill))
        shift *= 2
    return x


def _mlstm_kernel(q_ref, k_ref, v_ref, o_ref, g_ref, gb_ref, nw_ref, y_ref, c_scr, n_scr, m_scr,
                  *, chunk, heads):
    @pl.when(pl.program_id(1) == 0)
    def _():
        c_scr[...] = jnp.zeros_like(c_scr)
        n_scr[...] = jnp.zeros_like(n_scr)
        m_scr[...] = jnp.zeros_like(m_scr)

    tb = q_ref.shape[0]
    hd = HEAD_DIM
    scale = hd ** -0.5
    tri = (lax.broadcasted_iota(I32, (chunk, chunk), 0) >= lax.broadcasted_iota(I32, (chunk, chunk), 1))
    for c in range(tb // chunk):
        rows = slice(c * chunk, (c + 1) * chunk)
        gates = g_ref[rows, :] + gb_ref[...]
        lf = _log_sigmoid(pltpu.roll(gates, LANES - heads, axis=1))
        cumf = _cum_rows(lf, jnp.add, 0.0)
        a = gates - cumf
        m_prev = m_scr[...]
        mu = jnp.maximum(_cum_rows(a, jnp.maximum, -jnp.inf), m_prev)
        mu_last = mu[chunk - 1:chunk, :]
        a_t = a.T
        for h in range(heads):
            cols = slice(h * hd, (h + 1) * hd)
            q = q_ref[rows, cols]
            k = k_ref[rows, cols] * scale
            v = v_ref[rows, cols]
            qb, kb, vb = q.astype(BF16), k.astype(BF16), v.astype(BF16)
            mu_col = mu[:, h:h + 1]
            a_col = a[:, h:h + 1]
            m_prev_h = m_prev[:, h:h + 1]
            mu_last_h = mu_last[:, h:h + 1]
            dmat = jnp.exp(jnp.where(tri, a_t[h:h + 1, :] - mu_col, -jnp.inf))
            s = lax.dot_general(qb, kb, (((1,), (1,)), ((), ())), preferred_element_type=F32)
            p = dmat * s
            inter = jnp.exp(m_prev_h - mu_col)
            c_h = c_scr[h]
            n_h = n_scr[h]
            num = inter * jnp.dot(qb, c_h.astype(BF16), preferred_element_type=F32) \
                + jnp.dot(p.astype(BF16), vb, preferred_element_type=F32)
            den = inter * jnp.sum(q * n_h, axis=-1, keepdims=True) + jnp.sum(p, axis=-1, keepdims=True)
            floor = jnp.exp(-(cumf[:, h:h + 1] + mu_col))
            h_out = num / jnp.maximum(jnp.abs(den), floor)

            wg = jnp.exp(a_col - mu_last_h)
            decay = jnp.exp(m_prev_h - mu_last_h)
            kw = k * wg
            c_scr[h] = decay * c_h + lax.dot_general(kw.astype(BF16), vb, (((0,), (0,)), ((), ())),
                                                     preferred_element_type=F32)
            n_scr[h] = decay * n_h + jnp.sum(kw, axis=0, keepdims=True)

            gated = _sigmoid(o_ref[rows, cols]) * h_out
            y_ref[rows, cols] = _head_norm(gated, nw_ref[:, cols])
        m_scr[...] = cumf[chunk - 1:chunk, :] + mu_last


def _mlstm(proj, gates, gate_b, norm_w, *, batch, seq, width, col0):
    t = proj.shape[0]
    heads = width // HEAD_DIM
    n_tblk = seq // SCAN_BLOCK
    blk = lambda col: pl.BlockSpec((SCAN_BLOCK, width), lambda b, i: (b * n_tblk + i, col))
    return pl.pallas_call(
        functools.partial(_mlstm_kernel, chunk=SCAN_CHUNK, heads=heads),
        grid=(batch, n_tblk),
        in_specs=[blk(col0), blk(col0 + 1), blk(col0 + 2), blk(col0 + 3),
                  pl.BlockSpec((SCAN_BLOCK, LANES), lambda b, i: (b * n_tblk + i, 0)),
                  pl.BlockSpec((1, LANES), lambda b, i: (0, 0)),
                  pl.BlockSpec((1, width), lambda b, i: (0, 0))],
        out_specs=pl.BlockSpec((SCAN_BLOCK, width), lambda b, i: (b * n_tblk + i, 0)),
        out_shape=jax.ShapeDtypeStruct((t, width), F32),
        scratch_shapes=[pltpu.VMEM((heads, HEAD_DIM, HEAD_DIM), F32),
                        pltpu.VMEM((heads, 1, HEAD_DIM), F32),
                        pltpu.VMEM((1, LANES), F32)],
        compiler_params=_cp("parallel", "arbitrary"),
    )(proj, proj, proj, proj, gates, gate_b, norm_w)


def _ret_kernel(q_ref, k_ref, v_ref, g_ref, cos_ref, sin_ref, intra_ref, cross_ref, zeta_ref, nw_ref,
                y_ref, r_scr, *, chunk, heads, chunk_decay):
    @pl.when(pl.program_id(1) == 0)
    def _():
        r_scr[...] = jnp.zeros_like(r_scr)

    tb = q_ref.shape[0]
    hd = HEAD_DIM
    scale = hd ** -0.5
    for c in range(tb // chunk):
        rows = slice(c * chunk, (c + 1) * chunk)
        cos = cos_ref[rows, :]
        sin = sin_ref[rows, :]
        for h in range(heads):
            cols = slice(h * hd, (h + 1) * hd)
            q = q_ref[rows, cols]
            k = k_ref[rows, cols]
            q = q * cos + pltpu.roll(q, hd // 2, axis=1) * sin
            k = (k * cos + pltpu.roll(k, hd // 2, axis=1) * sin) * scale
            vb = v_ref[rows, cols].astype(BF16)
            qb = q.astype(BF16)
            s = lax.dot_general(qb, k.astype(BF16), (((1,), (1,)), ((), ())), preferred_element_type=F32)
            inner = jnp.dot((s * intra_ref[h]).astype(BF16), vb, preferred_element_type=F32)
            r_h = r_scr[h]
            crs = jnp.dot(qb, r_h.astype(BF16), preferred_element_type=F32) * cross_ref[h]
            r_scr[h] = chunk_decay[h] * r_h + lax.dot_general(
                (k * zeta_ref[h]).astype(BF16), vb, (((0,), (0,)), ((), ())), preferred_element_type=F32)
            y_ref[rows, cols] = _silu(g_ref[rows, cols]) * _head_norm(inner + crs, nw_ref[:, cols])


def _retention(proj, norm_w, *, batch, seq, width, col0):
    t = proj.shape[0]
    heads = width // HEAD_DIM
    chunk = SCAN_CHUNK
    n_tblk = seq // SCAN_BLOCK
    half = HEAD_DIM // 2
    inv = np.float32(ROPE_BASE) ** (-np.arange(half, dtype=np.float32) / np.float32(half))
    ang = (np.arange(seq, dtype=np.float32)[:, None] * inv[None, :]).astype(np.float64)
    cos_t = jnp.asarray(np.concatenate([np.cos(ang), np.cos(ang)], axis=-1), F32)
    sin_t = jnp.asarray(np.concatenate([-np.sin(ang), np.sin(ang)], axis=-1), F32)
    log_g = jnp.log(1.0 - 2.0 ** (-5.0 - jnp.arange(heads, dtype=F32)))
    tt = jnp.arange(chunk, dtype=F32)
    lag = tt[:, None] - tt[None, :]
    intra = jnp.where(lag >= 0, jnp.exp(jnp.maximum(lag, 0.0)[None] * log_g[:, None, None]), 0.0)
    cross = jnp.broadcast_to(jnp.exp((tt + 1.0)[None, :] * log_g[:, None])[:, :, None], (heads, chunk, HEAD_DIM))
    zeta = jnp.broadcast_to(jnp.exp((chunk - 1.0 - tt)[None, :] * log_g[:, None])[:, :, None],
                            (heads, chunk, HEAD_DIM))
    chunk_decay = tuple(float((1.0 - 2.0 ** (-5.0 - h)) ** chunk) for h in range(heads))

    blk = lambda col: pl.BlockSpec((SCAN_BLOCK, width), lambda b, i: (b * n_tblk + i, col))
    pos = pl.BlockSpec((SCAN_BLOCK, HEAD_DIM), lambda b, i: (i, 0))
    full3 = lambda a: pl.BlockSpec(a.shape, lambda b, i: (0, 0, 0))
    return pl.pallas_call(
        functools.partial(_ret_kernel, chunk=chunk, heads=heads, chunk_decay=chunk_decay),
        grid=(batch, n_tblk),
        in_specs=[blk(col0), blk(col0 + 1), blk(col0 + 2), blk(col0 + 3), pos, pos,
                  full3(intra), full3(cross), full3(zeta),
                  pl.BlockSpec((1, width), lambda b, i: (0, 0))],
        out_specs=pl.BlockSpec((SCAN_BLOCK, width), lambda b, i: (b * n_tblk + i, 0)),
        out_shape=jax.ShapeDtypeStruct((t, width), F32),
        scratch_shapes=[pltpu.VMEM((heads, HEAD_DIM, HEAD_DIM), F32)],
        compiler_params=_cp("parallel", "arbitrary"),
    )(proj, proj, proj, proj, cos_t, sin_t, intra, cross, zeta, norm_w)


def _merge_kernel(x_ref, *refs):
    y_refs, wg_refs = refs[:N_BRANCH], refs[N_BRANCH:2 * N_BRANCH]
    wb_ref, o_ref, xb_ref = refs[2 * N_BRANCH:]

    @pl.when(pl.program_id(1) == 0)
    def _():
        xb_ref[...] = x_ref[...].astype(BF16)

    xb = xb_ref[...]
    acc = None
    for n in range(N_BRANCH):
        gate = _sigmoid(jnp.dot(xb, wg_refs[n][...], preferred_element_type=F32))
        term = gate * jnp.dot(y_refs[n][...].astype(BF16), wb_ref[n], preferred_element_type=F32)
        acc = term if acc is None else acc + term
    o_ref[...] = acc


def _merge(x, branches, w_gate, gate_col0, w_branch, *, tm, tn):
    t, d = x.shape
    width = branches[0].shape[1]
    ybs = pl.BlockSpec((tm, width), lambda i, j: (i, 0))

    def gate_spec(n):
        return pl.BlockSpec((d, tn), lambda i, j: (0, (gate_col0 + n * d) // tn + j))

    return pl.pallas_call(
        _merge_kernel,
        grid=(t // tm, d // tn),
        in_specs=[pl.BlockSpec((tm, d), lambda i, j: (i, 0))] + [ybs] * N_BRANCH
                 + [gate_spec(n) for n in range(N_BRANCH)]
                 + [pl.BlockSpec((N_BRANCH, width, tn), lambda i, j: (0, 0, j))],
        out_specs=pl.BlockSpec((tm, tn), lambda i, j: (i, j)),
        out_shape=jax.ShapeDtypeStruct((t, d), F32),
        scratch_shapes=[pltpu.VMEM((tm, d), BF16)],
        compiler_params=_cp("parallel", "arbitrary"),
    )(x, *branches, *([w_gate] * N_BRANCH), w_branch)


def _proj_ln_kernel(a_ref, w_ref, r_ref, lw_ref, lb_ref, o_ref, *, alpha):
    y = jnp.dot(a_ref[...].astype(BF16), w_ref[...], preferred_element_type=F32)
    o_ref[...] = _layer_norm(alpha * r_ref[...] + y, lw_ref[...], lb_ref[...])


def _proj_ln(a, w, resid, ln_w, ln_b, *, alpha, tm):
    t, k = a.shape
    d = w.shape[1]
    row = lambda n: pl.BlockSpec((tm, n), lambda i: (i, 0))
    const = lambda shape: pl.BlockSpec(shape, lambda i: (0, 0))
    return pl.pallas_call(
        functools.partial(_proj_ln_kernel, alpha=alpha),
        grid=(t // tm,),
        in_specs=[row(k), const((k, d)), row(d), const((1, d)), const((1, d))],
        out_specs=row(d),
        out_shape=jax.ShapeDtypeStruct((t, d), F32),
        compiler_params=_cp("parallel"),
    )(a, w, resid, ln_w, ln_b)


def _xattn_kernel(x_ref, wq_ref, kv_ref, wo_ref, lw_ref, lb_ref, o_ref, *, alpha, heads):
    x = x_ref[...]
    hd = HEAD_DIM
    inner = heads * hd
    q = jnp.dot(x.astype(BF16), wq_ref[...], preferred_element_type=F32)
    outs = []
    for h in range(heads):
        qh = q[:, h * hd:(h + 1) * hd].astype(BF16)
        kh = kv_ref[:, h * hd:(h + 1) * hd].astype(BF16)
        vh = kv_ref[:, inner + h * hd:inner + (h + 1) * hd].astype(BF16)
        s = lax.dot_general(qh, kh, (((1,), (1,)), ((), ())), preferred_element_type=F32) * hd ** -0.5
        s = s - jnp.max(s, axis=-1, keepdims=True)
        e = jnp.exp(s)
        p = e / jnp.sum(e, axis=-1, keepdims=True)
        outs.append(jnp.dot(p.astype(BF16), vh, preferred_element_type=F32).astype(BF16))
    o = jnp.concatenate(outs, axis=-1)
    y = jnp.dot(o, wo_ref[...], preferred_element_type=F32)
    o_ref[...] = _layer_norm(alpha * x + y, lw_ref[...], lb_ref[...])


def _xattn(x, kv, wq, wo, ln_w, ln_b, *, alpha, seq, mem_len, tm):
    t, d = x.shape
    inner = wq.shape[1]
    n_tblk = seq // tm
    const = lambda shape: pl.BlockSpec(shape, lambda i: (0, 0))
    return pl.pallas_call(
        functools.partial(_xattn_kernel, alpha=alpha, heads=XATTN_HEADS),
        grid=(t // tm,),
        in_specs=[pl.BlockSpec((tm, d), lambda i: (i, 0)), const((d, inner)),
                  pl.BlockSpec((mem_len, 2 * inner), lambda i: (i // n_tblk, 0)),
                  const((inner, d)), const((1, d)), const((1, d))],
        out_specs=pl.BlockSpec((tm, d), lambda i: (i, 0)),
        out_shape=jax.ShapeDtypeStruct((t, d), F32),
        compiler_params=_cp("parallel"),
    )(x, wq, kv, wo, ln_w, ln_b)


def _round_up_pow2(x, m):
    shift = m.bit_length() - 1
    return jnp.left_shift(jnp.right_shift(x + (m - 1), shift), shift)


def _route_kernel(x_ref, wr_ref, rb_ref, w_ref, lrow_ref, segtab_ref, seg_ref, size_all, tot, *, tm):
    step = pl.program_id(0)

    @pl.when(step == 0)
    def _():
        tot[...] = jnp.zeros_like(tot)

    e_n, g_n = N_EXPERTS, N_GROUPS
    per = e_n // g_n
    logits = lax.dot_general(wr_ref[...], x_ref[...], (((1,), (1,)), ((), ())),
                             precision=lax.Precision.HIGHEST, preferred_element_type=F32)
    scores = _sigmoid(logits)
    biased = scores + rb_ref[...]
    b3 = biased.reshape(g_n, per, tm)
    member = lax.broadcasted_iota(I32, (g_n, per, tm), 1)
    top1 = jnp.max(b3, axis=1, keepdims=True)
    first = jnp.min(jnp.where(b3 == top1, member, per), axis=1, keepdims=True)
    top2 = jnp.max(jnp.where(member == first, -jnp.inf, b3), axis=1, keepdims=True)
    gs = top1 + top2
    gid = lax.broadcasted_iota(I32, (g_n, 1, tm), 0)
    rank = jnp.zeros((g_n, 1, tm), I32)
    for other in range(g_n):
        o = gs[other:other + 1]
        ahead = jnp.logical_or(o > gs, jnp.logical_and(o == gs, other < gid))
        rank = rank + jnp.where(ahead, 1, 0)
    cur = jnp.where(rank < TOPK_GROUPS, b3, -jnp.inf).reshape(e_n, tm)

    eid = lax.broadcasted_iota(I32, (e_n, tm), 0)
    picks, vals = [], []
    sel = jnp.zeros((e_n, tm), F32)
    for k in range(TOP_K):
        mx = jnp.max(cur, axis=0, keepdims=True)
        ik = jnp.min(jnp.where(cur == mx, eid, e_n), axis=0, keepdims=True)
        hit = eid == ik
        vals.append(jnp.sum(jnp.where(hit, scores, 0.0), axis=0, keepdims=True))
        cur = jnp.where(hit, -jnp.inf, cur)
        sel = jnp.where(hit, 1.0, sel)
        picks.append(ik)
    total = vals[0]
    for v in vals[1:]:
        total = total + v

    tri = jnp.where(lax.broadcasted_iota(I32, (tm, tm), 0) <= lax.broadcasted_iota(I32, (tm, tm), 1), 1.0, 0.0)
    incl = jnp.dot(sel.astype(BF16), tri.astype(BF16), preferred_element_type=F32)
    size = _round_up_pow2(jnp.broadcast_to(incl[:, tm - 1:tm], (e_n, LANES)).astype(I32), SEG_ALIGN)
    loff = _cum_rows(size, jnp.add, 0) - size
    base = loff[:, 0:1].astype(F32) + incl - 1.0
    for k in range(TOP_K):
        w_ref[k:k + 1, :] = vals[k] / total * ROUTE_SCALE
        lrow_ref[0, k:k + 1, :] = jnp.sum(jnp.where(eid == picks[k], base, 0.0),
                                          axis=0, keepdims=True).astype(I32)
    size_all[step] = size
    tot[...] = tot[...] + size

    @pl.when(step == pl.num_programs(0) - 1)
    def _():
        rows = tot[...]
        start = _cum_rows(rows, jnp.add, 0) - rows

        def tile_seg(i, run):
            segtab_ref[i, 0] = run
            segtab_ref[i, 1] = size_all[i]
            return run + size_all[i]

        lax.fori_loop(0, pl.num_programs(0), tile_seg, start)
        seg_ref[0] = start
        seg_ref[1] = rows


def _route(x, router_w_t, router_b, *, tm):
    t, d = x.shape
    e_n = N_EXPERTS
    n_t = t // tm
    return pl.pallas_call(
        functools.partial(_route_kernel, tm=tm),
        grid=(n_t,),
        in_specs=[pl.BlockSpec((tm, d), lambda i: (i, 0)),
                  pl.BlockSpec((e_n, d), lambda i: (0, 0)),
                  pl.BlockSpec((e_n, 1), lambda i: (0, 0))],
        out_specs=[pl.BlockSpec((TOP_K, tm), lambda i: (0, i)),
                   pl.BlockSpec((1, TOP_K, tm), lambda i: (i, 0, 0)),
                   pl.BlockSpec((n_t, 2, e_n, LANES), lambda i: (0, 0, 0, 0)),
                   pl.BlockSpec((2, e_n, LANES), lambda i: (0, 0, 0))],
        out_shape=[jax.ShapeDtypeStruct((TOP_K, t), F32), jax.ShapeDtypeStruct((n_t, TOP_K, tm), I32),
                   jax.ShapeDtypeStruct((n_t, 2, e_n, LANES), I32), jax.ShapeDtypeStruct((2, e_n, LANES), I32)],
        scratch_shapes=[pltpu.VMEM((n_t, e_n, LANES), I32), pltpu.VMEM((e_n, LANES), I32)],
        compiler_params=_cp("arbitrary"),
    )(x, router_w_t, router_b)


def _tile_rows(tm):
    worst = TOP_K * tm + N_EXPERTS * (SEG_ALIGN - 1)
    return -(-worst // ONEHOT_ROWS) * ONEHOT_ROWS


def _onehot_rows(chunk, lrow, values, tm):
    rid = chunk * ONEHOT_ROWS + lax.broadcasted_iota(I32, (ONEHOT_ROWS, tm), 0)
    acc = jnp.zeros((ONEHOT_ROWS, tm), F32)
    for k in range(TOP_K):
        acc = jnp.where(rid == lrow[k:k + 1, :], 1.0 if values is None else values[k:k + 1, :], acc)
    return acc.astype(BF16)


def _segments(gstart_ref, size_ref, tile, make_copy, act, keep=None):
    def body(e, loff):
        n = size_ref[tile * N_EXPERTS + e]
        wanted = n > 0 if keep is None else jnp.logical_and(n > 0, keep(loff + n))

        @pl.when(wanted)
        def _():
            act(make_copy(pl.multiple_of(gstart_ref[tile * N_EXPERTS + e], SEG_ALIGN),
                          pl.multiple_of(loff, SEG_ALIGN), pl.multiple_of(n, SEG_ALIGN)))
        return loff + n
    return lax.fori_loop(0, N_EXPERTS, body, 0)


def _start(cp):
    cp.start()


def _wait(cp):
    cp.wait()


def _dispatch_kernel(gstart_ref, size_ref, x_ref, lrow_ref, xs_ref, stage, sem_a, sem_b, *, tm):
    i = pl.program_id(0)
    last = pl.num_programs(0) - 1
    n_chunks = stage.shape[0] // ONEHOT_ROWS
    split_chunk = n_chunks // 2
    split = split_chunk * ONEHOT_ROWS
    assert tm <= ONEHOT_ROWS

    def seg_walk(tile, act, phase_b):
        sem = sem_b if phase_b else sem_a
        return _segments(
            gstart_ref, size_ref, tile,
            lambda g, loff, n: pltpu.make_async_copy(stage.at[pl.ds(loff, n), :], xs_ref.at[pl.ds(g, n), :], sem),
            act, (lambda end: end > split) if phase_b else (lambda end: end <= split))

    xb = x_ref[...].astype(BF16)
    lrow = lrow_ref[0]
    n_rows = lax.fori_loop(0, N_EXPERTS, lambda e, s: s + size_ref[i * N_EXPERTS + e], 0)

    def chunks(lo, hi):
        for c in range(lo, hi):
            def one(c=c):
                stage[c * ONEHOT_ROWS:(c + 1) * ONEHOT_ROWS, :] = jnp.dot(
                    _onehot_rows(c, lrow, None, tm), xb, preferred_element_type=F32).astype(BF16)
            if c * ONEHOT_ROWS < TOP_K * tm:
                one()
            else:
                pl.when(c * ONEHOT_ROWS < n_rows)(one)

    prev = jnp.maximum(i - 1, 0)

    @pl.when(i > 0)
    def _():
        seg_walk(prev, _wait, False)
    chunks(0, split_chunk - 1)

    @pl.when(i > 0)
    def _():
        seg_walk(prev, _wait, True)
    chunks(split_chunk - 1, split_chunk)
    seg_walk(i, _start, False)
    chunks(split_chunk, n_chunks)
    seg_walk(i, _start, True)

    @pl.when(i == last)
    def _():
        seg_walk(i, _wait, False)
        seg_walk(i, _wait, True)


def _dispatch(x, lrow, gstart, size, *, rows, tm):
    t, d = x.shape
    return pl.pallas_call(
        functools.partial(_dispatch_kernel, tm=tm),
        grid_spec=pltpu.PrefetchScalarGridSpec(
            num_scalar_prefetch=2,
            grid=(t // tm,),
            in_specs=[pl.BlockSpec((tm, d), lambda i, *_: (i, 0)),
                      pl.BlockSpec((1, TOP_K, tm), lambda i, *_: (i, 0, 0))],
            out_specs=pl.BlockSpec(memory_space=pl.ANY),
            scratch_shapes=[pltpu.VMEM((_tile_rows(tm), d), BF16)] + [pltpu.SemaphoreType.DMA(())] * 2,
        ),
        out_shape=jax.ShapeDtypeStruct((rows, d), BF16),
        compiler_params=_cp("arbitrary"),
    )(gstart, size, x, lrow)


def _expert_kernel(start_ref, rows_ref, wgu_ref, wdn_ref, xs_ref, ys_ref, wgu_b, wdn_b, xbuf, ybuf, sem_in, sem_out,
                   *, bm):
    e = pl.program_id(0)
    row0 = start_ref[e]
    n = rows_ref[e]
    n_blk = (n + bm - 1) // bm

    @pl.when(e == 0)
    def _():
        xbuf[...] = jnp.zeros_like(xbuf)

    def row_dma(b, slot, act, load):
        off = pl.multiple_of(row0 + b * bm, SEG_ALIGN)
        left = n - b * bm

        def copy(r):
            if load:
                return pltpu.make_async_copy(xs_ref.at[pl.ds(off, r), :], xbuf.at[slot, pl.ds(0, r), :],
                                             sem_in.at[slot])
            return pltpu.make_async_copy(ybuf.at[slot, pl.ds(0, r), :], ys_ref.at[pl.ds(off, r), :],
                                         sem_out.at[slot])

        @pl.when(left >= bm)
        def _():
            act(copy(bm))

        @pl.when(left < bm)
        def _():
            act(copy(pl.multiple_of(left, SEG_ALIGN)))

    def start(cp):
        cp.start(priority=ROW_DMA_PRIORITY)

    @pl.when(n_blk > 0)
    def _():
        row_dma(0, 0, start, True)

    wgu_b[...] = wgu_ref[...].astype(BF16)
    wdn_b[...] = wdn_ref[...].astype(BF16)

    def block(b, carry):
        slot = b % 2

        @pl.when(b + 1 < n_blk)
        def _():
            row_dma(b + 1, 1 - slot, start, True)

        row_dma(b, slot, _wait, True)

        @pl.when(b >= 2)
        def _():
            row_dma(b - 2, slot, _wait, False)

        f = wdn_b.shape[0]
        gu = jnp.dot(xbuf[slot], wgu_b[...], preferred_element_type=F32)
        hidden = (_silu(gu[:, :f]) * gu[:, f:]).astype(BF16)
        ybuf[slot] = jnp.dot(hidden, wdn_b[...], preferred_element_type=F32).astype(BF16)
        row_dma(b, slot, start, False)
        return carry

    lax.fori_loop(0, n_blk, block, 0)

    for back in (2, 1):
        @pl.when(n_blk >= back)
        def _():
            b = n_blk - back
            row_dma(b, b % 2, _wait, False)


def _experts(xs, start, rows, w_gu, w_dn, layer, *, bm):
    total, d = xs.shape
    f2 = w_gu.shape[3]
    f = w_dn.shape[2]
    return pl.pallas_call(
        functools.partial(_expert_kernel, bm=bm),
        grid_spec=pltpu.PrefetchScalarGridSpec(
            num_scalar_prefetch=2,
            grid=(N_EXPERTS,),
            in_specs=[pl.BlockSpec((None, None, d, f2), lambda e, *_: (layer, e, 0, 0)),
                      pl.BlockSpec((None, None, f, d), lambda e, *_: (layer, e, 0, 0)),
                      pl.BlockSpec(memory_space=pl.ANY)],
            out_specs=pl.BlockSpec(memory_space=pl.ANY),
            scratch_shapes=[pltpu.VMEM((d, f2), BF16), pltpu.VMEM((f, d), BF16),
                            pltpu.VMEM((2, bm, d), BF16), pltpu.VMEM((2, bm, d), BF16),
                            pltpu.SemaphoreType.DMA((2,)), pltpu.SemaphoreType.DMA((2,))],
        ),
        out_shape=jax.ShapeDtypeStruct((total, d), BF16),
        compiler_params=_cp("arbitrary"),
    )(start, rows, w_gu, w_dn, xs)


def _combine_kernel(gstart_ref, size_ref, x_ref, lrow_ref, w_ref, sdn_ref, lw_ref, lb_ref, sgu_hbm, ys_ref,
                    o_ref, ybuf, wt, hid, sgu, sems, wsem, *, tm, alpha):
    i, half = pl.program_id(0), pl.program_id(1)
    last_tile = pl.num_programs(0) - 1
    dh = ybuf.shape[2]
    n_chunks = ybuf.shape[1] // ONEHOT_ROWS
    sure_chunks = (TOP_K * tm) // ONEHOT_ROWS

    def contract_rows(w_rows, y_rows):
        return lax.dot_general(w_rows, y_rows, (((0,), (0,)), ((), ())), preferred_element_type=F32)

    def seg_walk(tile, hf, act):
        return _segments(
            gstart_ref, size_ref, tile,
            lambda g, loff, n: pltpu.make_async_copy(ys_ref.at[pl.ds(g, n), hf * dh:(hf + 1) * dh],
                                                     ybuf.at[hf, pl.ds(loff, n), :], sems.at[hf]),
            act)

    @pl.when(jnp.logical_and(i == 0, half == 0))
    def _():
        ybuf[...] = jnp.zeros_like(ybuf)
        weights = pltpu.make_async_copy(sgu_hbm, sgu, wsem)
        weights.start()
        weights.wait()
        seg_walk(0, 0, _start)

    @pl.when(half == 0)
    def _():
        seg_walk(i, 1, _start)
        f = hid.shape[1]
        gu = jnp.dot(x_ref[...].astype(BF16), sgu[...], preferred_element_type=F32)
        hidden = (_silu(gu[:, :f]) * gu[:, f:]).astype(BF16)
        hid[...] = hidden
        n_rows = seg_walk(i, 0, _wait)
        lrow, w = lrow_ref[0], w_ref[...]
        acc = jnp.dot(hidden, sdn_ref[...], preferred_element_type=F32)
        for c in range(sure_chunks):
            rows = slice(c * ONEHOT_ROWS, (c + 1) * ONEHOT_ROWS)
            w_rows = _onehot_rows(c, lrow, w, tm)
            wt[rows, :] = w_rows
            acc = acc + contract_rows(w_rows, ybuf[0, rows, :])
        o_ref[:, 0:dh] = acc
        for c in range(sure_chunks, n_chunks):
            @pl.when(c * ONEHOT_ROWS < n_rows)
            def _():
                rows = slice(c * ONEHOT_ROWS, (c + 1) * ONEHOT_ROWS)
                w_rows = _onehot_rows(c, lrow, w, tm)
                wt[rows, :] = w_rows
                o_ref[:, 0:dh] += contract_rows(w_rows, ybuf[0, rows, :])

    @pl.when(half == 1)
    def _():
        @pl.when(i < last_tile)
        def _():
            seg_walk(i + 1, 0, _start)
        n_rows = seg_walk(i, 1, _wait)
        sure = slice(0, sure_chunks * ONEHOT_ROWS)
        o_ref[:, dh:2 * dh] = contract_rows(wt[sure, :], ybuf[1, sure, :]) \
            + jnp.dot(hid[...], sdn_ref[...], preferred_element_type=F32)
        for c in range(sure_chunks, n_chunks):
            @pl.when(c * ONEHOT_ROWS < n_rows)
            def _():
                rows = slice(c * ONEHOT_ROWS, (c + 1) * ONEHOT_ROWS)
                o_ref[:, dh:2 * dh] += contract_rows(wt[rows, :], ybuf[1, rows, :])
        o_ref[...] = _layer_norm(alpha * x_ref[...] + o_ref[...], lw_ref[...], lb_ref[...])


def _combine(x, ys, lrow, wts, gstart, size, s_gu, s_dn, ln_w, ln_b, *, alpha, tm):
    t, d = x.shape
    dh = d // 2
    f = s_dn.shape[0]
    const = lambda shape: pl.BlockSpec(shape, lambda i, h, *_: (0, 0))
    return pl.pallas_call(
        functools.partial(_combine_kernel, tm=tm, alpha=alpha),
        grid_spec=pltpu.PrefetchScalarGridSpec(
            num_scalar_prefetch=2,
            grid=(t // tm, 2),
            in_specs=[pl.BlockSpec((tm, d), lambda i, h, *_: (i, 0)),
                      pl.BlockSpec((1, TOP_K, tm), lambda i, h, *_: (i, 0, 0)),
                      pl.BlockSpec((TOP_K, tm), lambda i, h, *_: (0, i)),
                      pl.BlockSpec((f, dh), lambda i, h, *_: (0, h)),
                      const((1, d)), const((1, d)),
                      pl.BlockSpec(memory_space=pl.ANY), pl.BlockSpec(memory_space=pl.ANY)],
            out_specs=pl.BlockSpec((tm, d), lambda i, h, *_: (i, 0)),
            scratch_shapes=[pltpu.VMEM((2, _tile_rows(tm), dh), BF16), pltpu.VMEM((_tile_rows(tm), tm), BF16),
                            pltpu.VMEM((tm, f), BF16), pltpu.VMEM(s_gu.shape, BF16),
                            pltpu.SemaphoreType.DMA((2,)), pltpu.SemaphoreType.DMA(())],
        ),
        out_shape=jax.ShapeDtypeStruct((t, d), F32),
        compiler_params=_cp("arbitrary", "arbitrary"),
    )(gstart, size, x, lrow, wts, s_dn, ln_w, ln_b, s_gu, ys)


def _mixer_sublayer(x, w_in_all, layer, gate_b, pool_w, pool_scale, conv_w, mlstm_norm_w, ret_norm_w, w_branch,
                    w_out, ln_w, ln_b, *, batch, seq, alpha):
    t, d = x.shape
    width = d // N_BRANCH
    heads = width // HEAD_DIM
    gate_off = 8 * width
    ret_off = gate_off
    g_off = ret_off + 4 * width
    if_off = g_off + N_BRANCH * d
    w_bf16 = _realign_cast(w_in_all, layer, lo_col=gate_off, hi_col=if_off, shift=2 * heads, tr=512, tn=512)
    gate_bias = jnp.pad(gate_b, (0, LANES - 2 * heads)).reshape(1, LANES)

    proj_a = _matmul(x, w_bf16, tm=1024, tn=1024, ncols=gate_off)
    proj_b = _matmul(x, w_bf16, tm=1024, tn=1024, ncols=4 * width, col0=ret_off)
    gates = _matmul(x, w_bf16, tm=1024, tn=LANES, ncols=LANES, col0=if_off)
    y_pool, y_conv = _pool_conv(proj_a, pool_w.astype(BF16), pool_scale.reshape(1, width), conv_w,
                                seq=seq, width=width, tb=512)
    y_mlstm = _mlstm(proj_a, gates, gate_bias, mlstm_norm_w.reshape(1, width),
                     batch=batch, seq=seq, width=width, col0=4)
    y_ret = _retention(proj_b, ret_norm_w.reshape(1, width), batch=batch, seq=seq, width=width, col0=0)
    merged = _merge(x, (y_pool, y_conv, y_mlstm, y_ret), w_bf16, g_off, w_branch.astype(BF16),
                    tm=512, tn=512)
    return _proj_ln(merged, w_out.astype(BF16), x, ln_w, ln_b, alpha=alpha, tm=512)


def _xattn_sublayer(x, mem2d, wq, wk, wv, wo, ln_w, ln_b, *, seq, mem_len, alpha):
    w_kv = jnp.concatenate([wk, wv], axis=1).astype(BF16)
    kv = _matmul(mem2d, w_kv, tm=min(mem2d.shape[0], 1024), tn=512)
    return _xattn(x, kv, wq.astype(BF16), wo.astype(BF16), ln_w, ln_b,
                  alpha=alpha, seq=seq, mem_len=mem_len, tm=512)


def _moe_sublayer(x, router_w, router_b, w_gu, w_dn, layer, s_gu, s_dn, ln_w, ln_b, *, alpha):
    t, d = x.shape
    e_n = N_EXPERTS
    tm = ROUTE_TM
    rows = t * TOP_K + (t // tm) * e_n * (SEG_ALIGN - 1)
    wts, lrow, segtab, seg = _route(x, router_w.T, router_b.reshape(e_n, 1), tm=tm)
    gstart, size = segtab[:, 0, :, 0].reshape(-1), segtab[:, 1, :, 0].reshape(-1)

    xs = _dispatch(x, lrow, gstart, size, rows=rows, tm=tm)
    ys = _experts(xs, seg[0, :, 0], seg[1, :, 0], w_gu, w_dn, layer, bm=MOE_BM)
    return _combine(x, ys, lrow, wts, gstart, size, s_gu.astype(BF16), s_dn.astype(BF16), ln_w, ln_b,
                    alpha=alpha, tm=tm)


def kernel(x, mem, w_in, mlstm_gate_b, pool_w, pool_scale, conv_w, mlstm_norm_w, ret_norm_w, w_branch,
           w_mix_out, xa_wq, xa_wk, xa_wv, xa_wo, router_w, router_b, moe_w_gu, moe_w_dn, shared_w_gu,
           shared_w_dn, ln_w, ln_b):
    batch, seq, d = x.shape
    depth = w_in.shape[0]
    mem_len = mem.shape[1]
    alpha = (2 * depth) ** 0.25
    h = x.reshape(batch * seq, d)
    mem2d = mem.reshape(batch * mem_len, d)
    for l in range(depth):
        lw = ln_w[l].reshape(3, 1, d)
        lb = ln_b[l].reshape(3, 1, d)
        h = _mixer_sublayer(h, w_in, l, mlstm_gate_b[l], pool_w[l], pool_scale[l], conv_w[l], mlstm_norm_w[l],
                            ret_norm_w[l], w_branch[l], w_mix_out[l], lw[0], lb[0],
                            batch=batch, seq=seq, alpha=alpha)
        h = _xattn_sublayer(h, mem2d, xa_wq[l], xa_wk[l], xa_wv[l], xa_wo[l], lw[1], lb[1],
                            seq=seq, mem_len=mem_len, alpha=alpha)
        h = _moe_sublayer(h, router_w[l], router_b[l], moe_w_gu, moe_w_dn, l, shared_w_gu[l], shared_w_dn[l],
                          lw[2], lb[2], alpha=alpha)
    return h.reshape(batch, seq, d)
```

```python
import functools

import numpy as np
import jax
import jax.numpy as jnp
from jax import lax
from jax.experimental import pallas as pl
from jax.experimental.pallas import tpu as pltpu

F32 = jnp.float32
BF16 = jnp.bfloat16
I32 = jnp.int32

N_BRANCH = 4
HEAD_DIM = 128
POOL_WINDOWS = (2, 4, 8, 16)
CONV_WIDTH = 3
ROPE_BASE = 10000.0
XATTN_HEADS = 4
N_EXPERTS = 64
TOP_K = 8
N_GROUPS = 8
TOPK_GROUPS = 4
ROUTE_SCALE = 2.5
LN_EPS = 1e-5

LANES = 128
V7X_VMEM_BYTES = 64 * 1024 * 1024
VMEM_LIMIT = 56 * 1024 * 1024

SCAN_CHUNK = 256
SCAN_BLOCK = 512
SCAN_BATCH = 2
HALO = 16
MOE_BM = 512
ROUTE_TM = 256
SEG_ALIGN = 16
ONEHOT_ROWS = 512
ROW_DMA_PRIORITY = 1


def _cp(*sem):
    return pltpu.CompilerParams(dimension_semantics=sem, vmem_limit_bytes=VMEM_LIMIT)


def _sigmoid(x):
    return 1.0 / (1.0 + jnp.exp(-x))


def _silu(x):
    return x * _sigmoid(x)


def _log_sigmoid(x):
    return jnp.minimum(x, 0.0) - jnp.log(1.0 + jnp.exp(-jnp.abs(x)))


def _layer_norm(z, w, b):
    mu = jnp.mean(z, axis=-1, keepdims=True)
    d = z - mu
    var = jnp.mean(d * d, axis=-1, keepdims=True)
    return d * lax.rsqrt(var + LN_EPS) * w + b


def _head_norm(h, w):
    mu = jnp.mean(h, axis=-1, keepdims=True)
    d = h - mu
    var = jnp.mean(d * d, axis=-1, keepdims=True)
    return d * lax.rsqrt(var + LN_EPS) * w


def _mm_kernel(x_ref, w_ref, o_ref, xb_ref):
    @pl.when(pl.program_id(1) == 0)
    def _():
        xb_ref[...] = x_ref[...].astype(BF16)

    o_ref[...] = jnp.dot(xb_ref[...], w_ref[...], preferred_element_type=F32)


def _realign_cast_kernel(a_ref, b_ref, o_ref, *, shift, lo, hi):
    j = pl.program_id(1)
    tn = o_ref.shape[1]
    shifted = jnp.logical_and(j >= lo, j < hi)

    @pl.when(shifted)
    def _():
        both = jnp.concatenate([a_ref[...], b_ref[...]], axis=1)
        o_ref[...] = both[:, shift:shift + tn].astype(BF16)

    @pl.when(jnp.logical_not(shifted))
    def _():
        o_ref[...] = a_ref[...].astype(BF16)


def _realign_cast(w_all, layer, *, lo_col, hi_col, shift, tr, tn):
    _, rows, _ = w_all.shape
    lo, hi = lo_col // tn, hi_col // tn
    src = lambda j: jnp.where(j == hi, lo, j)
    return pl.pallas_call(
        functools.partial(_realign_cast_kernel, shift=shift, lo=lo, hi=hi),
        grid=(rows // tr, hi + 1),
        in_specs=[pl.BlockSpec((None, tr, tn), lambda i, j: (layer, i, src(j))),
                  pl.BlockSpec((None, tr, LANES), lambda i, j: (layer, i, (src(j) + 1) * (tn // LANES)))],
        out_specs=pl.BlockSpec((tr, tn), lambda i, j: (i, j)),
        out_shape=jax.ShapeDtypeStruct((rows, hi_col + tn), BF16),
        compiler_params=_cp("parallel", "parallel"),
    )(w_all, w_all)


def _matmul(x, w, *, tm, tn, ncols=None, col0=0):
    t, k = x.shape
    n = w.shape[1] if ncols is None else ncols
    return pl.pallas_call(
        _mm_kernel,
        grid=(t // tm, n // tn),
        in_specs=[pl.BlockSpec((tm, k), lambda i, j: (i, 0)),
                  pl.BlockSpec((k, tn), lambda i, j: (0, col0 // tn + j))],
        out_specs=pl.BlockSpec((tm, tn), lambda i, j: (i, j)),
        out_shape=jax.ShapeDtypeStruct((t, n), F32),
        scratch_shapes=[pltpu.VMEM((tm, k), BF16)],
        compiler_params=_cp("parallel", "arbitrary"),
    )(x, w)


def _poolconv_kernel(u_ref, uh_ref, h_ref, hh_ref, b_ref, c_ref, ch_ref, pw_ref, ps_ref, cw_ref,
                     yp_ref, yc_ref, ubuf, zbuf, *, tb, n_tblk):
    first = (pl.program_id(0) % n_tblk) == 0
    ubuf[0:HALO, :] = jnp.where(first, 0.0, uh_ref[...])
    ubuf[HALO:HALO + tb, :] = u_ref[...]
    zbuf[0:HALO, :] = jnp.where(first, 0.0, ch_ref[...] * hh_ref[...])
    zbuf[HALO:HALO + tb, :] = c_ref[...] * h_ref[...]

    t_pos = (pl.program_id(0) % n_tblk) * tb + lax.broadcasted_iota(I32, (tb, LANES), 0)
    gw = u_ref.shape[1] // len(POOL_WINDOWS)
    for grp, win in enumerate(POOL_WINDOWS):
        lanes = slice(grp * gw, (grp + 1) * gw)
        cur = ubuf[HALO:HALO + tb, lanes]
        acc = cur
        for lag in range(1, win):
            acc = acc + ubuf[HALO - lag:HALO - lag + tb, lanes]
        count = jnp.minimum(t_pos + 1, win).astype(F32)
        mixed = acc / count - cur
        y = jnp.dot(mixed.astype(BF16), pw_ref[grp], preferred_element_type=F32)
        yp_ref[:, lanes] = y * ps_ref[:, lanes]

    conv = cw_ref[0:1, :] * zbuf[HALO:HALO + tb, :]
    for lag in range(1, CONV_WIDTH):
        conv = conv + cw_ref[lag:lag + 1, :] * zbuf[HALO - lag:HALO - lag + tb, :]
    yc_ref[...] = b_ref[...] * conv


def _pool_conv(proj, pool_w, pool_scale, conv_w, *, seq, width, tb):
    t = proj.shape[0]
    n_tblk = seq // tb
    ratio = tb // HALO

    def cur(col):
        return pl.BlockSpec((tb, width), lambda g: (g, col))

    def halo(col):
        return pl.BlockSpec((HALO, width), lambda g: (jnp.maximum(g * ratio - 1, 0), col))

    full = lambda shape: pl.BlockSpec(shape, lambda g: (0,) * len(shape))
    return pl.pallas_call(
        functools.partial(_poolconv_kernel, tb=tb, n_tblk=n_tblk),
        grid=(t // tb,),
        in_specs=[cur(0), halo(0), cur(1), halo(1), cur(2), cur(3), halo(3),
                  full(pool_w.shape), full(pool_scale.shape), full(conv_w.shape)],
        out_specs=[pl.BlockSpec((tb, width), lambda g: (g, 0))] * 2,
        out_shape=[jax.ShapeDtypeStruct((t, width), F32)] * 2,
        scratch_shapes=[pltpu.VMEM((HALO + tb, width), F32)] * 2,
        compiler_params=_cp("parallel"),
    )(proj, proj, proj, proj, proj, proj, proj, pool_w, pool_scale, conv_w)


def _cum_rows(x, op, fill):
    n = x.shape[0]
    row = lax.broadcasted_iota(I32, x.shape, 0)
    shift = 1
    while shift < n:
        x = op(x, jnp.where(row >= shift, pltpu.roll(x, shift, axis=0), fill))
        shift *= 2
    return x


def _mlstm_kernel(q_ref, k_ref, v_ref, o_ref, g_ref, gb_ref, nw_ref, y_ref, c_scr, n_scr, m_scr,
                  *, chunk, heads):
    @pl.when(pl.program_id(1) == 0)
    def _():
        c_scr[...] = jnp.zeros_like(c_scr)
        n_scr[...] = jnp.zeros_like(n_scr)
        m_scr[...] = jnp.zeros_like(m_scr)

    nb, tb = q_ref.shape[0], q_ref.shape[1]
    hd = HEAD_DIM
    scale = hd ** -0.5
    tri = (lax.broadcasted_iota(I32, (chunk, chunk), 0) >= lax.broadcasted_iota(I32, (chunk, chunk), 1))
    for c, bb in [(c, bb) for c in range(tb // chunk) for bb in range(nb)]:
        rows = slice(c * chunk, (c + 1) * chunk)
        gates = g_ref[bb, rows, :] + gb_ref[...]
        lf = _log_sigmoid(pltpu.roll(gates, LANES - heads, axis=1))
        cumf = _cum_rows(lf, jnp.add, 0.0)
        a = gates - cumf
        m_prev = m_scr[bb]
        mu = jnp.maximum(_cum_rows(a, jnp.maximum, -jnp.inf), m_prev)
        mu_last = mu[chunk - 1:chunk, :]
        a_t = a.T
        for h in range(heads):
            cols = slice(h * hd, (h + 1) * hd)
            q = q_ref[bb, rows, cols]
            k = k_ref[bb, rows, cols] * scale
            v = v_ref[bb, rows, cols]
            qb, kb, vb = q.astype(BF16), k.astype(BF16), v.astype(BF16)
            mu_col = mu[:, h:h + 1]
            a_col = a[:, h:h + 1]
            m_prev_h = m_prev[:, h:h + 1]
            mu_last_h = mu_last[:, h:h + 1]
            dmat = jnp.exp(jnp.where(tri, a_t[h:h + 1, :] - mu_col, -jnp.inf))
            s = lax.dot_general(qb, kb, (((1,), (1,)), ((), ())), preferred_element_type=F32)
            p = dmat * s
            inter = jnp.exp(m_prev_h - mu_col)
            state = bb * heads + h
            c_h = c_scr[state]
            n_h = n_scr[state]
            num = inter * jnp.dot(qb, c_h.astype(BF16), preferred_element_type=F32) \
                + jnp.dot(p.astype(BF16), vb, preferred_element_type=F32)
            den = inter * jnp.sum(q * n_h, axis=-1, keepdims=True) + jnp.sum(p, axis=-1, keepdims=True)
            floor = jnp.exp(-(cumf[:, h:h + 1] + mu_col))
            h_out = num / jnp.maximum(jnp.abs(den), floor)

            wg = jnp.exp(a_col - mu_last_h)
            decay = jnp.exp(m_prev_h - mu_last_h)
            kw = k * wg
            c_scr[state] = decay * c_h + lax.dot_general(kw.astype(BF16), vb, (((0,), (0,)), ((), ())),
                                                         preferred_element_type=F32)
            n_scr[state] = decay * n_h + jnp.sum(kw, axis=0, keepdims=True)

            gated = _sigmoid(o_ref[bb, rows, cols]) * h_out
            y_ref[bb, rows, cols] = _head_norm(gated, nw_ref[:, cols])
        m_scr[bb] = cumf[chunk - 1:chunk, :] + mu_last


def _mlstm(proj, gates, gate_b, norm_w, *, batch, seq, width, col0):
    t = proj.shape[0]
    heads = width // HEAD_DIM
    nb = SCAN_BATCH if batch % SCAN_BATCH == 0 else 1
    proj3 = proj.reshape(batch, seq, proj.shape[1])
    gates3 = gates.reshape(batch, seq, LANES)
    blk = lambda col: pl.BlockSpec((nb, SCAN_BLOCK, width), lambda b, i: (b, i, col))
    y = pl.pallas_call(
        functools.partial(_mlstm_kernel, chunk=SCAN_CHUNK, heads=heads),
        grid=(batch // nb, seq // SCAN_BLOCK),
        in_specs=[blk(col0), blk(col0 + 1), blk(col0 + 2), blk(col0 + 3),
                  pl.BlockSpec((nb, SCAN_BLOCK, LANES), lambda b, i: (b, i, 0)),
                  pl.BlockSpec((1, LANES), lambda b, i: (0, 0)),
                  pl.BlockSpec((1, width), lambda b, i: (0, 0))],
        out_specs=pl.BlockSpec((nb, SCAN_BLOCK, width), lambda b, i: (b, i, 0)),
        out_shape=jax.ShapeDtypeStruct((batch, seq, width), F32),
        scratch_shapes=[pltpu.VMEM((nb * heads, HEAD_DIM, HEAD_DIM), F32),
                        pltpu.VMEM((nb * heads, 1, HEAD_DIM), F32),
                        pltpu.VMEM((nb, 1, LANES), F32)],
        compiler_params=_cp("parallel", "arbitrary"),
    )(proj3, proj3, proj3, proj3, gates3, gate_b, norm_w)
    return y.reshape(t, width)


def _ret_kernel(q_ref, k_ref, v_ref, g_ref, cos_ref, sin_ref, intra_ref, cross_ref, zeta_ref, nw_ref,
                y_ref, r_scr, *, chunk, heads, chunk_decay):
    @pl.when(pl.program_id(1) == 0)
    def _():
        r_scr[...] = jnp.zeros_like(r_scr)

    tb = q_ref.shape[0]
    hd = HEAD_DIM
    scale = hd ** -0.5
    for c in range(tb // chunk):
        rows = slice(c * chunk, (c + 1) * chunk)
        cos = cos_ref[rows, :]
        sin = sin_ref[rows, :]
        for h in range(heads):
            cols = slice(h * hd, (h + 1) * hd)
            q = q_ref[rows, cols]
            k = k_ref[rows, cols]
            q = q * cos + pltpu.roll(q, hd // 2, axis=1) * sin
            k = (k * cos + pltpu.roll(k, hd // 2, axis=1) * sin) * scale
            vb = v_ref[rows, cols].astype(BF16)
            qb = q.astype(BF16)
            s = lax.dot_general(qb, k.astype(BF16), (((1,), (1,)), ((), ())), preferred_element_type=F32)
            inner = jnp.dot((s * intra_ref[h]).astype(BF16), vb, preferred_element_type=F32)
            r_h = r_scr[h]
            crs = jnp.dot(qb, r_h.astype(BF16), preferred_element_type=F32) * cross_ref[h]
            r_scr[h] = chunk_decay[h] * r_h + lax.dot_general(
                (k * zeta_ref[h]).astype(BF16), vb, (((0,), (0,)), ((), ())), preferred_element_type=F32)
            y_ref[rows, cols] = _silu(g_ref[rows, cols]) * _head_norm(inner + crs, nw_ref[:, cols])


def _retention(proj, norm_w, *, batch, seq, width, col0):
    t = proj.shape[0]
    heads = width // HEAD_DIM
    chunk = SCAN_CHUNK
    n_tblk = seq // SCAN_BLOCK
    half = HEAD_DIM // 2
    inv = np.float32(ROPE_BASE) ** (-np.arange(half, dtype=np.float32) / np.float32(half))
    ang = (np.arange(seq, dtype=np.float32)[:, None] * inv[None, :]).astype(np.float64)
    cos_t = jnp.asarray(np.concatenate([np.cos(ang), np.cos(ang)], axis=-1), F32)
    sin_t = jnp.asarray(np.concatenate([-np.sin(ang), np.sin(ang)], axis=-1), F32)
    log_g = jnp.log(1.0 - 2.0 ** (-5.0 - jnp.arange(heads, dtype=F32)))
    tt = jnp.arange(chunk, dtype=F32)
    lag = tt[:, None] - tt[None, :]
    intra = jnp.where(lag >= 0, jnp.exp(jnp.maximum(lag, 0.0)[None] * log_g[:, None, None]), 0.0)
    cross = jnp.broadcast_to(jnp.exp((tt + 1.0)[None, :] * log_g[:, None])[:, :, None], (heads, chunk, HEAD_DIM))
    zeta = jnp.broadcast_to(jnp.exp((chunk - 1.0 - tt)[None, :] * log_g[:, None])[:, :, None],
                            (heads, chunk, HEAD_DIM))
    chunk_decay = tuple(float((1.0 - 2.0 ** (-5.0 - h)) ** chunk) for h in range(heads))

    blk = lambda col: pl.BlockSpec((SCAN_BLOCK, width), lambda b, i: (b * n_tblk + i, col))
    pos = pl.BlockSpec((SCAN_BLOCK, HEAD_DIM), lambda b, i: (i, 0))
    full3 = lambda a: pl.BlockSpec(a.shape, lambda b, i: (0, 0, 0))
    return pl.pallas_call(
        functools.partial(_ret_kernel, chunk=chunk, heads=heads, chunk_decay=chunk_decay),
        grid=(batch, n_tblk),
        in_specs=[blk(col0), blk(col0 + 1), blk(col0 + 2), blk(col0 + 3), pos, pos,
                  full3(intra), full3(cross), full3(zeta),
                  pl.BlockSpec((1, width), lambda b, i: (0, 0))],
        out_specs=pl.BlockSpec((SCAN_BLOCK, width), lambda b, i: (b * n_tblk + i, 0)),
        out_shape=jax.ShapeDtypeStruct((t, width), F32),
        scratch_shapes=[pltpu.VMEM((heads, HEAD_DIM, HEAD_DIM), F32)],
        compiler_params=_cp("parallel", "arbitrary"),
    )(proj, proj, proj, proj, cos_t, sin_t, intra, cross, zeta, norm_w)


def _merge_kernel(x_ref, *refs):
    y_refs, wg_refs = refs[:N_BRANCH], refs[N_BRANCH:2 * N_BRANCH]
    wb_ref, o_ref, xb_ref = refs[2 * N_BRANCH:]

    @pl.when(pl.program_id(1) == 0)
    def _():
        xb_ref[...] = x_ref[...].astype(BF16)

    xb = xb_ref[...]
    acc = None
    for n in range(N_BRANCH):
        gate = _sigmoid(jnp.dot(xb, wg_refs[n][...], preferred_element_type=F32))
        term = gate * jnp.dot(y_refs[n][...].astype(BF16), wb_ref[n], preferred_element_type=F32)
        acc = term if acc is None else acc + term
    o_ref[...] = acc


def _merge(x, branches, w_gate, gate_col0, w_branch, *, tm, tn):
    t, d = x.shape
    width = branches[0].shape[1]
    ybs = pl.BlockSpec((tm, width), lambda i, j: (i, 0))

    def gate_spec(n):
        return pl.BlockSpec((d, tn), lambda i, j: (0, (gate_col0 + n * d) // tn + j))

    return pl.pallas_call(
        _merge_kernel,
        grid=(t // tm, d // tn),
        in_specs=[pl.BlockSpec((tm, d), lambda i, j: (i, 0))] + [ybs] * N_BRANCH
                 + [gate_spec(n) for n in range(N_BRANCH)]
                 + [pl.BlockSpec((N_BRANCH, width, tn), lambda i, j: (0, 0, j))],
        out_specs=pl.BlockSpec((tm, tn), lambda i, j: (i, j)),
        out_shape=jax.ShapeDtypeStruct((t, d), F32),
        scratch_shapes=[pltpu.VMEM((tm, d), BF16)],
        compiler_params=_cp("parallel", "arbitrary"),
    )(x, *branches, *([w_gate] * N_BRANCH), w_branch)


def _proj_ln_kernel(a_ref, w_ref, r_ref, lw_ref, lb_ref, o_ref, *, alpha):
    y = jnp.dot(a_ref[...].astype(BF16), w_ref[...], preferred_element_type=F32)
    o_ref[...] = _layer_norm(alpha * r_ref[...] + y, lw_ref[...], lb_ref[...])


def _proj_ln(a, w, resid, ln_w, ln_b, *, alpha, tm):
    t, k = a.shape
    d = w.shape[1]
    row = lambda n: pl.BlockSpec((tm, n), lambda i: (i, 0))
    const = lambda shape: pl.BlockSpec(shape, lambda i: (0, 0))
    return pl.pallas_call(
        functools.partial(_proj_ln_kernel, alpha=alpha),
        grid=(t // tm,),
        in_specs=[row(k), const((k, d)), row(d), const((1, d)), const((1, d))],
        out_specs=row(d),
        out_shape=jax.ShapeDtypeStruct((t, d), F32),
        compiler_params=_cp("parallel"),
    )(a, w, resid, ln_w, ln_b)


def _xattn_kernel(x_ref, wq_ref, kv_ref, wo_ref, lw_ref, lb_ref, o_ref, *, alpha, heads):
    x = x_ref[...]
    hd = HEAD_DIM
    inner = heads * hd
    q = jnp.dot(x.astype(BF16), wq_ref[...], preferred_element_type=F32)
    outs = []
    for h in range(heads):
        qh = q[:, h * hd:(h + 1) * hd].astype(BF16)
        kh = kv_ref[:, h * hd:(h + 1) * hd].astype(BF16)
        vh = kv_ref[:, inner + h * hd:inner + (h + 1) * hd].astype(BF16)
        s = lax.dot_general(qh, kh, (((1,), (1,)), ((), ())), preferred_element_type=F32) * hd ** -0.5
        s = s - jnp.max(s, axis=-1, keepdims=True)
        e = jnp.exp(s)
        p = e / jnp.sum(e, axis=-1, keepdims=True)
        outs.append(jnp.dot(p.astype(BF16), vh, preferred_element_type=F32).astype(BF16))
    o = jnp.concatenate(outs, axis=-1)
    y = jnp.dot(o, wo_ref[...], preferred_element_type=F32)
    o_ref[...] = _layer_norm(alpha * x + y, lw_ref[...], lb_ref[...])


def _xattn(x, kv, wq, wo, ln_w, ln_b, *, alpha, seq, mem_len, tm):
    t, d = x.shape
    inner = wq.shape[1]
    n_tblk = seq // tm
    const = lambda shape: pl.BlockSpec(shape, lambda i: (0, 0))
    return pl.pallas_call(
        functools.partial(_xattn_kernel, alpha=alpha, heads=XATTN_HEADS),
        grid=(t // tm,),
        in_specs=[pl.BlockSpec((tm, d), lambda i: (i, 0)), const((d, inner)),
                  pl.BlockSpec((mem_len, 2 * inner), lambda i: (i // n_tblk, 0)),
                  const((inner, d)), const((1, d)), const((1, d))],
        out_specs=pl.BlockSpec((tm, d), lambda i: (i, 0)),
        out_shape=jax.ShapeDtypeStruct((t, d), F32),
        compiler_params=_cp("parallel"),
    )(x, wq, kv, wo, ln_w, ln_b)


def _round_up_pow2(x, m):
    shift = m.bit_length() - 1
    return jnp.left_shift(jnp.right_shift(x + (m - 1), shift), shift)


def _route_kernel(x_ref, wr_ref, rb_ref, w_ref, lrow_ref, segtab_ref, seg_ref, size_all, tot, *, tm):
    step = pl.program_id(0)

    @pl.when(step == 0)
    def _():
        tot[...] = jnp.zeros_like(tot)

    e_n, g_n = N_EXPERTS, N_GROUPS
    per = e_n // g_n
    logits = lax.dot_general(wr_ref[...], x_ref[...], (((1,), (1,)), ((), ())),
                             precision=lax.Precision.HIGHEST, preferred_element_type=F32)
    scores = _sigmoid(logits)
    biased = scores + rb_ref[...]
    b3 = biased.reshape(g_n, per, tm)
    member = lax.broadcasted_iota(I32, (g_n, per, tm), 1)
    top1 = jnp.max(b3, axis=1, keepdims=True)
    first = jnp.min(jnp.where(b3 == top1, member, per), axis=1, keepdims=True)
    top2 = jnp.max(jnp.where(member == first, -jnp.inf, b3), axis=1, keepdims=True)
    gs = top1 + top2
    gid = lax.broadcasted_iota(I32, (g_n, 1, tm), 0)
    rank = jnp.zeros((g_n, 1, tm), I32)
    for other in range(g_n):
        o = gs[other:other + 1]
        ahead = jnp.logical_or(o > gs, jnp.logical_and(o == gs, other < gid))
        rank = rank + jnp.where(ahead, 1, 0)
    cur = jnp.where(rank < TOPK_GROUPS, b3, -jnp.inf).reshape(e_n, tm)

    eid = lax.broadcasted_iota(I32, (e_n, tm), 0)
    picks, vals = [], []
    sel = jnp.zeros((e_n, tm), F32)
    for k in range(TOP_K):
        mx = jnp.max(cur, axis=0, keepdims=True)
        ik = jnp.min(jnp.where(cur == mx, eid, e_n), axis=0, keepdims=True)
        hit = eid == ik
        vals.append(jnp.sum(jnp.where(hit, scores, 0.0), axis=0, keepdims=True))
        cur = jnp.where(hit, -jnp.inf, cur)
        sel = jnp.where(hit, 1.0, sel)
        picks.append(ik)
    total = vals[0]
    for v in vals[1:]:
        total = total + v

    tri = jnp.where(lax.broadcasted_iota(I32, (tm, tm), 0) <= lax.broadcasted_iota(I32, (tm, tm), 1), 1.0, 0.0)
    incl = jnp.dot(sel.astype(BF16), tri.astype(BF16), preferred_element_type=F32)
    size = _round_up_pow2(jnp.broadcast_to(incl[:, tm - 1:tm], (e_n, LANES)).astype(I32), SEG_ALIGN)
    loff = _cum_rows(size, jnp.add, 0) - size
    base = loff[:, 0:1].astype(F32) + incl - 1.0
    for k in range(TOP_K):
        w_ref[k:k + 1, :] = vals[k] / total * ROUTE_SCALE
        lrow_ref[0, k:k + 1, :] = jnp.sum(jnp.where(eid == picks[k], base, 0.0),
                                          axis=0, keepdims=True).astype(I32)
    size_all[step] = size
    tot[...] = tot[...] + size

    @pl.when(step == pl.num_programs(0) - 1)
    def _():
        rows = tot[...]
        start = _cum_rows(rows, jnp.add, 0) - rows

        def tile_seg(i, run):
            segtab_ref[i, 0] = run
            segtab_ref[i, 1] = size_all[i]
            return run + size_all[i]

        lax.fori_loop(0, pl.num_programs(0), tile_seg, start)
        seg_ref[0] = start
        seg_ref[1] = rows


def _route(x, router_w_t, router_b, *, tm):
    t, d = x.shape
    e_n = N_EXPERTS
    n_t = t // tm
    return pl.pallas_call(
        functools.partial(_route_kernel, tm=tm),
        grid=(n_t,),
        in_specs=[pl.BlockSpec((tm, d), lambda i: (i, 0)),
                  pl.BlockSpec((e_n, d), lambda i: (0, 0)),
                  pl.BlockSpec((e_n, 1), lambda i: (0, 0))],
        out_specs=[pl.BlockSpec((TOP_K, tm), lambda i: (0, i)),
                   pl.BlockSpec((1, TOP_K, tm), lambda i: (i, 0, 0)),
                   pl.BlockSpec((n_t, 2, e_n, LANES), lambda i: (0, 0, 0, 0)),
                   pl.BlockSpec((2, e_n, LANES), lambda i: (0, 0, 0))],
        out_shape=[jax.ShapeDtypeStruct((TOP_K, t), F32), jax.ShapeDtypeStruct((n_t, TOP_K, tm), I32),
                   jax.ShapeDtypeStruct((n_t, 2, e_n, LANES), I32), jax.ShapeDtypeStruct((2, e_n, LANES), I32)],
        scratch_shapes=[pltpu.VMEM((n_t, e_n, LANES), I32), pltpu.VMEM((e_n, LANES), I32)],
        compiler_params=_cp("arbitrary"),
    )(x, router_w_t, router_b)


def _tile_rows(tm):
    worst = TOP_K * tm + N_EXPERTS * (SEG_ALIGN - 1)
    return -(-worst // ONEHOT_ROWS) * ONEHOT_ROWS


def _onehot_rows(chunk, lrow, values, tm):
    rid = chunk * ONEHOT_ROWS + lax.broadcasted_iota(I32, (ONEHOT_ROWS, tm), 0)
    acc = jnp.zeros((ONEHOT_ROWS, tm), F32)
    for k in range(TOP_K):
        acc = jnp.where(rid == lrow[k:k + 1, :], 1.0 if values is None else values[k:k + 1, :], acc)
    return acc.astype(BF16)


def _segments(gstart_ref, size_ref, tile, make_copy, act, keep=None):
    def body(e, loff):
        n = size_ref[tile * N_EXPERTS + e]
        wanted = n > 0 if keep is None else jnp.logical_and(n > 0, keep(loff + n))

        @pl.when(wanted)
        def _():
            act(make_copy(pl.multiple_of(gstart_ref[tile * N_EXPERTS + e], SEG_ALIGN),
                          pl.multiple_of(loff, SEG_ALIGN), pl.multiple_of(n, SEG_ALIGN)))
        return loff + n
    return lax.fori_loop(0, N_EXPERTS, body, 0)


def _start(cp):
    cp.start()


def _wait(cp):
    cp.wait()


def _dispatch_kernel(gstart_ref, size_ref, x_ref, lrow_ref, xs_ref, stage, sem_a, sem_b, *, tm):
    i = pl.program_id(0)
    last = pl.num_programs(0) - 1
    n_chunks = stage.shape[0] // ONEHOT_ROWS
    split_chunk = n_chunks // 2
    split = split_chunk * ONEHOT_ROWS
    assert tm <= ONEHOT_ROWS

    def seg_walk(tile, act, phase_b):
        sem = sem_b if phase_b else sem_a
        return _segments(
            gstart_ref, size_ref, tile,
            lambda g, loff, n: pltpu.make_async_copy(stage.at[pl.ds(loff, n), :], xs_ref.at[pl.ds(g, n), :], sem),
            act, (lambda end: end > split) if phase_b else (lambda end: end <= split))

    xb = x_ref[...].astype(BF16)
    lrow = lrow_ref[0]
    n_rows = lax.fori_loop(0, N_EXPERTS, lambda e, s: s + size_ref[i * N_EXPERTS + e], 0)

    def chunks(lo, hi):
        for c in range(lo, hi):
            def one(c=c):
                stage[c * ONEHOT_ROWS:(c + 1) * ONEHOT_ROWS, :] = jnp.dot(
                    _onehot_rows(c, lrow, None, tm), xb, preferred_element_type=F32).astype(BF16)
            if c * ONEHOT_ROWS < TOP_K * tm:
                one()
            else:
                pl.when(c * ONEHOT_ROWS < n_rows)(one)

    prev = jnp.maximum(i - 1, 0)

    @pl.when(i > 0)
    def _():
        seg_walk(prev, _wait, False)
    chunks(0, split_chunk - 1)

    @pl.when(i > 0)
    def _():
        seg_walk(prev, _wait, True)
    chunks(split_chunk - 1, split_chunk)
    seg_walk(i, _start, False)
    chunks(split_chunk, n_chunks)
    seg_walk(i, _start, True)

    @pl.when(i == last)
    def _():
        seg_walk(i, _wait, False)
        seg_walk(i, _wait, True)


def _dispatch(x, lrow, gstart, size, *, rows, tm):
    t, d = x.shape
    return pl.pallas_call(
        functools.partial(_dispatch_kernel, tm=tm),
        grid_spec=pltpu.PrefetchScalarGridSpec(
            num_scalar_prefetch=2,
            grid=(t // tm,),
            in_specs=[pl.BlockSpec((tm, d), lambda i, *_: (i, 0)),
                      pl.BlockSpec((1, TOP_K, tm), lambda i, *_: (i, 0, 0))],
            out_specs=pl.BlockSpec(memory_space=pl.ANY),
            scratch_shapes=[pltpu.VMEM((_tile_rows(tm), d), BF16)] + [pltpu.SemaphoreType.DMA(())] * 2,
        ),
        out_shape=jax.ShapeDtypeStruct((rows, d), BF16),
        compiler_params=_cp("arbitrary"),
    )(gstart, size, x, lrow)


def _expert_kernel(start_ref, rows_ref, wgu_ref, wdn_ref, xs_ref, ys_ref, wgu_b, wdn_b, xbuf, ybuf, sem_in, sem_out,
                   *, bm):
    e = pl.program_id(0)
    row0 = start_ref[e]
    n = rows_ref[e]
    n_blk = (n + bm - 1) // bm

    @pl.when(e == 0)
    def _():
        xbuf[...] = jnp.zeros_like(xbuf)

    def row_dma(b, slot, act, load):
        off = pl.multiple_of(row0 + b * bm, SEG_ALIGN)
        left = n - b * bm

        def copy(r):
            if load:
                return pltpu.make_async_copy(xs_ref.at[pl.ds(off, r), :], xbuf.at[slot, pl.ds(0, r), :],
                                             sem_in.at[slot])
            return pltpu.make_async_copy(ybuf.at[slot, pl.ds(0, r), :], ys_ref.at[pl.ds(off, r), :],
                                         sem_out.at[slot])

        @pl.when(left >= bm)
        def _():
            act(copy(bm))

        @pl.when(left < bm)
        def _():
            act(copy(pl.multiple_of(left, SEG_ALIGN)))

    def start(cp):
        cp.start(priority=ROW_DMA_PRIORITY)

    @pl.when(n_blk > 0)
    def _():
        row_dma(0, 0, start, True)

    wgu_b[...] = wgu_ref[...].astype(BF16)
    wdn_b[...] = wdn_ref[...].astype(BF16)

    def block(b, slot):
        @pl.when(b < n_blk)
        def _():
            @pl.when(b + 1 < n_blk)
            def _():
                row_dma(b + 1, 1 - slot, start, True)

            row_dma(b, slot, _wait, True)

            @pl.when(b >= 2)
            def _():
                row_dma(b - 2, slot, _wait, False)

            f = wdn_b.shape[0]
            gu = jnp.dot(xbuf[slot], wgu_b[...], preferred_element_type=F32)
            hidden = (_silu(gu[:, :f]) * gu[:, f:]).astype(BF16)
            ybuf[slot] = jnp.dot(hidden, wdn_b[...], preferred_element_type=F32).astype(BF16)
            row_dma(b, slot, start, False)

    def pair(p, carry):
        block(2 * p, 0)
        block(2 * p + 1, 1)
        return carry

    lax.fori_loop(0, (n_blk + 1) // 2, pair, 0)

    for back in (2, 1):
        for slot in range(2):
            @pl.when(jnp.logical_and(n_blk >= back, (n_blk - back) % 2 == slot))
            def _():
                row_dma(n_blk - back, slot, _wait, False)


def _experts(xs, start, rows, w_gu, w_dn, layer, *, bm):
    total, d = xs.shape
    f2 = w_gu.shape[3]
    f = w_dn.shape[2]
    return pl.pallas_call(
        functools.partial(_expert_kernel, bm=bm),
        grid_spec=pltpu.PrefetchScalarGridSpec(
            num_scalar_prefetch=2,
            grid=(N_EXPERTS,),
            in_specs=[pl.BlockSpec((None, None, d, f2), lambda e, *_: (layer, e, 0, 0)),
                      pl.BlockSpec((None, None, f, d), lambda e, *_: (layer, e, 0, 0)),
                      pl.BlockSpec(memory_space=pl.ANY)],
            out_specs=pl.BlockSpec(memory_space=pl.ANY),
            scratch_shapes=[pltpu.VMEM((d, f2), BF16), pltpu.VMEM((f, d), BF16),
                            pltpu.VMEM((2, bm, d), BF16), pltpu.VMEM((2, bm, d), BF16),
                            pltpu.SemaphoreType.DMA((2,)), pltpu.SemaphoreType.DMA((2,))],
        ),
        out_shape=jax.ShapeDtypeStruct((total, d), BF16),
        compiler_params=_cp("arbitrary"),
    )(start, rows, w_gu, w_dn, xs)


def _combine_kernel(gstart_ref, size_ref, x_ref, lrow_ref, w_ref, sdn_ref, lw_ref, lb_ref, sgu_hbm, ys_ref,
                    o_ref, ybuf, wt, hid, sgu, sems, wsem, *, tm, alpha):
    i, half = pl.program_id(0), pl.program_id(1)
    last_tile = pl.num_programs(0) - 1
    dh = ybuf.shape[2]
    n_chunks = ybuf.shape[1] // ONEHOT_ROWS
    sure_chunks = (TOP_K * tm) // ONEHOT_ROWS

    def contract_rows(w_rows, y_rows):
        return lax.dot_general(w_rows, y_rows, (((0,), (0,)), ((), ())), preferred_element_type=F32)

    def seg_walk(tile, hf, act):
        return _segments(
            gstart_ref, size_ref, tile,
            lambda g, loff, n: pltpu.make_async_copy(ys_ref.at[pl.ds(g, n), hf * dh:(hf + 1) * dh],
                                                     ybuf.at[hf, pl.ds(loff, n), :], sems.at[hf]),
            act)

    @pl.when(jnp.logical_and(i == 0, half == 0))
    def _():
        ybuf[...] = jnp.zeros_like(ybuf)
        weights = pltpu.make_async_copy(sgu_hbm, sgu, wsem)
        weights.start()
        weights.wait()
        seg_walk(0, 0, _start)

    @pl.when(half == 0)
    def _():
        seg_walk(i, 1, _start)
        f = hid.shape[1]
        gu = jnp.dot(x_ref[...].astype(BF16), sgu[...], preferred_element_type=F32)
        hidden = (_silu(gu[:, :f]) * gu[:, f:]).astype(BF16)
        hid[...] = hidden
        n_rows = seg_walk(i, 0, _wait)
        lrow, w = lrow_ref[0], w_ref[...]
        acc = jnp.dot(hidden, sdn_ref[...], preferred_element_type=F32)
        for c in range(sure_chunks):
            rows = slice(c * ONEHOT_ROWS, (c + 1) * ONEHOT_ROWS)
            w_rows = _onehot_rows(c, lrow, w, tm)
            wt[rows, :] = w_rows
            acc = acc + contract_rows(w_rows, ybuf[0, rows, :])
        o_ref[:, 0:dh] = acc
        for c in range(sure_chunks, n_chunks):
            @pl.when(c * ONEHOT_ROWS < n_rows)
            def _():
                rows = slice(c * ONEHOT_ROWS, (c + 1) * ONEHOT_ROWS)
                w_rows = _onehot_rows(c, lrow, w, tm)
                wt[rows, :] = w_rows
                o_ref[:, 0:dh] += contract_rows(w_rows, ybuf[0, rows, :])

    @pl.when(half == 1)
    def _():
        @pl.when(i < last_tile)
        def _():
            seg_walk(i + 1, 0, _start)
        n_rows = seg_walk(i, 1, _wait)
        sure = slice(0, sure_chunks * ONEHOT_ROWS)
        o_ref[:, dh:2 * dh] = contract_rows(wt[sure, :], ybuf[1, sure, :]) \
            + jnp.dot(hid[...], sdn_ref[...], preferred_element_type=F32)
        for c in range(sure_chunks, n_chunks):
            @pl.when(c * ONEHOT_ROWS < n_rows)
            def _():
                rows = slice(c * ONEHOT_ROWS, (c + 1) * ONEHOT_ROWS)
                o_ref[:, dh:2 * dh] += contract_rows(wt[rows, :], ybuf[1, rows, :])
        o_ref[...] = _layer_norm(alpha * x_ref[...] + o_ref[...], lw_ref[...], lb_ref[...])


def _combine(x, ys, lrow, wts, gstart, size, s_gu, s_dn, ln_w, ln_b, *, alpha, tm):
    t, d = x.shape
    dh = d // 2
    f = s_dn.shape[0]
    const = lambda shape: pl.BlockSpec(shape, lambda i, h, *_: (0, 0))
    return pl.pallas_call(
        functools.partial(_combine_kernel, tm=tm, alpha=alpha),
        grid_spec=pltpu.PrefetchScalarGridSpec(
            num_scalar_prefetch=2,
            grid=(t // tm, 2),
            in_specs=[pl.BlockSpec((tm, d), lambda i, h, *_: (i, 0)),
                      pl.BlockSpec((1, TOP_K, tm), lambda i, h, *_: (i, 0, 0)),
                      pl.BlockSpec((TOP_K, tm), lambda i, h, *_: (0, i)),
                      pl.BlockSpec((f, dh), lambda i, h, *_: (0, h)),
                      const((1, d)), const((1, d)),
                      pl.BlockSpec(memory_space=pl.ANY), pl.BlockSpec(memory_space=pl.ANY)],
            out_specs=pl.BlockSpec((tm, d), lambda i, h, *_: (i, 0)),
            scratch_shapes=[pltpu.VMEM((2, _tile_rows(tm), dh), BF16), pltpu.VMEM((_tile_rows(tm), tm), BF16),
                            pltpu.VMEM((tm, f), BF16), pltpu.VMEM(s_gu.shape, BF16),
                            pltpu.SemaphoreType.DMA((2,)), pltpu.SemaphoreType.DMA(())],
        ),
        out_shape=jax.ShapeDtypeStruct((t, d), F32),
        compiler_params=_cp("arbitrary", "arbitrary"),
    )(gstart, size, x, lrow, wts, s_dn, ln_w, ln_b, s_gu, ys)


def _mixer_sublayer(x, w_in_all, layer, gate_b, pool_w, pool_scale, conv_w, mlstm_norm_w, ret_norm_w, w_branch,
                    w_out, ln_w, ln_b, *, batch, seq, alpha):
    t, d = x.shape
    width = d // N_BRANCH
    heads = width // HEAD_DIM
    gate_off = 8 * width
    ret_off = gate_off
    g_off = ret_off + 4 * width
    if_off = g_off + N_BRANCH * d
    w_bf16 = _realign_cast(w_in_all, layer, lo_col=gate_off, hi_col=if_off, shift=2 * heads, tr=512, tn=512)
    gate_bias = jnp.pad(gate_b, (0, LANES - 2 * heads)).reshape(1, LANES)

    proj_a = _matmul(x, w_bf16, tm=1024, tn=1024, ncols=gate_off)
    proj_b = _matmul(x, w_bf16, tm=1024, tn=1024, ncols=4 * width, col0=ret_off)
    gates = _matmul(x, w_bf16, tm=1024, tn=LANES, ncols=LANES, col0=if_off)
    y_pool, y_conv = _pool_conv(proj_a, pool_w.astype(BF16), pool_scale.reshape(1, width), conv_w,
                                seq=seq, width=width, tb=512)
    y_mlstm = _mlstm(proj_a, gates, gate_bias, mlstm_norm_w.reshape(1, width),
                     batch=batch, seq=seq, width=width, col0=4)
    y_ret = _retention(proj_b, ret_norm_w.reshape(1, width), batch=batch, seq=seq, width=width, col0=0)
    merged = _merge(x, (y_pool, y_conv, y_mlstm, y_ret), w_bf16, g_off, w_branch.astype(BF16),
                    tm=512, tn=512)
    return _proj_ln(merged, w_out.astype(BF16), x, ln_w, ln_b, alpha=alpha, tm=512)


def _xattn_sublayer(x, mem2d, wq, wk, wv, wo, ln_w, ln_b, *, seq, mem_len, alpha):
    w_kv = jnp.concatenate([wk, wv], axis=1).astype(BF16)
    kv = _matmul(mem2d, w_kv, tm=min(mem2d.shape[0], 1024), tn=512)
    return _xattn(x, kv, wq.astype(BF16), wo.astype(BF16), ln_w, ln_b,
                  alpha=alpha, seq=seq, mem_len=mem_len, tm=512)


def _moe_sublayer(x, router_w, router_b, w_gu, w_dn, layer, s_gu, s_dn, ln_w, ln_b, *, alpha):
    t, d = x.shape
    e_n = N_EXPERTS
    tm = ROUTE_TM
    rows = t * TOP_K + (t // tm) * e_n * (SEG_ALIGN - 1)
    wts, lrow, segtab, seg = _route(x, router_w.T, router_b.reshape(e_n, 1), tm=tm)
    gstart, size = segtab[:, 0, :, 0].reshape(-1), segtab[:, 1, :, 0].reshape(-1)

    xs = _dispatch(x, lrow, gstart, size, rows=rows, tm=tm)
    ys = _experts(xs, seg[0, :, 0], seg[1, :, 0], w_gu, w_dn, layer, bm=MOE_BM)
    return _combine(x, ys, lrow, wts, gstart, size, s_gu.astype(BF16), s_dn.astype(BF16), ln_w, ln_b,
                    alpha=alpha, tm=tm)


def kernel(x, mem, w_in, mlstm_gate_b, pool_w, pool_scale, conv_w, mlstm_norm_w, ret_norm_w, w_branch,
           w_mix_out, xa_wq, xa_wk, xa_wv, xa_wo, router_w, router_b, moe_w_gu, moe_w_dn, shared_w_gu,
           shared_w_dn, ln_w, ln_b):
    batch, seq, d = x.shape
    depth = w_in.shape[0]
    mem_len = mem.shape[1]
    alpha = (2 * depth) ** 0.25
    h = x.reshape(batch * seq, d)
    mem2d = mem.reshape(batch * mem_len, d)
    for l in range(depth):
        lw = ln_w[l].reshape(3, 1, d)
        lb = ln_b[l].reshape(3, 1, d)
        h = _mixer_sublayer(h, w_in, l, mlstm_gate_b[l], pool_w[l], pool_scale[l], conv_w[l], mlstm_norm_w[l],
                            ret_norm_w[l], w_branch[l], w_mix_out[l], lw[0], lb[0],
                            batch=batch, seq=seq, alpha=alpha)
        h = _xattn_sublayer(h, mem2d, xa_wq[l], xa_wk[l], xa_wv[l], xa_wo[l], lw[1], lb[1],
                            seq=seq, mem_len=mem_len, alpha=alpha)
        h = _moe_sublayer(h, router_w[l], router_b[l], moe_w_gu, moe_w_dn, l, shared_w_gu[l], shared_w_dn[l],
                          lw[2], lb[2], alpha=alpha)
    return h.reshape(batch, seq, d)
```

```python
import functools

import numpy as np
import jax
import jax.numpy as jnp
from jax import lax
from jax.experimental import pallas as pl
from jax.experimental.pallas import tpu as pltpu

F32 = jnp.float32
BF16 = jnp.bfloat16
I32 = jnp.int32

N_BRANCH = 4
HEAD_DIM = 128
POOL_WINDOWS = (2, 4, 8, 16)
CONV_WIDTH = 3
ROPE_BASE = 10000.0
XATTN_HEADS = 4
N_EXPERTS = 64
TOP_K = 8
N_GROUPS = 8
TOPK_GROUPS = 4
ROUTE_SCALE = 2.5
LN_EPS = 1e-5

LANES = 128
V7X_VMEM_BYTES = 64 * 1024 * 1024
VMEM_LIMIT = 56 * 1024 * 1024

SCAN_CHUNK = 256
SCAN_BLOCK = 512
SCAN_BATCH = 2
HALO = 16
MOE_BM = 512
ROUTE_TM = 256
SEG_ALIGN = 16
ONEHOT_ROWS = 512
ROW_DMA_PRIORITY = 1


def _cp(*sem):
    return pltpu.CompilerParams(dimension_semantics=sem, vmem_limit_bytes=VMEM_LIMIT)


def _sigmoid(x):
    return 1.0 / (1.0 + jnp.exp(-x))


def _silu(x):
    return x * _sigmoid(x)


def _log_sigmoid(x):
    return jnp.minimum(x, 0.0) - jnp.log(1.0 + jnp.exp(-jnp.abs(x)))


def _layer_norm(z, w, b):
    mu = jnp.mean(z, axis=-1, keepdims=True)
    d = z - mu
    var = jnp.mean(d * d, axis=-1, keepdims=True)
    return d * lax.rsqrt(var + LN_EPS) * w + b


def _head_norm(h, w):
    mu = jnp.mean(h, axis=-1, keepdims=True)
    d = h - mu
    var = jnp.mean(d * d, axis=-1, keepdims=True)
    return d * lax.rsqrt(var + LN_EPS) * w


def _mm_kernel(x_ref, w_ref, o_ref, xb_ref):
    @pl.when(pl.program_id(1) == 0)
    def _():
        xb_ref[...] = x_ref[...].astype(BF16)

    o_ref[...] = jnp.dot(xb_ref[...], w_ref[...], preferred_element_type=F32)


def _realign_cast_kernel(a_ref, b_ref, o_ref, *, shift, lo, hi):
    j = pl.program_id(1)
    tn = o_ref.shape[1]
    shifted = jnp.logical_and(j >= lo, j < hi)

    @pl.when(shifted)
    def _():
        both = jnp.concatenate([a_ref[...], b_ref[...]], axis=1)
        o_ref[...] = both[:, shift:shift + tn].astype(BF16)

    @pl.when(jnp.logical_not(shifted))
    def _():
        o_ref[...] = a_ref[...].astype(BF16)


def _realign_cast(w_all, layer, *, lo_col, hi_col, shift, tr, tn):
    _, rows, _ = w_all.shape
    lo, hi = lo_col // tn, hi_col // tn
    src = lambda j: jnp.where(j == hi, lo, j)
    return pl.pallas_call(
        functools.partial(_realign_cast_kernel, shift=shift, lo=lo, hi=hi),
        grid=(rows // tr, hi + 1),
        in_specs=[pl.BlockSpec((None, tr, tn), lambda i, j: (layer, i, src(j))),
                  pl.BlockSpec((None, tr, LANES), lambda i, j: (layer, i, (src(j) + 1) * (tn // LANES)))],
        out_specs=pl.BlockSpec((tr, tn), lambda i, j: (i, j)),
        out_shape=jax.ShapeDtypeStruct((rows, hi_col + tn), BF16),
        compiler_params=_cp("parallel", "parallel"),
    )(w_all, w_all)


def _matmul(x, w, *, tm, tn, ncols=None, col0=0):
    t, k = x.shape
    n = w.shape[1] if ncols is None else ncols
    return pl.pallas_call(
        _mm_kernel,
        grid=(t // tm, n // tn),
        in_specs=[pl.BlockSpec((tm, k), lambda i, j: (i, 0)),
                  pl.BlockSpec((k, tn), lambda i, j: (0, col0 // tn + j))],
        out_specs=pl.BlockSpec((tm, tn), lambda i, j: (i, j)),
        out_shape=jax.ShapeDtypeStruct((t, n), F32),
        scratch_shapes=[pltpu.VMEM((tm, k), BF16)],
        compiler_params=_cp("parallel", "arbitrary"),
    )(x, w)


def _poolconv_kernel(u_ref, uh_ref, h_ref, hh_ref, b_ref, c_ref, ch_ref, pw_ref, ps_ref, cw_ref,
                     yp_ref, yc_ref, ubuf, zbuf, *, tb, n_tblk):
    first = (pl.program_id(0) % n_tblk) == 0
    ubuf[0:HALO, :] = jnp.where(first, 0.0, uh_ref[...])
    ubuf[HALO:HALO + tb, :] = u_ref[...]
    zbuf[0:HALO, :] = jnp.where(first, 0.0, ch_ref[...] * hh_ref[...])
    zbuf[HALO:HALO + tb, :] = c_ref[...] * h_ref[...]

    t_pos = (pl.program_id(0) % n_tblk) * tb + lax.broadcasted_iota(I32, (tb, LANES), 0)
    gw = u_ref.shape[1] // len(POOL_WINDOWS)
    for grp, win in enumerate(POOL_WINDOWS):
        lanes = slice(grp * gw, (grp + 1) * gw)
        cur = ubuf[HALO:HALO + tb, lanes]
        acc = cur
        for lag in range(1, win):
            acc = acc + ubuf[HALO - lag:HALO - lag + tb, lanes]
        count = jnp.minimum(t_pos + 1, win).astype(F32)
        mixed = acc / count - cur
        y = jnp.dot(mixed.astype(BF16), pw_ref[grp], preferred_element_type=F32)
        yp_ref[:, lanes] = y * ps_ref[:, lanes]

    conv = cw_ref[0:1, :] * zbuf[HALO:HALO + tb, :]
    for lag in range(1, CONV_WIDTH):
        conv = conv + cw_ref[lag:lag + 1, :] * zbuf[HALO - lag:HALO - lag + tb, :]
    yc_ref[...] = b_ref[...] * conv


def _pool_conv(proj, pool_w, pool_scale, conv_w, *, seq, width, tb):
    t = proj.shape[0]
    n_tblk = seq // tb
    ratio = tb // HALO

    def cur(col):
        return pl.BlockSpec((tb, width), lambda g: (g, col))

    def halo(col):
        return pl.BlockSpec((HALO, width), lambda g: (jnp.maximum(g * ratio - 1, 0), col))

    full = lambda shape: pl.BlockSpec(shape, lambda g: (0,) * len(shape))
    return pl.pallas_call(
        functools.partial(_poolconv_kernel, tb=tb, n_tblk=n_tblk),
        grid=(t // tb,),
        in_specs=[cur(0), halo(0), cur(1), halo(1), cur(2), cur(3), halo(3),
                  full(pool_w.shape), full(pool_scale.shape), full(conv_w.shape)],
        out_specs=[pl.BlockSpec((tb, width), lambda g: (g, 0))] * 2,
        out_shape=[jax.ShapeDtypeStruct((t, width), F32)] * 2,
        scratch_shapes=[pltpu.VMEM((HALO + tb, width), F32)] * 2,
        compiler_params=_cp("parallel"),
    )(proj, proj, proj, proj, proj, proj, proj, pool_w, pool_scale, conv_w)


def _cum_rows(x, op, fill):
    n = x.shape[0]
    row = lax.broadcasted_iota(I32, x.shape, 0)
    shift = 1
    while shift < n:
        x = op(x, jnp.where(row >= shift, pltpu.roll(x, shift, axis=0), fill))
        shift *= 2
    return x


def _mlstm_kernel(q_ref, k_ref, v_ref, o_ref, g_ref, gb_ref, nw_ref, y_ref, c_scr, n_scr, m_scr,
                  *, chunk, heads):
    @pl.when(pl.program_id(1) == 0)
    def _():
        c_scr[...] = jnp.zeros_like(c_scr)
        n_scr[...] = jnp.zeros_like(n_scr)
        m_scr[...] = jnp.zeros_like(m_scr)

    nb, tb = q_ref.shape[0], q_ref.shape[1]
    hd = HEAD_DIM
    scale = hd ** -0.5
    tri = (lax.broadcasted_iota(I32, (chunk, chunk), 0) >= lax.broadcasted_iota(I32, (chunk, chunk), 1))
    for c, bb in [(c, bb) for c in range(tb // chunk) for bb in range(nb)]:
        rows = slice(c * chunk, (c + 1) * chunk)
        gates = g_ref[bb, rows, :] + gb_ref[...]
        lf = _log_sigmoid(pltpu.roll(gates, LANES - heads, axis=1))
        cumf = _cum_rows(lf, jnp.add, 0.0)
        a = gates - cumf
        m_prev = m_scr[bb]
        mu = jnp.maximum(_cum_rows(a, jnp.maximum, -jnp.inf), m_prev)
        mu_last = mu[chunk - 1:chunk, :]
        a_t = a.T
        for h in range(heads):
            cols = slice(h * hd, (h + 1) * hd)
            q = q_ref[bb, rows, cols]
            k = k_ref[bb, rows, cols] * scale
            v = v_ref[bb, rows, cols]
            qb, kb, vb = q.astype(BF16), k.astype(BF16), v.astype(BF16)
            mu_col = mu[:, h:h + 1]
            a_col = a[:, h:h + 1]
            m_prev_h = m_prev[:, h:h + 1]
            mu_last_h = mu_last[:, h:h + 1]
            dmat = jnp.exp(jnp.where(tri, a_t[h:h + 1, :] - mu_col, -jnp.inf))
            s = lax.dot_general(qb, kb, (((1,), (1,)), ((), ())), preferred_element_type=F32)
            p = dmat * s
            inter = jnp.exp(m_prev_h - mu_col)
            state = bb * heads + h
            c_h = c_scr[state]
            n_h = n_scr[state]
            num = inter * jnp.dot(qb, c_h.astype(BF16), preferred_element_type=F32) \
                + jnp.dot(p.astype(BF16), vb, preferred_element_type=F32)
            den = inter * jnp.sum(q * n_h, axis=-1, keepdims=True) + jnp.sum(p, axis=-1, keepdims=True)
            floor = jnp.exp(-(cumf[:, h:h + 1] + mu_col))
            h_out = num / jnp.maximum(jnp.abs(den), floor)

            wg = jnp.exp(a_col - mu_last_h)
            decay = jnp.exp(m_prev_h - mu_last_h)
            kw = k * wg
            c_scr[state] = decay * c_h + lax.dot_general(kw.astype(BF16), vb, (((0,), (0,)), ((), ())),
                                                         preferred_element_type=F32)
            n_scr[state] = decay * n_h + jnp.sum(kw, axis=0, keepdims=True)

            gated = _sigmoid(o_ref[bb, rows, cols]) * h_out
            y_ref[bb, rows, cols] = _head_norm(gated, nw_ref[:, cols])
        m_scr[bb] = cumf[chunk - 1:chunk, :] + mu_last


def _mlstm(proj, gates, gate_b, norm_w, *, batch, seq, width, col0):
    t = proj.shape[0]
    heads = width // HEAD_DIM
    nb = SCAN_BATCH if batch % SCAN_BATCH == 0 else 1
    proj3 = proj.reshape(batch, seq, proj.shape[1])
    gates3 = gates.reshape(batch, seq, LANES)
    blk = lambda col: pl.BlockSpec((nb, SCAN_BLOCK, width), lambda b, i: (b, i, col))
    y = pl.pallas_call(
        functools.partial(_mlstm_kernel, chunk=SCAN_CHUNK, heads=heads),
        grid=(batch // nb, seq // SCAN_BLOCK),
        in_specs=[blk(col0), blk(col0 + 1), blk(col0 + 2), blk(col0 + 3),
                  pl.BlockSpec((nb, SCAN_BLOCK, LANES), lambda b, i: (b, i, 0)),
                  pl.BlockSpec((1, LANES), lambda b, i: (0, 0)),
                  pl.BlockSpec((1, width), lambda b, i: (0, 0))],
        out_specs=pl.BlockSpec((nb, SCAN_BLOCK, width), lambda b, i: (b, i, 0)),
        out_shape=jax.ShapeDtypeStruct((batch, seq, width), F32),
        scratch_shapes=[pltpu.VMEM((nb * heads, HEAD_DIM, HEAD_DIM), F32),
                        pltpu.VMEM((nb * heads, 1, HEAD_DIM), F32),
                        pltpu.VMEM((nb, 1, LANES), F32)],
        compiler_params=_cp("parallel", "arbitrary"),
    )(proj3, proj3, proj3, proj3, gates3, gate_b, norm_w)
    return y.reshape(t, width)


def _ret_kernel(q_ref, k_ref, v_ref, g_ref, cos_ref, sin_ref, intra_ref, cross_ref, zeta_ref, nw_ref,
                y_ref, r_scr, *, chunk, heads, chunk_decay):
    @pl.when(pl.program_id(1) == 0)
    def _():
        r_scr[...] = jnp.zeros_like(r_scr)

    tb = q_ref.shape[0]
    hd = HEAD_DIM
    scale = hd ** -0.5
    for c in range(tb // chunk):
        rows = slice(c * chunk, (c + 1) * chunk)
        cos = cos_ref[rows, :]
        sin = sin_ref[rows, :]
        for h in range(heads):
            cols = slice(h * hd, (h + 1) * hd)
            q = q_ref[rows, cols]
            k = k_ref[rows, cols]
            q = q * cos + pltpu.roll(q, hd // 2, axis=1) * sin
            k = (k * cos + pltpu.roll(k, hd // 2, axis=1) * sin) * scale
            vb = v_ref[rows, cols].astype(BF16)
            qb = q.astype(BF16)
            s = lax.dot_general(qb, k.astype(BF16), (((1,), (1,)), ((), ())), preferred_element_type=F32)
            inner = jnp.dot((s * intra_ref[h]).astype(BF16), vb, preferred_element_type=F32)
            r_h = r_scr[h]
            crs = jnp.dot(qb, r_h.astype(BF16), preferred_element_type=F32) * cross_ref[h]
            r_scr[h] = chunk_decay[h] * r_h + lax.dot_general(
                (k * zeta_ref[h]).astype(BF16), vb, (((0,), (0,)), ((), ())), preferred_element_type=F32)
            y_ref[rows, cols] = _silu(g_ref[rows, cols]) * _head_norm(inner + crs, nw_ref[:, cols])


def _retention(proj, norm_w, *, batch, seq, width, col0):
    t = proj.shape[0]
    heads = width // HEAD_DIM
    chunk = SCAN_CHUNK
    n_tblk = seq // SCAN_BLOCK
    half = HEAD_DIM // 2
    inv = np.float32(ROPE_BASE) ** (-np.arange(half, dtype=np.float32) / np.float32(half))
    ang = (np.arange(seq, dtype=np.float32)[:, None] * inv[None, :]).astype(np.float64)
    cos_t = jnp.asarray(np.concatenate([np.cos(ang), np.cos(ang)], axis=-1), F32)
    sin_t = jnp.asarray(np.concatenate([-np.sin(ang), np.sin(ang)], axis=-1), F32)
    log_g = jnp.log(1.0 - 2.0 ** (-5.0 - jnp.arange(heads, dtype=F32)))
    tt = jnp.arange(chunk, dtype=F32)
    lag = tt[:, None] - tt[None, :]
    intra = jnp.where(lag >= 0, jnp.exp(jnp.maximum(lag, 0.0)[None] * log_g[:, None, None]), 0.0)
    cross = jnp.broadcast_to(jnp.exp((tt + 1.0)[None, :] * log_g[:, None])[:, :, None], (heads, chunk, HEAD_DIM))
    zeta = jnp.broadcast_to(jnp.exp((chunk - 1.0 - tt)[None, :] * log_g[:, None])[:, :, None],
                            (heads, chunk, HEAD_DIM))
    chunk_decay = tuple(float((1.0 - 2.0 ** (-5.0 - h)) ** chunk) for h in range(heads))

    blk = lambda col: pl.BlockSpec((SCAN_BLOCK, width), lambda b, i: (b * n_tblk + i, col))
    pos = pl.BlockSpec((SCAN_BLOCK, HEAD_DIM), lambda b, i: (i, 0))
    full3 = lambda a: pl.BlockSpec(a.shape, lambda b, i: (0, 0, 0))
    return pl.pallas_call(
        functools.partial(_ret_kernel, chunk=chunk, heads=heads, chunk_decay=chunk_decay),
        grid=(batch, n_tblk),
        in_specs=[blk(col0), blk(col0 + 1), blk(col0 + 2), blk(col0 + 3), pos, pos,
                  full3(intra), full3(cross), full3(zeta),
                  pl.BlockSpec((1, width), lambda b, i: (0, 0))],
        out_specs=pl.BlockSpec((SCAN_BLOCK, width), lambda b, i: (b * n_tblk + i, 0)),
        out_shape=jax.ShapeDtypeStruct((t, width), F32),
        scratch_shapes=[pltpu.VMEM((heads, HEAD_DIM, HEAD_DIM), F32)],
        compiler_params=_cp("parallel", "arbitrary"),
    )(proj, proj, proj, proj, cos_t, sin_t, intra, cross, zeta, norm_w)


def _merge_kernel(x_ref, *refs):
    y_refs, wg_refs = refs[:N_BRANCH], refs[N_BRANCH:2 * N_BRANCH]
    wb_ref, o_ref, xb_ref = refs[2 * N_BRANCH:]

    @pl.when(pl.program_id(1) == 0)
    def _():
        xb_ref[...] = x_ref[...].astype(BF16)

    xb = xb_ref[...]
    acc = None
    for n in range(N_BRANCH):
        gate = _sigmoid(jnp.dot(xb, wg_refs[n][...], preferred_element_type=F32))
        term = gate * jnp.dot(y_refs[n][...].astype(BF16), wb_ref[n], preferred_element_type=F32)
        acc = term if acc is None else acc + term
    o_ref[...] = acc


def _merge(x, branches, w_gate, gate_col0, w_branch, *, tm, tn):
    t, d = x.shape
    width = branches[0].shape[1]
    ybs = pl.BlockSpec((tm, width), lambda i, j: (i, 0))

    def gate_spec(n):
        return pl.BlockSpec((d, tn), lambda i, j: (0, (gate_col0 + n * d) // tn + j))

    return pl.pallas_call(
        _merge_kernel,
        grid=(t // tm, d // tn),
        in_specs=[pl.BlockSpec((tm, d), lambda i, j: (i, 0))] + [ybs] * N_BRANCH
                 + [gate_spec(n) for n in range(N_BRANCH)]
                 + [pl.BlockSpec((N_BRANCH, width, tn), lambda i, j: (0, 0, j))],
        out_specs=pl.BlockSpec((tm, tn), lambda i, j: (i, j)),
        out_shape=jax.ShapeDtypeStruct((t, d), F32),
        scratch_shapes=[pltpu.VMEM((tm, d), BF16)],
        compiler_params=_cp("parallel", "arbitrary"),
    )(x, *branches, *([w_gate] * N_BRANCH), w_branch)


def _proj_ln_kernel(a_ref, w_ref, r_ref, lw_ref, lb_ref, o_ref, *, alpha):
    y = jnp.dot(a_ref[...].astype(BF16), w_ref[...], preferred_element_type=F32)
    o_ref[...] = _layer_norm(alpha * r_ref[...] + y, lw_ref[...], lb_ref[...])


def _proj_ln(a, w, resid, ln_w, ln_b, *, alpha, tm):
    t, k = a.shape
    d = w.shape[1]
    row = lambda n: pl.BlockSpec((tm, n), lambda i: (i, 0))
    const = lambda shape: pl.BlockSpec(shape, lambda i: (0, 0))
    return pl.pallas_call(
        functools.partial(_proj_ln_kernel, alpha=alpha),
        grid=(t // tm,),
        in_specs=[row(k), const((k, d)), row(d), const((1, d)), const((1, d))],
        out_specs=row(d),
        out_shape=jax.ShapeDtypeStruct((t, d), F32),
        compiler_params=_cp("parallel"),
    )(a, w, resid, ln_w, ln_b)


def _xattn_kernel(x_ref, wq_ref, kv_ref, wo_ref, lw_ref, lb_ref, o_ref, *, alpha, heads):
    x = x_ref[...]
    hd = HEAD_DIM
    inner = heads * hd
    q = jnp.dot(x.astype(BF16), wq_ref[...], preferred_element_type=F32)
    outs = []
    for h in range(heads):
        qh = q[:, h * hd:(h + 1) * hd].astype(BF16)
        kh = kv_ref[:, h * hd:(h + 1) * hd].astype(BF16)
        vh = kv_ref[:, inner + h * hd:inner + (h + 1) * hd].astype(BF16)
        s = lax.dot_general(qh, kh, (((1,), (1,)), ((), ())), preferred_element_type=F32) * hd ** -0.5
        s = s - jnp.max(s, axis=-1, keepdims=True)
        e = jnp.exp(s)
        p = e / jnp.sum(e, axis=-1, keepdims=True)
        outs.append(jnp.dot(p.astype(BF16), vh, preferred_element_type=F32).astype(BF16))
    o = jnp.concatenate(outs, axis=-1)
    y = jnp.dot(o, wo_ref[...], preferred_element_type=F32)
    o_ref[...] = _layer_norm(alpha * x + y, lw_ref[...], lb_ref[...])


def _xattn(x, kv, wq, wo, ln_w, ln_b, *, alpha, seq, mem_len, tm):
    t, d = x.shape
    inner = wq.shape[1]
    n_tblk = seq // tm
    const = lambda shape: pl.BlockSpec(shape, lambda i: (0, 0))
    return pl.pallas_call(
        functools.partial(_xattn_kernel, alpha=alpha, heads=XATTN_HEADS),
        grid=(t // tm,),
        in_specs=[pl.BlockSpec((tm, d), lambda i: (i, 0)), const((d, inner)),
                  pl.BlockSpec((mem_len, 2 * inner), lambda i: (i // n_tblk, 0)),
                  const((inner, d)), const((1, d)), const((1, d))],
        out_specs=pl.BlockSpec((tm, d), lambda i: (i, 0)),
        out_shape=jax.ShapeDtypeStruct((t, d), F32),
        compiler_params=_cp("parallel"),
    )(x, wq, kv, wo, ln_w, ln_b)


def _round_up_pow2(x, m):
    shift = m.bit_length() - 1
    return jnp.left_shift(jnp.right_shift(x + (m - 1), shift), shift)


def _route_kernel(x_ref, wr_ref, rb_ref, w_ref, lrow_ref, segtab_ref, blk_ref, seg_ref, size_all, tot, *, tm, bm):
    step = pl.program_id(0)

    @pl.when(step == 0)
    def _():
        tot[...] = jnp.zeros_like(tot)

    e_n, g_n = N_EXPERTS, N_GROUPS
    per = e_n // g_n
    logits = lax.dot_general(wr_ref[...], x_ref[...], (((1,), (1,)), ((), ())),
                             precision=lax.Precision.HIGHEST, preferred_element_type=F32)
    scores = _sigmoid(logits)
    biased = scores + rb_ref[...]
    b3 = biased.reshape(g_n, per, tm)
    member = lax.broadcasted_iota(I32, (g_n, per, tm), 1)
    top1 = jnp.max(b3, axis=1, keepdims=True)
    first = jnp.min(jnp.where(b3 == top1, member, per), axis=1, keepdims=True)
    top2 = jnp.max(jnp.where(member == first, -jnp.inf, b3), axis=1, keepdims=True)
    gs = top1 + top2
    gid = lax.broadcasted_iota(I32, (g_n, 1, tm), 0)
    rank = jnp.zeros((g_n, 1, tm), I32)
    for other in range(g_n):
        o = gs[other:other + 1]
        ahead = jnp.logical_or(o > gs, jnp.logical_and(o == gs, other < gid))
        rank = rank + jnp.where(ahead, 1, 0)
    cur = jnp.where(rank < TOPK_GROUPS, b3, -jnp.inf).reshape(e_n, tm)

    eid = lax.broadcasted_iota(I32, (e_n, tm), 0)
    picks, vals = [], []
    sel = jnp.zeros((e_n, tm), F32)
    for k in range(TOP_K):
        mx = jnp.max(cur, axis=0, keepdims=True)
        ik = jnp.min(jnp.where(cur == mx, eid, e_n), axis=0, keepdims=True)
        hit = eid == ik
        vals.append(jnp.sum(jnp.where(hit, scores, 0.0), axis=0, keepdims=True))
        cur = jnp.where(hit, -jnp.inf, cur)
        sel = jnp.where(hit, 1.0, sel)
        picks.append(ik)
    total = vals[0]
    for v in vals[1:]:
        total = total + v

    tri = jnp.where(lax.broadcasted_iota(I32, (tm, tm), 0) <= lax.broadcasted_iota(I32, (tm, tm), 1), 1.0, 0.0)
    incl = jnp.dot(sel.astype(BF16), tri.astype(BF16), preferred_element_type=F32)
    size = _round_up_pow2(jnp.broadcast_to(incl[:, tm - 1:tm], (e_n, LANES)).astype(I32), SEG_ALIGN)
    loff = _cum_rows(size, jnp.add, 0) - size
    base = loff[:, 0:1].astype(F32) + incl - 1.0
    for k in range(TOP_K):
        w_ref[k:k + 1, :] = vals[k] / total * ROUTE_SCALE
        lrow_ref[0, k:k + 1, :] = jnp.sum(jnp.where(eid == picks[k], base, 0.0),
                                          axis=0, keepdims=True).astype(I32)
    size_all[step] = size
    tot[...] = tot[...] + size

    @pl.when(step == pl.num_programs(0) - 1)
    def _():
        rows = tot[...]
        pcnt = _round_up_pow2(rows, bm)
        pend = _cum_rows(pcnt, jnp.add, 0)

        def tile_seg(i, run):
            segtab_ref[i, 0] = run
            segtab_ref[i, 1] = size_all[i]
            return run + size_all[i]

        lax.fori_loop(0, pl.num_programs(0), tile_seg, pend - pcnt)

        nb = blk_ref.shape[2]
        row0 = lax.broadcasted_iota(I32, (e_n, nb), 1) * bm
        total_rows = pend[e_n - 1:e_n, 0:1]
        owner = jnp.sum(jnp.where(pend[:, 0:1] <= row0, 1, 0), axis=0, keepdims=True)
        last_owner = jnp.sum(jnp.where(pend[:, 0:1] < total_rows, 1, 0), axis=0, keepdims=True)
        blk_ref[0] = jnp.where(row0[0:1, :] < total_rows, jnp.minimum(owner, e_n - 1), last_owner)
        data_end = pend - pcnt + rows
        inside = jnp.logical_and(pend[:, 0:1] - pcnt[:, 0:1] <= row0, row0 < pend[:, 0:1])
        blk_ref[1] = jnp.sum(jnp.where(inside, jnp.clip(data_end[:, 0:1] - row0, 0, bm), 0), axis=0, keepdims=True)
        seg_ref[0] = pend
        seg_ref[1] = pcnt


def _route(x, router_w_t, router_b, *, tm, bm, n_blocks):
    t, d = x.shape
    e_n = N_EXPERTS
    n_t = t // tm
    assert bm & (bm - 1) == 0
    nb_pad = -(-n_blocks // LANES) * LANES
    return pl.pallas_call(
        functools.partial(_route_kernel, tm=tm, bm=bm),
        grid=(n_t,),
        in_specs=[pl.BlockSpec((tm, d), lambda i: (i, 0)),
                  pl.BlockSpec((e_n, d), lambda i: (0, 0)),
                  pl.BlockSpec((e_n, 1), lambda i: (0, 0))],
        out_specs=[pl.BlockSpec((TOP_K, tm), lambda i: (0, i)),
                   pl.BlockSpec((1, TOP_K, tm), lambda i: (i, 0, 0)),
                   pl.BlockSpec((n_t, 2, e_n, LANES), lambda i: (0, 0, 0, 0)),
                   pl.BlockSpec((2, 1, nb_pad), lambda i: (0, 0, 0)),
                   pl.BlockSpec((2, e_n, LANES), lambda i: (0, 0, 0))],
        out_shape=[jax.ShapeDtypeStruct((TOP_K, t), F32), jax.ShapeDtypeStruct((n_t, TOP_K, tm), I32),
                   jax.ShapeDtypeStruct((n_t, 2, e_n, LANES), I32), jax.ShapeDtypeStruct((2, 1, nb_pad), I32),
                   jax.ShapeDtypeStruct((2, e_n, LANES), I32)],
        scratch_shapes=[pltpu.VMEM((n_t, e_n, LANES), I32), pltpu.VMEM((e_n, LANES), I32)],
        compiler_params=_cp("arbitrary"),
    )(x, router_w_t, router_b)


def _tile_rows(tm):
    worst = TOP_K * tm + N_EXPERTS * (SEG_ALIGN - 1)
    return -(-worst // ONEHOT_ROWS) * ONEHOT_ROWS


def _onehot_rows(chunk, lrow, values, tm):
    rid = chunk * ONEHOT_ROWS + lax.broadcasted_iota(I32, (ONEHOT_ROWS, tm), 0)
    acc = jnp.zeros((ONEHOT_ROWS, tm), F32)
    for k in range(TOP_K):
        acc = jnp.where(rid == lrow[k:k + 1, :], 1.0 if values is None else values[k:k + 1, :], acc)
    return acc.astype(BF16)


def _segments(gstart_ref, size_ref, tile, make_copy, act, keep=None):
    def body(e, loff):
        n = size_ref[tile * N_EXPERTS + e]
        wanted = n > 0 if keep is None else jnp.logical_and(n > 0, keep(loff + n))

        @pl.when(wanted)
        def _():
            act(make_copy(pl.multiple_of(gstart_ref[tile * N_EXPERTS + e], SEG_ALIGN),
                          pl.multiple_of(loff, SEG_ALIGN), pl.multiple_of(n, SEG_ALIGN)))
        return loff + n
    return lax.fori_loop(0, N_EXPERTS, body, 0)


def _start(cp):
    cp.start()


def _wait(cp):
    cp.wait()


def _dispatch_kernel(gstart_ref, size_ref, x_ref, lrow_ref, xs_ref, stage, sem_a, sem_b, *, tm):
    i = pl.program_id(0)
    last = pl.num_programs(0) - 1
    n_chunks = stage.shape[0] // ONEHOT_ROWS
    split_chunk = n_chunks // 2
    split = split_chunk * ONEHOT_ROWS
    assert tm <= ONEHOT_ROWS

    def seg_walk(tile, act, phase_b):
        sem = sem_b if phase_b else sem_a
        return _segments(
            gstart_ref, size_ref, tile,
            lambda g, loff, n: pltpu.make_async_copy(stage.at[pl.ds(loff, n), :], xs_ref.at[pl.ds(g, n), :], sem),
            act, (lambda end: end > split) if phase_b else (lambda end: end <= split))

    xb = x_ref[...].astype(BF16)
    lrow = lrow_ref[0]
    n_rows = lax.fori_loop(0, N_EXPERTS, lambda e, s: s + size_ref[i * N_EXPERTS + e], 0)

    def chunks(lo, hi):
        for c in range(lo, hi):
            def one(c=c):
                stage[c * ONEHOT_ROWS:(c + 1) * ONEHOT_ROWS, :] = jnp.dot(
                    _onehot_rows(c, lrow, None, tm), xb, preferred_element_type=F32).astype(BF16)
            if c * ONEHOT_ROWS < TOP_K * tm:
                one()
            else:
                pl.when(c * ONEHOT_ROWS < n_rows)(one)

    prev = jnp.maximum(i - 1, 0)

    @pl.when(i > 0)
    def _():
        seg_walk(prev, _wait, False)
    chunks(0, split_chunk - 1)

    @pl.when(i > 0)
    def _():
        seg_walk(prev, _wait, True)
    chunks(split_chunk - 1, split_chunk)
    seg_walk(i, _start, False)
    chunks(split_chunk, n_chunks)
    seg_walk(i, _start, True)

    @pl.when(i == last)
    def _():
        seg_walk(i, _wait, False)
        seg_walk(i, _wait, True)


def _dispatch(x, lrow, gstart, size, *, rows, tm):
    t, d = x.shape
    return pl.pallas_call(
        functools.partial(_dispatch_kernel, tm=tm),
        grid_spec=pltpu.PrefetchScalarGridSpec(
            num_scalar_prefetch=2,
            grid=(t // tm,),
            in_specs=[pl.BlockSpec((tm, d), lambda i, *_: (i, 0)),
                      pl.BlockSpec((1, TOP_K, tm), lambda i, *_: (i, 0, 0))],
            out_specs=pl.BlockSpec(memory_space=pl.ANY),
            scratch_shapes=[pltpu.VMEM((_tile_rows(tm), d), BF16)] + [pltpu.SemaphoreType.DMA(())] * 2,
        ),
        out_shape=jax.ShapeDtypeStruct((rows, d), BF16),
        compiler_params=_cp("arbitrary"),
    )(gstart, size, x, lrow)


def _expert_kernel(start_ref, rows_ref, wgu_ref, wdn_ref, xs_ref, ys_ref, wgu_b, wdn_b, xbuf, ybuf, sem_in, sem_out,
                   *, bm):
    e = pl.program_id(0)
    row0 = start_ref[e]
    n = rows_ref[e]
    n_blk = (n + bm - 1) // bm

    @pl.when(e == 0)
    def _():
        xbuf[...] = jnp.zeros_like(xbuf)

    def row_dma(b, slot, act, load):
        off = pl.multiple_of(row0 + b * bm, SEG_ALIGN)
        left = n - b * bm

        def copy(r):
            if load:
                return pltpu.make_async_copy(xs_ref.at[pl.ds(off, r), :], xbuf.at[slot, pl.ds(0, r), :],
                                             sem_in.at[slot])
            return pltpu.make_async_copy(ybuf.at[slot, pl.ds(0, r), :], ys_ref.at[pl.ds(off, r), :],
                                         sem_out.at[slot])

        @pl.when(left >= bm)
        def _():
            act(copy(bm))

        @pl.when(left < bm)
        def _():
            act(copy(pl.multiple_of(left, SEG_ALIGN)))

    def start(cp):
        cp.start(priority=ROW_DMA_PRIORITY)

    @pl.when(n_blk > 0)
    def _():
        row_dma(0, 0, start, True)

    wgu_b[...] = wgu_ref[...].astype(BF16)
    wdn_b[...] = wdn_ref[...].astype(BF16)

    def block(b, slot):
        @pl.when(b < n_blk)
        def _():
            @pl.when(b + 1 < n_blk)
            def _():
                row_dma(b + 1, 1 - slot, start, True)

            row_dma(b, slot, _wait, True)

            @pl.when(b >= 2)
            def _():
                row_dma(b - 2, slot, _wait, False)

            f = wdn_b.shape[0]
            gu = jnp.dot(xbuf[slot], wgu_b[...], preferred_element_type=F32)
            hidden = (_silu(gu[:, :f]) * gu[:, f:]).astype(BF16)
            ybuf[slot] = jnp.dot(hidden, wdn_b[...], preferred_element_type=F32).astype(BF16)
            row_dma(b, slot, start, False)

    def pair(p, carry):
        block(2 * p, 0)
        block(2 * p + 1, 1)
        return carry

    lax.fori_loop(0, (n_blk + 1) // 2, pair, 0)

    for back in (2, 1):
        for slot in range(2):
            @pl.when(jnp.logical_and(n_blk >= back, (n_blk - back) % 2 == slot))
            def _():
                row_dma(n_blk - back, slot, _wait, False)


def _experts(xs, start, rows, w_gu, w_dn, layer, *, bm):
    total, d = xs.shape
    f2 = w_gu.shape[3]
    f = w_dn.shape[2]
    return pl.pallas_call(
        functools.partial(_expert_kernel, bm=bm),
        grid_spec=pltpu.PrefetchScalarGridSpec(
            num_scalar_prefetch=2,
            grid=(N_EXPERTS,),
            in_specs=[pl.BlockSpec((None, None, d, f2), lambda e, *_: (layer, e, 0, 0)),
                      pl.BlockSpec((None, None, f, d), lambda e, *_: (layer, e, 0, 0)),
                      pl.BlockSpec(memory_space=pl.ANY)],
            out_specs=pl.BlockSpec(memory_space=pl.ANY),
            scratch_shapes=[pltpu.VMEM((d, f2), BF16), pltpu.VMEM((f, d), BF16),
                            pltpu.VMEM((2, bm, d), BF16), pltpu.VMEM((2, bm, d), BF16),
                            pltpu.SemaphoreType.DMA((2,)), pltpu.SemaphoreType.DMA((2,))],
        ),
        out_shape=jax.ShapeDtypeStruct((total, d), BF16),
        compiler_params=_cp("arbitrary"),
    )(start, rows, w_gu, w_dn, xs)


def _expert_block_kernel(blk_ref, nused_ref, xs_ref, wgu_ref, wdn_ref, ys_ref, wgu_b, wdn_b, *, n_blocks, bm):
    j = pl.program_id(0)
    valid = blk_ref[n_blocks + j]
    changed = jnp.logical_or(j == 0, blk_ref[j] != blk_ref[jnp.maximum(j - 1, 0)])

    @pl.when(jnp.logical_and(valid > 0, changed))
    def _():
        wgu_b[...] = wgu_ref[...].astype(BF16)
        wdn_b[...] = wdn_ref[...].astype(BF16)

    def swiglu(rows, masked):
        x = xs_ref[0:rows, :]
        if masked:
            x = jnp.where(lax.broadcasted_iota(I32, x.shape, 0) < valid, x, jnp.zeros_like(x))
        f = wdn_b.shape[0]
        gu = jnp.dot(x, wgu_b[...], preferred_element_type=F32)
        hidden = (_silu(gu[:, :f]) * gu[:, f:]).astype(BF16)
        ys_ref[0:rows, :] = jnp.dot(hidden, wdn_b[...], preferred_element_type=F32).astype(BF16)

    @pl.when(valid == bm)
    def _():
        swiglu(bm, False)

    quarter = bm // 4
    for q in range(1, 5):
        @pl.when(jnp.logical_and(jnp.logical_and(valid > (q - 1) * quarter, valid <= q * quarter), valid < bm))
        def _():
            swiglu(q * quarter, True)


def _experts_blocked(xs, blk, nused, w_gu, w_dn, layer, *, n_blocks, bm):
    rows, d = xs.shape
    f2 = w_gu.shape[3]
    f = w_dn.shape[2]
    row_blk = lambda j, bl, nu: (jnp.minimum(j, nu[0] - 1), 0)
    return pl.pallas_call(
        functools.partial(_expert_block_kernel, n_blocks=n_blocks, bm=bm),
        grid_spec=pltpu.PrefetchScalarGridSpec(
            num_scalar_prefetch=2,
            grid=(n_blocks,),
            in_specs=[pl.BlockSpec((bm, d), row_blk),
                      pl.BlockSpec((None, None, d, f2), lambda j, bl, nu: (layer, bl[j], 0, 0)),
                      pl.BlockSpec((None, None, f, d), lambda j, bl, nu: (layer, bl[j], 0, 0))],
            out_specs=pl.BlockSpec((bm, d), row_blk),
            scratch_shapes=[pltpu.VMEM((d, f2), BF16), pltpu.VMEM((f, d), BF16)],
        ),
        out_shape=jax.ShapeDtypeStruct((rows, d), BF16),
        compiler_params=_cp("arbitrary"),
    )(blk, nused, xs, w_gu, w_dn)


def _combine_kernel(gstart_ref, size_ref, x_ref, lrow_ref, w_ref, sdn_ref, lw_ref, lb_ref, sgu_hbm, ys_ref,
                    o_ref, ybuf, wt, hid, sgu, sems, wsem, *, tm, alpha):
    i, half = pl.program_id(0), pl.program_id(1)
    last_tile = pl.num_programs(0) - 1
    dh = ybuf.shape[2]
    n_chunks = ybuf.shape[1] // ONEHOT_ROWS
    sure_chunks = (TOP_K * tm) // ONEHOT_ROWS

    def contract_rows(w_rows, y_rows):
        return lax.dot_general(w_rows, y_rows, (((0,), (0,)), ((), ())), preferred_element_type=F32)

    def seg_walk(tile, hf, act):
        return _segments(
            gstart_ref, size_ref, tile,
            lambda g, loff, n: pltpu.make_async_copy(ys_ref.at[pl.ds(g, n), hf * dh:(hf + 1) * dh],
                                                     ybuf.at[hf, pl.ds(loff, n), :], sems.at[hf]),
            act)

    @pl.when(jnp.logical_and(i == 0, half == 0))
    def _():
        ybuf[...] = jnp.zeros_like(ybuf)
        weights = pltpu.make_async_copy(sgu_hbm, sgu, wsem)
        weights.start()
        weights.wait()
        seg_walk(0, 0, _start)

    @pl.when(half == 0)
    def _():
        seg_walk(i, 1, _start)
        f = hid.shape[1]
        gu = jnp.dot(x_ref[...].astype(BF16), sgu[...], preferred_element_type=F32)
        hidden = (_silu(gu[:, :f]) * gu[:, f:]).astype(BF16)
        hid[...] = hidden
        n_rows = seg_walk(i, 0, _wait)
        lrow, w = lrow_ref[0], w_ref[...]
        acc = jnp.dot(hidden, sdn_ref[...], preferred_element_type=F32)
        for c in range(sure_chunks):
            rows = slice(c * ONEHOT_ROWS, (c + 1) * ONEHOT_ROWS)
            w_rows = _onehot_rows(c, lrow, w, tm)
            wt[rows, :] = w_rows
            acc = acc + contract_rows(w_rows, ybuf[0, rows, :])
        o_ref[:, 0:dh] = acc
        for c in range(sure_chunks, n_chunks):
            @pl.when(c * ONEHOT_ROWS < n_rows)
            def _():
                rows = slice(c * ONEHOT_ROWS, (c + 1) * ONEHOT_ROWS)
                w_rows = _onehot_rows(c, lrow, w, tm)
                wt[rows, :] = w_rows
                o_ref[:, 0:dh] += contract_rows(w_rows, ybuf[0, rows, :])

    @pl.when(half == 1)
    def _():
        @pl.when(i < last_tile)
        def _():
            seg_walk(i + 1, 0, _start)
        n_rows = seg_walk(i, 1, _wait)
        sure = slice(0, sure_chunks * ONEHOT_ROWS)
        o_ref[:, dh:2 * dh] = contract_rows(wt[sure, :], ybuf[1, sure, :]) \
            + jnp.dot(hid[...], sdn_ref[...], preferred_element_type=F32)
        for c in range(sure_chunks, n_chunks):
            @pl.when(c * ONEHOT_ROWS < n_rows)
            def _():
                rows = slice(c * ONEHOT_ROWS, (c + 1) * ONEHOT_ROWS)
                o_ref[:, dh:2 * dh] += contract_rows(wt[rows, :], ybuf[1, rows, :])
        o_ref[...] = _layer_norm(alpha * x_ref[...] + o_ref[...], lw_ref[...], lb_ref[...])


def _combine(x, ys, lrow, wts, gstart, size, s_gu, s_dn, ln_w, ln_b, *, alpha, tm):
    t, d = x.shape
    dh = d // 2
    f = s_dn.shape[0]
    const = lambda shape: pl.BlockSpec(shape, lambda i, h, *_: (0, 0))
    return pl.pallas_call(
        functools.partial(_combine_kernel, tm=tm, alpha=alpha),
        grid_spec=pltpu.PrefetchScalarGridSpec(
            num_scalar_prefetch=2,
            grid=(t // tm, 2),
            in_specs=[pl.BlockSpec((tm, d), lambda i, h, *_: (i, 0)),
                      pl.BlockSpec((1, TOP_K, tm), lambda i, h, *_: (i, 0, 0)),
                      pl.BlockSpec((TOP_K, tm), lambda i, h, *_: (0, i)),
                      pl.BlockSpec((f, dh), lambda i, h, *_: (0, h)),
                      const((1, d)), const((1, d)),
                      pl.BlockSpec(memory_space=pl.ANY), pl.BlockSpec(memory_space=pl.ANY)],
            out_specs=pl.BlockSpec((tm, d), lambda i, h, *_: (i, 0)),
            scratch_shapes=[pltpu.VMEM((2, _tile_rows(tm), dh), BF16), pltpu.VMEM((_tile_rows(tm), tm), BF16),
                            pltpu.VMEM((tm, f), BF16), pltpu.VMEM(s_gu.shape, BF16),
                            pltpu.SemaphoreType.DMA((2,)), pltpu.SemaphoreType.DMA(())],
        ),
        out_shape=jax.ShapeDtypeStruct((t, d), F32),
        compiler_params=_cp("arbitrary", "arbitrary"),
    )(gstart, size, x, lrow, wts, s_dn, ln_w, ln_b, s_gu, ys)


def _mixer_sublayer(x, w_in_all, layer, gate_b, pool_w, pool_scale, conv_w, mlstm_norm_w, ret_norm_w, w_branch,
                    w_out, ln_w, ln_b, *, batch, seq, alpha):
    t, d = x.shape
    width = d // N_BRANCH
    heads = width // HEAD_DIM
    gate_off = 8 * width
    ret_off = gate_off
    g_off = ret_off + 4 * width
    if_off = g_off + N_BRANCH * d
    w_bf16 = _realign_cast(w_in_all, layer, lo_col=gate_off, hi_col=if_off, shift=2 * heads, tr=512, tn=512)
    gate_bias = jnp.pad(gate_b, (0, LANES - 2 * heads)).reshape(1, LANES)

    proj_a = _matmul(x, w_bf16, tm=1024, tn=1024, ncols=gate_off)
    proj_b = _matmul(x, w_bf16, tm=1024, tn=1024, ncols=4 * width, col0=ret_off)
    gates = _matmul(x, w_bf16, tm=1024, tn=LANES, ncols=LANES, col0=if_off)
    y_pool, y_conv = _pool_conv(proj_a, pool_w.astype(BF16), pool_scale.reshape(1, width), conv_w,
                                seq=seq, width=width, tb=512)
    y_mlstm = _mlstm(proj_a, gates, gate_bias, mlstm_norm_w.reshape(1, width),
                     batch=batch, seq=seq, width=width, col0=4)
    y_ret = _retention(proj_b, ret_norm_w.reshape(1, width), batch=batch, seq=seq, width=width, col0=0)
    merged = _merge(x, (y_pool, y_conv, y_mlstm, y_ret), w_bf16, g_off, w_branch.astype(BF16),
                    tm=512, tn=512)
    return _proj_ln(merged, w_out.astype(BF16), x, ln_w, ln_b, alpha=alpha, tm=512)


def _xattn_sublayer(x, mem2d, wq, wk, wv, wo, ln_w, ln_b, *, seq, mem_len, alpha):
    w_kv = jnp.concatenate([wk, wv], axis=1).astype(BF16)
    kv = _matmul(mem2d, w_kv, tm=min(mem2d.shape[0], 1024), tn=512)
    return _xattn(x, kv, wq.astype(BF16), wo.astype(BF16), ln_w, ln_b,
                  alpha=alpha, seq=seq, mem_len=mem_len, tm=512)


def _moe_sublayer(x, router_w, router_b, w_gu, w_dn, layer, s_gu, s_dn, ln_w, ln_b, *, alpha):
    t, d = x.shape
    e_n, bm = N_EXPERTS, MOE_BM
    tm = ROUTE_TM
    n_blocks = -(-(t * TOP_K + (t // tm) * e_n * (SEG_ALIGN - 1)) // bm) + e_n
    wts, lrow, segtab, blk, seg = _route(x, router_w.T, router_b.reshape(e_n, 1), tm=tm, bm=bm, n_blocks=n_blocks)
    gstart, size = segtab[:, 0, :, 0].reshape(-1), segtab[:, 1, :, 0].reshape(-1)
    nused = seg[0, e_n - 1, 0] // bm

    xs = _dispatch(x, lrow, gstart, size, rows=n_blocks * bm, tm=tm)
    ys = _experts_blocked(xs, blk[:, 0, :n_blocks].reshape(-1), nused.reshape(1), w_gu, w_dn, layer,
                          n_blocks=n_blocks, bm=bm)
    return _combine(x, ys, lrow, wts, gstart, size, s_gu.astype(BF16), s_dn.astype(BF16), ln_w, ln_b,
                    alpha=alpha, tm=tm)


def kernel(x, mem, w_in, mlstm_gate_b, pool_w, pool_scale, conv_w, mlstm_norm_w, ret_norm_w, w_branch,
           w_mix_out, xa_wq, xa_wk, xa_wv, xa_wo, router_w, router_b, moe_w_gu, moe_w_dn, shared_w_gu,
           shared_w_dn, ln_w, ln_b):
    batch, seq, d = x.shape
    depth = w_in.shape[0]
    mem_len = mem.shape[1]
    alpha = (2 * depth) ** 0.25
    h = x.reshape(batch * seq, d)
    mem2d = mem.reshape(batch * mem_len, d)
    for l in range(depth):
        lw = ln_w[l].reshape(3, 1, d)
        lb = ln_b[l].reshape(3, 1, d)
        h = _mixer_sublayer(h, w_in, l, mlstm_gate_b[l], pool_w[l], pool_scale[l], conv_w[l], mlstm_norm_w[l],
                            ret_norm_w[l], w_branch[l], w_mix_out[l], lw[0], lb[0],
                            batch=batch, seq=seq, alpha=alpha)
        h = _xattn_sublayer(h, mem2d, xa_wq[l], xa_wk[l], xa_wv[l], xa_wo[l], lw[1], lb[1],
                            seq=seq, mem_len=mem_len, alpha=alpha)
        h = _moe_sublayer(h, router_w[l], router_b[l], moe_w_gu, moe_w_dn, l, shared_w_gu[l], shared_w_dn[l],
                          lw[2], lb[2], alpha=alpha)
    return h.reshape(batch, seq, d)
```

```python
import functools

import numpy as np
import jax
import jax.numpy as jnp
from jax import lax
from jax.experimental import pallas as pl
from jax.experimental.pallas import tpu as pltpu

F32 = jnp.float32
BF16 = jnp.bfloat16
I32 = jnp.int32

N_BRANCH = 4
HEAD_DIM = 128
POOL_WINDOWS = (2, 4, 8, 16)
CONV_WIDTH = 3
ROPE_BASE = 10000.0
XATTN_HEADS = 4
N_EXPERTS = 64
TOP_K = 8
N_GROUPS = 8
TOPK_GROUPS = 4
ROUTE_SCALE = 2.5
LN_EPS = 1e-5

LANES = 128
V7X_VMEM_BYTES = 64 * 1024 * 1024
VMEM_LIMIT = 56 * 1024 * 1024

SCAN_CHUNK = 256
SCAN_BLOCK = 512
SCAN_BATCH = 2
HALO = 16
MOE_BM = 1024
ROUTE_TM = 256
SEG_ALIGN = 16
ONEHOT_ROWS = 512

def _cp(*sem):
    return pltpu.CompilerParams(dimension_semantics=sem, vmem_limit_bytes=VMEM_LIMIT)


def _sigmoid(x):
    return 1.0 / (1.0 + jnp.exp(-x))


def _silu(x):
    return x * _sigmoid(x)


def _log_sigmoid(x):
    return jnp.minimum(x, 0.0) - jnp.log(1.0 + jnp.exp(-jnp.abs(x)))


def _layer_norm(z, w, b):
    mu = jnp.mean(z, axis=-1, keepdims=True)
    d = z - mu
    var = jnp.mean(d * d, axis=-1, keepdims=True)
    return d * lax.rsqrt(var + LN_EPS) * w + b


def _head_norm(h, w):
    mu = jnp.mean(h, axis=-1, keepdims=True)
    d = h - mu
    var = jnp.mean(d * d, axis=-1, keepdims=True)
    return d * lax.rsqrt(var + LN_EPS) * w


def _mm_kernel(x_ref, w_ref, o_ref, xb_ref):
    @pl.when(pl.program_id(1) == 0)
    def _():
        xb_ref[...] = x_ref[...].astype(BF16)

    o_ref[...] = jnp.dot(xb_ref[...], w_ref[...], preferred_element_type=F32)


def _realign_cast_kernel(a_ref, b_ref, o_ref, *, shift, lo, hi):
    j = pl.program_id(1)
    tn = o_ref.shape[1]
    shifted = jnp.logical_and(j >= lo, j < hi)

    @pl.when(shifted)
    def _():
        both = jnp.concatenate([a_ref[...], b_ref[...]], axis=1)
        o_ref[...] = both[:, shift:shift + tn].astype(BF16)

    @pl.when(jnp.logical_not(shifted))
    def _():
        o_ref[...] = a_ref[...].astype(BF16)


def _realign_cast(w_all, layer, *, lo_col, hi_col, shift, tr, tn):
    _, rows, _ = w_all.shape
    lo, hi = lo_col // tn, hi_col // tn
    src = lambda j: jnp.where(j == hi, lo, j)
    return pl.pallas_call(
        functools.partial(_realign_cast_kernel, shift=shift, lo=lo, hi=hi),
        grid=(rows // tr, hi + 1),
        in_specs=[pl.BlockSpec((None, tr, tn), lambda i, j: (layer, i, src(j))),
                  pl.BlockSpec((None, tr, LANES), lambda i, j: (layer, i, (src(j) + 1) * (tn // LANES)))],
        out_specs=pl.BlockSpec((tr, tn), lambda i, j: (i, j)),
        out_shape=jax.ShapeDtypeStruct((rows, hi_col + tn), BF16),
        compiler_params=_cp("parallel", "parallel"),
    )(w_all, w_all)


def _matmul(x, w, *, tm, tn, ncols=None, col0=0):
    t, k = x.shape
    n = w.shape[1] if ncols is None else ncols
    return pl.pallas_call(
        _mm_kernel,
        grid=(t // tm, n // tn),
        in_specs=[pl.BlockSpec((tm, k), lambda i, j: (i, 0)),
                  pl.BlockSpec((k, tn), lambda i, j: (0, col0 // tn + j))],
        out_specs=pl.BlockSpec((tm, tn), lambda i, j: (i, j)),
        out_shape=jax.ShapeDtypeStruct((t, n), F32),
        scratch_shapes=[pltpu.VMEM((tm, k), BF16)],
        compiler_params=_cp("parallel", "arbitrary"),
    )(x, w)


def _poolconv_kernel(u_ref, uh_ref, h_ref, hh_ref, b_ref, c_ref, ch_ref, pw_ref, ps_ref, cw_ref,
                     yp_ref, yc_ref, ubuf, zbuf, *, tb, n_tblk):
    first = (pl.program_id(0) % n_tblk) == 0
    ubuf[0:HALO, :] = jnp.where(first, 0.0, uh_ref[...])
    ubuf[HALO:HALO + tb, :] = u_ref[...]
    zbuf[0:HALO, :] = jnp.where(first, 0.0, ch_ref[...] * hh_ref[...])
    zbuf[HALO:HALO + tb, :] = c_ref[...] * h_ref[...]

    t_pos = (pl.program_id(0) % n_tblk) * tb + lax.broadcasted_iota(I32, (tb, LANES), 0)
    gw = u_ref.shape[1] // len(POOL_WINDOWS)
    for grp, win in enumerate(POOL_WINDOWS):
        lanes = slice(grp * gw, (grp + 1) * gw)
        cur = ubuf[HALO:HALO + tb, lanes]
        acc = cur
        for lag in range(1, win):
            acc = acc + ubuf[HALO - lag:HALO - lag + tb, lanes]
        count = jnp.minimum(t_pos + 1, win).astype(F32)
        mixed = acc / count - cur
        y = jnp.dot(mixed.astype(BF16), pw_ref[grp], preferred_element_type=F32)
        yp_ref[:, lanes] = y * ps_ref[:, lanes]

    conv = cw_ref[0:1, :] * zbuf[HALO:HALO + tb, :]
    for lag in range(1, CONV_WIDTH):
        conv = conv + cw_ref[lag:lag + 1, :] * zbuf[HALO - lag:HALO - lag + tb, :]
    yc_ref[...] = b_ref[...] * conv


def _pool_conv(proj, pool_w, pool_scale, conv_w, *, seq, width, tb):
    t = proj.shape[0]
    n_tblk = seq // tb
    ratio = tb // HALO

    def cur(col):
        return pl.BlockSpec((tb, width), lambda g: (g, col))

    def halo(col):
        return pl.BlockSpec((HALO, width), lambda g: (jnp.maximum(g * ratio - 1, 0), col))

    full = lambda shape: pl.BlockSpec(shape, lambda g: (0,) * len(shape))
    return pl.pallas_call(
        functools.partial(_poolconv_kernel, tb=tb, n_tblk=n_tblk),
        grid=(t // tb,),
        in_specs=[cur(0), halo(0), cur(1), halo(1), cur(2), cur(3), halo(3),
                  full(pool_w.shape), full(pool_scale.shape), full(conv_w.shape)],
        out_specs=[pl.BlockSpec((tb, width), lambda g: (g, 0))] * 2,
        out_shape=[jax.ShapeDtypeStruct((t, width), F32)] * 2,
        scratch_shapes=[pltpu.VMEM((HALO + tb, width), F32)] * 2,
        compiler_params=_cp("parallel"),
    )(proj, proj, proj, proj, proj, proj, proj, pool_w, pool_scale, conv_w)


def _cum_rows(x, op, fill):
    n = x.shape[0]
    row = lax.broadcasted_iota(I32, x.shape, 0)
    shift = 1
    while shift < n:
        x = op(x, jnp.where(row >= shift, pltpu.roll(x, shift, axis=0), fill))
        shift *= 2
    return x


def _mlstm_kernel(q_ref, k_ref, v_ref, o_ref, g_ref, gb_ref, nw_ref, y_ref, c_scr, n_scr, m_scr,
                  *, chunk, heads):
    @pl.when(pl.program_id(1) == 0)
    def _():
        c_scr[...] = jnp.zeros_like(c_scr)
        n_scr[...] = jnp.zeros_like(n_scr)
        m_scr[...] = jnp.zeros_like(m_scr)

    nb, tb = q_ref.shape[0], q_ref.shape[1]
    hd = HEAD_DIM
    scale = hd ** -0.5
    tri = (lax.broadcasted_iota(I32, (chunk, chunk), 0) >= lax.broadcasted_iota(I32, (chunk, chunk), 1))
    for c, bb in [(c, bb) for c in range(tb // chunk) for bb in range(nb)]:
        rows = slice(c * chunk, (c + 1) * chunk)
        gates = g_ref[bb, rows, :] + gb_ref[...]
        lf = _log_sigmoid(pltpu.roll(gates, LANES - heads, axis=1))
        cumf = _cum_rows(lf, jnp.add, 0.0)
        a = gates - cumf
        m_prev = m_scr[bb]
        mu = jnp.maximum(_cum_rows(a, jnp.maximum, -jnp.inf), m_prev)
        mu_last = mu[chunk - 1:chunk, :]
        a_t = a.T
        for h in range(heads):
            cols = slice(h * hd, (h + 1) * hd)
            q = q_ref[bb, rows, cols]
            k = k_ref[bb, rows, cols] * scale
            v = v_ref[bb, rows, cols]
            qb, kb, vb = q.astype(BF16), k.astype(BF16), v.astype(BF16)
            mu_col = mu[:, h:h + 1]
            a_col = a[:, h:h + 1]
            m_prev_h = m_prev[:, h:h + 1]
            mu_last_h = mu_last[:, h:h + 1]
            dmat = jnp.exp(jnp.where(tri, a_t[h:h + 1, :] - mu_col, -jnp.inf))
            s = lax.dot_general(qb, kb, (((1,), (1,)), ((), ())), preferred_element_type=F32)
            p = dmat * s
            inter = jnp.exp(m_prev_h - mu_col)
            state = bb * heads + h
            c_h = c_scr[state]
            n_h = n_scr[state]
            num = inter * jnp.dot(qb, c_h.astype(BF16), preferred_element_type=F32) \
                + jnp.dot(p.astype(BF16), vb, preferred_element_type=F32)
            den = inter * jnp.sum(q * n_h, axis=-1, keepdims=True) + jnp.sum(p, axis=-1, keepdims=True)
            floor = jnp.exp(-(cumf[:, h:h + 1] + mu_col))
            h_out = num / jnp.maximum(jnp.abs(den), floor)

            wg = jnp.exp(a_col - mu_last_h)
            decay = jnp.exp(m_prev_h - mu_last_h)
            kw = k * wg
            c_scr[state] = decay * c_h + lax.dot_general(kw.astype(BF16), vb, (((0,), (0,)), ((), ())),
                                                         preferred_element_type=F32)
            n_scr[state] = decay * n_h + jnp.sum(kw, axis=0, keepdims=True)

            gated = _sigmoid(o_ref[bb, rows, cols]) * h_out
            y_ref[bb, rows, cols] = _head_norm(gated, nw_ref[:, cols])
        m_scr[bb] = cumf[chunk - 1:chunk, :] + mu_last


def _mlstm(proj, gates, gate_b, norm_w, *, batch, seq, width, col0):
    t = proj.shape[0]
    heads = width // HEAD_DIM
    nb = SCAN_BATCH if batch % SCAN_BATCH == 0 else 1
    proj3 = proj.reshape(batch, seq, proj.shape[1])
    gates3 = gates.reshape(batch, seq, LANES)
    blk = lambda col: pl.BlockSpec((nb, SCAN_BLOCK, width), lambda b, i: (b, i, col))
    y = pl.pallas_call(
        functools.partial(_mlstm_kernel, chunk=SCAN_CHUNK, heads=heads),
        grid=(batch // nb, seq // SCAN_BLOCK),
        in_specs=[blk(col0), blk(col0 + 1), blk(col0 + 2), blk(col0 + 3),
                  pl.BlockSpec((nb, SCAN_BLOCK, LANES), lambda b, i: (b, i, 0)),
                  pl.BlockSpec((1, LANES), lambda b, i: (0, 0)),
                  pl.BlockSpec((1, width), lambda b, i: (0, 0))],
        out_specs=pl.BlockSpec((nb, SCAN_BLOCK, width), lambda b, i: (b, i, 0)),
        out_shape=jax.ShapeDtypeStruct((batch, seq, width), F32),
        scratch_shapes=[pltpu.VMEM((nb * heads, HEAD_DIM, HEAD_DIM), F32),
                        pltpu.VMEM((nb * heads, 1, HEAD_DIM), F32),
                        pltpu.VMEM((nb, 1, LANES), F32)],
        compiler_params=_cp("parallel", "arbitrary"),
    )(proj3, proj3, proj3, proj3, gates3, gate_b, norm_w)
    return y.reshape(t, width)


def _ret_kernel(q_ref, k_ref, v_ref, g_ref, cos_ref, sin_ref, intra_ref, cross_ref, zeta_ref, nw_ref,
                y_ref, r_scr, *, chunk, heads, chunk_decay):
    @pl.when(pl.program_id(1) == 0)
    def _():
        r_scr[...] = jnp.zeros_like(r_scr)

    tb = q_ref.shape[0]
    hd = HEAD_DIM
    scale = hd ** -0.5
    for c in range(tb // chunk):
        rows = slice(c * chunk, (c + 1) * chunk)
        cos = cos_ref[rows, :]
        sin = sin_ref[rows, :]
        for h in range(heads):
            cols = slice(h * hd, (h + 1) * hd)
            q = q_ref[rows, cols]
            k = k_ref[rows, cols]
            q = q * cos + pltpu.roll(q, hd // 2, axis=1) * sin
            k = (k * cos + pltpu.roll(k, hd // 2, axis=1) * sin) * scale
            vb = v_ref[rows, cols].astype(BF16)
            qb = q.astype(BF16)
            s = lax.dot_general(qb, k.astype(BF16), (((1,), (1,)), ((), ())), preferred_element_type=F32)
            inner = jnp.dot((s * intra_ref[h]).astype(BF16), vb, preferred_element_type=F32)
            r_h = r_scr[h]
            crs = jnp.dot(qb, r_h.astype(BF16), preferred_element_type=F32) * cross_ref[h]
            r_scr[h] = chunk_decay[h] * r_h + lax.dot_general(
                (k * zeta_ref[h]).astype(BF16), vb, (((0,), (0,)), ((), ())), preferred_element_type=F32)
            y_ref[rows, cols] = _silu(g_ref[rows, cols]) * _head_norm(inner + crs, nw_ref[:, cols])


def _retention(proj, norm_w, *, batch, seq, width, col0):
    t = proj.shape[0]
    heads = width // HEAD_DIM
    chunk = SCAN_CHUNK
    n_tblk = seq // SCAN_BLOCK
    half = HEAD_DIM // 2
    inv = np.float32(ROPE_BASE) ** (-np.arange(half, dtype=np.float32) / np.float32(half))
    ang = (np.arange(seq, dtype=np.float32)[:, None] * inv[None, :]).astype(np.float64)
    cos_t = jnp.asarray(np.concatenate([np.cos(ang), np.cos(ang)], axis=-1), F32)
    sin_t = jnp.asarray(np.concatenate([-np.sin(ang), np.sin(ang)], axis=-1), F32)
    log_g = jnp.log(1.0 - 2.0 ** (-5.0 - jnp.arange(heads, dtype=F32)))
    tt = jnp.arange(chunk, dtype=F32)
    lag = tt[:, None] - tt[None, :]
    intra = jnp.where(lag >= 0, jnp.exp(jnp.maximum(lag, 0.0)[None] * log_g[:, None, None]), 0.0)
    cross = jnp.broadcast_to(jnp.exp((tt + 1.0)[None, :] * log_g[:, None])[:, :, None], (heads, chunk, HEAD_DIM))
    zeta = jnp.broadcast_to(jnp.exp((chunk - 1.0 - tt)[None, :] * log_g[:, None])[:, :, None],
                            (heads, chunk, HEAD_DIM))
    chunk_decay = tuple(float((1.0 - 2.0 ** (-5.0 - h)) ** chunk) for h in range(heads))

    blk = lambda col: pl.BlockSpec((SCAN_BLOCK, width), lambda b, i: (b * n_tblk + i, col))
    pos = pl.BlockSpec((SCAN_BLOCK, HEAD_DIM), lambda b, i: (i, 0))
    full3 = lambda a: pl.BlockSpec(a.shape, lambda b, i: (0, 0, 0))
    return pl.pallas_call(
        functools.partial(_ret_kernel, chunk=chunk, heads=heads, chunk_decay=chunk_decay),
        grid=(batch, n_tblk),
        in_specs=[blk(col0), blk(col0 + 1), blk(col0 + 2), blk(col0 + 3), pos, pos,
                  full3(intra), full3(cross), full3(zeta),
                  pl.BlockSpec((1, width), lambda b, i: (0, 0))],
        out_specs=pl.BlockSpec((SCAN_BLOCK, width), lambda b, i: (b * n_tblk + i, 0)),
        out_shape=jax.ShapeDtypeStruct((t, width), F32),
        scratch_shapes=[pltpu.VMEM((heads, HEAD_DIM, HEAD_DIM), F32)],
        compiler_params=_cp("parallel", "arbitrary"),
    )(proj, proj, proj, proj, cos_t, sin_t, intra, cross, zeta, norm_w)


def _merge_kernel(x_ref, *refs):
    y_refs, wg_refs = refs[:N_BRANCH], refs[N_BRANCH:2 * N_BRANCH]
    wb_ref, o_ref, xb_ref = refs[2 * N_BRANCH:]

    @pl.when(pl.program_id(1) == 0)
    def _():
        xb_ref[...] = x_ref[...].astype(BF16)

    xb = xb_ref[...]
    acc = None
    for n in range(N_BRANCH):
        gate = _sigmoid(jnp.dot(xb, wg_refs[n][...], preferred_element_type=F32))
        term = gate * jnp.dot(y_refs[n][...].astype(BF16), wb_ref[n], preferred_element_type=F32)
        acc = term if acc is None else acc + term
    o_ref[...] = acc


def _merge(x, branches, w_gate, gate_col0, w_branch, *, tm, tn):
    t, d = x.shape
    width = branches[0].shape[1]
    ybs = pl.BlockSpec((tm, width), lambda i, j: (i, 0))

    def gate_spec(n):
        return pl.BlockSpec((d, tn), lambda i, j: (0, (gate_col0 + n * d) // tn + j))

    return pl.pallas_call(
        _merge_kernel,
        grid=(t // tm, d // tn),
        in_specs=[pl.BlockSpec((tm, d), lambda i, j: (i, 0))] + [ybs] * N_BRANCH
                 + [gate_spec(n) for n in range(N_BRANCH)]
                 + [pl.BlockSpec((N_BRANCH, width, tn), lambda i, j: (0, 0, j))],
        out_specs=pl.BlockSpec((tm, tn), lambda i, j: (i, j)),
        out_shape=jax.ShapeDtypeStruct((t, d), F32),
        scratch_shapes=[pltpu.VMEM((tm, d), BF16)],
        compiler_params=_cp("parallel", "arbitrary"),
    )(x, *branches, *([w_gate] * N_BRANCH), w_branch)


def _proj_ln_kernel(a_ref, w_ref, r_ref, lw_ref, lb_ref, o_ref, *, alpha):
    y = jnp.dot(a_ref[...].astype(BF16), w_ref[...], preferred_element_type=F32)
    o_ref[...] = _layer_norm(alpha * r_ref[...] + y, lw_ref[...], lb_ref[...])


def _proj_ln(a, w, resid, ln_w, ln_b, *, alpha, tm):
    t, k = a.shape
    d = w.shape[1]
    row = lambda n: pl.BlockSpec((tm, n), lambda i: (i, 0))
    const = lambda shape: pl.BlockSpec(shape, lambda i: (0, 0))
    return pl.pallas_call(
        functools.partial(_proj_ln_kernel, alpha=alpha),
        grid=(t // tm,),
        in_specs=[row(k), const((k, d)), row(d), const((1, d)), const((1, d))],
        out_specs=row(d),
        out_shape=jax.ShapeDtypeStruct((t, d), F32),
        compiler_params=_cp("parallel"),
    )(a, w, resid, ln_w, ln_b)


def _xattn_kernel(x_ref, wq_ref, kv_ref, wo_ref, lw_ref, lb_ref, o_ref, *, alpha, heads):
    x = x_ref[...]
    hd = HEAD_DIM
    inner = heads * hd
    q = jnp.dot(x.astype(BF16), wq_ref[...], preferred_element_type=F32)
    outs = []
    for h in range(heads):
        qh = q[:, h * hd:(h + 1) * hd].astype(BF16)
        kh = kv_ref[:, h * hd:(h + 1) * hd].astype(BF16)
        vh = kv_ref[:, inner + h * hd:inner + (h + 1) * hd].astype(BF16)
        s = lax.dot_general(qh, kh, (((1,), (1,)), ((), ())), preferred_element_type=F32) * hd ** -0.5
        s = s - jnp.max(s, axis=-1, keepdims=True)
        e = jnp.exp(s)
        p = e / jnp.sum(e, axis=-1, keepdims=True)
        outs.append(jnp.dot(p.astype(BF16), vh, preferred_element_type=F32).astype(BF16))
    o = jnp.concatenate(outs, axis=-1)
    y = jnp.dot(o, wo_ref[...], preferred_element_type=F32)
    o_ref[...] = _layer_norm(alpha * x + y, lw_ref[...], lb_ref[...])


def _xattn(x, kv, wq, wo, ln_w, ln_b, *, alpha, seq, mem_len, tm):
    t, d = x.shape
    inner = wq.shape[1]
    n_tblk = seq // tm
    const = lambda shape: pl.BlockSpec(shape, lambda i: (0, 0))
    return pl.pallas_call(
        functools.partial(_xattn_kernel, alpha=alpha, heads=XATTN_HEADS),
        grid=(t // tm,),
        in_specs=[pl.BlockSpec((tm, d), lambda i: (i, 0)), const((d, inner)),
                  pl.BlockSpec((mem_len, 2 * inner), lambda i: (i // n_tblk, 0)),
                  const((inner, d)), const((1, d)), const((1, d))],
        out_specs=pl.BlockSpec((tm, d), lambda i: (i, 0)),
        out_shape=jax.ShapeDtypeStruct((t, d), F32),
        compiler_params=_cp("parallel"),
    )(x, wq, kv, wo, ln_w, ln_b)


def _round_up_pow2(x, m):
    shift = m.bit_length() - 1
    return jnp.left_shift(jnp.right_shift(x + (m - 1), shift), shift)


def _route_kernel(x_ref, wr_ref, rb_ref, w_ref, lrow_ref, segtab_ref, blk_ref, seg_ref, size_all, tot, *, tm, bm):
    step = pl.program_id(0)

    @pl.when(step == 0)
    def _():
        tot[...] = jnp.zeros_like(tot)

    e_n, g_n = N_EXPERTS, N_GROUPS
    per = e_n // g_n
    logits = lax.dot_general(wr_ref[...], x_ref[...], (((1,), (1,)), ((), ())),
                             precision=lax.Precision.HIGHEST, preferred_element_type=F32)
    scores = _sigmoid(logits)
    biased = scores + rb_ref[...]
    b3 = biased.reshape(g_n, per, tm)
    member = lax.broadcasted_iota(I32, (g_n, per, tm), 1)
    top1 = jnp.max(b3, axis=1, keepdims=True)
    first = jnp.min(jnp.where(b3 == top1, member, per), axis=1, keepdims=True)
    top2 = jnp.max(jnp.where(member == first, -jnp.inf, b3), axis=1, keepdims=True)
    gs = top1 + top2
    gid = lax.broadcasted_iota(I32, (g_n, 1, tm), 0)
    rank = jnp.zeros((g_n, 1, tm), I32)
    for other in range(g_n):
        o = gs[other:other + 1]
        ahead = jnp.logical_or(o > gs, jnp.logical_and(o == gs, other < gid))
        rank = rank + jnp.where(ahead, 1, 0)
    cur = jnp.where(rank < TOPK_GROUPS, b3, -jnp.inf).reshape(e_n, tm)

    eid = lax.broadcasted_iota(I32, (e_n, tm), 0)
    picks, vals = [], []
    sel = jnp.zeros((e_n, tm), F32)
    for k in range(TOP_K):
        mx = jnp.max(cur, axis=0, keepdims=True)
        ik = jnp.min(jnp.where(cur == mx, eid, e_n), axis=0, keepdims=True)
        hit = eid == ik
        vals.append(jnp.sum(jnp.where(hit, scores, 0.0), axis=0, keepdims=True))
        cur = jnp.where(hit, -jnp.inf, cur)
        sel = jnp.where(hit, 1.0, sel)
        picks.append(ik)
    total = vals[0]
    for v in vals[1:]:
        total = total + v

    tri = jnp.where(lax.broadcasted_iota(I32, (tm, tm), 0) <= lax.broadcasted_iota(I32, (tm, tm), 1), 1.0, 0.0)
    incl = jnp.dot(sel.astype(BF16), tri.astype(BF16), preferred_element_type=F32)
    size = _round_up_pow2(jnp.broadcast_to(incl[:, tm - 1:tm], (e_n, LANES)).astype(I32), SEG_ALIGN)
    loff = _cum_rows(size, jnp.add, 0) - size
    base = loff[:, 0:1].astype(F32) + incl - 1.0
    for k in range(TOP_K):
        w_ref[k:k + 1, :] = vals[k] / total * ROUTE_SCALE
        lrow_ref[0, k:k + 1, :] = jnp.sum(jnp.where(eid == picks[k], base, 0.0),
                                          axis=0, keepdims=True).astype(I32)
    size_all[step] = size
    tot[...] = tot[...] + size

    @pl.when(step == pl.num_programs(0) - 1)
    def _():
        rows = tot[...]
        pcnt = _round_up_pow2(rows, bm)
        pend = _cum_rows(pcnt, jnp.add, 0)

        def tile_seg(i, run):
            segtab_ref[i, 0] = run
            segtab_ref[i, 1] = size_all[i]
            return run + size_all[i]

        lax.fori_loop(0, pl.num_programs(0), tile_seg, pend - pcnt)

        nb = blk_ref.shape[2]
        row0 = lax.broadcasted_iota(I32, (e_n, nb), 1) * bm
        total_rows = pend[e_n - 1:e_n, 0:1]
        owner = jnp.sum(jnp.where(pend[:, 0:1] <= row0, 1, 0), axis=0, keepdims=True)
        last_owner = jnp.sum(jnp.where(pend[:, 0:1] < total_rows, 1, 0), axis=0, keepdims=True)
        blk_ref[0] = jnp.where(row0[0:1, :] < total_rows, jnp.minimum(owner, e_n - 1), last_owner)
        data_end = pend - pcnt + rows
        inside = jnp.logical_and(pend[:, 0:1] - pcnt[:, 0:1] <= row0, row0 < pend[:, 0:1])
        blk_ref[1] = jnp.sum(jnp.where(inside, jnp.clip(data_end[:, 0:1] - row0, 0, bm), 0), axis=0, keepdims=True)
        seg_ref[0] = pend
        seg_ref[1] = pcnt


def _route(x, router_w_t, router_b, *, tm, bm, n_blocks):
    t, d = x.shape
    e_n = N_EXPERTS
    n_t = t // tm
    assert bm & (bm - 1) == 0
    nb_pad = -(-n_blocks // LANES) * LANES
    return pl.pallas_call(
        functools.partial(_route_kernel, tm=tm, bm=bm),
        grid=(n_t,),
        in_specs=[pl.BlockSpec((tm, d), lambda i: (i, 0)),
                  pl.BlockSpec((e_n, d), lambda i: (0, 0)),
                  pl.BlockSpec((e_n, 1), lambda i: (0, 0))],
        out_specs=[pl.BlockSpec((TOP_K, tm), lambda i: (0, i)),
                   pl.BlockSpec((1, TOP_K, tm), lambda i: (i, 0, 0)),
                   pl.BlockSpec((n_t, 2, e_n, LANES), lambda i: (0, 0, 0, 0)),
                   pl.BlockSpec((2, 1, nb_pad), lambda i: (0, 0, 0)),
                   pl.BlockSpec((2, e_n, LANES), lambda i: (0, 0, 0))],
        out_shape=[jax.ShapeDtypeStruct((TOP_K, t), F32), jax.ShapeDtypeStruct((n_t, TOP_K, tm), I32),
                   jax.ShapeDtypeStruct((n_t, 2, e_n, LANES), I32), jax.ShapeDtypeStruct((2, 1, nb_pad), I32),
                   jax.ShapeDtypeStruct((2, e_n, LANES), I32)],
        scratch_shapes=[pltpu.VMEM((n_t, e_n, LANES), I32), pltpu.VMEM((e_n, LANES), I32)],
        compiler_params=_cp("arbitrary"),
    )(x, router_w_t, router_b)


def _tile_rows(tm):
    worst = TOP_K * tm + N_EXPERTS * (SEG_ALIGN - 1)
    return -(-worst // ONEHOT_ROWS) * ONEHOT_ROWS


def _onehot_rows(chunk, lrow, values, tm):
    rid = chunk * ONEHOT_ROWS + lax.broadcasted_iota(I32, (ONEHOT_ROWS, tm), 0)
    acc = jnp.zeros((ONEHOT_ROWS, tm), F32)
    for k in range(TOP_K):
        acc = jnp.where(rid == lrow[k:k + 1, :], 1.0 if values is None else values[k:k + 1, :], acc)
    return acc.astype(BF16)


def _start_segments(gstart_ref, size_ref, tile, make_copy, keep=None):
    def body(e, carry):
        loff, started = carry
        n = size_ref[tile * N_EXPERTS + e]
        wanted = n > 0 if keep is None else jnp.logical_and(n > 0, keep(loff + n))

        @pl.when(wanted)
        def _():
            make_copy(pl.multiple_of(gstart_ref[tile * N_EXPERTS + e], SEG_ALIGN),
                      pl.multiple_of(loff, SEG_ALIGN), pl.multiple_of(n, SEG_ALIGN)).start()
        return loff + n, started + jnp.where(wanted, n, 0)
    return lax.fori_loop(0, N_EXPERTS, body, (0, 0))


def _tile_total(size_ref, tile):
    return lax.fori_loop(0, N_EXPERTS, lambda e, s: s + size_ref[tile * N_EXPERTS + e], 0)


def _wait_rows(make_copy, rows):
    @pl.when(rows > 0)
    def _():
        make_copy(0, 0, pl.multiple_of(rows, SEG_ALIGN)).wait()


def _dispatch_kernel(gstart_ref, size_ref, x_ref, lrow_ref, xs_ref, stage, inflight, sem_a, sem_b, *, tm):
    i = pl.program_id(0)
    last = pl.num_programs(0) - 1
    n_chunks = stage.shape[0] // ONEHOT_ROWS
    split_chunk = n_chunks // 2
    split = split_chunk * ONEHOT_ROWS
    assert tm <= ONEHOT_ROWS

    def copy_on(sem):
        return lambda g, loff, n: pltpu.make_async_copy(stage.at[pl.ds(loff, n), :], xs_ref.at[pl.ds(g, n), :], sem)

    xb = x_ref[...].astype(BF16)
    lrow = lrow_ref[0]
    n_rows = _tile_total(size_ref, i)

    def chunks(lo, hi):
        for c in range(lo, hi):
            def one(c=c):
                stage[c * ONEHOT_ROWS:(c + 1) * ONEHOT_ROWS, :] = jnp.dot(
                    _onehot_rows(c, lrow, None, tm), xb, preferred_element_type=F32).astype(BF16)
            if c * ONEHOT_ROWS < TOP_K * tm:
                one()
            else:
                pl.when(c * ONEHOT_ROWS < n_rows)(one)

    @pl.when(i == 0)
    def _():
        inflight[0] = 0
        inflight[1] = 0

    _wait_rows(copy_on(sem_a), inflight[0])
    chunks(0, split_chunk - 1)
    _wait_rows(copy_on(sem_b), inflight[1])
    chunks(split_chunk - 1, split_chunk)
    _, rows_a = _start_segments(gstart_ref, size_ref, i, copy_on(sem_a), lambda end: end <= split)
    chunks(split_chunk, n_chunks)
    _, rows_b = _start_segments(gstart_ref, size_ref, i, copy_on(sem_b), lambda end: end > split)
    inflight[0] = rows_a
    inflight[1] = rows_b

    @pl.when(i == last)
    def _():
        _wait_rows(copy_on(sem_a), rows_a)
        _wait_rows(copy_on(sem_b), rows_b)


def _dispatch(x, lrow, gstart, size, *, rows, tm):
    t, d = x.shape
    return pl.pallas_call(
        functools.partial(_dispatch_kernel, tm=tm),
        grid_spec=pltpu.PrefetchScalarGridSpec(
            num_scalar_prefetch=2,
            grid=(t // tm,),
            in_specs=[pl.BlockSpec((tm, d), lambda i, *_: (i, 0)),
                      pl.BlockSpec((1, TOP_K, tm), lambda i, *_: (i, 0, 0))],
            out_specs=pl.BlockSpec(memory_space=pl.ANY),
            scratch_shapes=[pltpu.VMEM((_tile_rows(tm), d), BF16), pltpu.SMEM((2,), I32)]
                           + [pltpu.SemaphoreType.DMA(())] * 2,
        ),
        out_shape=jax.ShapeDtypeStruct((rows, d), BF16),
        compiler_params=_cp("arbitrary"),
    )(gstart, size, x, lrow)


def _expert_block_kernel(blk_ref, nused_ref, xs_ref, wgu_ref, wdn_ref, ys_ref, wgu_b, wdn_b, *, n_blocks, bm):
    j = pl.program_id(0)
    valid = blk_ref[n_blocks + j]
    changed = jnp.logical_or(j == 0, blk_ref[j] != blk_ref[jnp.maximum(j - 1, 0)])

    @pl.when(jnp.logical_and(valid > 0, changed))
    def _():
        wgu_b[...] = wgu_ref[...].astype(BF16)
        wdn_b[...] = wdn_ref[...].astype(BF16)

    def swiglu(rows, masked):
        x = xs_ref[0:rows, :]
        if masked:
            x = jnp.where(lax.broadcasted_iota(I32, x.shape, 0) < valid, x, jnp.zeros_like(x))
        f = wdn_b.shape[0]
        gu = jnp.dot(x, wgu_b[...], preferred_element_type=F32)
        hidden = (_silu(gu[:, :f]) * gu[:, f:]).astype(BF16)
        ys_ref[0:rows, :] = jnp.dot(hidden, wdn_b[...], preferred_element_type=F32).astype(BF16)

    @pl.when(valid == bm)
    def _():
        swiglu(bm, False)

    quarter = bm // 4
    for q in range(1, 5):
        @pl.when(jnp.logical_and(jnp.logical_and(valid > (q - 1) * quarter, valid <= q * quarter), valid < bm))
        def _():
            swiglu(q * quarter, True)


def _experts_blocked(xs, blk, nused, w_gu, w_dn, layer, *, n_blocks, bm):
    rows, d = xs.shape
    f2 = w_gu.shape[3]
    f = w_dn.shape[2]
    row_blk = lambda j, bl, nu: (jnp.minimum(j, nu[0] - 1), 0)
    return pl.pallas_call(
        functools.partial(_expert_block_kernel, n_blocks=n_blocks, bm=bm),
        grid_spec=pltpu.PrefetchScalarGridSpec(
            num_scalar_prefetch=2,
            grid=(n_blocks,),
            in_specs=[pl.BlockSpec((bm, d), row_blk),
                      pl.BlockSpec((None, None, d, f2), lambda j, bl, nu: (layer, bl[j], 0, 0)),
                      pl.BlockSpec((None, None, f, d), lambda j, bl, nu: (layer, bl[j], 0, 0))],
            out_specs=pl.BlockSpec((bm, d), row_blk),
            scratch_shapes=[pltpu.VMEM((d, f2), BF16), pltpu.VMEM((f, d), BF16)],
        ),
        out_shape=jax.ShapeDtypeStruct((rows, d), BF16),
        compiler_params=_cp("arbitrary"),
    )(blk, nused, xs, w_gu, w_dn)


def _combine_kernel(gstart_ref, size_ref, x_ref, lrow_ref, w_ref, sdn_ref, lw_ref, lb_ref, sgu_hbm, ys_ref,
                    o_ref, ybuf, wt, hid, sgu, sems, wsem, *, tm, alpha):
    i, half = pl.program_id(0), pl.program_id(1)
    last_tile = pl.num_programs(0) - 1
    dh = ybuf.shape[2]
    n_chunks = ybuf.shape[1] // ONEHOT_ROWS
    sure_chunks = (TOP_K * tm) // ONEHOT_ROWS

    def contract_rows(w_rows, y_rows):
        return lax.dot_general(w_rows, y_rows, (((0,), (0,)), ((), ())), preferred_element_type=F32)

    def copy_to(hf):
        return lambda g, loff, n: pltpu.make_async_copy(ys_ref.at[pl.ds(g, n), hf * dh:(hf + 1) * dh],
                                                        ybuf.at[hf, pl.ds(loff, n), :], sems.at[hf])

    def fetch(tile, hf):
        _start_segments(gstart_ref, size_ref, tile, copy_to(hf))

    n_rows = _tile_total(size_ref, i)

    @pl.when(jnp.logical_and(i == 0, half == 0))
    def _():
        ybuf[...] = jnp.zeros_like(ybuf)
        weights = pltpu.make_async_copy(sgu_hbm, sgu, wsem)
        weights.start()
        weights.wait()
        fetch(0, 0)

    @pl.when(half == 0)
    def _():
        fetch(i, 1)
        f = hid.shape[1]
        gu = jnp.dot(x_ref[...].astype(BF16), sgu[...], preferred_element_type=F32)
        hidden = (_silu(gu[:, :f]) * gu[:, f:]).astype(BF16)
        hid[...] = hidden
        _wait_rows(copy_to(0), n_rows)
        lrow, w = lrow_ref[0], w_ref[...]
        acc = jnp.dot(hidden, sdn_ref[...], preferred_element_type=F32)
        for c in range(sure_chunks):
            rows = slice(c * ONEHOT_ROWS, (c + 1) * ONEHOT_ROWS)
            w_rows = _onehot_rows(c, lrow, w, tm)
            wt[rows, :] = w_rows
            acc = acc + contract_rows(w_rows, ybuf[0, rows, :])
        o_ref[:, 0:dh] = acc
        for c in range(sure_chunks, n_chunks):
            @pl.when(c * ONEHOT_ROWS < n_rows)
            def _():
                rows = slice(c * ONEHOT_ROWS, (c + 1) * ONEHOT_ROWS)
                w_rows = _onehot_rows(c, lrow, w, tm)
                wt[rows, :] = w_rows
                o_ref[:, 0:dh] += contract_rows(w_rows, ybuf[0, rows, :])

    @pl.when(half == 1)
    def _():
        @pl.when(i < last_tile)
        def _():
            fetch(i + 1, 0)
        _wait_rows(copy_to(1), n_rows)
        sure = slice(0, sure_chunks * ONEHOT_ROWS)
        o_ref[:, dh:2 * dh] = contract_rows(wt[sure, :], ybuf[1, sure, :]) \
            + jnp.dot(hid[...], sdn_ref[...], preferred_element_type=F32)
        for c in range(sure_chunks, n_chunks):
            @pl.when(c * ONEHOT_ROWS < n_rows)
            def _():
                rows = slice(c * ONEHOT_ROWS, (c + 1) * ONEHOT_ROWS)
                o_ref[:, dh:2 * dh] += contract_rows(wt[rows, :], ybuf[1, rows, :])
        o_ref[...] = _layer_norm(alpha * x_ref[...] + o_ref[...], lw_ref[...], lb_ref[...])


def _combine(x, ys, lrow, wts, gstart, size, s_gu, s_dn, ln_w, ln_b, *, alpha, tm):
    t, d = x.shape
    dh = d // 2
    f = s_dn.shape[0]
    const = lambda shape: pl.BlockSpec(shape, lambda i, h, *_: (0, 0))
    return pl.pallas_call(
        functools.partial(_combine_kernel, tm=tm, alpha=alpha),
        grid_spec=pltpu.PrefetchScalarGridSpec(
            num_scalar_prefetch=2,
            grid=(t // tm, 2),
            in_specs=[pl.BlockSpec((tm, d), lambda i, h, *_: (i, 0)),
                      pl.BlockSpec((1, TOP_K, tm), lambda i, h, *_: (i, 0, 0)),
                      pl.BlockSpec((TOP_K, tm), lambda i, h, *_: (0, i)),
                      pl.BlockSpec((f, dh), lambda i, h, *_: (0, h)),
                      const((1, d)), const((1, d)),
                      pl.BlockSpec(memory_space=pl.ANY), pl.BlockSpec(memory_space=pl.ANY)],
            out_specs=pl.BlockSpec((tm, d), lambda i, h, *_: (i, 0)),
            scratch_shapes=[pltpu.VMEM((2, _tile_rows(tm), dh), BF16), pltpu.VMEM((_tile_rows(tm), tm), BF16),
                            pltpu.VMEM((tm, f), BF16), pltpu.VMEM(s_gu.shape, BF16),
                            pltpu.SemaphoreType.DMA((2,)), pltpu.SemaphoreType.DMA(())],
        ),
        out_shape=jax.ShapeDtypeStruct((t, d), F32),
        compiler_params=_cp("arbitrary", "arbitrary"),
    )(gstart, size, x, lrow, wts, s_dn, ln_w, ln_b, s_gu, ys)


def _mixer_sublayer(x, w_in_all, layer, gate_b, pool_w, pool_scale, conv_w, mlstm_norm_w, ret_norm_w, w_branch,
                    w_out, ln_w, ln_b, *, batch, seq, alpha):
    t, d = x.shape
    width = d // N_BRANCH
    heads = width // HEAD_DIM
    gate_off = 8 * width
    ret_off = gate_off
    g_off = ret_off + 4 * width
    if_off = g_off + N_BRANCH * d
    w_bf16 = _realign_cast(w_in_all, layer, lo_col=gate_off, hi_col=if_off, shift=2 * heads, tr=512, tn=512)
    gate_bias = jnp.pad(gate_b, (0, LANES - 2 * heads)).reshape(1, LANES)

    proj_a = _matmul(x, w_bf16, tm=1024, tn=1024, ncols=gate_off)
    proj_b = _matmul(x, w_bf16, tm=1024, tn=1024, ncols=4 * width, col0=ret_off)
    gates = _matmul(x, w_bf16, tm=1024, tn=LANES, ncols=LANES, col0=if_off)
    y_pool, y_conv = _pool_conv(proj_a, pool_w.astype(BF16), pool_scale.reshape(1, width), conv_w,
                                seq=seq, width=width, tb=512)
    y_mlstm = _mlstm(proj_a, gates, gate_bias, mlstm_norm_w.reshape(1, width),
                     batch=batch, seq=seq, width=width, col0=4)
    y_ret = _retention(proj_b, ret_norm_w.reshape(1, width), batch=batch, seq=seq, width=width, col0=0)
    merged = _merge(x, (y_pool, y_conv, y_mlstm, y_ret), w_bf16, g_off, w_branch.astype(BF16),
                    tm=512, tn=512)
    return _proj_ln(merged, w_out.astype(BF16), x, ln_w, ln_b, alpha=alpha, tm=512)


def _xattn_sublayer(x, mem2d, wq, wk, wv, wo, ln_w, ln_b, *, seq, mem_len, alpha):
    w_kv = jnp.concatenate([wk, wv], axis=1).astype(BF16)
    kv = _matmul(mem2d, w_kv, tm=min(mem2d.shape[0], 1024), tn=512)
    return _xattn(x, kv, wq.astype(BF16), wo.astype(BF16), ln_w, ln_b,
                  alpha=alpha, seq=seq, mem_len=mem_len, tm=512)


def _moe_sublayer(x, router_w, router_b, w_gu, w_dn, layer, s_gu, s_dn, ln_w, ln_b, *, alpha):
    t, d = x.shape
    e_n, bm = N_EXPERTS, MOE_BM
    tm = ROUTE_TM
    n_blocks = -(-(t * TOP_K + (t // tm) * e_n * (SEG_ALIGN - 1)) // bm) + e_n
    wts, lrow, segtab, blk, seg = _route(x, router_w.T, router_b.reshape(e_n, 1), tm=tm, bm=bm, n_blocks=n_blocks)
    gstart, size = segtab[:, 0, :, 0].reshape(-1), segtab[:, 1, :, 0].reshape(-1)
    nused = seg[0, e_n - 1, 0] // bm

    xs = _dispatch(x, lrow, gstart, size, rows=n_blocks * bm, tm=tm)
    ys = _experts_blocked(xs, blk[:, 0, :n_blocks].reshape(-1), nused.reshape(1), w_gu, w_dn, layer,
                          n_blocks=n_blocks, bm=bm)
    return _combine(x, ys, lrow, wts, gstart, size, s_gu.astype(BF16), s_dn.astype(BF16), ln_w, ln_b,
                    alpha=alpha, tm=tm)


def kernel(x, mem, w_in, mlstm_gate_b, pool_w, pool_scale, conv_w, mlstm_norm_w, ret_norm_w, w_branch,
           w_mix_out, xa_wq, xa_wk, xa_wv, xa_wo, router_w, router_b, moe_w_gu, moe_w_dn, shared_w_gu,
           shared_w_dn, ln_w, ln_b):
    batch, seq, d = x.shape
    depth = w_in.shape[0]
    mem_len = mem.shape[1]
    alpha = (2 * depth) ** 0.25
    h = x.reshape(batch * seq, d)
    mem2d = mem.reshape(batch * mem_len, d)
    for l in range(depth):
        lw = ln_w[l].reshape(3, 1, d)
        lb = ln_b[l].reshape(3, 1, d)
        h = _mixer_sublayer(h, w_in, l, mlstm_gate_b[l], pool_w[l], pool_scale[l], conv_w[l], mlstm_norm_w[l],
                            ret_norm_w[l], w_branch[l], w_mix_out[l], lw[0], lb[0],
                            batch=batch, seq=seq, alpha=alpha)
        h = _xattn_sublayer(h, mem2d, xa_wq[l], xa_wk[l], xa_wv[l], xa_wo[l], lw[1], lb[1],
                            seq=seq, mem_len=mem_len, alpha=alpha)
        h = _moe_sublayer(h, router_w[l], router_b[l], moe_w_gu, moe_w_dn, l, shared_w_gu[l], shared_w_dn[l],
                          lw[2], lb[2], alpha=alpha)
    return h.reshape(batch, seq, d)
```

```python
import functools

import numpy as np
import jax
import jax.numpy as jnp
from jax import lax
from jax.experimental import pallas as pl
from jax.experimental.pallas import tpu as pltpu

F32 = jnp.float32
BF16 = jnp.bfloat16
I32 = jnp.int32

N_BRANCH = 4
HEAD_DIM = 128
POOL_WINDOWS = (2, 4, 8, 16)
CONV_WIDTH = 3
ROPE_BASE = 10000.0
XATTN_HEADS = 4
N_EXPERTS = 64
TOP_K = 8
N_GROUPS = 8
TOPK_GROUPS = 4
ROUTE_SCALE = 2.5
LN_EPS = 1e-5

LANES = 128
V7X_VMEM_BYTES = 64 * 1024 * 1024
VMEM_LIMIT = 56 * 1024 * 1024

SCAN_CHUNK = 256
SCAN_BLOCK = 512
SCAN_BATCH = 2
HALO = 16
MOE_BM = 1024
ROUTE_TM = 512
SEG_ALIGN = 16
ONEHOT_ROWS = 512

def _cp(*sem):
    return pltpu.CompilerParams(dimension_semantics=sem, vmem_limit_bytes=VMEM_LIMIT)


def _sigmoid(x):
    return 1.0 / (1.0 + jnp.exp(-x))


def _silu(x):
    return x * _sigmoid(x)


def _log_sigmoid(x):
    return jnp.minimum(x, 0.0) - jnp.log(1.0 + jnp.exp(-jnp.abs(x)))


def _layer_norm(z, w, b):
    mu = jnp.mean(z, axis=-1, keepdims=True)
    d = z - mu
    var = jnp.mean(d * d, axis=-1, keepdims=True)
    return d * lax.rsqrt(var + LN_EPS) * w + b


def _head_norm(h, w):
    mu = jnp.mean(h, axis=-1, keepdims=True)
    d = h - mu
    var = jnp.mean(d * d, axis=-1, keepdims=True)
    return d * lax.rsqrt(var + LN_EPS) * w


def _mm_kernel(x_ref, w_ref, o_ref, xb_ref):
    @pl.when(pl.program_id(1) == 0)
    def _():
        xb_ref[...] = x_ref[...].astype(BF16)

    o_ref[...] = jnp.dot(xb_ref[...], w_ref[...], preferred_element_type=F32)


def _realign_cast_kernel(a_ref, b_ref, o_ref, *, shift, lo, hi):
    j = pl.program_id(1)
    tn = o_ref.shape[1]
    shifted = jnp.logical_and(j >= lo, j < hi)

    @pl.when(shifted)
    def _():
        both = jnp.concatenate([a_ref[...], b_ref[...]], axis=1)
        o_ref[...] = both[:, shift:shift + tn].astype(BF16)

    @pl.when(jnp.logical_not(shifted))
    def _():
        o_ref[...] = a_ref[...].astype(BF16)


def _realign_cast(w_all, layer, *, lo_col, hi_col, shift, tr, tn):
    _, rows, _ = w_all.shape
    lo, hi = lo_col // tn, hi_col // tn
    src = lambda j: jnp.where(j == hi, lo, j)
    return pl.pallas_call(
        functools.partial(_realign_cast_kernel, shift=shift, lo=lo, hi=hi),
        grid=(rows // tr, hi + 1),
        in_specs=[pl.BlockSpec((None, tr, tn), lambda i, j: (layer, i, src(j))),
                  pl.BlockSpec((None, tr, LANES), lambda i, j: (layer, i, (src(j) + 1) * (tn // LANES)))],
        out_specs=pl.BlockSpec((tr, tn), lambda i, j: (i, j)),
        out_shape=jax.ShapeDtypeStruct((rows, hi_col + tn), BF16),
        compiler_params=_cp("parallel", "parallel"),
    )(w_all, w_all)


def _matmul(x, w, *, tm, tn, ncols=None, col0=0):
    t, k = x.shape
    n = w.shape[1] if ncols is None else ncols
    return pl.pallas_call(
        _mm_kernel,
        grid=(t // tm, n // tn),
        in_specs=[pl.BlockSpec((tm, k), lambda i, j: (i, 0)),
                  pl.BlockSpec((k, tn), lambda i, j: (0, col0 // tn + j))],
        out_specs=pl.BlockSpec((tm, tn), lambda i, j: (i, j)),
        out_shape=jax.ShapeDtypeStruct((t, n), F32),
        scratch_shapes=[pltpu.VMEM((tm, k), BF16)],
        compiler_params=_cp("parallel", "arbitrary"),
    )(x, w)


def _poolconv_kernel(u_ref, uh_ref, h_ref, hh_ref, b_ref, c_ref, ch_ref, pw_ref, ps_ref, cw_ref,
                     yp_ref, yc_ref, ubuf, zbuf, *, tb, n_tblk):
    first = (pl.program_id(0) % n_tblk) == 0
    ubuf[0:HALO, :] = jnp.where(first, 0.0, uh_ref[...])
    ubuf[HALO:HALO + tb, :] = u_ref[...]
    zbuf[0:HALO, :] = jnp.where(first, 0.0, ch_ref[...] * hh_ref[...])
    zbuf[HALO:HALO + tb, :] = c_ref[...] * h_ref[...]

    t_pos = (pl.program_id(0) % n_tblk) * tb + lax.broadcasted_iota(I32, (tb, LANES), 0)
    gw = u_ref.shape[1] // len(POOL_WINDOWS)
    for grp, win in enumerate(POOL_WINDOWS):
        lanes = slice(grp * gw, (grp + 1) * gw)
        cur = ubuf[HALO:HALO + tb, lanes]
        acc = cur
        for lag in range(1, win):
            acc = acc + ubuf[HALO - lag:HALO - lag + tb, lanes]
        count = jnp.minimum(t_pos + 1, win).astype(F32)
        mixed = acc / count - cur
        y = jnp.dot(mixed.astype(BF16), pw_ref[grp], preferred_element_type=F32)
        yp_ref[:, lanes] = y * ps_ref[:, lanes]

    conv = cw_ref[0:1, :] * zbuf[HALO:HALO + tb, :]
    for lag in range(1, CONV_WIDTH):
        conv = conv + cw_ref[lag:lag + 1, :] * zbuf[HALO - lag:HALO - lag + tb, :]
    yc_ref[...] = b_ref[...] * conv


def _pool_conv(proj, pool_w, pool_scale, conv_w, *, seq, width, tb):
    t = proj.shape[0]
    n_tblk = seq // tb
    ratio = tb // HALO

    def cur(col):
        return pl.BlockSpec((tb, width), lambda g: (g, col))

    def halo(col):
        return pl.BlockSpec((HALO, width), lambda g: (jnp.maximum(g * ratio - 1, 0), col))

    full = lambda shape: pl.BlockSpec(shape, lambda g: (0,) * len(shape))
    return pl.pallas_call(
        functools.partial(_poolconv_kernel, tb=tb, n_tblk=n_tblk),
        grid=(t // tb,),
        in_specs=[cur(0), halo(0), cur(1), halo(1), cur(2), cur(3), halo(3),
                  full(pool_w.shape), full(pool_scale.shape), full(conv_w.shape)],
        out_specs=[pl.BlockSpec((tb, width), lambda g: (g, 0))] * 2,
        out_shape=[jax.ShapeDtypeStruct((t, width), F32)] * 2,
        scratch_shapes=[pltpu.VMEM((HALO + tb, width), F32)] * 2,
        compiler_params=_cp("parallel"),
    )(proj, proj, proj, proj, proj, proj, proj, pool_w, pool_scale, conv_w)


def _cum_rows(x, op, fill):
    n = x.shape[0]
    row = lax.broadcasted_iota(I32, x.shape, 0)
    shift = 1
    while shift < n:
        x = op(x, jnp.where(row >= shift, pltpu.roll(x, shift, axis=0), fill))
        shift *= 2
    return x


def _mlstm_kernel(q_ref, k_ref, v_ref, o_ref, g_ref, gb_ref, nw_ref, y_ref, c_scr, n_scr, m_scr,
                  *, chunk, heads):
    @pl.when(pl.program_id(1) == 0)
    def _():
        c_scr[...] = jnp.zeros_like(c_scr)
        n_scr[...] = jnp.zeros_like(n_scr)
        m_scr[...] = jnp.zeros_like(m_scr)

    nb, tb = q_ref.shape[0], q_ref.shape[1]
    hd = HEAD_DIM
    scale = hd ** -0.5
    tri = (lax.broadcasted_iota(I32, (chunk, chunk), 0) >= lax.broadcasted_iota(I32, (chunk, chunk), 1))
    for c, bb in [(c, bb) for c in range(tb // chunk) for bb in range(nb)]:
        rows = slice(c * chunk, (c + 1) * chunk)
        gates = g_ref[bb, rows, :] + gb_ref[...]
        lf = _log_sigmoid(pltpu.roll(gates, LANES - heads, axis=1))
        cumf = _cum_rows(lf, jnp.add, 0.0)
        a = gates - cumf
        m_prev = m_scr[bb]
        mu = jnp.maximum(_cum_rows(a, jnp.maximum, -jnp.inf), m_prev)
        mu_last = mu[chunk - 1:chunk, :]
        a_t = a.T
        for h in range(heads):
            cols = slice(h * hd, (h + 1) * hd)
            q = q_ref[bb, rows, cols]
            k = k_ref[bb, rows, cols] * scale
            v = v_ref[bb, rows, cols]
            qb, kb, vb = q.astype(BF16), k.astype(BF16), v.astype(BF16)
            mu_col = mu[:, h:h + 1]
            a_col = a[:, h:h + 1]
            m_prev_h = m_prev[:, h:h + 1]
            mu_last_h = mu_last[:, h:h + 1]
            dmat = jnp.exp(jnp.where(tri, a_t[h:h + 1, :] - mu_col, -jnp.inf))
            s = lax.dot_general(qb, kb, (((1,), (1,)), ((), ())), preferred_element_type=F32)
            p = dmat * s
            inter = jnp.exp(m_prev_h - mu_col)
            state = bb * heads + h
            c_h = c_scr[state]
            n_h = n_scr[state]
            num = inter * jnp.dot(qb, c_h.astype(BF16), preferred_element_type=F32) \
                + jnp.dot(p.astype(BF16), vb, preferred_element_type=F32)
            den = inter * jnp.sum(q * n_h, axis=-1, keepdims=True) + jnp.sum(p, axis=-1, keepdims=True)
            floor = jnp.exp(-(cumf[:, h:h + 1] + mu_col))
            h_out = num / jnp.maximum(jnp.abs(den), floor)

            wg = jnp.exp(a_col - mu_last_h)
            decay = jnp.exp(m_prev_h - mu_last_h)
            kw = k * wg
            c_scr[state] = decay * c_h + lax.dot_general(kw.astype(BF16), vb, (((0,), (0,)), ((), ())),
                                                         preferred_element_type=F32)
            n_scr[state] = decay * n_h + jnp.sum(kw, axis=0, keepdims=True)

            gated = _sigmoid(o_ref[bb, rows, cols]) * h_out
            y_ref[bb, rows, cols] = _head_norm(gated, nw_ref[:, cols])
        m_scr[bb] = cumf[chunk - 1:chunk, :] + mu_last


def _mlstm(proj, gates, gate_b, norm_w, *, batch, seq, width, col0):
    t = proj.shape[0]
    heads = width // HEAD_DIM
    nb = SCAN_BATCH if batch % SCAN_BATCH == 0 else 1
    proj3 = proj.reshape(batch, seq, proj.shape[1])
    gates3 = gates.reshape(batch, seq, LANES)
    blk = lambda col: pl.BlockSpec((nb, SCAN_BLOCK, width), lambda b, i: (b, i, col))
    y = pl.pallas_call(
        functools.partial(_mlstm_kernel, chunk=SCAN_CHUNK, heads=heads),
        grid=(batch // nb, seq // SCAN_BLOCK),
        in_specs=[blk(col0), blk(col0 + 1), blk(col0 + 2), blk(col0 + 3),
                  pl.BlockSpec((nb, SCAN_BLOCK, LANES), lambda b, i: (b, i, 0)),
                  pl.BlockSpec((1, LANES), lambda b, i: (0, 0)),
                  pl.BlockSpec((1, width), lambda b, i: (0, 0))],
        out_specs=pl.BlockSpec((nb, SCAN_BLOCK, width), lambda b, i: (b, i, 0)),
        out_shape=jax.ShapeDtypeStruct((batch, seq, width), F32),
        scratch_shapes=[pltpu.VMEM((nb * heads, HEAD_DIM, HEAD_DIM), F32),
                        pltpu.VMEM((nb * heads, 1, HEAD_DIM), F32),
                        pltpu.VMEM((nb, 1, LANES), F32)],
        compiler_params=_cp("parallel", "arbitrary"),
    )(proj3, proj3, proj3, proj3, gates3, gate_b, norm_w)
    return y.reshape(t, width)


def _ret_kernel(q_ref, k_ref, v_ref, g_ref, cos_ref, sin_ref, intra_ref, cross_ref, zeta_ref, nw_ref,
                y_ref, r_scr, *, chunk, heads, chunk_decay):
    @pl.when(pl.program_id(1) == 0)
    def _():
        r_scr[...] = jnp.zeros_like(r_scr)

    tb = q_ref.shape[0]
    hd = HEAD_DIM
    scale = hd ** -0.5
    for c in range(tb // chunk):
        rows = slice(c * chunk, (c + 1) * chunk)
        cos = cos_ref[rows, :]
        sin = sin_ref[rows, :]
        for h in range(heads):
            cols = slice(h * hd, (h + 1) * hd)
            q = q_ref[rows, cols]
            k = k_ref[rows, cols]
            q = q * cos + pltpu.roll(q, hd // 2, axis=1) * sin
            k = (k * cos + pltpu.roll(k, hd // 2, axis=1) * sin) * scale
            vb = v_ref[rows, cols].astype(BF16)
            qb = q.astype(BF16)
            s = lax.dot_general(qb, k.astype(BF16), (((1,), (1,)), ((), ())), preferred_element_type=F32)
            inner = jnp.dot((s * intra_ref[h]).astype(BF16), vb, preferred_element_type=F32)
            r_h = r_scr[h]
            crs = jnp.dot(qb, r_h.astype(BF16), preferred_element_type=F32) * cross_ref[h]
            r_scr[h] = chunk_decay[h] * r_h + lax.dot_general(
                (k * zeta_ref[h]).astype(BF16), vb, (((0,), (0,)), ((), ())), preferred_element_type=F32)
            y_ref[rows, cols] = _silu(g_ref[rows, cols]) * _head_norm(inner + crs, nw_ref[:, cols])


def _retention(proj, norm_w, *, batch, seq, width, col0):
    t = proj.shape[0]
    heads = width // HEAD_DIM
    chunk = SCAN_CHUNK
    n_tblk = seq // SCAN_BLOCK
    half = HEAD_DIM // 2
    inv = np.float32(ROPE_BASE) ** (-np.arange(half, dtype=np.float32) / np.float32(half))
    ang = (np.arange(seq, dtype=np.float32)[:, None] * inv[None, :]).astype(np.float64)
    cos_t = jnp.asarray(np.concatenate([np.cos(ang), np.cos(ang)], axis=-1), F32)
    sin_t = jnp.asarray(np.concatenate([-np.sin(ang), np.sin(ang)], axis=-1), F32)
    log_g = jnp.log(1.0 - 2.0 ** (-5.0 - jnp.arange(heads, dtype=F32)))
    tt = jnp.arange(chunk, dtype=F32)
    lag = tt[:, None] - tt[None, :]
    intra = jnp.where(lag >= 0, jnp.exp(jnp.maximum(lag, 0.0)[None] * log_g[:, None, None]), 0.0)
    cross = jnp.broadcast_to(jnp.exp((tt + 1.0)[None, :] * log_g[:, None])[:, :, None], (heads, chunk, HEAD_DIM))
    zeta = jnp.broadcast_to(jnp.exp((chunk - 1.0 - tt)[None, :] * log_g[:, None])[:, :, None],
                            (heads, chunk, HEAD_DIM))
    chunk_decay = tuple(float((1.0 - 2.0 ** (-5.0 - h)) ** chunk) for h in range(heads))

    blk = lambda col: pl.BlockSpec((SCAN_BLOCK, width), lambda b, i: (b * n_tblk + i, col))
    pos = pl.BlockSpec((SCAN_BLOCK, HEAD_DIM), lambda b, i: (i, 0))
    full3 = lambda a: pl.BlockSpec(a.shape, lambda b, i: (0, 0, 0))
    return pl.pallas_call(
        functools.partial(_ret_kernel, chunk=chunk, heads=heads, chunk_decay=chunk_decay),
        grid=(batch, n_tblk),
        in_specs=[blk(col0), blk(col0 + 1), blk(col0 + 2), blk(col0 + 3), pos, pos,
                  full3(intra), full3(cross), full3(zeta),
                  pl.BlockSpec((1, width), lambda b, i: (0, 0))],
        out_specs=pl.BlockSpec((SCAN_BLOCK, width), lambda b, i: (b * n_tblk + i, 0)),
        out_shape=jax.ShapeDtypeStruct((t, width), F32),
        scratch_shapes=[pltpu.VMEM((heads, HEAD_DIM, HEAD_DIM), F32)],
        compiler_params=_cp("parallel", "arbitrary"),
    )(proj, proj, proj, proj, cos_t, sin_t, intra, cross, zeta, norm_w)


def _merge_kernel(x_ref, *refs):
    y_refs, wg_refs = refs[:N_BRANCH], refs[N_BRANCH:2 * N_BRANCH]
    wb_ref, o_ref, xb_ref = refs[2 * N_BRANCH:]

    @pl.when(pl.program_id(1) == 0)
    def _():
        xb_ref[...] = x_ref[...].astype(BF16)

    xb = xb_ref[...]
    acc = None
    for n in range(N_BRANCH):
        gate = _sigmoid(jnp.dot(xb, wg_refs[n][...], preferred_element_type=F32))
        term = gate * jnp.dot(y_refs[n][...].astype(BF16), wb_ref[n], preferred_element_type=F32)
        acc = term if acc is None else acc + term
    o_ref[...] = acc


def _merge(x, branches, w_gate, gate_col0, w_branch, *, tm, tn):
    t, d = x.shape
    width = branches[0].shape[1]
    ybs = pl.BlockSpec((tm, width), lambda i, j: (i, 0))

    def gate_spec(n):
        return pl.BlockSpec((d, tn), lambda i, j: (0, (gate_col0 + n * d) // tn + j))

    return pl.pallas_call(
        _merge_kernel,
        grid=(t // tm, d // tn),
        in_specs=[pl.BlockSpec((tm, d), lambda i, j: (i, 0))] + [ybs] * N_BRANCH
                 + [gate_spec(n) for n in range(N_BRANCH)]
                 + [pl.BlockSpec((N_BRANCH, width, tn), lambda i, j: (0, 0, j))],
        out_specs=pl.BlockSpec((tm, tn), lambda i, j: (i, j)),
        out_shape=jax.ShapeDtypeStruct((t, d), F32),
        scratch_shapes=[pltpu.VMEM((tm, d), BF16)],
        compiler_params=_cp("parallel", "arbitrary"),
    )(x, *branches, *([w_gate] * N_BRANCH), w_branch)


def _proj_ln_kernel(a_ref, w_ref, r_ref, lw_ref, lb_ref, o_ref, *, alpha):
    y = jnp.dot(a_ref[...].astype(BF16), w_ref[...], preferred_element_type=F32)
    o_ref[...] = _layer_norm(alpha * r_ref[...] + y, lw_ref[...], lb_ref[...])


def _proj_ln(a, w, resid, ln_w, ln_b, *, alpha, tm):
    t, k = a.shape
    d = w.shape[1]
    row = lambda n: pl.BlockSpec((tm, n), lambda i: (i, 0))
    const = lambda shape: pl.BlockSpec(shape, lambda i: (0, 0))
    return pl.pallas_call(
        functools.partial(_proj_ln_kernel, alpha=alpha),
        grid=(t // tm,),
        in_specs=[row(k), const((k, d)), row(d), const((1, d)), const((1, d))],
        out_specs=row(d),
        out_shape=jax.ShapeDtypeStruct((t, d), F32),
        compiler_params=_cp("parallel"),
    )(a, w, resid, ln_w, ln_b)


def _xattn_kernel(x_ref, wq_ref, kv_ref, wo_ref, lw_ref, lb_ref, o_ref, *, alpha, heads):
    x = x_ref[...]
    hd = HEAD_DIM
    inner = heads * hd
    q = jnp.dot(x.astype(BF16), wq_ref[...], preferred_element_type=F32)
    outs = []
    for h in range(heads):
        qh = q[:, h * hd:(h + 1) * hd].astype(BF16)
        kh = kv_ref[:, h * hd:(h + 1) * hd].astype(BF16)
        vh = kv_ref[:, inner + h * hd:inner + (h + 1) * hd].astype(BF16)
        s = lax.dot_general(qh, kh, (((1,), (1,)), ((), ())), preferred_element_type=F32) * hd ** -0.5
        s = s - jnp.max(s, axis=-1, keepdims=True)
        e = jnp.exp(s)
        p = e / jnp.sum(e, axis=-1, keepdims=True)
        outs.append(jnp.dot(p.astype(BF16), vh, preferred_element_type=F32).astype(BF16))
    o = jnp.concatenate(outs, axis=-1)
    y = jnp.dot(o, wo_ref[...], preferred_element_type=F32)
    o_ref[...] = _layer_norm(alpha * x + y, lw_ref[...], lb_ref[...])


def _xattn(x, kv, wq, wo, ln_w, ln_b, *, alpha, seq, mem_len, tm):
    t, d = x.shape
    inner = wq.shape[1]
    n_tblk = seq // tm
    const = lambda shape: pl.BlockSpec(shape, lambda i: (0, 0))
    return pl.pallas_call(
        functools.partial(_xattn_kernel, alpha=alpha, heads=XATTN_HEADS),
        grid=(t // tm,),
        in_specs=[pl.BlockSpec((tm, d), lambda i: (i, 0)), const((d, inner)),
                  pl.BlockSpec((mem_len, 2 * inner), lambda i: (i // n_tblk, 0)),
                  const((inner, d)), const((1, d)), const((1, d))],
        out_specs=pl.BlockSpec((tm, d), lambda i: (i, 0)),
        out_shape=jax.ShapeDtypeStruct((t, d), F32),
        compiler_params=_cp("parallel"),
    )(x, wq, kv, wo, ln_w, ln_b)


def _round_up_pow2(x, m):
    shift = m.bit_length() - 1
    return jnp.left_shift(jnp.right_shift(x + (m - 1), shift), shift)


def _route_kernel(x_ref, wr_ref, rb_ref, w_ref, lrow_ref, segtab_ref, blk_ref, seg_ref, size_all, tot, *, tm, bm):
    step = pl.program_id(0)

    @pl.when(step == 0)
    def _():
        tot[...] = jnp.zeros_like(tot)

    e_n, g_n = N_EXPERTS, N_GROUPS
    per = e_n // g_n
    logits = lax.dot_general(wr_ref[...], x_ref[...], (((1,), (1,)), ((), ())),
                             precision=lax.Precision.HIGHEST, preferred_element_type=F32)
    scores = _sigmoid(logits)
    biased = scores + rb_ref[...]
    b3 = biased.reshape(g_n, per, tm)
    member = lax.broadcasted_iota(I32, (g_n, per, tm), 1)
    top1 = jnp.max(b3, axis=1, keepdims=True)
    first = jnp.min(jnp.where(b3 == top1, member, per), axis=1, keepdims=True)
    top2 = jnp.max(jnp.where(member == first, -jnp.inf, b3), axis=1, keepdims=True)
    gs = top1 + top2
    gid = lax.broadcasted_iota(I32, (g_n, 1, tm), 0)
    rank = jnp.zeros((g_n, 1, tm), I32)
    for other in range(g_n):
        o = gs[other:other + 1]
        ahead = jnp.logical_or(o > gs, jnp.logical_and(o == gs, other < gid))
        rank = rank + jnp.where(ahead, 1, 0)
    cur = jnp.where(rank < TOPK_GROUPS, b3, -jnp.inf).reshape(e_n, tm)

    eid = lax.broadcasted_iota(I32, (e_n, tm), 0)
    picks, vals = [], []
    sel = jnp.zeros((e_n, tm), F32)
    for k in range(TOP_K):
        mx = jnp.max(cur, axis=0, keepdims=True)
        ik = jnp.min(jnp.where(cur == mx, eid, e_n), axis=0, keepdims=True)
        hit = eid == ik
        vals.append(jnp.sum(jnp.where(hit, scores, 0.0), axis=0, keepdims=True))
        cur = jnp.where(hit, -jnp.inf, cur)
        sel = jnp.where(hit, 1.0, sel)
        picks.append(ik)
    total = vals[0]
    for v in vals[1:]:
        total = total + v

    tri = jnp.where(lax.broadcasted_iota(I32, (tm, tm), 0) <= lax.broadcasted_iota(I32, (tm, tm), 1), 1.0, 0.0)
    incl = jnp.dot(sel.astype(BF16), tri.astype(BF16), preferred_element_type=F32)
    size = _round_up_pow2(jnp.broadcast_to(incl[:, tm - 1:tm], (e_n, LANES)).astype(I32), SEG_ALIGN)
    loff = _cum_rows(size, jnp.add, 0) - size
    base = loff[:, 0:1].astype(F32) + incl - 1.0
    for k in range(TOP_K):
        w_ref[k:k + 1, :] = vals[k] / total * ROUTE_SCALE
        lrow_ref[0, k:k + 1, :] = jnp.sum(jnp.where(eid == picks[k], base, 0.0),
                                          axis=0, keepdims=True).astype(I32)
    size_all[step] = size
    tot[...] = tot[...] + size

    @pl.when(step == pl.num_programs(0) - 1)
    def _():
        rows = tot[...]
        pcnt = _round_up_pow2(rows, bm)
        pend = _cum_rows(pcnt, jnp.add, 0)

        def tile_seg(i, run):
            segtab_ref[i, 0] = run
            segtab_ref[i, 1] = size_all[i]
            return run + size_all[i]

        lax.fori_loop(0, pl.num_programs(0), tile_seg, pend - pcnt)

        nb = blk_ref.shape[2]
        row0 = lax.broadcasted_iota(I32, (e_n, nb), 1) * bm
        total_rows = pend[e_n - 1:e_n, 0:1]
        owner = jnp.sum(jnp.where(pend[:, 0:1] <= row0, 1, 0), axis=0, keepdims=True)
        last_owner = jnp.sum(jnp.where(pend[:, 0:1] < total_rows, 1, 0), axis=0, keepdims=True)
        blk_ref[0] = jnp.where(row0[0:1, :] < total_rows, jnp.minimum(owner, e_n - 1), last_owner)
        data_end = pend - pcnt + rows
        inside = jnp.logical_and(pend[:, 0:1] - pcnt[:, 0:1] <= row0, row0 < pend[:, 0:1])
        blk_ref[1] = jnp.sum(jnp.where(inside, jnp.clip(data_end[:, 0:1] - row0, 0, bm), 0), axis=0, keepdims=True)
        seg_ref[0] = pend
        seg_ref[1] = pcnt


def _route(x, router_w_t, router_b, *, tm, bm, n_blocks):
    t, d = x.shape
    e_n = N_EXPERTS
    n_t = t // tm
    assert bm & (bm - 1) == 0
    nb_pad = -(-n_blocks // LANES) * LANES
    return pl.pallas_call(
        functools.partial(_route_kernel, tm=tm, bm=bm),
        grid=(n_t,),
        in_specs=[pl.BlockSpec((tm, d), lambda i: (i, 0)),
                  pl.BlockSpec((e_n, d), lambda i: (0, 0)),
                  pl.BlockSpec((e_n, 1), lambda i: (0, 0))],
        out_specs=[pl.BlockSpec((TOP_K, tm), lambda i: (0, i)),
                   pl.BlockSpec((1, TOP_K, tm), lambda i: (i, 0, 0)),
                   pl.BlockSpec((n_t, 2, e_n, LANES), lambda i: (0, 0, 0, 0)),
                   pl.BlockSpec((2, 1, nb_pad), lambda i: (0, 0, 0)),
                   pl.BlockSpec((2, e_n, LANES), lambda i: (0, 0, 0))],
        out_shape=[jax.ShapeDtypeStruct((TOP_K, t), F32), jax.ShapeDtypeStruct((n_t, TOP_K, tm), I32),
                   jax.ShapeDtypeStruct((n_t, 2, e_n, LANES), I32), jax.ShapeDtypeStruct((2, 1, nb_pad), I32),
                   jax.ShapeDtypeStruct((2, e_n, LANES), I32)],
        scratch_shapes=[pltpu.VMEM((n_t, e_n, LANES), I32), pltpu.VMEM((e_n, LANES), I32)],
        compiler_params=_cp("arbitrary"),
    )(x, router_w_t, router_b)


def _tile_rows(tm):
    worst = TOP_K * tm + N_EXPERTS * (SEG_ALIGN - 1)
    return -(-worst // ONEHOT_ROWS) * ONEHOT_ROWS


def _onehot_rows(chunk, lrow, values, tm):
    rid = chunk * ONEHOT_ROWS + lax.broadcasted_iota(I32, (ONEHOT_ROWS, tm), 0)
    acc = jnp.zeros((ONEHOT_ROWS, tm), F32)
    for k in range(TOP_K):
        acc = jnp.where(rid == lrow[k:k + 1, :], 1.0 if values is None else values[k:k + 1, :], acc)
    return acc.astype(BF16)


def _start_segments(gstart_ref, size_ref, tile, make_copy, keep=None):
    def body(e, carry):
        loff, started = carry
        n = size_ref[tile * N_EXPERTS + e]
        wanted = n > 0 if keep is None else jnp.logical_and(n > 0, keep(loff + n))

        @pl.when(wanted)
        def _():
            make_copy(pl.multiple_of(gstart_ref[tile * N_EXPERTS + e], SEG_ALIGN),
                      pl.multiple_of(loff, SEG_ALIGN), pl.multiple_of(n, SEG_ALIGN)).start()
        return loff + n, started + jnp.where(wanted, n, 0)
    return lax.fori_loop(0, N_EXPERTS, body, (0, 0))


def _tile_total(size_ref, tile):
    return lax.fori_loop(0, N_EXPERTS, lambda e, s: s + size_ref[tile * N_EXPERTS + e], 0)


def _wait_rows(make_copy, rows):
    @pl.when(rows > 0)
    def _():
        make_copy(0, 0, pl.multiple_of(rows, SEG_ALIGN)).wait()


def _dispatch_kernel(gstart_ref, size_ref, x_ref, lrow_ref, xs_ref, stage, inflight, sem_a, sem_b, *, tm):
    i = pl.program_id(0)
    last = pl.num_programs(0) - 1
    n_chunks = stage.shape[0] // ONEHOT_ROWS
    split_chunk = n_chunks // 2
    split = split_chunk * ONEHOT_ROWS
    assert tm <= ONEHOT_ROWS

    def copy_on(sem):
        return lambda g, loff, n: pltpu.make_async_copy(stage.at[pl.ds(loff, n), :], xs_ref.at[pl.ds(g, n), :], sem)

    xb = x_ref[...].astype(BF16)
    lrow = lrow_ref[0]
    n_rows = _tile_total(size_ref, i)

    def chunks(lo, hi):
        for c in range(lo, hi):
            def one(c=c):
                stage[c * ONEHOT_ROWS:(c + 1) * ONEHOT_ROWS, :] = jnp.dot(
                    _onehot_rows(c, lrow, None, tm), xb, preferred_element_type=F32).astype(BF16)
            if c * ONEHOT_ROWS < TOP_K * tm:
                one()
            else:
                pl.when(c * ONEHOT_ROWS < n_rows)(one)

    @pl.when(i == 0)
    def _():
        inflight[0] = 0
        inflight[1] = 0

    _wait_rows(copy_on(sem_a), inflight[0])
    chunks(0, split_chunk - 1)
    _wait_rows(copy_on(sem_b), inflight[1])
    chunks(split_chunk - 1, split_chunk)
    _, rows_a = _start_segments(gstart_ref, size_ref, i, copy_on(sem_a), lambda end: end <= split)
    chunks(split_chunk, n_chunks)
    _, rows_b = _start_segments(gstart_ref, size_ref, i, copy_on(sem_b), lambda end: end > split)
    inflight[0] = rows_a
    inflight[1] = rows_b

    @pl.when(i == last)
    def _():
        _wait_rows(copy_on(sem_a), rows_a)
        _wait_rows(copy_on(sem_b), rows_b)


def _dispatch(x, lrow, gstart, size, *, rows, tm):
    t, d = x.shape
    return pl.pallas_call(
        functools.partial(_dispatch_kernel, tm=tm),
        grid_spec=pltpu.PrefetchScalarGridSpec(
            num_scalar_prefetch=2,
            grid=(t // tm,),
            in_specs=[pl.BlockSpec((tm, d), lambda i, *_: (i, 0)),
                      pl.BlockSpec((1, TOP_K, tm), lambda i, *_: (i, 0, 0))],
            out_specs=pl.BlockSpec(memory_space=pl.ANY),
            scratch_shapes=[pltpu.VMEM((_tile_rows(tm), d), BF16), pltpu.SMEM((2,), I32)]
                           + [pltpu.SemaphoreType.DMA(())] * 2,
        ),
        out_shape=jax.ShapeDtypeStruct((rows, d), BF16),
        compiler_params=_cp("arbitrary"),
    )(gstart, size, x, lrow)


def _expert_block_kernel(blk_ref, nused_ref, xs_ref, wgu_ref, wdn_ref, ys_ref, wgu_b, wdn_b, *, n_blocks, bm):
    j = pl.program_id(0)
    valid = blk_ref[n_blocks + j]
    changed = jnp.logical_or(j == 0, blk_ref[j] != blk_ref[jnp.maximum(j - 1, 0)])

    @pl.when(jnp.logical_and(valid > 0, changed))
    def _():
        wgu_b[...] = wgu_ref[...].astype(BF16)
        wdn_b[...] = wdn_ref[...].astype(BF16)

    def swiglu(rows, masked):
        x = xs_ref[0:rows, :]
        if masked:
            x = jnp.where(lax.broadcasted_iota(I32, x.shape, 0) < valid, x, jnp.zeros_like(x))
        f = wdn_b.shape[0]
        gu = jnp.dot(x, wgu_b[...], preferred_element_type=F32)
        hidden = (_silu(gu[:, :f]) * gu[:, f:]).astype(BF16)
        ys_ref[0:rows, :] = jnp.dot(hidden, wdn_b[...], preferred_element_type=F32).astype(BF16)

    @pl.when(valid == bm)
    def _():
        swiglu(bm, False)

    quarter = bm // 4
    for q in range(1, 5):
        @pl.when(jnp.logical_and(jnp.logical_and(valid > (q - 1) * quarter, valid <= q * quarter), valid < bm))
        def _():
            swiglu(q * quarter, True)


def _experts_blocked(xs, blk, nused, w_gu, w_dn, layer, *, n_blocks, bm):
    rows, d = xs.shape
    f2 = w_gu.shape[3]
    f = w_dn.shape[2]
    row_blk = lambda j, bl, nu: (jnp.minimum(j, nu[0] - 1), 0)
    return pl.pallas_call(
        functools.partial(_expert_block_kernel, n_blocks=n_blocks, bm=bm),
        grid_spec=pltpu.PrefetchScalarGridSpec(
            num_scalar_prefetch=2,
            grid=(n_blocks,),
            in_specs=[pl.BlockSpec((bm, d), row_blk),
                      pl.BlockSpec((None, None, d, f2), lambda j, bl, nu: (layer, bl[j], 0, 0)),
                      pl.BlockSpec((None, None, f, d), lambda j, bl, nu: (layer, bl[j], 0, 0))],
            out_specs=pl.BlockSpec((bm, d), row_blk),
            scratch_shapes=[pltpu.VMEM((d, f2), BF16), pltpu.VMEM((f, d), BF16)],
        ),
        out_shape=jax.ShapeDtypeStruct((rows, d), BF16),
        compiler_params=_cp("arbitrary"),
    )(blk, nused, xs, w_gu, w_dn)


def _combine_kernel(gstart_ref, size_ref, x_ref, lrow_ref, w_ref, sdn_ref, lw_ref, lb_ref, sgu_hbm, ys_ref,
                    o_ref, ybuf, wt, hid, sgu, sems, wsem, *, tm, alpha):
    i, half = pl.program_id(0), pl.program_id(1)
    last_tile = pl.num_programs(0) - 1
    dh = ybuf.shape[2]
    n_chunks = ybuf.shape[1] // ONEHOT_ROWS
    sure_chunks = (TOP_K * tm) // ONEHOT_ROWS

    def contract_rows(w_rows, y_rows):
        return lax.dot_general(w_rows, y_rows, (((0,), (0,)), ((), ())), preferred_element_type=F32)

    def copy_to(hf):
        return lambda g, loff, n: pltpu.make_async_copy(ys_ref.at[pl.ds(g, n), hf * dh:(hf + 1) * dh],
                                                        ybuf.at[hf, pl.ds(loff, n), :], sems.at[hf])

    def fetch(tile, hf):
        _start_segments(gstart_ref, size_ref, tile, copy_to(hf))

    n_rows = _tile_total(size_ref, i)

    @pl.when(jnp.logical_and(i == 0, half == 0))
    def _():
        ybuf[...] = jnp.zeros_like(ybuf)
        weights = pltpu.make_async_copy(sgu_hbm, sgu, wsem)
        weights.start()
        weights.wait()
        fetch(0, 0)

    @pl.when(half == 0)
    def _():
        fetch(i, 1)
        f = hid.shape[1]
        gu = jnp.dot(x_ref[...].astype(BF16), sgu[...], preferred_element_type=F32)
        hidden = (_silu(gu[:, :f]) * gu[:, f:]).astype(BF16)
        hid[...] = hidden
        _wait_rows(copy_to(0), n_rows)
        lrow, w = lrow_ref[0], w_ref[...]
        acc = jnp.dot(hidden, sdn_ref[...], preferred_element_type=F32)
        for c in range(sure_chunks):
            rows = slice(c * ONEHOT_ROWS, (c + 1) * ONEHOT_ROWS)
            w_rows = _onehot_rows(c, lrow, w, tm)
            wt[rows, :] = w_rows
            acc = acc + contract_rows(w_rows, ybuf[0, rows, :])
        o_ref[:, 0:dh] = acc
        for c in range(sure_chunks, n_chunks):
            @pl.when(c * ONEHOT_ROWS < n_rows)
            def _():
                rows = slice(c * ONEHOT_ROWS, (c + 1) * ONEHOT_ROWS)
                w_rows = _onehot_rows(c, lrow, w, tm)
                wt[rows, :] = w_rows
                o_ref[:, 0:dh] += contract_rows(w_rows, ybuf[0, rows, :])

    @pl.when(half == 1)
    def _():
        @pl.when(i < last_tile)
        def _():
            fetch(i + 1, 0)
        _wait_rows(copy_to(1), n_rows)
        sure = slice(0, sure_chunks * ONEHOT_ROWS)
        o_ref[:, dh:2 * dh] = contract_rows(wt[sure, :], ybuf[1, sure, :]) \
            + jnp.dot(hid[...], sdn_ref[...], preferred_element_type=F32)
        for c in range(sure_chunks, n_chunks):
            @pl.when(c * ONEHOT_ROWS < n_rows)
            def _():
                rows = slice(c * ONEHOT_ROWS, (c + 1) * ONEHOT_ROWS)
                o_ref[:, dh:2 * dh] += contract_rows(wt[rows, :], ybuf[1, rows, :])
        o_ref[...] = _layer_norm(alpha * x_ref[...] + o_ref[...], lw_ref[...], lb_ref[...])


def _combine(x, ys, lrow, wts, gstart, size, s_gu, s_dn, ln_w, ln_b, *, alpha, tm):
    t, d = x.shape
    dh = d // 2
    f = s_dn.shape[0]
    const = lambda shape: pl.BlockSpec(shape, lambda i, h, *_: (0, 0))
    return pl.pallas_call(
        functools.partial(_combine_kernel, tm=tm, alpha=alpha),
        grid_spec=pltpu.PrefetchScalarGridSpec(
            num_scalar_prefetch=2,
            grid=(t // tm, 2),
            in_specs=[pl.BlockSpec((tm, d), lambda i, h, *_: (i, 0)),
                      pl.BlockSpec((1, TOP_K, tm), lambda i, h, *_: (i, 0, 0)),
                      pl.BlockSpec((TOP_K, tm), lambda i, h, *_: (0, i)),
                      pl.BlockSpec((f, dh), lambda i, h, *_: (0, h)),
                      const((1, d)), const((1, d)),
                      pl.BlockSpec(memory_space=pl.ANY), pl.BlockSpec(memory_space=pl.ANY)],
            out_specs=pl.BlockSpec((tm, d), lambda i, h, *_: (i, 0)),
            scratch_shapes=[pltpu.VMEM((2, _tile_rows(tm), dh), BF16), pltpu.VMEM((_tile_rows(tm), tm), BF16),
                            pltpu.VMEM((tm, f), BF16), pltpu.VMEM(s_gu.shape, BF16),
                            pltpu.SemaphoreType.DMA((2,)), pltpu.SemaphoreType.DMA(())],
        ),
        out_shape=jax.ShapeDtypeStruct((t, d), F32),
        compiler_params=_cp("arbitrary", "arbitrary"),
    )(gstart, size, x, lrow, wts, s_dn, ln_w, ln_b, s_gu, ys)


def _mixer_sublayer(x, w_in_all, layer, gate_b, pool_w, pool_scale, conv_w, mlstm_norm_w, ret_norm_w, w_branch,
                    w_out, ln_w, ln_b, *, batch, seq, alpha):
    t, d = x.shape
    width = d // N_BRANCH
    heads = width // HEAD_DIM
    gate_off = 8 * width
    ret_off = gate_off
    g_off = ret_off + 4 * width
    if_off = g_off + N_BRANCH * d
    w_bf16 = _realign_cast(w_in_all, layer, lo_col=gate_off, hi_col=if_off, shift=2 * heads, tr=512, tn=512)
    gate_bias = jnp.pad(gate_b, (0, LANES - 2 * heads)).reshape(1, LANES)

    proj_a = _matmul(x, w_bf16, tm=1024, tn=1024, ncols=gate_off)
    proj_b = _matmul(x, w_bf16, tm=1024, tn=1024, ncols=4 * width, col0=ret_off)
    gates = _matmul(x, w_bf16, tm=1024, tn=LANES, ncols=LANES, col0=if_off)
    y_pool, y_conv = _pool_conv(proj_a, pool_w.astype(BF16), pool_scale.reshape(1, width), conv_w,
                                seq=seq, width=width, tb=512)
    y_mlstm = _mlstm(proj_a, gates, gate_bias, mlstm_norm_w.reshape(1, width),
                     batch=batch, seq=seq, width=width, col0=4)
    y_ret = _retention(proj_b, ret_norm_w.reshape(1, width), batch=batch, seq=seq, width=width, col0=0)
    merged = _merge(x, (y_pool, y_conv, y_mlstm, y_ret), w_bf16, g_off, w_branch.astype(BF16),
                    tm=512, tn=512)
    return _proj_ln(merged, w_out.astype(BF16), x, ln_w, ln_b, alpha=alpha, tm=512)


def _xattn_sublayer(x, mem2d, wq, wk, wv, wo, ln_w, ln_b, *, seq, mem_len, alpha):
    w_kv = jnp.concatenate([wk, wv], axis=1).astype(BF16)
    kv = _matmul(mem2d, w_kv, tm=min(mem2d.shape[0], 1024), tn=512)
    return _xattn(x, kv, wq.astype(BF16), wo.astype(BF16), ln_w, ln_b,
                  alpha=alpha, seq=seq, mem_len=mem_len, tm=512)


def _moe_sublayer(x, router_w, router_b, w_gu, w_dn, layer, s_gu, s_dn, ln_w, ln_b, *, alpha):
    t, d = x.shape
    e_n, bm = N_EXPERTS, MOE_BM
    tm = ROUTE_TM
    n_blocks = -(-(t * TOP_K + (t // tm) * e_n * (SEG_ALIGN - 1)) // bm) + e_n
    wts, lrow, segtab, blk, seg = _route(x, router_w.T, router_b.reshape(e_n, 1), tm=tm, bm=bm, n_blocks=n_blocks)
    gstart, size = segtab[:, 0, :, 0].reshape(-1), segtab[:, 1, :, 0].reshape(-1)
    nused = seg[0, e_n - 1, 0] // bm

    xs = _dispatch(x, lrow, gstart, size, rows=n_blocks * bm, tm=tm)
    ys = _experts_blocked(xs, blk[:, 0, :n_blocks].reshape(-1), nused.reshape(1), w_gu, w_dn, layer,
                          n_blocks=n_blocks, bm=bm)
    return _combine(x, ys, lrow, wts, gstart, size, s_gu.astype(BF16), s_dn.astype(BF16), ln_w, ln_b,
                    alpha=alpha, tm=tm)


def kernel(x, mem, w_in, mlstm_gate_b, pool_w, pool_scale, conv_w, mlstm_norm_w, ret_norm_w, w_branch,
           w_mix_out, xa_wq, xa_wk, xa_wv, xa_wo, router_w, router_b, moe_w_gu, moe_w_dn, shared_w_gu,
           shared_w_dn, ln_w, ln_b):
    batch, seq, d = x.shape
    depth = w_in.shape[0]
    mem_len = mem.shape[1]
    alpha = (2 * depth) ** 0.25
    h = x.reshape(batch * seq, d)
    mem2d = mem.reshape(batch * mem_len, d)
    for l in range(depth):
        lw = ln_w[l].reshape(3, 1, d)
        lb = ln_b[l].reshape(3, 1, d)
        h = _mixer_sublayer(h, w_in, l, mlstm_gate_b[l], pool_w[l], pool_scale[l], conv_w[l], mlstm_norm_w[l],
                            ret_norm_w[l], w_branch[l], w_mix_out[l], lw[0], lb[0],
                            batch=batch, seq=seq, alpha=alpha)
        h = _xattn_sublayer(h, mem2d, xa_wq[l], xa_wk[l], xa_wv[l], xa_wo[l], lw[1], lb[1],
                            seq=seq, mem_len=mem_len, alpha=alpha)
        h = _moe_sublayer(h, router_w[l], router_b[l], moe_w_gu, moe_w_dn, l, shared_w_gu[l], shared_w_dn[l],
                          lw[2], lb[2], alpha=alpha)
    return h.reshape(batch, seq, d)
```

```python
import functools

import numpy as np
import jax
import jax.numpy as jnp
from jax import lax
from jax.experimental import pallas as pl
from jax.experimental.pallas import tpu as pltpu

F32 = jnp.float32
BF16 = jnp.bfloat16
I32 = jnp.int32

N_BRANCH = 4
HEAD_DIM = 128
POOL_WINDOWS = (2, 4, 8, 16)
CONV_WIDTH = 3
ROPE_BASE = 10000.0
XATTN_HEADS = 4
N_EXPERTS = 64
TOP_K = 8
N_GROUPS = 8
TOPK_GROUPS = 4
ROUTE_SCALE = 2.5
LN_EPS = 1e-5

LANES = 128
V7X_VMEM_BYTES = 64 * 1024 * 1024
VMEM_LIMIT = 56 * 1024 * 1024

SCAN_CHUNK = 256
SCAN_BLOCK = 512
SCAN_BATCH = 2
HALO = 16
MOE_BM = 1024
ROUTE_TM = 256
SEG_ALIGN = 16
ONEHOT_ROWS = 512

def _cp(*sem):
    return pltpu.CompilerParams(dimension_semantics=sem, vmem_limit_bytes=VMEM_LIMIT)


def _sigmoid(x):
    return 1.0 / (1.0 + jnp.exp(-x))


def _silu(x):
    return x * _sigmoid(x)


def _log_sigmoid(x):
    return jnp.minimum(x, 0.0) - jnp.log(1.0 + jnp.exp(-jnp.abs(x)))


def _layer_norm(z, w, b):
    mu = jnp.mean(z, axis=-1, keepdims=True)
    d = z - mu
    var = jnp.mean(d * d, axis=-1, keepdims=True)
    return d * lax.rsqrt(var + LN_EPS) * w + b


def _head_norm(h, w):
    mu = jnp.mean(h, axis=-1, keepdims=True)
    d = h - mu
    var = jnp.mean(d * d, axis=-1, keepdims=True)
    return d * lax.rsqrt(var + LN_EPS) * w


def _mm_kernel(x_ref, w_ref, o_ref, xb_ref):
    @pl.when(pl.program_id(1) == 0)
    def _():
        xb_ref[...] = x_ref[...].astype(BF16)

    o_ref[...] = jnp.dot(xb_ref[...], w_ref[...], preferred_element_type=F32)


def _realign_cast_kernel(a_ref, b_ref, o_ref, *, shift, lo, hi):
    j = pl.program_id(1)
    tn = o_ref.shape[1]
    shifted = jnp.logical_and(j >= lo, j < hi)

    @pl.when(shifted)
    def _():
        both = jnp.concatenate([a_ref[...], b_ref[...]], axis=1)
        o_ref[...] = both[:, shift:shift + tn].astype(BF16)

    @pl.when(jnp.logical_not(shifted))
    def _():
        o_ref[...] = a_ref[...].astype(BF16)


def _realign_cast(w_all, layer, *, lo_col, hi_col, shift, tr, tn):
    _, rows, _ = w_all.shape
    lo, hi = lo_col // tn, hi_col // tn
    src = lambda j: jnp.where(j == hi, lo, j)
    return pl.pallas_call(
        functools.partial(_realign_cast_kernel, shift=shift, lo=lo, hi=hi),
        grid=(rows // tr, hi + 1),
        in_specs=[pl.BlockSpec((None, tr, tn), lambda i, j: (layer, i, src(j))),
                  pl.BlockSpec((None, tr, LANES), lambda i, j: (layer, i, (src(j) + 1) * (tn // LANES)))],
        out_specs=pl.BlockSpec((tr, tn), lambda i, j: (i, j)),
        out_shape=jax.ShapeDtypeStruct((rows, hi_col + tn), BF16),
        compiler_params=_cp("parallel", "parallel"),
    )(w_all, w_all)


def _matmul(x, w, *, tm, tn, ncols=None, col0=0):
    t, k = x.shape
    n = w.shape[1] if ncols is None else ncols
    return pl.pallas_call(
        _mm_kernel,
        grid=(t // tm, n // tn),
        in_specs=[pl.BlockSpec((tm, k), lambda i, j: (i, 0)),
                  pl.BlockSpec((k, tn), lambda i, j: (0, col0 // tn + j))],
        out_specs=pl.BlockSpec((tm, tn), lambda i, j: (i, j)),
        out_shape=jax.ShapeDtypeStruct((t, n), F32),
        scratch_shapes=[pltpu.VMEM((tm, k), BF16)],
        compiler_params=_cp("parallel", "arbitrary"),
    )(x, w)


def _poolconv_kernel(u_ref, uh_ref, h_ref, hh_ref, b_ref, c_ref, ch_ref, pw_ref, ps_ref, cw_ref,
                     yp_ref, yc_ref, ubuf, zbuf, *, tb, n_tblk):
    first = (pl.program_id(0) % n_tblk) == 0
    ubuf[0:HALO, :] = jnp.where(first, 0.0, uh_ref[...])
    ubuf[HALO:HALO + tb, :] = u_ref[...]
    zbuf[0:HALO, :] = jnp.where(first, 0.0, ch_ref[...] * hh_ref[...])
    zbuf[HALO:HALO + tb, :] = c_ref[...] * h_ref[...]

    t_pos = (pl.program_id(0) % n_tblk) * tb + lax.broadcasted_iota(I32, (tb, LANES), 0)
    gw = u_ref.shape[1] // len(POOL_WINDOWS)
    for grp, win in enumerate(POOL_WINDOWS):
        lanes = slice(grp * gw, (grp + 1) * gw)
        cur = ubuf[HALO:HALO + tb, lanes]
        acc = cur
        for lag in range(1, win):
            acc = acc + ubuf[HALO - lag:HALO - lag + tb, lanes]
        count = jnp.minimum(t_pos + 1, win).astype(F32)
        mixed = acc / count - cur
        y = jnp.dot(mixed.astype(BF16), pw_ref[grp], preferred_element_type=F32)
        yp_ref[:, lanes] = y * ps_ref[:, lanes]

    conv = cw_ref[0:1, :] * zbuf[HALO:HALO + tb, :]
    for lag in range(1, CONV_WIDTH):
        conv = conv + cw_ref[lag:lag + 1, :] * zbuf[HALO - lag:HALO - lag + tb, :]
    yc_ref[...] = b_ref[...] * conv


def _pool_conv(proj, pool_w, pool_scale, conv_w, *, seq, width, tb):
    t = proj.shape[0]
    n_tblk = seq // tb
    ratio = tb // HALO

    def cur(col):
        return pl.BlockSpec((tb, width), lambda g: (g, col))

    def halo(col):
        return pl.BlockSpec((HALO, width), lambda g: (jnp.maximum(g * ratio - 1, 0), col))

    full = lambda shape: pl.BlockSpec(shape, lambda g: (0,) * len(shape))
    return pl.pallas_call(
        functools.partial(_poolconv_kernel, tb=tb, n_tblk=n_tblk),
        grid=(t // tb,),
        in_specs=[cur(0), halo(0), cur(1), halo(1), cur(2), cur(3), halo(3),
                  full(pool_w.shape), full(pool_scale.shape), full(conv_w.shape)],
        out_specs=[pl.BlockSpec((tb, width), lambda g: (g, 0))] * 2,
        out_shape=[jax.ShapeDtypeStruct((t, width), F32)] * 2,
        scratch_shapes=[pltpu.VMEM((HALO + tb, width), F32)] * 2,
        compiler_params=_cp("parallel"),
    )(proj, proj, proj, proj, proj, proj, proj, pool_w, pool_scale, conv_w)


def _cum_rows(x, op, fill):
    n = x.shape[0]
    row = lax.broadcasted_iota(I32, x.shape, 0)
    shift = 1
    while shift < n:
        x = op(x, jnp.where(row >= shift, pltpu.roll(x, shift, axis=0), fill))
        shift *= 2
    return x


def _mlstm_kernel(q_ref, k_ref, v_ref, o_ref, g_ref, gb_ref, nw_ref, y_ref, c_scr, n_scr, m_scr,
                  *, chunk, heads):
    @pl.when(pl.program_id(1) == 0)
    def _():
        c_scr[...] = jnp.zeros_like(c_scr)
        n_scr[...] = jnp.zeros_like(n_scr)
        m_scr[...] = jnp.zeros_like(m_scr)

    nb, tb = q_ref.shape[0], q_ref.shape[1]
    hd = HEAD_DIM
    scale = hd ** -0.5
    tri = (lax.broadcasted_iota(I32, (chunk, chunk), 0) >= lax.broadcasted_iota(I32, (chunk, chunk), 1))
    for c, bb in [(c, bb) for c in range(tb // chunk) for bb in range(nb)]:
        rows = slice(c * chunk, (c + 1) * chunk)
        gates = g_ref[bb, rows, :] + gb_ref[...]
        lf = _log_sigmoid(pltpu.roll(gates, LANES - heads, axis=1))
        cumf = _cum_rows(lf, jnp.add, 0.0)
        a = gates - cumf
        m_prev = m_scr[bb]
        mu = jnp.maximum(_cum_rows(a, jnp.maximum, -jnp.inf), m_prev)
        mu_last = mu[chunk - 1:chunk, :]
        a_t = a.T
        for h in range(heads):
            cols = slice(h * hd, (h + 1) * hd)
            q = q_ref[bb, rows, cols]
            k = k_ref[bb, rows, cols] * scale
            v = v_ref[bb, rows, cols]
            qb, kb, vb = q.astype(BF16), k.astype(BF16), v.astype(BF16)
            mu_col = mu[:, h:h + 1]
            a_col = a[:, h:h + 1]
            m_prev_h = m_prev[:, h:h + 1]
            mu_last_h = mu_last[:, h:h + 1]
            dmat = jnp.exp(jnp.where(tri, a_t[h:h + 1, :] - mu_col, -jnp.inf))
            s = lax.dot_general(qb, kb, (((1,), (1,)), ((), ())), preferred_element_type=F32)
            p = dmat * s
            inter = jnp.exp(m_prev_h - mu_col)
            state = bb * heads + h
            c_h = c_scr[state]
            n_h = n_scr[state]
            num = inter * jnp.dot(qb, c_h.astype(BF16), preferred_element_type=F32) \
                + jnp.dot(p.astype(BF16), vb, preferred_element_type=F32)
            den = inter * jnp.sum(q * n_h, axis=-1, keepdims=True) + jnp.sum(p, axis=-1, keepdims=True)
            floor = jnp.exp(-(cumf[:, h:h + 1] + mu_col))
            h_out = num / jnp.maximum(jnp.abs(den), floor)

            wg = jnp.exp(a_col - mu_last_h)
            decay = jnp.exp(m_prev_h - mu_last_h)
            kw = k * wg
            c_scr[state] = decay * c_h + lax.dot_general(kw.astype(BF16), vb, (((0,), (0,)), ((), ())),
                                                         preferred_element_type=F32)
            n_scr[state] = decay * n_h + jnp.sum(kw, axis=0, keepdims=True)

            gated = _sigmoid(o_ref[bb, rows, cols]) * h_out
            y_ref[bb, rows, cols] = _head_norm(gated, nw_ref[:, cols])
        m_scr[bb] = cumf[chunk - 1:chunk, :] + mu_last


def _mlstm(proj, gates, gate_b, norm_w, *, batch, seq, width, col0):
    t = proj.shape[0]
    heads = width // HEAD_DIM
    nb = SCAN_BATCH if batch % SCAN_BATCH == 0 else 1
    proj3 = proj.reshape(batch, seq, proj.shape[1])
    gates3 = gates.reshape(batch, seq, LANES)
    blk = lambda col: pl.BlockSpec((nb, SCAN_BLOCK, width), lambda b, i: (b, i, col))
    y = pl.pallas_call(
        functools.partial(_mlstm_kernel, chunk=SCAN_CHUNK, heads=heads),
        grid=(batch // nb, seq // SCAN_BLOCK),
        in_specs=[blk(col0), blk(col0 + 1), blk(col0 + 2), blk(col0 + 3),
                  pl.BlockSpec((nb, SCAN_BLOCK, LANES), lambda b, i: (b, i, 0)),
                  pl.BlockSpec((1, LANES), lambda b, i: (0, 0)),
                  pl.BlockSpec((1, width), lambda b, i: (0, 0))],
        out_specs=pl.BlockSpec((nb, SCAN_BLOCK, width), lambda b, i: (b, i, 0)),
        out_shape=jax.ShapeDtypeStruct((batch, seq, width), F32),
        scratch_shapes=[pltpu.VMEM((nb * heads, HEAD_DIM, HEAD_DIM), F32),
                        pltpu.VMEM((nb * heads, 1, HEAD_DIM), F32),
                        pltpu.VMEM((nb, 1, LANES), F32)],
        compiler_params=_cp("parallel", "arbitrary"),
    )(proj3, proj3, proj3, proj3, gates3, gate_b, norm_w)
    return y.reshape(t, width)


def _ret_kernel(q_ref, k_ref, v_ref, g_ref, cos_ref, sin_ref, intra_ref, cross_ref, zeta_ref, nw_ref,
                y_ref, r_scr, *, chunk, heads, chunk_decay):
    @pl.when(pl.program_id(1) == 0)
    def _():
        r_scr[...] = jnp.zeros_like(r_scr)

    tb = q_ref.shape[0]
    hd = HEAD_DIM
    scale = hd ** -0.5
    for c in range(tb // chunk):
        rows = slice(c * chunk, (c + 1) * chunk)
        cos = cos_ref[rows, :]
        sin = sin_ref[rows, :]
        for h in range(heads):
            cols = slice(h * hd, (h + 1) * hd)
            q = q_ref[rows, cols]
            k = k_ref[rows, cols]
            q = q * cos + pltpu.roll(q, hd // 2, axis=1) * sin
            k = (k * cos + pltpu.roll(k, hd // 2, axis=1) * sin) * scale
            vb = v_ref[rows, cols].astype(BF16)
            qb = q.astype(BF16)
            s = lax.dot_general(qb, k.astype(BF16), (((1,), (1,)), ((), ())), preferred_element_type=F32)
            inner = jnp.dot((s * intra_ref[h]).astype(BF16), vb, preferred_element_type=F32)
            r_h = r_scr[h]
            crs = jnp.dot(qb, r_h.astype(BF16), preferred_element_type=F32) * cross_ref[h]
            r_scr[h] = chunk_decay[h] * r_h + lax.dot_general(
                (k * zeta_ref[h]).astype(BF16), vb, (((0,), (0,)), ((), ())), preferred_element_type=F32)
            y_ref[rows, cols] = _silu(g_ref[rows, cols]) * _head_norm(inner + crs, nw_ref[:, cols])


def _retention(proj, norm_w, *, batch, seq, width, col0):
    t = proj.shape[0]
    heads = width // HEAD_DIM
    chunk = SCAN_CHUNK
    n_tblk = seq // SCAN_BLOCK
    half = HEAD_DIM // 2
    inv = np.float32(ROPE_BASE) ** (-np.arange(half, dtype=np.float32) / np.float32(half))
    ang = (np.arange(seq, dtype=np.float32)[:, None] * inv[None, :]).astype(np.float64)
    cos_t = jnp.asarray(np.concatenate([np.cos(ang), np.cos(ang)], axis=-1), F32)
    sin_t = jnp.asarray(np.concatenate([-np.sin(ang), np.sin(ang)], axis=-1), F32)
    log_g = jnp.log(1.0 - 2.0 ** (-5.0 - jnp.arange(heads, dtype=F32)))
    tt = jnp.arange(chunk, dtype=F32)
    lag = tt[:, None] - tt[None, :]
    intra = jnp.where(lag >= 0, jnp.exp(jnp.maximum(lag, 0.0)[None] * log_g[:, None, None]), 0.0)
    cross = jnp.broadcast_to(jnp.exp((tt + 1.0)[None, :] * log_g[:, None])[:, :, None], (heads, chunk, HEAD_DIM))
    zeta = jnp.broadcast_to(jnp.exp((chunk - 1.0 - tt)[None, :] * log_g[:, None])[:, :, None],
                            (heads, chunk, HEAD_DIM))
    chunk_decay = tuple(float((1.0 - 2.0 ** (-5.0 - h)) ** chunk) for h in range(heads))

    blk = lambda col: pl.BlockSpec((SCAN_BLOCK, width), lambda b, i: (b * n_tblk + i, col))
    pos = pl.BlockSpec((SCAN_BLOCK, HEAD_DIM), lambda b, i: (i, 0))
    full3 = lambda a: pl.BlockSpec(a.shape, lambda b, i: (0, 0, 0))
    return pl.pallas_call(
        functools.partial(_ret_kernel, chunk=chunk, heads=heads, chunk_decay=chunk_decay),
        grid=(batch, n_tblk),
        in_specs=[blk(col0), blk(col0 + 1), blk(col0 + 2), blk(col0 + 3), pos, pos,
                  full3(intra), full3(cross), full3(zeta),
                  pl.BlockSpec((1, width), lambda b, i: (0, 0))],
        out_specs=pl.BlockSpec((SCAN_BLOCK, width), lambda b, i: (b * n_tblk + i, 0)),
        out_shape=jax.ShapeDtypeStruct((t, width), F32),
        scratch_shapes=[pltpu.VMEM((heads, HEAD_DIM, HEAD_DIM), F32)],
        compiler_params=_cp("parallel", "arbitrary"),
    )(proj, proj, proj, proj, cos_t, sin_t, intra, cross, zeta, norm_w)


def _merge_kernel(x_ref, *refs):
    y_refs, wg_refs = refs[:N_BRANCH], refs[N_BRANCH:2 * N_BRANCH]
    wb_ref, o_ref, xb_ref = refs[2 * N_BRANCH:]

    @pl.when(pl.program_id(1) == 0)
    def _():
        xb_ref[...] = x_ref[...].astype(BF16)

    xb = xb_ref[...]
    acc = None
    for n in range(N_BRANCH):
        gate = _sigmoid(jnp.dot(xb, wg_refs[n][...], preferred_element_type=F32))
        term = gate * jnp.dot(y_refs[n][...].astype(BF16), wb_ref[n], preferred_element_type=F32)
        acc = term if acc is None else acc + term
    o_ref[...] = acc


def _merge(x, branches, w_gate, gate_col0, w_branch, *, tm, tn):
    t, d = x.shape
    width = branches[0].shape[1]
    ybs = pl.BlockSpec((tm, width), lambda i, j: (i, 0))

    def gate_spec(n):
        return pl.BlockSpec((d, tn), lambda i, j: (0, (gate_col0 + n * d) // tn + j))

    return pl.pallas_call(
        _merge_kernel,
        grid=(t // tm, d // tn),
        in_specs=[pl.BlockSpec((tm, d), lambda i, j: (i, 0))] + [ybs] * N_BRANCH
                 + [gate_spec(n) for n in range(N_BRANCH)]
                 + [pl.BlockSpec((N_BRANCH, width, tn), lambda i, j: (0, 0, j))],
        out_specs=pl.BlockSpec((tm, tn), lambda i, j: (i, j)),
        out_shape=jax.ShapeDtypeStruct((t, d), F32),
        scratch_shapes=[pltpu.VMEM((tm, d), BF16)],
        compiler_params=_cp("parallel", "arbitrary"),
    )(x, *branches, *([w_gate] * N_BRANCH), w_branch)


def _proj_ln_kernel(a_ref, w_ref, r_ref, lw_ref, lb_ref, o_ref, *, alpha):
    y = jnp.dot(a_ref[...].astype(BF16), w_ref[...], preferred_element_type=F32)
    o_ref[...] = _layer_norm(alpha * r_ref[...] + y, lw_ref[...], lb_ref[...])


def _proj_ln(a, w, resid, ln_w, ln_b, *, alpha, tm):
    t, k = a.shape
    d = w.shape[1]
    row = lambda n: pl.BlockSpec((tm, n), lambda i: (i, 0))
    const = lambda shape: pl.BlockSpec(shape, lambda i: (0, 0))
    return pl.pallas_call(
        functools.partial(_proj_ln_kernel, alpha=alpha),
        grid=(t // tm,),
        in_specs=[row(k), const((k, d)), row(d), const((1, d)), const((1, d))],
        out_specs=row(d),
        out_shape=jax.ShapeDtypeStruct((t, d), F32),
        compiler_params=_cp("parallel"),
    )(a, w, resid, ln_w, ln_b)


def _xattn_kernel(x_ref, wq_ref, kv_ref, wo_ref, lw_ref, lb_ref, o_ref, *, alpha, heads):
    x = x_ref[...]
    hd = HEAD_DIM
    inner = heads * hd
    q = jnp.dot(x.astype(BF16), wq_ref[...], preferred_element_type=F32)
    outs = []
    for h in range(heads):
        qh = q[:, h * hd:(h + 1) * hd].astype(BF16)
        kh = kv_ref[:, h * hd:(h + 1) * hd].astype(BF16)
        vh = kv_ref[:, inner + h * hd:inner + (h + 1) * hd].astype(BF16)
        s = lax.dot_general(qh, kh, (((1,), (1,)), ((), ())), preferred_element_type=F32) * hd ** -0.5
        s = s - jnp.max(s, axis=-1, keepdims=True)
        e = jnp.exp(s)
        p = e / jnp.sum(e, axis=-1, keepdims=True)
        outs.append(jnp.dot(p.astype(BF16), vh, preferred_element_type=F32).astype(BF16))
    o = jnp.concatenate(outs, axis=-1)
    y = jnp.dot(o, wo_ref[...], preferred_element_type=F32)
    o_ref[...] = _layer_norm(alpha * x + y, lw_ref[...], lb_ref[...])


def _xattn(x, kv, wq, wo, ln_w, ln_b, *, alpha, seq, mem_len, tm):
    t, d = x.shape
    inner = wq.shape[1]
    n_tblk = seq // tm
    const = lambda shape: pl.BlockSpec(shape, lambda i: (0, 0))
    return pl.pallas_call(
        functools.partial(_xattn_kernel, alpha=alpha, heads=XATTN_HEADS),
        grid=(t // tm,),
        in_specs=[pl.BlockSpec((tm, d), lambda i: (i, 0)), const((d, inner)),
                  pl.BlockSpec((mem_len, 2 * inner), lambda i: (i // n_tblk, 0)),
                  const((inner, d)), const((1, d)), const((1, d))],
        out_specs=pl.BlockSpec((tm, d), lambda i: (i, 0)),
        out_shape=jax.ShapeDtypeStruct((t, d), F32),
        compiler_params=_cp("parallel"),
    )(x, wq, kv, wo, ln_w, ln_b)


def _round_up_pow2(x, m):
    shift = m.bit_length() - 1
    return jnp.left_shift(jnp.right_shift(x + (m - 1), shift), shift)


def _route_kernel(x_ref, wr_ref, rb_ref, w_ref, lrow_ref, segtab_ref, blk_ref, seg_ref, size_all, tot, *, tm, bm):
    step = pl.program_id(0)

    @pl.when(step == 0)
    def _():
        tot[...] = jnp.zeros_like(tot)

    e_n, g_n = N_EXPERTS, N_GROUPS
    per = e_n // g_n
    def split(a):
        hi = a.astype(BF16)
        return hi, (a - hi.astype(F32)).astype(BF16)

    def nt(a, b):
        return lax.dot_general(a, b, (((1,), (1,)), ((), ())), preferred_element_type=F32)

    (w_hi, w_lo), (x_hi, x_lo) = split(wr_ref[...]), split(x_ref[...])
    logits = nt(w_hi, x_hi) + (nt(w_hi, x_lo) + nt(w_lo, x_hi))
    scores = _sigmoid(logits)
    biased = scores + rb_ref[...]
    b3 = biased.reshape(g_n, per, tm)
    member = lax.broadcasted_iota(I32, (g_n, per, tm), 1)
    top1 = jnp.max(b3, axis=1, keepdims=True)
    first = jnp.min(jnp.where(b3 == top1, member, per), axis=1, keepdims=True)
    top2 = jnp.max(jnp.where(member == first, -jnp.inf, b3), axis=1, keepdims=True)
    gs = top1 + top2
    gid = lax.broadcasted_iota(I32, (g_n, 1, tm), 0)
    rank = jnp.zeros((g_n, 1, tm), I32)
    for other in range(g_n):
        o = gs[other:other + 1]
        ahead = jnp.logical_or(o > gs, jnp.logical_and(o == gs, other < gid))
        rank = rank + jnp.where(ahead, 1, 0)
    cur = jnp.where(rank < TOPK_GROUPS, b3, -jnp.inf).reshape(e_n, tm)

    eid = lax.broadcasted_iota(I32, (e_n, tm), 0)
    picks, vals = [], []
    sel = jnp.zeros((e_n, tm), F32)
    for k in range(TOP_K):
        mx = jnp.max(cur, axis=0, keepdims=True)
        ik = jnp.min(jnp.where(cur == mx, eid, e_n), axis=0, keepdims=True)
        hit = eid == ik
        vals.append(jnp.sum(jnp.where(hit, scores, 0.0), axis=0, keepdims=True))
        cur = jnp.where(hit, -jnp.inf, cur)
        sel = jnp.where(hit, 1.0, sel)
        picks.append(ik)
    total = vals[0]
    for v in vals[1:]:
        total = total + v

    tri = jnp.where(lax.broadcasted_iota(I32, (tm, tm), 0) <= lax.broadcasted_iota(I32, (tm, tm), 1), 1.0, 0.0)
    incl = jnp.dot(sel.astype(BF16), tri.astype(BF16), preferred_element_type=F32)
    size = _round_up_pow2(jnp.broadcast_to(incl[:, tm - 1:tm], (e_n, LANES)).astype(I32), SEG_ALIGN)
    loff = _cum_rows(size, jnp.add, 0) - size
    base = loff[:, 0:1].astype(F32) + incl - 1.0
    for k in range(TOP_K):
        w_ref[k:k + 1, :] = vals[k] / total * ROUTE_SCALE
        lrow_ref[0, k:k + 1, :] = jnp.sum(jnp.where(eid == picks[k], base, 0.0),
                                          axis=0, keepdims=True).astype(I32)
    size_all[step] = size
    tot[...] = tot[...] + size

    @pl.when(step == pl.num_programs(0) - 1)
    def _():
        rows = tot[...]
        pcnt = _round_up_pow2(rows, bm)
        pend = _cum_rows(pcnt, jnp.add, 0)

        def tile_seg(i, run):
            segtab_ref[i, 0] = run
            segtab_ref[i, 1] = size_all[i]
            return run + size_all[i]

        lax.fori_loop(0, pl.num_programs(0), tile_seg, pend - pcnt)

        nb = blk_ref.shape[2]
        row0 = lax.broadcasted_iota(I32, (e_n, nb), 1) * bm
        total_rows = pend[e_n - 1:e_n, 0:1]
        owner = jnp.sum(jnp.where(pend[:, 0:1] <= row0, 1, 0), axis=0, keepdims=True)
        last_owner = jnp.sum(jnp.where(pend[:, 0:1] < total_rows, 1, 0), axis=0, keepdims=True)
        blk_ref[0] = jnp.where(row0[0:1, :] < total_rows, jnp.minimum(owner, e_n - 1), last_owner)
        data_end = pend - pcnt + rows
        inside = jnp.logical_and(pend[:, 0:1] - pcnt[:, 0:1] <= row0, row0 < pend[:, 0:1])
        blk_ref[1] = jnp.sum(jnp.where(inside, jnp.clip(data_end[:, 0:1] - row0, 0, bm), 0), axis=0, keepdims=True)
        seg_ref[0] = pend
        seg_ref[1] = pcnt


def _route(x, router_w_t, router_b, *, tm, bm, n_blocks):
    t, d = x.shape
    e_n = N_EXPERTS
    n_t = t // tm
    assert bm & (bm - 1) == 0
    nb_pad = -(-n_blocks // LANES) * LANES
    return pl.pallas_call(
        functools.partial(_route_kernel, tm=tm, bm=bm),
        grid=(n_t,),
        in_specs=[pl.BlockSpec((tm, d), lambda i: (i, 0)),
                  pl.BlockSpec((e_n, d), lambda i: (0, 0)),
                  pl.BlockSpec((e_n, 1), lambda i: (0, 0))],
        out_specs=[pl.BlockSpec((TOP_K, tm), lambda i: (0, i)),
                   pl.BlockSpec((1, TOP_K, tm), lambda i: (i, 0, 0)),
                   pl.BlockSpec((n_t, 2, e_n, LANES), lambda i: (0, 0, 0, 0)),
                   pl.BlockSpec((2, 1, nb_pad), lambda i: (0, 0, 0)),
                   pl.BlockSpec((2, e_n, LANES), lambda i: (0, 0, 0))],
        out_shape=[jax.ShapeDtypeStruct((TOP_K, t), F32), jax.ShapeDtypeStruct((n_t, TOP_K, tm), I32),
                   jax.ShapeDtypeStruct((n_t, 2, e_n, LANES), I32), jax.ShapeDtypeStruct((2, 1, nb_pad), I32),
                   jax.ShapeDtypeStruct((2, e_n, LANES), I32)],
        scratch_shapes=[pltpu.VMEM((n_t, e_n, LANES), I32), pltpu.VMEM((e_n, LANES), I32)],
        compiler_params=_cp("arbitrary"),
    )(x, router_w_t, router_b)


def _tile_rows(tm):
    worst = TOP_K * tm + N_EXPERTS * (SEG_ALIGN - 1)
    return -(-worst // ONEHOT_ROWS) * ONEHOT_ROWS


def _onehot_rows(chunk, lrow, values, tm):
    rid = chunk * ONEHOT_ROWS + lax.broadcasted_iota(I32, (ONEHOT_ROWS, tm), 0)
    acc = jnp.zeros((ONEHOT_ROWS, tm), F32)
    for k in range(TOP_K):
        acc = jnp.where(rid == lrow[k:k + 1, :], 1.0 if values is None else values[k:k + 1, :], acc)
    return acc.astype(BF16)


def _start_segments(gstart_ref, size_ref, tile, make_copy, keep=None):
    def body(e, carry):
        loff, started = carry
        n = size_ref[tile * N_EXPERTS + e]
        wanted = n > 0 if keep is None else jnp.logical_and(n > 0, keep(loff + n))

        @pl.when(wanted)
        def _():
            make_copy(pl.multiple_of(gstart_ref[tile * N_EXPERTS + e], SEG_ALIGN),
                      pl.multiple_of(loff, SEG_ALIGN), pl.multiple_of(n, SEG_ALIGN)).start()
        return loff + n, started + jnp.where(wanted, n, 0)
    return lax.fori_loop(0, N_EXPERTS, body, (0, 0))


def _tile_total(size_ref, tile):
    return lax.fori_loop(0, N_EXPERTS, lambda e, s: s + size_ref[tile * N_EXPERTS + e], 0)


def _wait_rows(make_copy, rows):
    @pl.when(rows > 0)
    def _():
        make_copy(0, 0, pl.multiple_of(rows, SEG_ALIGN)).wait()


def _dispatch_kernel(gstart_ref, size_ref, x_ref, lrow_ref, xs_ref, stage, inflight, sem_a, sem_b, *, tm):
    i = pl.program_id(0)
    last = pl.num_programs(0) - 1
    n_chunks = stage.shape[0] // ONEHOT_ROWS
    split_chunk = n_chunks // 2
    split = split_chunk * ONEHOT_ROWS
    assert tm <= ONEHOT_ROWS

    def copy_on(sem):
        return lambda g, loff, n: pltpu.make_async_copy(stage.at[pl.ds(loff, n), :], xs_ref.at[pl.ds(g, n), :], sem)

    xb = x_ref[...].astype(BF16)
    lrow = lrow_ref[0]
    n_rows = _tile_total(size_ref, i)

    def chunks(lo, hi):
        for c in range(lo, hi):
            def one(c=c):
                stage[c * ONEHOT_ROWS:(c + 1) * ONEHOT_ROWS, :] = jnp.dot(
                    _onehot_rows(c, lrow, None, tm), xb, preferred_element_type=F32).astype(BF16)
            if c * ONEHOT_ROWS < TOP_K * tm:
                one()
            else:
                pl.when(c * ONEHOT_ROWS < n_rows)(one)

    @pl.when(i == 0)
    def _():
        inflight[0] = 0
        inflight[1] = 0

    _wait_rows(copy_on(sem_a), inflight[0])
    chunks(0, split_chunk - 1)
    _wait_rows(copy_on(sem_b), inflight[1])
    chunks(split_chunk - 1, split_chunk)
    _, rows_a = _start_segments(gstart_ref, size_ref, i, copy_on(sem_a), lambda end: end <= split)
    chunks(split_chunk, n_chunks)
    _, rows_b = _start_segments(gstart_ref, size_ref, i, copy_on(sem_b), lambda end: end > split)
    inflight[0] = rows_a
    inflight[1] = rows_b

    @pl.when(i == last)
    def _():
        _wait_rows(copy_on(sem_a), rows_a)
        _wait_rows(copy_on(sem_b), rows_b)


def _dispatch(x, lrow, gstart, size, *, rows, tm):
    t, d = x.shape
    return pl.pallas_call(
        functools.partial(_dispatch_kernel, tm=tm),
        grid_spec=pltpu.PrefetchScalarGridSpec(
            num_scalar_prefetch=2,
            grid=(t // tm,),
            in_specs=[pl.BlockSpec((tm, d), lambda i, *_: (i, 0)),
                      pl.BlockSpec((1, TOP_K, tm), lambda i, *_: (i, 0, 0))],
            out_specs=pl.BlockSpec(memory_space=pl.ANY),
            scratch_shapes=[pltpu.VMEM((_tile_rows(tm), d), BF16), pltpu.SMEM((2,), I32)]
                           + [pltpu.SemaphoreType.DMA(())] * 2,
        ),
        out_shape=jax.ShapeDtypeStruct((rows, d), BF16),
        compiler_params=_cp("arbitrary"),
    )(gstart, size, x, lrow)


def _expert_block_kernel(blk_ref, nused_ref, xs_ref, wgu_ref, wdn_ref, ys_ref, wgu_b, wdn_b, *, n_blocks, bm):
    j = pl.program_id(0)
    valid = blk_ref[n_blocks + j]
    changed = jnp.logical_or(j == 0, blk_ref[j] != blk_ref[jnp.maximum(j - 1, 0)])

    @pl.when(jnp.logical_and(valid > 0, changed))
    def _():
        wgu_b[...] = wgu_ref[...].astype(BF16)
        wdn_b[...] = wdn_ref[...].astype(BF16)

    def swiglu(rows, masked):
        x = xs_ref[0:rows, :]
        if masked:
            x = jnp.where(lax.broadcasted_iota(I32, x.shape, 0) < valid, x, jnp.zeros_like(x))
        f = wdn_b.shape[0]
        gu = jnp.dot(x, wgu_b[...], preferred_element_type=F32)
        hidden = (_silu(gu[:, :f]) * gu[:, f:]).astype(BF16)
        ys_ref[0:rows, :] = jnp.dot(hidden, wdn_b[...], preferred_element_type=F32).astype(BF16)

    @pl.when(valid == bm)
    def _():
        swiglu(bm, False)

    quarter = bm // 4
    for q in range(1, 5):
        @pl.when(jnp.logical_and(jnp.logical_and(valid > (q - 1) * quarter, valid <= q * quarter), valid < bm))
        def _():
            swiglu(q * quarter, True)


def _experts_blocked(xs, blk, nused, w_gu, w_dn, layer, *, n_blocks, bm):
    rows, d = xs.shape
    f2 = w_gu.shape[3]
    f = w_dn.shape[2]
    row_blk = lambda j, bl, nu: (jnp.minimum(j, nu[0] - 1), 0)
    return pl.pallas_call(
        functools.partial(_expert_block_kernel, n_blocks=n_blocks, bm=bm),
        grid_spec=pltpu.PrefetchScalarGridSpec(
            num_scalar_prefetch=2,
            grid=(n_blocks,),
            in_specs=[pl.BlockSpec((bm, d), row_blk),
                      pl.BlockSpec((None, None, d, f2), lambda j, bl, nu: (layer, bl[j], 0, 0)),
                      pl.BlockSpec((None, None, f, d), lambda j, bl, nu: (layer, bl[j], 0, 0))],
            out_specs=pl.BlockSpec((bm, d), row_blk),
            scratch_shapes=[pltpu.VMEM((d, f2), BF16), pltpu.VMEM((f, d), BF16)],
        ),
        out_shape=jax.ShapeDtypeStruct((rows, d), BF16),
        compiler_params=_cp("arbitrary"),
    )(blk, nused, xs, w_gu, w_dn)


def _combine_tile_kernel(gstart_ref, size_ref, x_ref, lrow_ref, w_ref, sdn_ref, lw_ref, lb_ref, sgu_hbm, ys_ref,
                         o_ref, ybuf, sgu, sems, wsem, *, tm, alpha):
    i = pl.program_id(0)
    last_tile = pl.num_programs(0) - 1
    n_chunks = ybuf.shape[1] // ONEHOT_ROWS
    sure_chunks = (TOP_K * tm) // ONEHOT_ROWS

    def contract_rows(w_rows, y_rows):
        return lax.dot_general(w_rows, y_rows, (((0,), (0,)), ((), ())), preferred_element_type=F32)

    def copy_to(slot):
        return lambda g, loff, n: pltpu.make_async_copy(ys_ref.at[pl.ds(g, n), :], ybuf.at[slot, pl.ds(loff, n), :],
                                                        sems.at[slot])

    def fetch(tile, slot):
        _start_segments(gstart_ref, size_ref, tile, copy_to(slot))

    n_rows = _tile_total(size_ref, i)

    @pl.when(i == 0)
    def _():
        ybuf[...] = jnp.zeros_like(ybuf)
        weights = pltpu.make_async_copy(sgu_hbm, sgu, wsem)
        weights.start()
        weights.wait()
        fetch(0, 0)

    for slot in range(2):
        @pl.when(i % 2 == slot)
        def _():
            @pl.when(i < last_tile)
            def _():
                fetch(i + 1, 1 - slot)

            x = x_ref[...]
            f = sdn_ref.shape[0]
            gu = jnp.dot(x.astype(BF16), sgu[...], preferred_element_type=F32)
            hidden = (_silu(gu[:, :f]) * gu[:, f:]).astype(BF16)
            acc = jnp.dot(hidden, sdn_ref[...], preferred_element_type=F32)
            _wait_rows(copy_to(slot), n_rows)
            lrow, w = lrow_ref[0], w_ref[...]
            for c in range(sure_chunks):
                rows = slice(c * ONEHOT_ROWS, (c + 1) * ONEHOT_ROWS)
                acc = acc + contract_rows(_onehot_rows(c, lrow, w, tm), ybuf[slot, rows, :])
            o_ref[...] = acc
            for c in range(sure_chunks, n_chunks):
                @pl.when(c * ONEHOT_ROWS < n_rows)
                def _():
                    rows = slice(c * ONEHOT_ROWS, (c + 1) * ONEHOT_ROWS)
                    o_ref[...] += contract_rows(_onehot_rows(c, lrow, w, tm), ybuf[slot, rows, :])
            o_ref[...] = _layer_norm(alpha * x + o_ref[...], lw_ref[...], lb_ref[...])


def _combine(x, ys, lrow, wts, gstart, size, s_gu, s_dn, ln_w, ln_b, *, alpha, tm):
    t, d = x.shape
    const = lambda shape: pl.BlockSpec(shape, lambda i, *_: (0, 0))
    return pl.pallas_call(
        functools.partial(_combine_tile_kernel, tm=tm, alpha=alpha),
        grid_spec=pltpu.PrefetchScalarGridSpec(
            num_scalar_prefetch=2,
            grid=(t // tm,),
            in_specs=[pl.BlockSpec((tm, d), lambda i, *_: (i, 0)),
                      pl.BlockSpec((1, TOP_K, tm), lambda i, *_: (i, 0, 0)),
                      pl.BlockSpec((TOP_K, tm), lambda i, *_: (0, i)),
                      const(s_dn.shape), const((1, d)), const((1, d)),
                      pl.BlockSpec(memory_space=pl.ANY), pl.BlockSpec(memory_space=pl.ANY)],
            out_specs=pl.BlockSpec((tm, d), lambda i, *_: (i, 0)),
            scratch_shapes=[pltpu.VMEM((2, _tile_rows(tm), d), BF16), pltpu.VMEM(s_gu.shape, BF16),
                            pltpu.SemaphoreType.DMA((2,)), pltpu.SemaphoreType.DMA(())],
        ),
        out_shape=jax.ShapeDtypeStruct((t, d), F32),
        compiler_params=_cp("arbitrary"),
    )(gstart, size, x, lrow, wts, s_dn, ln_w, ln_b, s_gu, ys)


def _mixer_sublayer(x, w_in_all, layer, gate_b, pool_w, pool_scale, conv_w, mlstm_norm_w, ret_norm_w, w_branch,
                    w_out, ln_w, ln_b, *, batch, seq, alpha):
    t, d = x.shape
    width = d // N_BRANCH
    heads = width // HEAD_DIM
    gate_off = 8 * width
    ret_off = gate_off
    g_off = ret_off + 4 * width
    if_off = g_off + N_BRANCH * d
    w_bf16 = _realign_cast(w_in_all, layer, lo_col=gate_off, hi_col=if_off, shift=2 * heads, tr=512, tn=512)
    gate_bias = jnp.pad(gate_b, (0, LANES - 2 * heads)).reshape(1, LANES)

    proj = _matmul(x, w_bf16, tm=1024, tn=1024, ncols=g_off)
    gates = _matmul(x, w_bf16, tm=1024, tn=LANES, ncols=LANES, col0=if_off)
    y_pool, y_conv = _pool_conv(proj, pool_w.astype(BF16), pool_scale.reshape(1, width), conv_w,
                                seq=seq, width=width, tb=512)
    y_mlstm = _mlstm(proj, gates, gate_bias, mlstm_norm_w.reshape(1, width),
                     batch=batch, seq=seq, width=width, col0=4)
    y_ret = _retention(proj, ret_norm_w.reshape(1, width), batch=batch, seq=seq, width=width,
                       col0=ret_off // width)
    merged = _merge(x, (y_pool, y_conv, y_mlstm, y_ret), w_bf16, g_off, w_branch.astype(BF16),
                    tm=512, tn=512)
    return _proj_ln(merged, w_out.astype(BF16), x, ln_w, ln_b, alpha=alpha, tm=512)


def _xattn_sublayer(x, mem2d, wq, wk, wv, wo, ln_w, ln_b, *, seq, mem_len, alpha):
    w_kv = jnp.concatenate([wk, wv], axis=1).astype(BF16)
    kv = _matmul(mem2d, w_kv, tm=min(mem2d.shape[0], 1024), tn=512)
    return _xattn(x, kv, wq.astype(BF16), wo.astype(BF16), ln_w, ln_b,
                  alpha=alpha, seq=seq, mem_len=mem_len, tm=512)


def _moe_sublayer(x, router_w, router_b, w_gu, w_dn, layer, s_gu, s_dn, ln_w, ln_b, *, alpha):
    t, d = x.shape
    e_n, bm = N_EXPERTS, MOE_BM
    tm = ROUTE_TM
    n_blocks = -(-(t * TOP_K + (t // tm) * e_n * (SEG_ALIGN - 1)) // bm) + e_n
    wts, lrow, segtab, blk, seg = _route(x, router_w.T, router_b.reshape(e_n, 1), tm=tm, bm=bm, n_blocks=n_blocks)
    gstart, size = segtab[:, 0, :, 0].reshape(-1), segtab[:, 1, :, 0].reshape(-1)
    nused = seg[0, e_n - 1, 0] // bm

    xs = _dispatch(x, lrow, gstart, size, rows=n_blocks * bm, tm=tm)
    ys = _experts_blocked(xs, blk[:, 0, :n_blocks].reshape(-1), nused.reshape(1), w_gu, w_dn, layer,
                          n_blocks=n_blocks, bm=bm)
    return _combine(x, ys, lrow, wts, gstart, size, s_gu.astype(BF16), s_dn.astype(BF16), ln_w, ln_b,
                    alpha=alpha, tm=tm)


def kernel(x, mem, w_in, mlstm_gate_b, pool_w, pool_scale, conv_w, mlstm_norm_w, ret_norm_w, w_branch,
           w_mix_out, xa_wq, xa_wk, xa_wv, xa_wo, router_w, router_b, moe_w_gu, moe_w_dn, shared_w_gu,
           shared_w_dn, ln_w, ln_b):
    batch, seq, d = x.shape
    depth = w_in.shape[0]
    mem_len = mem.shape[1]
    alpha = (2 * depth) ** 0.25
    h = x.reshape(batch * seq, d)
    mem2d = mem.reshape(batch * mem_len, d)
    for l in range(depth):
        lw = ln_w[l].reshape(3, 1, d)
        lb = ln_b[l].reshape(3, 1, d)
        h = _mixer_sublayer(h, w_in, l, mlstm_gate_b[l], pool_w[l], pool_scale[l], conv_w[l], mlstm_norm_w[l],
                            ret_norm_w[l], w_branch[l], w_mix_out[l], lw[0], lb[0],
                            batch=batch, seq=seq, alpha=alpha)
        h = _xattn_sublayer(h, mem2d, xa_wq[l], xa_wk[l], xa_wv[l], xa_wo[l], lw[1], lb[1],
                            seq=seq, mem_len=mem_len, alpha=alpha)
        h = _moe_sublayer(h, router_w[l], router_b[l], moe_w_gu, moe_w_dn, l, shared_w_gu[l], shared_w_dn[l],
                          lw[2], lb[2], alpha=alpha)
    return h.reshape(batch, seq, d)
```

```python
import functools

import numpy as np
import jax
import jax.numpy as jnp
from jax import lax
from jax.experimental import pallas as pl
from jax.experimental.pallas import tpu as pltpu

F32 = jnp.float32
BF16 = jnp.bfloat16
I32 = jnp.int32

N_BRANCH = 4
HEAD_DIM = 128
POOL_WINDOWS = (2, 4, 8, 16)
CONV_WIDTH = 3
ROPE_BASE = 10000.0
XATTN_HEADS = 4
N_EXPERTS = 64
TOP_K = 8
N_GROUPS = 8
TOPK_GROUPS = 4
ROUTE_SCALE = 2.5
LN_EPS = 1e-5

LANES = 128
V7X_VMEM_BYTES = 64 * 1024 * 1024
VMEM_LIMIT = 56 * 1024 * 1024

SCAN_CHUNK = 256
SCAN_BLOCK = 512
SCAN_BATCH = 2
HALO = 16
MOE_BM = 1024
ROUTE_TM = 256
SEG_ALIGN = 16
ONEHOT_ROWS = 512

def _cp(*sem):
    return pltpu.CompilerParams(dimension_semantics=sem, vmem_limit_bytes=VMEM_LIMIT)


def _sigmoid(x):
    return 1.0 / (1.0 + jnp.exp(-x))


def _silu(x):
    return x * _sigmoid(x)


def _log_sigmoid(x):
    return jnp.minimum(x, 0.0) - jnp.log(1.0 + jnp.exp(-jnp.abs(x)))


def _layer_norm(z, w, b):
    mu = jnp.mean(z, axis=-1, keepdims=True)
    d = z - mu
    var = jnp.mean(d * d, axis=-1, keepdims=True)
    return d * lax.rsqrt(var + LN_EPS) * w + b


def _head_norm(h, w):
    mu = jnp.mean(h, axis=-1, keepdims=True)
    d = h - mu
    var = jnp.mean(d * d, axis=-1, keepdims=True)
    return d * lax.rsqrt(var + LN_EPS) * w


def _mm_kernel(x_ref, w_ref, o_ref, xb_ref):
    @pl.when(pl.program_id(1) == 0)
    def _():
        xb_ref[...] = x_ref[...].astype(BF16)

    o_ref[...] = jnp.dot(xb_ref[...], w_ref[...], preferred_element_type=F32)


def _realign_cast_kernel(a_ref, b_ref, o_ref, *, shift, lo, hi):
    j = pl.program_id(1)
    tn = o_ref.shape[1]
    shifted = jnp.logical_and(j >= lo, j < hi)

    @pl.when(shifted)
    def _():
        both = jnp.concatenate([a_ref[...], b_ref[...]], axis=1)
        o_ref[...] = both[:, shift:shift + tn].astype(BF16)

    @pl.when(jnp.logical_not(shifted))
    def _():
        o_ref[...] = a_ref[...].astype(BF16)


def _realign_cast(w_all, layer, *, lo_col, hi_col, shift, tr, tn):
    _, rows, _ = w_all.shape
    lo, hi = lo_col // tn, hi_col // tn
    src = lambda j: jnp.where(j == hi, lo, j)
    return pl.pallas_call(
        functools.partial(_realign_cast_kernel, shift=shift, lo=lo, hi=hi),
        grid=(rows // tr, hi + 1),
        in_specs=[pl.BlockSpec((None, tr, tn), lambda i, j: (layer, i, src(j))),
                  pl.BlockSpec((None, tr, LANES), lambda i, j: (layer, i, (src(j) + 1) * (tn // LANES)))],
        out_specs=pl.BlockSpec((tr, tn), lambda i, j: (i, j)),
        out_shape=jax.ShapeDtypeStruct((rows, hi_col + tn), BF16),
        compiler_params=_cp("parallel", "parallel"),
    )(w_all, w_all)


def _matmul(x, w, *, tm, tn, ncols=None, col0=0):
    t, k = x.shape
    n = w.shape[1] if ncols is None else ncols
    return pl.pallas_call(
        _mm_kernel,
        grid=(t // tm, n // tn),
        in_specs=[pl.BlockSpec((tm, k), lambda i, j: (i, 0)),
                  pl.BlockSpec((k, tn), lambda i, j: (0, col0 // tn + j))],
        out_specs=pl.BlockSpec((tm, tn), lambda i, j: (i, j)),
        out_shape=jax.ShapeDtypeStruct((t, n), F32),
        scratch_shapes=[pltpu.VMEM((tm, k), BF16)],
        compiler_params=_cp("parallel", "arbitrary"),
    )(x, w)


def _poolconv_kernel(u_ref, uh_ref, h_ref, hh_ref, b_ref, c_ref, ch_ref, pw_ref, ps_ref, cw_ref,
                     yp_ref, yc_ref, ubuf, zbuf, *, tb, n_tblk):
    first = (pl.program_id(0) % n_tblk) == 0
    ubuf[0:HALO, :] = jnp.where(first, 0.0, uh_ref[...])
    ubuf[HALO:HALO + tb, :] = u_ref[...]
    zbuf[0:HALO, :] = jnp.where(first, 0.0, ch_ref[...] * hh_ref[...])
    zbuf[HALO:HALO + tb, :] = c_ref[...] * h_ref[...]

    t_pos = (pl.program_id(0) % n_tblk) * tb + lax.broadcasted_iota(I32, (tb, LANES), 0)
    gw = u_ref.shape[1] // len(POOL_WINDOWS)
    for grp, win in enumerate(POOL_WINDOWS):
        lanes = slice(grp * gw, (grp + 1) * gw)
        cur = ubuf[HALO:HALO + tb, lanes]
        acc = cur
        for lag in range(1, win):
            acc = acc + ubuf[HALO - lag:HALO - lag + tb, lanes]
        count = jnp.minimum(t_pos + 1, win).astype(F32)
        mixed = acc / count - cur
        y = jnp.dot(mixed.astype(BF16), pw_ref[grp], preferred_element_type=F32)
        yp_ref[:, lanes] = y * ps_ref[:, lanes]

    conv = cw_ref[0:1, :] * zbuf[HALO:HALO + tb, :]
    for lag in range(1, CONV_WIDTH):
        conv = conv + cw_ref[lag:lag + 1, :] * zbuf[HALO - lag:HALO - lag + tb, :]
    yc_ref[...] = b_ref[...] * conv


def _pool_conv(proj, pool_w, pool_scale, conv_w, *, seq, width, tb):
    t = proj.shape[0]
    n_tblk = seq // tb
    ratio = tb // HALO

    def cur(col):
        return pl.BlockSpec((tb, width), lambda g: (g, col))

    def halo(col):
        return pl.BlockSpec((HALO, width), lambda g: (jnp.maximum(g * ratio - 1, 0), col))

    full = lambda shape: pl.BlockSpec(shape, lambda g: (0,) * len(shape))
    return pl.pallas_call(
        functools.partial(_poolconv_kernel, tb=tb, n_tblk=n_tblk),
        grid=(t // tb,),
        in_specs=[cur(0), halo(0), cur(1), halo(1), cur(2), cur(3), halo(3),
                  full(pool_w.shape), full(pool_scale.shape), full(conv_w.shape)],
        out_specs=[pl.BlockSpec((tb, width), lambda g: (g, 0))] * 2,
        out_shape=[jax.ShapeDtypeStruct((t, width), F32)] * 2,
        scratch_shapes=[pltpu.VMEM((HALO + tb, width), F32)] * 2,
        compiler_params=_cp("parallel"),
    )(proj, proj, proj, proj, proj, proj, proj, pool_w, pool_scale, conv_w)


def _cum_rows(x, op, fill):
    n = x.shape[0]
    row = lax.broadcasted_iota(I32, x.shape, 0)
    shift = 1
    while shift < n:
        x = op(x, jnp.where(row >= shift, pltpu.roll(x, shift, axis=0), fill))
        shift *= 2
    return x


def _mlstm_kernel(q_ref, k_ref, v_ref, o_ref, g_ref, gb_ref, nw_ref, y_ref, c_scr, n_scr, m_scr,
                  *, chunk, heads):
    @pl.when(pl.program_id(1) == 0)
    def _():
        c_scr[...] = jnp.zeros_like(c_scr)
        n_scr[...] = jnp.zeros_like(n_scr)
        m_scr[...] = jnp.zeros_like(m_scr)

    nb, tb = q_ref.shape[0], q_ref.shape[1]
    hd = HEAD_DIM
    scale = hd ** -0.5
    tri = (lax.broadcasted_iota(I32, (chunk, chunk), 0) >= lax.broadcasted_iota(I32, (chunk, chunk), 1))
    for c, bb in [(c, bb) for c in range(tb // chunk) for bb in range(nb)]:
        rows = slice(c * chunk, (c + 1) * chunk)
        gates = g_ref[bb, rows, :] + gb_ref[...]
        lf = _log_sigmoid(pltpu.roll(gates, LANES - heads, axis=1))
        cumf = _cum_rows(lf, jnp.add, 0.0)
        a = gates - cumf
        m_prev = m_scr[bb]
        mu = jnp.maximum(_cum_rows(a, jnp.maximum, -jnp.inf), m_prev)
        mu_last = mu[chunk - 1:chunk, :]
        a_t = a.T
        for h in range(heads):
            cols = slice(h * hd, (h + 1) * hd)
            q = q_ref[bb, rows, cols]
            k = k_ref[bb, rows, cols] * scale
            v = v_ref[bb, rows, cols]
            qb, kb, vb = q.astype(BF16), k.astype(BF16), v.astype(BF16)
            mu_col = mu[:, h:h + 1]
            a_col = a[:, h:h + 1]
            m_prev_h = m_prev[:, h:h + 1]
            mu_last_h = mu_last[:, h:h + 1]
            dmat = jnp.exp(jnp.where(tri, a_t[h:h + 1, :] - mu_col, -jnp.inf))
            s = lax.dot_general(qb, kb, (((1,), (1,)), ((), ())), preferred_element_type=F32)
            p = dmat * s
            inter = jnp.exp(m_prev_h - mu_col)
            state = bb * heads + h
            c_h = c_scr[state]
            n_h = n_scr[state]
            num = inter * jnp.dot(qb, c_h.astype(BF16), preferred_element_type=F32) \
                + jnp.dot(p.astype(BF16), vb, preferred_element_type=F32)
            den = inter * jnp.sum(q * n_h, axis=-1, keepdims=True) + jnp.sum(p, axis=-1, keepdims=True)
            floor = jnp.exp(-(cumf[:, h:h + 1] + mu_col))
            h_out = num / jnp.maximum(jnp.abs(den), floor)

            wg = jnp.exp(a_col - mu_last_h)
            decay = jnp.exp(m_prev_h - mu_last_h)
            kw = k * wg
            c_scr[state] = decay * c_h + lax.dot_general(kw.astype(BF16), vb, (((0,), (0,)), ((), ())),
                                                         preferred_element_type=F32)
            n_scr[state] = decay * n_h + jnp.sum(kw, axis=0, keepdims=True)

            gated = _sigmoid(o_ref[bb, rows, cols]) * h_out
            y_ref[bb, rows, cols] = _head_norm(gated, nw_ref[:, cols])
        m_scr[bb] = cumf[chunk - 1:chunk, :] + mu_last


def _mlstm(proj, gates, gate_b, norm_w, *, batch, seq, width, col0):
    t = proj.shape[0]
    heads = width // HEAD_DIM
    nb = SCAN_BATCH if batch % SCAN_BATCH == 0 else 1
    proj3 = proj.reshape(batch, seq, proj.shape[1])
    gates3 = gates.reshape(batch, seq, LANES)
    blk = lambda col: pl.BlockSpec((nb, SCAN_BLOCK, width), lambda b, i: (b, i, col))
    y = pl.pallas_call(
        functools.partial(_mlstm_kernel, chunk=SCAN_CHUNK, heads=heads),
        grid=(batch // nb, seq // SCAN_BLOCK),
        in_specs=[blk(col0), blk(col0 + 1), blk(col0 + 2), blk(col0 + 3),
                  pl.BlockSpec((nb, SCAN_BLOCK, LANES), lambda b, i: (b, i, 0)),
                  pl.BlockSpec((1, LANES), lambda b, i: (0, 0)),
                  pl.BlockSpec((1, width), lambda b, i: (0, 0))],
        out_specs=pl.BlockSpec((nb, SCAN_BLOCK, width), lambda b, i: (b, i, 0)),
        out_shape=jax.ShapeDtypeStruct((batch, seq, width), F32),
        scratch_shapes=[pltpu.VMEM((nb * heads, HEAD_DIM, HEAD_DIM), F32),
                        pltpu.VMEM((nb * heads, 1, HEAD_DIM), F32),
                        pltpu.VMEM((nb, 1, LANES), F32)],
        compiler_params=_cp("parallel", "arbitrary"),
    )(proj3, proj3, proj3, proj3, gates3, gate_b, norm_w)
    return y.reshape(t, width)


def _ret_kernel(q_ref, k_ref, v_ref, g_ref, cos_ref, sin_ref, intra_ref, cross_ref, zeta_ref, nw_ref,
                y_ref, r_scr, *, chunk, heads, chunk_decay):
    @pl.when(pl.program_id(1) == 0)
    def _():
        r_scr[...] = jnp.zeros_like(r_scr)

    tb = q_ref.shape[0]
    hd = HEAD_DIM
    scale = hd ** -0.5
    for c in range(tb // chunk):
        rows = slice(c * chunk, (c + 1) * chunk)
        cos = cos_ref[rows, :]
        sin = sin_ref[rows, :]
        for h in range(heads):
            cols = slice(h * hd, (h + 1) * hd)
            q = q_ref[rows, cols]
            k = k_ref[rows, cols]
            q = q * cos + pltpu.roll(q, hd // 2, axis=1) * sin
            k = (k * cos + pltpu.roll(k, hd // 2, axis=1) * sin) * scale
            vb = v_ref[rows, cols].astype(BF16)
            qb = q.astype(BF16)
            s = lax.dot_general(qb, k.astype(BF16), (((1,), (1,)), ((), ())), preferred_element_type=F32)
            inner = jnp.dot((s * intra_ref[h]).astype(BF16), vb, preferred_element_type=F32)
            r_h = r_scr[h]
            crs = jnp.dot(qb, r_h.astype(BF16), preferred_element_type=F32) * cross_ref[h]
            r_scr[h] = chunk_decay[h] * r_h + lax.dot_general(
                (k * zeta_ref[h]).astype(BF16), vb, (((0,), (0,)), ((), ())), preferred_element_type=F32)
            y_ref[rows, cols] = _silu(g_ref[rows, cols]) * _head_norm(inner + crs, nw_ref[:, cols])


def _retention(proj, norm_w, *, batch, seq, width, col0):
    t = proj.shape[0]
    heads = width // HEAD_DIM
    chunk = SCAN_CHUNK
    n_tblk = seq // SCAN_BLOCK
    half = HEAD_DIM // 2
    inv = np.float32(ROPE_BASE) ** (-np.arange(half, dtype=np.float32) / np.float32(half))
    ang = (np.arange(seq, dtype=np.float32)[:, None] * inv[None, :]).astype(np.float64)
    cos_t = jnp.asarray(np.concatenate([np.cos(ang), np.cos(ang)], axis=-1), F32)
    sin_t = jnp.asarray(np.concatenate([-np.sin(ang), np.sin(ang)], axis=-1), F32)
    log_g = jnp.log(1.0 - 2.0 ** (-5.0 - jnp.arange(heads, dtype=F32)))
    tt = jnp.arange(chunk, dtype=F32)
    lag = tt[:, None] - tt[None, :]
    intra = jnp.where(lag >= 0, jnp.exp(jnp.maximum(lag, 0.0)[None] * log_g[:, None, None]), 0.0)
    cross = jnp.broadcast_to(jnp.exp((tt + 1.0)[None, :] * log_g[:, None])[:, :, None], (heads, chunk, HEAD_DIM))
    zeta = jnp.broadcast_to(jnp.exp((chunk - 1.0 - tt)[None, :] * log_g[:, None])[:, :, None],
                            (heads, chunk, HEAD_DIM))
    chunk_decay = tuple(float((1.0 - 2.0 ** (-5.0 - h)) ** chunk) for h in range(heads))

    blk = lambda col: pl.BlockSpec((SCAN_BLOCK, width), lambda b, i: (b * n_tblk + i, col))
    pos = pl.BlockSpec((SCAN_BLOCK, HEAD_DIM), lambda b, i: (i, 0))
    full3 = lambda a: pl.BlockSpec(a.shape, lambda b, i: (0, 0, 0))
    return pl.pallas_call(
        functools.partial(_ret_kernel, chunk=chunk, heads=heads, chunk_decay=chunk_decay),
        grid=(batch, n_tblk),
        in_specs=[blk(col0), blk(col0 + 1), blk(col0 + 2), blk(col0 + 3), pos, pos,
                  full3(intra), full3(cross), full3(zeta),
                  pl.BlockSpec((1, width), lambda b, i: (0, 0))],
        out_specs=pl.BlockSpec((SCAN_BLOCK, width), lambda b, i: (b * n_tblk + i, 0)),
        out_shape=jax.ShapeDtypeStruct((t, width), F32),
        scratch_shapes=[pltpu.VMEM((heads, HEAD_DIM, HEAD_DIM), F32)],
        compiler_params=_cp("parallel", "arbitrary"),
    )(proj, proj, proj, proj, cos_t, sin_t, intra, cross, zeta, norm_w)


def _merge_kernel(x_ref, *refs):
    y_refs, wg_refs = refs[:N_BRANCH], refs[N_BRANCH:2 * N_BRANCH]
    wb_ref, o_ref, xb_ref = refs[2 * N_BRANCH:]

    @pl.when(pl.program_id(1) == 0)
    def _():
        xb_ref[...] = x_ref[...].astype(BF16)

    xb = xb_ref[...]
    acc = None
    for n in range(N_BRANCH):
        gate = _sigmoid(jnp.dot(xb, wg_refs[n][...], preferred_element_type=F32))
        term = gate * jnp.dot(y_refs[n][...].astype(BF16), wb_ref[n], preferred_element_type=F32)
        acc = term if acc is None else acc + term
    o_ref[...] = acc


def _merge(x, branches, w_gate, gate_col0, w_branch, *, tm, tn):
    t, d = x.shape
    width = branches[0].shape[1]
    ybs = pl.BlockSpec((tm, width), lambda i, j: (i, 0))

    def gate_spec(n):
        return pl.BlockSpec((d, tn), lambda i, j: (0, (gate_col0 + n * d) // tn + j))

    return pl.pallas_call(
        _merge_kernel,
        grid=(t // tm, d // tn),
        in_specs=[pl.BlockSpec((tm, d), lambda i, j: (i, 0))] + [ybs] * N_BRANCH
                 + [gate_spec(n) for n in range(N_BRANCH)]
                 + [pl.BlockSpec((N_BRANCH, width, tn), lambda i, j: (0, 0, j))],
        out_specs=pl.BlockSpec((tm, tn), lambda i, j: (i, j)),
        out_shape=jax.ShapeDtypeStruct((t, d), F32),
        scratch_shapes=[pltpu.VMEM((tm, d), BF16)],
        compiler_params=_cp("parallel", "arbitrary"),
    )(x, *branches, *([w_gate] * N_BRANCH), w_branch)


def _proj_ln_kernel(a_ref, w_ref, r_ref, lw_ref, lb_ref, o_ref, *, alpha):
    y = jnp.dot(a_ref[...].astype(BF16), w_ref[...], preferred_element_type=F32)
    o_ref[...] = _layer_norm(alpha * r_ref[...] + y, lw_ref[...], lb_ref[...])


def _proj_ln(a, w, resid, ln_w, ln_b, *, alpha, tm):
    t, k = a.shape
    d = w.shape[1]
    row = lambda n: pl.BlockSpec((tm, n), lambda i: (i, 0))
    const = lambda shape: pl.BlockSpec(shape, lambda i: (0, 0))
    return pl.pallas_call(
        functools.partial(_proj_ln_kernel, alpha=alpha),
        grid=(t // tm,),
        in_specs=[row(k), const((k, d)), row(d), const((1, d)), const((1, d))],
        out_specs=row(d),
        out_shape=jax.ShapeDtypeStruct((t, d), F32),
        compiler_params=_cp("parallel"),
    )(a, w, resid, ln_w, ln_b)


def _xattn_kernel(x_ref, wq_ref, kv_ref, wo_ref, lw_ref, lb_ref, o_ref, *, alpha, heads):
    x = x_ref[...]
    hd = HEAD_DIM
    inner = heads * hd
    q = jnp.dot(x.astype(BF16), wq_ref[...], preferred_element_type=F32)
    outs = []
    for h in range(heads):
        qh = q[:, h * hd:(h + 1) * hd].astype(BF16)
        kh = kv_ref[:, h * hd:(h + 1) * hd].astype(BF16)
        vh = kv_ref[:, inner + h * hd:inner + (h + 1) * hd].astype(BF16)
        s = lax.dot_general(qh, kh, (((1,), (1,)), ((), ())), preferred_element_type=F32) * hd ** -0.5
        s = s - jnp.max(s, axis=-1, keepdims=True)
        e = jnp.exp(s)
        p = e / jnp.sum(e, axis=-1, keepdims=True)
        outs.append(jnp.dot(p.astype(BF16), vh, preferred_element_type=F32).astype(BF16))
    o = jnp.concatenate(outs, axis=-1)
    y = jnp.dot(o, wo_ref[...], preferred_element_type=F32)
    o_ref[...] = _layer_norm(alpha * x + y, lw_ref[...], lb_ref[...])


def _xattn(x, kv, wq, wo, ln_w, ln_b, *, alpha, seq, mem_len, tm):
    t, d = x.shape
    inner = wq.shape[1]
    n_tblk = seq // tm
    const = lambda shape: pl.BlockSpec(shape, lambda i: (0, 0))
    return pl.pallas_call(
        functools.partial(_xattn_kernel, alpha=alpha, heads=XATTN_HEADS),
        grid=(t // tm,),
        in_specs=[pl.BlockSpec((tm, d), lambda i: (i, 0)), const((d, inner)),
                  pl.BlockSpec((mem_len, 2 * inner), lambda i: (i // n_tblk, 0)),
                  const((inner, d)), const((1, d)), const((1, d))],
        out_specs=pl.BlockSpec((tm, d), lambda i: (i, 0)),
        out_shape=jax.ShapeDtypeStruct((t, d), F32),
        compiler_params=_cp("parallel"),
    )(x, wq, kv, wo, ln_w, ln_b)


def _round_up_pow2(x, m):
    shift = m.bit_length() - 1
    return jnp.left_shift(jnp.right_shift(x + (m - 1), shift), shift)


def _route_kernel(x_ref, wr_ref, rb_ref, w_ref, lrow_ref, segtab_ref, blk_ref, seg_ref, size_all, tot, *, tm, bm):
    step = pl.program_id(0)

    @pl.when(step == 0)
    def _():
        tot[...] = jnp.zeros_like(tot)

    e_n, g_n = N_EXPERTS, N_GROUPS
    per = e_n // g_n
    def split(a):
        hi = a.astype(BF16)
        return hi, (a - hi.astype(F32)).astype(BF16)

    def nt(a, b):
        return lax.dot_general(a, b, (((1,), (1,)), ((), ())), preferred_element_type=F32)

    (w_hi, w_lo), (x_hi, x_lo) = split(wr_ref[...]), split(x_ref[...])
    logits = nt(w_hi, x_hi) + (nt(w_hi, x_lo) + nt(w_lo, x_hi))
    scores = _sigmoid(logits)
    biased = scores + rb_ref[...]
    b3 = biased.reshape(g_n, per, tm)
    member = lax.broadcasted_iota(I32, (g_n, per, tm), 1)
    top1 = jnp.max(b3, axis=1, keepdims=True)
    first = jnp.min(jnp.where(b3 == top1, member, per), axis=1, keepdims=True)
    top2 = jnp.max(jnp.where(member == first, -jnp.inf, b3), axis=1, keepdims=True)
    gs = top1 + top2
    gid = lax.broadcasted_iota(I32, (g_n, 1, tm), 0)
    rank = jnp.zeros((g_n, 1, tm), I32)
    for other in range(g_n):
        o = gs[other:other + 1]
        ahead = jnp.logical_or(o > gs, jnp.logical_and(o == gs, other < gid))
        rank = rank + jnp.where(ahead, 1, 0)
    cur = jnp.where(rank < TOPK_GROUPS, b3, -jnp.inf).reshape(e_n, tm)

    eid = lax.broadcasted_iota(I32, (e_n, tm), 0)
    picks, vals = [], []
    sel = jnp.zeros((e_n, tm), F32)
    for k in range(TOP_K):
        mx = jnp.max(cur, axis=0, keepdims=True)
        ik = jnp.min(jnp.where(cur == mx, eid, e_n), axis=0, keepdims=True)
        hit = eid == ik
        vals.append(jnp.sum(jnp.where(hit, scores, 0.0), axis=0, keepdims=True))
        cur = jnp.where(hit, -jnp.inf, cur)
        sel = jnp.where(hit, 1.0, sel)
        picks.append(ik)
    total = vals[0]
    for v in vals[1:]:
        total = total + v

    tri = jnp.where(lax.broadcasted_iota(I32, (tm, tm), 0) <= lax.broadcasted_iota(I32, (tm, tm), 1), 1.0, 0.0)
    incl = jnp.dot(sel.astype(BF16), tri.astype(BF16), preferred_element_type=F32)
    size = _round_up_pow2(jnp.broadcast_to(incl[:, tm - 1:tm], (e_n, LANES)).astype(I32), SEG_ALIGN)
    loff = _cum_rows(size, jnp.add, 0) - size
    base = loff[:, 0:1].astype(F32) + incl - 1.0
    for k in range(TOP_K):
        w_ref[k:k + 1, :] = vals[k] / total * ROUTE_SCALE
        lrow_ref[0, k:k + 1, :] = jnp.sum(jnp.where(eid == picks[k], base, 0.0),
                                          axis=0, keepdims=True).astype(I32)
    size_all[step] = size
    tot[...] = tot[...] + size

    @pl.when(step == pl.num_programs(0) - 1)
    def _():
        rows = tot[...]
        pcnt = _round_up_pow2(rows, bm)
        pend = _cum_rows(pcnt, jnp.add, 0)

        def tile_seg(i, run):
            segtab_ref[i, 0] = run
            segtab_ref[i, 1] = size_all[i]
            return run + size_all[i]

        lax.fori_loop(0, pl.num_programs(0), tile_seg, pend - pcnt)

        nb = blk_ref.shape[2]
        row0 = lax.broadcasted_iota(I32, (e_n, nb), 1) * bm
        total_rows = pend[e_n - 1:e_n, 0:1]
        owner = jnp.sum(jnp.where(pend[:, 0:1] <= row0, 1, 0), axis=0, keepdims=True)
        last_owner = jnp.sum(jnp.where(pend[:, 0:1] < total_rows, 1, 0), axis=0, keepdims=True)
        blk_ref[0] = jnp.where(row0[0:1, :] < total_rows, jnp.minimum(owner, e_n - 1), last_owner)
        data_end = pend - pcnt + rows
        inside = jnp.logical_and(pend[:, 0:1] - pcnt[:, 0:1] <= row0, row0 < pend[:, 0:1])
        blk_ref[1] = jnp.sum(jnp.where(inside, jnp.clip(data_end[:, 0:1] - row0, 0, bm), 0), axis=0, keepdims=True)
        seg_ref[0] = pend
        seg_ref[1] = pcnt


def _route(x, router_w_t, router_b, *, tm, bm, n_blocks):
    t, d = x.shape
    e_n = N_EXPERTS
    n_t = t // tm
    assert bm & (bm - 1) == 0
    nb_pad = -(-n_blocks // LANES) * LANES
    return pl.pallas_call(
        functools.partial(_route_kernel, tm=tm, bm=bm),
        grid=(n_t,),
        in_specs=[pl.BlockSpec((tm, d), lambda i: (i, 0)),
                  pl.BlockSpec((e_n, d), lambda i: (0, 0)),
                  pl.BlockSpec((e_n, 1), lambda i: (0, 0))],
        out_specs=[pl.BlockSpec((TOP_K, tm), lambda i: (0, i)),
                   pl.BlockSpec((1, TOP_K, tm), lambda i: (i, 0, 0)),
                   pl.BlockSpec((n_t, 2, e_n, LANES), lambda i: (0, 0, 0, 0)),
                   pl.BlockSpec((2, 1, nb_pad), lambda i: (0, 0, 0)),
                   pl.BlockSpec((2, e_n, LANES), lambda i: (0, 0, 0))],
        out_shape=[jax.ShapeDtypeStruct((TOP_K, t), F32), jax.ShapeDtypeStruct((n_t, TOP_K, tm), I32),
                   jax.ShapeDtypeStruct((n_t, 2, e_n, LANES), I32), jax.ShapeDtypeStruct((2, 1, nb_pad), I32),
                   jax.ShapeDtypeStruct((2, e_n, LANES), I32)],
        scratch_shapes=[pltpu.VMEM((n_t, e_n, LANES), I32), pltpu.VMEM((e_n, LANES), I32)],
        compiler_params=_cp("arbitrary"),
    )(x, router_w_t, router_b)


def _tile_rows(tm):
    worst = TOP_K * tm + N_EXPERTS * (SEG_ALIGN - 1)
    return -(-worst // ONEHOT_ROWS) * ONEHOT_ROWS


def _onehot_rows(chunk, lrow, values, tm):
    rid = chunk * ONEHOT_ROWS + lax.broadcasted_iota(I32, (ONEHOT_ROWS, tm), 0)
    acc = jnp.zeros((ONEHOT_ROWS, tm), F32)
    for k in range(TOP_K):
        acc = jnp.where(rid == lrow[k:k + 1, :], 1.0 if values is None else values[k:k + 1, :], acc)
    return acc.astype(BF16)


def _start_segments(gstart_ref, size_ref, tile, make_copy, first=0, last=N_EXPERTS, row0=0):
    def body(e, loff):
        n = size_ref[tile * N_EXPERTS + e]

        @pl.when(n > 0)
        def _():
            make_copy(pl.multiple_of(gstart_ref[tile * N_EXPERTS + e], SEG_ALIGN),
                      pl.multiple_of(loff, SEG_ALIGN), pl.multiple_of(n, SEG_ALIGN)).start()
        return loff + n
    return lax.fori_loop(first, last, body, row0)


def _segments_within(size_ref, tile, limit):
    def body(e, carry):
        count, rows = carry
        end = rows + size_ref[tile * N_EXPERTS + e]
        fits = jnp.logical_and(count == e, end <= limit)
        return count + jnp.where(fits, 1, 0), jnp.where(fits, end, rows)
    return lax.fori_loop(0, N_EXPERTS, body, (0, 0))


def _tile_total(size_ref, tile):
    return lax.fori_loop(0, N_EXPERTS, lambda e, s: s + size_ref[tile * N_EXPERTS + e], 0)


def _wait_rows(make_copy, rows):
    @pl.when(rows > 0)
    def _():
        make_copy(0, 0, pl.multiple_of(rows, SEG_ALIGN)).wait()


def _dispatch_kernel(gstart_ref, size_ref, x_ref, lrow_ref, xs_ref, stage, inflight, sem_a, sem_b, *, tm):
    i = pl.program_id(0)
    last = pl.num_programs(0) - 1
    n_chunks = stage.shape[0] // ONEHOT_ROWS
    split_chunk = n_chunks // 2
    split = split_chunk * ONEHOT_ROWS
    assert tm <= ONEHOT_ROWS

    def copy_on(sem):
        return lambda g, loff, n: pltpu.make_async_copy(stage.at[pl.ds(loff, n), :], xs_ref.at[pl.ds(g, n), :], sem)

    xb = x_ref[...].astype(BF16)
    lrow = lrow_ref[0]
    n_rows = _tile_total(size_ref, i)

    def chunks(lo, hi):
        for c in range(lo, hi):
            def one(c=c):
                stage[c * ONEHOT_ROWS:(c + 1) * ONEHOT_ROWS, :] = jnp.dot(
                    _onehot_rows(c, lrow, None, tm), xb, preferred_element_type=F32).astype(BF16)
            if c * ONEHOT_ROWS < TOP_K * tm:
                one()
            else:
                pl.when(c * ONEHOT_ROWS < n_rows)(one)

    @pl.when(i == 0)
    def _():
        inflight[0] = 0
        inflight[1] = 0

    _wait_rows(copy_on(sem_a), inflight[0])
    chunks(0, split_chunk - 1)
    _wait_rows(copy_on(sem_b), inflight[1])
    chunks(split_chunk - 1, split_chunk)
    experts_a, rows_a = _segments_within(size_ref, i, split)
    _start_segments(gstart_ref, size_ref, i, copy_on(sem_a), 0, experts_a)
    chunks(split_chunk, n_chunks)
    rows_b = _start_segments(gstart_ref, size_ref, i, copy_on(sem_b), experts_a, N_EXPERTS, rows_a) - rows_a
    inflight[0] = rows_a
    inflight[1] = rows_b

    @pl.when(i == last)
    def _():
        _wait_rows(copy_on(sem_a), rows_a)
        _wait_rows(copy_on(sem_b), rows_b)


def _dispatch(x, lrow, gstart, size, *, rows, tm):
    t, d = x.shape
    return pl.pallas_call(
        functools.partial(_dispatch_kernel, tm=tm),
        grid_spec=pltpu.PrefetchScalarGridSpec(
            num_scalar_prefetch=2,
            grid=(t // tm,),
            in_specs=[pl.BlockSpec((tm, d), lambda i, *_: (i, 0)),
                      pl.BlockSpec((1, TOP_K, tm), lambda i, *_: (i, 0, 0))],
            out_specs=pl.BlockSpec(memory_space=pl.ANY),
            scratch_shapes=[pltpu.VMEM((_tile_rows(tm), d), BF16), pltpu.SMEM((2,), I32)]
                           + [pltpu.SemaphoreType.DMA(())] * 2,
        ),
        out_shape=jax.ShapeDtypeStruct((rows, d), BF16),
        compiler_params=_cp("arbitrary"),
    )(gstart, size, x, lrow)


def _expert_block_kernel(blk_ref, nused_ref, xs_ref, wgu_ref, wdn_ref, ys_ref, wgu_b, wdn_b, *, n_blocks, bm):
    j = pl.program_id(0)
    valid = blk_ref[n_blocks + j]
    changed = jnp.logical_or(j == 0, blk_ref[j] != blk_ref[jnp.maximum(j - 1, 0)])

    @pl.when(jnp.logical_and(valid > 0, changed))
    def _():
        wgu_b[...] = wgu_ref[...].astype(BF16)
        wdn_b[...] = wdn_ref[...].astype(BF16)

    def swiglu(rows, masked):
        x = xs_ref[0:rows, :]
        if masked:
            x = jnp.where(lax.broadcasted_iota(I32, x.shape, 0) < valid, x, jnp.zeros_like(x))
        f = wdn_b.shape[0]
        gu = jnp.dot(x, wgu_b[...], preferred_element_type=F32)
        hidden = (_silu(gu[:, :f]) * gu[:, f:]).astype(BF16)
        ys_ref[0:rows, :] = jnp.dot(hidden, wdn_b[...], preferred_element_type=F32).astype(BF16)

    @pl.when(valid == bm)
    def _():
        swiglu(bm, False)

    quarter = bm // 4
    for q in range(1, 5):
        @pl.when(jnp.logical_and(jnp.logical_and(valid > (q - 1) * quarter, valid <= q * quarter), valid < bm))
        def _():
            swiglu(q * quarter, True)


def _experts_blocked(xs, blk, nused, w_gu, w_dn, layer, *, n_blocks, bm):
    rows, d = xs.shape
    f2 = w_gu.shape[3]
    f = w_dn.shape[2]
    row_blk = lambda j, bl, nu: (jnp.minimum(j, nu[0] - 1), 0)
    return pl.pallas_call(
        functools.partial(_expert_block_kernel, n_blocks=n_blocks, bm=bm),
        grid_spec=pltpu.PrefetchScalarGridSpec(
            num_scalar_prefetch=2,
            grid=(n_blocks,),
            in_specs=[pl.BlockSpec((bm, d), row_blk),
                      pl.BlockSpec((None, None, d, f2), lambda j, bl, nu: (layer, bl[j], 0, 0)),
                      pl.BlockSpec((None, None, f, d), lambda j, bl, nu: (layer, bl[j], 0, 0))],
            out_specs=pl.BlockSpec((bm, d), row_blk),
            scratch_shapes=[pltpu.VMEM((d, f2), BF16), pltpu.VMEM((f, d), BF16)],
        ),
        out_shape=jax.ShapeDtypeStruct((rows, d), BF16),
        compiler_params=_cp("arbitrary"),
    )(blk, nused, xs, w_gu, w_dn)


def _combine_tile_kernel(gstart_ref, size_ref, x_ref, lrow_ref, w_ref, sdn_ref, lw_ref, lb_ref, sgu_hbm, ys_ref,
                         o_ref, ybuf, sgu, sems, wsem, *, tm, alpha):
    i = pl.program_id(0)
    last_tile = pl.num_programs(0) - 1
    n_chunks = ybuf.shape[1] // ONEHOT_ROWS
    sure_chunks = (TOP_K * tm) // ONEHOT_ROWS

    def contract_rows(w_rows, y_rows):
        return lax.dot_general(w_rows, y_rows, (((0,), (0,)), ((), ())), preferred_element_type=F32)

    def copy_to(slot):
        return lambda g, loff, n: pltpu.make_async_copy(ys_ref.at[pl.ds(g, n), :], ybuf.at[slot, pl.ds(loff, n), :],
                                                        sems.at[slot])

    def fetch(tile, slot):
        _start_segments(gstart_ref, size_ref, tile, copy_to(slot))

    n_rows = _tile_total(size_ref, i)

    @pl.when(i == 0)
    def _():
        ybuf[...] = jnp.zeros_like(ybuf)
        weights = pltpu.make_async_copy(sgu_hbm, sgu, wsem)
        weights.start()
        weights.wait()
        fetch(0, 0)

    for slot in range(2):
        @pl.when(i % 2 == slot)
        def _():
            @pl.when(i < last_tile)
            def _():
                fetch(i + 1, 1 - slot)

            x = x_ref[...]
            f = sdn_ref.shape[0]
            gu = jnp.dot(x.astype(BF16), sgu[...], preferred_element_type=F32)
            hidden = (_silu(gu[:, :f]) * gu[:, f:]).astype(BF16)
            acc = jnp.dot(hidden, sdn_ref[...], preferred_element_type=F32)
            _wait_rows(copy_to(slot), n_rows)
            lrow, w = lrow_ref[0], w_ref[...]
            for c in range(sure_chunks):
                rows = slice(c * ONEHOT_ROWS, (c + 1) * ONEHOT_ROWS)
                acc = acc + contract_rows(_onehot_rows(c, lrow, w, tm), ybuf[slot, rows, :])
            o_ref[...] = acc
            for c in range(sure_chunks, n_chunks):
                @pl.when(c * ONEHOT_ROWS < n_rows)
                def _():
                    rows = slice(c * ONEHOT_ROWS, (c + 1) * ONEHOT_ROWS)
                    o_ref[...] += contract_rows(_onehot_rows(c, lrow, w, tm), ybuf[slot, rows, :])
            o_ref[...] = _layer_norm(alpha * x + o_ref[...], lw_ref[...], lb_ref[...])


def _combine(x, ys, lrow, wts, gstart, size, s_gu, s_dn, ln_w, ln_b, *, alpha, tm):
    t, d = x.shape
    const = lambda shape: pl.BlockSpec(shape, lambda i, *_: (0, 0))
    return pl.pallas_call(
        functools.partial(_combine_tile_kernel, tm=tm, alpha=alpha),
        grid_spec=pltpu.PrefetchScalarGridSpec(
            num_scalar_prefetch=2,
            grid=(t // tm,),
            in_specs=[pl.BlockSpec((tm, d), lambda i, *_: (i, 0)),
                      pl.BlockSpec((1, TOP_K, tm), lambda i, *_: (i, 0, 0)),
                      pl.BlockSpec((TOP_K, tm), lambda i, *_: (0, i)),
                      const(s_dn.shape), const((1, d)), const((1, d)),
                      pl.BlockSpec(memory_space=pl.ANY), pl.BlockSpec(memory_space=pl.ANY)],
            out_specs=pl.BlockSpec((tm, d), lambda i, *_: (i, 0)),
            scratch_shapes=[pltpu.VMEM((2, _tile_rows(tm), d), BF16), pltpu.VMEM(s_gu.shape, BF16),
                            pltpu.SemaphoreType.DMA((2,)), pltpu.SemaphoreType.DMA(())],
        ),
        out_shape=jax.ShapeDtypeStruct((t, d), F32),
        compiler_params=_cp("arbitrary"),
    )(gstart, size, x, lrow, wts, s_dn, ln_w, ln_b, s_gu, ys)


def _mixer_sublayer(x, w_in_all, layer, gate_b, pool_w, pool_scale, conv_w, mlstm_norm_w, ret_norm_w, w_branch,
                    w_out, ln_w, ln_b, *, batch, seq, alpha):
    t, d = x.shape
    width = d // N_BRANCH
    heads = width // HEAD_DIM
    gate_off = 8 * width
    ret_off = gate_off
    g_off = ret_off + 4 * width
    if_off = g_off + N_BRANCH * d
    w_bf16 = _realign_cast(w_in_all, layer, lo_col=gate_off, hi_col=if_off, shift=2 * heads, tr=512, tn=512)
    gate_bias = jnp.pad(gate_b, (0, LANES - 2 * heads)).reshape(1, LANES)

    proj = _matmul(x, w_bf16, tm=1024, tn=1024, ncols=g_off)
    gates = _matmul(x, w_bf16, tm=1024, tn=LANES, ncols=LANES, col0=if_off)
    y_pool, y_conv = _pool_conv(proj, pool_w.astype(BF16), pool_scale.reshape(1, width), conv_w,
                                seq=seq, width=width, tb=512)
    y_mlstm = _mlstm(proj, gates, gate_bias, mlstm_norm_w.reshape(1, width),
                     batch=batch, seq=seq, width=width, col0=4)
    y_ret = _retention(proj, ret_norm_w.reshape(1, width), batch=batch, seq=seq, width=width,
                       col0=ret_off // width)
    merged = _merge(x, (y_pool, y_conv, y_mlstm, y_ret), w_bf16, g_off, w_branch.astype(BF16),
                    tm=512, tn=512)
    return _proj_ln(merged, w_out.astype(BF16), x, ln_w, ln_b, alpha=alpha, tm=512)


def _xattn_sublayer(x, mem2d, wq, wk, wv, wo, ln_w, ln_b, *, seq, mem_len, alpha):
    w_kv = jnp.concatenate([wk, wv], axis=1).astype(BF16)
    kv = _matmul(mem2d, w_kv, tm=min(mem2d.shape[0], 1024), tn=512)
    return _xattn(x, kv, wq.astype(BF16), wo.astype(BF16), ln_w, ln_b,
                  alpha=alpha, seq=seq, mem_len=mem_len, tm=512)


def _moe_sublayer(x, router_w, router_b, w_gu, w_dn, layer, s_gu, s_dn, ln_w, ln_b, *, alpha):
    t, d = x.shape
    e_n, bm = N_EXPERTS, MOE_BM
    tm = ROUTE_TM
    n_blocks = -(-(t * TOP_K + (t // tm) * e_n * (SEG_ALIGN - 1)) // bm) + e_n
    wts, lrow, segtab, blk, seg = _route(x, router_w.T, router_b.reshape(e_n, 1), tm=tm, bm=bm, n_blocks=n_blocks)
    gstart, size = segtab[:, 0, :, 0].reshape(-1), segtab[:, 1, :, 0].reshape(-1)
    nused = seg[0, e_n - 1, 0] // bm

    xs = _dispatch(x, lrow, gstart, size, rows=n_blocks * bm, tm=tm)
    ys = _experts_blocked(xs, blk[:, 0, :n_blocks].reshape(-1), nused.reshape(1), w_gu, w_dn, layer,
                          n_blocks=n_blocks, bm=bm)
    return _combine(x, ys, lrow, wts, gstart, size, s_gu.astype(BF16), s_dn.astype(BF16), ln_w, ln_b,
                    alpha=alpha, tm=tm)


def kernel(x, mem, w_in, mlstm_gate_b, pool_w, pool_scale, conv_w, mlstm_norm_w, ret_norm_w, w_branch,
           w_mix_out, xa_wq, xa_wk, xa_wv, xa_wo, router_w, router_b, moe_w_gu, moe_w_dn, shared_w_gu,
           shared_w_dn, ln_w, ln_b):
    batch, seq, d = x.shape
    depth = w_in.shape[0]
    mem_len = mem.shape[1]
    alpha = (2 * depth) ** 0.25
    h = x.reshape(batch * seq, d)
    mem2d = mem.reshape(batch * mem_len, d)
    for l in range(depth):
        lw = ln_w[l].reshape(3, 1, d)
        lb = ln_b[l].reshape(3, 1, d)
        h = _mixer_sublayer(h, w_in, l, mlstm_gate_b[l], pool_w[l], pool_scale[l], conv_w[l], mlstm_norm_w[l],
                            ret_norm_w[l], w_branch[l], w_mix_out[l], lw[0], lb[0],
                            batch=batch, seq=seq, alpha=alpha)
        h = _xattn_sublayer(h, mem2d, xa_wq[l], xa_wk[l], xa_wv[l], xa_wo[l], lw[1], lb[1],
                            seq=seq, mem_len=mem_len, alpha=alpha)
        h = _moe_sublayer(h, router_w[l], router_b[l], moe_w_gu, moe_w_dn, l, shared_w_gu[l], shared_w_dn[l],
                          lw[2], lb[2], alpha=alpha)
    return h.reshape(batch, seq, d)
```

```python
import functools

import numpy as np
import jax
import jax.numpy as jnp
from jax import lax
from jax.experimental import pallas as pl
from jax.experimental.pallas import tpu as pltpu

F32 = jnp.float32
BF16 = jnp.bfloat16
I32 = jnp.int32

N_BRANCH = 4
HEAD_DIM = 128
POOL_WINDOWS = (2, 4, 8, 16)
CONV_WIDTH = 3
ROPE_BASE = 10000.0
XATTN_HEADS = 4
N_EXPERTS = 64
TOP_K = 8
N_GROUPS = 8
TOPK_GROUPS = 4
ROUTE_SCALE = 2.5
LN_EPS = 1e-5

LANES = 128
V7X_VMEM_BYTES = 64 * 1024 * 1024
VMEM_LIMIT = 56 * 1024 * 1024

SCAN_CHUNK = 256
SCAN_BLOCK = 512
SCAN_BATCH = 2
HALO = 16
MOE_BM = 1024
ROUTE_TM = 256
SEG_ALIGN = 8
ONEHOT_ROWS = 512

def _cp(*sem):
    return pltpu.CompilerParams(dimension_semantics=sem, vmem_limit_bytes=VMEM_LIMIT)


def _sigmoid(x):
    return 1.0 / (1.0 + jnp.exp(-x))


def _silu(x):
    return x * _sigmoid(x)


def _log_sigmoid(x):
    return jnp.minimum(x, 0.0) - jnp.log(1.0 + jnp.exp(-jnp.abs(x)))


def _layer_norm(z, w, b):
    mu = jnp.mean(z, axis=-1, keepdims=True)
    d = z - mu
    var = jnp.mean(d * d, axis=-1, keepdims=True)
    return d * lax.rsqrt(var + LN_EPS) * w + b


def _head_norm(h, w):
    mu = jnp.mean(h, axis=-1, keepdims=True)
    d = h - mu
    var = jnp.mean(d * d, axis=-1, keepdims=True)
    return d * lax.rsqrt(var + LN_EPS) * w


def _mm_kernel(x_ref, w_ref, o_ref, xb_ref):
    @pl.when(pl.program_id(1) == 0)
    def _():
        xb_ref[...] = x_ref[...].astype(BF16)

    o_ref[...] = jnp.dot(xb_ref[...], w_ref[...], preferred_element_type=F32)


def _realign_cast_kernel(a_ref, b_ref, o_ref, *, shift, lo, hi):
    j = pl.program_id(1)
    tn = o_ref.shape[1]
    shifted = jnp.logical_and(j >= lo, j < hi)

    @pl.when(shifted)
    def _():
        both = jnp.concatenate([a_ref[...], b_ref[...]], axis=1)
        o_ref[...] = both[:, shift:shift + tn].astype(BF16)

    @pl.when(jnp.logical_not(shifted))
    def _():
        o_ref[...] = a_ref[...].astype(BF16)


def _realign_cast(w_all, layer, *, lo_col, hi_col, shift, tr, tn):
    _, rows, _ = w_all.shape
    lo, hi = lo_col // tn, hi_col // tn
    src = lambda j: jnp.where(j == hi, lo, j)
    return pl.pallas_call(
        functools.partial(_realign_cast_kernel, shift=shift, lo=lo, hi=hi),
        grid=(rows // tr, hi + 1),
        in_specs=[pl.BlockSpec((None, tr, tn), lambda i, j: (layer, i, src(j))),
                  pl.BlockSpec((None, tr, LANES), lambda i, j: (layer, i, (src(j) + 1) * (tn // LANES)))],
        out_specs=pl.BlockSpec((tr, tn), lambda i, j: (i, j)),
        out_shape=jax.ShapeDtypeStruct((rows, hi_col + tn), BF16),
        compiler_params=_cp("parallel", "parallel"),
    )(w_all, w_all)


def _matmul(x, w, *, tm, tn, ncols=None, col0=0):
    t, k = x.shape
    n = w.shape[1] if ncols is None else ncols
    return pl.pallas_call(
        _mm_kernel,
        grid=(t // tm, n // tn),
        in_specs=[pl.BlockSpec((tm, k), lambda i, j: (i, 0)),
                  pl.BlockSpec((k, tn), lambda i, j: (0, col0 // tn + j))],
        out_specs=pl.BlockSpec((tm, tn), lambda i, j: (i, j)),
        out_shape=jax.ShapeDtypeStruct((t, n), F32),
        scratch_shapes=[pltpu.VMEM((tm, k), BF16)],
        compiler_params=_cp("parallel", "arbitrary"),
    )(x, w)


def _poolconv_kernel(u_ref, uh_ref, h_ref, hh_ref, b_ref, c_ref, ch_ref, pw_ref, ps_ref, cw_ref,
                     yp_ref, yc_ref, ubuf, zbuf, *, tb, n_tblk):
    first = (pl.program_id(0) % n_tblk) == 0
    ubuf[0:HALO, :] = jnp.where(first, 0.0, uh_ref[...])
    ubuf[HALO:HALO + tb, :] = u_ref[...]
    zbuf[0:HALO, :] = jnp.where(first, 0.0, ch_ref[...] * hh_ref[...])
    zbuf[HALO:HALO + tb, :] = c_ref[...] * h_ref[...]

    t_pos = (pl.program_id(0) % n_tblk) * tb + lax.broadcasted_iota(I32, (tb, LANES), 0)
    gw = u_ref.shape[1] // len(POOL_WINDOWS)
    for grp, win in enumerate(POOL_WINDOWS):
        lanes = slice(grp * gw, (grp + 1) * gw)
        cur = ubuf[HALO:HALO + tb, lanes]
        acc = cur
        for lag in range(1, win):
            acc = acc + ubuf[HALO - lag:HALO - lag + tb, lanes]
        count = jnp.minimum(t_pos + 1, win).astype(F32)
        mixed = acc / count - cur
        y = jnp.dot(mixed.astype(BF16), pw_ref[grp], preferred_element_type=F32)
        yp_ref[:, lanes] = y * ps_ref[:, lanes]

    conv = cw_ref[0:1, :] * zbuf[HALO:HALO + tb, :]
    for lag in range(1, CONV_WIDTH):
        conv = conv + cw_ref[lag:lag + 1, :] * zbuf[HALO - lag:HALO - lag + tb, :]
    yc_ref[...] = b_ref[...] * conv


def _pool_conv(proj, pool_w, pool_scale, conv_w, *, seq, width, tb):
    t = proj.shape[0]
    n_tblk = seq // tb
    ratio = tb // HALO

    def cur(col):
        return pl.BlockSpec((tb, width), lambda g: (g, col))

    def halo(col):
        return pl.BlockSpec((HALO, width), lambda g: (jnp.maximum(g * ratio - 1, 0), col))

    full = lambda shape: pl.BlockSpec(shape, lambda g: (0,) * len(shape))
    return pl.pallas_call(
        functools.partial(_poolconv_kernel, tb=tb, n_tblk=n_tblk),
        grid=(t // tb,),
        in_specs=[cur(0), halo(0), cur(1), halo(1), cur(2), cur(3), halo(3),
                  full(pool_w.shape), full(pool_scale.shape), full(conv_w.shape)],
        out_specs=[pl.BlockSpec((tb, width), lambda g: (g, 0))] * 2,
        out_shape=[jax.ShapeDtypeStruct((t, width), F32)] * 2,
        scratch_shapes=[pltpu.VMEM((HALO + tb, width), F32)] * 2,
        compiler_params=_cp("parallel"),
    )(proj, proj, proj, proj, proj, proj, proj, pool_w, pool_scale, conv_w)


def _cum_rows(x, op, fill):
    n = x.shape[0]
    row = lax.broadcasted_iota(I32, x.shape, 0)
    shift = 1
    while shift < n:
        x = op(x, jnp.where(row >= shift, pltpu.roll(x, shift, axis=0), fill))
        shift *= 2
    return x


def _mlstm_kernel(q_ref, k_ref, v_ref, o_ref, g_ref, gb_ref, nw_ref, y_ref, c_scr, n_scr, m_scr,
                  *, chunk, heads):
    @pl.when(pl.program_id(1) == 0)
    def _():
        c_scr[...] = jnp.zeros_like(c_scr)
        n_scr[...] = jnp.zeros_like(n_scr)
        m_scr[...] = jnp.zeros_like(m_scr)

    nb, tb = q_ref.shape[0], q_ref.shape[1]
    hd = HEAD_DIM
    scale = hd ** -0.5
    tri = (lax.broadcasted_iota(I32, (chunk, chunk), 0) >= lax.broadcasted_iota(I32, (chunk, chunk), 1))
    for c, bb in [(c, bb) for c in range(tb // chunk) for bb in range(nb)]:
        rows = slice(c * chunk, (c + 1) * chunk)
        gates = g_ref[bb, rows, :] + gb_ref[...]
        lf = _log_sigmoid(pltpu.roll(gates, LANES - heads, axis=1))
        cumf = _cum_rows(lf, jnp.add, 0.0)
        a = gates - cumf
        m_prev = m_scr[bb]
        mu = jnp.maximum(_cum_rows(a, jnp.maximum, -jnp.inf), m_prev)
        mu_last = mu[chunk - 1:chunk, :]
        a_t = a.T
        for h in range(heads):
            cols = slice(h * hd, (h + 1) * hd)
            q = q_ref[bb, rows, cols]
            k = k_ref[bb, rows, cols] * scale
            v = v_ref[bb, rows, cols]
            qb, kb, vb = q.astype(BF16), k.astype(BF16), v.astype(BF16)
            mu_col = mu[:, h:h + 1]
            a_col = a[:, h:h + 1]
            m_prev_h = m_prev[:, h:h + 1]
            mu_last_h = mu_last[:, h:h + 1]
            dmat = jnp.exp(jnp.where(tri, a_t[h:h + 1, :] - mu_col, -jnp.inf))
            s = lax.dot_general(qb, kb, (((1,), (1,)), ((), ())), preferred_element_type=F32)
            p = dmat * s
            inter = jnp.exp(m_prev_h - mu_col)
            state = bb * heads + h
            c_h = c_scr[state]
            n_h = n_scr[state]
            num = inter * jnp.dot(qb, c_h.astype(BF16), preferred_element_type=F32) \
                + jnp.dot(p.astype(BF16), vb, preferred_element_type=F32)
            den = inter * jnp.sum(q * n_h, axis=-1, keepdims=True) + jnp.sum(p, axis=-1, keepdims=True)
            floor = jnp.exp(-(cumf[:, h:h + 1] + mu_col))
            h_out = num / jnp.maximum(jnp.abs(den), floor)

            wg = jnp.exp(a_col - mu_last_h)
            decay = jnp.exp(m_prev_h - mu_last_h)
            kw = k * wg
            c_scr[state] = decay * c_h + lax.dot_general(kw.astype(BF16), vb, (((0,), (0,)), ((), ())),
                                                         preferred_element_type=F32)
            n_scr[state] = decay * n_h + jnp.sum(kw, axis=0, keepdims=True)

            gated = _sigmoid(o_ref[bb, rows, cols]) * h_out
            y_ref[bb, rows, cols] = _head_norm(gated, nw_ref[:, cols])
        m_scr[bb] = cumf[chunk - 1:chunk, :] + mu_last


def _mlstm(proj, gates, gate_b, norm_w, *, batch, seq, width, col0):
    t = proj.shape[0]
    heads = width // HEAD_DIM
    nb = SCAN_BATCH if batch % SCAN_BATCH == 0 else 1
    proj3 = proj.reshape(batch, seq, proj.shape[1])
    gates3 = gates.reshape(batch, seq, LANES)
    blk = lambda col: pl.BlockSpec((nb, SCAN_BLOCK, width), lambda b, i: (b, i, col))
    y = pl.pallas_call(
        functools.partial(_mlstm_kernel, chunk=SCAN_CHUNK, heads=heads),
        grid=(batch // nb, seq // SCAN_BLOCK),
        in_specs=[blk(col0), blk(col0 + 1), blk(col0 + 2), blk(col0 + 3),
                  pl.BlockSpec((nb, SCAN_BLOCK, LANES), lambda b, i: (b, i, 0)),
                  pl.BlockSpec((1, LANES), lambda b, i: (0, 0)),
                  pl.BlockSpec((1, width), lambda b, i: (0, 0))],
        out_specs=pl.BlockSpec((nb, SCAN_BLOCK, width), lambda b, i: (b, i, 0)),
        out_shape=jax.ShapeDtypeStruct((batch, seq, width), F32),
        scratch_shapes=[pltpu.VMEM((nb * heads, HEAD_DIM, HEAD_DIM), F32),
                        pltpu.VMEM((nb * heads, 1, HEAD_DIM), F32),
                        pltpu.VMEM((nb, 1, LANES), F32)],
        compiler_params=_cp("parallel", "arbitrary"),
    )(proj3, proj3, proj3, proj3, gates3, gate_b, norm_w)
    return y.reshape(t, width)


def _ret_kernel(q_ref, k_ref, v_ref, g_ref, cos_ref, sin_ref, intra_ref, cross_ref, zeta_ref, nw_ref,
                y_ref, r_scr, *, chunk, heads, chunk_decay):
    @pl.when(pl.program_id(1) == 0)
    def _():
        r_scr[...] = jnp.zeros_like(r_scr)

    tb = q_ref.shape[0]
    hd = HEAD_DIM
    scale = hd ** -0.5
    for c in range(tb // chunk):
        rows = slice(c * chunk, (c + 1) * chunk)
        cos = cos_ref[rows, :]
        sin = sin_ref[rows, :]
        for h in range(heads):
            cols = slice(h * hd, (h + 1) * hd)
            q = q_ref[rows, cols]
            k = k_ref[rows, cols]
            q = q * cos + pltpu.roll(q, hd // 2, axis=1) * sin
            k = (k * cos + pltpu.roll(k, hd // 2, axis=1) * sin) * scale
            vb = v_ref[rows, cols].astype(BF16)
            qb = q.astype(BF16)
            s = lax.dot_general(qb, k.astype(BF16), (((1,), (1,)), ((), ())), preferred_element_type=F32)
            inner = jnp.dot((s * intra_ref[h]).astype(BF16), vb, preferred_element_type=F32)
            r_h = r_scr[h]
            crs = jnp.dot(qb, r_h.astype(BF16), preferred_element_type=F32) * cross_ref[h]
            r_scr[h] = chunk_decay[h] * r_h + lax.dot_general(
                (k * zeta_ref[h]).astype(BF16), vb, (((0,), (0,)), ((), ())), preferred_element_type=F32)
            y_ref[rows, cols] = _silu(g_ref[rows, cols]) * _head_norm(inner + crs, nw_ref[:, cols])


def _retention(proj, norm_w, *, batch, seq, width, col0):
    t = proj.shape[0]
    heads = width // HEAD_DIM
    chunk = SCAN_CHUNK
    n_tblk = seq // SCAN_BLOCK
    half = HEAD_DIM // 2
    inv = np.float32(ROPE_BASE) ** (-np.arange(half, dtype=np.float32) / np.float32(half))
    ang = (np.arange(seq, dtype=np.float32)[:, None] * inv[None, :]).astype(np.float64)
    cos_t = jnp.asarray(np.concatenate([np.cos(ang), np.cos(ang)], axis=-1), F32)
    sin_t = jnp.asarray(np.concatenate([-np.sin(ang), np.sin(ang)], axis=-1), F32)
    log_g = jnp.log(1.0 - 2.0 ** (-5.0 - jnp.arange(heads, dtype=F32)))
    tt = jnp.arange(chunk, dtype=F32)
    lag = tt[:, None] - tt[None, :]
    intra = jnp.where(lag >= 0, jnp.exp(jnp.maximum(lag, 0.0)[None] * log_g[:, None, None]), 0.0)
    cross = jnp.broadcast_to(jnp.exp((tt + 1.0)[None, :] * log_g[:, None])[:, :, None], (heads, chunk, HEAD_DIM))
    zeta = jnp.broadcast_to(jnp.exp((chunk - 1.0 - tt)[None, :] * log_g[:, None])[:, :, None],
                            (heads, chunk, HEAD_DIM))
    chunk_decay = tuple(float((1.0 - 2.0 ** (-5.0 - h)) ** chunk) for h in range(heads))

    blk = lambda col: pl.BlockSpec((SCAN_BLOCK, width), lambda b, i: (b * n_tblk + i, col))
    pos = pl.BlockSpec((SCAN_BLOCK, HEAD_DIM), lambda b, i: (i, 0))
    full3 = lambda a: pl.BlockSpec(a.shape, lambda b, i: (0, 0, 0))
    return pl.pallas_call(
        functools.partial(_ret_kernel, chunk=chunk, heads=heads, chunk_decay=chunk_decay),
        grid=(batch, n_tblk),
        in_specs=[blk(col0), blk(col0 + 1), blk(col0 + 2), blk(col0 + 3), pos, pos,
                  full3(intra), full3(cross), full3(zeta),
                  pl.BlockSpec((1, width), lambda b, i: (0, 0))],
        out_specs=pl.BlockSpec((SCAN_BLOCK, width), lambda b, i: (b * n_tblk + i, 0)),
        out_shape=jax.ShapeDtypeStruct((t, width), F32),
        scratch_shapes=[pltpu.VMEM((heads, HEAD_DIM, HEAD_DIM), F32)],
        compiler_params=_cp("parallel", "arbitrary"),
    )(proj, proj, proj, proj, cos_t, sin_t, intra, cross, zeta, norm_w)


def _merge_kernel(x_ref, *refs):
    y_refs, wg_refs = refs[:N_BRANCH], refs[N_BRANCH:2 * N_BRANCH]
    wb_ref, o_ref, xb_ref = refs[2 * N_BRANCH:]

    @pl.when(pl.program_id(1) == 0)
    def _():
        xb_ref[...] = x_ref[...].astype(BF16)

    xb = xb_ref[...]
    acc = None
    for n in range(N_BRANCH):
        gate = _sigmoid(jnp.dot(xb, wg_refs[n][...], preferred_element_type=F32))
        term = gate * jnp.dot(y_refs[n][...].astype(BF16), wb_ref[n], preferred_element_type=F32)
        acc = term if acc is None else acc + term
    o_ref[...] = acc


def _merge(x, branches, w_gate, gate_col0, w_branch, *, tm, tn):
    t, d = x.shape
    width = branches[0].shape[1]
    ybs = pl.BlockSpec((tm, width), lambda i, j: (i, 0))

    def gate_spec(n):
        return pl.BlockSpec((d, tn), lambda i, j: (0, (gate_col0 + n * d) // tn + j))

    return pl.pallas_call(
        _merge_kernel,
        grid=(t // tm, d // tn),
        in_specs=[pl.BlockSpec((tm, d), lambda i, j: (i, 0))] + [ybs] * N_BRANCH
                 + [gate_spec(n) for n in range(N_BRANCH)]
                 + [pl.BlockSpec((N_BRANCH, width, tn), lambda i, j: (0, 0, j))],
        out_specs=pl.BlockSpec((tm, tn), lambda i, j: (i, j)),
        out_shape=jax.ShapeDtypeStruct((t, d), F32),
        scratch_shapes=[pltpu.VMEM((tm, d), BF16)],
        compiler_params=_cp("parallel", "arbitrary"),
    )(x, *branches, *([w_gate] * N_BRANCH), w_branch)


def _proj_ln_kernel(a_ref, w_ref, r_ref, lw_ref, lb_ref, o_ref, *, alpha):
    y = jnp.dot(a_ref[...].astype(BF16), w_ref[...], preferred_element_type=F32)
    o_ref[...] = _layer_norm(alpha * r_ref[...] + y, lw_ref[...], lb_ref[...])


def _proj_ln(a, w, resid, ln_w, ln_b, *, alpha, tm):
    t, k = a.shape
    d = w.shape[1]
    row = lambda n: pl.BlockSpec((tm, n), lambda i: (i, 0))
    const = lambda shape: pl.BlockSpec(shape, lambda i: (0, 0))
    return pl.pallas_call(
        functools.partial(_proj_ln_kernel, alpha=alpha),
        grid=(t // tm,),
        in_specs=[row(k), const((k, d)), row(d), const((1, d)), const((1, d))],
        out_specs=row(d),
        out_shape=jax.ShapeDtypeStruct((t, d), F32),
        compiler_params=_cp("parallel"),
    )(a, w, resid, ln_w, ln_b)


def _xattn_kernel(x_ref, wq_ref, kv_ref, wo_ref, lw_ref, lb_ref, o_ref, *, alpha, heads):
    x = x_ref[...]
    hd = HEAD_DIM
    inner = heads * hd
    q = jnp.dot(x.astype(BF16), wq_ref[...], preferred_element_type=F32)
    outs = []
    for h in range(heads):
        qh = q[:, h * hd:(h + 1) * hd].astype(BF16)
        kh = kv_ref[:, h * hd:(h + 1) * hd].astype(BF16)
        vh = kv_ref[:, inner + h * hd:inner + (h + 1) * hd].astype(BF16)
        s = lax.dot_general(qh, kh, (((1,), (1,)), ((), ())), preferred_element_type=F32) * hd ** -0.5
        s = s - jnp.max(s, axis=-1, keepdims=True)
        e = jnp.exp(s)
        p = e / jnp.sum(e, axis=-1, keepdims=True)
        outs.append(jnp.dot(p.astype(BF16), vh, preferred_element_type=F32).astype(BF16))
    o = jnp.concatenate(outs, axis=-1)
    y = jnp.dot(o, wo_ref[...], preferred_element_type=F32)
    o_ref[...] = _layer_norm(alpha * x + y, lw_ref[...], lb_ref[...])


def _xattn(x, kv, wq, wo, ln_w, ln_b, *, alpha, seq, mem_len, tm):
    t, d = x.shape
    inner = wq.shape[1]
    n_tblk = seq // tm
    const = lambda shape: pl.BlockSpec(shape, lambda i: (0, 0))
    return pl.pallas_call(
        functools.partial(_xattn_kernel, alpha=alpha, heads=XATTN_HEADS),
        grid=(t // tm,),
        in_specs=[pl.BlockSpec((tm, d), lambda i: (i, 0)), const((d, inner)),
                  pl.BlockSpec((mem_len, 2 * inner), lambda i: (i // n_tblk, 0)),
                  const((inner, d)), const((1, d)), const((1, d))],
        out_specs=pl.BlockSpec((tm, d), lambda i: (i, 0)),
        out_shape=jax.ShapeDtypeStruct((t, d), F32),
        compiler_params=_cp("parallel"),
    )(x, wq, kv, wo, ln_w, ln_b)


def _round_up_pow2(x, m):
    shift = m.bit_length() - 1
    return jnp.left_shift(jnp.right_shift(x + (m - 1), shift), shift)


def _route_kernel(x_ref, wr_ref, rb_ref, w_ref, lrow_ref, segtab_ref, blk_ref, seg_ref, size_all, tot, *, tm, bm):
    step = pl.program_id(0)

    @pl.when(step == 0)
    def _():
        tot[...] = jnp.zeros_like(tot)

    e_n, g_n = N_EXPERTS, N_GROUPS
    per = e_n // g_n
    def split(a):
        hi = a.astype(BF16)
        return hi, (a - hi.astype(F32)).astype(BF16)

    def nt(a, b):
        return lax.dot_general(a, b, (((1,), (1,)), ((), ())), preferred_element_type=F32)

    (w_hi, w_lo), (x_hi, x_lo) = split(wr_ref[...]), split(x_ref[...])
    logits = nt(w_hi, x_hi) + (nt(w_hi, x_lo) + nt(w_lo, x_hi))
    scores = _sigmoid(logits)
    biased = scores + rb_ref[...]
    b3 = biased.reshape(g_n, per, tm)
    member = lax.broadcasted_iota(I32, (g_n, per, tm), 1)
    top1 = jnp.max(b3, axis=1, keepdims=True)
    first = jnp.min(jnp.where(b3 == top1, member, per), axis=1, keepdims=True)
    top2 = jnp.max(jnp.where(member == first, -jnp.inf, b3), axis=1, keepdims=True)
    gs = top1 + top2
    gid = lax.broadcasted_iota(I32, (g_n, 1, tm), 0)
    rank = jnp.zeros((g_n, 1, tm), I32)
    for other in range(g_n):
        o = gs[other:other + 1]
        ahead = jnp.logical_or(o > gs, jnp.logical_and(o == gs, other < gid))
        rank = rank + jnp.where(ahead, 1, 0)
    cur = jnp.where(rank < TOPK_GROUPS, b3, -jnp.inf).reshape(e_n, tm)

    eid = lax.broadcasted_iota(I32, (e_n, tm), 0)
    picks, vals = [], []
    sel = jnp.zeros((e_n, tm), F32)
    for k in range(TOP_K):
        mx = jnp.max(cur, axis=0, keepdims=True)
        ik = jnp.min(jnp.where(cur == mx, eid, e_n), axis=0, keepdims=True)
        hit = eid == ik
        vals.append(jnp.sum(jnp.where(hit, scores, 0.0), axis=0, keepdims=True))
        cur = jnp.where(hit, -jnp.inf, cur)
        sel = jnp.where(hit, 1.0, sel)
        picks.append(ik)
    total = vals[0]
    for v in vals[1:]:
        total = total + v

    tri = jnp.where(lax.broadcasted_iota(I32, (tm, tm), 0) <= lax.broadcasted_iota(I32, (tm, tm), 1), 1.0, 0.0)
    incl = jnp.dot(sel.astype(BF16), tri.astype(BF16), preferred_element_type=F32)
    size = _round_up_pow2(jnp.broadcast_to(incl[:, tm - 1:tm], (e_n, LANES)).astype(I32), SEG_ALIGN)
    loff = _cum_rows(size, jnp.add, 0) - size
    base = loff[:, 0:1].astype(F32) + incl - 1.0
    for k in range(TOP_K):
        w_ref[k:k + 1, :] = vals[k] / total * ROUTE_SCALE
        lrow_ref[0, k:k + 1, :] = jnp.sum(jnp.where(eid == picks[k], base, 0.0),
                                          axis=0, keepdims=True).astype(I32)
    size_all[step] = size
    tot[...] = tot[...] + size

    @pl.when(step == pl.num_programs(0) - 1)
    def _():
        rows = tot[...]
        pcnt = _round_up_pow2(rows, bm)
        pend = _cum_rows(pcnt, jnp.add, 0)

        def tile_seg(i, run):
            segtab_ref[i, 0] = run
            segtab_ref[i, 1] = size_all[i]
            return run + size_all[i]

        lax.fori_loop(0, pl.num_programs(0), tile_seg, pend - pcnt)

        nb = blk_ref.shape[2]
        row0 = lax.broadcasted_iota(I32, (e_n, nb), 1) * bm
        total_rows = pend[e_n - 1:e_n, 0:1]
        owner = jnp.sum(jnp.where(pend[:, 0:1] <= row0, 1, 0), axis=0, keepdims=True)
        last_owner = jnp.sum(jnp.where(pend[:, 0:1] < total_rows, 1, 0), axis=0, keepdims=True)
        blk_ref[0] = jnp.where(row0[0:1, :] < total_rows, jnp.minimum(owner, e_n - 1), last_owner)
        data_end = pend - pcnt + rows
        inside = jnp.logical_and(pend[:, 0:1] - pcnt[:, 0:1] <= row0, row0 < pend[:, 0:1])
        blk_ref[1] = jnp.sum(jnp.where(inside, jnp.clip(data_end[:, 0:1] - row0, 0, bm), 0), axis=0, keepdims=True)
        seg_ref[0] = pend
        seg_ref[1] = pcnt


def _route(x, router_w_t, router_b, *, tm, bm, n_blocks):
    t, d = x.shape
    e_n = N_EXPERTS
    n_t = t // tm
    assert bm & (bm - 1) == 0
    nb_pad = -(-n_blocks // LANES) * LANES
    return pl.pallas_call(
        functools.partial(_route_kernel, tm=tm, bm=bm),
        grid=(n_t,),
        in_specs=[pl.BlockSpec((tm, d), lambda i: (i, 0)),
                  pl.BlockSpec((e_n, d), lambda i: (0, 0)),
                  pl.BlockSpec((e_n, 1), lambda i: (0, 0))],
        out_specs=[pl.BlockSpec((TOP_K, tm), lambda i: (0, i)),
                   pl.BlockSpec((1, TOP_K, tm), lambda i: (i, 0, 0)),
                   pl.BlockSpec((n_t, 2, e_n, LANES), lambda i: (0, 0, 0, 0)),
                   pl.BlockSpec((2, 1, nb_pad), lambda i: (0, 0, 0)),
                   pl.BlockSpec((2, e_n, LANES), lambda i: (0, 0, 0))],
        out_shape=[jax.ShapeDtypeStruct((TOP_K, t), F32), jax.ShapeDtypeStruct((n_t, TOP_K, tm), I32),
                   jax.ShapeDtypeStruct((n_t, 2, e_n, LANES), I32), jax.ShapeDtypeStruct((2, 1, nb_pad), I32),
                   jax.ShapeDtypeStruct((2, e_n, LANES), I32)],
        scratch_shapes=[pltpu.VMEM((n_t, e_n, LANES), I32), pltpu.VMEM((e_n, LANES), I32)],
        compiler_params=_cp("arbitrary"),
    )(x, router_w_t, router_b)


U32 = jnp.uint32
_HI_HALF = 0xFFFF0000


def _pack_pairs(a):
    n = a.shape[1] // 2
    lo = pltpu.bitcast(a[:, :n].astype(BF16).astype(F32), U32)
    hi = pltpu.bitcast(a[:, n:].astype(BF16).astype(F32), U32)
    return jnp.bitwise_or(jnp.bitwise_and(hi, U32(_HI_HALF)), jnp.right_shift(lo, U32(16)))


def _unpack_pairs(words):
    lo = pltpu.bitcast(jnp.left_shift(words, U32(16)), F32).astype(BF16)
    hi = pltpu.bitcast(jnp.bitwise_and(words, U32(_HI_HALF)), F32).astype(BF16)
    return lo, hi


def _tile_rows(tm):
    worst = TOP_K * tm + N_EXPERTS * (SEG_ALIGN - 1)
    return -(-worst // ONEHOT_ROWS) * ONEHOT_ROWS


def _onehot_rows(chunk, lrow, values, tm):
    rid = chunk * ONEHOT_ROWS + lax.broadcasted_iota(I32, (ONEHOT_ROWS, tm), 0)
    acc = jnp.zeros((ONEHOT_ROWS, tm), F32)
    for k in range(TOP_K):
        acc = jnp.where(rid == lrow[k:k + 1, :], 1.0 if values is None else values[k:k + 1, :], acc)
    return acc.astype(BF16)


def _start_segments(gstart_ref, size_ref, tile, make_copy, first=0, last=N_EXPERTS, row0=0):
    def body(e, loff):
        n = size_ref[tile * N_EXPERTS + e]

        @pl.when(n > 0)
        def _():
            make_copy(pl.multiple_of(gstart_ref[tile * N_EXPERTS + e], SEG_ALIGN),
                      pl.multiple_of(loff, SEG_ALIGN), pl.multiple_of(n, SEG_ALIGN)).start()
        return loff + n
    return lax.fori_loop(first, last, body, row0)


def _segments_within(size_ref, tile, limit):
    def body(e, carry):
        count, rows = carry
        end = rows + size_ref[tile * N_EXPERTS + e]
        fits = jnp.logical_and(count == e, end <= limit)
        return count + jnp.where(fits, 1, 0), jnp.where(fits, end, rows)
    return lax.fori_loop(0, N_EXPERTS, body, (0, 0))


def _tile_total(size_ref, tile):
    return lax.fori_loop(0, N_EXPERTS, lambda e, s: s + size_ref[tile * N_EXPERTS + e], 0)


def _wait_rows(make_copy, rows):
    @pl.when(rows > 0)
    def _():
        make_copy(0, 0, pl.multiple_of(rows, SEG_ALIGN)).wait()


def _dispatch_kernel(gstart_ref, size_ref, x_ref, lrow_ref, xs_ref, stage, inflight, sem_a, sem_b, *, tm):
    i = pl.program_id(0)
    last = pl.num_programs(0) - 1
    n_chunks = stage.shape[0] // ONEHOT_ROWS
    split_chunk = n_chunks // 2
    split = split_chunk * ONEHOT_ROWS
    assert tm <= ONEHOT_ROWS

    def copy_on(sem):
        return lambda g, loff, n: pltpu.make_async_copy(stage.at[pl.ds(loff, n), :], xs_ref.at[pl.ds(g, n), :], sem)

    xb = x_ref[...].astype(BF16)
    lrow = lrow_ref[0]
    n_rows = _tile_total(size_ref, i)

    def chunks(lo, hi):
        for c in range(lo, hi):
            def one(c=c):
                stage[c * ONEHOT_ROWS:(c + 1) * ONEHOT_ROWS, :] = _pack_pairs(jnp.dot(
                    _onehot_rows(c, lrow, None, tm), xb, preferred_element_type=F32))
            if c * ONEHOT_ROWS < TOP_K * tm:
                one()
            else:
                pl.when(c * ONEHOT_ROWS < n_rows)(one)

    @pl.when(i == 0)
    def _():
        inflight[0] = 0
        inflight[1] = 0

    _wait_rows(copy_on(sem_a), inflight[0])
    chunks(0, split_chunk - 1)
    _wait_rows(copy_on(sem_b), inflight[1])
    chunks(split_chunk - 1, split_chunk)
    experts_a, rows_a = _segments_within(size_ref, i, split)
    _start_segments(gstart_ref, size_ref, i, copy_on(sem_a), 0, experts_a)
    chunks(split_chunk, n_chunks)
    rows_b = _start_segments(gstart_ref, size_ref, i, copy_on(sem_b), experts_a, N_EXPERTS, rows_a) - rows_a
    inflight[0] = rows_a
    inflight[1] = rows_b

    @pl.when(i == last)
    def _():
        _wait_rows(copy_on(sem_a), rows_a)
        _wait_rows(copy_on(sem_b), rows_b)


def _dispatch(x, lrow, gstart, size, *, rows, tm):
    t, d = x.shape
    return pl.pallas_call(
        functools.partial(_dispatch_kernel, tm=tm),
        grid_spec=pltpu.PrefetchScalarGridSpec(
            num_scalar_prefetch=2,
            grid=(t // tm,),
            in_specs=[pl.BlockSpec((tm, d), lambda i, *_: (i, 0)),
                      pl.BlockSpec((1, TOP_K, tm), lambda i, *_: (i, 0, 0))],
            out_specs=pl.BlockSpec(memory_space=pl.ANY),
            scratch_shapes=[pltpu.VMEM((_tile_rows(tm), d // 2), U32), pltpu.SMEM((2,), I32)]
                           + [pltpu.SemaphoreType.DMA(())] * 2,
        ),
        out_shape=jax.ShapeDtypeStruct((rows, d // 2), U32),
        compiler_params=_cp("arbitrary"),
    )(gstart, size, x, lrow)


def _expert_block_kernel(blk_ref, nused_ref, xs_ref, wgu_ref, wdn_ref, ys_ref, wgu_b, wdn_b, *, n_blocks, bm):
    j = pl.program_id(0)
    valid = blk_ref[n_blocks + j]
    changed = jnp.logical_or(j == 0, blk_ref[j] != blk_ref[jnp.maximum(j - 1, 0)])

    @pl.when(jnp.logical_and(valid > 0, changed))
    def _():
        wgu_b[...] = wgu_ref[...].astype(BF16)
        wdn_b[...] = wdn_ref[...].astype(BF16)

    def swiglu(rows, masked):
        x = xs_ref[0:rows, :]
        if masked:
            x = jnp.where(lax.broadcasted_iota(I32, x.shape, 0) < valid, x, jnp.zeros_like(x))
        x_lo, x_hi = _unpack_pairs(x)
        f = wdn_b.shape[0]
        half = x.shape[1]
        gu = jnp.dot(x_lo, wgu_b[0:half, :], preferred_element_type=F32) \
            + jnp.dot(x_hi, wgu_b[half:2 * half, :], preferred_element_type=F32)
        hidden = (_silu(gu[:, :f]) * gu[:, f:]).astype(BF16)
        ys_ref[0:rows, :] = _pack_pairs(jnp.dot(hidden, wdn_b[...], preferred_element_type=F32))

    @pl.when(valid == bm)
    def _():
        swiglu(bm, False)

    quarter = bm // 4
    for q in range(1, 5):
        @pl.when(jnp.logical_and(jnp.logical_and(valid > (q - 1) * quarter, valid <= q * quarter), valid < bm))
        def _():
            swiglu(q * quarter, True)


def _experts_blocked(xs, blk, nused, w_gu, w_dn, layer, *, n_blocks, bm):
    rows, half = xs.shape
    d = 2 * half
    f2 = w_gu.shape[3]
    f = w_dn.shape[2]
    row_blk = lambda j, bl, nu: (jnp.minimum(j, nu[0] - 1), 0)
    return pl.pallas_call(
        functools.partial(_expert_block_kernel, n_blocks=n_blocks, bm=bm),
        grid_spec=pltpu.PrefetchScalarGridSpec(
            num_scalar_prefetch=2,
            grid=(n_blocks,),
            in_specs=[pl.BlockSpec((bm, half), row_blk),
                      pl.BlockSpec((None, None, d, f2), lambda j, bl, nu: (layer, bl[j], 0, 0)),
                      pl.BlockSpec((None, None, f, d), lambda j, bl, nu: (layer, bl[j], 0, 0))],
            out_specs=pl.BlockSpec((bm, half), row_blk),
            scratch_shapes=[pltpu.VMEM((d, f2), BF16), pltpu.VMEM((f, d), BF16)],
        ),
        out_shape=jax.ShapeDtypeStruct((rows, half), U32),
        compiler_params=_cp("arbitrary"),
    )(blk, nused, xs, w_gu, w_dn)


def _combine_tile_kernel(gstart_ref, size_ref, x_ref, lrow_ref, w_ref, sdn_ref, lw_ref, lb_ref, sgu_hbm, ys_ref,
                         o_ref, ybuf, sgu, sems, wsem, *, tm, alpha):
    i = pl.program_id(0)
    last_tile = pl.num_programs(0) - 1
    n_chunks = ybuf.shape[1] // ONEHOT_ROWS
    sure_chunks = (TOP_K * tm) // ONEHOT_ROWS

    def contract_rows(w_rows, y_rows):
        return lax.dot_general(w_rows, y_rows, (((0,), (0,)), ((), ())), preferred_element_type=F32)

    def copy_to(slot):
        return lambda g, loff, n: pltpu.make_async_copy(ys_ref.at[pl.ds(g, n), :], ybuf.at[slot, pl.ds(loff, n), :],
                                                        sems.at[slot])

    def fetch(tile, slot):
        _start_segments(gstart_ref, size_ref, tile, copy_to(slot))

    n_rows = _tile_total(size_ref, i)

    @pl.when(i == 0)
    def _():
        ybuf[...] = jnp.zeros_like(ybuf)
        weights = pltpu.make_async_copy(sgu_hbm, sgu, wsem)
        weights.start()
        weights.wait()
        fetch(0, 0)

    for slot in range(2):
        @pl.when(i % 2 == slot)
        def _():
            @pl.when(i < last_tile)
            def _():
                fetch(i + 1, 1 - slot)

            x = x_ref[...]
            f = sdn_ref.shape[0]
            gu = jnp.dot(x.astype(BF16), sgu[...], preferred_element_type=F32)
            hidden = (_silu(gu[:, :f]) * gu[:, f:]).astype(BF16)
            acc = jnp.dot(hidden, sdn_ref[...], preferred_element_type=F32)
            _wait_rows(copy_to(slot), n_rows)
            lrow, w = lrow_ref[0], w_ref[...]
            half = ybuf.shape[2]

            def routed(c):
                rows = slice(c * ONEHOT_ROWS, (c + 1) * ONEHOT_ROWS)
                w_rows = _onehot_rows(c, lrow, w, tm)
                y_lo, y_hi = _unpack_pairs(ybuf[slot, rows, :])
                return contract_rows(w_rows, y_lo), contract_rows(w_rows, y_hi)

            acc_lo, acc_hi = acc[:, :half], acc[:, half:]
            for c in range(sure_chunks):
                r_lo, r_hi = routed(c)
                acc_lo, acc_hi = acc_lo + r_lo, acc_hi + r_hi
            o_ref[:, 0:half] = acc_lo
            o_ref[:, half:2 * half] = acc_hi
            for c in range(sure_chunks, n_chunks):
                @pl.when(c * ONEHOT_ROWS < n_rows)
                def _():
                    r_lo, r_hi = routed(c)
                    o_ref[:, 0:half] += r_lo
                    o_ref[:, half:2 * half] += r_hi
            o_ref[...] = _layer_norm(alpha * x + o_ref[...], lw_ref[...], lb_ref[...])


def _combine(x, ys, lrow, wts, gstart, size, s_gu, s_dn, ln_w, ln_b, *, alpha, tm):
    t, d = x.shape
    const = lambda shape: pl.BlockSpec(shape, lambda i, *_: (0, 0))
    return pl.pallas_call(
        functools.partial(_combine_tile_kernel, tm=tm, alpha=alpha),
        grid_spec=pltpu.PrefetchScalarGridSpec(
            num_scalar_prefetch=2,
            grid=(t // tm,),
            in_specs=[pl.BlockSpec((tm, d), lambda i, *_: (i, 0)),
                      pl.BlockSpec((1, TOP_K, tm), lambda i, *_: (i, 0, 0)),
                      pl.BlockSpec((TOP_K, tm), lambda i, *_: (0, i)),
                      const(s_dn.shape), const((1, d)), const((1, d)),
                      pl.BlockSpec(memory_space=pl.ANY), pl.BlockSpec(memory_space=pl.ANY)],
            out_specs=pl.BlockSpec((tm, d), lambda i, *_: (i, 0)),
            scratch_shapes=[pltpu.VMEM((2, _tile_rows(tm), d // 2), U32), pltpu.VMEM(s_gu.shape, BF16),
                            pltpu.SemaphoreType.DMA((2,)), pltpu.SemaphoreType.DMA(())],
        ),
        out_shape=jax.ShapeDtypeStruct((t, d), F32),
        compiler_params=_cp("arbitrary"),
    )(gstart, size, x, lrow, wts, s_dn, ln_w, ln_b, s_gu, ys)


def _mixer_sublayer(x, w_in_all, layer, gate_b, pool_w, pool_scale, conv_w, mlstm_norm_w, ret_norm_w, w_branch,
                    w_out, ln_w, ln_b, *, batch, seq, alpha):
    t, d = x.shape
    width = d // N_BRANCH
    heads = width // HEAD_DIM
    gate_off = 8 * width
    ret_off = gate_off
    g_off = ret_off + 4 * width
    if_off = g_off + N_BRANCH * d
    w_bf16 = _realign_cast(w_in_all, layer, lo_col=gate_off, hi_col=if_off, shift=2 * heads, tr=512, tn=512)
    gate_bias = jnp.pad(gate_b, (0, LANES - 2 * heads)).reshape(1, LANES)

    proj = _matmul(x, w_bf16, tm=1024, tn=1024, ncols=g_off)
    gates = _matmul(x, w_bf16, tm=1024, tn=LANES, ncols=LANES, col0=if_off)
    y_pool, y_conv = _pool_conv(proj, pool_w.astype(BF16), pool_scale.reshape(1, width), conv_w,
                                seq=seq, width=width, tb=512)
    y_mlstm = _mlstm(proj, gates, gate_bias, mlstm_norm_w.reshape(1, width),
                     batch=batch, seq=seq, width=width, col0=4)
    y_ret = _retention(proj, ret_norm_w.reshape(1, width), batch=batch, seq=seq, width=width,
                       col0=ret_off // width)
    merged = _merge(x, (y_pool, y_conv, y_mlstm, y_ret), w_bf16, g_off, w_branch.astype(BF16),
                    tm=512, tn=512)
    return _proj_ln(merged, w_out.astype(BF16), x, ln_w, ln_b, alpha=alpha, tm=512)


def _xattn_sublayer(x, mem2d, wq, wk, wv, wo, ln_w, ln_b, *, seq, mem_len, alpha):
    w_kv = jnp.concatenate([wk, wv], axis=1).astype(BF16)
    kv = _matmul(mem2d, w_kv, tm=min(mem2d.shape[0], 1024), tn=512)
    return _xattn(x, kv, wq.astype(BF16), wo.astype(BF16), ln_w, ln_b,
                  alpha=alpha, seq=seq, mem_len=mem_len, tm=512)


def _moe_sublayer(x, router_w, router_b, w_gu, w_dn, layer, s_gu, s_dn, ln_w, ln_b, *, alpha):
    t, d = x.shape
    e_n, bm = N_EXPERTS, MOE_BM
    tm = ROUTE_TM
    n_blocks = -(-(t * TOP_K + (t // tm) * e_n * (SEG_ALIGN - 1)) // bm) + e_n
    wts, lrow, segtab, blk, seg = _route(x, router_w.T, router_b.reshape(e_n, 1), tm=tm, bm=bm, n_blocks=n_blocks)
    gstart, size = segtab[:, 0, :, 0].reshape(-1), segtab[:, 1, :, 0].reshape(-1)
    nused = seg[0, e_n - 1, 0] // bm

    xs = _dispatch(x, lrow, gstart, size, rows=n_blocks * bm, tm=tm)
    ys = _experts_blocked(xs, blk[:, 0, :n_blocks].reshape(-1), nused.reshape(1), w_gu, w_dn, layer,
                          n_blocks=n_blocks, bm=bm)
    return _combine(x, ys, lrow, wts, gstart, size, s_gu.astype(BF16), s_dn.astype(BF16), ln_w, ln_b,
                    alpha=alpha, tm=tm)


def kernel(x, mem, w_in, mlstm_gate_b, pool_w, pool_scale, conv_w, mlstm_norm_w, ret_norm_w, w_branch,
           w_mix_out, xa_wq, xa_wk, xa_wv, xa_wo, router_w, router_b, moe_w_gu, moe_w_dn, shared_w_gu,
           shared_w_dn, ln_w, ln_b):
    batch, seq, d = x.shape
    depth = w_in.shape[0]
    mem_len = mem.shape[1]
    alpha = (2 * depth) ** 0.25
    h = x.reshape(batch * seq, d)
    mem2d = mem.reshape(batch * mem_len, d)
    for l in range(depth):
        lw = ln_w[l].reshape(3, 1, d)
        lb = ln_b[l].reshape(3, 1, d)
        h = _mixer_sublayer(h, w_in, l, mlstm_gate_b[l], pool_w[l], pool_scale[l], conv_w[l], mlstm_norm_w[l],
                            ret_norm_w[l], w_branch[l], w_mix_out[l], lw[0], lb[0],
                            batch=batch, seq=seq, alpha=alpha)
        h = _xattn_sublayer(h, mem2d, xa_wq[l], xa_wk[l], xa_wv[l], xa_wo[l], lw[1], lb[1],
                            seq=seq, mem_len=mem_len, alpha=alpha)
        h = _moe_sublayer(h, router_w[l], router_b[l], moe_w_gu, moe_w_dn, l, shared_w_gu[l], shared_w_dn[l],
                          lw[2], lb[2], alpha=alpha)
    return h.reshape(batch, seq, d)
```

```python
import functools

import numpy as np
import jax
import jax.numpy as jnp
from jax import lax
from jax.experimental import pallas as pl
from jax.experimental.pallas import tpu as pltpu

F32 = jnp.float32
BF16 = jnp.bfloat16
I32 = jnp.int32

N_BRANCH = 4
HEAD_DIM = 128
POOL_WINDOWS = (2, 4, 8, 16)
CONV_WIDTH = 3
ROPE_BASE = 10000.0
XATTN_HEADS = 4
N_EXPERTS = 64
TOP_K = 8
N_GROUPS = 8
TOPK_GROUPS = 4
ROUTE_SCALE = 2.5
LN_EPS = 1e-5

LANES = 128
V7X_VMEM_BYTES = 64 * 1024 * 1024
VMEM_LIMIT = 56 * 1024 * 1024

SCAN_CHUNK = 256
SCAN_BLOCK = 512
SCAN_BATCH = 2
HALO = 16
MOE_BM = 1024
ROUTE_TM = 256
SEG_ALIGN = 16
ONEHOT_ROWS = 512

def _cp(*sem):
    return pltpu.CompilerParams(dimension_semantics=sem, vmem_limit_bytes=VMEM_LIMIT)


def _sigmoid(x):
    return 1.0 / (1.0 + jnp.exp(-x))


def _silu(x):
    return x * _sigmoid(x)


def _log_sigmoid(x):
    return jnp.minimum(x, 0.0) - jnp.log(1.0 + jnp.exp(-jnp.abs(x)))


def _layer_norm(z, w, b):
    mu = jnp.mean(z, axis=-1, keepdims=True)
    d = z - mu
    var = jnp.mean(d * d, axis=-1, keepdims=True)
    return d * lax.rsqrt(var + LN_EPS) * w + b


def _head_norm(h, w):
    mu = jnp.mean(h, axis=-1, keepdims=True)
    d = h - mu
    var = jnp.mean(d * d, axis=-1, keepdims=True)
    return d * lax.rsqrt(var + LN_EPS) * w


def _mm_kernel(x_ref, w_ref, o_ref, xb_ref):
    @pl.when(pl.program_id(1) == 0)
    def _():
        xb_ref[...] = x_ref[...].astype(BF16)

    o_ref[...] = jnp.dot(xb_ref[...], w_ref[...], preferred_element_type=F32)


def _realign_cast_kernel(a_ref, b_ref, o_ref, *, shift, lo, hi):
    j = pl.program_id(1)
    tn = o_ref.shape[1]
    shifted = jnp.logical_and(j >= lo, j < hi)

    @pl.when(shifted)
    def _():
        both = jnp.concatenate([a_ref[...], b_ref[...]], axis=1)
        o_ref[...] = both[:, shift:shift + tn].astype(BF16)

    @pl.when(jnp.logical_not(shifted))
    def _():
        o_ref[...] = a_ref[...].astype(BF16)


def _realign_cast(w_all, layer, *, lo_col, hi_col, shift, tr, tn):
    _, rows, _ = w_all.shape
    lo, hi = lo_col // tn, hi_col // tn
    src = lambda j: jnp.where(j == hi, lo, j)
    return pl.pallas_call(
        functools.partial(_realign_cast_kernel, shift=shift, lo=lo, hi=hi),
        grid=(rows // tr, hi + 1),
        in_specs=[pl.BlockSpec((None, tr, tn), lambda i, j: (layer, i, src(j))),
                  pl.BlockSpec((None, tr, LANES), lambda i, j: (layer, i, (src(j) + 1) * (tn // LANES)))],
        out_specs=pl.BlockSpec((tr, tn), lambda i, j: (i, j)),
        out_shape=jax.ShapeDtypeStruct((rows, hi_col + tn), BF16),
        compiler_params=_cp("parallel", "parallel"),
    )(w_all, w_all)


def _matmul(x, w, *, tm, tn, ncols=None, col0=0):
    t, k = x.shape
    n = w.shape[1] if ncols is None else ncols
    return pl.pallas_call(
        _mm_kernel,
        grid=(t // tm, n // tn),
        in_specs=[pl.BlockSpec((tm, k), lambda i, j: (i, 0)),
                  pl.BlockSpec((k, tn), lambda i, j: (0, col0 // tn + j))],
        out_specs=pl.BlockSpec((tm, tn), lambda i, j: (i, j)),
        out_shape=jax.ShapeDtypeStruct((t, n), F32),
        scratch_shapes=[pltpu.VMEM((tm, k), BF16)],
        compiler_params=_cp("parallel", "arbitrary"),
    )(x, w)


def _poolconv_kernel(u_ref, uh_ref, h_ref, hh_ref, b_ref, c_ref, ch_ref, pw_ref, ps_ref, cw_ref,
                     yp_ref, yc_ref, ubuf, zbuf, *, tb, n_tblk):
    first = (pl.program_id(0) % n_tblk) == 0
    ubuf[0:HALO, :] = jnp.where(first, 0.0, uh_ref[...])
    ubuf[HALO:HALO + tb, :] = u_ref[...]
    zbuf[0:HALO, :] = jnp.where(first, 0.0, ch_ref[...] * hh_ref[...])
    zbuf[HALO:HALO + tb, :] = c_ref[...] * h_ref[...]

    t_pos = (pl.program_id(0) % n_tblk) * tb + lax.broadcasted_iota(I32, (tb, LANES), 0)
    gw = u_ref.shape[1] // len(POOL_WINDOWS)
    for grp, win in enumerate(POOL_WINDOWS):
        lanes = slice(grp * gw, (grp + 1) * gw)
        cur = ubuf[HALO:HALO + tb, lanes]
        acc = cur
        for lag in range(1, win):
            acc = acc + ubuf[HALO - lag:HALO - lag + tb, lanes]
        count = jnp.minimum(t_pos + 1, win).astype(F32)
        mixed = acc / count - cur
        y = jnp.dot(mixed.astype(BF16), pw_ref[grp], preferred_element_type=F32)
        yp_ref[:, lanes] = y * ps_ref[:, lanes]

    conv = cw_ref[0:1, :] * zbuf[HALO:HALO + tb, :]
    for lag in range(1, CONV_WIDTH):
        conv = conv + cw_ref[lag:lag + 1, :] * zbuf[HALO - lag:HALO - lag + tb, :]
    yc_ref[...] = b_ref[...] * conv


def _pool_conv(proj, pool_w, pool_scale, conv_w, *, seq, width, tb):
    t = proj.shape[0]
    n_tblk = seq // tb
    ratio = tb // HALO

    def cur(col):
        return pl.BlockSpec((tb, width), lambda g: (g, col))

    def halo(col):
        return pl.BlockSpec((HALO, width), lambda g: (jnp.maximum(g * ratio - 1, 0), col))

    full = lambda shape: pl.BlockSpec(shape, lambda g: (0,) * len(shape))
    return pl.pallas_call(
        functools.partial(_poolconv_kernel, tb=tb, n_tblk=n_tblk),
        grid=(t // tb,),
        in_specs=[cur(0), halo(0), cur(1), halo(1), cur(2), cur(3), halo(3),
                  full(pool_w.shape), full(pool_scale.shape), full(conv_w.shape)],
        out_specs=[pl.BlockSpec((tb, width), lambda g: (g, 0))] * 2,
        out_shape=[jax.ShapeDtypeStruct((t, width), F32)] * 2,
        scratch_shapes=[pltpu.VMEM((HALO + tb, width), F32)] * 2,
        compiler_params=_cp("parallel"),
    )(proj, proj, proj, proj, proj, proj, proj, pool_w, pool_scale, conv_w)


def _cum_rows(x, op, fill):
    n = x.shape[0]
    row = lax.broadcasted_iota(I32, x.shape, 0)
    shift = 1
    while shift < n:
        x = op(x, jnp.where(row >= shift, pltpu.roll(x, shift, axis=0), fill))
        shift *= 2
    return x


def _mlstm_kernel(q_ref, k_ref, v_ref, o_ref, g_ref, gb_ref, nw_ref, y_ref, c_scr, n_scr, m_scr,
                  *, chunk, heads):
    @pl.when(pl.program_id(1) == 0)
    def _():
        c_scr[...] = jnp.zeros_like(c_scr)
        n_scr[...] = jnp.zeros_like(n_scr)
        m_scr[...] = jnp.zeros_like(m_scr)

    nb, tb = q_ref.shape[0], q_ref.shape[1]
    hd = HEAD_DIM
    scale = hd ** -0.5
    tri = (lax.broadcasted_iota(I32, (chunk, chunk), 0) >= lax.broadcasted_iota(I32, (chunk, chunk), 1))
    for c, bb in [(c, bb) for c in range(tb // chunk) for bb in range(nb)]:
        rows = slice(c * chunk, (c + 1) * chunk)
        gates = g_ref[bb, rows, :] + gb_ref[...]
        lf = _log_sigmoid(pltpu.roll(gates, LANES - heads, axis=1))
        cumf = _cum_rows(lf, jnp.add, 0.0)
        a = gates - cumf
        m_prev = m_scr[bb]
        mu = jnp.maximum(_cum_rows(a, jnp.maximum, -jnp.inf), m_prev)
        mu_last = mu[chunk - 1:chunk, :]
        a_t = a.T
        for h in range(heads):
            cols = slice(h * hd, (h + 1) * hd)
            q = q_ref[bb, rows, cols]
            k = k_ref[bb, rows, cols] * scale
            v = v_ref[bb, rows, cols]
            qb, kb, vb = q.astype(BF16), k.astype(BF16), v.astype(BF16)
            mu_col = mu[:, h:h + 1]
            a_col = a[:, h:h + 1]
            m_prev_h = m_prev[:, h:h + 1]
            mu_last_h = mu_last[:, h:h + 1]
            dmat = jnp.exp(jnp.where(tri, a_t[h:h + 1, :] - mu_col, -jnp.inf))
            s = lax.dot_general(qb, kb, (((1,), (1,)), ((), ())), preferred_element_type=F32)
            p = dmat * s
            inter = jnp.exp(m_prev_h - mu_col)
            state = bb * heads + h
            c_h = c_scr[state]
            n_h = n_scr[state]
            num = inter * jnp.dot(qb, c_h.astype(BF16), preferred_element_type=F32) \
                + jnp.dot(p.astype(BF16), vb, preferred_element_type=F32)
            den = inter * jnp.sum(q * n_h, axis=-1, keepdims=True) + jnp.sum(p, axis=-1, keepdims=True)
            floor = jnp.exp(-(cumf[:, h:h + 1] + mu_col))
            h_out = num / jnp.maximum(jnp.abs(den), floor)

            wg = jnp.exp(a_col - mu_last_h)
            decay = jnp.exp(m_prev_h - mu_last_h)
            kw = k * wg
            c_scr[state] = decay * c_h + lax.dot_general(kw.astype(BF16), vb, (((0,), (0,)), ((), ())),
                                                         preferred_element_type=F32)
            n_scr[state] = decay * n_h + jnp.sum(kw, axis=0, keepdims=True)

            gated = _sigmoid(o_ref[bb, rows, cols]) * h_out
            y_ref[bb, rows, cols] = _head_norm(gated, nw_ref[:, cols])
        m_scr[bb] = cumf[chunk - 1:chunk, :] + mu_last


def _mlstm(proj, gates, gate_b, norm_w, *, batch, seq, width, col0):
    t = proj.shape[0]
    heads = width // HEAD_DIM
    nb = SCAN_BATCH if batch % SCAN_BATCH == 0 else 1
    proj3 = proj.reshape(batch, seq, proj.shape[1])
    gates3 = gates.reshape(batch, seq, LANES)
    blk = lambda col: pl.BlockSpec((nb, SCAN_BLOCK, width), lambda b, i: (b, i, col))
    y = pl.pallas_call(
        functools.partial(_mlstm_kernel, chunk=SCAN_CHUNK, heads=heads),
        grid=(batch // nb, seq // SCAN_BLOCK),
        in_specs=[blk(col0), blk(col0 + 1), blk(col0 + 2), blk(col0 + 3),
                  pl.BlockSpec((nb, SCAN_BLOCK, LANES), lambda b, i: (b, i, 0)),
                  pl.BlockSpec((1, LANES), lambda b, i: (0, 0)),
                  pl.BlockSpec((1, width), lambda b, i: (0, 0))],
        out_specs=pl.BlockSpec((nb, SCAN_BLOCK, width), lambda b, i: (b, i, 0)),
        out_shape=jax.ShapeDtypeStruct((batch, seq, width), F32),
        scratch_shapes=[pltpu.VMEM((nb * heads, HEAD_DIM, HEAD_DIM), F32),
                        pltpu.VMEM((nb * heads, 1, HEAD_DIM), F32),
                        pltpu.VMEM((nb, 1, LANES), F32)],
        compiler_params=_cp("parallel", "arbitrary"),
    )(proj3, proj3, proj3, proj3, gates3, gate_b, norm_w)
    return y.reshape(t, width)


def _ret_kernel(q_ref, k_ref, v_ref, g_ref, cos_ref, sin_ref, intra_ref, cross_ref, zeta_ref, nw_ref,
                y_ref, r_scr, *, chunk, heads, chunk_decay):
    @pl.when(pl.program_id(1) == 0)
    def _():
        r_scr[...] = jnp.zeros_like(r_scr)

    tb = q_ref.shape[0]
    hd = HEAD_DIM
    scale = hd ** -0.5
    for c in range(tb // chunk):
        rows = slice(c * chunk, (c + 1) * chunk)
        cos = cos_ref[rows, :]
        sin = sin_ref[rows, :]
        for h in range(heads):
            cols = slice(h * hd, (h + 1) * hd)
            q = q_ref[rows, cols]
            k = k_ref[rows, cols]
            q = q * cos + pltpu.roll(q, hd // 2, axis=1) * sin
            k = (k * cos + pltpu.roll(k, hd // 2, axis=1) * sin) * scale
            vb = v_ref[rows, cols].astype(BF16)
            qb = q.astype(BF16)
            s = lax.dot_general(qb, k.astype(BF16), (((1,), (1,)), ((), ())), preferred_element_type=F32)
            inner = jnp.dot((s * intra_ref[h]).astype(BF16), vb, preferred_element_type=F32)
            r_h = r_scr[h]
            crs = jnp.dot(qb, r_h.astype(BF16), preferred_element_type=F32) * cross_ref[h]
            r_scr[h] = chunk_decay[h] * r_h + lax.dot_general(
                (k * zeta_ref[h]).astype(BF16), vb, (((0,), (0,)), ((), ())), preferred_element_type=F32)
            y_ref[rows, cols] = _silu(g_ref[rows, cols]) * _head_norm(inner + crs, nw_ref[:, cols])


def _retention(proj, norm_w, *, batch, seq, width, col0):
    t = proj.shape[0]
    heads = width // HEAD_DIM
    chunk = SCAN_CHUNK
    n_tblk = seq // SCAN_BLOCK
    half = HEAD_DIM // 2
    inv = np.float32(ROPE_BASE) ** (-np.arange(half, dtype=np.float32) / np.float32(half))
    ang = (np.arange(seq, dtype=np.float32)[:, None] * inv[None, :]).astype(np.float64)
    cos_t = jnp.asarray(np.concatenate([np.cos(ang), np.cos(ang)], axis=-1), F32)
    sin_t = jnp.asarray(np.concatenate([-np.sin(ang), np.sin(ang)], axis=-1), F32)
    log_g = jnp.log(1.0 - 2.0 ** (-5.0 - jnp.arange(heads, dtype=F32)))
    tt = jnp.arange(chunk, dtype=F32)
    lag = tt[:, None] - tt[None, :]
    intra = jnp.where(lag >= 0, jnp.exp(jnp.maximum(lag, 0.0)[None] * log_g[:, None, None]), 0.0)
    cross = jnp.broadcast_to(jnp.exp((tt + 1.0)[None, :] * log_g[:, None])[:, :, None], (heads, chunk, HEAD_DIM))
    zeta = jnp.broadcast_to(jnp.exp((chunk - 1.0 - tt)[None, :] * log_g[:, None])[:, :, None],
                            (heads, chunk, HEAD_DIM))
    chunk_decay = tuple(float((1.0 - 2.0 ** (-5.0 - h)) ** chunk) for h in range(heads))

    blk = lambda col: pl.BlockSpec((SCAN_BLOCK, width), lambda b, i: (b * n_tblk + i, col))
    pos = pl.BlockSpec((SCAN_BLOCK, HEAD_DIM), lambda b, i: (i, 0))
    full3 = lambda a: pl.BlockSpec(a.shape, lambda b, i: (0, 0, 0))
    return pl.pallas_call(
        functools.partial(_ret_kernel, chunk=chunk, heads=heads, chunk_decay=chunk_decay),
        grid=(batch, n_tblk),
        in_specs=[blk(col0), blk(col0 + 1), blk(col0 + 2), blk(col0 + 3), pos, pos,
                  full3(intra), full3(cross), full3(zeta),
                  pl.BlockSpec((1, width), lambda b, i: (0, 0))],
        out_specs=pl.BlockSpec((SCAN_BLOCK, width), lambda b, i: (b * n_tblk + i, 0)),
        out_shape=jax.ShapeDtypeStruct((t, width), F32),
        scratch_shapes=[pltpu.VMEM((heads, HEAD_DIM, HEAD_DIM), F32)],
        compiler_params=_cp("parallel", "arbitrary"),
    )(proj, proj, proj, proj, cos_t, sin_t, intra, cross, zeta, norm_w)


def _merge_kernel(x_ref, *refs):
    y_refs, wg_refs = refs[:N_BRANCH], refs[N_BRANCH:2 * N_BRANCH]
    wb_ref, o_ref, xb_ref = refs[2 * N_BRANCH:]

    @pl.when(pl.program_id(1) == 0)
    def _():
        xb_ref[...] = x_ref[...].astype(BF16)

    xb = xb_ref[...]
    acc = None
    for n in range(N_BRANCH):
        gate = _sigmoid(jnp.dot(xb, wg_refs[n][...], preferred_element_type=F32))
        term = gate * jnp.dot(y_refs[n][...].astype(BF16), wb_ref[n], preferred_element_type=F32)
        acc = term if acc is None else acc + term
    o_ref[...] = acc


def _merge(x, branches, w_gate, gate_col0, w_branch, *, tm, tn):
    t, d = x.shape
    width = branches[0].shape[1]
    ybs = pl.BlockSpec((tm, width), lambda i, j: (i, 0))

    def gate_spec(n):
        return pl.BlockSpec((d, tn), lambda i, j: (0, (gate_col0 + n * d) // tn + j))

    return pl.pallas_call(
        _merge_kernel,
        grid=(t // tm, d // tn),
        in_specs=[pl.BlockSpec((tm, d), lambda i, j: (i, 0))] + [ybs] * N_BRANCH
                 + [gate_spec(n) for n in range(N_BRANCH)]
                 + [pl.BlockSpec((N_BRANCH, width, tn), lambda i, j: (0, 0, j))],
        out_specs=pl.BlockSpec((tm, tn), lambda i, j: (i, j)),
        out_shape=jax.ShapeDtypeStruct((t, d), F32),
        scratch_shapes=[pltpu.VMEM((tm, d), BF16)],
        compiler_params=_cp("parallel", "arbitrary"),
    )(x, *branches, *([w_gate] * N_BRANCH), w_branch)


def _proj_ln_kernel(a_ref, w_ref, r_ref, lw_ref, lb_ref, o_ref, *, alpha):
    y = jnp.dot(a_ref[...].astype(BF16), w_ref[...], preferred_element_type=F32)
    o_ref[...] = _layer_norm(alpha * r_ref[...] + y, lw_ref[...], lb_ref[...])


def _proj_ln(a, w, resid, ln_w, ln_b, *, alpha, tm):
    t, k = a.shape
    d = w.shape[1]
    row = lambda n: pl.BlockSpec((tm, n), lambda i: (i, 0))
    const = lambda shape: pl.BlockSpec(shape, lambda i: (0, 0))
    return pl.pallas_call(
        functools.partial(_proj_ln_kernel, alpha=alpha),
        grid=(t // tm,),
        in_specs=[row(k), const((k, d)), row(d), const((1, d)), const((1, d))],
        out_specs=row(d),
        out_shape=jax.ShapeDtypeStruct((t, d), F32),
        compiler_params=_cp("parallel"),
    )(a, w, resid, ln_w, ln_b)


def _xattn_kernel(x_ref, wq_ref, kv_ref, wo_ref, lw_ref, lb_ref, o_ref, *, alpha, heads):
    x = x_ref[...]
    hd = HEAD_DIM
    inner = heads * hd
    q = jnp.dot(x.astype(BF16), wq_ref[...], preferred_element_type=F32)
    outs = []
    for h in range(heads):
        qh = q[:, h * hd:(h + 1) * hd].astype(BF16)
        kh = kv_ref[:, h * hd:(h + 1) * hd].astype(BF16)
        vh = kv_ref[:, inner + h * hd:inner + (h + 1) * hd].astype(BF16)
        s = lax.dot_general(qh, kh, (((1,), (1,)), ((), ())), preferred_element_type=F32) * hd ** -0.5
        s = s - jnp.max(s, axis=-1, keepdims=True)
        e = jnp.exp(s)
        p = e / jnp.sum(e, axis=-1, keepdims=True)
        outs.append(jnp.dot(p.astype(BF16), vh, preferred_element_type=F32).astype(BF16))
    o = jnp.concatenate(outs, axis=-1)
    y = jnp.dot(o, wo_ref[...], preferred_element_type=F32)
    o_ref[...] = _layer_norm(alpha * x + y, lw_ref[...], lb_ref[...])


def _xattn(x, kv, wq, wo, ln_w, ln_b, *, alpha, seq, mem_len, tm):
    t, d = x.shape
    inner = wq.shape[1]
    n_tblk = seq // tm
    const = lambda shape: pl.BlockSpec(shape, lambda i: (0, 0))
    return pl.pallas_call(
        functools.partial(_xattn_kernel, alpha=alpha, heads=XATTN_HEADS),
        grid=(t // tm,),
        in_specs=[pl.BlockSpec((tm, d), lambda i: (i, 0)), const((d, inner)),
                  pl.BlockSpec((mem_len, 2 * inner), lambda i: (i // n_tblk, 0)),
                  const((inner, d)), const((1, d)), const((1, d))],
        out_specs=pl.BlockSpec((tm, d), lambda i: (i, 0)),
        out_shape=jax.ShapeDtypeStruct((t, d), F32),
        compiler_params=_cp("parallel"),
    )(x, wq, kv, wo, ln_w, ln_b)


def _round_up_pow2(x, m):
    shift = m.bit_length() - 1
    return jnp.left_shift(jnp.right_shift(x + (m - 1), shift), shift)


def _route_kernel(x_ref, wr_ref, rb_ref, w_ref, lrow_ref, segtab_ref, blk_ref, seg_ref, size_all, tot, *, tm, bm):
    step = pl.program_id(0)

    @pl.when(step == 0)
    def _():
        tot[...] = jnp.zeros_like(tot)

    e_n, g_n = N_EXPERTS, N_GROUPS
    per = e_n // g_n
    def split(a):
        hi = a.astype(BF16)
        return hi, (a - hi.astype(F32)).astype(BF16)

    def nt(a, b):
        return lax.dot_general(a, b, (((1,), (1,)), ((), ())), preferred_element_type=F32)

    (w_hi, w_lo), (x_hi, x_lo) = split(wr_ref[...]), split(x_ref[...])
    logits = nt(w_hi, x_hi) + (nt(w_hi, x_lo) + nt(w_lo, x_hi))
    scores = _sigmoid(logits)
    biased = scores + rb_ref[...]
    b3 = biased.reshape(g_n, per, tm)
    member = lax.broadcasted_iota(I32, (g_n, per, tm), 1)
    top1 = jnp.max(b3, axis=1, keepdims=True)
    first = jnp.min(jnp.where(b3 == top1, member, per), axis=1, keepdims=True)
    top2 = jnp.max(jnp.where(member == first, -jnp.inf, b3), axis=1, keepdims=True)
    gs = top1 + top2
    gid = lax.broadcasted_iota(I32, (g_n, 1, tm), 0)
    rank = jnp.zeros((g_n, 1, tm), I32)
    for other in range(g_n):
        o = gs[other:other + 1]
        ahead = jnp.logical_or(o > gs, jnp.logical_and(o == gs, other < gid))
        rank = rank + jnp.where(ahead, 1, 0)
    cur = jnp.where(rank < TOPK_GROUPS, b3, -jnp.inf).reshape(e_n, tm)

    eid = lax.broadcasted_iota(I32, (e_n, tm), 0)
    picks, vals = [], []
    sel = jnp.zeros((e_n, tm), F32)
    for k in range(TOP_K):
        mx = jnp.max(cur, axis=0, keepdims=True)
        ik = jnp.min(jnp.where(cur == mx, eid, e_n), axis=0, keepdims=True)
        hit = eid == ik
        vals.append(jnp.sum(jnp.where(hit, scores, 0.0), axis=0, keepdims=True))
        cur = jnp.where(hit, -jnp.inf, cur)
        sel = jnp.where(hit, 1.0, sel)
        picks.append(ik)
    total = vals[0]
    for v in vals[1:]:
        total = total + v

    tri = jnp.where(lax.broadcasted_iota(I32, (tm, tm), 0) <= lax.broadcasted_iota(I32, (tm, tm), 1), 1.0, 0.0)
    incl = jnp.dot(sel.astype(BF16), tri.astype(BF16), preferred_element_type=F32)
    size = _round_up_pow2(jnp.broadcast_to(incl[:, tm - 1:tm], (e_n, LANES)).astype(I32), SEG_ALIGN)
    loff = _cum_rows(size, jnp.add, 0) - size
    base = loff[:, 0:1].astype(F32) + incl - 1.0
    for k in range(TOP_K):
        w_ref[k:k + 1, :] = vals[k] / total * ROUTE_SCALE
        lrow_ref[0, k:k + 1, :] = jnp.sum(jnp.where(eid == picks[k], base, 0.0),
                                          axis=0, keepdims=True).astype(I32)
    size_all[step] = size
    tot[...] = tot[...] + size

    @pl.when(step == pl.num_programs(0) - 1)
    def _():
        rows = tot[...]
        pcnt = _round_up_pow2(rows, bm)
        pend = _cum_rows(pcnt, jnp.add, 0)

        def tile_seg(i, run):
            segtab_ref[i, 0] = run
            segtab_ref[i, 1] = size_all[i]
            return run + size_all[i]

        data_start = pend - rows
        lax.fori_loop(0, pl.num_programs(0), tile_seg, data_start)

        nb = blk_ref.shape[2]
        row0 = lax.broadcasted_iota(I32, (e_n, nb), 1) * bm
        total_rows = pend[e_n - 1:e_n, 0:1]
        owner = jnp.sum(jnp.where(pend[:, 0:1] <= row0, 1, 0), axis=0, keepdims=True)
        last_owner = jnp.sum(jnp.where(pend[:, 0:1] < total_rows, 1, 0), axis=0, keepdims=True)
        blk_ref[0] = jnp.where(row0[0:1, :] < total_rows, jnp.minimum(owner, e_n - 1), last_owner)
        inside = jnp.logical_and(pend[:, 0:1] - pcnt[:, 0:1] <= row0, row0 < pend[:, 0:1])
        blk_ref[1] = jnp.sum(jnp.where(inside, jnp.clip(row0 + bm - data_start[:, 0:1], 0, bm), 0),
                             axis=0, keepdims=True)
        seg_ref[0] = pend
        seg_ref[1] = pcnt


def _route(x, router_w_t, router_b, *, tm, bm, n_blocks):
    t, d = x.shape
    e_n = N_EXPERTS
    n_t = t // tm
    assert bm & (bm - 1) == 0
    nb_pad = -(-n_blocks // LANES) * LANES
    return pl.pallas_call(
        functools.partial(_route_kernel, tm=tm, bm=bm),
        grid=(n_t,),
        in_specs=[pl.BlockSpec((tm, d), lambda i: (i, 0)),
                  pl.BlockSpec((e_n, d), lambda i: (0, 0)),
                  pl.BlockSpec((e_n, 1), lambda i: (0, 0))],
        out_specs=[pl.BlockSpec((TOP_K, tm), lambda i: (0, i)),
                   pl.BlockSpec((1, TOP_K, tm), lambda i: (i, 0, 0)),
                   pl.BlockSpec((n_t, 2, e_n, LANES), lambda i: (0, 0, 0, 0)),
                   pl.BlockSpec((2, 1, nb_pad), lambda i: (0, 0, 0)),
                   pl.BlockSpec((2, e_n, LANES), lambda i: (0, 0, 0))],
        out_shape=[jax.ShapeDtypeStruct((TOP_K, t), F32), jax.ShapeDtypeStruct((n_t, TOP_K, tm), I32),
                   jax.ShapeDtypeStruct((n_t, 2, e_n, LANES), I32), jax.ShapeDtypeStruct((2, 1, nb_pad), I32),
                   jax.ShapeDtypeStruct((2, e_n, LANES), I32)],
        scratch_shapes=[pltpu.VMEM((n_t, e_n, LANES), I32), pltpu.VMEM((e_n, LANES), I32)],
        compiler_params=_cp("arbitrary"),
    )(x, router_w_t, router_b)


def _tile_rows(tm):
    worst = TOP_K * tm + N_EXPERTS * (SEG_ALIGN - 1)
    return -(-worst // ONEHOT_ROWS) * ONEHOT_ROWS


def _onehot_rows(chunk, lrow, values, tm):
    rid = chunk * ONEHOT_ROWS + lax.broadcasted_iota(I32, (ONEHOT_ROWS, tm), 0)
    acc = jnp.zeros((ONEHOT_ROWS, tm), F32)
    for k in range(TOP_K):
        acc = jnp.where(rid == lrow[k:k + 1, :], 1.0 if values is None else values[k:k + 1, :], acc)
    return acc.astype(BF16)


def _start_segments(gstart_ref, size_ref, tile, make_copy, first=0, last=N_EXPERTS, row0=0):
    def body(e, loff):
        n = size_ref[tile * N_EXPERTS + e]

        @pl.when(n > 0)
        def _():
            make_copy(pl.multiple_of(gstart_ref[tile * N_EXPERTS + e], SEG_ALIGN),
                      pl.multiple_of(loff, SEG_ALIGN), pl.multiple_of(n, SEG_ALIGN)).start()
        return loff + n
    return lax.fori_loop(first, last, body, row0)


def _segments_within(size_ref, tile, limit):
    def body(e, carry):
        count, rows = carry
        end = rows + size_ref[tile * N_EXPERTS + e]
        fits = jnp.logical_and(count == e, end <= limit)
        return count + jnp.where(fits, 1, 0), jnp.where(fits, end, rows)
    return lax.fori_loop(0, N_EXPERTS, body, (0, 0))


def _tile_total(size_ref, tile):
    return lax.fori_loop(0, N_EXPERTS, lambda e, s: s + size_ref[tile * N_EXPERTS + e], 0)


def _wait_rows(make_copy, rows):
    @pl.when(rows > 0)
    def _():
        make_copy(0, 0, pl.multiple_of(rows, SEG_ALIGN)).wait()


def _dispatch_kernel(gstart_ref, size_ref, x_ref, lrow_ref, xs_ref, stage, inflight, sem_a, sem_b, *, tm):
    i = pl.program_id(0)
    last = pl.num_programs(0) - 1
    n_chunks = stage.shape[0] // ONEHOT_ROWS
    split_chunk = n_chunks // 2
    split = split_chunk * ONEHOT_ROWS
    assert tm <= ONEHOT_ROWS

    def copy_on(sem):
        return lambda g, loff, n: pltpu.make_async_copy(stage.at[pl.ds(loff, n), :], xs_ref.at[pl.ds(g, n), :], sem)

    xb = x_ref[...].astype(BF16)
    lrow = lrow_ref[0]
    n_rows = _tile_total(size_ref, i)

    def chunks(lo, hi):
        for c in range(lo, hi):
            def one(c=c):
                stage[c * ONEHOT_ROWS:(c + 1) * ONEHOT_ROWS, :] = jnp.dot(
                    _onehot_rows(c, lrow, None, tm), xb, preferred_element_type=F32).astype(BF16)
            if c * ONEHOT_ROWS < TOP_K * tm:
                one()
            else:
                pl.when(c * ONEHOT_ROWS < n_rows)(one)

    @pl.when(i == 0)
    def _():
        inflight[0] = 0
        inflight[1] = 0

    _wait_rows(copy_on(sem_a), inflight[0])
    chunks(0, split_chunk - 1)
    _wait_rows(copy_on(sem_b), inflight[1])
    chunks(split_chunk - 1, split_chunk)
    experts_a, rows_a = _segments_within(size_ref, i, split)
    _start_segments(gstart_ref, size_ref, i, copy_on(sem_a), 0, experts_a)
    chunks(split_chunk, n_chunks)
    rows_b = _start_segments(gstart_ref, size_ref, i, copy_on(sem_b), experts_a, N_EXPERTS, rows_a) - rows_a
    inflight[0] = rows_a
    inflight[1] = rows_b

    @pl.when(i == last)
    def _():
        _wait_rows(copy_on(sem_a), rows_a)
        _wait_rows(copy_on(sem_b), rows_b)


def _dispatch(x, lrow, gstart, size, *, rows, tm):
    t, d = x.shape
    return pl.pallas_call(
        functools.partial(_dispatch_kernel, tm=tm),
        grid_spec=pltpu.PrefetchScalarGridSpec(
            num_scalar_prefetch=2,
            grid=(t // tm,),
            in_specs=[pl.BlockSpec((tm, d), lambda i, *_: (i, 0)),
                      pl.BlockSpec((1, TOP_K, tm), lambda i, *_: (i, 0, 0))],
            out_specs=pl.BlockSpec(memory_space=pl.ANY),
            scratch_shapes=[pltpu.VMEM((_tile_rows(tm), d), BF16), pltpu.SMEM((2,), I32)]
                           + [pltpu.SemaphoreType.DMA(())] * 2,
        ),
        out_shape=jax.ShapeDtypeStruct((rows, d), BF16),
        compiler_params=_cp("arbitrary"),
    )(gstart, size, x, lrow)


def _expert_block_kernel(blk_ref, nused_ref, xs_ref, wgu_ref, wdn_ref, ys_ref, wgu_b, wdn_b, *, n_blocks, bm):
    j = pl.program_id(0)
    valid = blk_ref[n_blocks + j]
    changed = jnp.logical_or(j == 0, blk_ref[j] != blk_ref[jnp.maximum(j - 1, 0)])

    @pl.when(jnp.logical_and(valid > 0, changed))
    def _():
        wgu_b[...] = wgu_ref[...].astype(BF16)
        wdn_b[...] = wdn_ref[...].astype(BF16)

    def swiglu(rows, masked):
        first = bm - rows
        x = xs_ref[first:bm, :]
        if masked:
            x = jnp.where(first + lax.broadcasted_iota(I32, x.shape, 0) >= bm - valid, x, jnp.zeros_like(x))
        f = wdn_b.shape[0]
        gu = jnp.dot(x, wgu_b[...], preferred_element_type=F32)
        hidden = (_silu(gu[:, :f]) * gu[:, f:]).astype(BF16)
        ys_ref[first:bm, :] = jnp.dot(hidden, wdn_b[...], preferred_element_type=F32).astype(BF16)

    @pl.when(valid == bm)
    def _():
        swiglu(bm, False)

    quarter = bm // 4
    for q in range(1, 5):
        @pl.when(jnp.logical_and(jnp.logical_and(valid > (q - 1) * quarter, valid <= q * quarter), valid < bm))
        def _():
            swiglu(q * quarter, True)


def _experts_blocked(xs, blk, nused, w_gu, w_dn, layer, *, n_blocks, bm):
    rows, d = xs.shape
    f2 = w_gu.shape[3]
    f = w_dn.shape[2]
    row_blk = lambda j, bl, nu: (jnp.minimum(j, nu[0] - 1), 0)
    return pl.pallas_call(
        functools.partial(_expert_block_kernel, n_blocks=n_blocks, bm=bm),
        grid_spec=pltpu.PrefetchScalarGridSpec(
            num_scalar_prefetch=2,
            grid=(n_blocks,),
            in_specs=[pl.BlockSpec((bm, d), row_blk),
                      pl.BlockSpec((None, None, d, f2), lambda j, bl, nu: (layer, bl[j], 0, 0)),
                      pl.BlockSpec((None, None, f, d), lambda j, bl, nu: (layer, bl[j], 0, 0))],
            out_specs=pl.BlockSpec((bm, d), row_blk),
            scratch_shapes=[pltpu.VMEM((d, f2), BF16), pltpu.VMEM((f, d), BF16)],
        ),
        out_shape=jax.ShapeDtypeStruct((rows, d), BF16),
        compiler_params=_cp("arbitrary"),
    )(blk, nused, xs, w_gu, w_dn)


def _combine_tile_kernel(gstart_ref, size_ref, x_ref, lrow_ref, w_ref, sdn_ref, lw_ref, lb_ref, sgu_hbm, ys_ref,
                         o_ref, ybuf, sgu, sems, wsem, *, tm, alpha):
    i = pl.program_id(0)
    last_tile = pl.num_programs(0) - 1
    n_chunks = ybuf.shape[1] // ONEHOT_ROWS
    sure_chunks = (TOP_K * tm) // ONEHOT_ROWS

    def contract_rows(w_rows, y_rows):
        return lax.dot_general(w_rows, y_rows, (((0,), (0,)), ((), ())), preferred_element_type=F32)

    def copy_to(slot):
        return lambda g, loff, n: pltpu.make_async_copy(ys_ref.at[pl.ds(g, n), :], ybuf.at[slot, pl.ds(loff, n), :],
                                                        sems.at[slot])

    def fetch(tile, slot):
        _start_segments(gstart_ref, size_ref, tile, copy_to(slot))

    n_rows = _tile_total(size_ref, i)

    @pl.when(i == 0)
    def _():
        ybuf[...] = jnp.zeros_like(ybuf)
        weights = pltpu.make_async_copy(sgu_hbm, sgu, wsem)
        weights.start()
        weights.wait()
        fetch(0, 0)

    for slot in range(2):
        @pl.when(i % 2 == slot)
        def _():
            @pl.when(i < last_tile)
            def _():
                fetch(i + 1, 1 - slot)

            x = x_ref[...]
            f = sdn_ref.shape[0]
            gu = jnp.dot(x.astype(BF16), sgu[...], preferred_element_type=F32)
            hidden = (_silu(gu[:, :f]) * gu[:, f:]).astype(BF16)
            acc = jnp.dot(hidden, sdn_ref[...], preferred_element_type=F32)
            _wait_rows(copy_to(slot), n_rows)
            lrow, w = lrow_ref[0], w_ref[...]
            for c in range(sure_chunks):
                rows = slice(c * ONEHOT_ROWS, (c + 1) * ONEHOT_ROWS)
                acc = acc + contract_rows(_onehot_rows(c, lrow, w, tm), ybuf[slot, rows, :])
            o_ref[...] = acc
            for c in range(sure_chunks, n_chunks):
                @pl.when(c * ONEHOT_ROWS < n_rows)
                def _():
                    rows = slice(c * ONEHOT_ROWS, (c + 1) * ONEHOT_ROWS)
                    o_ref[...] += contract_rows(_onehot_rows(c, lrow, w, tm), ybuf[slot, rows, :])
            o_ref[...] = _layer_norm(alpha * x + o_ref[...], lw_ref[...], lb_ref[...])


def _combine(x, ys, lrow, wts, gstart, size, s_gu, s_dn, ln_w, ln_b, *, alpha, tm):
    t, d = x.shape
    const = lambda shape: pl.BlockSpec(shape, lambda i, *_: (0, 0))
    return pl.pallas_call(
        functools.partial(_combine_tile_kernel, tm=tm, alpha=alpha),
        grid_spec=pltpu.PrefetchScalarGridSpec(
            num_scalar_prefetch=2,
            grid=(t // tm,),
            in_specs=[pl.BlockSpec((tm, d), lambda i, *_: (i, 0)),
                      pl.BlockSpec((1, TOP_K, tm), lambda i, *_: (i, 0, 0)),
                      pl.BlockSpec((TOP_K, tm), lambda i, *_: (0, i)),
                      const(s_dn.shape), const((1, d)), const((1, d)),
                      pl.BlockSpec(memory_space=pl.ANY), pl.BlockSpec(memory_space=pl.ANY)],
            out_specs=pl.BlockSpec((tm, d), lambda i, *_: (i, 0)),
            scratch_shapes=[pltpu.VMEM((2, _tile_rows(tm), d), BF16), pltpu.VMEM(s_gu.shape, BF16),
                            pltpu.SemaphoreType.DMA((2,)), pltpu.SemaphoreType.DMA(())],
        ),
        out_shape=jax.ShapeDtypeStruct((t, d), F32),
        compiler_params=_cp("arbitrary"),
    )(gstart, size, x, lrow, wts, s_dn, ln_w, ln_b, s_gu, ys)


def _mixer_sublayer(x, w_in_all, layer, gate_b, pool_w, pool_scale, conv_w, mlstm_norm_w, ret_norm_w, w_branch,
                    w_out, ln_w, ln_b, *, batch, seq, alpha):
    t, d = x.shape
    width = d // N_BRANCH
    heads = width // HEAD_DIM
    gate_off = 8 * width
    ret_off = gate_off
    g_off = ret_off + 4 * width
    if_off = g_off + N_BRANCH * d
    w_bf16 = _realign_cast(w_in_all, layer, lo_col=gate_off, hi_col=if_off, shift=2 * heads, tr=512, tn=512)
    gate_bias = jnp.pad(gate_b, (0, LANES - 2 * heads)).reshape(1, LANES)

    proj = _matmul(x, w_bf16, tm=1024, tn=1024, ncols=g_off)
    gates = _matmul(x, w_bf16, tm=1024, tn=LANES, ncols=LANES, col0=if_off)
    y_pool, y_conv = _pool_conv(proj, pool_w.astype(BF16), pool_scale.reshape(1, width), conv_w,
                                seq=seq, width=width, tb=512)
    y_mlstm = _mlstm(proj, gates, gate_bias, mlstm_norm_w.reshape(1, width),
                     batch=batch, seq=seq, width=width, col0=4)
    y_ret = _retention(proj, ret_norm_w.reshape(1, width), batch=batch, seq=seq, width=width,
                       col0=ret_off // width)
    merged = _merge(x, (y_pool, y_conv, y_mlstm, y_ret), w_bf16, g_off, w_branch.astype(BF16),
                    tm=512, tn=512)
    return _proj_ln(merged, w_out.astype(BF16), x, ln_w, ln_b, alpha=alpha, tm=512)


def _xattn_sublayer(x, mem2d, wq, wk, wv, wo, ln_w, ln_b, *, seq, mem_len, alpha):
    w_kv = jnp.concatenate([wk, wv], axis=1).astype(BF16)
    kv = _matmul(mem2d, w_kv, tm=min(mem2d.shape[0], 1024), tn=512)
    return _xattn(x, kv, wq.astype(BF16), wo.astype(BF16), ln_w, ln_b,
                  alpha=alpha, seq=seq, mem_len=mem_len, tm=512)


def _moe_sublayer(x, router_w, router_b, w_gu, w_dn, layer, s_gu, s_dn, ln_w, ln_b, *, alpha):
    t, d = x.shape
    e_n, bm = N_EXPERTS, MOE_BM
    tm = ROUTE_TM
    n_blocks = -(-(t * TOP_K + (t // tm) * e_n * (SEG_ALIGN - 1)) // bm) + e_n
    wts, lrow, segtab, blk, seg = _route(x, router_w.T, router_b.reshape(e_n, 1), tm=tm, bm=bm, n_blocks=n_blocks)
    gstart, size = segtab[:, 0, :, 0].reshape(-1), segtab[:, 1, :, 0].reshape(-1)
    nused = seg[0, e_n - 1, 0] // bm

    xs = _dispatch(x, lrow, gstart, size, rows=n_blocks * bm, tm=tm)
    ys = _experts_blocked(xs, blk[:, 0, :n_blocks].reshape(-1), nused.reshape(1), w_gu, w_dn, layer,
                          n_blocks=n_blocks, bm=bm)
    return _combine(x, ys, lrow, wts, gstart, size, s_gu.astype(BF16), s_dn.astype(BF16), ln_w, ln_b,
                    alpha=alpha, tm=tm)


def kernel(x, mem, w_in, mlstm_gate_b, pool_w, pool_scale, conv_w, mlstm_norm_w, ret_norm_w, w_branch,
           w_mix_out, xa_wq, xa_wk, xa_wv, xa_wo, router_w, router_b, moe_w_gu, moe_w_dn, shared_w_gu,
           shared_w_dn, ln_w, ln_b):
    batch, seq, d = x.shape
    depth = w_in.shape[0]
    mem_len = mem.shape[1]
    alpha = (2 * depth) ** 0.25
    h = x.reshape(batch * seq, d)
    mem2d = mem.reshape(batch * mem_len, d)
    for l in range(depth):
        lw = ln_w[l].reshape(3, 1, d)
        lb = ln_b[l].reshape(3, 1, d)
        h = _mixer_sublayer(h, w_in, l, mlstm_gate_b[l], pool_w[l], pool_scale[l], conv_w[l], mlstm_norm_w[l],
                            ret_norm_w[l], w_branch[l], w_mix_out[l], lw[0], lb[0],
                            batch=batch, seq=seq, alpha=alpha)
        h = _xattn_sublayer(h, mem2d, xa_wq[l], xa_wk[l], xa_wv[l], xa_wo[l], lw[1], lb[1],
                            seq=seq, mem_len=mem_len, alpha=alpha)
        h = _moe_sublayer(h, router_w[l], router_b[l], moe_w_gu, moe_w_dn, l, shared_w_gu[l], shared_w_dn[l],
                          lw[2], lb[2], alpha=alpha)
    return h.reshape(batch, seq, d)
```

```python
import functools

import numpy as np
import jax
import jax.numpy as jnp
from jax import lax
from jax.experimental import pallas as pl
from jax.experimental.pallas import tpu as pltpu

F32 = jnp.float32
BF16 = jnp.bfloat16
I32 = jnp.int32

N_BRANCH = 4
HEAD_DIM = 128
POOL_WINDOWS = (2, 4, 8, 16)
CONV_WIDTH = 3
ROPE_BASE = 10000.0
XATTN_HEADS = 4
N_EXPERTS = 64
TOP_K = 8
N_GROUPS = 8
TOPK_GROUPS = 4
ROUTE_SCALE = 2.5
LN_EPS = 1e-5

LANES = 128
V7X_VMEM_BYTES = 64 * 1024 * 1024
VMEM_LIMIT = 56 * 1024 * 1024

SCAN_CHUNK = 256
SCAN_BLOCK = 512
SCAN_BATCH = 2
HALO = 16
MOE_BM = 1024
ROUTE_TM = 256
SEG_ALIGN = 16
ONEHOT_ROWS = 512

def _cp(*sem):
    return pltpu.CompilerParams(dimension_semantics=sem, vmem_limit_bytes=VMEM_LIMIT)


def _sigmoid(x):
    return 1.0 / (1.0 + jnp.exp(-x))


def _silu(x):
    return x * _sigmoid(x)


def _log_sigmoid(x):
    return jnp.minimum(x, 0.0) - jnp.log(1.0 + jnp.exp(-jnp.abs(x)))


def _layer_norm(z, w, b):
    mu = jnp.mean(z, axis=-1, keepdims=True)
    d = z - mu
    var = jnp.mean(d * d, axis=-1, keepdims=True)
    return d * lax.rsqrt(var + LN_EPS) * w + b


def _head_norm(h, w):
    mu = jnp.mean(h, axis=-1, keepdims=True)
    d = h - mu
    var = jnp.mean(d * d, axis=-1, keepdims=True)
    return d * lax.rsqrt(var + LN_EPS) * w


def _mm_kernel(x_ref, w_ref, o_ref, xb_ref):
    @pl.when(pl.program_id(1) == 0)
    def _():
        xb_ref[...] = x_ref[...].astype(BF16)

    o_ref[...] = jnp.dot(xb_ref[...], w_ref[...], preferred_element_type=F32)


def _realign_cast_kernel(a_ref, b_ref, o_ref, *, shift, lo, hi):
    j = pl.program_id(1)
    tn = o_ref.shape[1]
    shifted = jnp.logical_and(j >= lo, j < hi)

    @pl.when(shifted)
    def _():
        both = jnp.concatenate([a_ref[...], b_ref[...]], axis=1)
        o_ref[...] = both[:, shift:shift + tn].astype(BF16)

    @pl.when(jnp.logical_not(shifted))
    def _():
        o_ref[...] = a_ref[...].astype(BF16)


def _realign_cast(w_all, layer, *, lo_col, hi_col, shift, tr, tn):
    _, rows, _ = w_all.shape
    lo, hi = lo_col // tn, hi_col // tn
    src = lambda j: jnp.where(j == hi, lo, j)
    return pl.pallas_call(
        functools.partial(_realign_cast_kernel, shift=shift, lo=lo, hi=hi),
        grid=(rows // tr, hi + 1),
        in_specs=[pl.BlockSpec((None, tr, tn), lambda i, j: (layer, i, src(j))),
                  pl.BlockSpec((None, tr, LANES), lambda i, j: (layer, i, (src(j) + 1) * (tn // LANES)))],
        out_specs=pl.BlockSpec((tr, tn), lambda i, j: (i, j)),
        out_shape=jax.ShapeDtypeStruct((rows, hi_col + tn), BF16),
        compiler_params=_cp("parallel", "parallel"),
    )(w_all, w_all)


def _matmul(x, w, *, tm, tn, ncols=None, col0=0):
    t, k = x.shape
    n = w.shape[1] if ncols is None else ncols
    return pl.pallas_call(
        _mm_kernel,
        grid=(t // tm, n // tn),
        in_specs=[pl.BlockSpec((tm, k), lambda i, j: (i, 0)),
                  pl.BlockSpec((k, tn), lambda i, j: (0, col0 // tn + j))],
        out_specs=pl.BlockSpec((tm, tn), lambda i, j: (i, j)),
        out_shape=jax.ShapeDtypeStruct((t, n), F32),
        scratch_shapes=[pltpu.VMEM((tm, k), BF16)],
        compiler_params=_cp("parallel", "arbitrary"),
    )(x, w)


def _poolconv_kernel(u_ref, uh_ref, h_ref, hh_ref, b_ref, c_ref, ch_ref, pw_ref, ps_ref, cw_ref,
                     yp_ref, yc_ref, ubuf, zbuf, *, tb, n_tblk):
    first = (pl.program_id(0) % n_tblk) == 0
    ubuf[0:HALO, :] = jnp.where(first, 0.0, uh_ref[...])
    ubuf[HALO:HALO + tb, :] = u_ref[...]
    zbuf[0:HALO, :] = jnp.where(first, 0.0, ch_ref[...] * hh_ref[...])
    zbuf[HALO:HALO + tb, :] = c_ref[...] * h_ref[...]

    t_pos = (pl.program_id(0) % n_tblk) * tb + lax.broadcasted_iota(I32, (tb, LANES), 0)
    gw = u_ref.shape[1] // len(POOL_WINDOWS)
    for grp, win in enumerate(POOL_WINDOWS):
        lanes = slice(grp * gw, (grp + 1) * gw)
        cur = ubuf[HALO:HALO + tb, lanes]
        acc = cur
        for lag in range(1, win):
            acc = acc + ubuf[HALO - lag:HALO - lag + tb, lanes]
        count = jnp.minimum(t_pos + 1, win).astype(F32)
        mixed = acc / count - cur
        y = jnp.dot(mixed.astype(BF16), pw_ref[grp], preferred_element_type=F32)
        yp_ref[:, lanes] = (y * ps_ref[:, lanes]).astype(BF16)

    conv = cw_ref[0:1, :] * zbuf[HALO:HALO + tb, :]
    for lag in range(1, CONV_WIDTH):
        conv = conv + cw_ref[lag:lag + 1, :] * zbuf[HALO - lag:HALO - lag + tb, :]
    yc_ref[...] = (b_ref[...] * conv).astype(BF16)


def _pool_conv(proj, pool_w, pool_scale, conv_w, *, seq, width, tb):
    t = proj.shape[0]
    n_tblk = seq // tb
    ratio = tb // HALO

    def cur(col):
        return pl.BlockSpec((tb, width), lambda g: (g, col))

    def halo(col):
        return pl.BlockSpec((HALO, width), lambda g: (jnp.maximum(g * ratio - 1, 0), col))

    full = lambda shape: pl.BlockSpec(shape, lambda g: (0,) * len(shape))
    return pl.pallas_call(
        functools.partial(_poolconv_kernel, tb=tb, n_tblk=n_tblk),
        grid=(t // tb,),
        in_specs=[cur(0), halo(0), cur(1), halo(1), cur(2), cur(3), halo(3),
                  full(pool_w.shape), full(pool_scale.shape), full(conv_w.shape)],
        out_specs=[pl.BlockSpec((tb, width), lambda g: (g, 0))] * 2,
        out_shape=[jax.ShapeDtypeStruct((t, width), BF16)] * 2,
        scratch_shapes=[pltpu.VMEM((HALO + tb, width), F32)] * 2,
        compiler_params=_cp("parallel"),
    )(proj, proj, proj, proj, proj, proj, proj, pool_w, pool_scale, conv_w)


def _cum_rows(x, op, fill):
    n = x.shape[0]
    row = lax.broadcasted_iota(I32, x.shape, 0)
    shift = 1
    while shift < n:
        x = op(x, jnp.where(row >= shift, pltpu.roll(x, shift, axis=0), fill))
        shift *= 2
    return x


def _mlstm_kernel(q_ref, k_ref, v_ref, o_ref, g_ref, gb_ref, nw_ref, y_ref, c_scr, n_scr, m_scr,
                  *, chunk, heads):
    @pl.when(pl.program_id(1) == 0)
    def _():
        c_scr[...] = jnp.zeros_like(c_scr)
        n_scr[...] = jnp.zeros_like(n_scr)
        m_scr[...] = jnp.zeros_like(m_scr)

    nb, tb = q_ref.shape[0], q_ref.shape[1]
    hd = HEAD_DIM
    scale = hd ** -0.5
    tri = (lax.broadcasted_iota(I32, (chunk, chunk), 0) >= lax.broadcasted_iota(I32, (chunk, chunk), 1))
    for c, bb in [(c, bb) for c in range(tb // chunk) for bb in range(nb)]:
        rows = slice(c * chunk, (c + 1) * chunk)
        gates = g_ref[bb, rows, :] + gb_ref[...]
        lf = _log_sigmoid(pltpu.roll(gates, LANES - heads, axis=1))
        cumf = _cum_rows(lf, jnp.add, 0.0)
        a = gates - cumf
        m_prev = m_scr[bb]
        mu = jnp.maximum(_cum_rows(a, jnp.maximum, -jnp.inf), m_prev)
        mu_last = mu[chunk - 1:chunk, :]
        a_t = a.T
        for h in range(heads):
            cols = slice(h * hd, (h + 1) * hd)
            q = q_ref[bb, rows, cols]
            k = k_ref[bb, rows, cols] * scale
            v = v_ref[bb, rows, cols]
            qb, kb, vb = q.astype(BF16), k.astype(BF16), v.astype(BF16)
            mu_col = mu[:, h:h + 1]
            a_col = a[:, h:h + 1]
            m_prev_h = m_prev[:, h:h + 1]
            mu_last_h = mu_last[:, h:h + 1]
            dmat = jnp.exp(jnp.where(tri, a_t[h:h + 1, :] - mu_col, -jnp.inf))
            s = lax.dot_general(qb, kb, (((1,), (1,)), ((), ())), preferred_element_type=F32)
            p = dmat * s
            inter = jnp.exp(m_prev_h - mu_col)
            state = bb * heads + h
            c_h = c_scr[state]
            n_h = n_scr[state]
            num = inter * jnp.dot(qb, c_h.astype(BF16), preferred_element_type=F32) \
                + jnp.dot(p.astype(BF16), vb, preferred_element_type=F32)
            den = inter * jnp.sum(q * n_h, axis=-1, keepdims=True) + jnp.sum(p, axis=-1, keepdims=True)
            floor = jnp.exp(-(cumf[:, h:h + 1] + mu_col))
            h_out = num / jnp.maximum(jnp.abs(den), floor)

            wg = jnp.exp(a_col - mu_last_h)
            decay = jnp.exp(m_prev_h - mu_last_h)
            kw = k * wg
            c_scr[state] = decay * c_h + lax.dot_general(kw.astype(BF16), vb, (((0,), (0,)), ((), ())),
                                                         preferred_element_type=F32)
            n_scr[state] = decay * n_h + jnp.sum(kw, axis=0, keepdims=True)

            gated = _sigmoid(o_ref[bb, rows, cols]) * h_out
            y_ref[bb, rows, cols] = _head_norm(gated, nw_ref[:, cols]).astype(BF16)
        m_scr[bb] = cumf[chunk - 1:chunk, :] + mu_last


def _mlstm(proj, gates, gate_b, norm_w, *, batch, seq, width, col0):
    t = proj.shape[0]
    heads = width // HEAD_DIM
    nb = SCAN_BATCH if batch % SCAN_BATCH == 0 else 1
    proj3 = proj.reshape(batch, seq, proj.shape[1])
    gates3 = gates.reshape(batch, seq, LANES)
    blk = lambda col: pl.BlockSpec((nb, SCAN_BLOCK, width), lambda b, i: (b, i, col))
    y = pl.pallas_call(
        functools.partial(_mlstm_kernel, chunk=SCAN_CHUNK, heads=heads),
        grid=(batch // nb, seq // SCAN_BLOCK),
        in_specs=[blk(col0), blk(col0 + 1), blk(col0 + 2), blk(col0 + 3),
                  pl.BlockSpec((nb, SCAN_BLOCK, LANES), lambda b, i: (b, i, 0)),
                  pl.BlockSpec((1, LANES), lambda b, i: (0, 0)),
                  pl.BlockSpec((1, width), lambda b, i: (0, 0))],
        out_specs=pl.BlockSpec((nb, SCAN_BLOCK, width), lambda b, i: (b, i, 0)),
        out_shape=jax.ShapeDtypeStruct((batch, seq, width), BF16),
        scratch_shapes=[pltpu.VMEM((nb * heads, HEAD_DIM, HEAD_DIM), F32),
                        pltpu.VMEM((nb * heads, 1, HEAD_DIM), F32),
                        pltpu.VMEM((nb, 1, LANES), F32)],
        compiler_params=_cp("parallel", "arbitrary"),
    )(proj3, proj3, proj3, proj3, gates3, gate_b, norm_w)
    return y.reshape(t, width)


def _ret_kernel(q_ref, k_ref, v_ref, g_ref, cos_ref, sin_ref, intra_ref, cross_ref, zeta_ref, nw_ref,
                y_ref, r_scr, *, chunk, heads, chunk_decay):
    @pl.when(pl.program_id(1) == 0)
    def _():
        r_scr[...] = jnp.zeros_like(r_scr)

    tb = q_ref.shape[0]
    hd = HEAD_DIM
    scale = hd ** -0.5
    for c in range(tb // chunk):
        rows = slice(c * chunk, (c + 1) * chunk)
        cos = cos_ref[rows, :]
        sin = sin_ref[rows, :]
        for h in range(heads):
            cols = slice(h * hd, (h + 1) * hd)
            q = q_ref[rows, cols]
            k = k_ref[rows, cols]
            q = q * cos + pltpu.roll(q, hd // 2, axis=1) * sin
            k = (k * cos + pltpu.roll(k, hd // 2, axis=1) * sin) * scale
            vb = v_ref[rows, cols].astype(BF16)
            qb = q.astype(BF16)
            s = lax.dot_general(qb, k.astype(BF16), (((1,), (1,)), ((), ())), preferred_element_type=F32)
            inner = jnp.dot((s * intra_ref[h]).astype(BF16), vb, preferred_element_type=F32)
            r_h = r_scr[h]
            crs = jnp.dot(qb, r_h.astype(BF16), preferred_element_type=F32) * cross_ref[h]
            r_scr[h] = chunk_decay[h] * r_h + lax.dot_general(
                (k * zeta_ref[h]).astype(BF16), vb, (((0,), (0,)), ((), ())), preferred_element_type=F32)
            y_ref[rows, cols] = (_silu(g_ref[rows, cols]) * _head_norm(inner + crs, nw_ref[:, cols])).astype(BF16)


def _retention(proj, norm_w, *, batch, seq, width, col0):
    t = proj.shape[0]
    heads = width // HEAD_DIM
    chunk = SCAN_CHUNK
    n_tblk = seq // SCAN_BLOCK
    half = HEAD_DIM // 2
    inv = np.float32(ROPE_BASE) ** (-np.arange(half, dtype=np.float32) / np.float32(half))
    ang = (np.arange(seq, dtype=np.float32)[:, None] * inv[None, :]).astype(np.float64)
    cos_t = jnp.asarray(np.concatenate([np.cos(ang), np.cos(ang)], axis=-1), F32)
    sin_t = jnp.asarray(np.concatenate([-np.sin(ang), np.sin(ang)], axis=-1), F32)
    log_g = jnp.log(1.0 - 2.0 ** (-5.0 - jnp.arange(heads, dtype=F32)))
    tt = jnp.arange(chunk, dtype=F32)
    lag = tt[:, None] - tt[None, :]
    intra = jnp.where(lag >= 0, jnp.exp(jnp.maximum(lag, 0.0)[None] * log_g[:, None, None]), 0.0)
    cross = jnp.broadcast_to(jnp.exp((tt + 1.0)[None, :] * log_g[:, None])[:, :, None], (heads, chunk, HEAD_DIM))
    zeta = jnp.broadcast_to(jnp.exp((chunk - 1.0 - tt)[None, :] * log_g[:, None])[:, :, None],
                            (heads, chunk, HEAD_DIM))
    chunk_decay = tuple(float((1.0 - 2.0 ** (-5.0 - h)) ** chunk) for h in range(heads))

    blk = lambda col: pl.BlockSpec((SCAN_BLOCK, width), lambda b, i: (b * n_tblk + i, col))
    pos = pl.BlockSpec((SCAN_BLOCK, HEAD_DIM), lambda b, i: (i, 0))
    full3 = lambda a: pl.BlockSpec(a.shape, lambda b, i: (0, 0, 0))
    return pl.pallas_call(
        functools.partial(_ret_kernel, chunk=chunk, heads=heads, chunk_decay=chunk_decay),
        grid=(batch, n_tblk),
        in_specs=[blk(col0), blk(col0 + 1), blk(col0 + 2), blk(col0 + 3), pos, pos,
                  full3(intra), full3(cross), full3(zeta),
                  pl.BlockSpec((1, width), lambda b, i: (0, 0))],
        out_specs=pl.BlockSpec((SCAN_BLOCK, width), lambda b, i: (b * n_tblk + i, 0)),
        out_shape=jax.ShapeDtypeStruct((t, width), BF16),
        scratch_shapes=[pltpu.VMEM((heads, HEAD_DIM, HEAD_DIM), F32)],
        compiler_params=_cp("parallel", "arbitrary"),
    )(proj, proj, proj, proj, cos_t, sin_t, intra, cross, zeta, norm_w)


def _merge_kernel(x_ref, *refs):
    y_refs, wg_refs = refs[:N_BRANCH], refs[N_BRANCH:2 * N_BRANCH]
    wb_ref, o_ref, xb_ref = refs[2 * N_BRANCH:]

    @pl.when(pl.program_id(1) == 0)
    def _():
        xb_ref[...] = x_ref[...].astype(BF16)

    xb = xb_ref[...]
    acc = None
    for n in range(N_BRANCH):
        gate = _sigmoid(jnp.dot(xb, wg_refs[n][...], preferred_element_type=F32))
        term = gate * jnp.dot(y_refs[n][...], wb_ref[n], preferred_element_type=F32)
        acc = term if acc is None else acc + term
    o_ref[...] = acc.astype(BF16)


def _merge(x, branches, w_gate, gate_col0, w_branch, *, tm, tn):
    t, d = x.shape
    width = branches[0].shape[1]
    ybs = pl.BlockSpec((tm, width), lambda i, j: (i, 0))

    def gate_spec(n):
        return pl.BlockSpec((d, tn), lambda i, j: (0, (gate_col0 + n * d) // tn + j))

    return pl.pallas_call(
        _merge_kernel,
        grid=(t // tm, d // tn),
        in_specs=[pl.BlockSpec((tm, d), lambda i, j: (i, 0))] + [ybs] * N_BRANCH
                 + [gate_spec(n) for n in range(N_BRANCH)]
                 + [pl.BlockSpec((N_BRANCH, width, tn), lambda i, j: (0, 0, j))],
        out_specs=pl.BlockSpec((tm, tn), lambda i, j: (i, j)),
        out_shape=jax.ShapeDtypeStruct((t, d), BF16),
        scratch_shapes=[pltpu.VMEM((tm, d), BF16)],
        compiler_params=_cp("parallel", "arbitrary"),
    )(x, *branches, *([w_gate] * N_BRANCH), w_branch)


def _proj_ln_kernel(a_ref, w_ref, r_ref, lw_ref, lb_ref, o_ref, *, alpha):
    y = jnp.dot(a_ref[...], w_ref[...], preferred_element_type=F32)
    o_ref[...] = _layer_norm(alpha * r_ref[...] + y, lw_ref[...], lb_ref[...])


def _proj_ln(a, w, resid, ln_w, ln_b, *, alpha, tm):
    t, k = a.shape
    d = w.shape[1]
    row = lambda n: pl.BlockSpec((tm, n), lambda i: (i, 0))
    const = lambda shape: pl.BlockSpec(shape, lambda i: (0, 0))
    return pl.pallas_call(
        functools.partial(_proj_ln_kernel, alpha=alpha),
        grid=(t // tm,),
        in_specs=[row(k), const((k, d)), row(d), const((1, d)), const((1, d))],
        out_specs=row(d),
        out_shape=jax.ShapeDtypeStruct((t, d), F32),
        compiler_params=_cp("parallel"),
    )(a, w, resid, ln_w, ln_b)


def _xattn_kernel(x_ref, wq_ref, kv_ref, wo_ref, lw_ref, lb_ref, o_ref, *, alpha, heads):
    x = x_ref[...]
    hd = HEAD_DIM
    inner = heads * hd
    q = jnp.dot(x.astype(BF16), wq_ref[...], preferred_element_type=F32)
    outs = []
    for h in range(heads):
        qh = q[:, h * hd:(h + 1) * hd].astype(BF16)
        kh = kv_ref[:, h * hd:(h + 1) * hd].astype(BF16)
        vh = kv_ref[:, inner + h * hd:inner + (h + 1) * hd].astype(BF16)
        s = lax.dot_general(qh, kh, (((1,), (1,)), ((), ())), preferred_element_type=F32) * hd ** -0.5
        s = s - jnp.max(s, axis=-1, keepdims=True)
        e = jnp.exp(s)
        p = e / jnp.sum(e, axis=-1, keepdims=True)
        outs.append(jnp.dot(p.astype(BF16), vh, preferred_element_type=F32).astype(BF16))
    o = jnp.concatenate(outs, axis=-1)
    y = jnp.dot(o, wo_ref[...], preferred_element_type=F32)
    o_ref[...] = _layer_norm(alpha * x + y, lw_ref[...], lb_ref[...])


def _xattn(x, kv, wq, wo, ln_w, ln_b, *, alpha, seq, mem_len, tm):
    t, d = x.shape
    inner = wq.shape[1]
    n_tblk = seq // tm
    const = lambda shape: pl.BlockSpec(shape, lambda i: (0, 0))
    return pl.pallas_call(
        functools.partial(_xattn_kernel, alpha=alpha, heads=XATTN_HEADS),
        grid=(t // tm,),
        in_specs=[pl.BlockSpec((tm, d), lambda i: (i, 0)), const((d, inner)),
                  pl.BlockSpec((mem_len, 2 * inner), lambda i: (i // n_tblk, 0)),
                  const((inner, d)), const((1, d)), const((1, d))],
        out_specs=pl.BlockSpec((tm, d), lambda i: (i, 0)),
        out_shape=jax.ShapeDtypeStruct((t, d), F32),
        compiler_params=_cp("parallel"),
    )(x, wq, kv, wo, ln_w, ln_b)


def _round_up_pow2(x, m):
    shift = m.bit_length() - 1
    return jnp.left_shift(jnp.right_shift(x + (m - 1), shift), shift)


def _route_kernel(x_ref, wr_ref, rb_ref, w_ref, lrow_ref, segtab_ref, blk_ref, seg_ref, size_all, tot, *, tm, bm):
    step = pl.program_id(0)

    @pl.when(step == 0)
    def _():
        tot[...] = jnp.zeros_like(tot)

    e_n, g_n = N_EXPERTS, N_GROUPS
    per = e_n // g_n
    def split(a):
        hi = a.astype(BF16)
        return hi, (a - hi.astype(F32)).astype(BF16)

    def nt(a, b):
        return lax.dot_general(a, b, (((1,), (1,)), ((), ())), preferred_element_type=F32)

    (w_hi, w_lo), (x_hi, x_lo) = split(wr_ref[...]), split(x_ref[...])
    logits = nt(w_hi, x_hi) + (nt(w_hi, x_lo) + nt(w_lo, x_hi))
    scores = _sigmoid(logits)
    biased = scores + rb_ref[...]
    b3 = biased.reshape(g_n, per, tm)
    member = lax.broadcasted_iota(I32, (g_n, per, tm), 1)
    top1 = jnp.max(b3, axis=1, keepdims=True)
    first = jnp.min(jnp.where(b3 == top1, member, per), axis=1, keepdims=True)
    top2 = jnp.max(jnp.where(member == first, -jnp.inf, b3), axis=1, keepdims=True)
    gs = top1 + top2
    gid = lax.broadcasted_iota(I32, (g_n, 1, tm), 0)
    rank = jnp.zeros((g_n, 1, tm), I32)
    for other in range(g_n):
        o = gs[other:other + 1]
        ahead = jnp.logical_or(o > gs, jnp.logical_and(o == gs, other < gid))
        rank = rank + jnp.where(ahead, 1, 0)
    cur = jnp.where(rank < TOPK_GROUPS, b3, -jnp.inf).reshape(e_n, tm)

    eid = lax.broadcasted_iota(I32, (e_n, tm), 0)
    picks, vals = [], []
    sel = jnp.zeros((e_n, tm), F32)
    for k in range(TOP_K):
        mx = jnp.max(cur, axis=0, keepdims=True)
        ik = jnp.min(jnp.where(cur == mx, eid, e_n), axis=0, keepdims=True)
        hit = eid == ik
        vals.append(jnp.sum(jnp.where(hit, scores, 0.0), axis=0, keepdims=True))
        cur = jnp.where(hit, -jnp.inf, cur)
        sel = jnp.where(hit, 1.0, sel)
        picks.append(ik)
    total = vals[0]
    for v in vals[1:]:
        total = total + v

    tri = jnp.where(lax.broadcasted_iota(I32, (tm, tm), 0) <= lax.broadcasted_iota(I32, (tm, tm), 1), 1.0, 0.0)
    incl = jnp.dot(sel.astype(BF16), tri.astype(BF16), preferred_element_type=F32)
    size = _round_up_pow2(jnp.broadcast_to(incl[:, tm - 1:tm], (e_n, LANES)).astype(I32), SEG_ALIGN)
    loff = _cum_rows(size, jnp.add, 0) - size
    base = loff[:, 0:1].astype(F32) + incl - 1.0
    for k in range(TOP_K):
        w_ref[k:k + 1, :] = vals[k] / total * ROUTE_SCALE
        lrow_ref[0, k:k + 1, :] = jnp.sum(jnp.where(eid == picks[k], base, 0.0),
                                          axis=0, keepdims=True).astype(I32)
    size_all[step] = size
    tot[...] = tot[...] + size

    @pl.when(step == pl.num_programs(0) - 1)
    def _():
        rows = tot[...]
        pcnt = _round_up_pow2(rows, bm)
        pend = _cum_rows(pcnt, jnp.add, 0)

        def tile_seg(i, run):
            segtab_ref[i, 0] = run
            segtab_ref[i, 1] = size_all[i]
            return run + size_all[i]

        data_start = pend - rows
        lax.fori_loop(0, pl.num_programs(0), tile_seg, data_start)

        nb = blk_ref.shape[2]
        row0 = lax.broadcasted_iota(I32, (e_n, nb), 1) * bm
        total_rows = pend[e_n - 1:e_n, 0:1]
        owner = jnp.sum(jnp.where(pend[:, 0:1] <= row0, 1, 0), axis=0, keepdims=True)
        last_owner = jnp.sum(jnp.where(pend[:, 0:1] < total_rows, 1, 0), axis=0, keepdims=True)
        blk_ref[0] = jnp.where(row0[0:1, :] < total_rows, jnp.minimum(owner, e_n - 1), last_owner)
        inside = jnp.logical_and(pend[:, 0:1] - pcnt[:, 0:1] <= row0, row0 < pend[:, 0:1])
        blk_ref[1] = jnp.sum(jnp.where(inside, jnp.clip(row0 + bm - data_start[:, 0:1], 0, bm), 0),
                             axis=0, keepdims=True)
        seg_ref[0] = pend
        seg_ref[1] = pcnt


def _route(x, router_w_t, router_b, *, tm, bm, n_blocks):
    t, d = x.shape
    e_n = N_EXPERTS
    n_t = t // tm
    assert bm & (bm - 1) == 0
    nb_pad = -(-n_blocks // LANES) * LANES
    return pl.pallas_call(
        functools.partial(_route_kernel, tm=tm, bm=bm),
        grid=(n_t,),
        in_specs=[pl.BlockSpec((tm, d), lambda i: (i, 0)),
                  pl.BlockSpec((e_n, d), lambda i: (0, 0)),
                  pl.BlockSpec((e_n, 1), lambda i: (0, 0))],
        out_specs=[pl.BlockSpec((TOP_K, tm), lambda i: (0, i)),
                   pl.BlockSpec((1, TOP_K, tm), lambda i: (i, 0, 0)),
                   pl.BlockSpec((n_t, 2, e_n, LANES), lambda i: (0, 0, 0, 0)),
                   pl.BlockSpec((2, 1, nb_pad), lambda i: (0, 0, 0)),
                   pl.BlockSpec((2, e_n, LANES), lambda i: (0, 0, 0))],
        out_shape=[jax.ShapeDtypeStruct((TOP_K, t), F32), jax.ShapeDtypeStruct((n_t, TOP_K, tm), I32),
                   jax.ShapeDtypeStruct((n_t, 2, e_n, LANES), I32), jax.ShapeDtypeStruct((2, 1, nb_pad), I32),
                   jax.ShapeDtypeStruct((2, e_n, LANES), I32)],
        scratch_shapes=[pltpu.VMEM((n_t, e_n, LANES), I32), pltpu.VMEM((e_n, LANES), I32)],
        compiler_params=_cp("arbitrary"),
    )(x, router_w_t, router_b)


def _tile_rows(tm):
    worst = TOP_K * tm + N_EXPERTS * (SEG_ALIGN - 1)
    return -(-worst // ONEHOT_ROWS) * ONEHOT_ROWS


def _onehot_rows(chunk, lrow, values, tm):
    rid = chunk * ONEHOT_ROWS + lax.broadcasted_iota(I32, (ONEHOT_ROWS, tm), 0)
    acc = jnp.zeros((ONEHOT_ROWS, tm), F32)
    for k in range(TOP_K):
        acc = jnp.where(rid == lrow[k:k + 1, :], 1.0 if values is None else values[k:k + 1, :], acc)
    return acc.astype(BF16)


def _start_segments(gstart_ref, size_ref, tile, make_copy, first=0, last=N_EXPERTS, row0=0):
    def body(e, loff):
        n = size_ref[tile * N_EXPERTS + e]

        @pl.when(n > 0)
        def _():
            make_copy(pl.multiple_of(gstart_ref[tile * N_EXPERTS + e], SEG_ALIGN),
                      pl.multiple_of(loff, SEG_ALIGN), pl.multiple_of(n, SEG_ALIGN)).start()
        return loff + n
    return lax.fori_loop(first, last, body, row0)


def _segments_within(size_ref, tile, limit):
    def body(e, carry):
        count, rows = carry
        end = rows + size_ref[tile * N_EXPERTS + e]
        fits = jnp.logical_and(count == e, end <= limit)
        return count + jnp.where(fits, 1, 0), jnp.where(fits, end, rows)
    return lax.fori_loop(0, N_EXPERTS, body, (0, 0))


def _tile_total(size_ref, tile):
    return lax.fori_loop(0, N_EXPERTS, lambda e, s: s + size_ref[tile * N_EXPERTS + e], 0)


def _wait_rows(make_copy, rows):
    @pl.when(rows > 0)
    def _():
        make_copy(0, 0, pl.multiple_of(rows, SEG_ALIGN)).wait()


def _dispatch_kernel(gstart_ref, size_ref, x_ref, lrow_ref, xs_ref, stage, inflight, sem_a, sem_b, *, tm):
    i = pl.program_id(0)
    last = pl.num_programs(0) - 1
    n_chunks = stage.shape[0] // ONEHOT_ROWS
    split_chunk = n_chunks // 2
    split = split_chunk * ONEHOT_ROWS
    assert tm <= ONEHOT_ROWS

    def copy_on(sem):
        return lambda g, loff, n: pltpu.make_async_copy(stage.at[pl.ds(loff, n), :], xs_ref.at[pl.ds(g, n), :], sem)

    xb = x_ref[...].astype(BF16)
    lrow = lrow_ref[0]
    n_rows = _tile_total(size_ref, i)

    def chunks(lo, hi):
        for c in range(lo, hi):
            def one(c=c):
                stage[c * ONEHOT_ROWS:(c + 1) * ONEHOT_ROWS, :] = jnp.dot(
                    _onehot_rows(c, lrow, None, tm), xb, preferred_element_type=F32).astype(BF16)
            if c * ONEHOT_ROWS < TOP_K * tm:
                one()
            else:
                pl.when(c * ONEHOT_ROWS < n_rows)(one)

    @pl.when(i == 0)
    def _():
        inflight[0] = 0
        inflight[1] = 0

    _wait_rows(copy_on(sem_a), inflight[0])
    chunks(0, split_chunk - 1)
    _wait_rows(copy_on(sem_b), inflight[1])
    chunks(split_chunk - 1, split_chunk)
    experts_a, rows_a = _segments_within(size_ref, i, split)
    _start_segments(gstart_ref, size_ref, i, copy_on(sem_a), 0, experts_a)
    chunks(split_chunk, n_chunks)
    rows_b = _start_segments(gstart_ref, size_ref, i, copy_on(sem_b), experts_a, N_EXPERTS, rows_a) - rows_a
    inflight[0] = rows_a
    inflight[1] = rows_b

    @pl.when(i == last)
    def _():
        _wait_rows(copy_on(sem_a), rows_a)
        _wait_rows(copy_on(sem_b), rows_b)


def _dispatch(x, lrow, gstart, size, *, rows, tm):
    t, d = x.shape
    return pl.pallas_call(
        functools.partial(_dispatch_kernel, tm=tm),
        grid_spec=pltpu.PrefetchScalarGridSpec(
            num_scalar_prefetch=2,
            grid=(t // tm,),
            in_specs=[pl.BlockSpec((tm, d), lambda i, *_: (i, 0)),
                      pl.BlockSpec((1, TOP_K, tm), lambda i, *_: (i, 0, 0))],
            out_specs=pl.BlockSpec(memory_space=pl.ANY),
            scratch_shapes=[pltpu.VMEM((_tile_rows(tm), d), BF16), pltpu.SMEM((2,), I32)]
                           + [pltpu.SemaphoreType.DMA(())] * 2,
        ),
        out_shape=jax.ShapeDtypeStruct((rows, d), BF16),
        compiler_params=_cp("arbitrary"),
    )(gstart, size, x, lrow)


def _expert_block_kernel(blk_ref, nused_ref, xs_ref, wgu_ref, wdn_ref, ys_ref, wgu_b, wdn_b, *, n_blocks, bm):
    j = pl.program_id(0)
    valid = blk_ref[n_blocks + j]
    changed = jnp.logical_or(j == 0, blk_ref[j] != blk_ref[jnp.maximum(j - 1, 0)])

    @pl.when(jnp.logical_and(valid > 0, changed))
    def _():
        wgu_b[...] = wgu_ref[...].astype(BF16)
        wdn_b[...] = wdn_ref[...].astype(BF16)

    def swiglu(rows, masked):
        first = bm - rows
        x = xs_ref[first:bm, :]
        if masked:
            x = jnp.where(first + lax.broadcasted_iota(I32, x.shape, 0) >= bm - valid, x, jnp.zeros_like(x))
        f = wdn_b.shape[0]
        gu = jnp.dot(x, wgu_b[...], preferred_element_type=F32)
        hidden = (_silu(gu[:, :f]) * gu[:, f:]).astype(BF16)
        ys_ref[first:bm, :] = jnp.dot(hidden, wdn_b[...], preferred_element_type=F32).astype(BF16)

    @pl.when(valid == bm)
    def _():
        swiglu(bm, False)

    quarter = bm // 4
    for q in range(1, 5):
        @pl.when(jnp.logical_and(jnp.logical_and(valid > (q - 1) * quarter, valid <= q * quarter), valid < bm))
        def _():
            swiglu(q * quarter, True)


def _experts_blocked(xs, blk, nused, w_gu, w_dn, layer, *, n_blocks, bm):
    rows, d = xs.shape
    f2 = w_gu.shape[3]
    f = w_dn.shape[2]
    row_blk = lambda j, bl, nu: (jnp.minimum(j, nu[0] - 1), 0)
    return pl.pallas_call(
        functools.partial(_expert_block_kernel, n_blocks=n_blocks, bm=bm),
        grid_spec=pltpu.PrefetchScalarGridSpec(
            num_scalar_prefetch=2,
            grid=(n_blocks,),
            in_specs=[pl.BlockSpec((bm, d), row_blk),
                      pl.BlockSpec((None, None, d, f2), lambda j, bl, nu: (layer, bl[j], 0, 0)),
                      pl.BlockSpec((None, None, f, d), lambda j, bl, nu: (layer, bl[j], 0, 0))],
            out_specs=pl.BlockSpec((bm, d), row_blk),
            scratch_shapes=[pltpu.VMEM((d, f2), BF16), pltpu.VMEM((f, d), BF16)],
        ),
        out_shape=jax.ShapeDtypeStruct((rows, d), BF16),
        compiler_params=_cp("arbitrary"),
    )(blk, nused, xs, w_gu, w_dn)


def _combine_tile_kernel(gstart_ref, size_ref, x_ref, lrow_ref, w_ref, sdn_ref, lw_ref, lb_ref, sgu_hbm, ys_ref,
                         o_ref, ybuf, sgu, sems, wsem, *, tm, alpha):
    i = pl.program_id(0)
    last_tile = pl.num_programs(0) - 1
    n_chunks = ybuf.shape[1] // ONEHOT_ROWS
    sure_chunks = (TOP_K * tm) // ONEHOT_ROWS

    def contract_rows(w_rows, y_rows):
        return lax.dot_general(w_rows, y_rows, (((0,), (0,)), ((), ())), preferred_element_type=F32)

    def copy_to(slot):
        return lambda g, loff, n: pltpu.make_async_copy(ys_ref.at[pl.ds(g, n), :], ybuf.at[slot, pl.ds(loff, n), :],
                                                        sems.at[slot])

    def fetch(tile, slot):
        _start_segments(gstart_ref, size_ref, tile, copy_to(slot))

    n_rows = _tile_total(size_ref, i)

    @pl.when(i == 0)
    def _():
        ybuf[...] = jnp.zeros_like(ybuf)
        weights = pltpu.make_async_copy(sgu_hbm, sgu, wsem)
        weights.start()
        weights.wait()
        fetch(0, 0)

    for slot in range(2):
        @pl.when(i % 2 == slot)
        def _():
            @pl.when(i < last_tile)
            def _():
                fetch(i + 1, 1 - slot)

            x = x_ref[...]
            f = sdn_ref.shape[0]
            gu = jnp.dot(x.astype(BF16), sgu[...], preferred_element_type=F32)
            hidden = (_silu(gu[:, :f]) * gu[:, f:]).astype(BF16)
            acc = jnp.dot(hidden, sdn_ref[...], preferred_element_type=F32)
            _wait_rows(copy_to(slot), n_rows)
            lrow, w = lrow_ref[0], w_ref[...]
            for c in range(sure_chunks):
                rows = slice(c * ONEHOT_ROWS, (c + 1) * ONEHOT_ROWS)
                acc = acc + contract_rows(_onehot_rows(c, lrow, w, tm), ybuf[slot, rows, :])
            o_ref[...] = acc
            for c in range(sure_chunks, n_chunks):
                @pl.when(c * ONEHOT_ROWS < n_rows)
                def _():
                    rows = slice(c * ONEHOT_ROWS, (c + 1) * ONEHOT_ROWS)
                    o_ref[...] += contract_rows(_onehot_rows(c, lrow, w, tm), ybuf[slot, rows, :])
            o_ref[...] = _layer_norm(alpha * x + o_ref[...], lw_ref[...], lb_ref[...])


def _combine(x, ys, lrow, wts, gstart, size, s_gu, s_dn, ln_w, ln_b, *, alpha, tm):
    t, d = x.shape
    const = lambda shape: pl.BlockSpec(shape, lambda i, *_: (0, 0))
    return pl.pallas_call(
        functools.partial(_combine_tile_kernel, tm=tm, alpha=alpha),
        grid_spec=pltpu.PrefetchScalarGridSpec(
            num_scalar_prefetch=2,
            grid=(t // tm,),
            in_specs=[pl.BlockSpec((tm, d), lambda i, *_: (i, 0)),
                      pl.BlockSpec((1, TOP_K, tm), lambda i, *_: (i, 0, 0)),
                      pl.BlockSpec((TOP_K, tm), lambda i, *_: (0, i)),
                      const(s_dn.shape), const((1, d)), const((1, d)),
                      pl.BlockSpec(memory_space=pl.ANY), pl.BlockSpec(memory_space=pl.ANY)],
            out_specs=pl.BlockSpec((tm, d), lambda i, *_: (i, 0)),
            scratch_shapes=[pltpu.VMEM((2, _tile_rows(tm), d), BF16), pltpu.VMEM(s_gu.shape, BF16),
                            pltpu.SemaphoreType.DMA((2,)), pltpu.SemaphoreType.DMA(())],
        ),
        out_shape=jax.ShapeDtypeStruct((t, d), F32),
        compiler_params=_cp("arbitrary"),
    )(gstart, size, x, lrow, wts, s_dn, ln_w, ln_b, s_gu, ys)


def _mixer_sublayer(x, w_in_all, layer, gate_b, pool_w, pool_scale, conv_w, mlstm_norm_w, ret_norm_w, w_branch,
                    w_out, ln_w, ln_b, *, batch, seq, alpha):
    t, d = x.shape
    width = d // N_BRANCH
    heads = width // HEAD_DIM
    gate_off = 8 * width
    ret_off = gate_off
    g_off = ret_off + 4 * width
    if_off = g_off + N_BRANCH * d
    w_bf16 = _realign_cast(w_in_all, layer, lo_col=gate_off, hi_col=if_off, shift=2 * heads, tr=512, tn=512)
    gate_bias = jnp.pad(gate_b, (0, LANES - 2 * heads)).reshape(1, LANES)

    proj = _matmul(x, w_bf16, tm=1024, tn=1024, ncols=g_off)
    gates = _matmul(x, w_bf16, tm=1024, tn=LANES, ncols=LANES, col0=if_off)
    y_pool, y_conv = _pool_conv(proj, pool_w.astype(BF16), pool_scale.reshape(1, width), conv_w,
                                seq=seq, width=width, tb=512)
    y_mlstm = _mlstm(proj, gates, gate_bias, mlstm_norm_w.reshape(1, width),
                     batch=batch, seq=seq, width=width, col0=4)
    y_ret = _retention(proj, ret_norm_w.reshape(1, width), batch=batch, seq=seq, width=width,
                       col0=ret_off // width)
    merged = _merge(x, (y_pool, y_conv, y_mlstm, y_ret), w_bf16, g_off, w_branch.astype(BF16),
                    tm=512, tn=512)
    return _proj_ln(merged, w_out.astype(BF16), x, ln_w, ln_b, alpha=alpha, tm=512)


def _xattn_sublayer(x, mem2d, wq, wk, wv, wo, ln_w, ln_b, *, seq, mem_len, alpha):
    w_kv = jnp.concatenate([wk, wv], axis=1).astype(BF16)
    kv = _matmul(mem2d, w_kv, tm=min(mem2d.shape[0], 1024), tn=512)
    return _xattn(x, kv, wq.astype(BF16), wo.astype(BF16), ln_w, ln_b,
                  alpha=alpha, seq=seq, mem_len=mem_len, tm=512)


def _moe_sublayer(x, router_w, router_b, w_gu, w_dn, layer, s_gu, s_dn, ln_w, ln_b, *, alpha):
    t, d = x.shape
    e_n, bm = N_EXPERTS, MOE_BM
    tm = ROUTE_TM
    n_blocks = -(-(t * TOP_K + (t // tm) * e_n * (SEG_ALIGN - 1)) // bm) + e_n
    wts, lrow, segtab, blk, seg = _route(x, router_w.T, router_b.reshape(e_n, 1), tm=tm, bm=bm, n_blocks=n_blocks)
    gstart, size = segtab[:, 0, :, 0].reshape(-1), segtab[:, 1, :, 0].reshape(-1)
    nused = seg[0, e_n - 1, 0] // bm

    xs = _dispatch(x, lrow, gstart, size, rows=n_blocks * bm, tm=tm)
    ys = _experts_blocked(xs, blk[:, 0, :n_blocks].reshape(-1), nused.reshape(1), w_gu, w_dn, layer,
                          n_blocks=n_blocks, bm=bm)
    return _combine(x, ys, lrow, wts, gstart, size, s_gu.astype(BF16), s_dn.astype(BF16), ln_w, ln_b,
                    alpha=alpha, tm=tm)


def kernel(x, mem, w_in, mlstm_gate_b, pool_w, pool_scale, conv_w, mlstm_norm_w, ret_norm_w, w_branch,
           w_mix_out, xa_wq, xa_wk, xa_wv, xa_wo, router_w, router_b, moe_w_gu, moe_w_dn, shared_w_gu,
           shared_w_dn, ln_w, ln_b):
    batch, seq, d = x.shape
    depth = w_in.shape[0]
    mem_len = mem.shape[1]
    alpha = (2 * depth) ** 0.25
    h = x.reshape(batch * seq, d)
    mem2d = mem.reshape(batch * mem_len, d)
    for l in range(depth):
        lw = ln_w[l].reshape(3, 1, d)
        lb = ln_b[l].reshape(3, 1, d)
        h = _mixer_sublayer(h, w_in, l, mlstm_gate_b[l], pool_w[l], pool_scale[l], conv_w[l], mlstm_norm_w[l],
                            ret_norm_w[l], w_branch[l], w_mix_out[l], lw[0], lb[0],
                            batch=batch, seq=seq, alpha=alpha)
        h = _xattn_sublayer(h, mem2d, xa_wq[l], xa_wk[l], xa_wv[l], xa_wo[l], lw[1], lb[1],
                            seq=seq, mem_len=mem_len, alpha=alpha)
        h = _moe_sublayer(h, router_w[l], router_b[l], moe_w_gu, moe_w_dn, l, shared_w_gu[l], shared_w_dn[l],
                          lw[2], lb[2], alpha=alpha)
    return h.reshape(batch, seq, d)
```

```python
import functools

import numpy as np
import jax
import jax.numpy as jnp
from jax import lax
from jax.experimental import pallas as pl
from jax.experimental.pallas import tpu as pltpu

F32 = jnp.float32
BF16 = jnp.bfloat16
I32 = jnp.int32

N_BRANCH = 4
HEAD_DIM = 128
POOL_WINDOWS = (2, 4, 8, 16)
CONV_WIDTH = 3
ROPE_BASE = 10000.0
XATTN_HEADS = 4
N_EXPERTS = 64
TOP_K = 8
N_GROUPS = 8
TOPK_GROUPS = 4
ROUTE_SCALE = 2.5
LN_EPS = 1e-5

LANES = 128
V7X_VMEM_BYTES = 64 * 1024 * 1024
VMEM_LIMIT = V7X_VMEM_BYTES - 8 * 1024 * 1024

SCAN_CHUNK = 256
SCAN_BLOCK = 512
SCAN_BATCH = 2
HALO = 16
MOE_BM = 1024
ROUTE_TM = 256
SEG_ALIGN = 16
ONEHOT_ROWS = 512


def _cp(*sem):
    return pltpu.CompilerParams(dimension_semantics=sem, vmem_limit_bytes=VMEM_LIMIT)


def _sigmoid(x):
    return 1.0 / (1.0 + jnp.exp(-x))


def _silu(x):
    return x * _sigmoid(x)


def _log_sigmoid(x):
    return jnp.minimum(x, 0.0) - jnp.log(1.0 + jnp.exp(-jnp.abs(x)))


def _layer_norm(z, w, b):
    mu = jnp.mean(z, axis=-1, keepdims=True)
    d = z - mu
    var = jnp.mean(d * d, axis=-1, keepdims=True)
    return d * lax.rsqrt(var + LN_EPS) * w + b


def _head_norm(h, w):
    mu = jnp.mean(h, axis=-1, keepdims=True)
    d = h - mu
    var = jnp.mean(d * d, axis=-1, keepdims=True)
    return d * lax.rsqrt(var + LN_EPS) * w


def _mm_kernel(x_ref, w_ref, o_ref, xb_ref):
    @pl.when(pl.program_id(1) == 0)
    def _():
        xb_ref[...] = x_ref[...].astype(BF16)

    o_ref[...] = jnp.dot(xb_ref[...], w_ref[...], preferred_element_type=F32)


def _realign_cast_kernel(a_ref, b_ref, o_ref, *, shift, lo, hi):
    j = pl.program_id(1)
    tn = o_ref.shape[1]
    shifted = jnp.logical_and(j >= lo, j < hi)

    @pl.when(shifted)
    def _():
        both = jnp.concatenate([a_ref[...], b_ref[...]], axis=1)
        o_ref[...] = both[:, shift:shift + tn].astype(BF16)

    @pl.when(jnp.logical_not(shifted))
    def _():
        o_ref[...] = a_ref[...].astype(BF16)


def _realign_cast(w_all, layer, *, lo_col, hi_col, shift, tr, tn):
    _, rows, _ = w_all.shape
    lo, hi = lo_col // tn, hi_col // tn
    src = lambda j: jnp.where(j == hi, lo, j)
    return pl.pallas_call(
        functools.partial(_realign_cast_kernel, shift=shift, lo=lo, hi=hi),
        grid=(rows // tr, hi + 1),
        in_specs=[pl.BlockSpec((None, tr, tn), lambda i, j: (layer, i, src(j))),
                  pl.BlockSpec((None, tr, LANES), lambda i, j: (layer, i, (src(j) + 1) * (tn // LANES)))],
        out_specs=pl.BlockSpec((tr, tn), lambda i, j: (i, j)),
        out_shape=jax.ShapeDtypeStruct((rows, hi_col + tn), BF16),
        compiler_params=_cp("parallel", "parallel"),
    )(w_all, w_all)


def _matmul(x, w, *, tm, tn, ncols=None, col0=0):
    t, k = x.shape
    n = w.shape[1] if ncols is None else ncols
    return pl.pallas_call(
        _mm_kernel,
        grid=(t // tm, n // tn),
        in_specs=[pl.BlockSpec((tm, k), lambda i, j: (i, 0)),
                  pl.BlockSpec((k, tn), lambda i, j: (0, col0 // tn + j))],
        out_specs=pl.BlockSpec((tm, tn), lambda i, j: (i, j)),
        out_shape=jax.ShapeDtypeStruct((t, n), F32),
        scratch_shapes=[pltpu.VMEM((tm, k), BF16)],
        compiler_params=_cp("parallel", "arbitrary"),
    )(x, w)


def _poolconv_kernel(u_ref, uh_ref, h_ref, hh_ref, b_ref, c_ref, ch_ref, pw_ref, ps_ref, cw_ref,
                     yp_ref, yc_ref, ubuf, zbuf, *, tb, n_tblk):
    first = (pl.program_id(0) % n_tblk) == 0
    ubuf[0:HALO, :] = jnp.where(first, 0.0, uh_ref[...])
    ubuf[HALO:HALO + tb, :] = u_ref[...]
    zbuf[0:HALO, :] = jnp.where(first, 0.0, ch_ref[...] * hh_ref[...])
    zbuf[HALO:HALO + tb, :] = c_ref[...] * h_ref[...]

    t_pos = (pl.program_id(0) % n_tblk) * tb + lax.broadcasted_iota(I32, (tb, LANES), 0)
    gw = u_ref.shape[1] // len(POOL_WINDOWS)
    for grp, win in enumerate(POOL_WINDOWS):
        lanes = slice(grp * gw, (grp + 1) * gw)
        cur = ubuf[HALO:HALO + tb, lanes]
        acc = cur
        for lag in range(1, win):
            acc = acc + ubuf[HALO - lag:HALO - lag + tb, lanes]
        count = jnp.minimum(t_pos + 1, win).astype(F32)
        mixed = acc / count - cur
        y = jnp.dot(mixed.astype(BF16), pw_ref[grp], preferred_element_type=F32)
        yp_ref[:, lanes] = (y * ps_ref[:, lanes]).astype(BF16)

    conv = cw_ref[0:1, :] * zbuf[HALO:HALO + tb, :]
    for lag in range(1, CONV_WIDTH):
        conv = conv + cw_ref[lag:lag + 1, :] * zbuf[HALO - lag:HALO - lag + tb, :]
    yc_ref[...] = (b_ref[...] * conv).astype(BF16)


def _pool_conv(proj, pool_w, pool_scale, conv_w, *, seq, width, tb):
    t = proj.shape[0]
    n_tblk = seq // tb
    ratio = tb // HALO

    def cur(col):
        return pl.BlockSpec((tb, width), lambda g: (g, col))

    def halo(col):
        return pl.BlockSpec((HALO, width), lambda g: (jnp.maximum(g * ratio - 1, 0), col))

    full = lambda shape: pl.BlockSpec(shape, lambda g: (0,) * len(shape))
    return pl.pallas_call(
        functools.partial(_poolconv_kernel, tb=tb, n_tblk=n_tblk),
        grid=(t // tb,),
        in_specs=[cur(0), halo(0), cur(1), halo(1), cur(2), cur(3), halo(3),
                  full(pool_w.shape), full(pool_scale.shape), full(conv_w.shape)],
        out_specs=[pl.BlockSpec((tb, width), lambda g: (g, 0))] * 2,
        out_shape=[jax.ShapeDtypeStruct((t, width), BF16)] * 2,
        scratch_shapes=[pltpu.VMEM((HALO + tb, width), F32)] * 2,
        compiler_params=_cp("parallel"),
    )(proj, proj, proj, proj, proj, proj, proj, pool_w, pool_scale, conv_w)


def _cum_rows(x, op, fill):
    n = x.shape[0]
    row = lax.broadcasted_iota(I32, x.shape, 0)
    shift = 1
    while shift < n:
        x = op(x, jnp.where(row >= shift, pltpu.roll(x, shift, axis=0), fill))
        shift *= 2
    return x


def _mlstm_kernel(q_ref, k_ref, v_ref, o_ref, g_ref, gb_ref, nw_ref, y_ref, c_scr, n_scr, m_scr,
                  *, chunk, heads):
    @pl.when(pl.program_id(1) == 0)
    def _():
        c_scr[...] = jnp.zeros_like(c_scr)
        n_scr[...] = jnp.zeros_like(n_scr)
        m_scr[...] = jnp.zeros_like(m_scr)

    nb, tb = q_ref.shape[0], q_ref.shape[1]
    hd = HEAD_DIM
    scale = hd ** -0.5
    tri = (lax.broadcasted_iota(I32, (chunk, chunk), 0) >= lax.broadcasted_iota(I32, (chunk, chunk), 1))
    for c, bb in [(c, bb) for c in range(tb // chunk) for bb in range(nb)]:
        rows = slice(c * chunk, (c + 1) * chunk)
        gates = g_ref[bb, rows, :] + gb_ref[...]
        lf = _log_sigmoid(pltpu.roll(gates, LANES - heads, axis=1))
        cumf = _cum_rows(lf, jnp.add, 0.0)
        a = gates - cumf
        m_prev = m_scr[bb]
        mu = jnp.maximum(_cum_rows(a, jnp.maximum, -jnp.inf), m_prev)
        mu_last = mu[chunk - 1:chunk, :]
        a_t = a.T
        for h in range(heads):
            cols = slice(h * hd, (h + 1) * hd)
            q = q_ref[bb, rows, cols]
            k = k_ref[bb, rows, cols] * scale
            v = v_ref[bb, rows, cols]
            qb, kb, vb = q.astype(BF16), k.astype(BF16), v.astype(BF16)
            mu_col = mu[:, h:h + 1]
            a_col = a[:, h:h + 1]
            m_prev_h = m_prev[:, h:h + 1]
            mu_last_h = mu_last[:, h:h + 1]
            dmat = jnp.exp(jnp.where(tri, a_t[h:h + 1, :] - mu_col, -jnp.inf))
            s = lax.dot_general(qb, kb, (((1,), (1,)), ((), ())), preferred_element_type=F32)
            p = dmat * s
            inter = jnp.exp(m_prev_h - mu_col)
            state = bb * heads + h
            c_h = c_scr[state]
            n_h = n_scr[state]
            num = inter * jnp.dot(qb, c_h.astype(BF16), preferred_element_type=F32) \
                + jnp.dot(p.astype(BF16), vb, preferred_element_type=F32)
            den = inter * jnp.sum(q * n_h, axis=-1, keepdims=True) + jnp.sum(p, axis=-1, keepdims=True)
            floor = jnp.exp(-(cumf[:, h:h + 1] + mu_col))
            h_out = num / jnp.maximum(jnp.abs(den), floor)

            wg = jnp.exp(a_col - mu_last_h)
            decay = jnp.exp(m_prev_h - mu_last_h)
            kw = k * wg
            c_scr[state] = decay * c_h + lax.dot_general(kw.astype(BF16), vb, (((0,), (0,)), ((), ())),
                                                         preferred_element_type=F32)
            n_scr[state] = decay * n_h + jnp.sum(kw, axis=0, keepdims=True)

            gated = _sigmoid(o_ref[bb, rows, cols]) * h_out
            y_ref[bb, rows, cols] = _head_norm(gated, nw_ref[:, cols]).astype(BF16)
        m_scr[bb] = cumf[chunk - 1:chunk, :] + mu_last


def _mlstm(proj, gates, gate_b, norm_w, *, batch, seq, width, col0):
    t = proj.shape[0]
    heads = width // HEAD_DIM
    nb = SCAN_BATCH if batch % SCAN_BATCH == 0 else 1
    proj3 = proj.reshape(batch, seq, proj.shape[1])
    gates3 = gates.reshape(batch, seq, LANES)
    blk = lambda col: pl.BlockSpec((nb, SCAN_BLOCK, width), lambda b, i: (b, i, col))
    y = pl.pallas_call(
        functools.partial(_mlstm_kernel, chunk=SCAN_CHUNK, heads=heads),
        grid=(batch // nb, seq // SCAN_BLOCK),
        in_specs=[blk(col0), blk(col0 + 1), blk(col0 + 2), blk(col0 + 3),
                  pl.BlockSpec((nb, SCAN_BLOCK, LANES), lambda b, i: (b, i, 0)),
                  pl.BlockSpec((1, LANES), lambda b, i: (0, 0)),
                  pl.BlockSpec((1, width), lambda b, i: (0, 0))],
        out_specs=pl.BlockSpec((nb, SCAN_BLOCK, width), lambda b, i: (b, i, 0)),
        out_shape=jax.ShapeDtypeStruct((batch, seq, width), BF16),
        scratch_shapes=[pltpu.VMEM((nb * heads, HEAD_DIM, HEAD_DIM), F32),
                        pltpu.VMEM((nb * heads, 1, HEAD_DIM), F32),
                        pltpu.VMEM((nb, 1, LANES), F32)],
        compiler_params=_cp("parallel", "arbitrary"),
    )(proj3, proj3, proj3, proj3, gates3, gate_b, norm_w)
    return y.reshape(t, width)


def _ret_kernel(q_ref, k_ref, v_ref, g_ref, cos_ref, sin_ref, intra_ref, cross_ref, zeta_ref, nw_ref,
                y_ref, r_scr, *, chunk, heads, chunk_decay):
    @pl.when(pl.program_id(1) == 0)
    def _():
        r_scr[...] = jnp.zeros_like(r_scr)

    tb = q_ref.shape[0]
    hd = HEAD_DIM
    scale = hd ** -0.5
    for c in range(tb // chunk):
        rows = slice(c * chunk, (c + 1) * chunk)
        cos = cos_ref[rows, :]
        sin = sin_ref[rows, :]
        for h in range(heads):
            cols = slice(h * hd, (h + 1) * hd)
            q = q_ref[rows, cols]
            k = k_ref[rows, cols]
            q = q * cos + pltpu.roll(q, hd // 2, axis=1) * sin
            k = (k * cos + pltpu.roll(k, hd // 2, axis=1) * sin) * scale
            vb = v_ref[rows, cols].astype(BF16)
            qb = q.astype(BF16)
            s = lax.dot_general(qb, k.astype(BF16), (((1,), (1,)), ((), ())), preferred_element_type=F32)
            inner = jnp.dot((s * intra_ref[h]).astype(BF16), vb, preferred_element_type=F32)
            r_h = r_scr[h]
            crs = jnp.dot(qb, r_h.astype(BF16), preferred_element_type=F32) * cross_ref[h]
            r_scr[h] = chunk_decay[h] * r_h + lax.dot_general(
                (k * zeta_ref[h]).astype(BF16), vb, (((0,), (0,)), ((), ())), preferred_element_type=F32)
            y_ref[rows, cols] = (_silu(g_ref[rows, cols]) * _head_norm(inner + crs, nw_ref[:, cols])).astype(BF16)


def _retention(proj, norm_w, *, batch, seq, width, col0):
    t = proj.shape[0]
    heads = width // HEAD_DIM
    chunk = SCAN_CHUNK
    n_tblk = seq // SCAN_BLOCK
    half = HEAD_DIM // 2
    inv = np.float32(ROPE_BASE) ** (-np.arange(half, dtype=np.float32) / np.float32(half))
    ang = (np.arange(seq, dtype=np.float32)[:, None] * inv[None, :]).astype(np.float64)
    cos_t = jnp.asarray(np.concatenate([np.cos(ang), np.cos(ang)], axis=-1), F32)
    sin_t = jnp.asarray(np.concatenate([-np.sin(ang), np.sin(ang)], axis=-1), F32)
    log_g = jnp.log(1.0 - 2.0 ** (-5.0 - jnp.arange(heads, dtype=F32)))
    tt = jnp.arange(chunk, dtype=F32)
    lag = tt[:, None] - tt[None, :]
    intra = jnp.where(lag >= 0, jnp.exp(jnp.maximum(lag, 0.0)[None] * log_g[:, None, None]), 0.0)
    cross = jnp.broadcast_to(jnp.exp((tt + 1.0)[None, :] * log_g[:, None])[:, :, None], (heads, chunk, HEAD_DIM))
    zeta = jnp.broadcast_to(jnp.exp((chunk - 1.0 - tt)[None, :] * log_g[:, None])[:, :, None],
                            (heads, chunk, HEAD_DIM))
    chunk_decay = tuple(float((1.0 - 2.0 ** (-5.0 - h)) ** chunk) for h in range(heads))

    blk = lambda col: pl.BlockSpec((SCAN_BLOCK, width), lambda b, i: (b * n_tblk + i, col))
    pos = pl.BlockSpec((SCAN_BLOCK, HEAD_DIM), lambda b, i: (i, 0))
    full3 = lambda a: pl.BlockSpec(a.shape, lambda b, i: (0, 0, 0))
    return pl.pallas_call(
        functools.partial(_ret_kernel, chunk=chunk, heads=heads, chunk_decay=chunk_decay),
        grid=(batch, n_tblk),
        in_specs=[blk(col0), blk(col0 + 1), blk(col0 + 2), blk(col0 + 3), pos, pos,
                  full3(intra), full3(cross), full3(zeta),
                  pl.BlockSpec((1, width), lambda b, i: (0, 0))],
        out_specs=pl.BlockSpec((SCAN_BLOCK, width), lambda b, i: (b * n_tblk + i, 0)),
        out_shape=jax.ShapeDtypeStruct((t, width), BF16),
        scratch_shapes=[pltpu.VMEM((heads, HEAD_DIM, HEAD_DIM), F32)],
        compiler_params=_cp("parallel", "arbitrary"),
    )(proj, proj, proj, proj, cos_t, sin_t, intra, cross, zeta, norm_w)


def _merge_kernel(x_ref, *refs):
    y_refs, wg_refs = refs[:N_BRANCH], refs[N_BRANCH:2 * N_BRANCH]
    wb_ref, o_ref, xb_ref = refs[2 * N_BRANCH:]

    @pl.when(pl.program_id(1) == 0)
    def _():
        xb_ref[...] = x_ref[...].astype(BF16)

    xb = xb_ref[...]
    acc = None
    for n in range(N_BRANCH):
        gate = _sigmoid(jnp.dot(xb, wg_refs[n][...], preferred_element_type=F32))
        term = gate * jnp.dot(y_refs[n][...], wb_ref[n], preferred_element_type=F32)
        acc = term if acc is None else acc + term
    o_ref[...] = acc.astype(BF16)


def _merge(x, branches, w_gate, gate_col0, w_branch, *, tm, tn):
    t, d = x.shape
    width = branches[0].shape[1]
    ybs = pl.BlockSpec((tm, width), lambda i, j: (i, 0))

    def gate_spec(n):
        return pl.BlockSpec((d, tn), lambda i, j: (0, (gate_col0 + n * d) // tn + j))

    return pl.pallas_call(
        _merge_kernel,
        grid=(t // tm, d // tn),
        in_specs=[pl.BlockSpec((tm, d), lambda i, j: (i, 0))] + [ybs] * N_BRANCH
                 + [gate_spec(n) for n in range(N_BRANCH)]
                 + [pl.BlockSpec((N_BRANCH, width, tn), lambda i, j: (0, 0, j))],
        out_specs=pl.BlockSpec((tm, tn), lambda i, j: (i, j)),
        out_shape=jax.ShapeDtypeStruct((t, d), BF16),
        scratch_shapes=[pltpu.VMEM((tm, d), BF16)],
        compiler_params=_cp("parallel", "arbitrary"),
    )(x, *branches, *([w_gate] * N_BRANCH), w_branch)


def _proj_ln_kernel(a_ref, w_ref, r_ref, lw_ref, lb_ref, o_ref, *, alpha):
    y = jnp.dot(a_ref[...], w_ref[...], preferred_element_type=F32)
    o_ref[...] = _layer_norm(alpha * r_ref[...] + y, lw_ref[...], lb_ref[...])


def _proj_ln(a, w, resid, ln_w, ln_b, *, alpha, tm):
    t, k = a.shape
    d = w.shape[1]
    row = lambda n: pl.BlockSpec((tm, n), lambda i: (i, 0))
    const = lambda shape: pl.BlockSpec(shape, lambda i: (0, 0))
    return pl.pallas_call(
        functools.partial(_proj_ln_kernel, alpha=alpha),
        grid=(t // tm,),
        in_specs=[row(k), const((k, d)), row(d), const((1, d)), const((1, d))],
        out_specs=row(d),
        out_shape=jax.ShapeDtypeStruct((t, d), F32),
        compiler_params=_cp("parallel"),
    )(a, w, resid, ln_w, ln_b)


def _xattn_kernel(x_ref, wq_ref, kv_ref, wo_ref, lw_ref, lb_ref, o_ref, *, alpha, heads):
    x = x_ref[...]
    hd = HEAD_DIM
    inner = heads * hd
    q = jnp.dot(x.astype(BF16), wq_ref[...], preferred_element_type=F32)
    outs = []
    for h in range(heads):
        qh = q[:, h * hd:(h + 1) * hd].astype(BF16)
        kh = kv_ref[:, h * hd:(h + 1) * hd].astype(BF16)
        vh = kv_ref[:, inner + h * hd:inner + (h + 1) * hd].astype(BF16)
        s = lax.dot_general(qh, kh, (((1,), (1,)), ((), ())), preferred_element_type=F32) * hd ** -0.5
        s = s - jnp.max(s, axis=-1, keepdims=True)
        e = jnp.exp(s)
        p = e / jnp.sum(e, axis=-1, keepdims=True)
        outs.append(jnp.dot(p.astype(BF16), vh, preferred_element_type=F32).astype(BF16))
    o = jnp.concatenate(outs, axis=-1)
    y = jnp.dot(o, wo_ref[...], preferred_element_type=F32)
    o_ref[...] = _layer_norm(alpha * x + y, lw_ref[...], lb_ref[...])


def _xattn(x, kv, wq, wo, ln_w, ln_b, *, alpha, seq, mem_len, tm):
    t, d = x.shape
    inner = wq.shape[1]
    n_tblk = seq // tm
    const = lambda shape: pl.BlockSpec(shape, lambda i: (0, 0))
    return pl.pallas_call(
        functools.partial(_xattn_kernel, alpha=alpha, heads=XATTN_HEADS),
        grid=(t // tm,),
        in_specs=[pl.BlockSpec((tm, d), lambda i: (i, 0)), const((d, inner)),
                  pl.BlockSpec((mem_len, 2 * inner), lambda i: (i // n_tblk, 0)),
                  const((inner, d)), const((1, d)), const((1, d))],
        out_specs=pl.BlockSpec((tm, d), lambda i: (i, 0)),
        out_shape=jax.ShapeDtypeStruct((t, d), F32),
        compiler_params=_cp("parallel"),
    )(x, wq, kv, wo, ln_w, ln_b)


def _round_up_pow2(x, m):
    shift = m.bit_length() - 1
    return jnp.left_shift(jnp.right_shift(x + (m - 1), shift), shift)


def _route_kernel(x_ref, wr_ref, rb_ref, w_ref, lrow_ref, segtab_ref, blk_ref, seg_ref, size_all, tot, *, tm, bm):
    step = pl.program_id(0)

    @pl.when(step == 0)
    def _():
        tot[...] = jnp.zeros_like(tot)

    e_n, g_n = N_EXPERTS, N_GROUPS
    per = e_n // g_n
    def split(a):
        hi = a.astype(BF16)
        return hi, (a - hi.astype(F32)).astype(BF16)

    def nt(a, b):
        return lax.dot_general(a, b, (((1,), (1,)), ((), ())), preferred_element_type=F32)

    (w_hi, w_lo), (x_hi, x_lo) = split(wr_ref[...]), split(x_ref[...])
    logits = nt(w_hi, x_hi) + (nt(w_hi, x_lo) + nt(w_lo, x_hi))
    scores = _sigmoid(logits)
    biased = scores + rb_ref[...]
    b3 = biased.reshape(g_n, per, tm)
    member = lax.broadcasted_iota(I32, (g_n, per, tm), 1)
    top1 = jnp.max(b3, axis=1, keepdims=True)
    first = jnp.min(jnp.where(b3 == top1, member, per), axis=1, keepdims=True)
    top2 = jnp.max(jnp.where(member == first, -jnp.inf, b3), axis=1, keepdims=True)
    gs = top1 + top2
    gid = lax.broadcasted_iota(I32, (g_n, 1, tm), 0)
    rank = jnp.zeros((g_n, 1, tm), I32)
    for other in range(g_n):
        o = gs[other:other + 1]
        ahead = jnp.logical_or(o > gs, jnp.logical_and(o == gs, other < gid))
        rank = rank + jnp.where(ahead, 1, 0)
    cur = jnp.where(rank < TOPK_GROUPS, b3, -jnp.inf).reshape(e_n, tm)

    eid = lax.broadcasted_iota(I32, (e_n, tm), 0)
    picks, vals = [], []
    sel = jnp.zeros((e_n, tm), F32)
    for k in range(TOP_K):
        mx = jnp.max(cur, axis=0, keepdims=True)
        ik = jnp.min(jnp.where(cur == mx, eid, e_n), axis=0, keepdims=True)
        hit = eid == ik
        vals.append(jnp.sum(jnp.where(hit, scores, 0.0), axis=0, keepdims=True))
        cur = jnp.where(hit, -jnp.inf, cur)
        sel = jnp.where(hit, 1.0, sel)
        picks.append(ik)
    total = vals[0]
    for v in vals[1:]:
        total = total + v

    tri = jnp.where(lax.broadcasted_iota(I32, (tm, tm), 0) <= lax.broadcasted_iota(I32, (tm, tm), 1), 1.0, 0.0)
    incl = jnp.dot(sel.astype(BF16), tri.astype(BF16), preferred_element_type=F32)
    size = _round_up_pow2(jnp.broadcast_to(incl[:, tm - 1:tm], (e_n, LANES)).astype(I32), SEG_ALIGN)
    loff = _cum_rows(size, jnp.add, 0) - size
    base = loff[:, 0:1].astype(F32) + incl - 1.0
    for k in range(TOP_K):
        w_ref[k:k + 1, :] = vals[k] / total * ROUTE_SCALE
        lrow_ref[0, k:k + 1, :] = jnp.sum(jnp.where(eid == picks[k], base, 0.0),
                                          axis=0, keepdims=True).astype(I32)
    size_all[step] = size
    tot[...] = tot[...] + size

    @pl.when(step == pl.num_programs(0) - 1)
    def _():
        rows = tot[...]
        pcnt = _round_up_pow2(rows, bm)
        pend = _cum_rows(pcnt, jnp.add, 0)

        def tile_seg(i, run):
            segtab_ref[i, 0] = run
            segtab_ref[i, 1] = size_all[i]
            return run + size_all[i]

        data_start = pend - rows
        lax.fori_loop(0, pl.num_programs(0), tile_seg, data_start)

        nb = blk_ref.shape[2]
        row0 = lax.broadcasted_iota(I32, (e_n, nb), 1) * bm
        total_rows = pend[e_n - 1:e_n, 0:1]
        owner = jnp.sum(jnp.where(pend[:, 0:1] <= row0, 1, 0), axis=0, keepdims=True)
        last_owner = jnp.sum(jnp.where(pend[:, 0:1] < total_rows, 1, 0), axis=0, keepdims=True)
        blk_ref[0] = jnp.where(row0[0:1, :] < total_rows, jnp.minimum(owner, e_n - 1), last_owner)
        inside = jnp.logical_and(pend[:, 0:1] - pcnt[:, 0:1] <= row0, row0 < pend[:, 0:1])
        blk_ref[1] = jnp.sum(jnp.where(inside, jnp.clip(row0 + bm - data_start[:, 0:1], 0, bm), 0),
                             axis=0, keepdims=True)
        seg_ref[0] = pend
        seg_ref[1] = pcnt


def _route(x, router_w_t, router_b, *, tm, bm, n_blocks):
    t, d = x.shape
    e_n = N_EXPERTS
    n_t = t // tm
    assert bm & (bm - 1) == 0
    nb_pad = -(-n_blocks // LANES) * LANES
    return pl.pallas_call(
        functools.partial(_route_kernel, tm=tm, bm=bm),
        grid=(n_t,),
        in_specs=[pl.BlockSpec((tm, d), lambda i: (i, 0)),
                  pl.BlockSpec((e_n, d), lambda i: (0, 0)),
                  pl.BlockSpec((e_n, 1), lambda i: (0, 0))],
        out_specs=[pl.BlockSpec((TOP_K, tm), lambda i: (0, i)),
                   pl.BlockSpec((1, TOP_K, tm), lambda i: (i, 0, 0)),
                   pl.BlockSpec((n_t, 2, e_n, LANES), lambda i: (0, 0, 0, 0)),
                   pl.BlockSpec((2, 1, nb_pad), lambda i: (0, 0, 0)),
                   pl.BlockSpec((2, e_n, LANES), lambda i: (0, 0, 0))],
        out_shape=[jax.ShapeDtypeStruct((TOP_K, t), F32), jax.ShapeDtypeStruct((n_t, TOP_K, tm), I32),
                   jax.ShapeDtypeStruct((n_t, 2, e_n, LANES), I32), jax.ShapeDtypeStruct((2, 1, nb_pad), I32),
                   jax.ShapeDtypeStruct((2, e_n, LANES), I32)],
        scratch_shapes=[pltpu.VMEM((n_t, e_n, LANES), I32), pltpu.VMEM((e_n, LANES), I32)],
        compiler_params=_cp("arbitrary"),
    )(x, router_w_t, router_b)


def _tile_rows(tm):
    worst = TOP_K * tm + N_EXPERTS * (SEG_ALIGN - 1)
    return -(-worst // ONEHOT_ROWS) * ONEHOT_ROWS


def _onehot_rows(chunk, lrow, values, tm):
    rid = chunk * ONEHOT_ROWS + lax.broadcasted_iota(I32, (ONEHOT_ROWS, tm), 0)
    acc = jnp.zeros((ONEHOT_ROWS, tm), F32)
    for k in range(TOP_K):
        acc = jnp.where(rid == lrow[k:k + 1, :], 1.0 if values is None else values[k:k + 1, :], acc)
    return acc.astype(BF16)


def _start_segments(gstart_ref, size_ref, tile, make_copy, first=0, last=N_EXPERTS, row0=0):
    def body(e, loff):
        n = size_ref[tile * N_EXPERTS + e]

        @pl.when(n > 0)
        def _():
            make_copy(pl.multiple_of(gstart_ref[tile * N_EXPERTS + e], SEG_ALIGN),
                      pl.multiple_of(loff, SEG_ALIGN), pl.multiple_of(n, SEG_ALIGN)).start()
        return loff + n
    return lax.fori_loop(first, last, body, row0)


def _segments_within(size_ref, tile, limit):
    def body(e, carry):
        count, rows = carry
        end = rows + size_ref[tile * N_EXPERTS + e]
        fits = jnp.logical_and(count == e, end <= limit)
        return count + jnp.where(fits, 1, 0), jnp.where(fits, end, rows)
    return lax.fori_loop(0, N_EXPERTS, body, (0, 0))


def _tile_total(size_ref, tile):
    return lax.fori_loop(0, N_EXPERTS, lambda e, s: s + size_ref[tile * N_EXPERTS + e], 0)


def _wait_rows(make_copy, rows):
    @pl.when(rows > 0)
    def _():
        make_copy(0, 0, pl.multiple_of(rows, SEG_ALIGN)).wait()


def _dispatch_kernel(gstart_ref, size_ref, x_ref, lrow_ref, xs_ref, stage, inflight, sem_a, sem_b, *, tm):
    i = pl.program_id(0)
    last = pl.num_programs(0) - 1
    n_chunks = stage.shape[0] // ONEHOT_ROWS
    split_chunk = n_chunks // 2
    split = split_chunk * ONEHOT_ROWS
    assert tm <= ONEHOT_ROWS

    def copy_on(sem):
        return lambda g, loff, n: pltpu.make_async_copy(stage.at[pl.ds(loff, n), :], xs_ref.at[pl.ds(g, n), :], sem)

    xb = x_ref[...].astype(BF16)
    lrow = lrow_ref[0]
    n_rows = _tile_total(size_ref, i)

    def chunks(lo, hi):
        for c in range(lo, hi):
            def one(c=c):
                stage[c * ONEHOT_ROWS:(c + 1) * ONEHOT_ROWS, :] = jnp.dot(
                    _onehot_rows(c, lrow, None, tm), xb, preferred_element_type=F32).astype(BF16)
            if c * ONEHOT_ROWS < TOP_K * tm:
                one()
            else:
                pl.when(c * ONEHOT_ROWS < n_rows)(one)

    @pl.when(i == 0)
    def _():
        inflight[0] = 0
        inflight[1] = 0

    _wait_rows(copy_on(sem_a), inflight[0])
    chunks(0, split_chunk - 1)
    _wait_rows(copy_on(sem_b), inflight[1])
    chunks(split_chunk - 1, split_chunk)
    experts_a, rows_a = _segments_within(size_ref, i, split)
    _start_segments(gstart_ref, size_ref, i, copy_on(sem_a), 0, experts_a)
    chunks(split_chunk, n_chunks)
    rows_b = _start_segments(gstart_ref, size_ref, i, copy_on(sem_b), experts_a, N_EXPERTS, rows_a) - rows_a
    inflight[0] = rows_a
    inflight[1] = rows_b

    @pl.when(i == last)
    def _():
        _wait_rows(copy_on(sem_a), rows_a)
        _wait_rows(copy_on(sem_b), rows_b)


def _dispatch(x, lrow, gstart, size, *, rows, tm):
    t, d = x.shape
    return pl.pallas_call(
        functools.partial(_dispatch_kernel, tm=tm),
        grid_spec=pltpu.PrefetchScalarGridSpec(
            num_scalar_prefetch=2,
            grid=(t // tm,),
            in_specs=[pl.BlockSpec((tm, d), lambda i, *_: (i, 0)),
                      pl.BlockSpec((1, TOP_K, tm), lambda i, *_: (i, 0, 0))],
            out_specs=pl.BlockSpec(memory_space=pl.ANY),
            scratch_shapes=[pltpu.VMEM((_tile_rows(tm), d), BF16), pltpu.SMEM((2,), I32)]
                           + [pltpu.SemaphoreType.DMA(())] * 2,
        ),
        out_shape=jax.ShapeDtypeStruct((rows, d), BF16),
        compiler_params=_cp("arbitrary"),
    )(gstart, size, x, lrow)


def _expert_block_kernel(blk_ref, nused_ref, xs_ref, wgu_ref, wdn_ref, ys_ref, wgu_b, wdn_b, *, n_blocks, bm):
    j = pl.program_id(0)
    valid = blk_ref[n_blocks + j]
    changed = jnp.logical_or(j == 0, blk_ref[j] != blk_ref[jnp.maximum(j - 1, 0)])

    @pl.when(jnp.logical_and(valid > 0, changed))
    def _():
        wgu_b[...] = wgu_ref[...].astype(BF16)
        wdn_b[...] = wdn_ref[...].astype(BF16)

    def swiglu(rows, masked):
        first = bm - rows
        x = xs_ref[first:bm, :]
        if masked:
            x = jnp.where(first + lax.broadcasted_iota(I32, x.shape, 0) >= bm - valid, x, jnp.zeros_like(x))
        f = wdn_b.shape[0]
        gu = jnp.dot(x, wgu_b[...], preferred_element_type=F32)
        hidden = (_silu(gu[:, :f]) * gu[:, f:]).astype(BF16)
        ys_ref[first:bm, :] = jnp.dot(hidden, wdn_b[...], preferred_element_type=F32).astype(BF16)

    @pl.when(valid == bm)
    def _():
        swiglu(bm, False)

    quarter = bm // 4
    for q in range(1, 5):
        @pl.when(jnp.logical_and(jnp.logical_and(valid > (q - 1) * quarter, valid <= q * quarter), valid < bm))
        def _():
            swiglu(q * quarter, True)


def _experts_blocked(xs, blk, nused, w_gu, w_dn, layer, *, n_blocks, bm):
    rows, d = xs.shape
    f2 = w_gu.shape[3]
    f = w_dn.shape[2]
    row_blk = lambda j, bl, nu: (jnp.minimum(j, nu[0] - 1), 0)
    return pl.pallas_call(
        functools.partial(_expert_block_kernel, n_blocks=n_blocks, bm=bm),
        grid_spec=pltpu.PrefetchScalarGridSpec(
            num_scalar_prefetch=2,
            grid=(n_blocks,),
            in_specs=[pl.BlockSpec((bm, d), row_blk),
                      pl.BlockSpec((None, None, d, f2), lambda j, bl, nu: (layer, bl[j], 0, 0)),
                      pl.BlockSpec((None, None, f, d), lambda j, bl, nu: (layer, bl[j], 0, 0))],
            out_specs=pl.BlockSpec((bm, d), row_blk),
            scratch_shapes=[pltpu.VMEM((d, f2), BF16), pltpu.VMEM((f, d), BF16)],
        ),
        out_shape=jax.ShapeDtypeStruct((rows, d), BF16),
        compiler_params=_cp("arbitrary"),
    )(blk, nused, xs, w_gu, w_dn)


def _combine_tile_kernel(gstart_ref, size_ref, x_ref, lrow_ref, w_ref, sdn_ref, lw_ref, lb_ref, sgu_hbm, ys_ref,
                         o_ref, ybuf, sgu, sems, wsem, *, tm, alpha):
    i = pl.program_id(0)
    last_tile = pl.num_programs(0) - 1
    n_chunks = ybuf.shape[1] // ONEHOT_ROWS
    sure_chunks = (TOP_K * tm) // ONEHOT_ROWS

    def contract_rows(w_rows, y_rows):
        return lax.dot_general(w_rows, y_rows, (((0,), (0,)), ((), ())), preferred_element_type=F32)

    def copy_to(slot):
        return lambda g, loff, n: pltpu.make_async_copy(ys_ref.at[pl.ds(g, n), :], ybuf.at[slot, pl.ds(loff, n), :],
                                                        sems.at[slot])

    def fetch(tile, slot):
        _start_segments(gstart_ref, size_ref, tile, copy_to(slot))

    n_rows = _tile_total(size_ref, i)

    @pl.when(i == 0)
    def _():
        ybuf[...] = jnp.zeros_like(ybuf)
        weights = pltpu.make_async_copy(sgu_hbm, sgu, wsem)
        weights.start()
        weights.wait()
        fetch(0, 0)

    for slot in range(2):
        @pl.when(i % 2 == slot)
        def _():
            @pl.when(i < last_tile)
            def _():
                fetch(i + 1, 1 - slot)

            x = x_ref[...]
            f = sdn_ref.shape[0]
            gu = jnp.dot(x.astype(BF16), sgu[...], preferred_element_type=F32)
            hidden = (_silu(gu[:, :f]) * gu[:, f:]).astype(BF16)
            acc = jnp.dot(hidden, sdn_ref[...], preferred_element_type=F32)
            _wait_rows(copy_to(slot), n_rows)
            lrow, w = lrow_ref[0], w_ref[...]
            for c in range(sure_chunks):
                rows = slice(c * ONEHOT_ROWS, (c + 1) * ONEHOT_ROWS)
                acc = acc + contract_rows(_onehot_rows(c, lrow, w, tm), ybuf[slot, rows, :])
            o_ref[...] = acc
            for c in range(sure_chunks, n_chunks):
                @pl.when(c * ONEHOT_ROWS < n_rows)
                def _():
                    rows = slice(c * ONEHOT_ROWS, (c + 1) * ONEHOT_ROWS)
                    o_ref[...] += contract_rows(_onehot_rows(c, lrow, w, tm), ybuf[slot, rows, :])
            o_ref[...] = _layer_norm(alpha * x + o_ref[...], lw_ref[...], lb_ref[...])


def _combine(x, ys, lrow, wts, gstart, size, s_gu, s_dn, ln_w, ln_b, *, alpha, tm):
    t, d = x.shape
    const = lambda shape: pl.BlockSpec(shape, lambda i, *_: (0, 0))
    return pl.pallas_call(
        functools.partial(_combine_tile_kernel, tm=tm, alpha=alpha),
        grid_spec=pltpu.PrefetchScalarGridSpec(
            num_scalar_prefetch=2,
            grid=(t // tm,),
            in_specs=[pl.BlockSpec((tm, d), lambda i, *_: (i, 0)),
                      pl.BlockSpec((1, TOP_K, tm), lambda i, *_: (i, 0, 0)),
                      pl.BlockSpec((TOP_K, tm), lambda i, *_: (0, i)),
                      const(s_dn.shape), const((1, d)), const((1, d)),
                      pl.BlockSpec(memory_space=pl.ANY), pl.BlockSpec(memory_space=pl.ANY)],
            out_specs=pl.BlockSpec((tm, d), lambda i, *_: (i, 0)),
            scratch_shapes=[pltpu.VMEM((2, _tile_rows(tm), d), BF16), pltpu.VMEM(s_gu.shape, BF16),
                            pltpu.SemaphoreType.DMA((2,)), pltpu.SemaphoreType.DMA(())],
        ),
        out_shape=jax.ShapeDtypeStruct((t, d), F32),
        compiler_params=_cp("arbitrary"),
    )(gstart, size, x, lrow, wts, s_dn, ln_w, ln_b, s_gu, ys)


def _mixer_sublayer(x, w_in_all, layer, gate_b, pool_w, pool_scale, conv_w, mlstm_norm_w, ret_norm_w, w_branch,
                    w_out, ln_w, ln_b, *, batch, seq, alpha):
    t, d = x.shape
    width = d // N_BRANCH
    heads = width // HEAD_DIM
    gate_off = 8 * width
    ret_off = gate_off
    g_off = ret_off + 4 * width
    if_off = g_off + N_BRANCH * d
    w_bf16 = _realign_cast(w_in_all, layer, lo_col=gate_off, hi_col=if_off, shift=2 * heads, tr=1024, tn=512)
    gate_bias = jnp.pad(gate_b, (0, LANES - 2 * heads)).reshape(1, LANES)

    proj = _matmul(x, w_bf16, tm=1024, tn=1024, ncols=g_off)
    gates = _matmul(x, w_bf16, tm=1024, tn=LANES, ncols=LANES, col0=if_off)
    y_pool, y_conv = _pool_conv(proj, pool_w.astype(BF16), pool_scale.reshape(1, width), conv_w,
                                seq=seq, width=width, tb=512)
    y_mlstm = _mlstm(proj, gates, gate_bias, mlstm_norm_w.reshape(1, width),
                     batch=batch, seq=seq, width=width, col0=4)
    y_ret = _retention(proj, ret_norm_w.reshape(1, width), batch=batch, seq=seq, width=width,
                       col0=ret_off // width)
    merged = _merge(x, (y_pool, y_conv, y_mlstm, y_ret), w_bf16, g_off, w_branch.astype(BF16),
                    tm=512, tn=512)
    return _proj_ln(merged, w_out.astype(BF16), x, ln_w, ln_b, alpha=alpha, tm=512)


def _xattn_sublayer(x, mem2d, wq, wk, wv, wo, ln_w, ln_b, *, seq, mem_len, alpha):
    w_kv = jnp.concatenate([wk, wv], axis=1).astype(BF16)
    kv = _matmul(mem2d, w_kv, tm=min(mem2d.shape[0], 1024), tn=512)
    return _xattn(x, kv, wq.astype(BF16), wo.astype(BF16), ln_w, ln_b,
                  alpha=alpha, seq=seq, mem_len=mem_len, tm=512)


def _moe_sublayer(x, router_w, router_b, w_gu, w_dn, layer, s_gu, s_dn, ln_w, ln_b, *, alpha):
    t, d = x.shape
    e_n, bm = N_EXPERTS, MOE_BM
    tm = ROUTE_TM
    n_blocks = -(-(t * TOP_K + (t // tm) * e_n * (SEG_ALIGN - 1)) // bm) + e_n
    wts, lrow, segtab, blk, seg = _route(x, router_w.T, router_b.reshape(e_n, 1), tm=tm, bm=bm, n_blocks=n_blocks)
    gstart, size = segtab[:, 0, :, 0].reshape(-1), segtab[:, 1, :, 0].reshape(-1)
    nused = seg[0, e_n - 1, 0] // bm

    xs = _dispatch(x, lrow, gstart, size, rows=n_blocks * bm, tm=tm)
    ys = _experts_blocked(xs, blk[:, 0, :n_blocks].reshape(-1), nused.reshape(1), w_gu, w_dn, layer,
                          n_blocks=n_blocks, bm=bm)
    return _combine(x, ys, lrow, wts, gstart, size, s_gu.astype(BF16), s_dn.astype(BF16), ln_w, ln_b,
                    alpha=alpha, tm=tm)


def kernel(x, mem, w_in, mlstm_gate_b, pool_w, pool_scale, conv_w, mlstm_norm_w, ret_norm_w, w_branch,
           w_mix_out, xa_wq, xa_wk, xa_wv, xa_wo, router_w, router_b, moe_w_gu, moe_w_dn, shared_w_gu,
           shared_w_dn, ln_w, ln_b):
    batch, seq, d = x.shape
    depth = w_in.shape[0]
    mem_len = mem.shape[1]
    alpha = (2 * depth) ** 0.25
    h = x.reshape(batch * seq, d)
    mem2d = mem.reshape(batch * mem_len, d)
    for l in range(depth):
        lw = ln_w[l].reshape(3, 1, d)
        lb = ln_b[l].reshape(3, 1, d)
        h = _mixer_sublayer(h, w_in, l, mlstm_gate_b[l], pool_w[l], pool_scale[l], conv_w[l], mlstm_norm_w[l],
                            ret_norm_w[l], w_branch[l], w_mix_out[l], lw[0], lb[0],
                            batch=batch, seq=seq, alpha=alpha)
        h = _xattn_sublayer(h, mem2d, xa_wq[l], xa_wk[l], xa_wv[l], xa_wo[l], lw[1], lb[1],
                            seq=seq, mem_len=mem_len, alpha=alpha)
        h = _moe_sublayer(h, router_w[l], router_b[l], moe_w_gu, moe_w_dn, l, shared_w_gu[l], shared_w_dn[l],
                          lw[2], lb[2], alpha=alpha)
    return h.reshape(batch, seq, d)
```

```python
import functools

import numpy as np
import jax
import jax.numpy as jnp
from jax import lax
from jax.experimental import pallas as pl
from jax.experimental.pallas import tpu as pltpu

F32 = jnp.float32
BF16 = jnp.bfloat16
I32 = jnp.int32

N_BRANCH = 4
HEAD_DIM = 128
POOL_WINDOWS = (2, 4, 8, 16)
CONV_WIDTH = 3
ROPE_BASE = 10000.0
XATTN_HEADS = 4
N_EXPERTS = 64
TOP_K = 8
N_GROUPS = 8
TOPK_GROUPS = 4
ROUTE_SCALE = 2.5
LN_EPS = 1e-5

LANES = 128
V7X_VMEM_BYTES = 64 * 1024 * 1024
VMEM_LIMIT = V7X_VMEM_BYTES - 8 * 1024 * 1024

SCAN_CHUNK = 256
SCAN_BLOCK = 512
SCAN_BATCH = 4
HALO = 16
MOE_BM = 1024
ROUTE_TM = 256
SEG_ALIGN = 16
ONEHOT_ROWS = 512


def _cp(*sem):
    return pltpu.CompilerParams(dimension_semantics=sem, vmem_limit_bytes=VMEM_LIMIT)


def _sigmoid(x):
    return 1.0 / (1.0 + jnp.exp(-x))


def _silu(x):
    return x * _sigmoid(x)


def _log_sigmoid(x):
    return jnp.minimum(x, 0.0) - jnp.log(1.0 + jnp.exp(-jnp.abs(x)))


def _layer_norm(z, w, b):
    mu = jnp.mean(z, axis=-1, keepdims=True)
    d = z - mu
    var = jnp.mean(d * d, axis=-1, keepdims=True)
    return d * lax.rsqrt(var + LN_EPS) * w + b


def _head_norm(h, w):
    mu = jnp.mean(h, axis=-1, keepdims=True)
    d = h - mu
    var = jnp.mean(d * d, axis=-1, keepdims=True)
    return d * lax.rsqrt(var + LN_EPS) * w


def _mm_kernel(x_ref, w_ref, o_ref, xb_ref):
    @pl.when(pl.program_id(1) == 0)
    def _():
        xb_ref[...] = x_ref[...].astype(BF16)

    o_ref[...] = jnp.dot(xb_ref[...], w_ref[...], preferred_element_type=F32)


def _realign_cast_kernel(a_ref, b_ref, o_ref, *, shift, lo, hi):
    j = pl.program_id(1)
    tn = o_ref.shape[1]
    shifted = jnp.logical_and(j >= lo, j < hi)

    @pl.when(shifted)
    def _():
        both = jnp.concatenate([a_ref[...], b_ref[...]], axis=1)
        o_ref[...] = both[:, shift:shift + tn].astype(BF16)

    @pl.when(jnp.logical_not(shifted))
    def _():
        o_ref[...] = a_ref[...].astype(BF16)


def _realign_cast(w_all, layer, *, lo_col, hi_col, shift, tr, tn):
    _, rows, _ = w_all.shape
    lo, hi = lo_col // tn, hi_col // tn
    src = lambda j: jnp.where(j == hi, lo, j)
    return pl.pallas_call(
        functools.partial(_realign_cast_kernel, shift=shift, lo=lo, hi=hi),
        grid=(rows // tr, hi + 1),
        in_specs=[pl.BlockSpec((None, tr, tn), lambda i, j: (layer, i, src(j))),
                  pl.BlockSpec((None, tr, LANES), lambda i, j: (layer, i, (src(j) + 1) * (tn // LANES)))],
        out_specs=pl.BlockSpec((tr, tn), lambda i, j: (i, j)),
        out_shape=jax.ShapeDtypeStruct((rows, hi_col + tn), BF16),
        compiler_params=_cp("parallel", "parallel"),
    )(w_all, w_all)


def _matmul(x, w, *, tm, tn, ncols=None, col0=0):
    t, k = x.shape
    n = w.shape[1] if ncols is None else ncols
    return pl.pallas_call(
        _mm_kernel,
        grid=(t // tm, n // tn),
        in_specs=[pl.BlockSpec((tm, k), lambda i, j: (i, 0)),
                  pl.BlockSpec((k, tn), lambda i, j: (0, col0 // tn + j))],
        out_specs=pl.BlockSpec((tm, tn), lambda i, j: (i, j)),
        out_shape=jax.ShapeDtypeStruct((t, n), F32),
        scratch_shapes=[pltpu.VMEM((tm, k), BF16)],
        compiler_params=_cp("parallel", "arbitrary"),
    )(x, w)


def _poolconv_kernel(u_ref, uh_ref, h_ref, hh_ref, b_ref, c_ref, ch_ref, pw_ref, ps_ref, cw_ref,
                     yp_ref, yc_ref, ubuf, zbuf, *, tb, n_tblk):
    first = (pl.program_id(0) % n_tblk) == 0
    ubuf[0:HALO, :] = jnp.where(first, 0.0, uh_ref[...])
    ubuf[HALO:HALO + tb, :] = u_ref[...]
    zbuf[0:HALO, :] = jnp.where(first, 0.0, ch_ref[...] * hh_ref[...])
    zbuf[HALO:HALO + tb, :] = c_ref[...] * h_ref[...]

    t_pos = (pl.program_id(0) % n_tblk) * tb + lax.broadcasted_iota(I32, (tb, LANES), 0)
    gw = u_ref.shape[1] // len(POOL_WINDOWS)
    for grp, win in enumerate(POOL_WINDOWS):
        lanes = slice(grp * gw, (grp + 1) * gw)
        cur = ubuf[HALO:HALO + tb, lanes]
        acc = cur
        for lag in range(1, win):
            acc = acc + ubuf[HALO - lag:HALO - lag + tb, lanes]
        count = jnp.minimum(t_pos + 1, win).astype(F32)
        mixed = acc / count - cur
        y = jnp.dot(mixed.astype(BF16), pw_ref[grp], preferred_element_type=F32)
        yp_ref[:, lanes] = (y * ps_ref[:, lanes]).astype(BF16)

    conv = cw_ref[0:1, :] * zbuf[HALO:HALO + tb, :]
    for lag in range(1, CONV_WIDTH):
        conv = conv + cw_ref[lag:lag + 1, :] * zbuf[HALO - lag:HALO - lag + tb, :]
    yc_ref[...] = (b_ref[...] * conv).astype(BF16)


def _pool_conv(proj, pool_w, pool_scale, conv_w, *, seq, width, tb):
    t = proj.shape[0]
    n_tblk = seq // tb
    ratio = tb // HALO

    def cur(col):
        return pl.BlockSpec((tb, width), lambda g: (g, col))

    def halo(col):
        return pl.BlockSpec((HALO, width), lambda g: (jnp.maximum(g * ratio - 1, 0), col))

    full = lambda shape: pl.BlockSpec(shape, lambda g: (0,) * len(shape))
    return pl.pallas_call(
        functools.partial(_poolconv_kernel, tb=tb, n_tblk=n_tblk),
        grid=(t // tb,),
        in_specs=[cur(0), halo(0), cur(1), halo(1), cur(2), cur(3), halo(3),
                  full(pool_w.shape), full(pool_scale.shape), full(conv_w.shape)],
        out_specs=[pl.BlockSpec((tb, width), lambda g: (g, 0))] * 2,
        out_shape=[jax.ShapeDtypeStruct((t, width), BF16)] * 2,
        scratch_shapes=[pltpu.VMEM((HALO + tb, width), F32)] * 2,
        compiler_params=_cp("parallel"),
    )(proj, proj, proj, proj, proj, proj, proj, pool_w, pool_scale, conv_w)


def _cum_rows(x, op, fill):
    n = x.shape[0]
    row = lax.broadcasted_iota(I32, x.shape, 0)
    shift = 1
    while shift < n:
        x = op(x, jnp.where(row >= shift, pltpu.roll(x, shift, axis=0), fill))
        shift *= 2
    return x


def _mlstm_kernel(q_ref, k_ref, v_ref, o_ref, g_ref, gb_ref, nw_ref, y_ref, c_scr, n_scr, m_scr,
                  *, chunk, heads):
    @pl.when(pl.program_id(1) == 0)
    def _():
        c_scr[...] = jnp.zeros_like(c_scr)
        n_scr[...] = jnp.zeros_like(n_scr)
        m_scr[...] = jnp.zeros_like(m_scr)

    nb, tb = q_ref.shape[0], q_ref.shape[1]
    hd = HEAD_DIM
    scale = hd ** -0.5
    tri = (lax.broadcasted_iota(I32, (chunk, chunk), 0) >= lax.broadcasted_iota(I32, (chunk, chunk), 1))
    for c, bb in [(c, bb) for c in range(tb // chunk) for bb in range(nb)]:
        rows = slice(c * chunk, (c + 1) * chunk)
        gates = g_ref[bb, rows, :] + gb_ref[...]
        lf = _log_sigmoid(pltpu.roll(gates, LANES - heads, axis=1))
        cumf = _cum_rows(lf, jnp.add, 0.0)
        a = gates - cumf
        m_prev = m_scr[bb]
        mu = jnp.maximum(_cum_rows(a, jnp.maximum, -jnp.inf), m_prev)
        mu_last = mu[chunk - 1:chunk, :]
        a_t = a.T
        for h in range(heads):
            cols = slice(h * hd, (h + 1) * hd)
            q = q_ref[bb, rows, cols]
            k = k_ref[bb, rows, cols] * scale
            v = v_ref[bb, rows, cols]
            qb, kb, vb = q.astype(BF16), k.astype(BF16), v.astype(BF16)
            mu_col = mu[:, h:h + 1]
            a_col = a[:, h:h + 1]
            m_prev_h = m_prev[:, h:h + 1]
            mu_last_h = mu_last[:, h:h + 1]
            dmat = jnp.exp(jnp.where(tri, a_t[h:h + 1, :] - mu_col, -jnp.inf))
            s = lax.dot_general(qb, kb, (((1,), (1,)), ((), ())), preferred_element_type=F32)
            p = dmat * s
            inter = jnp.exp(m_prev_h - mu_col)
            state = bb * heads + h
            c_h = c_scr[state]
            n_h = n_scr[state]
            num = inter * jnp.dot(qb, c_h.astype(BF16), preferred_element_type=F32) \
                + jnp.dot(p.astype(BF16), vb, preferred_element_type=F32)
            den = inter * jnp.sum(q * n_h, axis=-1, keepdims=True) + jnp.sum(p, axis=-1, keepdims=True)
            floor = jnp.exp(-(cumf[:, h:h + 1] + mu_col))
            h_out = num / jnp.maximum(jnp.abs(den), floor)

            wg = jnp.exp(a_col - mu_last_h)
            decay = jnp.exp(m_prev_h - mu_last_h)
            kw = k * wg
            c_scr[state] = decay * c_h + lax.dot_general(kw.astype(BF16), vb, (((0,), (0,)), ((), ())),
                                                         preferred_element_type=F32)
            n_scr[state] = decay * n_h + jnp.sum(kw, axis=0, keepdims=True)

            gated = _sigmoid(o_ref[bb, rows, cols]) * h_out
            y_ref[bb, rows, cols] = _head_norm(gated, nw_ref[:, cols]).astype(BF16)
        m_scr[bb] = cumf[chunk - 1:chunk, :] + mu_last


def _mlstm(proj, gates, gate_b, norm_w, *, batch, seq, width, col0):
    t = proj.shape[0]
    heads = width // HEAD_DIM
    nb = SCAN_BATCH if batch % SCAN_BATCH == 0 else 1
    proj3 = proj.reshape(batch, seq, proj.shape[1])
    gates3 = gates.reshape(batch, seq, LANES)
    blk = lambda col: pl.BlockSpec((nb, SCAN_BLOCK, width), lambda b, i: (b, i, col))
    y = pl.pallas_call(
        functools.partial(_mlstm_kernel, chunk=SCAN_CHUNK, heads=heads),
        grid=(batch // nb, seq // SCAN_BLOCK),
        in_specs=[blk(col0), blk(col0 + 1), blk(col0 + 2), blk(col0 + 3),
                  pl.BlockSpec((nb, SCAN_BLOCK, LANES), lambda b, i: (b, i, 0)),
                  pl.BlockSpec((1, LANES), lambda b, i: (0, 0)),
                  pl.BlockSpec((1, width), lambda b, i: (0, 0))],
        out_specs=pl.BlockSpec((nb, SCAN_BLOCK, width), lambda b, i: (b, i, 0)),
        out_shape=jax.ShapeDtypeStruct((batch, seq, width), BF16),
        scratch_shapes=[pltpu.VMEM((nb * heads, HEAD_DIM, HEAD_DIM), F32),
                        pltpu.VMEM((nb * heads, 1, HEAD_DIM), F32),
                        pltpu.VMEM((nb, 1, LANES), F32)],
        compiler_params=_cp("parallel", "arbitrary"),
    )(proj3, proj3, proj3, proj3, gates3, gate_b, norm_w)
    return y.reshape(t, width)


def _ret_kernel(q_ref, k_ref, v_ref, g_ref, cos_ref, sin_ref, intra_ref, cross_ref, zeta_ref, nw_ref,
                y_ref, r_scr, *, chunk, heads, chunk_decay):
    @pl.when(pl.program_id(1) == 0)
    def _():
        r_scr[...] = jnp.zeros_like(r_scr)

    tb = q_ref.shape[0]
    hd = HEAD_DIM
    scale = hd ** -0.5
    for c in range(tb // chunk):
        rows = slice(c * chunk, (c + 1) * chunk)
        cos = cos_ref[rows, :]
        sin = sin_ref[rows, :]
        for h in range(heads):
            cols = slice(h * hd, (h + 1) * hd)
            q = q_ref[rows, cols]
            k = k_ref[rows, cols]
            q = q * cos + pltpu.roll(q, hd // 2, axis=1) * sin
            k = (k * cos + pltpu.roll(k, hd // 2, axis=1) * sin) * scale
            vb = v_ref[rows, cols].astype(BF16)
            qb = q.astype(BF16)
            s = lax.dot_general(qb, k.astype(BF16), (((1,), (1,)), ((), ())), preferred_element_type=F32)
            inner = jnp.dot((s * intra_ref[h]).astype(BF16), vb, preferred_element_type=F32)
            r_h = r_scr[h]
            crs = jnp.dot(qb, r_h.astype(BF16), preferred_element_type=F32) * cross_ref[h]
            r_scr[h] = chunk_decay[h] * r_h + lax.dot_general(
                (k * zeta_ref[h]).astype(BF16), vb, (((0,), (0,)), ((), ())), preferred_element_type=F32)
            y_ref[rows, cols] = (_silu(g_ref[rows, cols]) * _head_norm(inner + crs, nw_ref[:, cols])).astype(BF16)


def _retention(proj, norm_w, *, batch, seq, width, col0):
    t = proj.shape[0]
    heads = width // HEAD_DIM
    chunk = SCAN_CHUNK
    n_tblk = seq // SCAN_BLOCK
    half = HEAD_DIM // 2
    inv = np.float32(ROPE_BASE) ** (-np.arange(half, dtype=np.float32) / np.float32(half))
    ang = (np.arange(seq, dtype=np.float32)[:, None] * inv[None, :]).astype(np.float64)
    cos_t = jnp.asarray(np.concatenate([np.cos(ang), np.cos(ang)], axis=-1), F32)
    sin_t = jnp.asarray(np.concatenate([-np.sin(ang), np.sin(ang)], axis=-1), F32)
    log_g = jnp.log(1.0 - 2.0 ** (-5.0 - jnp.arange(heads, dtype=F32)))
    tt = jnp.arange(chunk, dtype=F32)
    lag = tt[:, None] - tt[None, :]
    intra = jnp.where(lag >= 0, jnp.exp(jnp.maximum(lag, 0.0)[None] * log_g[:, None, None]), 0.0)
    cross = jnp.broadcast_to(jnp.exp((tt + 1.0)[None, :] * log_g[:, None])[:, :, None], (heads, chunk, HEAD_DIM))
    zeta = jnp.broadcast_to(jnp.exp((chunk - 1.0 - tt)[None, :] * log_g[:, None])[:, :, None],
                            (heads, chunk, HEAD_DIM))
    chunk_decay = tuple(float((1.0 - 2.0 ** (-5.0 - h)) ** chunk) for h in range(heads))

    blk = lambda col: pl.BlockSpec((SCAN_BLOCK, width), lambda b, i: (b * n_tblk + i, col))
    pos = pl.BlockSpec((SCAN_BLOCK, HEAD_DIM), lambda b, i: (i, 0))
    full3 = lambda a: pl.BlockSpec(a.shape, lambda b, i: (0, 0, 0))
    return pl.pallas_call(
        functools.partial(_ret_kernel, chunk=chunk, heads=heads, chunk_decay=chunk_decay),
        grid=(batch, n_tblk),
        in_specs=[blk(col0), blk(col0 + 1), blk(col0 + 2), blk(col0 + 3), pos, pos,
                  full3(intra), full3(cross), full3(zeta),
                  pl.BlockSpec((1, width), lambda b, i: (0, 0))],
        out_specs=pl.BlockSpec((SCAN_BLOCK, width), lambda b, i: (b * n_tblk + i, 0)),
        out_shape=jax.ShapeDtypeStruct((t, width), BF16),
        scratch_shapes=[pltpu.VMEM((heads, HEAD_DIM, HEAD_DIM), F32)],
        compiler_params=_cp("parallel", "arbitrary"),
    )(proj, proj, proj, proj, cos_t, sin_t, intra, cross, zeta, norm_w)


def _merge_kernel(x_ref, *refs):
    y_refs, wg_refs = refs[:N_BRANCH], refs[N_BRANCH:2 * N_BRANCH]
    wb_ref, o_ref, xb_ref = refs[2 * N_BRANCH:]

    @pl.when(pl.program_id(1) == 0)
    def _():
        xb_ref[...] = x_ref[...].astype(BF16)

    xb = xb_ref[...]
    acc = None
    for n in range(N_BRANCH):
        gate = _sigmoid(jnp.dot(xb, wg_refs[n][...], preferred_element_type=F32))
        term = gate * jnp.dot(y_refs[n][...], wb_ref[n], preferred_element_type=F32)
        acc = term if acc is None else acc + term
    o_ref[...] = acc.astype(BF16)


def _merge(x, branches, w_gate, gate_col0, w_branch, *, tm, tn):
    t, d = x.shape
    width = branches[0].shape[1]
    ybs = pl.BlockSpec((tm, width), lambda i, j: (i, 0))

    def gate_spec(n):
        return pl.BlockSpec((d, tn), lambda i, j: (0, (gate_col0 + n * d) // tn + j))

    return pl.pallas_call(
        _merge_kernel,
        grid=(t // tm, d // tn),
        in_specs=[pl.BlockSpec((tm, d), lambda i, j: (i, 0))] + [ybs] * N_BRANCH
                 + [gate_spec(n) for n in range(N_BRANCH)]
                 + [pl.BlockSpec((N_BRANCH, width, tn), lambda i, j: (0, 0, j))],
        out_specs=pl.BlockSpec((tm, tn), lambda i, j: (i, j)),
        out_shape=jax.ShapeDtypeStruct((t, d), BF16),
        scratch_shapes=[pltpu.VMEM((tm, d), BF16)],
        compiler_params=_cp("parallel", "arbitrary"),
    )(x, *branches, *([w_gate] * N_BRANCH), w_branch)


def _proj_ln_kernel(a_ref, w_ref, r_ref, lw_ref, lb_ref, o_ref, *, alpha):
    y = jnp.dot(a_ref[...], w_ref[...], preferred_element_type=F32)
    o_ref[...] = _layer_norm(alpha * r_ref[...] + y, lw_ref[...], lb_ref[...])


def _proj_ln(a, w, resid, ln_w, ln_b, *, alpha, tm):
    t, k = a.shape
    d = w.shape[1]
    row = lambda n: pl.BlockSpec((tm, n), lambda i: (i, 0))
    const = lambda shape: pl.BlockSpec(shape, lambda i: (0, 0))
    return pl.pallas_call(
        functools.partial(_proj_ln_kernel, alpha=alpha),
        grid=(t // tm,),
        in_specs=[row(k), const((k, d)), row(d), const((1, d)), const((1, d))],
        out_specs=row(d),
        out_shape=jax.ShapeDtypeStruct((t, d), F32),
        compiler_params=_cp("parallel"),
    )(a, w, resid, ln_w, ln_b)


def _xattn_kernel(x_ref, wq_ref, kv_ref, wo_ref, lw_ref, lb_ref, o_ref, *, alpha, heads):
    x = x_ref[...]
    hd = HEAD_DIM
    inner = heads * hd
    q = jnp.dot(x.astype(BF16), wq_ref[...], preferred_element_type=F32)
    outs = []
    for h in range(heads):
        qh = q[:, h * hd:(h + 1) * hd].astype(BF16)
        kh = kv_ref[:, h * hd:(h + 1) * hd].astype(BF16)
        vh = kv_ref[:, inner + h * hd:inner + (h + 1) * hd].astype(BF16)
        s = lax.dot_general(qh, kh, (((1,), (1,)), ((), ())), preferred_element_type=F32) * hd ** -0.5
        s = s - jnp.max(s, axis=-1, keepdims=True)
        e = jnp.exp(s)
        p = e / jnp.sum(e, axis=-1, keepdims=True)
        outs.append(jnp.dot(p.astype(BF16), vh, preferred_element_type=F32).astype(BF16))
    o = jnp.concatenate(outs, axis=-1)
    y = jnp.dot(o, wo_ref[...], preferred_element_type=F32)
    o_ref[...] = _layer_norm(alpha * x + y, lw_ref[...], lb_ref[...])


def _xattn(x, kv, wq, wo, ln_w, ln_b, *, alpha, seq, mem_len, tm):
    t, d = x.shape
    inner = wq.shape[1]
    n_tblk = seq // tm
    const = lambda shape: pl.BlockSpec(shape, lambda i: (0, 0))
    return pl.pallas_call(
        functools.partial(_xattn_kernel, alpha=alpha, heads=XATTN_HEADS),
        grid=(t // tm,),
        in_specs=[pl.BlockSpec((tm, d), lambda i: (i, 0)), const((d, inner)),
                  pl.BlockSpec((mem_len, 2 * inner), lambda i: (i // n_tblk, 0)),
                  const((inner, d)), const((1, d)), const((1, d))],
        out_specs=pl.BlockSpec((tm, d), lambda i: (i, 0)),
        out_shape=jax.ShapeDtypeStruct((t, d), F32),
        compiler_params=_cp("parallel"),
    )(x, wq, kv, wo, ln_w, ln_b)


def _round_up_pow2(x, m):
    shift = m.bit_length() - 1
    return jnp.left_shift(jnp.right_shift(x + (m - 1), shift), shift)


def _route_kernel(x_ref, wr_ref, rb_ref, w_ref, lrow_ref, segtab_ref, blk_ref, seg_ref, size_all, tot, *, tm, bm):
    step = pl.program_id(0)

    @pl.when(step == 0)
    def _():
        tot[...] = jnp.zeros_like(tot)

    e_n, g_n = N_EXPERTS, N_GROUPS
    per = e_n // g_n
    def split(a):
        hi = a.astype(BF16)
        return hi, (a - hi.astype(F32)).astype(BF16)

    def nt(a, b):
        return lax.dot_general(a, b, (((1,), (1,)), ((), ())), preferred_element_type=F32)

    (w_hi, w_lo), (x_hi, x_lo) = split(wr_ref[...]), split(x_ref[...])
    logits = nt(w_hi, x_hi) + (nt(w_hi, x_lo) + nt(w_lo, x_hi))
    scores = _sigmoid(logits)
    biased = scores + rb_ref[...]
    b3 = biased.reshape(g_n, per, tm)
    member = lax.broadcasted_iota(I32, (g_n, per, tm), 1)
    top1 = jnp.max(b3, axis=1, keepdims=True)
    first = jnp.min(jnp.where(b3 == top1, member, per), axis=1, keepdims=True)
    top2 = jnp.max(jnp.where(member == first, -jnp.inf, b3), axis=1, keepdims=True)
    gs = top1 + top2
    gid = lax.broadcasted_iota(I32, (g_n, 1, tm), 0)
    rank = jnp.zeros((g_n, 1, tm), I32)
    for other in range(g_n):
        o = gs[other:other + 1]
        ahead = jnp.logical_or(o > gs, jnp.logical_and(o == gs, other < gid))
        rank = rank + jnp.where(ahead, 1, 0)
    cur = jnp.where(rank < TOPK_GROUPS, b3, -jnp.inf).reshape(e_n, tm)

    eid = lax.broadcasted_iota(I32, (e_n, tm), 0)
    picks, vals = [], []
    sel = jnp.zeros((e_n, tm), F32)
    for k in range(TOP_K):
        mx = jnp.max(cur, axis=0, keepdims=True)
        ik = jnp.min(jnp.where(cur == mx, eid, e_n), axis=0, keepdims=True)
        hit = eid == ik
        vals.append(jnp.sum(jnp.where(hit, scores, 0.0), axis=0, keepdims=True))
        cur = jnp.where(hit, -jnp.inf, cur)
        sel = jnp.where(hit, 1.0, sel)
        picks.append(ik)
    total = vals[0]
    for v in vals[1:]:
        total = total + v

    tri = jnp.where(lax.broadcasted_iota(I32, (tm, tm), 0) <= lax.broadcasted_iota(I32, (tm, tm), 1), 1.0, 0.0)
    incl = jnp.dot(sel.astype(BF16), tri.astype(BF16), preferred_element_type=F32)
    size = _round_up_pow2(jnp.broadcast_to(incl[:, tm - 1:tm], (e_n, LANES)).astype(I32), SEG_ALIGN)
    loff = _cum_rows(size, jnp.add, 0) - size
    base = loff[:, 0:1].astype(F32) + incl - 1.0
    for k in range(TOP_K):
        w_ref[k:k + 1, :] = vals[k] / total * ROUTE_SCALE
        lrow_ref[0, k:k + 1, :] = jnp.sum(jnp.where(eid == picks[k], base, 0.0),
                                          axis=0, keepdims=True).astype(I32)
    size_all[step] = size
    tot[...] = tot[...] + size

    @pl.when(step == pl.num_programs(0) - 1)
    def _():
        rows = tot[...]
        pcnt = _round_up_pow2(rows, bm)
        pend = _cum_rows(pcnt, jnp.add, 0)

        def tile_seg(i, run):
            segtab_ref[i, 0] = run
            segtab_ref[i, 1] = size_all[i]
            return run + size_all[i]

        data_start = pend - rows
        lax.fori_loop(0, pl.num_programs(0), tile_seg, data_start)

        nb = blk_ref.shape[2]
        row0 = lax.broadcasted_iota(I32, (e_n, nb), 1) * bm
        total_rows = pend[e_n - 1:e_n, 0:1]
        owner = jnp.sum(jnp.where(pend[:, 0:1] <= row0, 1, 0), axis=0, keepdims=True)
        last_owner = jnp.sum(jnp.where(pend[:, 0:1] < total_rows, 1, 0), axis=0, keepdims=True)
        blk_ref[0] = jnp.where(row0[0:1, :] < total_rows, jnp.minimum(owner, e_n - 1), last_owner)
        inside = jnp.logical_and(pend[:, 0:1] - pcnt[:, 0:1] <= row0, row0 < pend[:, 0:1])
        blk_ref[1] = jnp.sum(jnp.where(inside, jnp.clip(row0 + bm - data_start[:, 0:1], 0, bm), 0),
                             axis=0, keepdims=True)
        seg_ref[0] = pend
        seg_ref[1] = pcnt


def _route(x, router_w_t, router_b, *, tm, bm, n_blocks):
    t, d = x.shape
    e_n = N_EXPERTS
    n_t = t // tm
    assert bm & (bm - 1) == 0
    nb_pad = -(-n_blocks // LANES) * LANES
    return pl.pallas_call(
        functools.partial(_route_kernel, tm=tm, bm=bm),
        grid=(n_t,),
        in_specs=[pl.BlockSpec((tm, d), lambda i: (i, 0)),
                  pl.BlockSpec((e_n, d), lambda i: (0, 0)),
                  pl.BlockSpec((e_n, 1), lambda i: (0, 0))],
        out_specs=[pl.BlockSpec((TOP_K, tm), lambda i: (0, i)),
                   pl.BlockSpec((1, TOP_K, tm), lambda i: (i, 0, 0)),
                   pl.BlockSpec((n_t, 2, e_n, LANES), lambda i: (0, 0, 0, 0)),
                   pl.BlockSpec((2, 1, nb_pad), lambda i: (0, 0, 0)),
                   pl.BlockSpec((2, e_n, LANES), lambda i: (0, 0, 0))],
        out_shape=[jax.ShapeDtypeStruct((TOP_K, t), F32), jax.ShapeDtypeStruct((n_t, TOP_K, tm), I32),
                   jax.ShapeDtypeStruct((n_t, 2, e_n, LANES), I32), jax.ShapeDtypeStruct((2, 1, nb_pad), I32),
                   jax.ShapeDtypeStruct((2, e_n, LANES), I32)],
        scratch_shapes=[pltpu.VMEM((n_t, e_n, LANES), I32), pltpu.VMEM((e_n, LANES), I32)],
        compiler_params=_cp("arbitrary"),
    )(x, router_w_t, router_b)


def _tile_rows(tm):
    worst = TOP_K * tm + N_EXPERTS * (SEG_ALIGN - 1)
    return -(-worst // ONEHOT_ROWS) * ONEHOT_ROWS


def _onehot_rows(chunk, lrow, values, tm):
    rid = chunk * ONEHOT_ROWS + lax.broadcasted_iota(I32, (ONEHOT_ROWS, tm), 0)
    acc = jnp.zeros((ONEHOT_ROWS, tm), F32)
    for k in range(TOP_K):
        acc = jnp.where(rid == lrow[k:k + 1, :], 1.0 if values is None else values[k:k + 1, :], acc)
    return acc.astype(BF16)


def _start_segments(gstart_ref, size_ref, tile, make_copy, first=0, last=N_EXPERTS, row0=0):
    def body(e, loff):
        n = size_ref[tile * N_EXPERTS + e]

        @pl.when(n > 0)
        def _():
            make_copy(pl.multiple_of(gstart_ref[tile * N_EXPERTS + e], SEG_ALIGN),
                      pl.multiple_of(loff, SEG_ALIGN), pl.multiple_of(n, SEG_ALIGN)).start()
        return loff + n
    return lax.fori_loop(first, last, body, row0)


def _segments_within(size_ref, tile, limit):
    def body(e, carry):
        count, rows = carry
        end = rows + size_ref[tile * N_EXPERTS + e]
        fits = jnp.logical_and(count == e, end <= limit)
        return count + jnp.where(fits, 1, 0), jnp.where(fits, end, rows)
    return lax.fori_loop(0, N_EXPERTS, body, (0, 0))


def _tile_total(size_ref, tile):
    return lax.fori_loop(0, N_EXPERTS, lambda e, s: s + size_ref[tile * N_EXPERTS + e], 0)


def _wait_rows(make_copy, rows):
    @pl.when(rows > 0)
    def _():
        make_copy(0, 0, pl.multiple_of(rows, SEG_ALIGN)).wait()


def _dispatch_kernel(gstart_ref, size_ref, x_ref, lrow_ref, xs_ref, stage, inflight, sem_a, sem_b, *, tm):
    i = pl.program_id(0)
    last = pl.num_programs(0) - 1
    n_chunks = stage.shape[0] // ONEHOT_ROWS
    split_chunk = n_chunks // 2
    split = split_chunk * ONEHOT_ROWS
    assert tm <= ONEHOT_ROWS

    def copy_on(sem):
        return lambda g, loff, n: pltpu.make_async_copy(stage.at[pl.ds(loff, n), :], xs_ref.at[pl.ds(g, n), :], sem)

    xb = x_ref[...].astype(BF16)
    lrow = lrow_ref[0]
    n_rows = _tile_total(size_ref, i)

    def chunks(lo, hi):
        for c in range(lo, hi):
            def one(c=c):
                stage[c * ONEHOT_ROWS:(c + 1) * ONEHOT_ROWS, :] = jnp.dot(
                    _onehot_rows(c, lrow, None, tm), xb, preferred_element_type=F32).astype(BF16)
            if c * ONEHOT_ROWS < TOP_K * tm:
                one()
            else:
                pl.when(c * ONEHOT_ROWS < n_rows)(one)

    @pl.when(i == 0)
    def _():
        inflight[0] = 0
        inflight[1] = 0

    _wait_rows(copy_on(sem_a), inflight[0])
    chunks(0, split_chunk - 1)
    _wait_rows(copy_on(sem_b), inflight[1])
    chunks(split_chunk - 1, split_chunk)
    experts_a, rows_a = _segments_within(size_ref, i, split)
    _start_segments(gstart_ref, size_ref, i, copy_on(sem_a), 0, experts_a)
    chunks(split_chunk, n_chunks)
    rows_b = _start_segments(gstart_ref, size_ref, i, copy_on(sem_b), experts_a, N_EXPERTS, rows_a) - rows_a
    inflight[0] = rows_a
    inflight[1] = rows_b

    @pl.when(i == last)
    def _():
        _wait_rows(copy_on(sem_a), rows_a)
        _wait_rows(copy_on(sem_b), rows_b)


def _dispatch(x, lrow, gstart, size, *, rows, tm):
    t, d = x.shape
    return pl.pallas_call(
        functools.partial(_dispatch_kernel, tm=tm),
        grid_spec=pltpu.PrefetchScalarGridSpec(
            num_scalar_prefetch=2,
            grid=(t // tm,),
            in_specs=[pl.BlockSpec((tm, d), lambda i, *_: (i, 0)),
                      pl.BlockSpec((1, TOP_K, tm), lambda i, *_: (i, 0, 0))],
            out_specs=pl.BlockSpec(memory_space=pl.ANY),
            scratch_shapes=[pltpu.VMEM((_tile_rows(tm), d), BF16), pltpu.SMEM((2,), I32)]
                           + [pltpu.SemaphoreType.DMA(())] * 2,
        ),
        out_shape=jax.ShapeDtypeStruct((rows, d), BF16),
        compiler_params=_cp("arbitrary"),
    )(gstart, size, x, lrow)


def _expert_block_kernel(blk_ref, nused_ref, xs_ref, wgu_ref, wdn_ref, ys_ref, wgu_b, wdn_b, *, n_blocks, bm):
    j = pl.program_id(0)
    valid = blk_ref[n_blocks + j]
    changed = jnp.logical_or(j == 0, blk_ref[j] != blk_ref[jnp.maximum(j - 1, 0)])

    @pl.when(jnp.logical_and(valid > 0, changed))
    def _():
        wgu_b[...] = wgu_ref[...].astype(BF16)
        wdn_b[...] = wdn_ref[...].astype(BF16)

    def swiglu(rows, masked):
        first = bm - rows
        x = xs_ref[first:bm, :]
        if masked:
            x = jnp.where(first + lax.broadcasted_iota(I32, x.shape, 0) >= bm - valid, x, jnp.zeros_like(x))
        f = wdn_b.shape[0]
        gu = jnp.dot(x, wgu_b[...], preferred_element_type=F32)
        hidden = (_silu(gu[:, :f]) * gu[:, f:]).astype(BF16)
        ys_ref[first:bm, :] = jnp.dot(hidden, wdn_b[...], preferred_element_type=F32).astype(BF16)

    @pl.when(valid == bm)
    def _():
        swiglu(bm, False)

    quarter = bm // 4
    for q in range(1, 5):
        @pl.when(jnp.logical_and(jnp.logical_and(valid > (q - 1) * quarter, valid <= q * quarter), valid < bm))
        def _():
            swiglu(q * quarter, True)


def _experts_blocked(xs, blk, nused, w_gu, w_dn, layer, *, n_blocks, bm):
    rows, d = xs.shape
    f2 = w_gu.shape[3]
    f = w_dn.shape[2]
    row_blk = lambda j, bl, nu: (jnp.minimum(j, nu[0] - 1), 0)
    return pl.pallas_call(
        functools.partial(_expert_block_kernel, n_blocks=n_blocks, bm=bm),
        grid_spec=pltpu.PrefetchScalarGridSpec(
            num_scalar_prefetch=2,
            grid=(n_blocks,),
            in_specs=[pl.BlockSpec((bm, d), row_blk),
                      pl.BlockSpec((None, None, d, f2), lambda j, bl, nu: (layer, bl[j], 0, 0)),
                      pl.BlockSpec((None, None, f, d), lambda j, bl, nu: (layer, bl[j], 0, 0))],
            out_specs=pl.BlockSpec((bm, d), row_blk),
            scratch_shapes=[pltpu.VMEM((d, f2), BF16), pltpu.VMEM((f, d), BF16)],
        ),
        out_shape=jax.ShapeDtypeStruct((rows, d), BF16),
        compiler_params=_cp("arbitrary"),
    )(blk, nused, xs, w_gu, w_dn)


def _combine_tile_kernel(gstart_ref, size_ref, x_ref, lrow_ref, w_ref, sdn_ref, lw_ref, lb_ref, sgu_hbm, ys_ref,
                         o_ref, ybuf, sgu, sems, wsem, *, tm, alpha):
    i = pl.program_id(0)
    last_tile = pl.num_programs(0) - 1
    n_chunks = ybuf.shape[1] // ONEHOT_ROWS
    sure_chunks = (TOP_K * tm) // ONEHOT_ROWS

    def contract_rows(w_rows, y_rows):
        return lax.dot_general(w_rows, y_rows, (((0,), (0,)), ((), ())), preferred_element_type=F32)

    def copy_to(slot):
        return lambda g, loff, n: pltpu.make_async_copy(ys_ref.at[pl.ds(g, n), :], ybuf.at[slot, pl.ds(loff, n), :],
                                                        sems.at[slot])

    def fetch(tile, slot):
        _start_segments(gstart_ref, size_ref, tile, copy_to(slot))

    n_rows = _tile_total(size_ref, i)

    @pl.when(i == 0)
    def _():
        ybuf[...] = jnp.zeros_like(ybuf)
        weights = pltpu.make_async_copy(sgu_hbm, sgu, wsem)
        weights.start()
        weights.wait()
        fetch(0, 0)

    for slot in range(2):
        @pl.when(i % 2 == slot)
        def _():
            @pl.when(i < last_tile)
            def _():
                fetch(i + 1, 1 - slot)

            x = x_ref[...]
            f = sdn_ref.shape[0]
            gu = jnp.dot(x.astype(BF16), sgu[...], preferred_element_type=F32)
            hidden = (_silu(gu[:, :f]) * gu[:, f:]).astype(BF16)
            acc = jnp.dot(hidden, sdn_ref[...], preferred_element_type=F32)
            _wait_rows(copy_to(slot), n_rows)
            lrow, w = lrow_ref[0], w_ref[...]
            for c in range(sure_chunks):
                rows = slice(c * ONEHOT_ROWS, (c + 1) * ONEHOT_ROWS)
                acc = acc + contract_rows(_onehot_rows(c, lrow, w, tm), ybuf[slot, rows, :])
            o_ref[...] = acc
            for c in range(sure_chunks, n_chunks):
                @pl.when(c * ONEHOT_ROWS < n_rows)
                def _():
                    rows = slice(c * ONEHOT_ROWS, (c + 1) * ONEHOT_ROWS)
                    o_ref[...] += contract_rows(_onehot_rows(c, lrow, w, tm), ybuf[slot, rows, :])
            o_ref[...] = _layer_norm(alpha * x + o_ref[...], lw_ref[...], lb_ref[...])


def _combine(x, ys, lrow, wts, gstart, size, s_gu, s_dn, ln_w, ln_b, *, alpha, tm):
    t, d = x.shape
    const = lambda shape: pl.BlockSpec(shape, lambda i, *_: (0, 0))
    return pl.pallas_call(
        functools.partial(_combine_tile_kernel, tm=tm, alpha=alpha),
        grid_spec=pltpu.PrefetchScalarGridSpec(
            num_scalar_prefetch=2,
            grid=(t // tm,),
            in_specs=[pl.BlockSpec((tm, d), lambda i, *_: (i, 0)),
                      pl.BlockSpec((1, TOP_K, tm), lambda i, *_: (i, 0, 0)),
                      pl.BlockSpec((TOP_K, tm), lambda i, *_: (0, i)),
                      const(s_dn.shape), const((1, d)), const((1, d)),
                      pl.BlockSpec(memory_space=pl.ANY), pl.BlockSpec(memory_space=pl.ANY)],
            out_specs=pl.BlockSpec((tm, d), lambda i, *_: (i, 0)),
            scratch_shapes=[pltpu.VMEM((2, _tile_rows(tm), d), BF16), pltpu.VMEM(s_gu.shape, BF16),
                            pltpu.SemaphoreType.DMA((2,)), pltpu.SemaphoreType.DMA(())],
        ),
        out_shape=jax.ShapeDtypeStruct((t, d), F32),
        compiler_params=_cp("arbitrary"),
    )(gstart, size, x, lrow, wts, s_dn, ln_w, ln_b, s_gu, ys)


def _mixer_sublayer(x, w_in_all, layer, gate_b, pool_w, pool_scale, conv_w, mlstm_norm_w, ret_norm_w, w_branch,
                    w_out, ln_w, ln_b, *, batch, seq, alpha):
    t, d = x.shape
    width = d // N_BRANCH
    heads = width // HEAD_DIM
    gate_off = 8 * width
    ret_off = gate_off
    g_off = ret_off + 4 * width
    if_off = g_off + N_BRANCH * d
    w_bf16 = _realign_cast(w_in_all, layer, lo_col=gate_off, hi_col=if_off, shift=2 * heads, tr=2048, tn=512)
    gate_bias = jnp.pad(gate_b, (0, LANES - 2 * heads)).reshape(1, LANES)

    proj = _matmul(x, w_bf16, tm=1024, tn=1024, ncols=g_off)
    gates = _matmul(x, w_bf16, tm=1024, tn=LANES, ncols=LANES, col0=if_off)
    y_pool, y_conv = _pool_conv(proj, pool_w.astype(BF16), pool_scale.reshape(1, width), conv_w,
                                seq=seq, width=width, tb=512)
    y_mlstm = _mlstm(proj, gates, gate_bias, mlstm_norm_w.reshape(1, width),
                     batch=batch, seq=seq, width=width, col0=4)
    y_ret = _retention(proj, ret_norm_w.reshape(1, width), batch=batch, seq=seq, width=width,
                       col0=ret_off // width)
    merged = _merge(x, (y_pool, y_conv, y_mlstm, y_ret), w_bf16, g_off, w_branch.astype(BF16),
                    tm=512, tn=512)
    return _proj_ln(merged, w_out.astype(BF16), x, ln_w, ln_b, alpha=alpha, tm=512)


def _xattn_sublayer(x, mem2d, wq, wk, wv, wo, ln_w, ln_b, *, seq, mem_len, alpha):
    w_kv = jnp.concatenate([wk, wv], axis=1).astype(BF16)
    kv = _matmul(mem2d, w_kv, tm=min(mem2d.shape[0], 1024), tn=512)
    return _xattn(x, kv, wq.astype(BF16), wo.astype(BF16), ln_w, ln_b,
                  alpha=alpha, seq=seq, mem_len=mem_len, tm=512)


def _moe_sublayer(x, router_w, router_b, w_gu, w_dn, layer, s_gu, s_dn, ln_w, ln_b, *, alpha):
    t, d = x.shape
    e_n, bm = N_EXPERTS, MOE_BM
    tm = ROUTE_TM
    n_blocks = -(-(t * TOP_K + (t // tm) * e_n * (SEG_ALIGN - 1)) // bm) + e_n
    wts, lrow, segtab, blk, seg = _route(x, router_w.T, router_b.reshape(e_n, 1), tm=tm, bm=bm, n_blocks=n_blocks)
    gstart, size = segtab[:, 0, :, 0].reshape(-1), segtab[:, 1, :, 0].reshape(-1)
    nused = seg[0, e_n - 1, 0] // bm

    xs = _dispatch(x, lrow, gstart, size, rows=n_blocks * bm, tm=tm)
    ys = _experts_blocked(xs, blk[:, 0, :n_blocks].reshape(-1), nused.reshape(1), w_gu, w_dn, layer,
                          n_blocks=n_blocks, bm=bm)
    return _combine(x, ys, lrow, wts, gstart, size, s_gu.astype(BF16), s_dn.astype(BF16), ln_w, ln_b,
                    alpha=alpha, tm=tm)


def kernel(x, mem, w_in, mlstm_gate_b, pool_w, pool_scale, conv_w, mlstm_norm_w, ret_norm_w, w_branch,
           w_mix_out, xa_wq, xa_wk, xa_wv, xa_wo, router_w, router_b, moe_w_gu, moe_w_dn, shared_w_gu,
           shared_w_dn, ln_w, ln_b):
    batch, seq, d = x.shape
    depth = w_in.shape[0]
    mem_len = mem.shape[1]
    alpha = (2 * depth) ** 0.25
    h = x.reshape(batch * seq, d)
    mem2d = mem.reshape(batch * mem_len, d)
    for l in range(depth):
        lw = ln_w[l].reshape(3, 1, d)
        lb = ln_b[l].reshape(3, 1, d)
        h = _mixer_sublayer(h, w_in, l, mlstm_gate_b[l], pool_w[l], pool_scale[l], conv_w[l], mlstm_norm_w[l],
                            ret_norm_w[l], w_branch[l], w_mix_out[l], lw[0], lb[0],
                            batch=batch, seq=seq, alpha=alpha)
        h = _xattn_sublayer(h, mem2d, xa_wq[l], xa_wk[l], xa_wv[l], xa_wo[l], lw[1], lb[1],
                            seq=seq, mem_len=mem_len, alpha=alpha)
        h = _moe_sublayer(h, router_w[l], router_b[l], moe_w_gu, moe_w_dn, l, shared_w_gu[l], shared_w_dn[l],
                          lw[2], lb[2], alpha=alpha)
    return h.reshape(batch, seq, d)
```

```python
import functools

import numpy as np
import jax
import jax.numpy as jnp
from jax import lax
from jax.experimental import pallas as pl
from jax.experimental.pallas import tpu as pltpu

F32 = jnp.float32
BF16 = jnp.bfloat16
I32 = jnp.int32

N_BRANCH = 4
HEAD_DIM = 128
POOL_WINDOWS = (2, 4, 8, 16)
CONV_WIDTH = 3
ROPE_BASE = 10000.0
XATTN_HEADS = 4
N_EXPERTS = 64
TOP_K = 8
N_GROUPS = 8
TOPK_GROUPS = 4
ROUTE_SCALE = 2.5
LN_EPS = 1e-5

LANES = 128
V7X_VMEM_BYTES = 64 * 1024 * 1024
VMEM_LIMIT = V7X_VMEM_BYTES - 8 * 1024 * 1024

SCAN_CHUNK = 256
SCAN_BLOCK = 512
SCAN_BATCH = 4
HALO = 16
MOE_BM = 1024
ROUTE_TM = 256
SEG_ALIGN = 16
ONEHOT_ROWS = 512


def _cp(*sem):
    return pltpu.CompilerParams(dimension_semantics=sem, vmem_limit_bytes=VMEM_LIMIT)


def _sigmoid(x):
    return 1.0 / (1.0 + jnp.exp(-x))


def _silu(x):
    return x * _sigmoid(x)


def _log_sigmoid(x):
    return jnp.minimum(x, 0.0) - jnp.log(1.0 + jnp.exp(-jnp.abs(x)))


def _layer_norm(z, w, b):
    mu = jnp.mean(z, axis=-1, keepdims=True)
    d = z - mu
    var = jnp.mean(d * d, axis=-1, keepdims=True)
    return d * lax.rsqrt(var + LN_EPS) * w + b


def _head_norm(h, w):
    mu = jnp.mean(h, axis=-1, keepdims=True)
    d = h - mu
    var = jnp.mean(d * d, axis=-1, keepdims=True)
    return d * lax.rsqrt(var + LN_EPS) * w


def _mm_kernel(x_ref, w_ref, o_ref, xb_ref):
    @pl.when(pl.program_id(1) == 0)
    def _():
        xb_ref[...] = x_ref[...].astype(BF16)

    o_ref[...] = jnp.dot(xb_ref[...], w_ref[...], preferred_element_type=F32)


def _realign_cast_kernel(a_ref, b_ref, o_ref, *, shift, lo, hi):
    j = pl.program_id(1)
    tn = o_ref.shape[1]
    shifted = jnp.logical_and(j >= lo, j < hi)

    @pl.when(shifted)
    def _():
        both = jnp.concatenate([a_ref[...], b_ref[...]], axis=1)
        o_ref[...] = both[:, shift:shift + tn].astype(BF16)

    @pl.when(jnp.logical_not(shifted))
    def _():
        o_ref[...] = a_ref[...].astype(BF16)


def _realign_cast(w_all, layer, *, lo_col, hi_col, shift, tr, tn):
    _, rows, _ = w_all.shape
    lo, hi = lo_col // tn, hi_col // tn
    src = lambda j: jnp.where(j == hi, lo, j)
    return pl.pallas_call(
        functools.partial(_realign_cast_kernel, shift=shift, lo=lo, hi=hi),
        grid=(rows // tr, hi + 1),
        in_specs=[pl.BlockSpec((None, tr, tn), lambda i, j: (layer, i, src(j))),
                  pl.BlockSpec((None, tr, LANES), lambda i, j: (layer, i, (src(j) + 1) * (tn // LANES)))],
        out_specs=pl.BlockSpec((tr, tn), lambda i, j: (i, j)),
        out_shape=jax.ShapeDtypeStruct((rows, hi_col + tn), BF16),
        compiler_params=_cp("parallel", "parallel"),
    )(w_all, w_all)


def _matmul(x, w, *, tm, tn, ncols=None, col0=0):
    t, k = x.shape
    n = w.shape[1] if ncols is None else ncols
    return pl.pallas_call(
        _mm_kernel,
        grid=(t // tm, n // tn),
        in_specs=[pl.BlockSpec((tm, k), lambda i, j: (i, 0)),
                  pl.BlockSpec((k, tn), lambda i, j: (0, col0 // tn + j))],
        out_specs=pl.BlockSpec((tm, tn), lambda i, j: (i, j)),
        out_shape=jax.ShapeDtypeStruct((t, n), F32),
        scratch_shapes=[pltpu.VMEM((tm, k), BF16)],
        compiler_params=_cp("parallel", "arbitrary"),
    )(x, w)


def _poolconv_kernel(u_ref, uh_ref, h_ref, hh_ref, b_ref, c_ref, ch_ref, pw_ref, ps_ref, cw_ref,
                     yp_ref, yc_ref, ubuf, zbuf, *, tb, n_tblk):
    first = (pl.program_id(0) % n_tblk) == 0
    ubuf[0:HALO, :] = jnp.where(first, 0.0, uh_ref[...])
    ubuf[HALO:HALO + tb, :] = u_ref[...]
    zbuf[0:HALO, :] = jnp.where(first, 0.0, ch_ref[...] * hh_ref[...])
    zbuf[HALO:HALO + tb, :] = c_ref[...] * h_ref[...]

    t_pos = (pl.program_id(0) % n_tblk) * tb + lax.broadcasted_iota(I32, (tb, LANES), 0)
    gw = u_ref.shape[1] // len(POOL_WINDOWS)
    for grp, win in enumerate(POOL_WINDOWS):
        lanes = slice(grp * gw, (grp + 1) * gw)
        cur = ubuf[HALO:HALO + tb, lanes]
        acc = cur
        for lag in range(1, win):
            acc = acc + ubuf[HALO - lag:HALO - lag + tb, lanes]
        count = jnp.minimum(t_pos + 1, win).astype(F32)
        mixed = acc / count - cur
        y = jnp.dot(mixed.astype(BF16), pw_ref[grp], preferred_element_type=F32)
        yp_ref[:, lanes] = (y * ps_ref[:, lanes]).astype(BF16)

    conv = cw_ref[0:1, :] * zbuf[HALO:HALO + tb, :]
    for lag in range(1, CONV_WIDTH):
        conv = conv + cw_ref[lag:lag + 1, :] * zbuf[HALO - lag:HALO - lag + tb, :]
    yc_ref[...] = (b_ref[...] * conv).astype(BF16)


def _pool_conv(proj, pool_w, pool_scale, conv_w, *, seq, width, tb):
    t = proj.shape[0]
    n_tblk = seq // tb
    ratio = tb // HALO

    def cur(col):
        return pl.BlockSpec((tb, width), lambda g: (g, col))

    def halo(col):
        return pl.BlockSpec((HALO, width), lambda g: (jnp.maximum(g * ratio - 1, 0), col))

    full = lambda shape: pl.BlockSpec(shape, lambda g: (0,) * len(shape))
    return pl.pallas_call(
        functools.partial(_poolconv_kernel, tb=tb, n_tblk=n_tblk),
        grid=(t // tb,),
        in_specs=[cur(0), halo(0), cur(1), halo(1), cur(2), cur(3), halo(3),
                  full(pool_w.shape), full(pool_scale.shape), full(conv_w.shape)],
        out_specs=[pl.BlockSpec((tb, width), lambda g: (g, 0))] * 2,
        out_shape=[jax.ShapeDtypeStruct((t, width), BF16)] * 2,
        scratch_shapes=[pltpu.VMEM((HALO + tb, width), F32)] * 2,
        compiler_params=_cp("parallel"),
    )(proj, proj, proj, proj, proj, proj, proj, pool_w, pool_scale, conv_w)


def _cum_rows(x, op, fill):
    n = x.shape[0]
    row = lax.broadcasted_iota(I32, x.shape, 0)
    shift = 1
    while shift < n:
        x = op(x, jnp.where(row >= shift, pltpu.roll(x, shift, axis=0), fill))
        shift *= 2
    return x


def _mlstm_kernel(q_ref, k_ref, v_ref, o_ref, g_ref, gb_ref, nw_ref, y_ref, c_scr, n_scr, m_scr,
                  *, chunk, heads):
    @pl.when(pl.program_id(1) == 0)
    def _():
        c_scr[...] = jnp.zeros_like(c_scr)
        n_scr[...] = jnp.zeros_like(n_scr)
        m_scr[...] = jnp.zeros_like(m_scr)

    nb, tb = q_ref.shape[0], q_ref.shape[1]
    hd = HEAD_DIM
    scale = hd ** -0.5
    tri = (lax.broadcasted_iota(I32, (chunk, chunk), 0) >= lax.broadcasted_iota(I32, (chunk, chunk), 1))
    for c, bb in [(c, bb) for c in range(tb // chunk) for bb in range(nb)]:
        rows = slice(c * chunk, (c + 1) * chunk)
        gates = g_ref[bb, rows, :] + gb_ref[...]
        lf = _log_sigmoid(pltpu.roll(gates, LANES - heads, axis=1))
        cumf = _cum_rows(lf, jnp.add, 0.0)
        a = gates - cumf
        m_prev = m_scr[bb]
        mu = jnp.maximum(_cum_rows(a, jnp.maximum, -jnp.inf), m_prev)
        mu_last = mu[chunk - 1:chunk, :]
        a_t = a.T
        for h in range(heads):
            cols = slice(h * hd, (h + 1) * hd)
            q = q_ref[bb, rows, cols]
            k = k_ref[bb, rows, cols] * scale
            v = v_ref[bb, rows, cols]
            qb, kb, vb = q.astype(BF16), k.astype(BF16), v.astype(BF16)
            mu_col = mu[:, h:h + 1]
            a_col = a[:, h:h + 1]
            m_prev_h = m_prev[:, h:h + 1]
            mu_last_h = mu_last[:, h:h + 1]
            dmat = jnp.exp(jnp.where(tri, a_t[h:h + 1, :] - mu_col, -jnp.inf))
            s = lax.dot_general(qb, kb, (((1,), (1,)), ((), ())), preferred_element_type=F32)
            p = dmat * s
            inter = jnp.exp(m_prev_h - mu_col)
            state = bb * heads + h
            c_h = c_scr[state]
            n_h = n_scr[state]
            num = inter * jnp.dot(qb, c_h.astype(BF16), preferred_element_type=F32) \
                + jnp.dot(p.astype(BF16), vb, preferred_element_type=F32)
            den = inter * jnp.sum(q * n_h, axis=-1, keepdims=True) + jnp.sum(p, axis=-1, keepdims=True)
            floor = jnp.exp(-(cumf[:, h:h + 1] + mu_col))
            h_out = num / jnp.maximum(jnp.abs(den), floor)

            wg = jnp.exp(a_col - mu_last_h)
            decay = jnp.exp(m_prev_h - mu_last_h)
            kw = k * wg
            c_scr[state] = decay * c_h + lax.dot_general(kw.astype(BF16), vb, (((0,), (0,)), ((), ())),
                                                         preferred_element_type=F32)
            n_scr[state] = decay * n_h + jnp.sum(kw, axis=0, keepdims=True)

            gated = _sigmoid(o_ref[bb, rows, cols]) * h_out
            y_ref[bb, rows, cols] = _head_norm(gated, nw_ref[:, cols]).astype(BF16)
        m_scr[bb] = cumf[chunk - 1:chunk, :] + mu_last


def _mlstm(proj, gates, gate_b, norm_w, *, batch, seq, width, col0):
    t = proj.shape[0]
    heads = width // HEAD_DIM
    nb = SCAN_BATCH if batch % SCAN_BATCH == 0 else 1
    proj3 = proj.reshape(batch, seq, proj.shape[1])
    gates3 = gates.reshape(batch, seq, LANES)
    blk = lambda col: pl.BlockSpec((nb, SCAN_BLOCK, width), lambda b, i: (b, i, col))
    y = pl.pallas_call(
        functools.partial(_mlstm_kernel, chunk=SCAN_CHUNK, heads=heads),
        grid=(batch // nb, seq // SCAN_BLOCK),
        in_specs=[blk(col0), blk(col0 + 1), blk(col0 + 2), blk(col0 + 3),
                  pl.BlockSpec((nb, SCAN_BLOCK, LANES), lambda b, i: (b, i, 0)),
                  pl.BlockSpec((1, LANES), lambda b, i: (0, 0)),
                  pl.BlockSpec((1, width), lambda b, i: (0, 0))],
        out_specs=pl.BlockSpec((nb, SCAN_BLOCK, width), lambda b, i: (b, i, 0)),
        out_shape=jax.ShapeDtypeStruct((batch, seq, width), BF16),
        scratch_shapes=[pltpu.VMEM((nb * heads, HEAD_DIM, HEAD_DIM), F32),
                        pltpu.VMEM((nb * heads, 1, HEAD_DIM), F32),
                        pltpu.VMEM((nb, 1, LANES), F32)],
        compiler_params=_cp("parallel", "arbitrary"),
    )(proj3, proj3, proj3, proj3, gates3, gate_b, norm_w)
    return y.reshape(t, width)


def _ret_kernel(q_ref, k_ref, v_ref, g_ref, cos_ref, sin_ref, intra_ref, cross_ref, zeta_ref, nw_ref,
                y_ref, r_scr, *, chunk, heads, chunk_decay):
    @pl.when(pl.program_id(1) == 0)
    def _():
        r_scr[...] = jnp.zeros_like(r_scr)

    tb = q_ref.shape[0]
    hd = HEAD_DIM
    scale = hd ** -0.5
    for c in range(tb // chunk):
        rows = slice(c * chunk, (c + 1) * chunk)
        cos = cos_ref[rows, :]
        sin = sin_ref[rows, :]
        for h in range(heads):
            cols = slice(h * hd, (h + 1) * hd)
            q = q_ref[rows, cols]
            k = k_ref[rows, cols]
            q = q * cos + pltpu.roll(q, hd // 2, axis=1) * sin
            k = (k * cos + pltpu.roll(k, hd // 2, axis=1) * sin) * scale
            vb = v_ref[rows, cols].astype(BF16)
            qb = q.astype(BF16)
            s = lax.dot_general(qb, k.astype(BF16), (((1,), (1,)), ((), ())), preferred_element_type=F32)
            inner = jnp.dot((s * intra_ref[h]).astype(BF16), vb, preferred_element_type=F32)
            r_h = r_scr[h]
            crs = jnp.dot(qb, r_h.astype(BF16), preferred_element_type=F32) * cross_ref[h]
            r_scr[h] = chunk_decay[h] * r_h + lax.dot_general(
                (k * zeta_ref[h]).astype(BF16), vb, (((0,), (0,)), ((), ())), preferred_element_type=F32)
            y_ref[rows, cols] = (_silu(g_ref[rows, cols]) * _head_norm(inner + crs, nw_ref[:, cols])).astype(BF16)


def _retention(proj, norm_w, *, batch, seq, width, col0):
    t = proj.shape[0]
    heads = width // HEAD_DIM
    chunk = SCAN_CHUNK
    n_tblk = seq // SCAN_BLOCK
    half = HEAD_DIM // 2
    inv = np.float32(ROPE_BASE) ** (-np.arange(half, dtype=np.float32) / np.float32(half))
    ang = (np.arange(seq, dtype=np.float32)[:, None] * inv[None, :]).astype(np.float64)
    cos_t = jnp.asarray(np.concatenate([np.cos(ang), np.cos(ang)], axis=-1), F32)
    sin_t = jnp.asarray(np.concatenate([-np.sin(ang), np.sin(ang)], axis=-1), F32)
    log_g = jnp.log(1.0 - 2.0 ** (-5.0 - jnp.arange(heads, dtype=F32)))
    tt = jnp.arange(chunk, dtype=F32)
    lag = tt[:, None] - tt[None, :]
    intra = jnp.where(lag >= 0, jnp.exp(jnp.maximum(lag, 0.0)[None] * log_g[:, None, None]), 0.0)
    cross = jnp.broadcast_to(jnp.exp((tt + 1.0)[None, :] * log_g[:, None])[:, :, None], (heads, chunk, HEAD_DIM))
    zeta = jnp.broadcast_to(jnp.exp((chunk - 1.0 - tt)[None, :] * log_g[:, None])[:, :, None],
                            (heads, chunk, HEAD_DIM))
    chunk_decay = tuple(float((1.0 - 2.0 ** (-5.0 - h)) ** chunk) for h in range(heads))

    blk = lambda col: pl.BlockSpec((SCAN_BLOCK, width), lambda b, i: (b * n_tblk + i, col))
    pos = pl.BlockSpec((SCAN_BLOCK, HEAD_DIM), lambda b, i: (i, 0))
    full3 = lambda a: pl.BlockSpec(a.shape, lambda b, i: (0, 0, 0))
    return pl.pallas_call(
        functools.partial(_ret_kernel, chunk=chunk, heads=heads, chunk_decay=chunk_decay),
        grid=(batch, n_tblk),
        in_specs=[blk(col0), blk(col0 + 1), blk(col0 + 2), blk(col0 + 3), pos, pos,
                  full3(intra), full3(cross), full3(zeta),
                  pl.BlockSpec((1, width), lambda b, i: (0, 0))],
        out_specs=pl.BlockSpec((SCAN_BLOCK, width), lambda b, i: (b * n_tblk + i, 0)),
        out_shape=jax.ShapeDtypeStruct((t, width), BF16),
        scratch_shapes=[pltpu.VMEM((heads, HEAD_DIM, HEAD_DIM), F32)],
        compiler_params=_cp("parallel", "arbitrary"),
    )(proj, proj, proj, proj, cos_t, sin_t, intra, cross, zeta, norm_w)


def _merge_kernel(x_ref, *refs):
    y_refs, wg_refs = refs[:N_BRANCH], refs[N_BRANCH:2 * N_BRANCH]
    wb_ref, o_ref, xb_ref = refs[2 * N_BRANCH:]

    @pl.when(pl.program_id(1) == 0)
    def _():
        xb_ref[...] = x_ref[...].astype(BF16)

    xb = xb_ref[...]
    acc = None
    for n in range(N_BRANCH):
        gate = _sigmoid(jnp.dot(xb, wg_refs[n][...], preferred_element_type=F32))
        term = gate * jnp.dot(y_refs[n][...], wb_ref[n], preferred_element_type=F32)
        acc = term if acc is None else acc + term
    o_ref[...] = acc.astype(BF16)


def _merge(x, branches, w_gate, gate_col0, w_branch, *, tm, tn):
    t, d = x.shape
    width = branches[0].shape[1]
    ybs = pl.BlockSpec((tm, width), lambda i, j: (i, 0))

    def gate_spec(n):
        return pl.BlockSpec((d, tn), lambda i, j: (0, (gate_col0 + n * d) // tn + j))

    return pl.pallas_call(
        _merge_kernel,
        grid=(t // tm, d // tn),
        in_specs=[pl.BlockSpec((tm, d), lambda i, j: (i, 0))] + [ybs] * N_BRANCH
                 + [gate_spec(n) for n in range(N_BRANCH)]
                 + [pl.BlockSpec((N_BRANCH, width, tn), lambda i, j: (0, 0, j))],
        out_specs=pl.BlockSpec((tm, tn), lambda i, j: (i, j)),
        out_shape=jax.ShapeDtypeStruct((t, d), BF16),
        scratch_shapes=[pltpu.VMEM((tm, d), BF16)],
        compiler_params=_cp("parallel", "arbitrary"),
    )(x, *branches, *([w_gate] * N_BRANCH), w_branch)


def _proj_ln_kernel(a_ref, w_ref, r_ref, lw_ref, lb_ref, o_ref, *, alpha):
    y = jnp.dot(a_ref[...], w_ref[...], preferred_element_type=F32)
    o_ref[...] = _layer_norm(alpha * r_ref[...] + y, lw_ref[...], lb_ref[...])


def _proj_ln(a, w, resid, ln_w, ln_b, *, alpha, tm):
    t, k = a.shape
    d = w.shape[1]
    row = lambda n: pl.BlockSpec((tm, n), lambda i: (i, 0))
    const = lambda shape: pl.BlockSpec(shape, lambda i: (0, 0))
    return pl.pallas_call(
        functools.partial(_proj_ln_kernel, alpha=alpha),
        grid=(t // tm,),
        in_specs=[row(k), const((k, d)), row(d), const((1, d)), const((1, d))],
        out_specs=row(d),
        out_shape=jax.ShapeDtypeStruct((t, d), F32),
        compiler_params=_cp("parallel"),
    )(a, w, resid, ln_w, ln_b)


def _xattn_kernel(x_ref, wq_ref, kv_ref, wo_ref, lw_ref, lb_ref, o_ref, *, alpha, heads):
    x = x_ref[...]
    hd = HEAD_DIM
    inner = heads * hd
    q = jnp.dot(x.astype(BF16), wq_ref[...], preferred_element_type=F32)
    outs = []
    for h in range(heads):
        qh = q[:, h * hd:(h + 1) * hd].astype(BF16)
        kh = kv_ref[:, h * hd:(h + 1) * hd].astype(BF16)
        vh = kv_ref[:, inner + h * hd:inner + (h + 1) * hd].astype(BF16)
        s = lax.dot_general(qh, kh, (((1,), (1,)), ((), ())), preferred_element_type=F32) * hd ** -0.5
        s = s - jnp.max(s, axis=-1, keepdims=True)
        e = jnp.exp(s)
        p = e / jnp.sum(e, axis=-1, keepdims=True)
        outs.append(jnp.dot(p.astype(BF16), vh, preferred_element_type=F32).astype(BF16))
    o = jnp.concatenate(outs, axis=-1)
    y = jnp.dot(o, wo_ref[...], preferred_element_type=F32)
    o_ref[...] = _layer_norm(alpha * x + y, lw_ref[...], lb_ref[...])


def _xattn(x, kv, wq, wo, ln_w, ln_b, *, alpha, seq, mem_len, tm):
    t, d = x.shape
    inner = wq.shape[1]
    n_tblk = seq // tm
    const = lambda shape: pl.BlockSpec(shape, lambda i: (0, 0))
    return pl.pallas_call(
        functools.partial(_xattn_kernel, alpha=alpha, heads=XATTN_HEADS),
        grid=(t // tm,),
        in_specs=[pl.BlockSpec((tm, d), lambda i: (i, 0)), const((d, inner)),
                  pl.BlockSpec((mem_len, 2 * inner), lambda i: (i // n_tblk, 0)),
                  const((inner, d)), const((1, d)), const((1, d))],
        out_specs=pl.BlockSpec((tm, d), lambda i: (i, 0)),
        out_shape=jax.ShapeDtypeStruct((t, d), F32),
        compiler_params=_cp("parallel"),
    )(x, wq, kv, wo, ln_w, ln_b)


def _round_up_pow2(x, m):
    shift = m.bit_length() - 1
    return jnp.left_shift(jnp.right_shift(x + (m - 1), shift), shift)


def _route_kernel(x_ref, wr_ref, rb_ref, w_ref, lrow_ref, segtab_ref, blk_ref, seg_ref, size_all, tot, *, tm, bm):
    step = pl.program_id(0)

    @pl.when(step == 0)
    def _():
        tot[...] = jnp.zeros_like(tot)

    e_n, g_n = N_EXPERTS, N_GROUPS
    per = e_n // g_n
    def split(a):
        hi = a.astype(BF16)
        return hi, (a - hi.astype(F32)).astype(BF16)

    def nt(a, b):
        return lax.dot_general(a, b, (((1,), (1,)), ((), ())), preferred_element_type=F32)

    (w_hi, w_lo), (x_hi, x_lo) = split(wr_ref[...]), split(x_ref[...])
    logits = nt(w_hi, x_hi) + (nt(w_hi, x_lo) + nt(w_lo, x_hi))
    scores = _sigmoid(logits)
    biased = scores + rb_ref[...]
    b3 = biased.reshape(g_n, per, tm)
    member = lax.broadcasted_iota(I32, (g_n, per, tm), 1)
    top1 = jnp.max(b3, axis=1, keepdims=True)
    first = jnp.min(jnp.where(b3 == top1, member, per), axis=1, keepdims=True)
    top2 = jnp.max(jnp.where(member == first, -jnp.inf, b3), axis=1, keepdims=True)
    gs = top1 + top2
    gid = lax.broadcasted_iota(I32, (g_n, 1, tm), 0)
    rank = jnp.zeros((g_n, 1, tm), I32)
    for other in range(g_n):
        o = gs[other:other + 1]
        ahead = jnp.logical_or(o > gs, jnp.logical_and(o == gs, other < gid))
        rank = rank + jnp.where(ahead, 1, 0)
    cur = jnp.where(rank < TOPK_GROUPS, b3, -jnp.inf).reshape(e_n, tm)

    eid = lax.broadcasted_iota(I32, (e_n, tm), 0)
    picks, vals = [], []
    sel = jnp.zeros((e_n, tm), F32)
    for k in range(TOP_K):
        mx = jnp.max(cur, axis=0, keepdims=True)
        ik = jnp.min(jnp.where(cur == mx, eid, e_n), axis=0, keepdims=True)
        hit = eid == ik
        vals.append(jnp.sum(jnp.where(hit, scores, 0.0), axis=0, keepdims=True))
        cur = jnp.where(hit, -jnp.inf, cur)
        sel = jnp.where(hit, 1.0, sel)
        picks.append(ik)
    total = vals[0]
    for v in vals[1:]:
        total = total + v

    tri = jnp.where(lax.broadcasted_iota(I32, (tm, tm), 0) <= lax.broadcasted_iota(I32, (tm, tm), 1), 1.0, 0.0)
    incl = jnp.dot(sel.astype(BF16), tri.astype(BF16), preferred_element_type=F32)
    size = _round_up_pow2(jnp.broadcast_to(incl[:, tm - 1:tm], (e_n, LANES)).astype(I32), SEG_ALIGN)
    loff = _cum_rows(size, jnp.add, 0) - size
    base = loff[:, 0:1].astype(F32) + incl - 1.0
    for k in range(TOP_K):
        w_ref[k:k + 1, :] = vals[k] / total * ROUTE_SCALE
        lrow_ref[0, k:k + 1, :] = jnp.sum(jnp.where(eid == picks[k], base, 0.0),
                                          axis=0, keepdims=True).astype(I32)
    size_all[step] = size
    tot[...] = tot[...] + size

    @pl.when(step == pl.num_programs(0) - 1)
    def _():
        rows = tot[...]
        pcnt = _round_up_pow2(rows, bm)
        pend = _cum_rows(pcnt, jnp.add, 0)

        def tile_seg(i, run):
            segtab_ref[i, 0] = run
            segtab_ref[i, 1] = size_all[i]
            return run + size_all[i]

        data_start = pend - rows
        lax.fori_loop(0, pl.num_programs(0), tile_seg, data_start)

        nb = blk_ref.shape[2]
        row0 = lax.broadcasted_iota(I32, (e_n, nb), 1) * bm
        total_rows = pend[e_n - 1:e_n, 0:1]
        owner = jnp.sum(jnp.where(pend[:, 0:1] <= row0, 1, 0), axis=0, keepdims=True)
        last_owner = jnp.sum(jnp.where(pend[:, 0:1] < total_rows, 1, 0), axis=0, keepdims=True)
        blk_ref[0] = jnp.where(row0[0:1, :] < total_rows, jnp.minimum(owner, e_n - 1), last_owner)
        inside = jnp.logical_and(pend[:, 0:1] - pcnt[:, 0:1] <= row0, row0 < pend[:, 0:1])
        blk_ref[1] = jnp.sum(jnp.where(inside, jnp.clip(row0 + bm - data_start[:, 0:1], 0, bm), 0),
                             axis=0, keepdims=True)
        seg_ref[0] = pend
        seg_ref[1] = pcnt


def _route(x, router_w_t, router_b, *, tm, bm, n_blocks):
    t, d = x.shape
    e_n = N_EXPERTS
    n_t = t // tm
    assert bm & (bm - 1) == 0
    nb_pad = -(-n_blocks // LANES) * LANES
    return pl.pallas_call(
        functools.partial(_route_kernel, tm=tm, bm=bm),
        grid=(n_t,),
        in_specs=[pl.BlockSpec((tm, d), lambda i: (i, 0)),
                  pl.BlockSpec((e_n, d), lambda i: (0, 0)),
                  pl.BlockSpec((e_n, 1), lambda i: (0, 0))],
        out_specs=[pl.BlockSpec((TOP_K, tm), lambda i: (0, i)),
                   pl.BlockSpec((1, TOP_K, tm), lambda i: (i, 0, 0)),
                   pl.BlockSpec((n_t, 2, e_n, LANES), lambda i: (0, 0, 0, 0)),
                   pl.BlockSpec((2, 1, nb_pad), lambda i: (0, 0, 0)),
                   pl.BlockSpec((2, e_n, LANES), lambda i: (0, 0, 0))],
        out_shape=[jax.ShapeDtypeStruct((TOP_K, t), F32), jax.ShapeDtypeStruct((n_t, TOP_K, tm), I32),
                   jax.ShapeDtypeStruct((n_t, 2, e_n, LANES), I32), jax.ShapeDtypeStruct((2, 1, nb_pad), I32),
                   jax.ShapeDtypeStruct((2, e_n, LANES), I32)],
        scratch_shapes=[pltpu.VMEM((n_t, e_n, LANES), I32), pltpu.VMEM((e_n, LANES), I32)],
        compiler_params=_cp("arbitrary"),
    )(x, router_w_t, router_b)


def _tile_rows(tm):
    worst = TOP_K * tm + N_EXPERTS * (SEG_ALIGN - 1)
    return -(-worst // ONEHOT_ROWS) * ONEHOT_ROWS


def _onehot_rows(chunk, lrow, values, tm):
    rid = chunk * ONEHOT_ROWS + lax.broadcasted_iota(I32, (ONEHOT_ROWS, tm), 0)
    acc = jnp.zeros((ONEHOT_ROWS, tm), F32)
    for k in range(TOP_K):
        acc = jnp.where(rid == lrow[k:k + 1, :], 1.0 if values is None else values[k:k + 1, :], acc)
    return acc.astype(BF16)


def _start_segments(gstart_ref, size_ref, tile, make_copy, first=0, last=N_EXPERTS, row0=0):
    def body(e, loff):
        n = size_ref[tile * N_EXPERTS + e]

        @pl.when(n > 0)
        def _():
            make_copy(pl.multiple_of(gstart_ref[tile * N_EXPERTS + e], SEG_ALIGN),
                      pl.multiple_of(loff, SEG_ALIGN), pl.multiple_of(n, SEG_ALIGN)).start()
        return loff + n
    return lax.fori_loop(first, last, body, row0)


def _segments_within(size_ref, tile, limit):
    def body(e, carry):
        count, rows = carry
        end = rows + size_ref[tile * N_EXPERTS + e]
        fits = jnp.logical_and(count == e, end <= limit)
        return count + jnp.where(fits, 1, 0), jnp.where(fits, end, rows)
    return lax.fori_loop(0, N_EXPERTS, body, (0, 0))


def _tile_total(size_ref, tile):
    return lax.fori_loop(0, N_EXPERTS, lambda e, s: s + size_ref[tile * N_EXPERTS + e], 0)


def _wait_rows(make_copy, rows):
    @pl.when(rows > 0)
    def _():
        make_copy(0, 0, pl.multiple_of(rows, SEG_ALIGN)).wait()


def _dispatch_kernel(gstart_ref, size_ref, x_ref, lrow_ref, xs_ref, stage, inflight, sem_a, sem_b, *, tm):
    i = pl.program_id(0)
    last = pl.num_programs(0) - 1
    n_chunks = stage.shape[0] // ONEHOT_ROWS
    split_chunk = n_chunks // 2
    split = split_chunk * ONEHOT_ROWS
    assert tm <= ONEHOT_ROWS

    def copy_on(sem):
        return lambda g, loff, n: pltpu.make_async_copy(stage.at[pl.ds(loff, n), :], xs_ref.at[pl.ds(g, n), :], sem)

    xb = x_ref[...].astype(BF16)
    lrow = lrow_ref[0]
    n_rows = _tile_total(size_ref, i)

    def chunks(lo, hi):
        for c in range(lo, hi):
            def one(c=c):
                stage[c * ONEHOT_ROWS:(c + 1) * ONEHOT_ROWS, :] = jnp.dot(
                    _onehot_rows(c, lrow, None, tm), xb, preferred_element_type=F32).astype(BF16)
            if c * ONEHOT_ROWS < TOP_K * tm:
                one()
            else:
                pl.when(c * ONEHOT_ROWS < n_rows)(one)

    @pl.when(i == 0)
    def _():
        inflight[0] = 0
        inflight[1] = 0

    _wait_rows(copy_on(sem_a), inflight[0])
    chunks(0, split_chunk - 1)
    _wait_rows(copy_on(sem_b), inflight[1])
    chunks(split_chunk - 1, split_chunk)
    experts_a, rows_a = _segments_within(size_ref, i, split)
    _start_segments(gstart_ref, size_ref, i, copy_on(sem_a), 0, experts_a)
    chunks(split_chunk, n_chunks)
    rows_b = _start_segments(gstart_ref, size_ref, i, copy_on(sem_b), experts_a, N_EXPERTS, rows_a) - rows_a
    inflight[0] = rows_a
    inflight[1] = rows_b

    @pl.when(i == last)
    def _():
        _wait_rows(copy_on(sem_a), rows_a)
        _wait_rows(copy_on(sem_b), rows_b)


def _dispatch(x, lrow, gstart, size, *, rows, tm):
    t, d = x.shape
    return pl.pallas_call(
        functools.partial(_dispatch_kernel, tm=tm),
        grid_spec=pltpu.PrefetchScalarGridSpec(
            num_scalar_prefetch=2,
            grid=(t // tm,),
            in_specs=[pl.BlockSpec((tm, d), lambda i, *_: (i, 0)),
                      pl.BlockSpec((1, TOP_K, tm), lambda i, *_: (i, 0, 0))],
            out_specs=pl.BlockSpec(memory_space=pl.ANY),
            scratch_shapes=[pltpu.VMEM((_tile_rows(tm), d), BF16), pltpu.SMEM((2,), I32)]
                           + [pltpu.SemaphoreType.DMA(())] * 2,
        ),
        out_shape=jax.ShapeDtypeStruct((rows, d), BF16),
        compiler_params=_cp("arbitrary"),
    )(gstart, size, x, lrow)


def _expert_block_kernel(blk_ref, nused_ref, xs_ref, wgu_ref, wdn_ref, ys_ref, wgu_b, wdn_b, *, n_blocks, bm):
    j = pl.program_id(0)
    valid = blk_ref[n_blocks + j]
    changed = jnp.logical_or(j == 0, blk_ref[j] != blk_ref[jnp.maximum(j - 1, 0)])

    @pl.when(jnp.logical_and(valid > 0, changed))
    def _():
        wgu_b[...] = wgu_ref[...].astype(BF16)
        wdn_b[...] = wdn_ref[...].astype(BF16)

    def swiglu(rows, masked):
        first = bm - rows
        x = xs_ref[first:bm, :]
        if masked:
            x = jnp.where(first + lax.broadcasted_iota(I32, x.shape, 0) >= bm - valid, x, jnp.zeros_like(x))
        f = wdn_b.shape[0]
        gu = jnp.dot(x, wgu_b[...], preferred_element_type=F32)
        hidden = (_silu(gu[:, :f]) * gu[:, f:]).astype(BF16)
        ys_ref[first:bm, :] = jnp.dot(hidden, wdn_b[...], preferred_element_type=F32).astype(BF16)

    @pl.when(valid == bm)
    def _():
        swiglu(bm, False)

    quarter = bm // 4
    for q in range(1, 5):
        @pl.when(jnp.logical_and(jnp.logical_and(valid > (q - 1) * quarter, valid <= q * quarter), valid < bm))
        def _():
            swiglu(q * quarter, True)


def _experts_blocked(xs, blk, nused, w_gu, w_dn, layer, *, n_blocks, bm):
    rows, d = xs.shape
    f2 = w_gu.shape[3]
    f = w_dn.shape[2]
    row_blk = lambda j, bl, nu: (jnp.minimum(j, nu[0] - 1), 0)
    return pl.pallas_call(
        functools.partial(_expert_block_kernel, n_blocks=n_blocks, bm=bm),
        grid_spec=pltpu.PrefetchScalarGridSpec(
            num_scalar_prefetch=2,
            grid=(n_blocks,),
            in_specs=[pl.BlockSpec((bm, d), row_blk),
                      pl.BlockSpec((None, None, d, f2), lambda j, bl, nu: (layer, bl[j], 0, 0)),
                      pl.BlockSpec((None, None, f, d), lambda j, bl, nu: (layer, bl[j], 0, 0))],
            out_specs=pl.BlockSpec((bm, d), row_blk),
            scratch_shapes=[pltpu.VMEM((d, f2), BF16), pltpu.VMEM((f, d), BF16)],
        ),
        out_shape=jax.ShapeDtypeStruct((rows, d), BF16),
        compiler_params=_cp("arbitrary"),
    )(blk, nused, xs, w_gu, w_dn)


def _combine_tile_kernel(gstart_ref, size_ref, x_ref, lrow_ref, w_ref, sdn_ref, lw_ref, lb_ref, sgu_hbm, ys_ref,
                         o_ref, ybuf, sgu, sems, wsem, *, tm, alpha):
    i = pl.program_id(0)
    last_tile = pl.num_programs(0) - 1
    n_chunks = ybuf.shape[1] // ONEHOT_ROWS
    sure_chunks = (TOP_K * tm) // ONEHOT_ROWS

    def contract_rows(w_rows, y_rows):
        return lax.dot_general(w_rows, y_rows, (((0,), (0,)), ((), ())), preferred_element_type=F32)

    def copy_to(slot):
        return lambda g, loff, n: pltpu.make_async_copy(ys_ref.at[pl.ds(g, n), :], ybuf.at[slot, pl.ds(loff, n), :],
                                                        sems.at[slot])

    def fetch(tile, slot):
        _start_segments(gstart_ref, size_ref, tile, copy_to(slot))

    n_rows = _tile_total(size_ref, i)

    @pl.when(i == 0)
    def _():
        ybuf[...] = jnp.zeros_like(ybuf)
        weights = pltpu.make_async_copy(sgu_hbm, sgu, wsem)
        weights.start()
        weights.wait()
        fetch(0, 0)

    for slot in range(2):
        @pl.when(i % 2 == slot)
        def _():
            @pl.when(i < last_tile)
            def _():
                fetch(i + 1, 1 - slot)

            x = x_ref[...]
            f = sdn_ref.shape[0]
            gu = jnp.dot(x.astype(BF16), sgu[...], preferred_element_type=F32)
            hidden = (_silu(gu[:, :f]) * gu[:, f:]).astype(BF16)
            acc = jnp.dot(hidden, sdn_ref[...], preferred_element_type=F32)
            _wait_rows(copy_to(slot), n_rows)
            lrow, w = lrow_ref[0], w_ref[...]
            for c in range(sure_chunks):
                rows = slice(c * ONEHOT_ROWS, (c + 1) * ONEHOT_ROWS)
                acc = acc + contract_rows(_onehot_rows(c, lrow, w, tm), ybuf[slot, rows, :])
            o_ref[...] = acc
            for c in range(sure_chunks, n_chunks):
                @pl.when(c * ONEHOT_ROWS < n_rows)
                def _():
                    rows = slice(c * ONEHOT_ROWS, (c + 1) * ONEHOT_ROWS)
                    o_ref[...] += contract_rows(_onehot_rows(c, lrow, w, tm), ybuf[slot, rows, :])
            o_ref[...] = _layer_norm(alpha * x + o_ref[...], lw_ref[...], lb_ref[...])


def _combine(x, ys, lrow, wts, gstart, size, s_gu, s_dn, ln_w, ln_b, *, alpha, tm):
    t, d = x.shape
    const = lambda shape: pl.BlockSpec(shape, lambda i, *_: (0, 0))
    return pl.pallas_call(
        functools.partial(_combine_tile_kernel, tm=tm, alpha=alpha),
        grid_spec=pltpu.PrefetchScalarGridSpec(
            num_scalar_prefetch=2,
            grid=(t // tm,),
            in_specs=[pl.BlockSpec((tm, d), lambda i, *_: (i, 0)),
                      pl.BlockSpec((1, TOP_K, tm), lambda i, *_: (i, 0, 0)),
                      pl.BlockSpec((TOP_K, tm), lambda i, *_: (0, i)),
                      const(s_dn.shape), const((1, d)), const((1, d)),
                      pl.BlockSpec(memory_space=pl.ANY), pl.BlockSpec(memory_space=pl.ANY)],
            out_specs=pl.BlockSpec((tm, d), lambda i, *_: (i, 0)),
            scratch_shapes=[pltpu.VMEM((2, _tile_rows(tm), d), BF16), pltpu.VMEM(s_gu.shape, BF16),
                            pltpu.SemaphoreType.DMA((2,)), pltpu.SemaphoreType.DMA(())],
        ),
        out_shape=jax.ShapeDtypeStruct((t, d), F32),
        compiler_params=_cp("arbitrary"),
    )(gstart, size, x, lrow, wts, s_dn, ln_w, ln_b, s_gu, ys)


def _mixer_sublayer(x, w_in_all, layer, gate_b, pool_w, pool_scale, conv_w, mlstm_norm_w, ret_norm_w, w_branch,
                    w_out, ln_w, ln_b, *, batch, seq, alpha):
    t, d = x.shape
    width = d // N_BRANCH
    heads = width // HEAD_DIM
    gate_off = 8 * width
    ret_off = gate_off
    g_off = ret_off + 4 * width
    if_off = g_off + N_BRANCH * d
    w_bf16 = _realign_cast(w_in_all, layer, lo_col=gate_off, hi_col=if_off, shift=2 * heads, tr=2048, tn=512)
    gate_bias = jnp.pad(gate_b, (0, LANES - 2 * heads)).reshape(1, LANES)

    proj = _matmul(x, w_bf16, tm=1024, tn=1024, ncols=g_off)
    gates = _matmul(x, w_bf16, tm=1024, tn=LANES, ncols=LANES, col0=if_off)
    y_pool, y_conv = _pool_conv(proj, pool_w.astype(BF16), pool_scale.reshape(1, width), conv_w,
                                seq=seq, width=width, tb=512)
    y_mlstm = _mlstm(proj, gates, gate_bias, mlstm_norm_w.reshape(1, width),
                     batch=batch, seq=seq, width=width, col0=4)
    y_ret = _retention(proj, ret_norm_w.reshape(1, width), batch=batch, seq=seq, width=width,
                       col0=ret_off // width)
    merged = _merge(x, (y_pool, y_conv, y_mlstm, y_ret), w_bf16, g_off, w_branch.astype(BF16),
                    tm=512, tn=512)
    return _proj_ln(merged, w_out.astype(BF16), x, ln_w, ln_b, alpha=alpha, tm=512)


def _xattn_sublayer(x, mem2d, wq, wk, wv, wo, ln_w, ln_b, *, seq, mem_len, alpha):
    w_kv = jnp.concatenate([wk, wv], axis=1).astype(BF16)
    kv = _matmul(mem2d, w_kv, tm=min(mem2d.shape[0], 1024), tn=512)
    return _xattn(x, kv, wq.astype(BF16), wo.astype(BF16), ln_w, ln_b,
                  alpha=alpha, seq=seq, mem_len=mem_len, tm=1024)


def _moe_sublayer(x, router_w, router_b, w_gu, w_dn, layer, s_gu, s_dn, ln_w, ln_b, *, alpha):
    t, d = x.shape
    e_n, bm = N_EXPERTS, MOE_BM
    tm = ROUTE_TM
    n_blocks = -(-(t * TOP_K + (t // tm) * e_n * (SEG_ALIGN - 1)) // bm) + e_n
    wts, lrow, segtab, blk, seg = _route(x, router_w.T, router_b.reshape(e_n, 1), tm=tm, bm=bm, n_blocks=n_blocks)
    gstart, size = segtab[:, 0, :, 0].reshape(-1), segtab[:, 1, :, 0].reshape(-1)
    nused = seg[0, e_n - 1, 0] // bm

    xs = _dispatch(x, lrow, gstart, size, rows=n_blocks * bm, tm=tm)
    ys = _experts_blocked(xs, blk[:, 0, :n_blocks].reshape(-1), nused.reshape(1), w_gu, w_dn, layer,
                          n_blocks=n_blocks, bm=bm)
    return _combine(x, ys, lrow, wts, gstart, size, s_gu.astype(BF16), s_dn.astype(BF16), ln_w, ln_b,
                    alpha=alpha, tm=tm)


def kernel(x, mem, w_in, mlstm_gate_b, pool_w, pool_scale, conv_w, mlstm_norm_w, ret_norm_w, w_branch,
           w_mix_out, xa_wq, xa_wk, xa_wv, xa_wo, router_w, router_b, moe_w_gu, moe_w_dn, shared_w_gu,
           shared_w_dn, ln_w, ln_b):
    batch, seq, d = x.shape
    depth = w_in.shape[0]
    mem_len = mem.shape[1]
    alpha = (2 * depth) ** 0.25
    h = x.reshape(batch * seq, d)
    mem2d = mem.reshape(batch * mem_len, d)
    for l in range(depth):
        lw = ln_w[l].reshape(3, 1, d)
        lb = ln_b[l].reshape(3, 1, d)
        h = _mixer_sublayer(h, w_in, l, mlstm_gate_b[l], pool_w[l], pool_scale[l], conv_w[l], mlstm_norm_w[l],
                            ret_norm_w[l], w_branch[l], w_mix_out[l], lw[0], lb[0],
                            batch=batch, seq=seq, alpha=alpha)
        h = _xattn_sublayer(h, mem2d, xa_wq[l], xa_wk[l], xa_wv[l], xa_wo[l], lw[1], lb[1],
                            seq=seq, mem_len=mem_len, alpha=alpha)
        h = _moe_sublayer(h, router_w[l], router_b[l], moe_w_gu, moe_w_dn, l, shared_w_gu[l], shared_w_dn[l],
                          lw[2], lb[2], alpha=alpha)
    return h.reshape(batch, seq, d)
```
